```python
import math
import jax, jax.numpy as jnp
from jax import lax
import numpy as np

D_MODEL = 1024
BATCH = 8
SEQ = 4096
DEPTH = 2

CTX_LEN = 256
GRID_W = 64

SSD_D_INNER = 2 * D_MODEL
SSD_HEADDIM = 64
SSD_HEADS = SSD_D_INNER // SSD_HEADDIM
SSD_GROUPS = 8
SSD_HPG = SSD_HEADS // SSD_GROUPS
SSD_STATE = 128
SSD_CONV = 5
SSD_CHUNK = 128
SSD_GN = SSD_GROUPS * SSD_STATE
XBC_WIDTH = SSD_D_INNER + 2 * SSD_GN

POOL_WIDTH = D_MODEL
POOL_WINDOWS = (2, 4, 8, 16)
POOL_GROUP = POOL_WIDTH // len(POOL_WINDOWS)

FFN_HIDDEN = -(-(8 * D_MODEL) // (3 * 256)) * 256

IN_COLS = SSD_D_INNER + XBC_WIDTH + 2 * SSD_HEADS + POOL_WIDTH + 2 * D_MODEL
IN_SPLITS = (SSD_D_INNER,
             SSD_D_INNER + XBC_WIDTH,
             SSD_D_INNER + XBC_WIDTH + 2 * SSD_HEADS,
             SSD_D_INNER + XBC_WIDTH + 2 * SSD_HEADS + POOL_WIDTH)
EPS = 1e-6

kernel_name = "hybrid_ssd_pool_dit_block"


def rmsnorm(x, g):
    xf = x.astype(jnp.float32)
    y = xf * lax.rsqrt(jnp.mean(xf * xf, axis=-1, keepdims=True) + EPS)
    return (y * g).astype(x.dtype)


def modulate(h, shift, scale):
    return h * (1.0 + scale) + shift


def adaln(cond, w, b):
    m = jax.nn.silu(cond) @ w + b
    return jnp.split(m[:, None, :], 6, axis=-1)


def dwconv_centred(u, w, b):
    k = w.shape[0]
    pad = k // 2
    L = u.shape[1]
    up = jnp.pad(u, ((0, 0), (pad, pad), (0, 0)))
    out = b
    for i in range(k):
        out = out + w[i] * up[:, i:i + L]
    return out


def pool_minus_self(u):
    L = u.shape[-2]
    uf = u.astype(jnp.float32)
    cs = jnp.cumsum(uf, axis=-2)
    cs = jnp.concatenate([jnp.zeros_like(cs[..., :1, :]), cs], axis=-2)
    t = jnp.arange(L)
    outs = []
    for gi, k in enumerate(POOL_WINDOWS):
        lo = jnp.clip(t - k // 2, 0, L)
        hi = jnp.clip(t + k // 2, 0, L)
        seg = cs[..., gi * POOL_GROUP:(gi + 1) * POOL_GROUP]
        s = jnp.take(seg, hi, axis=-2) - jnp.take(seg, lo, axis=-2)
        outs.append(s / (hi - lo).astype(jnp.float32)[:, None])
    mean = jnp.concatenate(outs, axis=-1)
    return (mean - uf).astype(u.dtype)


def ssd_scan(xh, dt, a_log, bm, cm, init_state, return_y):
    Bsz, L, H, P = xh.shape
    G, N = bm.shape[2], bm.shape[3]
    R = H // G
    Q = SSD_CHUNK
    nc = L // Q
    A = -jnp.exp(a_log.astype(jnp.float32))
    acum = jnp.cumsum((dt * A).reshape(Bsz, nc, Q, H), axis=2)
    xc = (xh.astype(jnp.float32) * dt[..., None]).reshape(Bsz, nc, Q, G, R, P)
    bc = bm.astype(jnp.float32).reshape(Bsz, nc, Q, G, N)
    cc = cm.astype(jnp.float32).reshape(Bsz, nc, Q, G, N)
    a_last = acum[:, :, -1]
    decay_to_end = jnp.exp(a_last[:, :, None, :] - acum).reshape(Bsz, nc, Q, G, R)
    chunk_states = jnp.einsum('bcjgn,bcjgr,bcjgrp->bcgrpn', bc, decay_to_end, xc)

    def step(state, inp):
        cs, al = inp
        return state * jnp.exp(al)[..., None, None] + cs, state

    final, entering = lax.scan(
        step, init_state.reshape(Bsz, G, R, P, N),
        (jnp.moveaxis(chunk_states, 1, 0), jnp.moveaxis(a_last.reshape(Bsz, nc, G, R), 1, 0)))
    final = final.reshape(Bsz, H, P, N)
    if not return_y:
        return None, final
    entering = jnp.moveaxis(entering, 0, 1)
    mask = jnp.tril(jnp.ones((Q, Q), dtype=bool))[None, None, :, :, None]
    diff = acum[:, :, :, None, :] - acum[:, :, None, :, :]
    decay = jnp.exp(jnp.where(mask, diff, -jnp.inf)).reshape(Bsz, nc, Q, Q, G, R)
    cb = jnp.einsum('bcign,bcjgn->bcijg', cc, bc)
    y_intra = jnp.einsum('bcijgr,bcjgrp->bcigrp', cb[..., None] * decay, xc)
    y_inter = (jnp.einsum('bcign,bcgrpn->bcigrp', cc, entering)
               * jnp.exp(acum).reshape(Bsz, nc, Q, G, R)[..., None])
    y = (y_intra + y_inter).reshape(Bsz, L, H, P)
    return y.astype(xh.dtype), final


def ssd_bidir(xs, dt, bm, cm, a_log, d_skip, init_states, return_y):
    ys = []
    finals = []
    for d in range(2):
        flip = (lambda a: jnp.flip(a, axis=1)) if d == 1 else (lambda a: a)
        y, s = ssd_scan(flip(xs), flip(dt[:, :, d]), a_log[d], flip(bm), flip(cm), init_states[d], return_y)
        finals.append(s)
        if return_y:
            ys.append(flip(y) + d_skip[d][:, None] * xs)
    y = (ys[0] + ys[1]) if return_y else None
    return y, finals


def prepare(h, w_in, conv_w, conv_b, dt_bias):
    Bsz, L = h.shape[0], h.shape[1]
    proj = h @ w_in
    z, xbc, dt_raw, u_pool, gate_logits = jnp.split(proj, IN_SPLITS, axis=-1)
    xbc = jax.nn.silu(dwconv_centred(xbc, conv_w, conv_b))
    xs, bm, cm = jnp.split(xbc, (SSD_D_INNER, SSD_D_INNER + SSD_GN), axis=-1)
    xs = xs.reshape(Bsz, L, SSD_HEADS, SSD_HEADDIM)
    bm = bm.reshape(Bsz, L, SSD_GROUPS, SSD_STATE)
    cm = cm.reshape(Bsz, L, SSD_GROUPS, SSD_STATE)
    dt = jax.nn.softplus((dt_raw.reshape(Bsz, L, 2, SSD_HEADS) + dt_bias).astype(jnp.float32))
    return z, xs, bm, cm, dt, u_pool, gate_logits


def mixer_out(z, y, u_pool, gate_logits, ssd_norm_w, w_ssd_out, pool_w, pool_scale, w_pool_out, w_out, on_grid):
    Bsz, L = z.shape[0], z.shape[1]
    yz = y.reshape(Bsz, L, SSD_D_INNER) * jax.nn.silu(z)
    yn = rmsnorm(yz.reshape(Bsz, L, SSD_GROUPS, SSD_D_INNER // SSD_GROUPS), 1.0).reshape(Bsz, L, SSD_D_INNER)
    o_ssd = (yn * ssd_norm_w) @ w_ssd_out
    if on_grid:
        rows = L // GRID_W
        pm = pool_minus_self(u_pool.reshape(Bsz, rows, GRID_W, POOL_WIDTH)).reshape(Bsz, L, POOL_WIDTH)
    else:
        pm = pool_minus_self(u_pool)
    pm = jnp.einsum('blgi,gio->blgo', pm.reshape(Bsz, L, len(POOL_WINDOWS), POOL_GROUP), pool_w)
    o_pool = (pm.reshape(Bsz, L, POOL_WIDTH) * pool_scale) @ w_pool_out
    g_ssd, g_pool = jnp.split(jax.nn.sigmoid(gate_logits), 2, axis=-1)
    return (g_ssd * o_ssd + g_pool * o_pool) @ w_out


def swiglu(h, w_gate_up, w_down):
    a, b = jnp.split(h @ w_gate_up, 2, axis=-1)
    return (jax.nn.silu(a) * b) @ w_down


def _fwd_setup_inputs(seed: int = 0) -> dict:
    key = jax.random.key(seed)
    ks = jax.random.split(key, 24)
    f32 = jnp.float32

    def nrm(k, shape, scale):
        return jax.random.normal(k, shape, f32) * scale

    H = SSD_HEADS
    dt0 = jnp.exp(jax.random.uniform(ks[10], (DEPTH, 2, H), f32, minval=math.log(1e-3), maxval=math.log(1e-1)))
    return {
        "x": nrm(ks[0], (BATCH, SEQ, D_MODEL), 1.0),
        "c": nrm(ks[1], (BATCH, D_MODEL), 1.0),
        "ctx": nrm(ks[2], (BATCH, CTX_LEN, D_MODEL), 1.0),
        "c_ctx": nrm(ks[3], (D_MODEL,), 1.0),
        "w_ada": nrm(ks[4], (DEPTH, D_MODEL, 6 * D_MODEL), 0.5 * D_MODEL ** -0.5),
        "b_ada": nrm(ks[5], (DEPTH, 6 * D_MODEL), 0.02),
        "g_mix": 1.0 + nrm(ks[6], (DEPTH, D_MODEL), 0.05),
        "w_in": nrm(ks[7], (DEPTH, D_MODEL, IN_COLS), D_MODEL ** -0.5),
        "conv_w": nrm(ks[8], (DEPTH, SSD_CONV, XBC_WIDTH), SSD_CONV ** -0.5),
        "conv_b": nrm(ks[9], (DEPTH, XBC_WIDTH), 0.02),
        "dt_bias": dt0 + jnp.log(-jnp.expm1(-dt0)),
        "a_log": jnp.log(jax.random.uniform(ks[11], (DEPTH, 2, H), f32, minval=1.0, maxval=16.0)),
        "d_skip": 1.0 + nrm(ks[12], (DEPTH, 2, H), 0.05),
        "ssd_norm_w": 1.0 + nrm(ks[13], (DEPTH, SSD_D_INNER), 0.05),
        "w_ssd_out": nrm(ks[14], (DEPTH, SSD_D_INNER, D_MODEL), SSD_D_INNER ** -0.5),
        "pool_w": nrm(ks[15], (DEPTH, len(POOL_WINDOWS), POOL_GROUP, POOL_GROUP), POOL_GROUP ** -0.5),
        "pool_scale": 1.0 + nrm(ks[16], (DEPTH, POOL_WIDTH), 0.1),
        "w_pool_out": nrm(ks[17], (DEPTH, POOL_WIDTH, D_MODEL), POOL_WIDTH ** -0.5),
        "w_out": nrm(ks[18], (DEPTH, D_MODEL, D_MODEL), D_MODEL ** -0.5),
        "g_ffn": 1.0 + nrm(ks[19], (DEPTH, D_MODEL), 0.05),
        "w_gate_up": nrm(ks[20], (DEPTH, D_MODEL, 2 * FFN_HIDDEN), D_MODEL ** -0.5),
        "w_down": nrm(ks[21], (DEPTH, FFN_HIDDEN, D_MODEL), FFN_HIDDEN ** -0.5),
        "g_final": 1.0 + nrm(ks[22], (D_MODEL,), 0.05),
    }


def _fwd_reference(x, c, ctx, c_ctx, w_ada, b_ada, g_mix, w_in, conv_w, conv_b, dt_bias, a_log, d_skip,
              ssd_norm_w, w_ssd_out, pool_w, pool_scale, w_pool_out, w_out, g_ffn, w_gate_up, w_down,
              g_final):
    cx = ctx
    for l in range(DEPTH):
        last = l == DEPTH - 1
        sh1, sc1, ga1, sh2, sc2, ga2 = adaln(c, w_ada[l], b_ada[l])
        csh1, csc1, cga1, csh2, csc2, cga2 = adaln(c_ctx[None, :], w_ada[l], b_ada[l])

        h_lat = modulate(rmsnorm(x, g_mix[l]), sh1, sc1)
        h_ctx = modulate(rmsnorm(cx, g_mix[l]), csh1, csc1)
        zc, xsc, bmc, cmc, dtc, upc, glc = prepare(h_ctx, w_in[l], conv_w[l], conv_b[l], dt_bias[l])
        zl, xsl, bml, cml, dtl, upl, gll = prepare(h_lat, w_in[l], conv_w[l], conv_b[l], dt_bias[l])
        zero = jnp.zeros((xsc.shape[0], SSD_HEADS, SSD_HEADDIM, SSD_STATE), jnp.float32)
        y_ctx, ctx_states = ssd_bidir(xsc, dtc, bmc, cmc, a_log[l], d_skip[l], (zero, zero), not last)
        y_lat, _ = ssd_bidir(xsl, dtl, bml, cml, a_log[l], d_skip[l], ctx_states, True)
        x = x + ga1 * mixer_out(zl, y_lat, upl, gll, ssd_norm_w[l], w_ssd_out[l], pool_w[l], pool_scale[l],
                                w_pool_out[l], w_out[l], True)
        x = x + ga2 * swiglu(modulate(rmsnorm(x, g_ffn[l]), sh2, sc2), w_gate_up[l], w_down[l])

        if not last:
            cx = cx + cga1 * mixer_out(zc, y_ctx, upc, glc, ssd_norm_w[l], w_ssd_out[l], pool_w[l],
                                       pool_scale[l], w_pool_out[l], w_out[l], False)
            cx = cx + cga2 * swiglu(modulate(rmsnorm(cx, g_ffn[l]), csh2, csc2), w_gate_up[l], w_down[l])
    return rmsnorm(x, g_final)


import jax as _jax
import jax.numpy as _jnp

TWIN_FORMAT = 'train_step'
FWD_PARAMS = ['x', 'c', 'ctx', 'c_ctx', 'w_ada', 'b_ada', 'g_mix', 'w_in', 'conv_w', 'conv_b', 'dt_bias', 'a_log', 'd_skip', 'ssd_norm_w', 'w_ssd_out', 'pool_w', 'pool_scale', 'w_pool_out', 'w_out', 'g_ffn', 'w_gate_up', 'w_down', 'g_final']
TWIN_WEIGHTS = ['c_ctx', 'w_ada', 'b_ada', 'g_mix', 'w_in', 'conv_w', 'conv_b', 'dt_bias', 'a_log', 'd_skip', 'ssd_norm_w', 'w_ssd_out', 'pool_w', 'pool_scale', 'w_pool_out', 'w_out', 'g_ffn', 'w_gate_up', 'w_down', 'g_final']
TWIN_DIFF_INPUT = 'x'
TWIN_INPUTS = ['x', 'c', 'ctx', 'c_ctx', 'w_ada', 'b_ada', 'g_mix', 'w_in', 'conv_w', 'conv_b', 'dt_bias', 'a_log', 'd_skip', 'ssd_norm_w', 'w_ssd_out', 'pool_w', 'pool_scale', 'w_pool_out', 'w_out', 'g_ffn', 'w_gate_up', 'w_down', 'g_final', 'loss_target', 'm_c_ctx', 'm_w_ada', 'm_b_ada', 'm_g_mix', 'm_w_in', 'm_conv_w', 'm_conv_b', 'm_dt_bias', 'm_a_log', 'm_d_skip', 'm_ssd_norm_w', 'm_w_ssd_out', 'm_pool_w', 'm_pool_scale', 'm_w_pool_out', 'm_w_out', 'm_g_ffn', 'm_w_gate_up', 'm_w_down', 'm_g_final', 'v_c_ctx', 'v_w_ada', 'v_b_ada', 'v_g_mix', 'v_w_in', 'v_conv_w', 'v_conv_b', 'v_dt_bias', 'v_a_log', 'v_d_skip', 'v_ssd_norm_w', 'v_w_ssd_out', 'v_pool_w', 'v_pool_scale', 'v_w_pool_out', 'v_w_out', 'v_g_ffn', 'v_w_gate_up', 'v_w_down', 'v_g_final']
TWIN_OUTPUTS = ['loss', 'grad_x', 'grad_c_ctx', 'grad_w_ada', 'grad_b_ada', 'grad_g_mix', 'grad_w_in', 'grad_conv_w', 'grad_conv_b', 'grad_dt_bias', 'grad_a_log', 'grad_d_skip', 'grad_ssd_norm_w', 'grad_w_ssd_out', 'grad_pool_w', 'grad_pool_scale', 'grad_w_pool_out', 'grad_w_out', 'grad_g_ffn', 'grad_w_gate_up', 'grad_w_down', 'grad_g_final', 'delta_c_ctx', 'delta_w_ada', 'delta_b_ada', 'delta_g_mix', 'delta_w_in', 'delta_conv_w', 'delta_conv_b', 'delta_dt_bias', 'delta_a_log', 'delta_d_skip', 'delta_ssd_norm_w', 'delta_w_ssd_out', 'delta_pool_w', 'delta_pool_scale', 'delta_w_pool_out', 'delta_w_out', 'delta_g_ffn', 'delta_w_gate_up', 'delta_w_down', 'delta_g_final', 'new_m_c_ctx', 'new_m_w_ada', 'new_m_b_ada', 'new_m_g_mix', 'new_m_w_in', 'new_m_conv_w', 'new_m_conv_b', 'new_m_dt_bias', 'new_m_a_log', 'new_m_d_skip', 'new_m_ssd_norm_w', 'new_m_w_ssd_out', 'new_m_pool_w', 'new_m_pool_scale', 'new_m_w_pool_out', 'new_m_w_out', 'new_m_g_ffn', 'new_m_w_gate_up', 'new_m_w_down', 'new_m_g_final', 'new_v_c_ctx', 'new_v_w_ada', 'new_v_b_ada', 'new_v_g_mix', 'new_v_w_in', 'new_v_conv_w', 'new_v_conv_b', 'new_v_dt_bias', 'new_v_a_log', 'new_v_d_skip', 'new_v_ssd_norm_w', 'new_v_w_ssd_out', 'new_v_pool_w', 'new_v_pool_scale', 'new_v_w_pool_out', 'new_v_w_out', 'new_v_g_ffn', 'new_v_w_gate_up', 'new_v_w_down', 'new_v_g_final']
TWIN_LEAF_KINDS = {'loss': 'loss', 'grad_x': 'grad_x', 'grad_c_ctx': 'grad_w', 'grad_w_ada': 'grad_w', 'grad_b_ada': 'grad_w', 'grad_g_mix': 'grad_w', 'grad_w_in': 'grad_w', 'grad_conv_w': 'grad_w', 'grad_conv_b': 'grad_w', 'grad_dt_bias': 'grad_w', 'grad_a_log': 'grad_w', 'grad_d_skip': 'grad_w', 'grad_ssd_norm_w': 'grad_w', 'grad_w_ssd_out': 'grad_w', 'grad_pool_w': 'grad_w', 'grad_pool_scale': 'grad_w', 'grad_w_pool_out': 'grad_w', 'grad_w_out': 'grad_w', 'grad_g_ffn': 'grad_w', 'grad_w_gate_up': 'grad_w', 'grad_w_down': 'grad_w', 'grad_g_final': 'grad_w', 'delta_c_ctx': 'delta_w', 'delta_w_ada': 'delta_w', 'delta_b_ada': 'delta_w', 'delta_g_mix': 'delta_w', 'delta_w_in': 'delta_w', 'delta_conv_w': 'delta_w', 'delta_conv_b': 'delta_w', 'delta_dt_bias': 'delta_w', 'delta_a_log': 'delta_w', 'delta_d_skip': 'delta_w', 'delta_ssd_norm_w': 'delta_w', 'delta_w_ssd_out': 'delta_w', 'delta_pool_w': 'delta_w', 'delta_pool_scale': 'delta_w', 'delta_w_pool_out': 'delta_w', 'delta_w_out': 'delta_w', 'delta_g_ffn': 'delta_w', 'delta_w_gate_up': 'delta_w', 'delta_w_down': 'delta_w', 'delta_g_final': 'delta_w', 'new_m_c_ctx': 'new_m', 'new_m_w_ada': 'new_m', 'new_m_b_ada': 'new_m', 'new_m_g_mix': 'new_m', 'new_m_w_in': 'new_m', 'new_m_conv_w': 'new_m', 'new_m_conv_b': 'new_m', 'new_m_dt_bias': 'new_m', 'new_m_a_log': 'new_m', 'new_m_d_skip': 'new_m', 'new_m_ssd_norm_w': 'new_m', 'new_m_w_ssd_out': 'new_m', 'new_m_pool_w': 'new_m', 'new_m_pool_scale': 'new_m', 'new_m_w_pool_out': 'new_m', 'new_m_w_out': 'new_m', 'new_m_g_ffn': 'new_m', 'new_m_w_gate_up': 'new_m', 'new_m_w_down': 'new_m', 'new_m_g_final': 'new_m', 'new_v_c_ctx': 'new_v', 'new_v_w_ada': 'new_v', 'new_v_b_ada': 'new_v', 'new_v_g_mix': 'new_v', 'new_v_w_in': 'new_v', 'new_v_conv_w': 'new_v', 'new_v_conv_b': 'new_v', 'new_v_dt_bias': 'new_v', 'new_v_a_log': 'new_v', 'new_v_d_skip': 'new_v', 'new_v_ssd_norm_w': 'new_v', 'new_v_w_ssd_out': 'new_v', 'new_v_pool_w': 'new_v', 'new_v_pool_scale': 'new_v', 'new_v_w_pool_out': 'new_v', 'new_v_w_out': 'new_v', 'new_v_g_ffn': 'new_v', 'new_v_w_gate_up': 'new_v', 'new_v_w_down': 'new_v', 'new_v_g_final': 'new_v'}


def _forward(args):
    return _fwd_reference(*[args[k] for k in FWD_PARAMS])


def _output_shape():
    def fwd():
        inp = _fwd_setup_inputs(0)
        return _fwd_reference(*[inp[k] for k in FWD_PARAMS])
    out = _jax.eval_shape(fwd)
    return out.shape, out.dtype

N_MICROBATCH = 1
ADAM_LR = 0.001
ADAM_B1 = 0.9
ADAM_B2 = 0.999
ADAM_EPS = 1e-08
ADAM_WD = 0.01
ADAM_STEP = 10
PER_EXAMPLE_BATCH_AXIS = {'x': 0, 'c': 0, 'ctx': 0, 'loss_target': 0}
SHARED_INPUTS = []
_WEIGHT_DTYPES = {'c_ctx': _jnp.float32, 'w_ada': _jnp.float32, 'b_ada': _jnp.float32, 'g_mix': _jnp.float32, 'w_in': _jnp.float32, 'conv_w': _jnp.float32, 'conv_b': _jnp.float32, 'dt_bias': _jnp.float32, 'a_log': _jnp.float32, 'd_skip': _jnp.float32, 'ssd_norm_w': _jnp.float32, 'w_ssd_out': _jnp.float32, 'pool_w': _jnp.float32, 'pool_scale': _jnp.float32, 'w_pool_out': _jnp.float32, 'w_out': _jnp.float32, 'g_ffn': _jnp.float32, 'w_gate_up': _jnp.float32, 'w_down': _jnp.float32, 'g_final': _jnp.float32}
MOMENT_SCALE = {'c_ctx': 2.345721e-03, 'w_ada': 6.468017e-02, 'b_ada': 1.242078e-01, 'g_mix': 5.387687e-02, 'w_in': 1.763887e-02, 'conv_w': 1.540575e-02, 'conv_b': 2.072593e-02, 'dt_bias': 2.771929e-02, 'a_log': 5.129993e-02, 'd_skip': 4.444450e-02, 'ssd_norm_w': 2.230890e-02, 'w_ssd_out': 2.936299e-02, 'pool_w': 2.695959e-02, 'pool_scale': 2.665600e-02, 'w_pool_out': 2.714249e-02, 'w_out': 4.009206e-02, 'g_ffn': 5.219234e-02, 'w_gate_up': 2.277357e-02, 'w_down': 3.709451e-02, 'g_final': 3.207735e+01}


def _to_microbatches(a, axis):
    t = _jnp.moveaxis(a, axis, 0)
    t = t.reshape((N_MICROBATCH, t.shape[0] // N_MICROBATCH) + t.shape[1:])
    return _jnp.moveaxis(t, 1, axis + 1)


def setup_inputs(seed: int = 0) -> dict:
    inp = _fwd_setup_inputs(seed)
    key = _jax.random.fold_in(_jax.random.key(seed), 7919)
    shape, _ = _output_shape()
    out = dict(inp)
    out["loss_target"] = _jax.random.normal(_jax.random.fold_in(key, 0), shape, _jnp.float32)
    for i, name in enumerate(TWIN_WEIGHTS):
        w = inp[name].astype(_jnp.float32)
        if MOMENT_SCALE is None:
            s = _jnp.sqrt(_jnp.mean(_jnp.square(w)) + 1e-30)
        else:
            s = MOMENT_SCALE[name]
        km, kv = _jax.random.split(_jax.random.fold_in(key, i + 1))
        out[name] = w
        out["m_" + name] = s * _jax.random.normal(km, w.shape, _jnp.float32)
        out["v_" + name] = (s * s) * _jax.random.uniform(kv, w.shape, _jnp.float32, 0.5, 1.5)
    if N_MICROBATCH > 1:
        for name, axis in PER_EXAMPLE_BATCH_AXIS.items():
            out[name] = _to_microbatches(out[name], axis)
    return {'x': out['x'], 'c': out['c'], 'ctx': out['ctx'], 'c_ctx': out['c_ctx'], 'w_ada': out['w_ada'], 'b_ada': out['b_ada'], 'g_mix': out['g_mix'], 'w_in': out['w_in'], 'conv_w': out['conv_w'], 'conv_b': out['conv_b'], 'dt_bias': out['dt_bias'], 'a_log': out['a_log'], 'd_skip': out['d_skip'], 'ssd_norm_w': out['ssd_norm_w'], 'w_ssd_out': out['w_ssd_out'], 'pool_w': out['pool_w'], 'pool_scale': out['pool_scale'], 'w_pool_out': out['w_pool_out'], 'w_out': out['w_out'], 'g_ffn': out['g_ffn'], 'w_gate_up': out['w_gate_up'], 'w_down': out['w_down'], 'g_final': out['g_final'], 'loss_target': out['loss_target'], 'm_c_ctx': out['m_c_ctx'], 'm_w_ada': out['m_w_ada'], 'm_b_ada': out['m_b_ada'], 'm_g_mix': out['m_g_mix'], 'm_w_in': out['m_w_in'], 'm_conv_w': out['m_conv_w'], 'm_conv_b': out['m_conv_b'], 'm_dt_bias': out['m_dt_bias'], 'm_a_log': out['m_a_log'], 'm_d_skip': out['m_d_skip'], 'm_ssd_norm_w': out['m_ssd_norm_w'], 'm_w_ssd_out': out['m_w_ssd_out'], 'm_pool_w': out['m_pool_w'], 'm_pool_scale': out['m_pool_scale'], 'm_w_pool_out': out['m_w_pool_out'], 'm_w_out': out['m_w_out'], 'm_g_ffn': out['m_g_ffn'], 'm_w_gate_up': out['m_w_gate_up'], 'm_w_down': out['m_w_down'], 'm_g_final': out['m_g_final'], 'v_c_ctx': out['v_c_ctx'], 'v_w_ada': out['v_w_ada'], 'v_b_ada': out['v_b_ada'], 'v_g_mix': out['v_g_mix'], 'v_w_in': out['v_w_in'], 'v_conv_w': out['v_conv_w'], 'v_conv_b': out['v_conv_b'], 'v_dt_bias': out['v_dt_bias'], 'v_a_log': out['v_a_log'], 'v_d_skip': out['v_d_skip'], 'v_ssd_norm_w': out['v_ssd_norm_w'], 'v_w_ssd_out': out['v_w_ssd_out'], 'v_pool_w': out['v_pool_w'], 'v_pool_scale': out['v_pool_scale'], 'v_w_pool_out': out['v_w_pool_out'], 'v_w_out': out['v_w_out'], 'v_g_ffn': out['v_g_ffn'], 'v_w_gate_up': out['v_w_gate_up'], 'v_w_down': out['v_w_down'], 'v_g_final': out['v_g_final']}


def _loss(weights, diff, rest, loss_target):
    with _jax.named_scope("forward"):
        args = {**rest, TWIN_DIFF_INPUT: diff, **{k: w.astype(_WEIGHT_DTYPES[k]) for k, w in weights.items()}}
        y = _forward(args)
    with _jax.named_scope("loss_head"):
        err = _jnp.square(y.astype(_jnp.float32) - loss_target)
        return 0.5 * _jnp.sum(_jnp.mean(err, axis=-1)) if err.ndim else 0.5 * err


def _adamw(w, g, m, v):
    m = ADAM_B1 * m + (1.0 - ADAM_B1) * g
    v = ADAM_B2 * v + (1.0 - ADAM_B2) * _jnp.square(g)
    m_hat = m / (1.0 - ADAM_B1 ** ADAM_STEP)
    v_hat = v / (1.0 - ADAM_B2 ** ADAM_STEP)
    delta = -ADAM_LR * (m_hat / (_jnp.sqrt(v_hat) + ADAM_EPS) + ADAM_WD * w)
    return delta, m, v


def reference(x, c, ctx, c_ctx, w_ada, b_ada, g_mix, w_in, conv_w, conv_b, dt_bias, a_log, d_skip, ssd_norm_w, w_ssd_out, pool_w, pool_scale, w_pool_out, w_out, g_ffn, w_gate_up, w_down, g_final, loss_target, m_c_ctx, m_w_ada, m_b_ada, m_g_mix, m_w_in, m_conv_w, m_conv_b, m_dt_bias, m_a_log, m_d_skip, m_ssd_norm_w, m_w_ssd_out, m_pool_w, m_pool_scale, m_w_pool_out, m_w_out, m_g_ffn, m_w_gate_up, m_w_down, m_g_final, v_c_ctx, v_w_ada, v_b_ada, v_g_mix, v_w_in, v_conv_w, v_conv_b, v_dt_bias, v_a_log, v_d_skip, v_ssd_norm_w, v_w_ssd_out, v_pool_w, v_pool_scale, v_w_pool_out, v_w_out, v_g_ffn, v_w_gate_up, v_w_down, v_g_final):
    given = dict(x=x, c=c, ctx=ctx, c_ctx=c_ctx, w_ada=w_ada, b_ada=b_ada, g_mix=g_mix, w_in=w_in, conv_w=conv_w, conv_b=conv_b, dt_bias=dt_bias, a_log=a_log, d_skip=d_skip, ssd_norm_w=ssd_norm_w, w_ssd_out=w_ssd_out, pool_w=pool_w, pool_scale=pool_scale, w_pool_out=w_pool_out, w_out=w_out, g_ffn=g_ffn, w_gate_up=w_gate_up, w_down=w_down, g_final=g_final, loss_target=loss_target, m_c_ctx=m_c_ctx, m_w_ada=m_w_ada, m_b_ada=m_b_ada, m_g_mix=m_g_mix, m_w_in=m_w_in, m_conv_w=m_conv_w, m_conv_b=m_conv_b, m_dt_bias=m_dt_bias, m_a_log=m_a_log, m_d_skip=m_d_skip, m_ssd_norm_w=m_ssd_norm_w, m_w_ssd_out=m_w_ssd_out, m_pool_w=m_pool_w, m_pool_scale=m_pool_scale, m_w_pool_out=m_w_pool_out, m_w_out=m_w_out, m_g_ffn=m_g_ffn, m_w_gate_up=m_w_gate_up, m_w_down=m_w_down, m_g_final=m_g_final, v_c_ctx=v_c_ctx, v_w_ada=v_w_ada, v_b_ada=v_b_ada, v_g_mix=v_g_mix, v_w_in=v_w_in, v_conv_w=v_conv_w, v_conv_b=v_conv_b, v_dt_bias=v_dt_bias, v_a_log=v_a_log, v_d_skip=v_d_skip, v_ssd_norm_w=v_ssd_norm_w, v_w_ssd_out=v_w_ssd_out, v_pool_w=v_pool_w, v_pool_scale=v_pool_scale, v_w_pool_out=v_w_pool_out, v_w_out=v_w_out, v_g_ffn=v_g_ffn, v_w_gate_up=v_w_gate_up, v_w_down=v_w_down, v_g_final=v_g_final)
    weights = {n: given[n] for n in TWIN_WEIGHTS}
    shared = {n: given[n] for n in SHARED_INPUTS}
    per_example = {n: given[n] for n in ['x', 'c', 'ctx']}
    grad_fn = _jax.value_and_grad(_loss, argnums=(0, 1))

    def one_microbatch(ex, loss_target):
        ex = dict(ex)
        diff = ex.pop(TWIN_DIFF_INPUT)
        return grad_fn(weights, diff, {**shared, **ex}, loss_target)

    if N_MICROBATCH == 1:
        loss, (grad_w, grad_x) = one_microbatch(per_example, given["loss_target"])
    else:
        def body(carry, xs):
            loss_sum, grad_sum = carry
            l_k, (gw_k, gx_k) = one_microbatch(xs[0], xs[1])
            with _jax.named_scope("update"):
                return (loss_sum + l_k, _jax.tree.map(_jnp.add, grad_sum, gw_k)), gx_k

        init = (_jnp.zeros((), _jnp.float32), _jax.tree.map(_jnp.zeros_like, weights))
        (loss, grad_w), grad_x = _jax.lax.scan(body, init, (per_example, given["loss_target"]))
    with _jax.named_scope("update"):
        delta_w, new_m, new_v = {}, {}, {}
        for n in TWIN_WEIGHTS:
            delta_w[n], new_m[n], new_v[n] = _adamw(weights[n], grad_w[n], given["m_" + n], given["v_" + n])
    return (loss, grad_x, *[grad_w[n] for n in TWIN_WEIGHTS], *[delta_w[n] for n in TWIN_WEIGHTS],
            *[new_m[n] for n in TWIN_WEIGHTS], *[new_v[n] for n in TWIN_WEIGHTS])
```

```python
import functools

import jax
import jax.numpy as jnp
import numpy as np
from jax import lax
from jax.experimental import pallas as pl
from jax.experimental.pallas import tpu as pltpu

F32 = jnp.float32
MXU_DTYPE = jnp.bfloat16
ACT_DTYPE = jnp.bfloat16
VMEM_LIMIT_BYTES = 48 * 1024 * 1024
EPS = 1e-6
NEG = -1e30

SSD_HEADDIM = 64
SSD_GROUPS = 8
SSD_STATE = 128
SSD_CHUNK = 128
SSD_CONV = 5
GRID_W = 64
POOL_WINDOWS = (2, 4, 8, 16)
ROW_TILE = 256
DT_PAD = 512

ADAM_LR = 0.001
ADAM_B1 = 0.9
ADAM_B2 = 0.999
ADAM_EPS = 1e-08
ADAM_WD = 0.01
ADAM_STEP = 10

MESH = pl.DeviceIdType.MESH


def _pcall(body, **kw):
    return pl.pallas_call(body, **kw)


def _params(*sem):
    return pltpu.CompilerParams(dimension_semantics=tuple(sem), vmem_limit_bytes=VMEM_LIMIT_BYTES)


def _pick_tile(n, cands):
    for t in cands:
        if n % t == 0:
            return t
    return n


def _dot(a, b, dims):
    return lax.dot_general(a.astype(MXU_DTYPE), b.astype(MXU_DTYPE), (dims, ((), ())), preferred_element_type=F32)


_NN = ((1,), (0,))
_NT = ((1,), (1,))
_TN = ((0,), (0,))


@jax.custom_vjp
def _mm(a, b):
    return _dot(a, b, _NN)


def _mm_fwd(a, b):
    return _mm(a, b), (a, b)


def _mm_bwd(res, g):
    a, b = res
    return _dot(g, b, _NT).astype(a.dtype), _dot(a, g, _TN).astype(b.dtype)


_mm.defvjp(_mm_fwd, _mm_bwd)


@jax.custom_vjp
def _mm_nt(a, b):
    return _dot(a, b, _NT)


def _mm_nt_fwd(a, b):
    return _mm_nt(a, b), (a, b)


def _mm_nt_bwd(res, g):
    a, b = res
    return _dot(g, b, _NN).astype(a.dtype), _dot(g, a, _TN).astype(b.dtype)


_mm_nt.defvjp(_mm_nt_fwd, _mm_nt_bwd)


@jax.custom_vjp
def _mm_tn(a, b):
    return _dot(a, b, _TN)


def _mm_tn_fwd(a, b):
    return _mm_tn(a, b), (a, b)


def _mm_tn_bwd(res, g):
    a, b = res
    return _dot(b, g, _NT).astype(a.dtype), _dot(a, g, _NN).astype(b.dtype)


_mm_tn.defvjp(_mm_tn_fwd, _mm_tn_bwd)


def _dot_exact(m01, v):
    m = m01.astype(jnp.bfloat16)
    hi = v.astype(jnp.bfloat16)
    r1 = v - hi.astype(F32)
    mid = r1.astype(jnp.bfloat16)
    lo = (r1 - mid.astype(F32)).astype(jnp.bfloat16)
    out = jnp.dot(m, hi, preferred_element_type=F32)
    out = out + jnp.dot(m, mid, preferred_element_type=F32)
    return out + jnp.dot(m, lo, preferred_element_type=F32)


@jax.custom_vjp
def _lin01(m, mt, v):
    return _dot_exact(m, v)


def _lin01_fwd(m, mt, v):
    return _dot_exact(m, v), (m, mt)


def _lin01_bwd(res, g):
    m, mt = res
    return jnp.zeros_like(m), jnp.zeros_like(mt), _dot_exact(mt, g)


_lin01.defvjp(_lin01_fwd, _lin01_bwd)


def matmul_nn(name, a, b, out_dtype=F32):
    M, K = a.shape
    N = b.shape[1]
    tm = _pick_tile(M, (1088, 768, 512, 256, 128, 16))
    tn = _pick_tile(N, (512, 256, 128))

    def body(a_ref, b_ref, o_ref):
        o_ref[...] = _dot(a_ref[...], b_ref[...], _NN).astype(o_ref.dtype)

    return _pcall(
        body, name=name, out_shape=jax.ShapeDtypeStruct((M, N), out_dtype), grid=(N // tn, M // tm),
        in_specs=[pl.BlockSpec((tm, K), lambda j, i: (i, 0)), pl.BlockSpec((K, tn), lambda j, i: (0, j))],
        out_specs=pl.BlockSpec((tm, tn), lambda j, i: (i, j)), compiler_params=_params("parallel", "parallel"),
    )(a, b)


def matmul_nt(name, g, b, out_dtype=F32):
    M, N = g.shape
    K = b.shape[0]
    tm = _pick_tile(M, (256, 128, 16))
    tk = _pick_tile(K, (512, 256, 128))

    def body(g_ref, b_ref, o_ref):
        o_ref[...] = _dot(g_ref[...], b_ref[...], _NT).astype(o_ref.dtype)

    return _pcall(
        body, name=name, out_shape=jax.ShapeDtypeStruct((M, K), out_dtype), grid=(K // tk, M // tm),
        in_specs=[pl.BlockSpec((tm, N), lambda j, i: (i, 0)), pl.BlockSpec((tk, N), lambda j, i: (j, 0))],
        out_specs=pl.BlockSpec((tm, tk), lambda j, i: (i, j)), compiler_params=_params("parallel", "parallel"),
    )(g, b)


def matmul_tn(name, a, g):
    M, K = a.shape
    N = g.shape[1]
    tk = _pick_tile(K, (512, 256, 128))
    tn = _pick_tile(N, (512, 256, 128))

    def body(a_ref, g_ref, o_ref):
        o_ref[...] = _dot(a_ref[...], g_ref[...], _TN)

    return _pcall(
        body, name=name, out_shape=jax.ShapeDtypeStruct((K, N), F32), grid=(K // tk, N // tn),
        in_specs=[pl.BlockSpec((M, tk), lambda i, j: (0, i)), pl.BlockSpec((M, tn), lambda i, j: (0, j))],
        out_specs=pl.BlockSpec((tk, tn), lambda i, j: (i, j)), compiler_params=_params("parallel", "parallel"),
    )(a, g)


class Arg:
    def __init__(self, arr, block, imap, kind):
        self.arr, self.block, self.imap, self.kind = arr, block, imap, kind


class Rows:
    def __init__(self, nt, nct, tm, ncol=1):
        self.nt, self.nct, self.tm, self.ncol = nt, nct, tm, ncol

    def seg(self, i):
        return jnp.where(i >= self.nct, 1, 0)

    def spec(self, block, imap):
        return pl.BlockSpec(block, lambda j, i: imap(j, i, self.seg(i)))

    def row(self, arr, width, cb0=0, follow=False, roff=0, stride=1):
        f = stride if follow else 0
        return Arg(arr, (self.tm, width), lambda j, i, s: (i + roff, cb0 + f * j), "row")

    def vec(self, arr, follow=False, kind="acc"):
        w = arr.shape[1] // (self.ncol if follow else 1)
        f = 1 if follow else 0
        return Arg(arr, (1, w), lambda j, i, s: (0, f * j), kind)

    def segvec(self, arr, kind="seg"):
        return Arg(arr, (None, 1, arr.shape[2]), lambda j, i, s: (s, 0, 0), kind)


def _load(ref):
    return ref[...].astype(F32) if ref.dtype != F32 else ref[...]


def stage_fwd(name, f, rows, args, outs):
    n_in = len(args)

    def body(*refs):
        vals = [_load(r) for r in refs[:n_in]]
        res = f(*vals)
        for r, v in zip(refs[n_in:], res):
            r[...] = v.astype(r.dtype)

    T = rows.nt * rows.tm
    out_shape = [jax.ShapeDtypeStruct((T, w * (rows.ncol if fo else 1)), dt) for w, dt, fo in outs]
    out_specs = [pl.BlockSpec((rows.tm, w), (lambda j, i, fo=fo: (i, j if fo else 0))) for w, dt, fo in outs]
    res = _pcall(
        body, name=name, out_shape=out_shape, grid=(rows.ncol, rows.nt),
        in_specs=[rows.spec(a.block, a.imap) for a in args], out_specs=out_specs,
        compiler_params=_params("parallel", "parallel"),
    )(*[a.arr for a in args])
    return res


def stage_bwd(name, f, rows, args, cots, row_dtypes):
    n_in, n_ct = len(args), len(cots)
    diff = [k for k, a in enumerate(args) if a.kind != "const"]
    row_dt = {}
    for k in diff:
        if args[k].kind == "row":
            row_dt[k] = row_dtypes[len(row_dt)]

    def body(*refs):
        i = pl.program_id(1)
        vals = [_load(r) for r in refs[:n_in]]
        cts = tuple(_load(r) for r in refs[n_in:n_in + n_ct])
        outs = refs[n_in + n_ct:]

        def g(*dv):
            full = list(vals)
            for k, v in zip(diff, dv):
                full[k] = v
            return tuple(f(*full))

        _, vjp = jax.vjp(g, *[vals[k] for k in diff])
        grads = vjp(cts)
        for k, o, gr in zip(diff, outs, grads):
            kind = args[k].kind
            if kind == "row":
                o[...] = gr.astype(o.dtype)
            else:
                first = (i == 0) | (i == rows.nct) if kind == "seg" else (i == 0)

                @pl.when(first)
                def _():
                    o[...] = gr.astype(o.dtype)

                @pl.when(jnp.logical_not(first))
                def _():
                    o[...] += gr.astype(o.dtype)

    T = rows.nt * rows.tm
    out_shape, out_specs = [], []
    for k in diff:
        a = args[k]
        if a.kind == "row":
            out_shape.append(jax.ShapeDtypeStruct((T, a.block[1] * (rows.ncol if _follows(a) else 1)), row_dt[k]))
            fo = _follows(a)
            out_specs.append(pl.BlockSpec(a.block, (lambda j, i, fo=fo: (i, j if fo else 0))))
        else:
            out_shape.append(jax.ShapeDtypeStruct(a.arr.shape, F32))
            out_specs.append(rows.spec(a.block, a.imap))
    return _pcall(
        body, name=name, out_shape=out_shape, grid=(rows.ncol, rows.nt),
        in_specs=[rows.spec(a.block, a.imap) for a in list(args) + list(cots)], out_specs=out_specs,
        compiler_params=_params("arbitrary", "arbitrary"),
    )(*[a.arr for a in list(args) + list(cots)])


def _follows(a):
    return a.imap(1, 0, 0)[-1] != a.imap(0, 0, 0)[-1]


def _rms(x):
    return x * lax.rsqrt(jnp.mean(x * x, axis=-1, keepdims=True) + EPS)


def f_norm_mod(x, g, sh, sc):
    return ((_rms(x) * g) * (1.0 + sc) + sh,)


def f_resid_norm_mod(x, mo, ga, g, sh, sc):
    x1 = x + ga * mo
    return x1, (_rms(x1) * g) * (1.0 + sc) + sh


def f_resid(x, dn, ga):
    return (x + ga * dn,)


def f_silu(x):
    return (x * jax.nn.sigmoid(x),)


def f_bias(x, b):
    return (x + b,)


def f_ssd_gate(y0, y1, xs, z, dskip, nw):
    y = y0 + y1 + dskip * xs
    return (_rms(y * (z * jax.nn.sigmoid(z))) * nw,)


def f_pool(u, pmat, pmat_t, inv_cnt, pw, scale):
    pm = _lin01(pmat, pmat_t, u) * inv_cnt - u
    return (_mm(pm, pw) * scale,)


def f_merge(o_ssd, o_pool, gl_ssd, gl_pool):
    return (jax.nn.sigmoid(gl_ssd) * o_ssd + jax.nn.sigmoid(gl_pool) * o_pool,)


def f_swiglu(a, b):
    return ((a * jax.nn.sigmoid(a)) * b,)


def f_loss(x, tgt, g):
    err = _rms(x) * g - tgt
    return (0.5 * jnp.mean(err * err, axis=-1, keepdims=True),)


CONV_TILE = 128


def _shift_rows(v, j, n_ctx):
    if j == 0:
        return v
    T = v.shape[0]
    r = lax.broadcasted_iota(jnp.int32, v.shape, 0)
    lo = jnp.where(r >= n_ctx, n_ctx, 0)
    hi = jnp.where(r >= n_ctx, T, n_ctx)
    ok = (r + j >= lo) & (r + j < hi)
    return jnp.where(ok, pltpu.roll(v, (-j) % T, 0), 0.0)


def conv_fwd(name, proj, conv_w, conv_b, n_ctx, width):
    T = proj.shape[0]
    half = SSD_CONV // 2

    def body(u_ref, w_ref, b_ref, o_ref):
        u = u_ref[...]
        pre = jnp.broadcast_to(b_ref[...], u.shape)
        for k in range(SSD_CONV):
            pre = pre + w_ref[k:k + 1, :] * _shift_rows(u, k - half, n_ctx)
        o_ref[...] = pre * jax.nn.sigmoid(pre)

    col = lambda t: (0, t)
    return _pcall(
        body, name=name, out_shape=jax.ShapeDtypeStruct((T, width), F32), grid=(width // CONV_TILE,),
        in_specs=[pl.BlockSpec((T, CONV_TILE), col), pl.BlockSpec((SSD_CONV, CONV_TILE), col),
                  pl.BlockSpec((1, CONV_TILE), col)],
        out_specs=pl.BlockSpec((T, CONV_TILE), col), compiler_params=_params("parallel"),
    )(proj, conv_w, conv_b)


def conv_bwd(name, proj, conv_w, conv_b, d_act2, d_skip, n_ctx, width):
    T = proj.shape[0]
    half = SSD_CONV // 2

    def body(u_ref, w_ref, b_ref, c0_ref, c1_ref, cs_ref, du_ref, dw_ref, db_ref):
        t = pl.program_id(0)
        u = u_ref[...]
        pre = jnp.broadcast_to(b_ref[...], u.shape)
        for k in range(SSD_CONV):
            pre = pre + w_ref[k:k + 1, :] * _shift_rows(u, k - half, n_ctx)
        sg = jax.nn.sigmoid(pre)
        ct = c0_ref[...] + c1_ref[...] + jnp.where(t % 4 < 2, cs_ref[...], 0.0)
        dpre = ct * (sg * (1.0 + pre * (1.0 - sg)))
        du = jnp.zeros_like(u)
        for k in range(SSD_CONV):
            du = du + w_ref[k:k + 1, :] * _shift_rows(dpre, half - k, n_ctx)
            dw_ref[k:k + 1, :] = jnp.sum(dpre * _shift_rows(u, k - half, n_ctx), axis=0, keepdims=True)
        du_ref[...] = du.astype(du_ref.dtype)
        db_ref[...] = jnp.sum(dpre, axis=0, keepdims=True)

    col = lambda t: (0, t)
    skip_col = lambda t: (0, (t // 4) * 2 + jnp.minimum(t % 4, 1))
    return _pcall(
        body, name=name,
        out_shape=[jax.ShapeDtypeStruct((T, width), ACT_DTYPE), jax.ShapeDtypeStruct((SSD_CONV, width), F32),
                   jax.ShapeDtypeStruct((1, width), F32)],
        grid=(width // CONV_TILE,),
        in_specs=[pl.BlockSpec((T, CONV_TILE), col), pl.BlockSpec((SSD_CONV, CONV_TILE), col),
                  pl.BlockSpec((1, CONV_TILE), col), pl.BlockSpec((T, CONV_TILE), col),
                  pl.BlockSpec((T, CONV_TILE), lambda t: (1, t)), pl.BlockSpec((T, CONV_TILE), skip_col)],
        out_specs=[pl.BlockSpec((T, CONV_TILE), col), pl.BlockSpec((SSD_CONV, CONV_TILE), col),
                   pl.BlockSpec((1, CONV_TILE), col)],
        compiler_params=_params("parallel"),
    )(proj, conv_w, conv_b, d_act2, d_act2, d_skip)


@jax.custom_vjp
def _cumsum_mat(tri, tri_t, a):
    return jnp.dot(tri, a, precision=lax.Precision.HIGHEST, preferred_element_type=F32)


def _cumsum_fwd(tri, tri_t, a):
    return _cumsum_mat(tri, tri_t, a), (tri, tri_t)


def _cumsum_bwd(res, g):
    tri, tri_t = res
    return (jnp.zeros_like(tri), jnp.zeros_like(tri_t),
            jnp.dot(tri_t, g, precision=lax.Precision.HIGHEST, preferred_element_type=F32))


_cumsum_mat.defvjp(_cumsum_fwd, _cumsum_bwd)


def _ssd_chunk(xs, bm, cm, dtraw, s_in, dt_bias, a_log, tri, tri_t, mask, idx0):
    Q = xs.shape[0]
    hpg = xs.shape[1] // SSD_HEADDIM
    lane = lax.broadcasted_iota(jnp.int32, dtraw.shape, 1)
    lane1 = lax.broadcasted_iota(jnp.int32, (1, dtraw.shape[1]), 1)
    head = lax.broadcasted_iota(jnp.int32, xs.shape, 1) // SSD_HEADDIM
    head1 = lax.broadcasted_iota(jnp.int32, (1, xs.shape[1]), 1) // SSD_HEADDIM

    dt_all = jax.nn.softplus(dtraw + dt_bias)
    a_all = dt_all * (-jnp.exp(a_log))
    s_all = _cumsum_mat(tri, tri_t, a_all)

    def pick(v, r):
        return jnp.sum(jnp.where(lane == idx0 + r, v, 0.0), axis=1, keepdims=True)

    def expand(cols, hd):
        out = cols[hpg - 1]
        for r in range(hpg - 2, -1, -1):
            out = jnp.where(hd == r, cols[r], out)
        return out

    dt_r = [pick(dt_all, r) for r in range(hpg)]
    s_r = [pick(s_all, r) for r in range(hpg)]
    stot_r = [jnp.sum(jnp.where(lane == idx0 + r, a_all, 0.0), keepdims=True).reshape(1, 1) for r in range(hpg)]

    xd = xs * expand([jnp.broadcast_to(c, xs.shape) for c in dt_r], head)
    cb = _mm_nt(cm, bm)
    y = expand([jnp.broadcast_to(jnp.exp(c), xs.shape) for c in s_r], head) * _mm(cm, s_in)
    for r in range(hpg):
        sm = jnp.broadcast_to(s_r[r], (Q, Q))
        decay = jnp.exp(jnp.where(mask, sm - sm.T, NEG))
        y = y + _mm(cb * decay, jnp.where(head == r, xd, 0.0))
    to_end = expand([jnp.broadcast_to(jnp.exp(t - c), xs.shape) for t, c in zip(stot_r, s_r)], head)
    carry = expand([jnp.broadcast_to(jnp.exp(t), (1, xs.shape[1])) for t in stot_r], head1)
    s_out = carry * s_in + _mm_tn(bm, xd * to_end)
    return y, s_out


def _scan_consts():
    q = SSD_CHUNK
    i = np.arange(q)[:, None]
    j = np.arange(q)[None, :]
    fwd = (j <= i).astype(np.float32)
    bwd = (j >= i).astype(np.float32)
    tri = np.stack([fwd, bwd])
    return jnp.asarray(tri), jnp.asarray(np.stack([fwd.T, bwd.T]))


def _chunk_of(d, k, ncc, nc):
    rev = jnp.where(k < ncc, ncc - 1 - k, nc - 1 + ncc - k)
    return jnp.where(d == 0, k, rev)


def ssd_fwd(name, xbc, proj, dt_cb, dt_bias, a_log, n_ctx):
    T = xbc.shape[0]
    q, G = SSD_CHUNK, SSD_GROUPS
    nc, ncc = T // q, n_ctx // q
    gw = xbc.shape[1] // G
    xw = gw - 2 * SSD_STATE
    hpg = xw // SSD_HEADDIM
    nh = G * hpg
    tri, tri_t = _scan_consts()

    def body(x_ref, dt_ref, bias_ref, alog_ref, tri_ref, trit_ref, y_ref, sin_ref, state):
        d, g, k = pl.program_id(0), pl.program_id(1), pl.program_id(2)

        @pl.when(k == 0)
        def _():
            state[...] = jnp.zeros_like(state)

        s_in = state[...]
        sin_ref[...] = s_in
        y, s_out = _ssd_chunk(
            x_ref[:, :xw], x_ref[:, xw:xw + SSD_STATE], x_ref[:, xw + SSD_STATE:], dt_ref[...], s_in,
            bias_ref[...], alog_ref[...], tri_ref[...], trit_ref[...], tri_ref[...] > 0.5, d * nh + g * hpg)
        y_ref[...] = y
        state[...] = s_out

    ch = lambda d, g, k: _chunk_of(d, k, ncc, nc)
    return _pcall(
        body, name=name,
        out_shape=[jax.ShapeDtypeStruct((2 * T, G * xw), F32),
                   jax.ShapeDtypeStruct((2, nc, G, SSD_STATE, xw), F32)],
        grid=(2, G, nc),
        in_specs=[pl.BlockSpec((q, gw), lambda d, g, k: (ch(d, g, k), g)),
                  pl.BlockSpec((q, 128), lambda d, g, k: (ch(d, g, k), dt_cb)),
                  pl.BlockSpec((1, 128), lambda d, g, k: (0, 0)),
                  pl.BlockSpec((1, 128), lambda d, g, k: (0, 0)),
                  pl.BlockSpec((None, q, q), lambda d, g, k: (d, 0, 0)),
                  pl.BlockSpec((None, q, q), lambda d, g, k: (d, 0, 0))],
        out_specs=[pl.BlockSpec((q, xw), lambda d, g, k: (d * nc + ch(d, g, k), g)),
                   pl.BlockSpec((None, None, None, SSD_STATE, xw), lambda d, g, k: (d, k, g, 0, 0))],
        scratch_shapes=[pltpu.VMEM((SSD_STATE, xw), F32)],
        compiler_params=_params("arbitrary", "arbitrary", "arbitrary"),
    )(xbc, proj, dt_bias, a_log, tri, tri_t)


def ssd_bwd(name, xbc, proj, dt_cb, dt_bias, a_log, states, dy, n_ctx):
    T = xbc.shape[0]
    q, G = SSD_CHUNK, SSD_GROUPS
    nc, ncc = T // q, n_ctx // q
    gw = xbc.shape[1] // G
    xw = gw - 2 * SSD_STATE
    hpg = xw // SSD_HEADDIM
    nh = G * hpg
    tri, tri_t = _scan_consts()

    def body(x_ref, dt_ref, bias_ref, alog_ref, tri_ref, trit_ref, sin_ref, dy_ref,
             dx_ref, ddt_ref, dbias_ref, dalog_ref, dstate):
        d, g, k = pl.program_id(0), pl.program_id(1), pl.program_id(2)
        first = (d == 0) & (g == 0) & (k == 0)

        @pl.when(first)
        def _():
            ddt_ref[...] = jnp.zeros_like(ddt_ref)
            dbias_ref[...] = jnp.zeros_like(dbias_ref)
            dalog_ref[...] = jnp.zeros_like(dalog_ref)

        @pl.when(k == 0)
        def _():
            dstate[...] = jnp.zeros_like(dstate)

        tri_v, trit_v = tri_ref[...], trit_ref[...]
        mask = tri_v > 0.5
        idx0 = d * nh + g * hpg

        def fn(xs, bm, cm, dtraw, s_in, bias, alog):
            return _ssd_chunk(xs, bm, cm, dtraw, s_in, bias, alog, tri_v, trit_v, mask, idx0)

        _, vjp = jax.vjp(fn, x_ref[:, :xw], x_ref[:, xw:xw + SSD_STATE], x_ref[:, xw + SSD_STATE:], dt_ref[...],
                         sin_ref[...], bias_ref[...], alog_ref[...])
        dxs, dbm, dcm, ddt, ds_in, dbias, dalog = vjp((dy_ref[...], dstate[...]))
        dx_ref[:, :xw] = dxs
        dx_ref[:, xw:xw + SSD_STATE] = dbm
        dx_ref[:, xw + SSD_STATE:] = dcm
        dstate[...] = ds_in
        row0 = pl.multiple_of(_chunk_of(d, nc - 1 - k, ncc, nc) * q, q)
        ddt_ref[pl.ds(row0, q), :] += ddt
        dbias_ref[...] += dbias
        dalog_ref[...] += dalog

    ch = lambda d, g, k: _chunk_of(d, nc - 1 - k, ncc, nc)
    return _pcall(
        body, name=name,
        out_shape=[jax.ShapeDtypeStruct((2 * T, G * gw), F32), jax.ShapeDtypeStruct((T, 128), F32),
                   jax.ShapeDtypeStruct((1, 128), F32), jax.ShapeDtypeStruct((1, 128), F32)],
        grid=(2, G, nc),
        in_specs=[pl.BlockSpec((q, gw), lambda d, g, k: (ch(d, g, k), g)),
                  pl.BlockSpec((q, 128), lambda d, g, k: (ch(d, g, k), dt_cb)),
                  pl.BlockSpec((1, 128), lambda d, g, k: (0, 0)),
                  pl.BlockSpec((1, 128), lambda d, g, k: (0, 0)),
                  pl.BlockSpec((None, q, q), lambda d, g, k: (d, 0, 0)),
                  pl.BlockSpec((None, q, q), lambda d, g, k: (d, 0, 0)),
                  pl.BlockSpec((None, None, None, SSD_STATE, xw), lambda d, g, k: (d, nc - 1 - k, g, 0, 0)),
                  pl.BlockSpec((q, xw), lambda d, g, k: (ch(d, g, k), g))],
        out_specs=[pl.BlockSpec((q, gw), lambda d, g, k: (d * nc + ch(d, g, k), g)),
                   pl.BlockSpec((T, 128), lambda d, g, k: (0, 0)),
                   pl.BlockSpec((1, 128), lambda d, g, k: (0, 0)),
                   pl.BlockSpec((1, 128), lambda d, g, k: (0, 0))],
        scratch_shapes=[pltpu.VMEM((SSD_STATE, xw), F32)],
        compiler_params=_params("arbitrary", "arbitrary", "arbitrary"),
    )(xbc, proj, dt_bias, a_log, tri, tri_t, states, dy)


def _perm_xbc(a):
    G = SSD_GROUPS
    n = a.shape[-1]
    gn = G * SSD_STATE
    di = n - 2 * gn
    lead = a.shape[:-1]
    xs = a[..., :di].reshape(lead + (G, di // G))
    bm = a[..., di:di + gn].reshape(lead + (G, SSD_STATE))
    cm = a[..., di + gn:].reshape(lead + (G, SSD_STATE))
    return jnp.concatenate([xs, bm, cm], axis=-1).reshape(lead + (n,))


def _unperm_xbc(a):
    G = SSD_GROUPS
    n = a.shape[-1]
    gn = G * SSD_STATE
    di = n - 2 * gn
    lead = a.shape[:-1]
    r = a.reshape(lead + (G, n // G))
    xw = di // G
    return jnp.concatenate([r[..., :xw].reshape(lead + (di,)), r[..., xw:xw + SSD_STATE].reshape(lead + (gn,)),
                            r[..., xw + SSD_STATE:].reshape(lead + (gn,))], axis=-1)


def _pool_consts(tm, n_ctx):
    assert n_ctx == tm and tm % GRID_W == 0
    mats, cnts = [], []
    for seq in (n_ctx, GRID_W):
        t = np.arange(tm)
        tt = t % seq
        base = t - tt
        ms, cs = [], []
        for k in POOL_WINDOWS:
            lo = np.clip(tt - k // 2, 0, seq) + base
            hi = np.clip(tt + k // 2, 0, seq) + base
            m = ((t[None, :] >= lo[:, None]) & (t[None, :] < hi[:, None])).astype(np.float32)
            ms.append(m)
            cs.append((1.0 / (hi - lo).astype(np.float32))[:, None])
        mats.append(np.stack(ms))
        cnts.append(np.stack(cs))
    m = np.stack(mats)
    return jnp.asarray(m), jnp.asarray(np.swapaxes(m, -1, -2)), jnp.asarray(np.stack(cnts).astype(np.float32))


def _prep_layer_weights(w_ada, b_ada, g_mix, w_in, conv_w, conv_b, dt_bias, a_log, d_skip, ssd_norm_w, w_ssd_out,
                        pool_w, pool_scale, w_pool_out, w_out, g_ffn, w_gate_up, w_down):
    D = w_in.shape[0]
    di = ssd_norm_w.shape[0]
    xbc = conv_w.shape[1]
    nh2 = dt_bias.size
    pw = pool_scale.shape[0]
    o = 0
    wz = w_in[:, o:o + di]; o += di
    wx = w_in[:, o:o + xbc]; o += xbc
    wdt = w_in[:, o:o + nh2]; o += nh2
    wp = w_in[:, o:o + pw]; o += pw
    wg = w_in[:, o:]
    w1 = jnp.concatenate([_perm_xbc(wx), wz, wg, wp, wdt, jnp.zeros((D, DT_PAD - nh2), w_in.dtype)], axis=1)
    pad128 = lambda v: jnp.concatenate([v.reshape(1, -1), jnp.zeros((1, 128 - v.size), F32)], axis=1)
    return dict(
        w_ada=w_ada, b_ada=b_ada.reshape(1, -1), g_mix=g_mix.reshape(1, -1), w1=w1,
        conv_w=_perm_xbc(conv_w), conv_b=_perm_xbc(conv_b.reshape(1, -1)),
        dt_bias=pad128(dt_bias), a_log=pad128(a_log),
        dskip=jnp.repeat(d_skip[0] + d_skip[1], SSD_HEADDIM).reshape(1, -1),
        ssd_norm_w=ssd_norm_w.reshape(1, -1), w_ssd_out=w_ssd_out, pool_w=pool_w,
        pool_scale=pool_scale.reshape(1, -1), w_pool_out=w_pool_out, w_out=w_out, g_ffn=g_ffn.reshape(1, -1),
        w_gate_up=w_gate_up, w_down=w_down)


def _unprep_layer_grads(g, dims):
    di, xbc, nh2, pw = dims
    d1 = g["w1"]
    o = 0
    dxbc = d1[:, o:o + xbc]; o += xbc
    dz = d1[:, o:o + di]; o += di
    dg = d1[:, o:o + 2 * pw]; o += 2 * pw
    dp = d1[:, o:o + pw]; o += pw
    ddt = d1[:, o:o + nh2]
    nh = nh2 // 2
    dsk = g["dskip"].reshape(nh, SSD_HEADDIM).sum(axis=1)
    return dict(
        w_ada=g["w_ada"], b_ada=g["b_ada"].reshape(-1), g_mix=g["g_mix"].reshape(-1),
        w_in=jnp.concatenate([dz, _unperm_xbc(dxbc), ddt, dp, dg], axis=1),
        conv_w=_unperm_xbc(g["conv_w"]), conv_b=_unperm_xbc(g["conv_b"]).reshape(-1),
        dt_bias=g["dt_bias"][0, :nh2].reshape(2, nh), a_log=g["a_log"][0, :nh2].reshape(2, nh),
        d_skip=jnp.stack([dsk, dsk]), ssd_norm_w=g["ssd_norm_w"].reshape(-1), w_ssd_out=g["w_ssd_out"],
        pool_w=g["pool_w"], pool_scale=g["pool_scale"].reshape(-1), w_pool_out=g["w_pool_out"], w_out=g["w_out"],
        g_ffn=g["g_ffn"].reshape(-1), w_gate_up=g["w_gate_up"], w_down=g["w_down"])


COND_ROWS = 16


def _split_mods(m):
    d = m.shape[1] // 6
    return [m[:2, k * d:(k + 1) * d].reshape(2, 1, d) for k in range(6)]


def _layer_fwd(l, x, cond_s, w, rows, n_ctx, pc):
    T, D = x.shape
    nt, nct, tm = rows.nt, rows.nct, rows.tm
    n = lambda s: f"l{l}_{s}"
    crow = Rows(1, 0, COND_ROWS)
    mraw = matmul_nn(n("ada_mm"), cond_s, w["w_ada"])
    (m,) = stage_fwd(n("ada_bias"), f_bias, crow, [crow.row(mraw, mraw.shape[1]), crow.vec(w["b_ada"])],
                     [(mraw.shape[1], F32, False)])
    sh1, sc1, ga1, sh2, sc2, ga2 = _split_mods(m)

    (h1,) = stage_fwd(n("norm1"), f_norm_mod, rows,
                      [rows.row(x, D), rows.vec(w["g_mix"]), rows.segvec(sh1), rows.segvec(sc1)],
                      [(D, ACT_DTYPE, False)])
    proj = matmul_nn(n("in_mm"), h1, w["w1"])
    xbc_w = w["conv_w"].shape[1]
    di = w["ssd_norm_w"].shape[1]
    pw = w["pool_scale"].shape[1]
    c_z, c_g, c_p, c_dt = xbc_w, xbc_w + di, xbc_w + di + 2 * pw, xbc_w + di + 3 * pw
    xbc = conv_fwd(n("conv"), proj, w["conv_w"], w["conv_b"], n_ctx, xbc_w)
    y2, states = ssd_fwd(n("ssd"), xbc, proj, c_dt // 128, w["dt_bias"], w["a_log"], n_ctx)

    G = SSD_GROUPS
    gw = di // G
    r8 = Rows(nt, nct, tm, G)
    gate_args = [r8.row(y2, gw, 0, True), r8.row(y2, gw, 0, True, roff=nt), r8.row(xbc, gw, 0, True, stride=2),
                 r8.row(proj, gw, c_z // gw, True), r8.vec(w["dskip"], True), r8.vec(w["ssd_norm_w"], True)]
    (ynw,) = stage_fwd(n("ssd_gate"), f_ssd_gate, r8, gate_args, [(gw, ACT_DTYPE, True)])
    o_ssd = matmul_nn(n("ssd_out_mm"), ynw, w["w_ssd_out"])

    nw = len(POOL_WINDOWS)
    pg = pw // nw
    r4 = Rows(nt, nct, tm, nw)
    pmat, pmat_t, inv_cnt = pc
    cblk = lambda a: Arg(a, (None, None) + a.shape[2:], lambda j, i, s: (s, j, 0, 0), "const")
    pool_args = [r4.row(proj, pg, c_p // pg, True), cblk(pmat), cblk(pmat_t), cblk(inv_cnt),
                 Arg(w["pool_w"], (None, pg, pg), lambda j, i, s: (j, 0, 0), "acc"), r4.vec(w["pool_scale"], True)]
    (ps,) = stage_fwd(n("pool"), f_pool, r4, pool_args, [(pg, ACT_DTYPE, True)])
    o_pool = matmul_nn(n("pool_out_mm"), ps, w["w_pool_out"])

    merge_args = [rows.row(o_ssd, D), rows.row(o_pool, D), rows.row(proj, pw, c_g // pw), rows.row(proj, pw, c_g // pw + 1)]
    (mg,) = stage_fwd(n("merge"), f_merge, rows, merge_args, [(D, ACT_DTYPE, False)])
    mo = matmul_nn(n("out_mm"), mg, w["w_out"])

    rn_args = [rows.row(x, D), rows.row(mo, D), rows.segvec(ga1), rows.vec(w["g_ffn"]), rows.segvec(sh2), rows.segvec(sc2)]
    x1, h2 = stage_fwd(n("norm2"), f_resid_norm_mod, rows, rn_args, [(D, F32, False), (D, ACT_DTYPE, False)])
    gu = matmul_nn(n("gate_up_mm"), h2, w["w_gate_up"])
    fh = gu.shape[1] // 2
    sw_args = [rows.row(gu, fh, 0), rows.row(gu, fh, 1)]
    (act,) = stage_fwd(n("swiglu"), f_swiglu, rows, sw_args, [(fh, ACT_DTYPE, False)])
    dn = matmul_nn(n("down_mm"), act, w["w_down"])
    res_args = [rows.row(x1, D), rows.row(dn, D), rows.segvec(ga2)]
    (x2,) = stage_fwd(n("resid2"), f_resid, rows, res_args, [(D, F32, False)])
    saved = dict(x=x, mraw=mraw, mods=(sh1, sc1, ga1, sh2, sc2, ga2), h1=h1, proj=proj, xbc=xbc, y2=y2, states=states,
                 ynw=ynw, o_ssd=o_ssd, ps=ps, o_pool=o_pool, mg=mg, mo=mo, x1=x1, h2=h2, gu=gu, act=act, dn=dn,
                 cols=(c_z, c_g, c_p, c_dt))
    return x2, saved


def f_norm_mod_keep(x, g, sh, sc):
    return f_norm_mod(x, g, sh, sc)[0], x


def _layer_bwd(l, dx2, cond_s, w, s, rows, n_ctx, pc):
    T, D = dx2.shape
    nt, nct, tm = rows.nt, rows.nct, rows.tm
    n = lambda t: f"l{l}_{t}_bwd"
    sh1, sc1, ga1, sh2, sc2, ga2 = s["mods"]
    c_z, c_g, c_p, c_dt = s["cols"]
    x, proj, xbc, y2, gu = s["x"], s["proj"], s["xbc"], s["y2"], s["gu"]
    g = {}

    res_args = [rows.row(s["x1"], D), rows.row(s["dn"], D), rows.segvec(ga2)]
    dx1, ddn, dga2 = stage_bwd(n("resid2"), f_resid, rows, res_args, [rows.row(dx2, D)], [F32, ACT_DTYPE])
    dact = matmul_nt(n("down_dx"), ddn, w["w_down"])
    g["w_down"] = matmul_tn(n("down_dw"), s["act"], ddn)
    fh = gu.shape[1] // 2
    sw_args = [rows.row(gu, fh, 0), rows.row(gu, fh, 1)]
    da, db = stage_bwd(n("swiglu"), f_swiglu, rows, sw_args, [rows.row(dact, fh)], [ACT_DTYPE, ACT_DTYPE])
    dgu = jnp.concatenate([da, db], axis=1)
    dh2 = matmul_nt(n("gate_up_dx"), dgu, w["w_gate_up"])
    g["w_gate_up"] = matmul_tn(n("gate_up_dw"), s["h2"], dgu)

    rn_args = [rows.row(x, D), rows.row(s["mo"], D), rows.segvec(ga1), rows.vec(w["g_ffn"]), rows.segvec(sh2), rows.segvec(sc2)]
    dxr, dmo, dga1, g["g_ffn"], dsh2, dsc2 = stage_bwd(
        n("norm2"), f_resid_norm_mod, rows, rn_args, [rows.row(dx1, D), rows.row(dh2, D)], [F32, ACT_DTYPE])
    dmg = matmul_nt(n("out_dx"), dmo, w["w_out"])
    g["w_out"] = matmul_tn(n("out_dw"), s["mg"], dmo)

    pw = w["pool_scale"].shape[1]
    merge_args = [rows.row(s["o_ssd"], D), rows.row(s["o_pool"], D), rows.row(proj, pw, c_g // pw), rows.row(proj, pw, c_g // pw + 1)]
    do_ssd, do_pool, dgl_s, dgl_p = stage_bwd(n("merge"), f_merge, rows, merge_args, [rows.row(dmg, D)], [ACT_DTYPE] * 4)
    dps = matmul_nt(n("pool_out_dx"), do_pool, w["w_pool_out"])
    g["w_pool_out"] = matmul_tn(n("pool_out_dw"), s["ps"], do_pool)

    nw = len(POOL_WINDOWS)
    pg = pw // nw
    r4 = Rows(nt, nct, tm, nw)
    pmat, pmat_t, inv_cnt = pc
    cblk = lambda a: Arg(a, (None, None) + a.shape[2:], lambda j, i, s_: (s_, j, 0, 0), "const")
    pool_args = [r4.row(proj, pg, c_p // pg, True), cblk(pmat), cblk(pmat_t), cblk(inv_cnt),
                 Arg(w["pool_w"], (None, pg, pg), lambda j, i, s_: (j, 0, 0), "acc"), r4.vec(w["pool_scale"], True)]
    du_pool, g["pool_w"], g["pool_scale"] = stage_bwd(n("pool"), f_pool, r4, pool_args, [r4.row(dps, pg, 0, True)], [ACT_DTYPE])

    dynw = matmul_nt(n("ssd_out_dx"), do_ssd, w["w_ssd_out"])
    g["w_ssd_out"] = matmul_tn(n("ssd_out_dw"), s["ynw"], do_ssd)
    G = SSD_GROUPS
    di = w["ssd_norm_w"].shape[1]
    gw = di // G
    r8 = Rows(nt, nct, tm, G)
    gate_args = [r8.row(y2, gw, 0, True), r8.row(y2, gw, 0, True, roff=nt), r8.row(xbc, gw, 0, True, stride=2),
                 r8.row(proj, gw, c_z // gw, True), r8.vec(w["dskip"], True), r8.vec(w["ssd_norm_w"], True)]
    dy, _, dxs_skip, dz, g["dskip"], g["ssd_norm_w"] = stage_bwd(
        n("ssd_gate"), f_ssd_gate, r8, gate_args, [r8.row(dynw, gw, 0, True)], [F32, F32, F32, ACT_DTYPE])

    dxbc2, ddt, g["dt_bias"], g["a_log"] = ssd_bwd(n("ssd"), xbc, proj, c_dt // 128, w["dt_bias"], w["a_log"], s["states"], dy, n_ctx)
    xbc_w = xbc.shape[1]
    dxbc_raw, g["conv_w"], g["conv_b"] = conv_bwd(n("conv"), proj, w["conv_w"], w["conv_b"], dxbc2, dxs_skip, n_ctx, xbc_w)
    dproj = jnp.concatenate([dxbc_raw, dz, dgl_s, dgl_p, du_pool, ddt.astype(ACT_DTYPE),
                             jnp.zeros((T, DT_PAD - 128), ACT_DTYPE)], axis=1)
    dh1 = matmul_nt(n("in_dx"), dproj, w["w1"])
    g["w1"] = matmul_tn(n("in_dw"), s["h1"], dproj)

    n1_args = [rows.row(x, D), rows.vec(w["g_mix"]), rows.segvec(sh1), rows.segvec(sc1)]
    dx, g["g_mix"], dsh1, dsc1 = stage_bwd(n("norm1"), f_norm_mod_keep, rows, n1_args,
                                           [rows.row(dh1, D), rows.row(dxr, D)], [F32])

    dm = jnp.concatenate([v.reshape(2, D) for v in (dsh1, dsc1, dga1, dsh2, dsc2, dga2)], axis=1)
    dm = jnp.concatenate([dm, jnp.zeros((COND_ROWS - 2, dm.shape[1]), F32)], axis=0)
    crow = Rows(1, 0, COND_ROWS)
    dmraw, g["b_ada"] = stage_bwd(n("ada_bias"), f_bias, crow, [crow.row(s["mraw"], dm.shape[1]), crow.vec(w["b_ada"])],
                                  [crow.row(dm, dm.shape[1])], [ACT_DTYPE])
    dcs = matmul_nt(n("ada_dx"), dmraw, w["w_ada"])
    g["w_ada"] = matmul_tn(n("ada_dw"), cond_s, dmraw)
    return dx, dcs, g


def local_step(x, ctx, c, c_ctx, target, layer_w, g_final):
    L, D = x.shape
    n_ctx = ctx.shape[0]
    tm = ROW_TILE
    T = L + n_ctx
    rows = Rows(T // tm, n_ctx // tm, tm)
    pc = _pool_consts(tm, n_ctx)
    xa = jnp.concatenate([ctx, x], axis=0)
    cond = jnp.concatenate([c_ctx.reshape(1, D), c.reshape(1, D), jnp.zeros((COND_ROWS - 2, D), F32)], axis=0)
    crow = Rows(1, 0, COND_ROWS)
    (cond_s,) = stage_fwd("cond_silu", f_silu, crow, [crow.row(cond, D)], [(D, ACT_DTYPE, False)])

    saved = []
    for l, w in enumerate(layer_w):
        xa, s = _layer_fwd(l, xa, cond_s, w, rows, n_ctx, pc)
        saved.append(s)

    rl = Rows(L // tm, 0, tm)
    gf = g_final.reshape(1, D)
    tgt = rl.row(target, D)
    tgt.kind = "const"
    loss_args = [rl.row(xa, D, roff=n_ctx // tm), tgt, rl.vec(gf)]
    (loss_rows,) = stage_fwd("loss", f_loss, rl, loss_args, [(1, F32, False)])
    ones = jnp.ones((L, 1), F32)
    dx_lat, dgf = stage_bwd("loss_bwd", f_loss, rl, loss_args, [rl.row(ones, 1)], [F32])
    loss = jnp.sum(loss_rows)
    dx = jnp.concatenate([jnp.zeros((n_ctx, D), F32), dx_lat], axis=0)

    grads = [None] * len(layer_w)
    dcs = jnp.zeros((COND_ROWS, D), F32)
    for l in reversed(range(len(layer_w))):
        dx, dcs_l, grads[l] = _layer_bwd(l, dx, cond_s, layer_w[l], saved[l], rows, n_ctx, pc)
        dcs = dcs + dcs_l
    (dcond,) = stage_bwd("cond_silu_bwd", f_silu, crow, [crow.row(cond, D)], [crow.row(dcs, D)], [F32])
    return loss, dx[n_ctx:], grads, dcond[0], dgf


PLANE = ((1, 0, 0), (0, 1, 0), (1, 1, 0))
PAIR = ((0, 0, 1),)
EVERYONE = tuple((a, b, d) for a in (0, 1) for b in (0, 1) for d in (0, 1) if a + b + d)
HBM = pl.BlockSpec(memory_space=pl.ANY)


def _position():
    return lax.axis_index("x"), lax.axis_index("y"), lax.axis_index("c")


def _flip(pos, rel):
    return tuple(1 - p if r else p for p, r in zip(pos, rel))


def _chip_index(pos):
    return 2 * pos[0] + pos[1]


def _device_index(pos):
    return 4 * pos[0] + 2 * pos[1] + pos[2]


def comm_call(name, phases, n_remote, n_local, operands, out_shapes, aliases=None):
    n_in, n_out = len(operands), len(out_shapes)

    def body(*refs):
        ins, outs = refs[:n_in], refs[n_in:n_in + n_out]
        send_sems, recv_sems, local_sems = refs[n_in + n_out:]
        pos = _position()
        k = kl = 0
        for local, remote in phases(ins, outs, pos):
            cps = []
            for src, dst in local:
                cps.append(pltpu.make_async_copy(src, dst, local_sems.at[kl]))
                kl += 1
            for src, dst, peer in remote:
                cps.append(pltpu.make_async_remote_copy(src_ref=src, dst_ref=dst, send_sem=send_sems.at[k],
                                                        recv_sem=recv_sems.at[k], device_id=peer, device_id_type=MESH))
                k += 1
            for cp in cps:
                cp.start()
            for cp in cps:
                cp.wait()

        assert k == n_remote and kl == n_local, (name, k, kl)

    return _pcall(
        body, name=name, out_shape=list(out_shapes), in_specs=[HBM] * n_in, out_specs=[HBM] * n_out,
        scratch_shapes=[pltpu.SemaphoreType.DMA((n_remote,)), pltpu.SemaphoreType.DMA((n_remote,)),
                        pltpu.SemaphoreType.DMA((max(n_local, 1),))],
        input_output_aliases=aliases or {},
        compiler_params=pltpu.CompilerParams(has_side_effects=True),
    )(*operands)


def gather_weights(shards):
    n = len(shards)

    def phases(ins, outs, pos):
        c = pos[2]
        me = _chip_index(pos)
        first_l = [(s.at[c], o.at[c, me]) for s, o in zip(ins, outs)]
        first_r = [(s.at[c], o.at[c, me], _flip(pos, rel)) for rel in PLANE for s, o in zip(ins, outs)]
        second_r = [(o.at[c], o.at[c], _flip(pos, PAIR[0])) for o in outs]
        return [(first_l, first_r), ([], second_r)]

    shapes = [jax.ShapeDtypeStruct((2, 4) + s.shape[1:], s.dtype) for s in shards]
    return comm_call("gather_weights", phases, 4 * n, n, shards, shapes)


def swap_other_layer(grads):
    n = len(grads)

    def phases(ins, outs, pos):
        c = pos[2]
        return [([], [(g.at[1 - c], o, _flip(pos, PAIR[0])) for g, o in zip(ins, outs)])]

    shapes = [jax.ShapeDtypeStruct(g.shape[1:], g.dtype) for g in grads]
    return comm_call("grad_pair_swap", phases, n, 0, grads, shapes)


def scatter_blocks(sums):
    n = len(sums)

    def phases(ins, outs, pos):
        me = _chip_index(pos)
        local = [(p.at[me], o.at[me]) for p, o in zip(ins, outs)]
        remote = []
        for rel in PLANE:
            peer = _flip(pos, rel)
            remote += [(p.at[_chip_index(peer)], o.at[me], peer) for p, o in zip(ins, outs)]
        return [(local, remote)]

    shapes = [jax.ShapeDtypeStruct(p.shape, p.dtype) for p in sums]
    return comm_call("grad_scatter", phases, 3 * n, n, sums, shapes)


def share_layers(finals):
    n = len(finals)

    def phases(ins, outs, pos):
        c = pos[2]
        local = [(f, o.at[c]) for f, o in zip(ins, outs)]
        remote = [(f, o.at[c], _flip(pos, PAIR[0])) for f, o in zip(ins, outs)]
        return [(local, remote)]

    shapes = [jax.ShapeDtypeStruct((2,) + f.shape, f.dtype) for f in finals]
    return comm_call("grad_pair_share", phases, n, n, finals, shapes)


def gather_small(vec):
    def phases(ins, outs, pos):
        me = _device_index(pos)
        (v,), (o,) = ins, outs
        return [([(v, o.at[me])], [(v, o.at[me], _flip(pos, rel)) for rel in EVERYONE])]

    return comm_call("gather_small", phases, len(EVERYONE), 1, [vec], [jax.ShapeDtypeStruct((8,) + vec.shape, vec.dtype)])[0]


def _row_tile(rows, cols, n_bufs):
    cap = VMEM_LIMIT_BYTES // 4 // (2 * n_bufs * cols * 4)
    for t in (1024, 512, 256, 128, 64, 32, 16, 8):
        if t <= cap and rows % t == 0:
            return t
    return rows


def add_own_layer(name, grads, recv, c):
    _, nb, R, C = grads.shape
    tr = _row_tile(R, C, 3)

    def body(c_ref, g_ref, r_ref, o_ref):
        o_ref[...] = g_ref[...] + r_ref[...]

    spec = pl.BlockSpec((None, tr, C), lambda b, i, c_ref: (b, i, 0))
    return _pcall(
        body, name=name, out_shape=jax.ShapeDtypeStruct(recv.shape, F32),
        grid_spec=pltpu.PrefetchScalarGridSpec(
            num_scalar_prefetch=1, grid=(nb, R // tr),
            in_specs=[pl.BlockSpec((None, None, tr, C), lambda b, i, c_ref: (c_ref[0], b, i, 0)), spec],
            out_specs=spec),
        compiler_params=_params("parallel", "parallel"),
    )(c, grads, recv)


def sum_slots(name, a):
    n, R, C = a.shape
    tr = _row_tile(R, C, n + 1)

    def body(a_ref, o_ref):
        acc = a_ref[0]
        for k in range(1, n):
            acc = acc + a_ref[k]
        o_ref[...] = acc

    return _pcall(
        body, name=name, out_shape=jax.ShapeDtypeStruct((R, C), a.dtype), grid=(R // tr,),
        in_specs=[pl.BlockSpec((n, tr, C), lambda i: (0, i, 0))], out_specs=pl.BlockSpec((tr, C), lambda i: (i, 0)),
        compiler_params=_params("parallel"),
    )(a)


def adamw(name, w, g, m, v):
    R, C = w.shape
    tr = _row_tile(R, C, 7)

    def body(w_ref, g_ref, m_ref, v_ref, d_ref, nm_ref, nv_ref):
        gr = g_ref[...]
        nm = ADAM_B1 * m_ref[...] + (1.0 - ADAM_B1) * gr
        nv = ADAM_B2 * v_ref[...] + (1.0 - ADAM_B2) * jnp.square(gr)
        m_hat = nm / (1.0 - ADAM_B1 ** ADAM_STEP)
        v_hat = nv / (1.0 - ADAM_B2 ** ADAM_STEP)
        d_ref[...] = -ADAM_LR * (m_hat / (jnp.sqrt(v_hat) + ADAM_EPS) + ADAM_WD * w_ref[...])
        nm_ref[...] = nm
        nv_ref[...] = nv

    spec = pl.BlockSpec((tr, C), lambda i: (i, 0))
    return _pcall(
        body, name=name, out_shape=[jax.ShapeDtypeStruct((R, C), F32)] * 3, grid=(R // tr,),
        in_specs=[spec] * 4, out_specs=[spec] * 3, compiler_params=_params("parallel"),
    )(w, g, m, v)


BIG = ("w_ada", "w_in", "w_ssd_out", "pool_w", "w_pool_out", "w_out", "w_gate_up", "w_down")
COL_SHARDED = ("w_ada", "w_in", "w_gate_up")
SMALL = ("c_ctx", "b_ada", "g_mix", "conv_w", "conv_b", "dt_bias", "a_log", "d_skip", "ssd_norm_w", "pool_scale",
         "g_ffn", "g_final")
WEIGHTS = ("c_ctx", "w_ada", "b_ada", "g_mix", "w_in", "conv_w", "conv_b", "dt_bias", "a_log", "d_skip", "ssd_norm_w",
           "w_ssd_out", "pool_w", "pool_scale", "w_pool_out", "w_out", "g_ffn", "w_gate_up", "w_down", "g_final")
LAYER_KEYS = ("w_ada", "b_ada", "g_mix", "w_in", "conv_w", "conv_b", "dt_bias", "a_log", "d_skip", "ssd_norm_w",
              "w_ssd_out", "pool_w", "pool_scale", "w_pool_out", "w_out", "g_ffn", "w_gate_up", "w_down")


def _shard2d(name, a):
    if name == "pool_w":
        return a.reshape(a.shape[0], a.shape[1] * a.shape[2], a.shape[3])
    return a


def _full_from_blocks(name, a):
    nb, R, C = a.shape
    if name in COL_SHARDED:
        return jnp.transpose(a, (1, 0, 2)).reshape(R, nb * C)
    if name == "pool_w":
        nw = len(POOL_WINDOWS)
        return jnp.transpose(a.reshape(nb, nw, R // nw, C), (1, 0, 2, 3)).reshape(nw, nb * R // nw, C)
    return a.reshape(nb * R, C)


def _blocks_from_full(name, g):
    nb = 4
    if name in COL_SHARDED:
        K, N = g.shape
        return jnp.transpose(g.reshape(K, nb, N // nb), (1, 0, 2))
    if name == "pool_w":
        nw, r, C = g.shape
        return jnp.transpose(g.reshape(nw, nb, r // nb, C), (1, 0, 2, 3)).reshape(nb, nw * r // nb, C)
    return g.reshape(nb, g.shape[0] // nb, g.shape[1])


def _pack(arrs, rows):
    flat = jnp.concatenate([a.reshape(-1).astype(F32) for a in arrs])
    return jnp.concatenate([flat, jnp.zeros((rows * 128 - flat.size,), F32)]).reshape(rows, 128)


def _unpack(vec, shapes):
    flat = vec.reshape(-1)
    out, o = [], 0
    for s in shapes:
        n = int(np.prod(s))
        out.append(flat[o:o + n].reshape(s))
        o += n
    return out


def _rows_for(shapes):
    n = sum(int(np.prod(s)) for s in shapes)
    return -(-n // (8 * 128)) * 8


def kernel(x, c, ctx, c_ctx, w_ada, b_ada, g_mix, w_in, conv_w, conv_b, dt_bias, a_log, d_skip, ssd_norm_w, w_ssd_out, pool_w, pool_scale, w_pool_out, w_out, g_ffn, w_gate_up, w_down, g_final, loss_target, m_c_ctx, m_w_ada, m_b_ada, m_g_mix, m_w_in, m_conv_w, m_conv_b, m_dt_bias, m_a_log, m_d_skip, m_ssd_norm_w, m_w_ssd_out, m_pool_w, m_pool_scale, m_w_pool_out, m_w_out, m_g_ffn, m_w_gate_up, m_w_down, m_g_final, v_c_ctx, v_w_ada, v_b_ada, v_g_mix, v_w_in, v_conv_w, v_conv_b, v_dt_bias, v_a_log, v_d_skip, v_ssd_norm_w, v_w_ssd_out, v_pool_w, v_pool_scale, v_w_pool_out, v_w_out, v_g_ffn, v_w_gate_up, v_w_down, v_g_final):
    w = dict(c_ctx=c_ctx, w_ada=w_ada, b_ada=b_ada, g_mix=g_mix, w_in=w_in, conv_w=conv_w, conv_b=conv_b, dt_bias=dt_bias,
             a_log=a_log, d_skip=d_skip, ssd_norm_w=ssd_norm_w, w_ssd_out=w_ssd_out, pool_w=pool_w, pool_scale=pool_scale,
             w_pool_out=w_pool_out, w_out=w_out, g_ffn=g_ffn, w_gate_up=w_gate_up, w_down=w_down, g_final=g_final)
    m = dict(c_ctx=m_c_ctx, w_ada=m_w_ada, b_ada=m_b_ada, g_mix=m_g_mix, w_in=m_w_in, conv_w=m_conv_w, conv_b=m_conv_b,
             dt_bias=m_dt_bias, a_log=m_a_log, d_skip=m_d_skip, ssd_norm_w=m_ssd_norm_w, w_ssd_out=m_w_ssd_out,
             pool_w=m_pool_w, pool_scale=m_pool_scale, w_pool_out=m_w_pool_out, w_out=m_w_out, g_ffn=m_g_ffn,
             w_gate_up=m_w_gate_up, w_down=m_w_down, g_final=m_g_final)
    v = dict(c_ctx=v_c_ctx, w_ada=v_w_ada, b_ada=v_b_ada, g_mix=v_g_mix, w_in=v_w_in, conv_w=v_conv_w, conv_b=v_conv_b,
             dt_bias=v_dt_bias, a_log=v_a_log, d_skip=v_d_skip, ssd_norm_w=v_ssd_norm_w, w_ssd_out=v_w_ssd_out,
             pool_w=v_pool_w, pool_scale=v_pool_scale, w_pool_out=v_w_pool_out, w_out=v_w_out, g_ffn=v_g_ffn,
             w_gate_up=v_w_gate_up, w_down=v_w_down, g_final=v_g_final)
    assert x.shape[0] == 1, "one example per device"
    pos = _position()
    core = pos[2].astype(jnp.int32).reshape(1)
    n_layers = w_in.shape[0]

    shards = [_shard2d(k, w[k]).astype(MXU_DTYPE) for k in BIG] + [conv_w]
    gathered = gather_weights(shards)
    full = {k: [_full_from_blocks(k, a[l]) for l in range(n_layers)] for k, a in zip(BIG, gathered[:-1])}
    conv_full = gathered[-1]
    full["conv_w"] = [jnp.transpose(conv_full[l], (1, 0, 2)).reshape(conv_full.shape[2], -1) for l in range(n_layers)]
    layer_w = []
    for l in range(n_layers):
        args = [full[k][l] if k in full else w[k][l] for k in LAYER_KEYS]
        layer_w.append(_prep_layer_weights(*args))

    loss, grad_x, grads, d_c_ctx, d_g_final = local_step(x[0], ctx[0], c[0], c_ctx, loss_target[0], layer_w, g_final)
    loss = lax.psum(loss, ("x", "y", "c"))
    dims = (ssd_norm_w.shape[1], conv_w.shape[2] * 4, dt_bias[0].size, pool_scale.shape[1])
    gl = [_unprep_layer_grads(g, dims) for g in grads]

    stacked = [jnp.stack([_blocks_from_full(k, gl[l][k]) for l in range(n_layers)]) for k in BIG]
    recv = swap_other_layer(stacked)
    pair = [add_own_layer(f"pair_sum_{k}", g, r, core) for k, g, r in zip(BIG, stacked, recv)]
    parts = scatter_blocks(pair)
    finals = [sum_slots(f"chip_sum_{k}", p) for k, p in zip(BIG, parts)]
    reduced = dict(zip(BIG, share_layers(finals)))

    small_full = dict(c_ctx=d_c_ctx, g_final=d_g_final.reshape(-1))
    for k in SMALL:
        if k not in small_full:
            small_full[k] = jnp.stack([gl[l][k] for l in range(n_layers)])
    shapes = [small_full[k].shape for k in SMALL]
    packed = _pack([small_full[k] for k in SMALL], _rows_for(shapes))
    total = sum_slots("small_sum", gather_small(packed))
    small_g = dict(zip(SMALL, _unpack(total, shapes)))
    cw = conv_w.shape[2]
    small_g["conv_w"] = lax.dynamic_slice_in_dim(small_g["conv_w"], _chip_index(pos) * cw, cw, axis=2)

    grad, delta, new_m, new_v = {}, {}, {}, {}
    for k in BIG:
        g2 = reduced[k]
        shp = w[k].shape
        flat = lambda a: a.reshape(-1, g2.shape[-1])
        d_, m_, v_ = adamw(f"adamw_{k}", flat(w[k]), flat(g2), flat(m[k]), flat(v[k]))
        grad[k], delta[k], new_m[k], new_v[k] = g2.reshape(shp), d_.reshape(shp), m_.reshape(shp), v_.reshape(shp)
    sshapes = [w[k].shape for k in SMALL]
    srows = _rows_for(sshapes)
    d_, m_, v_ = adamw("adamw_small", _pack([w[k] for k in SMALL], srows), _pack([small_g[k] for k in SMALL], srows),
                       _pack([m[k] for k in SMALL], srows), _pack([v[k] for k in SMALL], srows))
    for k, dd, mm, vv in zip(SMALL, _unpack(d_, sshapes), _unpack(m_, sshapes), _unpack(v_, sshapes)):
        grad[k], delta[k], new_m[k], new_v[k] = small_g[k], dd, mm, vv

    return (loss, grad_x[None], *[grad[k] for k in WEIGHTS], *[delta[k] for k in WEIGHTS],
            *[new_m[k] for k in WEIGHTS], *[new_v[k] for k in WEIGHTS])
```

```python
import functools

import jax
import jax.numpy as jnp
import numpy as np
from jax import lax
from jax.experimental import pallas as pl
from jax.experimental.pallas import tpu as pltpu

F32 = jnp.float32
MXU_DTYPE = jnp.bfloat16
ACT_DTYPE = jnp.bfloat16
VMEM_LIMIT_BYTES = 48 * 1024 * 1024
EPS = 1e-6
NEG = -1e30

SSD_HEADDIM = 64
SSD_GROUPS = 8
SSD_STATE = 128
SSD_CHUNK = 128
SSD_CONV = 5
GRID_W = 64
POOL_WINDOWS = (2, 4, 8, 16)
ROW_TILE = 256
DT_PAD = 512

ADAM_LR = 0.001
ADAM_B1 = 0.9
ADAM_B2 = 0.999
ADAM_EPS = 1e-08
ADAM_WD = 0.01
ADAM_STEP = 10

MESH = pl.DeviceIdType.MESH


def _pcall(body, **kw):
    return pl.pallas_call(body, **kw)


def _params(*sem):
    return pltpu.CompilerParams(dimension_semantics=tuple(sem), vmem_limit_bytes=VMEM_LIMIT_BYTES)


def _pick_tile(n, cands):
    for t in cands:
        if n % t == 0:
            return t
    return n


PLANE = ((1, 0, 0), (0, 1, 0), (1, 1, 0))
PAIR = ((0, 0, 1),)
EVERYONE = tuple((a, b, d) for a in (0, 1) for b in (0, 1) for d in (0, 1) if a + b + d)
HBM = pl.BlockSpec(memory_space=pl.ANY)


def _position():
    return lax.axis_index("x"), lax.axis_index("y"), lax.axis_index("c")


def _flip(pos, rel):
    return tuple(1 - p if r else p for p, r in zip(pos, rel))


def _chip_index(pos):
    return 2 * pos[0] + pos[1]


def _device_index(pos):
    return 4 * pos[0] + 2 * pos[1] + pos[2]


class Exchange:
    def __init__(self, copies, n_remote, n_local, operands, out_shapes, aliases=None):
        self.copies, self.n_remote, self.n_local = copies, n_remote, n_local
        self.operands, self.out_shapes, self.aliases = list(operands), list(out_shapes), dict(aliases or {})

    def scratch(self):
        return [pltpu.SemaphoreType.DMA((max(self.n_remote, 1),)), pltpu.SemaphoreType.DMA((max(self.n_remote, 1),)),
                pltpu.SemaphoreType.DMA((max(self.n_local, 1),))]

    def descriptors(self, ins, outs, sems):
        send_sems, recv_sems, local_sems = sems
        local, remote = self.copies(ins, outs, _position())
        assert len(local) == self.n_local and len(remote) == self.n_remote
        cps = [pltpu.make_async_copy(src, dst, local_sems.at[k]) for k, (src, dst) in enumerate(local)]
        cps += [pltpu.make_async_remote_copy(src_ref=src, dst_ref=dst, send_sem=send_sems.at[k], recv_sem=recv_sems.at[k],
                                             device_id=peer, device_id_type=MESH) for k, (src, dst, peer) in enumerate(remote)]
        return cps


def comm_call(name, ex):
    n_in, n_out = len(ex.operands), len(ex.out_shapes)

    def body(*refs):
        cps = ex.descriptors(refs[:n_in], refs[n_in:n_in + n_out], refs[n_in + n_out:])
        for cp in cps:
            cp.start()
        for cp in cps:
            cp.wait()

    return _pcall(
        body, name=name, out_shape=ex.out_shapes, in_specs=[HBM] * n_in, out_specs=[HBM] * n_out,
        scratch_shapes=ex.scratch(), input_output_aliases=ex.aliases,
        compiler_params=pltpu.CompilerParams(has_side_effects=True),
    )(*ex.operands)


def hosted_call(body, ex, operands, *, name, out_shape, grid, in_specs, out_specs, scratch_shapes=()):
    n_in, n_out, n_scr = len(operands), len(out_shape), len(scratch_shapes)
    sem = ("arbitrary",) * len(grid)
    if ex is None:
        res = _pcall(body, name=name, out_shape=list(out_shape), grid=grid, in_specs=list(in_specs),
                     out_specs=list(out_specs), scratch_shapes=list(scratch_shapes), compiler_params=_params(*sem))(*operands)
        return res, []
    x_in, x_out = len(ex.operands), len(ex.out_shapes)

    def wrapped(*refs):
        o = 0
        ins = refs[o:o + n_in]; o += n_in
        xins = refs[o:o + x_in]; o += x_in
        outs = refs[o:o + n_out]; o += n_out
        xouts = refs[o:o + x_out]; o += x_out
        scr = refs[o:o + n_scr]; o += n_scr
        sems = refs[o:]
        first = last = None
        for a, n in enumerate(grid):
            i = pl.program_id(a)
            first = (i == 0) if first is None else first & (i == 0)
            last = (i == n - 1) if last is None else last & (i == n - 1)

        @pl.when(first)
        def _():
            for cp in ex.descriptors(xins, xouts, sems):
                cp.start()

        body(*ins, *outs, *scr)

        @pl.when(last)
        def _():
            for cp in ex.descriptors(xins, xouts, sems):
                cp.wait()

    aliases = {n_in + k: n_out + v for k, v in ex.aliases.items()}
    res = _pcall(
        wrapped, name=name, out_shape=list(out_shape) + ex.out_shapes, grid=grid,
        in_specs=list(in_specs) + [HBM] * x_in, out_specs=list(out_specs) + [HBM] * x_out,
        scratch_shapes=list(scratch_shapes) + ex.scratch(), input_output_aliases=aliases,
        compiler_params=pltpu.CompilerParams(dimension_semantics=sem, vmem_limit_bytes=VMEM_LIMIT_BYTES,
                                             has_side_effects=True),
    )(*operands, *ex.operands)
    return res[:n_out], res[n_out:]


def _dot(a, b, dims):
    return lax.dot_general(a.astype(MXU_DTYPE), b.astype(MXU_DTYPE), (dims, ((), ())), preferred_element_type=F32)


_NN = ((1,), (0,))
_NT = ((1,), (1,))
_TN = ((0,), (0,))


@jax.custom_vjp
def _mm(a, b):
    return _dot(a, b, _NN)


def _mm_fwd(a, b):
    return _mm(a, b), (a, b)


def _mm_bwd(res, g):
    a, b = res
    return _dot(g, b, _NT).astype(a.dtype), _dot(a, g, _TN).astype(b.dtype)


_mm.defvjp(_mm_fwd, _mm_bwd)


@jax.custom_vjp
def _mm_nt(a, b):
    return _dot(a, b, _NT)


def _mm_nt_fwd(a, b):
    return _mm_nt(a, b), (a, b)


def _mm_nt_bwd(res, g):
    a, b = res
    return _dot(g, b, _NN).astype(a.dtype), _dot(g, a, _TN).astype(b.dtype)


_mm_nt.defvjp(_mm_nt_fwd, _mm_nt_bwd)


@jax.custom_vjp
def _mm_tn(a, b):
    return _dot(a, b, _TN)


def _mm_tn_fwd(a, b):
    return _mm_tn(a, b), (a, b)


def _mm_tn_bwd(res, g):
    a, b = res
    return _dot(b, g, _NT).astype(a.dtype), _dot(a, g, _NN).astype(b.dtype)


_mm_tn.defvjp(_mm_tn_fwd, _mm_tn_bwd)


def _dot_exact(m01, v):
    m = m01.astype(jnp.bfloat16)
    hi = v.astype(jnp.bfloat16)
    r1 = v - hi.astype(F32)
    mid = r1.astype(jnp.bfloat16)
    lo = (r1 - mid.astype(F32)).astype(jnp.bfloat16)
    out = jnp.dot(m, hi, preferred_element_type=F32)
    out = out + jnp.dot(m, mid, preferred_element_type=F32)
    return out + jnp.dot(m, lo, preferred_element_type=F32)


@jax.custom_vjp
def _lin01(m, mt, v):
    return _dot_exact(m, v)


def _lin01_fwd(m, mt, v):
    return _dot_exact(m, v), (m, mt)


def _lin01_bwd(res, g):
    m, mt = res
    return jnp.zeros_like(m), jnp.zeros_like(mt), _dot_exact(mt, g)


_lin01.defvjp(_lin01_fwd, _lin01_bwd)


MATMUL_VMEM_BUDGET = VMEM_LIMIT_BYTES * 3 // 4


def _mm_tiles(m, n, k_bytes_a, k_bytes_b, out_bytes, cands_m, cands_n):
    best = None
    for tm in cands_m:
        if m % tm:
            continue
        for tn in cands_n:
            if n % tn:
                continue
            need = 2 * (tm * k_bytes_a + tn * k_bytes_b + tm * tn * out_bytes)
            if need <= MATMUL_VMEM_BUDGET and (best is None or tm * tn > best[0] * best[1]):
                best = (tm, tn)
    assert best is not None, (m, n)
    return best


_ROW_CANDS = (1088, 768, 544, 512, 272, 256, 128, 16)
_COL_CANDS = (2816, 2048, 1408, 1024, 512, 256, 128)


def _one(res, xres, ex):
    return res[0] if ex is None else (res[0], xres)


def matmul_nn(name, a, b, out_dtype=F32, ex=None):
    M, K = a.shape
    N = b.shape[1]
    tm, tn = _mm_tiles(M, N, K * a.dtype.itemsize, K * b.dtype.itemsize, jnp.dtype(out_dtype).itemsize,
                       _ROW_CANDS, (512, 256, 128))

    def body(a_ref, b_ref, o_ref):
        o_ref[...] = _dot(a_ref[...], b_ref[...], _NN).astype(o_ref.dtype)

    res, xres = hosted_call(
        body, ex, [a, b], name=name, out_shape=[jax.ShapeDtypeStruct((M, N), out_dtype)], grid=(N // tn, M // tm),
        in_specs=[pl.BlockSpec((tm, K), lambda j, i: (i, 0)), pl.BlockSpec((K, tn), lambda j, i: (0, j))],
        out_specs=[pl.BlockSpec((tm, tn), lambda j, i: (i, j))])
    return _one(res, xres, ex)


def matmul_nt(name, g, b, out_dtype=F32, ex=None):
    M, N = g.shape
    K = b.shape[0]
    tm, tk = _mm_tiles(M, K, N * g.dtype.itemsize, N * b.dtype.itemsize, jnp.dtype(out_dtype).itemsize,
                       _ROW_CANDS, _COL_CANDS)

    def body(g_ref, b_ref, o_ref):
        o_ref[...] = _dot(g_ref[...], b_ref[...], _NT).astype(o_ref.dtype)

    res, xres = hosted_call(
        body, ex, [g, b], name=name, out_shape=[jax.ShapeDtypeStruct((M, K), out_dtype)], grid=(K // tk, M // tm),
        in_specs=[pl.BlockSpec((tm, N), lambda j, i: (i, 0)), pl.BlockSpec((tk, N), lambda j, i: (j, 0))],
        out_specs=[pl.BlockSpec((tm, tk), lambda j, i: (i, j))])
    return _one(res, xres, ex)


def matmul_tn(name, a, g, ex=None):
    M, K = a.shape
    N = g.shape[1]
    tk, tn = _mm_tiles(K, N, M * a.dtype.itemsize, M * g.dtype.itemsize, 4, (512, 256, 128), (512, 256, 128))

    def body(a_ref, g_ref, o_ref):
        o_ref[...] = _dot(a_ref[...], g_ref[...], _TN)

    res, xres = hosted_call(
        body, ex, [a, g], name=name, out_shape=[jax.ShapeDtypeStruct((K, N), F32)], grid=(K // tk, N // tn),
        in_specs=[pl.BlockSpec((M, tk), lambda i, j: (0, i)), pl.BlockSpec((M, tn), lambda i, j: (0, j))],
        out_specs=[pl.BlockSpec((tk, tn), lambda i, j: (i, j))])
    return _one(res, xres, ex)


class Arg:
    def __init__(self, arr, block, imap, kind):
        self.arr, self.block, self.imap, self.kind = arr, block, imap, kind


class Rows:
    def __init__(self, nt, nct, tm, ncol=1):
        self.nt, self.nct, self.tm, self.ncol = nt, nct, tm, ncol

    def seg(self, i):
        return jnp.where(i >= self.nct, 1, 0)

    def spec(self, block, imap):
        return pl.BlockSpec(block, lambda j, i: imap(j, i, self.seg(i)))

    def row(self, arr, width, cb0=0, follow=False, roff=0, stride=1):
        f = stride if follow else 0
        return Arg(arr, (self.tm, width), lambda j, i, s: (i + roff, cb0 + f * j), "row")

    def vec(self, arr, follow=False, kind="acc"):
        w = arr.shape[1] // (self.ncol if follow else 1)
        f = 1 if follow else 0
        return Arg(arr, (1, w), lambda j, i, s: (0, f * j), kind)

    def segvec(self, arr, kind="seg"):
        return Arg(arr, (None, 1, arr.shape[2]), lambda j, i, s: (s, 0, 0), kind)


def _load(ref):
    return ref[...].astype(F32) if ref.dtype != F32 else ref[...]


def stage_fwd(name, f, rows, args, outs):
    n_in = len(args)

    def body(*refs):
        vals = [_load(r) for r in refs[:n_in]]
        res = f(*vals)
        for r, v in zip(refs[n_in:], res):
            r[...] = v.astype(r.dtype)

    T = rows.nt * rows.tm
    out_shape = [jax.ShapeDtypeStruct((T, w * (rows.ncol if fo else 1)), dt) for w, dt, fo in outs]
    out_specs = [pl.BlockSpec((rows.tm, w), (lambda j, i, fo=fo: (i, j if fo else 0))) for w, dt, fo in outs]
    res = _pcall(
        body, name=name, out_shape=out_shape, grid=(rows.ncol, rows.nt),
        in_specs=[rows.spec(a.block, a.imap) for a in args], out_specs=out_specs,
        compiler_params=_params("parallel", "parallel"),
    )(*[a.arr for a in args])
    return res


def stage_bwd(name, f, rows, args, cots, row_dtypes):
    n_in, n_ct = len(args), len(cots)
    diff = [k for k, a in enumerate(args) if a.kind != "const"]
    row_dt = {}
    for k in diff:
        if args[k].kind == "row":
            row_dt[k] = row_dtypes[len(row_dt)]

    def body(*refs):
        i = pl.program_id(1)
        vals = [_load(r) for r in refs[:n_in]]
        cts = tuple(_load(r) for r in refs[n_in:n_in + n_ct])
        outs = refs[n_in + n_ct:]

        def g(*dv):
            full = list(vals)
            for k, v in zip(diff, dv):
                full[k] = v
            return tuple(f(*full))

        _, vjp = jax.vjp(g, *[vals[k] for k in diff])
        grads = vjp(cts)
        for k, o, gr in zip(diff, outs, grads):
            kind = args[k].kind
            if kind == "row":
                o[...] = gr.astype(o.dtype)
            else:
                first = (i == 0) | (i == rows.nct) if kind == "seg" else (i == 0)

                @pl.when(first)
                def _():
                    o[...] = gr.astype(o.dtype)

                @pl.when(jnp.logical_not(first))
                def _():
                    o[...] += gr.astype(o.dtype)

    T = rows.nt * rows.tm
    out_shape, out_specs = [], []
    for k in diff:
        a = args[k]
        if a.kind == "row":
            out_shape.append(jax.ShapeDtypeStruct((T, a.block[1] * (rows.ncol if _follows(a) else 1)), row_dt[k]))
            fo = _follows(a)
            out_specs.append(pl.BlockSpec(a.block, (lambda j, i, fo=fo: (i, j if fo else 0))))
        else:
            out_shape.append(jax.ShapeDtypeStruct(a.arr.shape, F32))
            out_specs.append(rows.spec(a.block, a.imap))
    return _pcall(
        body, name=name, out_shape=out_shape, grid=(rows.ncol, rows.nt),
        in_specs=[rows.spec(a.block, a.imap) for a in list(args) + list(cots)], out_specs=out_specs,
        compiler_params=_params("arbitrary", "arbitrary"),
    )(*[a.arr for a in list(args) + list(cots)])


def _follows(a):
    return a.imap(1, 0, 0)[-1] != a.imap(0, 0, 0)[-1]


def _rms(x):
    return x * lax.rsqrt(jnp.mean(x * x, axis=-1, keepdims=True) + EPS)


def f_norm_mod(x, g, sh, sc):
    return ((_rms(x) * g) * (1.0 + sc) + sh,)


def f_resid_norm_mod(x, mo, ga, g, sh, sc):
    x1 = x + ga * mo
    return x1, (_rms(x1) * g) * (1.0 + sc) + sh


def f_resid(x, dn, ga):
    return (x + ga * dn,)


def f_silu(x):
    return (x * jax.nn.sigmoid(x),)


def f_bias(x, b):
    return (x + b,)


def f_ssd_gate(y0, y1, xs, z, dskip, nw):
    y = y0 + y1 + dskip * xs
    return (_rms(y * (z * jax.nn.sigmoid(z))) * nw,)


def f_pool(u, pmat, pmat_t, inv_cnt, pw, scale):
    pm = _lin01(pmat, pmat_t, u) * inv_cnt - u
    return (_mm(pm, pw) * scale,)


def f_merge(o_ssd, o_pool, gl_ssd, gl_pool):
    return (jax.nn.sigmoid(gl_ssd) * o_ssd + jax.nn.sigmoid(gl_pool) * o_pool,)


def f_swiglu(a, b):
    return ((a * jax.nn.sigmoid(a)) * b,)


def f_loss(x, tgt, g):
    err = _rms(x) * g - tgt
    return (0.5 * jnp.mean(err * err, axis=-1, keepdims=True),)


CONV_TILE = 128


def _shift_rows(v, j, n_ctx):
    if j == 0:
        return v
    T = v.shape[0]
    r = lax.broadcasted_iota(jnp.int32, v.shape, 0)
    lo = jnp.where(r >= n_ctx, n_ctx, 0)
    hi = jnp.where(r >= n_ctx, T, n_ctx)
    ok = (r + j >= lo) & (r + j < hi)
    return jnp.where(ok, pltpu.roll(v, (-j) % T, 0), 0.0)


def conv_fwd(name, proj, conv_w, conv_b, n_ctx, width):
    T = proj.shape[0]
    half = SSD_CONV // 2

    def body(u_ref, w_ref, b_ref, o_ref):
        u = u_ref[...]
        pre = jnp.broadcast_to(b_ref[...], u.shape)
        for k in range(SSD_CONV):
            pre = pre + w_ref[k:k + 1, :] * _shift_rows(u, k - half, n_ctx)
        o_ref[...] = pre * jax.nn.sigmoid(pre)

    col = lambda t: (0, t)
    return _pcall(
        body, name=name, out_shape=jax.ShapeDtypeStruct((T, width), F32), grid=(width // CONV_TILE,),
        in_specs=[pl.BlockSpec((T, CONV_TILE), col), pl.BlockSpec((SSD_CONV, CONV_TILE), col),
                  pl.BlockSpec((1, CONV_TILE), col)],
        out_specs=pl.BlockSpec((T, CONV_TILE), col), compiler_params=_params("parallel"),
    )(proj, conv_w, conv_b)


def conv_bwd(name, proj, conv_w, conv_b, d_act2, d_skip, n_ctx, width):
    T = proj.shape[0]
    half = SSD_CONV // 2

    def body(u_ref, w_ref, b_ref, c0_ref, c1_ref, cs_ref, du_ref, dw_ref, db_ref):
        t = pl.program_id(0)
        u = u_ref[...]
        pre = jnp.broadcast_to(b_ref[...], u.shape)
        for k in range(SSD_CONV):
            pre = pre + w_ref[k:k + 1, :] * _shift_rows(u, k - half, n_ctx)
        sg = jax.nn.sigmoid(pre)
        ct = c0_ref[...] + c1_ref[...] + jnp.where(t % 4 < 2, cs_ref[...], 0.0)
        dpre = ct * (sg * (1.0 + pre * (1.0 - sg)))
        du = jnp.zeros_like(u)
        for k in range(SSD_CONV):
            du = du + w_ref[k:k + 1, :] * _shift_rows(dpre, half - k, n_ctx)
            dw_ref[k:k + 1, :] = jnp.sum(dpre * _shift_rows(u, k - half, n_ctx), axis=0, keepdims=True)
        du_ref[...] = du.astype(du_ref.dtype)
        db_ref[...] = jnp.sum(dpre, axis=0, keepdims=True)

    col = lambda t: (0, t)
    skip_col = lambda t: (0, (t // 4) * 2 + jnp.minimum(t % 4, 1))
    return _pcall(
        body, name=name,
        out_shape=[jax.ShapeDtypeStruct((T, width), ACT_DTYPE), jax.ShapeDtypeStruct((SSD_CONV, width), F32),
                   jax.ShapeDtypeStruct((1, width), F32)],
        grid=(width // CONV_TILE,),
        in_specs=[pl.BlockSpec((T, CONV_TILE), col), pl.BlockSpec((SSD_CONV, CONV_TILE), col),
                  pl.BlockSpec((1, CONV_TILE), col), pl.BlockSpec((T, CONV_TILE), col),
                  pl.BlockSpec((T, CONV_TILE), lambda t: (1, t)), pl.BlockSpec((T, CONV_TILE), skip_col)],
        out_specs=[pl.BlockSpec((T, CONV_TILE), col), pl.BlockSpec((SSD_CONV, CONV_TILE), col),
                   pl.BlockSpec((1, CONV_TILE), col)],
        compiler_params=_params("parallel"),
    )(proj, conv_w, conv_b, d_act2, d_act2, d_skip)


@jax.custom_vjp
def _cumsum_mat(tri, tri_t, a):
    return jnp.dot(tri, a, precision=lax.Precision.HIGHEST, preferred_element_type=F32)


def _cumsum_fwd(tri, tri_t, a):
    return _cumsum_mat(tri, tri_t, a), (tri, tri_t)


def _cumsum_bwd(res, g):
    tri, tri_t = res
    return (jnp.zeros_like(tri), jnp.zeros_like(tri_t),
            jnp.dot(tri_t, g, precision=lax.Precision.HIGHEST, preferred_element_type=F32))


_cumsum_mat.defvjp(_cumsum_fwd, _cumsum_bwd)


def _ssd_chunk(xs, bm, cm, dtraw, s_in, dt_bias, a_log, tri, tri_t, mask, idx0):
    Q = xs.shape[0]
    hpg = xs.shape[1] // SSD_HEADDIM
    lane = lax.broadcasted_iota(jnp.int32, dtraw.shape, 1)
    lane1 = lax.broadcasted_iota(jnp.int32, (1, dtraw.shape[1]), 1)
    head = lax.broadcasted_iota(jnp.int32, xs.shape, 1) // SSD_HEADDIM
    head1 = lax.broadcasted_iota(jnp.int32, (1, xs.shape[1]), 1) // SSD_HEADDIM

    dt_all = jax.nn.softplus(dtraw + dt_bias)
    a_all = dt_all * (-jnp.exp(a_log))
    s_all = _cumsum_mat(tri, tri_t, a_all)

    def pick(v, r):
        return jnp.sum(jnp.where(lane == idx0 + r, v, 0.0), axis=1, keepdims=True)

    def expand(cols, hd):
        out = cols[hpg - 1]
        for r in range(hpg - 2, -1, -1):
            out = jnp.where(hd == r, cols[r], out)
        return out

    dt_r = [pick(dt_all, r) for r in range(hpg)]
    s_r = [pick(s_all, r) for r in range(hpg)]
    stot_r = [jnp.sum(jnp.where(lane == idx0 + r, a_all, 0.0), keepdims=True).reshape(1, 1) for r in range(hpg)]

    xd = xs * expand([jnp.broadcast_to(c, xs.shape) for c in dt_r], head)
    cb = _mm_nt(cm, bm)
    y = expand([jnp.broadcast_to(jnp.exp(c), xs.shape) for c in s_r], head) * _mm(cm, s_in)
    for r in range(hpg):
        sm = jnp.broadcast_to(s_r[r], (Q, Q))
        decay = jnp.exp(jnp.where(mask, sm - sm.T, NEG))
        y = y + _mm(cb * decay, jnp.where(head == r, xd, 0.0))
    to_end = expand([jnp.broadcast_to(jnp.exp(t - c), xs.shape) for t, c in zip(stot_r, s_r)], head)
    carry = expand([jnp.broadcast_to(jnp.exp(t), (1, xs.shape[1])) for t in stot_r], head1)
    s_out = carry * s_in + _mm_tn(bm, xd * to_end)
    return y, s_out


def _scan_consts():
    q = SSD_CHUNK
    i = np.arange(q)[:, None]
    j = np.arange(q)[None, :]
    fwd = (j <= i).astype(np.float32)
    bwd = (j >= i).astype(np.float32)
    tri = np.stack([fwd, bwd])
    return jnp.asarray(tri), jnp.asarray(np.stack([fwd.T, bwd.T]))


def _chunk_of(d, k, ncc, nc):
    rev = jnp.where(k < ncc, ncc - 1 - k, nc - 1 + ncc - k)
    return jnp.where(d == 0, k, rev)


def ssd_fwd(name, xbc, proj, dt_cb, dt_bias, a_log, n_ctx, ex=None):
    T = xbc.shape[0]
    q, G = SSD_CHUNK, SSD_GROUPS
    nc, ncc = T // q, n_ctx // q
    gw = xbc.shape[1] // G
    xw = gw - 2 * SSD_STATE
    hpg = xw // SSD_HEADDIM
    nh = G * hpg
    tri, tri_t = _scan_consts()

    def body(x_ref, dt_ref, bias_ref, alog_ref, tri_ref, trit_ref, y_ref, sin_ref, state):
        d, g, k = pl.program_id(0), pl.program_id(1), pl.program_id(2)

        @pl.when(k == 0)
        def _():
            state[...] = jnp.zeros_like(state)

        s_in = state[...]
        sin_ref[...] = s_in
        y, s_out = _ssd_chunk(
            x_ref[:, :xw], x_ref[:, xw:xw + SSD_STATE], x_ref[:, xw + SSD_STATE:], dt_ref[...], s_in,
            bias_ref[...], alog_ref[...], tri_ref[...], trit_ref[...], tri_ref[...] > 0.5, d * nh + g * hpg)
        y_ref[...] = y
        state[...] = s_out

    ch = lambda d, g, k: _chunk_of(d, k, ncc, nc)
    res, xres = hosted_call(
        body, ex, [xbc, proj, dt_bias, a_log, tri, tri_t], name=name,
        out_shape=[jax.ShapeDtypeStruct((2 * T, G * xw), F32),
                   jax.ShapeDtypeStruct((2, nc, G, SSD_STATE, xw), F32)],
        grid=(2, G, nc),
        in_specs=[pl.BlockSpec((q, gw), lambda d, g, k: (ch(d, g, k), g)),
                  pl.BlockSpec((q, 128), lambda d, g, k: (ch(d, g, k), dt_cb)),
                  pl.BlockSpec((1, 128), lambda d, g, k: (0, 0)),
                  pl.BlockSpec((1, 128), lambda d, g, k: (0, 0)),
                  pl.BlockSpec((None, q, q), lambda d, g, k: (d, 0, 0)),
                  pl.BlockSpec((None, q, q), lambda d, g, k: (d, 0, 0))],
        out_specs=[pl.BlockSpec((q, xw), lambda d, g, k: (d * nc + ch(d, g, k), g)),
                   pl.BlockSpec((None, None, None, SSD_STATE, xw), lambda d, g, k: (d, k, g, 0, 0))],
        scratch_shapes=[pltpu.VMEM((SSD_STATE, xw), F32)])
    return res[0], res[1], xres


def ssd_bwd(name, xbc, proj, dt_cb, dt_bias, a_log, states, dy, n_ctx, ex=None):
    T = xbc.shape[0]
    q, G = SSD_CHUNK, SSD_GROUPS
    nc, ncc = T // q, n_ctx // q
    gw = xbc.shape[1] // G
    xw = gw - 2 * SSD_STATE
    hpg = xw // SSD_HEADDIM
    nh = G * hpg
    tri, tri_t = _scan_consts()

    def body(x_ref, dt_ref, bias_ref, alog_ref, tri_ref, trit_ref, sin_ref, dy_ref,
             dx_ref, ddt_ref, dbias_ref, dalog_ref, dstate):
        d, g, k = pl.program_id(0), pl.program_id(1), pl.program_id(2)
        first = (d == 0) & (g == 0) & (k == 0)

        @pl.when(first)
        def _():
            ddt_ref[...] = jnp.zeros_like(ddt_ref)
            dbias_ref[...] = jnp.zeros_like(dbias_ref)
            dalog_ref[...] = jnp.zeros_like(dalog_ref)

        @pl.when(k == 0)
        def _():
            dstate[...] = jnp.zeros_like(dstate)

        tri_v, trit_v = tri_ref[...], trit_ref[...]
        mask = tri_v > 0.5
        idx0 = d * nh + g * hpg

        def fn(xs, bm, cm, dtraw, s_in, bias, alog):
            return _ssd_chunk(xs, bm, cm, dtraw, s_in, bias, alog, tri_v, trit_v, mask, idx0)

        _, vjp = jax.vjp(fn, x_ref[:, :xw], x_ref[:, xw:xw + SSD_STATE], x_ref[:, xw + SSD_STATE:], dt_ref[...],
                         sin_ref[...], bias_ref[...], alog_ref[...])
        dxs, dbm, dcm, ddt, ds_in, dbias, dalog = vjp((dy_ref[...], dstate[...]))
        dx_ref[:, :xw] = dxs
        dx_ref[:, xw:xw + SSD_STATE] = dbm
        dx_ref[:, xw + SSD_STATE:] = dcm
        dstate[...] = ds_in
        row0 = pl.multiple_of(_chunk_of(d, nc - 1 - k, ncc, nc) * q, q)
        ddt_ref[pl.ds(row0, q), :] += ddt
        dbias_ref[...] += dbias
        dalog_ref[...] += dalog

    ch = lambda d, g, k: _chunk_of(d, nc - 1 - k, ncc, nc)
    res, xres = hosted_call(
        body, ex, [xbc, proj, dt_bias, a_log, tri, tri_t, states, dy], name=name,
        out_shape=[jax.ShapeDtypeStruct((2 * T, G * gw), F32), jax.ShapeDtypeStruct((T, 128), F32),
                   jax.ShapeDtypeStruct((1, 128), F32), jax.ShapeDtypeStruct((1, 128), F32)],
        grid=(2, G, nc),
        in_specs=[pl.BlockSpec((q, gw), lambda d, g, k: (ch(d, g, k), g)),
                  pl.BlockSpec((q, 128), lambda d, g, k: (ch(d, g, k), dt_cb)),
                  pl.BlockSpec((1, 128), lambda d, g, k: (0, 0)),
                  pl.BlockSpec((1, 128), lambda d, g, k: (0, 0)),
                  pl.BlockSpec((None, q, q), lambda d, g, k: (d, 0, 0)),
                  pl.BlockSpec((None, q, q), lambda d, g, k: (d, 0, 0)),
                  pl.BlockSpec((None, None, None, SSD_STATE, xw), lambda d, g, k: (d, nc - 1 - k, g, 0, 0)),
                  pl.BlockSpec((q, xw), lambda d, g, k: (ch(d, g, k), g))],
        out_specs=[pl.BlockSpec((q, gw), lambda d, g, k: (d * nc + ch(d, g, k), g)),
                   pl.BlockSpec((T, 128), lambda d, g, k: (0, 0)),
                   pl.BlockSpec((1, 128), lambda d, g, k: (0, 0)),
                   pl.BlockSpec((1, 128), lambda d, g, k: (0, 0))],
        scratch_shapes=[pltpu.VMEM((SSD_STATE, xw), F32)])
    return res[0], res[1], res[2], res[3], xres


def _perm_xbc(a):
    G = SSD_GROUPS
    n = a.shape[-1]
    gn = G * SSD_STATE
    di = n - 2 * gn
    lead = a.shape[:-1]
    xs = a[..., :di].reshape(lead + (G, di // G))
    bm = a[..., di:di + gn].reshape(lead + (G, SSD_STATE))
    cm = a[..., di + gn:].reshape(lead + (G, SSD_STATE))
    return jnp.concatenate([xs, bm, cm], axis=-1).reshape(lead + (n,))


def _unperm_xbc(a):
    G = SSD_GROUPS
    n = a.shape[-1]
    gn = G * SSD_STATE
    di = n - 2 * gn
    lead = a.shape[:-1]
    r = a.reshape(lead + (G, n // G))
    xw = di // G
    return jnp.concatenate([r[..., :xw].reshape(lead + (di,)), r[..., xw:xw + SSD_STATE].reshape(lead + (gn,)),
                            r[..., xw + SSD_STATE:].reshape(lead + (gn,))], axis=-1)


def _pool_consts(tm, n_ctx):
    assert n_ctx == tm and tm % GRID_W == 0
    mats, cnts = [], []
    for seq in (n_ctx, GRID_W):
        t = np.arange(tm)
        tt = t % seq
        base = t - tt
        ms, cs = [], []
        for k in POOL_WINDOWS:
            lo = np.clip(tt - k // 2, 0, seq) + base
            hi = np.clip(tt + k // 2, 0, seq) + base
            m = ((t[None, :] >= lo[:, None]) & (t[None, :] < hi[:, None])).astype(np.float32)
            ms.append(m)
            cs.append((1.0 / (hi - lo).astype(np.float32))[:, None])
        mats.append(np.stack(ms))
        cnts.append(np.stack(cs))
    m = np.stack(mats)
    return jnp.asarray(m), jnp.asarray(np.swapaxes(m, -1, -2)), jnp.asarray(np.stack(cnts).astype(np.float32))


def _prep_layer_weights(w_ada, b_ada, g_mix, w_in, conv_w, conv_b, dt_bias, a_log, d_skip, ssd_norm_w, w_ssd_out,
                        pool_w, pool_scale, w_pool_out, w_out, g_ffn, w_gate_up, w_down):
    D = w_in.shape[0]
    di = ssd_norm_w.shape[0]
    xbc = conv_w.shape[1]
    nh2 = dt_bias.size
    pw = pool_scale.shape[0]
    o = 0
    wz = w_in[:, o:o + di]; o += di
    wx = w_in[:, o:o + xbc]; o += xbc
    wdt = w_in[:, o:o + nh2]; o += nh2
    wp = w_in[:, o:o + pw]; o += pw
    wg = w_in[:, o:]
    w1 = jnp.concatenate([_perm_xbc(wx), wz, wg, wp, wdt, jnp.zeros((D, DT_PAD - nh2), w_in.dtype)], axis=1)
    pad128 = lambda v: jnp.concatenate([v.reshape(1, -1), jnp.zeros((1, 128 - v.size), F32)], axis=1)
    return dict(
        w_ada=w_ada, b_ada=b_ada.reshape(1, -1), g_mix=g_mix.reshape(1, -1), w1=w1,
        conv_w=_perm_xbc(conv_w), conv_b=_perm_xbc(conv_b.reshape(1, -1)),
        dt_bias=pad128(dt_bias), a_log=pad128(a_log),
        dskip=jnp.repeat(d_skip[0] + d_skip[1], SSD_HEADDIM).reshape(1, -1),
        ssd_norm_w=ssd_norm_w.reshape(1, -1), w_ssd_out=w_ssd_out, pool_w=pool_w,
        pool_scale=pool_scale.reshape(1, -1), w_pool_out=w_pool_out, w_out=w_out, g_ffn=g_ffn.reshape(1, -1),
        w_gate_up=w_gate_up, w_down=w_down)


def _unprep_layer_grads(g, dims):
    di, xbc, nh2, pw = dims
    d1 = g["w1"]
    o = 0
    dxbc = d1[:, o:o + xbc]; o += xbc
    dz = d1[:, o:o + di]; o += di
    dg = d1[:, o:o + 2 * pw]; o += 2 * pw
    dp = d1[:, o:o + pw]; o += pw
    ddt = d1[:, o:o + nh2]
    nh = nh2 // 2
    dsk = g["dskip"].reshape(nh, SSD_HEADDIM).sum(axis=1)
    return dict(
        w_ada=g["w_ada"], b_ada=g["b_ada"].reshape(-1), g_mix=g["g_mix"].reshape(-1),
        w_in=jnp.concatenate([dz, _unperm_xbc(dxbc), ddt, dp, dg], axis=1),
        conv_w=_unperm_xbc(g["conv_w"]), conv_b=_unperm_xbc(g["conv_b"]).reshape(-1),
        dt_bias=g["dt_bias"][0, :nh2].reshape(2, nh), a_log=g["a_log"][0, :nh2].reshape(2, nh),
        d_skip=jnp.stack([dsk, dsk]), ssd_norm_w=g["ssd_norm_w"].reshape(-1), w_ssd_out=g["w_ssd_out"],
        pool_w=g["pool_w"], pool_scale=g["pool_scale"].reshape(-1), w_pool_out=g["w_pool_out"], w_out=g["w_out"],
        g_ffn=g["g_ffn"].reshape(-1), w_gate_up=g["w_gate_up"], w_down=g["w_down"])


COND_ROWS = 16


def _split_mods(m):
    d = m.shape[1] // 6
    return [m[:2, k * d:(k + 1) * d].reshape(2, 1, d) for k in range(6)]


def _hosted(hosts, box, key):
    fn = (hosts or {}).get(key)
    return fn(box) if fn else None


def _layer_fwd(l, x, cond_s, w, rows, n_ctx, pc, hosts=None, box=None):
    T, D = x.shape
    nt, nct, tm = rows.nt, rows.nct, rows.tm
    n = lambda s: f"l{l}_{s}"
    crow = Rows(1, 0, COND_ROWS)
    mraw = matmul_nn(n("ada_mm"), cond_s, w["w_ada"])
    (m,) = stage_fwd(n("ada_bias"), f_bias, crow, [crow.row(mraw, mraw.shape[1]), crow.vec(w["b_ada"])],
                     [(mraw.shape[1], F32, False)])
    sh1, sc1, ga1, sh2, sc2, ga2 = _split_mods(m)

    (h1,) = stage_fwd(n("norm1"), f_norm_mod, rows,
                      [rows.row(x, D), rows.vec(w["g_mix"]), rows.segvec(sh1), rows.segvec(sc1)],
                      [(D, ACT_DTYPE, False)])
    proj = matmul_nn(n("in_mm"), h1, w["w1"])
    xbc_w = w["conv_w"].shape[1]
    di = w["ssd_norm_w"].shape[1]
    pw = w["pool_scale"].shape[1]
    c_z, c_g, c_p, c_dt = xbc_w, xbc_w + di, xbc_w + di + 2 * pw, xbc_w + di + 3 * pw
    xbc = conv_fwd(n("conv"), proj, w["conv_w"], w["conv_b"], n_ctx, xbc_w)
    ex = _hosted(hosts, box, "ssd")
    y2, states, xres = ssd_fwd(n("ssd"), xbc, proj, c_dt // 128, w["dt_bias"], w["a_log"], n_ctx, ex)
    if ex is not None:
        box["ssd"] = xres

    G = SSD_GROUPS
    gw = di // G
    r8 = Rows(nt, nct, tm, G)
    gate_args = [r8.row(y2, gw, 0, True), r8.row(y2, gw, 0, True, roff=nt), r8.row(xbc, gw, 0, True, stride=2),
                 r8.row(proj, gw, c_z // gw, True), r8.vec(w["dskip"], True), r8.vec(w["ssd_norm_w"], True)]
    (ynw,) = stage_fwd(n("ssd_gate"), f_ssd_gate, r8, gate_args, [(gw, ACT_DTYPE, True)])
    o_ssd = matmul_nn(n("ssd_out_mm"), ynw, w["w_ssd_out"])

    nw = len(POOL_WINDOWS)
    pg = pw // nw
    r4 = Rows(nt, nct, tm, nw)
    pmat, pmat_t, inv_cnt = pc
    cblk = lambda a: Arg(a, (None, None) + a.shape[2:], lambda j, i, s: (s, j, 0, 0), "const")
    pool_args = [r4.row(proj, pg, c_p // pg, True), cblk(pmat), cblk(pmat_t), cblk(inv_cnt),
                 Arg(w["pool_w"], (None, pg, pg), lambda j, i, s: (j, 0, 0), "acc"), r4.vec(w["pool_scale"], True)]
    (ps,) = stage_fwd(n("pool"), f_pool, r4, pool_args, [(pg, ACT_DTYPE, True)])
    o_pool = matmul_nn(n("pool_out_mm"), ps, w["w_pool_out"])

    merge_args = [rows.row(o_ssd, D), rows.row(o_pool, D), rows.row(proj, pw, c_g // pw), rows.row(proj, pw, c_g // pw + 1)]
    (mg,) = stage_fwd(n("merge"), f_merge, rows, merge_args, [(D, ACT_DTYPE, False)])
    mo = matmul_nn(n("out_mm"), mg, w["w_out"])

    rn_args = [rows.row(x, D), rows.row(mo, D), rows.segvec(ga1), rows.vec(w["g_ffn"]), rows.segvec(sh2), rows.segvec(sc2)]
    x1, h2 = stage_fwd(n("norm2"), f_resid_norm_mod, rows, rn_args, [(D, F32, False), (D, ACT_DTYPE, False)])
    ex = _hosted(hosts, box, "gate_up_mm")
    gu = matmul_nn(n("gate_up_mm"), h2, w["w_gate_up"], ex=ex)
    if ex is not None:
        gu, box["gate_up_mm"] = gu
    fh = gu.shape[1] // 2
    sw_args = [rows.row(gu, fh, 0), rows.row(gu, fh, 1)]
    (act,) = stage_fwd(n("swiglu"), f_swiglu, rows, sw_args, [(fh, ACT_DTYPE, False)])
    dn = matmul_nn(n("down_mm"), act, w["w_down"])
    res_args = [rows.row(x1, D), rows.row(dn, D), rows.segvec(ga2)]
    (x2,) = stage_fwd(n("resid2"), f_resid, rows, res_args, [(D, F32, False)])
    saved = dict(x=x, mraw=mraw, mods=(sh1, sc1, ga1, sh2, sc2, ga2), h1=h1, proj=proj, xbc=xbc, y2=y2, states=states,
                 ynw=ynw, o_ssd=o_ssd, ps=ps, o_pool=o_pool, mg=mg, mo=mo, x1=x1, h2=h2, gu=gu, act=act, dn=dn,
                 cols=(c_z, c_g, c_p, c_dt))
    return x2, saved


def f_norm_mod_keep(x, g, sh, sc):
    return f_norm_mod(x, g, sh, sc)[0], x


def _layer_bwd(l, dx2, cond_s, w, s, rows, n_ctx, pc, hosts=None, box=None):
    T, D = dx2.shape
    nt, nct, tm = rows.nt, rows.nct, rows.tm
    n = lambda t: f"l{l}_{t}_bwd"
    sh1, sc1, ga1, sh2, sc2, ga2 = s["mods"]
    c_z, c_g, c_p, c_dt = s["cols"]
    x, proj, xbc, y2, gu = s["x"], s["proj"], s["xbc"], s["y2"], s["gu"]
    g = {}

    res_args = [rows.row(s["x1"], D), rows.row(s["dn"], D), rows.segvec(ga2)]
    dx1, ddn, dga2 = stage_bwd(n("resid2"), f_resid, rows, res_args, [rows.row(dx2, D)], [F32, ACT_DTYPE])
    ex = _hosted(hosts, box, "down_dx")
    dact = matmul_nt(n("down_dx"), ddn, w["w_down"], ex=ex)
    if ex is not None:
        dact, box["down_dx"] = dact
    g["w_down"] = matmul_tn(n("down_dw"), s["act"], ddn)
    fh = gu.shape[1] // 2
    sw_args = [rows.row(gu, fh, 0), rows.row(gu, fh, 1)]
    da, db = stage_bwd(n("swiglu"), f_swiglu, rows, sw_args, [rows.row(dact, fh)], [ACT_DTYPE, ACT_DTYPE])
    dgu = jnp.concatenate([da, db], axis=1)
    dh2 = matmul_nt(n("gate_up_dx"), dgu, w["w_gate_up"])
    g["w_gate_up"] = matmul_tn(n("gate_up_dw"), s["h2"], dgu)

    rn_args = [rows.row(x, D), rows.row(s["mo"], D), rows.segvec(ga1), rows.vec(w["g_ffn"]), rows.segvec(sh2), rows.segvec(sc2)]
    dxr, dmo, dga1, g["g_ffn"], dsh2, dsc2 = stage_bwd(
        n("norm2"), f_resid_norm_mod, rows, rn_args, [rows.row(dx1, D), rows.row(dh2, D)], [F32, ACT_DTYPE])
    dmg = matmul_nt(n("out_dx"), dmo, w["w_out"])
    g["w_out"] = matmul_tn(n("out_dw"), s["mg"], dmo)

    pw = w["pool_scale"].shape[1]
    merge_args = [rows.row(s["o_ssd"], D), rows.row(s["o_pool"], D), rows.row(proj, pw, c_g // pw), rows.row(proj, pw, c_g // pw + 1)]
    do_ssd, do_pool, dgl_s, dgl_p = stage_bwd(n("merge"), f_merge, rows, merge_args, [rows.row(dmg, D)], [ACT_DTYPE] * 4)
    dps = matmul_nt(n("pool_out_dx"), do_pool, w["w_pool_out"])
    g["w_pool_out"] = matmul_tn(n("pool_out_dw"), s["ps"], do_pool)

    nw = len(POOL_WINDOWS)
    pg = pw // nw
    r4 = Rows(nt, nct, tm, nw)
    pmat, pmat_t, inv_cnt = pc
    cblk = lambda a: Arg(a, (None, None) + a.shape[2:], lambda j, i, s_: (s_, j, 0, 0), "const")
    pool_args = [r4.row(proj, pg, c_p // pg, True), cblk(pmat), cblk(pmat_t), cblk(inv_cnt),
                 Arg(w["pool_w"], (None, pg, pg), lambda j, i, s_: (j, 0, 0), "acc"), r4.vec(w["pool_scale"], True)]
    du_pool, g["pool_w"], g["pool_scale"] = stage_bwd(n("pool"), f_pool, r4, pool_args, [r4.row(dps, pg, 0, True)], [ACT_DTYPE])

    dynw = matmul_nt(n("ssd_out_dx"), do_ssd, w["w_ssd_out"])
    g["w_ssd_out"] = matmul_tn(n("ssd_out_dw"), s["ynw"], do_ssd)
    G = SSD_GROUPS
    di = w["ssd_norm_w"].shape[1]
    gw = di // G
    r8 = Rows(nt, nct, tm, G)
    gate_args = [r8.row(y2, gw, 0, True), r8.row(y2, gw, 0, True, roff=nt), r8.row(xbc, gw, 0, True, stride=2),
                 r8.row(proj, gw, c_z // gw, True), r8.vec(w["dskip"], True), r8.vec(w["ssd_norm_w"], True)]
    dy, _, dxs_skip, dz, g["dskip"], g["ssd_norm_w"] = stage_bwd(
        n("ssd_gate"), f_ssd_gate, r8, gate_args, [r8.row(dynw, gw, 0, True)], [F32, F32, F32, ACT_DTYPE])

    ex = _hosted(hosts, box, "ssd")
    dxbc2, ddt, g["dt_bias"], g["a_log"], xres = ssd_bwd(n("ssd"), xbc, proj, c_dt // 128, w["dt_bias"], w["a_log"],
                                                         s["states"], dy, n_ctx, ex)
    if ex is not None:
        box["ssd"] = xres
    xbc_w = xbc.shape[1]
    dxbc_raw, g["conv_w"], g["conv_b"] = conv_bwd(n("conv"), proj, w["conv_w"], w["conv_b"], dxbc2, dxs_skip, n_ctx, xbc_w)
    dproj = jnp.concatenate([dxbc_raw, dz, dgl_s, dgl_p, du_pool, ddt.astype(ACT_DTYPE),
                             jnp.zeros((T, DT_PAD - 128), ACT_DTYPE)], axis=1)
    ex = _hosted(hosts, box, "in_dx")
    dh1 = matmul_nt(n("in_dx"), dproj, w["w1"], ex=ex)
    if ex is not None:
        dh1, box["in_dx"] = dh1
    g["w1"] = matmul_tn(n("in_dw"), s["h1"], dproj)

    n1_args = [rows.row(x, D), rows.vec(w["g_mix"]), rows.segvec(sh1), rows.segvec(sc1)]
    dx, g["g_mix"], dsh1, dsc1 = stage_bwd(n("norm1"), f_norm_mod_keep, rows, n1_args,
                                           [rows.row(dh1, D), rows.row(dxr, D)], [F32])

    dm = jnp.concatenate([v.reshape(2, D) for v in (dsh1, dsc1, dga1, dsh2, dsc2, dga2)], axis=1)
    dm = jnp.concatenate([dm, jnp.zeros((COND_ROWS - 2, dm.shape[1]), F32)], axis=0)
    crow = Rows(1, 0, COND_ROWS)
    dmraw, g["b_ada"] = stage_bwd(n("ada_bias"), f_bias, crow, [crow.row(s["mraw"], dm.shape[1]), crow.vec(w["b_ada"])],
                                  [crow.row(dm, dm.shape[1])], [ACT_DTYPE])
    dcs = matmul_nt(n("ada_dx"), dmraw, w["w_ada"])
    g["w_ada"] = matmul_tn(n("ada_dw"), cond_s, dmraw)
    return dx, dcs, g


def local_step(x, ctx, c, c_ctx, target, layer_w_fn, n_layers, g_final, fwd_hosts=None, bwd_hosts=None):
    L, D = x.shape
    n_ctx = ctx.shape[0]
    tm = ROW_TILE
    T = L + n_ctx
    rows = Rows(T // tm, n_ctx // tm, tm)
    pc = _pool_consts(tm, n_ctx)
    xa = jnp.concatenate([ctx, x], axis=0)
    cond = jnp.concatenate([c_ctx.reshape(1, D), c.reshape(1, D), jnp.zeros((COND_ROWS - 2, D), F32)], axis=0)
    crow = Rows(1, 0, COND_ROWS)
    (cond_s,) = stage_fwd("cond_silu", f_silu, crow, [crow.row(cond, D)], [(D, ACT_DTYPE, False)])

    saved, layer_w = [], []
    for l in range(n_layers):
        layer_w.append(layer_w_fn(l))
        box = {}
        xa, s = _layer_fwd(l, xa, cond_s, layer_w[l], rows, n_ctx, pc, fwd_hosts(l, box) if fwd_hosts else None, box)
        saved.append(s)

    rl = Rows(L // tm, 0, tm)
    gf = g_final.reshape(1, D)
    tgt = rl.row(target, D)
    tgt.kind = "const"
    loss_args = [rl.row(xa, D, roff=n_ctx // tm), tgt, rl.vec(gf)]
    (loss_rows,) = stage_fwd("loss", f_loss, rl, loss_args, [(1, F32, False)])
    ones = jnp.ones((L, 1), F32)
    dx_lat, dgf = stage_bwd("loss_bwd", f_loss, rl, loss_args, [rl.row(ones, 1)], [F32])
    loss = jnp.sum(loss_rows)
    dx = jnp.concatenate([jnp.zeros((n_ctx, D), F32), dx_lat], axis=0)

    grads = [None] * n_layers
    dcs = jnp.zeros((COND_ROWS, D), F32)
    for l in reversed(range(n_layers)):
        box = {}
        hosts = bwd_hosts(l, grads, box) if bwd_hosts else None
        dx, dcs_l, grads[l] = _layer_bwd(l, dx, cond_s, layer_w[l], saved[l], rows, n_ctx, pc, hosts, box)
        dcs = dcs + dcs_l
    (dcond,) = stage_bwd("cond_silu_bwd", f_silu, crow, [crow.row(cond, D)], [crow.row(dcs, D)], [F32])
    return loss, dx[n_ctx:], grads, dcond[0], dgf


def gather_chips(halves, conv=None):
    n = len(halves)
    ops = list(halves) + ([conv] if conv is not None else [])

    def copies(ins, outs, pos):
        c, me = pos[2], _chip_index(pos)
        pairs = [(s.at[c], o.at[me, c]) for s, o in zip(ins[:n], outs[:n])]
        pairs += [(s, o.at[me]) for s, o in zip(ins[n:], outs[n:])]
        return pairs, [(s, d, _flip(pos, rel)) for rel in PLANE for s, d in pairs]

    shapes = [jax.ShapeDtypeStruct((4,) + s.shape, s.dtype) for s in ops]
    return Exchange(copies, 3 * len(ops), len(ops), ops, shapes)


def gather_pair(gathered):
    n = len(gathered)

    def copies(ins, outs, pos):
        c = pos[2]
        return [], [(s.at[b, c], o.at[b, c], _flip(pos, PAIR[0])) for s, o in zip(ins, outs) for b in range(4)]

    shapes = [jax.ShapeDtypeStruct(g.shape, g.dtype) for g in gathered]
    return Exchange(copies, 4 * n, 0, gathered, shapes, aliases={k: k for k in range(n)})


def swap_halves(grads):
    n = len(grads)

    def copies(ins, outs, pos):
        c = pos[2]
        return [], [(g.at[b, 1 - c], o.at[b], _flip(pos, PAIR[0])) for g, o in zip(ins, outs) for b in range(4)]

    shapes = [jax.ShapeDtypeStruct((g.shape[0],) + g.shape[2:], g.dtype) for g in grads]
    return Exchange(copies, 4 * n, 0, grads, shapes)


def scatter_chips(sums):
    n = len(sums)

    def copies(ins, outs, pos):
        me = _chip_index(pos)
        local = [(p.at[me], o.at[me]) for p, o in zip(ins, outs)]
        remote = []
        for rel in PLANE:
            peer = _flip(pos, rel)
            remote += [(p.at[_chip_index(peer)], o.at[me], peer) for p, o in zip(ins, outs)]
        return local, remote

    shapes = [jax.ShapeDtypeStruct(p.shape, p.dtype) for p in sums]
    return Exchange(copies, 3 * n, n, sums, shapes)


def share_halves(finals):
    n = len(finals)

    def copies(ins, outs, pos):
        c = pos[2]
        local = [(f, o.at[c]) for f, o in zip(ins, outs)]
        return local, [(f, d, _flip(pos, PAIR[0])) for f, d in local]

    shapes = [jax.ShapeDtypeStruct((2,) + f.shape, f.dtype) for f in finals]
    return Exchange(copies, n, n, finals, shapes)


def gather_everyone(vec):
    def copies(ins, outs, pos):
        me = _device_index(pos)
        (v,), (o,) = ins, outs
        return [(v, o.at[me])], [(v, o.at[me], _flip(pos, rel)) for rel in EVERYONE]

    return Exchange(copies, len(EVERYONE), 1, [vec], [jax.ShapeDtypeStruct((8,) + vec.shape, vec.dtype)])


def _row_tile(rows, cols, n_bufs):
    cap = VMEM_LIMIT_BYTES // 4 // (2 * n_bufs * cols * 4)
    for t in (1024, 512, 256, 128, 64, 32, 16, 8):
        if t <= cap and rows % t == 0:
            return t
    return rows


WIRE_DTYPE = jnp.bfloat16


def add_own_half(name, grads, recv, c):
    nb, _, R, C = grads.shape
    tr = _row_tile(R, C, 3)

    def body(c_ref, g_ref, r_ref, o_ref):
        o_ref[...] = (g_ref[...] + r_ref[...]).astype(o_ref.dtype)

    spec = pl.BlockSpec((None, tr, C), lambda b, i, c_ref: (b, i, 0))
    return _pcall(
        body, name=name, out_shape=jax.ShapeDtypeStruct(recv.shape, WIRE_DTYPE),
        grid_spec=pltpu.PrefetchScalarGridSpec(
            num_scalar_prefetch=1, grid=(nb, R // tr),
            in_specs=[pl.BlockSpec((None, None, tr, C), lambda b, i, c_ref: (b, c_ref[0], i, 0)), spec],
            out_specs=spec),
        compiler_params=_params("parallel", "parallel"),
    )(c, grads, recv)


def sum_slots(name, a):
    n, R, C = a.shape
    tr = _row_tile(R, C, n + 1)

    def body(a_ref, o_ref):
        acc = a_ref[0].astype(F32)
        for k in range(1, n):
            acc = acc + a_ref[k].astype(F32)
        o_ref[...] = acc

    return _pcall(
        body, name=name, out_shape=jax.ShapeDtypeStruct((R, C), F32), grid=(R // tr,),
        in_specs=[pl.BlockSpec((n, tr, C), lambda i: (0, i, 0))], out_specs=pl.BlockSpec((tr, C), lambda i: (i, 0)),
        compiler_params=_params("parallel"),
    )(a)


def adamw(name, w, g_layers, m, v):
    nl, R, C = w.shape
    assert len(g_layers) == nl
    tr = _row_tile(R, C, 8 + nl)
    nr = R // tr

    def body(*refs):
        w_ref, m_ref, v_ref = refs[:3]
        g_refs = refs[3:3 + nl]
        go_ref, d_ref, nm_ref, nv_ref = refs[3 + nl:]
        l = pl.program_id(0)
        gr = g_refs[0][...]
        for k in range(1, nl):
            gr = jnp.where(l == k, g_refs[k][...], gr)
        nm = ADAM_B1 * m_ref[...] + (1.0 - ADAM_B1) * gr
        nv = ADAM_B2 * v_ref[...] + (1.0 - ADAM_B2) * jnp.square(gr)
        m_hat = nm / (1.0 - ADAM_B1 ** ADAM_STEP)
        v_hat = nv / (1.0 - ADAM_B2 ** ADAM_STEP)
        d_ref[...] = -ADAM_LR * (m_hat / (jnp.sqrt(v_hat) + ADAM_EPS) + ADAM_WD * w_ref[...])
        go_ref[...] = gr
        nm_ref[...] = nm
        nv_ref[...] = nv

    spec = pl.BlockSpec((None, tr, C), lambda l, i: (l, i, 0))
    g_specs = [pl.BlockSpec((tr, C), (lambda l, i, k=k: (jnp.where(l == k, i, jnp.where(l < k, 0, nr - 1)), 0)))
               for k in range(nl)]
    return _pcall(
        body, name=name, out_shape=[jax.ShapeDtypeStruct((nl, R, C), F32)] * 4, grid=(nl, nr),
        in_specs=[spec] * 3 + g_specs, out_specs=[spec] * 4, compiler_params=_params("arbitrary", "arbitrary"),
    )(w, m, v, *g_layers)


BIG = ("w_ada", "w_in", "w_ssd_out", "pool_w", "w_pool_out", "w_out", "w_gate_up", "w_down")
COL_SHARDED = ("w_ada", "w_in", "w_gate_up")
SMALL = ("c_ctx", "b_ada", "g_mix", "conv_w", "conv_b", "dt_bias", "a_log", "d_skip", "ssd_norm_w", "pool_scale",
         "g_ffn", "g_final")
WEIGHTS = ("c_ctx", "w_ada", "b_ada", "g_mix", "w_in", "conv_w", "conv_b", "dt_bias", "a_log", "d_skip", "ssd_norm_w",
           "w_ssd_out", "pool_w", "pool_scale", "w_pool_out", "w_out", "g_ffn", "w_gate_up", "w_down", "g_final")
LAYER_KEYS = ("w_ada", "b_ada", "g_mix", "w_in", "conv_w", "conv_b", "dt_bias", "a_log", "d_skip", "ssd_norm_w",
              "w_ssd_out", "pool_w", "pool_scale", "w_pool_out", "w_out", "g_ffn", "w_gate_up", "w_down")


def _shard2d(name, a):
    if name == "pool_w":
        return a.reshape(a.shape[0], a.shape[1] * a.shape[2], a.shape[3])
    return a


def _full_from_blocks(name, a):
    nb, R, C = a.shape
    if name in COL_SHARDED:
        return jnp.transpose(a, (1, 0, 2)).reshape(R, nb * C)
    if name == "pool_w":
        nw = len(POOL_WINDOWS)
        return jnp.transpose(a.reshape(nb, nw, R // nw, C), (1, 0, 2, 3)).reshape(nw, nb * R // nw, C)
    return a.reshape(nb * R, C)


def _blocks_from_full(name, g):
    nb = 4
    if name in COL_SHARDED:
        K, N = g.shape
        return jnp.transpose(g.reshape(K, nb, N // nb), (1, 0, 2))
    if name == "pool_w":
        nw, r, C = g.shape
        return jnp.transpose(g.reshape(nw, nb, r // nb, C), (1, 0, 2, 3)).reshape(nb, nw * r // nb, C)
    return g.reshape(nb, g.shape[0] // nb, g.shape[1])


def _pack(arrs, rows):
    flat = jnp.concatenate([a.reshape(-1).astype(F32) for a in arrs])
    return jnp.concatenate([flat, jnp.zeros((rows * 128 - flat.size,), F32)]).reshape(rows, 128)


def _unpack(vec, shapes):
    flat = vec.reshape(-1)
    out, o = [], 0
    for s in shapes:
        n = int(np.prod(s))
        out.append(flat[o:o + n].reshape(s))
        o += n
    return out


def _rows_for(shapes):
    n = sum(int(np.prod(s)) for s in shapes)
    return -(-n // (8 * 128)) * 8


def kernel(x, c, ctx, c_ctx, w_ada, b_ada, g_mix, w_in, conv_w, conv_b, dt_bias, a_log, d_skip, ssd_norm_w, w_ssd_out, pool_w, pool_scale, w_pool_out, w_out, g_ffn, w_gate_up, w_down, g_final, loss_target, m_c_ctx, m_w_ada, m_b_ada, m_g_mix, m_w_in, m_conv_w, m_conv_b, m_dt_bias, m_a_log, m_d_skip, m_ssd_norm_w, m_w_ssd_out, m_pool_w, m_pool_scale, m_w_pool_out, m_w_out, m_g_ffn, m_w_gate_up, m_w_down, m_g_final, v_c_ctx, v_w_ada, v_b_ada, v_g_mix, v_w_in, v_conv_w, v_conv_b, v_dt_bias, v_a_log, v_d_skip, v_ssd_norm_w, v_w_ssd_out, v_pool_w, v_pool_scale, v_w_pool_out, v_w_out, v_g_ffn, v_w_gate_up, v_w_down, v_g_final):
    w = dict(c_ctx=c_ctx, w_ada=w_ada, b_ada=b_ada, g_mix=g_mix, w_in=w_in, conv_w=conv_w, conv_b=conv_b, dt_bias=dt_bias,
             a_log=a_log, d_skip=d_skip, ssd_norm_w=ssd_norm_w, w_ssd_out=w_ssd_out, pool_w=pool_w, pool_scale=pool_scale,
             w_pool_out=w_pool_out, w_out=w_out, g_ffn=g_ffn, w_gate_up=w_gate_up, w_down=w_down, g_final=g_final)
    m = dict(c_ctx=m_c_ctx, w_ada=m_w_ada, b_ada=m_b_ada, g_mix=m_g_mix, w_in=m_w_in, conv_w=m_conv_w, conv_b=m_conv_b,
             dt_bias=m_dt_bias, a_log=m_a_log, d_skip=m_d_skip, ssd_norm_w=m_ssd_norm_w, w_ssd_out=m_w_ssd_out,
             pool_w=m_pool_w, pool_scale=m_pool_scale, w_pool_out=m_w_pool_out, w_out=m_w_out, g_ffn=m_g_ffn,
             w_gate_up=m_w_gate_up, w_down=m_w_down, g_final=m_g_final)
    v = dict(c_ctx=v_c_ctx, w_ada=v_w_ada, b_ada=v_b_ada, g_mix=v_g_mix, w_in=v_w_in, conv_w=v_conv_w, conv_b=v_conv_b,
             dt_bias=v_dt_bias, a_log=v_a_log, d_skip=v_d_skip, ssd_norm_w=v_ssd_norm_w, w_ssd_out=v_w_ssd_out,
             pool_w=v_pool_w, pool_scale=v_pool_scale, w_pool_out=v_w_pool_out, w_out=v_w_out, g_ffn=v_g_ffn,
             w_gate_up=v_w_gate_up, w_down=v_w_down, g_final=v_g_final)
    assert x.shape[0] == 1, "one example per device"
    pos = _position()
    core = pos[2].astype(jnp.int32).reshape(1)
    n_layers = w_in.shape[0]
    assert n_layers == 2
    dims = (ssd_norm_w.shape[1], conv_w.shape[2] * 4, dt_bias[0].size, pool_scale.shape[1])
    shard = {k: _shard2d(k, w[k]) for k in BIG}

    def halves(a):
        return a.reshape(a.shape[:-2] + (2, a.shape[-2] // 2, a.shape[-1]))

    def whole(a):
        return a.reshape(a.shape[:-3] + (2 * a.shape[-2], a.shape[-1]))

    def wire_shards(l):
        return [halves(shard[k][l].astype(MXU_DTYPE)) for k in BIG]

    first = comm_call("gather0_chips", gather_chips(wire_shards(0), conv=conv_w))
    got = {0: comm_call("gather0_pair", gather_pair(first[:-1]))}
    conv_all = first[-1]
    conv_full = [jnp.transpose(conv_all[:, l], (1, 0, 2)).reshape(conv_all.shape[2], -1) for l in range(n_layers)]

    boxes = {}

    def layer_w_fn(l):
        if l == 1:
            got[1] = boxes[("fwd", 0)]["gate_up_mm"]
        full = {k: _full_from_blocks(k, whole(a)) for k, a in zip(BIG, got[l])}
        full["conv_w"] = conv_full[l]
        return _prep_layer_weights(*[full[k] if k in full else w[k][l] for k in LAYER_KEYS])

    def fwd_hosts(l, box):
        boxes[("fwd", l)] = box
        if l != 0:
            return None
        return {"ssd": lambda box: gather_chips(wire_shards(1)), "gate_up_mm": lambda box: gather_pair(box["ssd"])}

    def blocks(gl):
        return [halves(_blocks_from_full(k, gl[k])) for k in BIG]

    def reduce_now(tag, gl):
        G = blocks(gl)
        recv = comm_call(f"swap{tag}", swap_halves(G))
        pair = [add_own_half(f"pair_sum{tag}_{k}", g, r, core) for k, g, r in zip(BIG, G, recv)]
        parts = comm_call(f"scatter{tag}", scatter_chips(pair))
        fin = [sum_slots(f"chip_sum{tag}_{k}", p) for k, p in zip(BIG, parts)]
        return [whole(a) for a in comm_call(f"share{tag}", share_halves(fin))]

    small_layers = {}

    def bwd_hosts(l, grads, box):
        boxes[("bwd", l)] = box
        if l != 0:
            return None
        gl1 = _unprep_layer_grads(grads[1], dims)
        small_layers[1] = gl1
        G = blocks(gl1)

        def scatter(box):
            pair = [add_own_half(f"pair_sum1_{k}", g, r, core) for k, g, r in zip(BIG, G, box["down_dx"])]
            return scatter_chips(pair)

        def share(box):
            return share_halves([sum_slots(f"chip_sum1_{k}", p) for k, p in zip(BIG, box["ssd"])])

        return {"down_dx": lambda box: swap_halves(G), "ssd": scatter, "in_dx": share}

    loss, grad_x, grads, d_c_ctx, d_g_final = local_step(
        x[0], ctx[0], c[0], c_ctx, loss_target[0], layer_w_fn, n_layers, g_final, fwd_hosts, bwd_hosts)
    loss = lax.psum(loss, ("x", "y", "c"))
    reduced1 = [whole(a) for a in boxes[("bwd", 0)]["in_dx"]]
    gl0 = _unprep_layer_grads(grads[0], dims)
    small_layers[0] = gl0
    reduced0 = reduce_now("0", gl0)

    small_full = dict(c_ctx=d_c_ctx, g_final=d_g_final.reshape(-1))
    for k in SMALL:
        if k not in small_full:
            small_full[k] = jnp.stack([small_layers[l][k] for l in range(n_layers)])
    shapes = [small_full[k].shape for k in SMALL]
    packed = _pack([small_full[k] for k in SMALL], _rows_for(shapes))
    total = sum_slots("small_sum", comm_call("gather_small", gather_everyone(packed))[0])
    small_g = dict(zip(SMALL, _unpack(total, shapes)))
    cw = conv_w.shape[2]
    small_g["conv_w"] = lax.dynamic_slice_in_dim(small_g["conv_w"], _chip_index(pos) * cw, cw, axis=2)

    grad, delta, new_m, new_v = {}, {}, {}, {}
    for k, g0, g1 in zip(BIG, reduced0, reduced1):
        shp = w[k].shape
        flat = lambda a: _shard2d(k, a)
        outs = adamw(f"adamw_{k}", flat(w[k]), [g0, g1], flat(m[k]), flat(v[k]))
        grad[k], delta[k], new_m[k], new_v[k] = [a.reshape(shp) for a in outs]
    sshapes = [w[k].shape for k in SMALL]
    srows = _rows_for(sshapes)
    pk = lambda d: _pack([d[k] for k in SMALL], srows)[None]
    _, d_, m_, v_ = adamw("adamw_small", pk(w), [pk(small_g)[0]], pk(m), pk(v))
    for k, dd, mm, vv in zip(SMALL, _unpack(d_, sshapes), _unpack(m_, sshapes), _unpack(v_, sshapes)):
        grad[k], delta[k], new_m[k], new_v[k] = small_g[k], dd, mm, vv

    return (loss, grad_x[None], *[grad[k] for k in WEIGHTS], *[delta[k] for k in WEIGHTS],
            *[new_m[k] for k in WEIGHTS], *[new_v[k] for k in WEIGHTS])
```

```python
import functools

import jax
import jax.numpy as jnp
import numpy as np
from jax import lax
from jax.experimental import pallas as pl
from jax.experimental.pallas import tpu as pltpu

F32 = jnp.float32
MXU_DTYPE = jnp.bfloat16
ACT_DTYPE = jnp.bfloat16
VMEM_LIMIT_BYTES = 48 * 1024 * 1024
EPS = 1e-6
NEG = -1e30

SSD_HEADDIM = 64
SSD_GROUPS = 8
SSD_STATE = 128
SSD_CHUNK = 128
SSD_GROUPS_PER_STEP = 8
SSD_CONV = 5
GRID_W = 64
POOL_WINDOWS = (2, 4, 8, 16)
ROW_TILE = 256
DT_PAD = 512

ADAM_LR = 0.001
ADAM_B1 = 0.9
ADAM_B2 = 0.999
ADAM_EPS = 1e-08
ADAM_WD = 0.01
ADAM_STEP = 10

MESH = pl.DeviceIdType.MESH


def _pcall(body, **kw):
    return pl.pallas_call(body, **kw)


def _params(*sem):
    return pltpu.CompilerParams(dimension_semantics=tuple(sem), vmem_limit_bytes=VMEM_LIMIT_BYTES)


def _pick_tile(n, cands):
    for t in cands:
        if n % t == 0:
            return t
    return n


PLANE = ((1, 0, 0), (0, 1, 0), (1, 1, 0))
PAIR = ((0, 0, 1),)
EVERYONE = tuple((a, b, d) for a in (0, 1) for b in (0, 1) for d in (0, 1) if a + b + d)
HBM = pl.BlockSpec(memory_space=pl.ANY)


def _position():
    return lax.axis_index("x"), lax.axis_index("y"), lax.axis_index("c")


def _flip(pos, rel):
    return tuple(1 - p if r else p for p, r in zip(pos, rel))


def _chip_index(pos):
    return 2 * pos[0] + pos[1]


def _device_index(pos):
    return 4 * pos[0] + 2 * pos[1] + pos[2]


class Exchange:
    def __init__(self, copies, n_remote, n_local, operands, out_shapes, aliases=None):
        self.copies, self.n_remote, self.n_local = copies, n_remote, n_local
        self.operands, self.out_shapes, self.aliases = list(operands), list(out_shapes), dict(aliases or {})

    def scratch(self):
        return [pltpu.SemaphoreType.DMA((max(self.n_remote, 1),)), pltpu.SemaphoreType.DMA((max(self.n_remote, 1),)),
                pltpu.SemaphoreType.DMA((max(self.n_local, 1),))]

    def descriptors(self, ins, outs, sems):
        send_sems, recv_sems, local_sems = sems
        local, remote = self.copies(ins, outs, _position())
        assert len(local) == self.n_local and len(remote) == self.n_remote
        cps = [pltpu.make_async_copy(src, dst, local_sems.at[k]) for k, (src, dst) in enumerate(local)]
        cps += [pltpu.make_async_remote_copy(src_ref=src, dst_ref=dst, send_sem=send_sems.at[k], recv_sem=recv_sems.at[k],
                                             device_id=peer, device_id_type=MESH) for k, (src, dst, peer) in enumerate(remote)]
        return cps


def comm_call(name, ex):
    n_in, n_out = len(ex.operands), len(ex.out_shapes)

    def body(*refs):
        cps = ex.descriptors(refs[:n_in], refs[n_in:n_in + n_out], refs[n_in + n_out:])
        for cp in cps:
            cp.start()
        for cp in cps:
            cp.wait()

    return _pcall(
        body, name=name, out_shape=ex.out_shapes, in_specs=[HBM] * n_in, out_specs=[HBM] * n_out,
        scratch_shapes=ex.scratch(), input_output_aliases=ex.aliases,
        compiler_params=pltpu.CompilerParams(has_side_effects=True),
    )(*ex.operands)


def hosted_call(body, ex, operands, *, name, out_shape, grid, in_specs, out_specs, scratch_shapes=()):
    n_in, n_out, n_scr = len(operands), len(out_shape), len(scratch_shapes)
    sem = ("arbitrary",) * len(grid)
    if ex is None:
        res = _pcall(body, name=name, out_shape=list(out_shape), grid=grid, in_specs=list(in_specs),
                     out_specs=list(out_specs), scratch_shapes=list(scratch_shapes), compiler_params=_params(*sem))(*operands)
        return res, []
    x_in, x_out = len(ex.operands), len(ex.out_shapes)

    def wrapped(*refs):
        o = 0
        ins = refs[o:o + n_in]; o += n_in
        xins = refs[o:o + x_in]; o += x_in
        outs = refs[o:o + n_out]; o += n_out
        xouts = refs[o:o + x_out]; o += x_out
        scr = refs[o:o + n_scr]; o += n_scr
        sems = refs[o:]
        first = last = None
        for a, n in enumerate(grid):
            i = pl.program_id(a)
            first = (i == 0) if first is None else first & (i == 0)
            last = (i == n - 1) if last is None else last & (i == n - 1)

        @pl.when(first)
        def _():
            for cp in ex.descriptors(xins, xouts, sems):
                cp.start()

        body(*ins, *outs, *scr)

        @pl.when(last)
        def _():
            for cp in ex.descriptors(xins, xouts, sems):
                cp.wait()

    aliases = {n_in + k: n_out + v for k, v in ex.aliases.items()}
    res = _pcall(
        wrapped, name=name, out_shape=list(out_shape) + ex.out_shapes, grid=grid,
        in_specs=list(in_specs) + [HBM] * x_in, out_specs=list(out_specs) + [HBM] * x_out,
        scratch_shapes=list(scratch_shapes) + ex.scratch(), input_output_aliases=aliases,
        compiler_params=pltpu.CompilerParams(dimension_semantics=sem, vmem_limit_bytes=VMEM_LIMIT_BYTES,
                                             has_side_effects=True),
    )(*operands, *ex.operands)
    return res[:n_out], res[n_out:]


def _dot(a, b, dims):
    return lax.dot_general(a.astype(MXU_DTYPE), b.astype(MXU_DTYPE), (dims, ((), ())), preferred_element_type=F32)


_NN = ((1,), (0,))
_NT = ((1,), (1,))
_TN = ((0,), (0,))


@jax.custom_vjp
def _mm(a, b):
    return _dot(a, b, _NN)


def _mm_fwd(a, b):
    return _mm(a, b), (a, b)


def _mm_bwd(res, g):
    a, b = res
    return _dot(g, b, _NT).astype(a.dtype), _dot(a, g, _TN).astype(b.dtype)


_mm.defvjp(_mm_fwd, _mm_bwd)


@jax.custom_vjp
def _mm_nt(a, b):
    return _dot(a, b, _NT)


def _mm_nt_fwd(a, b):
    return _mm_nt(a, b), (a, b)


def _mm_nt_bwd(res, g):
    a, b = res
    return _dot(g, b, _NN).astype(a.dtype), _dot(g, a, _TN).astype(b.dtype)


_mm_nt.defvjp(_mm_nt_fwd, _mm_nt_bwd)


@jax.custom_vjp
def _mm_tn(a, b):
    return _dot(a, b, _TN)


def _mm_tn_fwd(a, b):
    return _mm_tn(a, b), (a, b)


def _mm_tn_bwd(res, g):
    a, b = res
    return _dot(b, g, _NT).astype(a.dtype), _dot(a, g, _NN).astype(b.dtype)


_mm_tn.defvjp(_mm_tn_fwd, _mm_tn_bwd)


def _dot_exact(m01, v):
    m = m01.astype(jnp.bfloat16)
    hi = v.astype(jnp.bfloat16)
    r1 = v - hi.astype(F32)
    mid = r1.astype(jnp.bfloat16)
    lo = (r1 - mid.astype(F32)).astype(jnp.bfloat16)
    out = jnp.dot(m, hi, preferred_element_type=F32)
    out = out + jnp.dot(m, mid, preferred_element_type=F32)
    return out + jnp.dot(m, lo, preferred_element_type=F32)


@jax.custom_vjp
def _lin01(m, mt, v):
    return _dot_exact(m, v)


def _lin01_fwd(m, mt, v):
    return _dot_exact(m, v), (m, mt)


def _lin01_bwd(res, g):
    m, mt = res
    return jnp.zeros_like(m), jnp.zeros_like(mt), _dot_exact(mt, g)


_lin01.defvjp(_lin01_fwd, _lin01_bwd)


MATMUL_VMEM_BUDGET = VMEM_LIMIT_BYTES * 3 // 4


def _mm_tiles(m, n, k_bytes_a, k_bytes_b, out_bytes, cands_m, cands_n):
    best = None
    for tm in cands_m:
        if m % tm:
            continue
        for tn in cands_n:
            if n % tn:
                continue
            need = 2 * (tm * k_bytes_a + tn * k_bytes_b + tm * tn * out_bytes)
            if need <= MATMUL_VMEM_BUDGET and (best is None or tm * tn > best[0] * best[1]):
                best = (tm, tn)
    assert best is not None, (m, n)
    return best


_ROW_CANDS = (1088, 768, 544, 512, 272, 256, 128, 16)
_COL_CANDS = (2816, 2048, 1408, 1024, 512, 256, 128)


def _one(res, xres, ex):
    return res[0] if ex is None else (res[0], xres)


def matmul_nn(name, a, b, out_dtype=F32, ex=None):
    M, K = a.shape
    N = b.shape[1]
    tm, tn = _mm_tiles(M, N, K * a.dtype.itemsize, K * b.dtype.itemsize, jnp.dtype(out_dtype).itemsize,
                       _ROW_CANDS, (512, 256, 128))

    def body(a_ref, b_ref, o_ref):
        o_ref[...] = _dot(a_ref[...], b_ref[...], _NN).astype(o_ref.dtype)

    res, xres = hosted_call(
        body, ex, [a, b], name=name, out_shape=[jax.ShapeDtypeStruct((M, N), out_dtype)], grid=(N // tn, M // tm),
        in_specs=[pl.BlockSpec((tm, K), lambda j, i: (i, 0)), pl.BlockSpec((K, tn), lambda j, i: (0, j))],
        out_specs=[pl.BlockSpec((tm, tn), lambda j, i: (i, j))])
    return _one(res, xres, ex)


def matmul_nt(name, g, b, out_dtype=F32, ex=None):
    M, N = g.shape
    K = b.shape[0]
    tm, tk = _mm_tiles(M, K, N * g.dtype.itemsize, N * b.dtype.itemsize, jnp.dtype(out_dtype).itemsize,
                       _ROW_CANDS, _COL_CANDS)

    def body(g_ref, b_ref, o_ref):
        o_ref[...] = _dot(g_ref[...], b_ref[...], _NT).astype(o_ref.dtype)

    res, xres = hosted_call(
        body, ex, [g, b], name=name, out_shape=[jax.ShapeDtypeStruct((M, K), out_dtype)], grid=(K // tk, M // tm),
        in_specs=[pl.BlockSpec((tm, N), lambda j, i: (i, 0)), pl.BlockSpec((tk, N), lambda j, i: (j, 0))],
        out_specs=[pl.BlockSpec((tm, tk), lambda j, i: (i, j))])
    return _one(res, xres, ex)


def matmul_tn(name, a, g, ex=None):
    M, K = a.shape
    N = g.shape[1]
    tk, tn = _mm_tiles(K, N, M * a.dtype.itemsize, M * g.dtype.itemsize, 4, (512, 256, 128), (512, 256, 128))

    def body(a_ref, g_ref, o_ref):
        o_ref[...] = _dot(a_ref[...], g_ref[...], _TN)

    res, xres = hosted_call(
        body, ex, [a, g], name=name, out_shape=[jax.ShapeDtypeStruct((K, N), F32)], grid=(K // tk, N // tn),
        in_specs=[pl.BlockSpec((M, tk), lambda i, j: (0, i)), pl.BlockSpec((M, tn), lambda i, j: (0, j))],
        out_specs=[pl.BlockSpec((tk, tn), lambda i, j: (i, j))])
    return _one(res, xres, ex)


class Arg:
    def __init__(self, arr, block, imap, kind):
        self.arr, self.block, self.imap, self.kind = arr, block, imap, kind


class Rows:
    def __init__(self, nt, nct, tm, ncol=1):
        self.nt, self.nct, self.tm, self.ncol = nt, nct, tm, ncol

    def seg(self, i):
        return jnp.where(i >= self.nct, 1, 0)

    def spec(self, block, imap):
        return pl.BlockSpec(block, lambda j, i: imap(j, i, self.seg(i)))

    def row(self, arr, width, cb0=0, follow=False, roff=0, stride=1):
        f = stride if follow else 0
        return Arg(arr, (self.tm, width), lambda j, i, s: (i + roff, cb0 + f * j), "row")

    def vec(self, arr, follow=False, kind="acc"):
        w = arr.shape[1] // (self.ncol if follow else 1)
        f = 1 if follow else 0
        return Arg(arr, (1, w), lambda j, i, s: (0, f * j), kind)

    def segvec(self, arr, kind="seg"):
        return Arg(arr, (None, 1, arr.shape[2]), lambda j, i, s: (s, 0, 0), kind)


def _load(ref):
    return ref[...].astype(F32) if ref.dtype != F32 else ref[...]


def stage_fwd(name, f, rows, args, outs):
    n_in = len(args)

    def body(*refs):
        vals = [_load(r) for r in refs[:n_in]]
        res = f(*vals)
        for r, v in zip(refs[n_in:], res):
            r[...] = v.astype(r.dtype)

    T = rows.nt * rows.tm
    out_shape = [jax.ShapeDtypeStruct((T, w * (rows.ncol if fo else 1)), dt) for w, dt, fo in outs]
    out_specs = [pl.BlockSpec((rows.tm, w), (lambda j, i, fo=fo: (i, j if fo else 0))) for w, dt, fo in outs]
    res = _pcall(
        body, name=name, out_shape=out_shape, grid=(rows.ncol, rows.nt),
        in_specs=[rows.spec(a.block, a.imap) for a in args], out_specs=out_specs,
        compiler_params=_params("parallel", "parallel"),
    )(*[a.arr for a in args])
    return res


def stage_bwd(name, f, rows, args, cots, row_dtypes):
    n_in, n_ct = len(args), len(cots)
    diff = [k for k, a in enumerate(args) if a.kind != "const"]
    row_dt = {}
    for k in diff:
        if args[k].kind == "row":
            row_dt[k] = row_dtypes[len(row_dt)]

    def body(*refs):
        i = pl.program_id(1)
        vals = [_load(r) for r in refs[:n_in]]
        cts = tuple(_load(r) for r in refs[n_in:n_in + n_ct])
        outs = refs[n_in + n_ct:]

        def g(*dv):
            full = list(vals)
            for k, v in zip(diff, dv):
                full[k] = v
            return tuple(f(*full))

        _, vjp = jax.vjp(g, *[vals[k] for k in diff])
        grads = vjp(cts)
        for k, o, gr in zip(diff, outs, grads):
            kind = args[k].kind
            if kind == "row":
                o[...] = gr.astype(o.dtype)
            else:
                first = (i == 0) | (i == rows.nct) if kind == "seg" else (i == 0)

                @pl.when(first)
                def _():
                    o[...] = gr.astype(o.dtype)

                @pl.when(jnp.logical_not(first))
                def _():
                    o[...] += gr.astype(o.dtype)

    T = rows.nt * rows.tm
    out_shape, out_specs = [], []
    for k in diff:
        a = args[k]
        if a.kind == "row":
            out_shape.append(jax.ShapeDtypeStruct((T, a.block[1] * (rows.ncol if _follows(a) else 1)), row_dt[k]))
            fo = _follows(a)
            out_specs.append(pl.BlockSpec(a.block, (lambda j, i, fo=fo: (i, j if fo else 0))))
        else:
            out_shape.append(jax.ShapeDtypeStruct(a.arr.shape, F32))
            out_specs.append(rows.spec(a.block, a.imap))
    return _pcall(
        body, name=name, out_shape=out_shape, grid=(rows.ncol, rows.nt),
        in_specs=[rows.spec(a.block, a.imap) for a in list(args) + list(cots)], out_specs=out_specs,
        compiler_params=_params("arbitrary", "arbitrary"),
    )(*[a.arr for a in list(args) + list(cots)])


def _follows(a):
    return a.imap(1, 0, 0)[-1] != a.imap(0, 0, 0)[-1]


def _rms(x):
    return x * lax.rsqrt(jnp.mean(x * x, axis=-1, keepdims=True) + EPS)


def f_norm_mod(x, g, sh, sc):
    return ((_rms(x) * g) * (1.0 + sc) + sh,)


def f_resid_norm_mod(x, mo, ga, g, sh, sc):
    x1 = x + ga * mo
    return x1, (_rms(x1) * g) * (1.0 + sc) + sh


def f_resid(x, dn, ga):
    return (x + ga * dn,)


def f_silu(x):
    return (x * jax.nn.sigmoid(x),)


def f_bias(x, b):
    return (x + b,)


def f_ssd_gate(y0, y1, xs, z, dskip, nw):
    y = y0 + y1 + dskip * xs
    return (_rms(y * (z * jax.nn.sigmoid(z))) * nw,)


def f_pool(u, pmat, pmat_t, inv_cnt, pw, scale):
    pm = _lin01(pmat, pmat_t, u) * inv_cnt - u
    return (_mm(pm, pw) * scale,)


def f_merge(o_ssd, o_pool, gl_ssd, gl_pool):
    return (jax.nn.sigmoid(gl_ssd) * o_ssd + jax.nn.sigmoid(gl_pool) * o_pool,)


def f_swiglu(a, b):
    return ((a * jax.nn.sigmoid(a)) * b,)


def f_loss(x, tgt, g):
    err = _rms(x) * g - tgt
    return (0.5 * jnp.mean(err * err, axis=-1, keepdims=True),)


CONV_TILE = 128


def _shift_rows(v, j, n_ctx):
    if j == 0:
        return v
    T = v.shape[0]
    r = lax.broadcasted_iota(jnp.int32, v.shape, 0)
    lo = jnp.where(r >= n_ctx, n_ctx, 0)
    hi = jnp.where(r >= n_ctx, T, n_ctx)
    ok = (r + j >= lo) & (r + j < hi)
    return jnp.where(ok, pltpu.roll(v, (-j) % T, 0), 0.0)


def conv_fwd(name, proj, conv_w, conv_b, n_ctx, width):
    T = proj.shape[0]
    half = SSD_CONV // 2

    def body(u_ref, w_ref, b_ref, o_ref):
        u = u_ref[...]
        pre = jnp.broadcast_to(b_ref[...], u.shape)
        for k in range(SSD_CONV):
            pre = pre + w_ref[k:k + 1, :] * _shift_rows(u, k - half, n_ctx)
        o_ref[...] = pre * jax.nn.sigmoid(pre)

    col = lambda t: (0, t)
    return _pcall(
        body, name=name, out_shape=jax.ShapeDtypeStruct((T, width), F32), grid=(width // CONV_TILE,),
        in_specs=[pl.BlockSpec((T, CONV_TILE), col), pl.BlockSpec((SSD_CONV, CONV_TILE), col),
                  pl.BlockSpec((1, CONV_TILE), col)],
        out_specs=pl.BlockSpec((T, CONV_TILE), col), compiler_params=_params("parallel"),
    )(proj, conv_w, conv_b)


def conv_bwd(name, proj, conv_w, conv_b, d_act2, d_skip, n_ctx, width):
    T = proj.shape[0]
    half = SSD_CONV // 2

    def body(u_ref, w_ref, b_ref, c0_ref, c1_ref, cs_ref, du_ref, dw_ref, db_ref):
        t = pl.program_id(0)
        u = u_ref[...]
        pre = jnp.broadcast_to(b_ref[...], u.shape)
        for k in range(SSD_CONV):
            pre = pre + w_ref[k:k + 1, :] * _shift_rows(u, k - half, n_ctx)
        sg = jax.nn.sigmoid(pre)
        ct = c0_ref[...] + c1_ref[...] + jnp.where(t % 4 < 2, cs_ref[...], 0.0)
        dpre = ct * (sg * (1.0 + pre * (1.0 - sg)))
        du = jnp.zeros_like(u)
        for k in range(SSD_CONV):
            du = du + w_ref[k:k + 1, :] * _shift_rows(dpre, half - k, n_ctx)
            dw_ref[k:k + 1, :] = jnp.sum(dpre * _shift_rows(u, k - half, n_ctx), axis=0, keepdims=True)
        du_ref[...] = du.astype(du_ref.dtype)
        db_ref[...] = jnp.sum(dpre, axis=0, keepdims=True)

    col = lambda t: (0, t)
    skip_col = lambda t: (0, (t // 4) * 2 + jnp.minimum(t % 4, 1))
    return _pcall(
        body, name=name,
        out_shape=[jax.ShapeDtypeStruct((T, width), ACT_DTYPE), jax.ShapeDtypeStruct((SSD_CONV, width), F32),
                   jax.ShapeDtypeStruct((1, width), F32)],
        grid=(width // CONV_TILE,),
        in_specs=[pl.BlockSpec((T, CONV_TILE), col), pl.BlockSpec((SSD_CONV, CONV_TILE), col),
                  pl.BlockSpec((1, CONV_TILE), col), pl.BlockSpec((T, CONV_TILE), col),
                  pl.BlockSpec((T, CONV_TILE), lambda t: (1, t)), pl.BlockSpec((T, CONV_TILE), skip_col)],
        out_specs=[pl.BlockSpec((T, CONV_TILE), col), pl.BlockSpec((SSD_CONV, CONV_TILE), col),
                   pl.BlockSpec((1, CONV_TILE), col)],
        compiler_params=_params("parallel"),
    )(proj, conv_w, conv_b, d_act2, d_act2, d_skip)


@jax.custom_vjp
def _cumsum_mat(tri, tri_t, a):
    return jnp.dot(tri, a, precision=lax.Precision.HIGHEST, preferred_element_type=F32)


def _cumsum_fwd(tri, tri_t, a):
    return _cumsum_mat(tri, tri_t, a), (tri, tri_t)


def _cumsum_bwd(res, g):
    tri, tri_t = res
    return (jnp.zeros_like(tri), jnp.zeros_like(tri_t),
            jnp.dot(tri_t, g, precision=lax.Precision.HIGHEST, preferred_element_type=F32))


_cumsum_mat.defvjp(_cumsum_fwd, _cumsum_bwd)


def _ssd_dt(dtraw, dt_bias, a_log, tri, tri_t):
    dt_all = jax.nn.softplus(dtraw + dt_bias)
    a_all = dt_all * (-jnp.exp(a_log))
    return dt_all, a_all, _cumsum_mat(tri, tri_t, a_all)


def _ssd_chunk(xs, bm, cm, dt_all, a_all, s_all, s_in, mask, idx0):
    Q = xs.shape[0]
    hpg = xs.shape[1] // SSD_HEADDIM
    lane = lax.broadcasted_iota(jnp.int32, dt_all.shape, 1)
    head = lax.broadcasted_iota(jnp.int32, xs.shape, 1) // SSD_HEADDIM
    head1 = lax.broadcasted_iota(jnp.int32, (1, xs.shape[1]), 1) // SSD_HEADDIM

    def pick(v, r):
        return jnp.sum(jnp.where(lane == idx0 + r, v, 0.0), axis=1, keepdims=True)

    def expand(cols, hd):
        out = cols[hpg - 1]
        for r in range(hpg - 2, -1, -1):
            out = jnp.where(hd == r, cols[r], out)
        return out

    dt_r = [pick(dt_all, r) for r in range(hpg)]
    s_r = [pick(s_all, r) for r in range(hpg)]
    stot_r = [jnp.sum(jnp.where(lane == idx0 + r, a_all, 0.0), keepdims=True).reshape(1, 1) for r in range(hpg)]

    xd = xs * expand([jnp.broadcast_to(c, xs.shape) for c in dt_r], head)
    cb = _mm_nt(cm, bm)
    y = expand([jnp.broadcast_to(jnp.exp(c), xs.shape) for c in s_r], head) * _mm(cm, s_in)
    for r in range(hpg):
        sm = jnp.broadcast_to(s_r[r], (Q, Q))
        decay = jnp.exp(jnp.where(mask, sm - sm.T, NEG))
        y = y + _mm(cb * decay, jnp.where(head == r, xd, 0.0))
    to_end = expand([jnp.broadcast_to(jnp.exp(t - c), xs.shape) for t, c in zip(stot_r, s_r)], head)
    carry = expand([jnp.broadcast_to(jnp.exp(t), (1, xs.shape[1])) for t in stot_r], head1)
    s_out = carry * s_in + _mm_tn(bm, xd * to_end)
    return y, s_out


def _scan_consts():
    q = SSD_CHUNK
    i = np.arange(q)[:, None]
    j = np.arange(q)[None, :]
    fwd = (j <= i).astype(np.float32)
    bwd = (j >= i).astype(np.float32)
    tri = np.stack([fwd, bwd])
    return jnp.asarray(tri), jnp.asarray(np.stack([fwd.T, bwd.T]))


def _chunk_of(d, k, ncc, nc):
    rev = jnp.where(k < ncc, ncc - 1 - k, nc - 1 + ncc - k)
    return jnp.where(d == 0, k, rev)


def ssd_fwd(name, xbc, proj, dt_cb, dt_bias, a_log, n_ctx, ex=None):
    T = xbc.shape[0]
    q, G = SSD_CHUNK, SSD_GROUPS
    nc, ncc = T // q, n_ctx // q
    gw = xbc.shape[1] // G
    xw = gw - 2 * SSD_STATE
    hpg = xw // SSD_HEADDIM
    nh = G * hpg
    tri, tri_t = _scan_consts()

    gs = SSD_GROUPS_PER_STEP

    def body(x_ref, dt_ref, bias_ref, alog_ref, tri_ref, trit_ref, y_ref, sin_ref, state):
        d, gb, k = pl.program_id(0), pl.program_id(1), pl.program_id(2)

        @pl.when(k == 0)
        def _():
            state[...] = jnp.zeros_like(state)

        tri_v = tri_ref[...]
        dt_all, a_all, s_all = _ssd_dt(dt_ref[...], bias_ref[...], alog_ref[...], tri_v, trit_ref[...])
        for j in range(gs):
            o = j * gw
            s_in = state[j]
            sin_ref[j] = s_in
            y, s_out = _ssd_chunk(
                x_ref[:, o:o + xw], x_ref[:, o + xw:o + xw + SSD_STATE], x_ref[:, o + xw + SSD_STATE:o + gw],
                dt_all, a_all, s_all, s_in, tri_v > 0.5, d * nh + (gb * gs + j) * hpg)
            y_ref[:, j * xw:(j + 1) * xw] = y
            state[j] = s_out

    ch = lambda d, g, k: _chunk_of(d, k, ncc, nc)
    res, xres = hosted_call(
        body, ex, [xbc, proj, dt_bias, a_log, tri, tri_t], name=name,
        out_shape=[jax.ShapeDtypeStruct((2 * T, G * xw), F32),
                   jax.ShapeDtypeStruct((2, nc, G, SSD_STATE, xw), F32)],
        grid=(2, G // gs, nc),
        in_specs=[pl.BlockSpec((q, gs * gw), lambda d, g, k: (ch(d, g, k), g)),
                  pl.BlockSpec((q, 128), lambda d, g, k: (ch(d, g, k), dt_cb)),
                  pl.BlockSpec((1, 128), lambda d, g, k: (0, 0)),
                  pl.BlockSpec((1, 128), lambda d, g, k: (0, 0)),
                  pl.BlockSpec((None, q, q), lambda d, g, k: (d, 0, 0)),
                  pl.BlockSpec((None, q, q), lambda d, g, k: (d, 0, 0))],
        out_specs=[pl.BlockSpec((q, gs * xw), lambda d, g, k: (d * nc + ch(d, g, k), g)),
                   pl.BlockSpec((None, None, gs, SSD_STATE, xw), lambda d, g, k: (d, k, g, 0, 0))],
        scratch_shapes=[pltpu.VMEM((gs, SSD_STATE, xw), F32)])
    return res[0], res[1], xres


def ssd_bwd(name, xbc, proj, dt_cb, dt_bias, a_log, states, dy, n_ctx, ex=None):
    T = xbc.shape[0]
    q, G = SSD_CHUNK, SSD_GROUPS
    nc, ncc = T // q, n_ctx // q
    gw = xbc.shape[1] // G
    xw = gw - 2 * SSD_STATE
    hpg = xw // SSD_HEADDIM
    nh = G * hpg
    tri, tri_t = _scan_consts()

    gs = SSD_GROUPS_PER_STEP

    def body(x_ref, dt_ref, bias_ref, alog_ref, tri_ref, trit_ref, sin_ref, dy_ref,
             dx_ref, ddt_ref, dbias_ref, dalog_ref, dstate):
        d, gb, k = pl.program_id(0), pl.program_id(1), pl.program_id(2)
        first = (d == 0) & (gb == 0) & (k == 0)

        @pl.when(first)
        def _():
            ddt_ref[...] = jnp.zeros_like(ddt_ref)
            dbias_ref[...] = jnp.zeros_like(dbias_ref)
            dalog_ref[...] = jnp.zeros_like(dalog_ref)

        @pl.when(k == 0)
        def _():
            dstate[...] = jnp.zeros_like(dstate)

        tri_v, trit_v = tri_ref[...], trit_ref[...]
        mask = tri_v > 0.5

        def fn(dtraw, bias, alog, *per_group):
            dt_all, a_all, s_all = _ssd_dt(dtraw, bias, alog, tri_v, trit_v)
            ys, s_outs = [], []
            for j in range(gs):
                xs, bm, cm, s_in = per_group[4 * j:4 * j + 4]
                y, s_out = _ssd_chunk(xs, bm, cm, dt_all, a_all, s_all, s_in, mask, d * nh + (gb * gs + j) * hpg)
                ys.append(y)
                s_outs.append(s_out)
            return ys, s_outs

        per_group = []
        for j in range(gs):
            o = j * gw
            per_group += [x_ref[:, o:o + xw], x_ref[:, o + xw:o + xw + SSD_STATE], x_ref[:, o + xw + SSD_STATE:o + gw],
                          sin_ref[j]]
        _, vjp = jax.vjp(fn, dt_ref[...], bias_ref[...], alog_ref[...], *per_group)
        cts = vjp(([dy_ref[:, j * xw:(j + 1) * xw] for j in range(gs)], [dstate[j] for j in range(gs)]))
        ddt, dbias, dalog = cts[:3]
        for j in range(gs):
            o = j * gw
            dxs, dbm, dcm, ds_in = cts[3 + 4 * j:7 + 4 * j]
            dx_ref[:, o:o + xw] = dxs
            dx_ref[:, o + xw:o + xw + SSD_STATE] = dbm
            dx_ref[:, o + xw + SSD_STATE:o + gw] = dcm
            dstate[j] = ds_in
        row0 = pl.multiple_of(_chunk_of(d, nc - 1 - k, ncc, nc) * q, q)
        ddt_ref[pl.ds(row0, q), :] += ddt
        dbias_ref[...] += dbias
        dalog_ref[...] += dalog

    ch = lambda d, g, k: _chunk_of(d, nc - 1 - k, ncc, nc)
    res, xres = hosted_call(
        body, ex, [xbc, proj, dt_bias, a_log, tri, tri_t, states, dy], name=name,
        out_shape=[jax.ShapeDtypeStruct((2 * T, G * gw), F32), jax.ShapeDtypeStruct((T, 128), F32),
                   jax.ShapeDtypeStruct((1, 128), F32), jax.ShapeDtypeStruct((1, 128), F32)],
        grid=(2, G // gs, nc),
        in_specs=[pl.BlockSpec((q, gs * gw), lambda d, g, k: (ch(d, g, k), g)),
                  pl.BlockSpec((q, 128), lambda d, g, k: (ch(d, g, k), dt_cb)),
                  pl.BlockSpec((1, 128), lambda d, g, k: (0, 0)),
                  pl.BlockSpec((1, 128), lambda d, g, k: (0, 0)),
                  pl.BlockSpec((None, q, q), lambda d, g, k: (d, 0, 0)),
                  pl.BlockSpec((None, q, q), lambda d, g, k: (d, 0, 0)),
                  pl.BlockSpec((None, None, gs, SSD_STATE, xw), lambda d, g, k: (d, nc - 1 - k, g, 0, 0)),
                  pl.BlockSpec((q, gs * xw), lambda d, g, k: (ch(d, g, k), g))],
        out_specs=[pl.BlockSpec((q, gs * gw), lambda d, g, k: (d * nc + ch(d, g, k), g)),
                   pl.BlockSpec((T, 128), lambda d, g, k: (0, 0)),
                   pl.BlockSpec((1, 128), lambda d, g, k: (0, 0)),
                   pl.BlockSpec((1, 128), lambda d, g, k: (0, 0))],
        scratch_shapes=[pltpu.VMEM((gs, SSD_STATE, xw), F32)])
    return res[0], res[1], res[2], res[3], xres


def _perm_xbc(a):
    G = SSD_GROUPS
    n = a.shape[-1]
    gn = G * SSD_STATE
    di = n - 2 * gn
    lead = a.shape[:-1]
    xs = a[..., :di].reshape(lead + (G, di // G))
    bm = a[..., di:di + gn].reshape(lead + (G, SSD_STATE))
    cm = a[..., di + gn:].reshape(lead + (G, SSD_STATE))
    return jnp.concatenate([xs, bm, cm], axis=-1).reshape(lead + (n,))


def _unperm_xbc(a):
    G = SSD_GROUPS
    n = a.shape[-1]
    gn = G * SSD_STATE
    di = n - 2 * gn
    lead = a.shape[:-1]
    r = a.reshape(lead + (G, n // G))
    xw = di // G
    return jnp.concatenate([r[..., :xw].reshape(lead + (di,)), r[..., xw:xw + SSD_STATE].reshape(lead + (gn,)),
                            r[..., xw + SSD_STATE:].reshape(lead + (gn,))], axis=-1)


def _pool_consts(tm, n_ctx):
    assert n_ctx == tm and tm % GRID_W == 0
    mats, cnts = [], []
    for seq in (n_ctx, GRID_W):
        t = np.arange(tm)
        tt = t % seq
        base = t - tt
        ms, cs = [], []
        for k in POOL_WINDOWS:
            lo = np.clip(tt - k // 2, 0, seq) + base
            hi = np.clip(tt + k // 2, 0, seq) + base
            m = ((t[None, :] >= lo[:, None]) & (t[None, :] < hi[:, None])).astype(np.float32)
            ms.append(m)
            cs.append((1.0 / (hi - lo).astype(np.float32))[:, None])
        mats.append(np.stack(ms))
        cnts.append(np.stack(cs))
    m = np.stack(mats)
    return jnp.asarray(m), jnp.asarray(np.swapaxes(m, -1, -2)), jnp.asarray(np.stack(cnts).astype(np.float32))


def _prep_layer_weights(w_ada, b_ada, g_mix, w_in, conv_w, conv_b, dt_bias, a_log, d_skip, ssd_norm_w, w_ssd_out,
                        pool_w, pool_scale, w_pool_out, w_out, g_ffn, w_gate_up, w_down):
    D = w_in.shape[0]
    di = ssd_norm_w.shape[0]
    xbc = conv_w.shape[1]
    nh2 = dt_bias.size
    pw = pool_scale.shape[0]
    o = 0
    wz = w_in[:, o:o + di]; o += di
    wx = w_in[:, o:o + xbc]; o += xbc
    wdt = w_in[:, o:o + nh2]; o += nh2
    wp = w_in[:, o:o + pw]; o += pw
    wg = w_in[:, o:]
    w1 = jnp.concatenate([_perm_xbc(wx), wz, wg, wp, wdt, jnp.zeros((D, DT_PAD - nh2), w_in.dtype)], axis=1)
    pad128 = lambda v: jnp.concatenate([v.reshape(1, -1), jnp.zeros((1, 128 - v.size), F32)], axis=1)
    return dict(
        w_ada=w_ada, b_ada=b_ada.reshape(1, -1), g_mix=g_mix.reshape(1, -1), w1=w1,
        conv_w=_perm_xbc(conv_w), conv_b=_perm_xbc(conv_b.reshape(1, -1)),
        dt_bias=pad128(dt_bias), a_log=pad128(a_log),
        dskip=jnp.repeat(d_skip[0] + d_skip[1], SSD_HEADDIM).reshape(1, -1),
        ssd_norm_w=ssd_norm_w.reshape(1, -1), w_ssd_out=w_ssd_out, pool_w=pool_w,
        pool_scale=pool_scale.reshape(1, -1), w_pool_out=w_pool_out, w_out=w_out, g_ffn=g_ffn.reshape(1, -1),
        w_gate_up=w_gate_up, w_down=w_down)


def _unprep_layer_grads(g, dims):
    di, xbc, nh2, pw = dims
    d1 = g["w1"]
    o = 0
    dxbc = d1[:, o:o + xbc]; o += xbc
    dz = d1[:, o:o + di]; o += di
    dg = d1[:, o:o + 2 * pw]; o += 2 * pw
    dp = d1[:, o:o + pw]; o += pw
    ddt = d1[:, o:o + nh2]
    nh = nh2 // 2
    dsk = g["dskip"].reshape(nh, SSD_HEADDIM).sum(axis=1)
    return dict(
        w_ada=g["w_ada"], b_ada=g["b_ada"].reshape(-1), g_mix=g["g_mix"].reshape(-1),
        w_in=jnp.concatenate([dz, _unperm_xbc(dxbc), ddt, dp, dg], axis=1),
        conv_w=_unperm_xbc(g["conv_w"]), conv_b=_unperm_xbc(g["conv_b"]).reshape(-1),
        dt_bias=g["dt_bias"][0, :nh2].reshape(2, nh), a_log=g["a_log"][0, :nh2].reshape(2, nh),
        d_skip=jnp.stack([dsk, dsk]), ssd_norm_w=g["ssd_norm_w"].reshape(-1), w_ssd_out=g["w_ssd_out"],
        pool_w=g["pool_w"], pool_scale=g["pool_scale"].reshape(-1), w_pool_out=g["w_pool_out"], w_out=g["w_out"],
        g_ffn=g["g_ffn"].reshape(-1), w_gate_up=g["w_gate_up"], w_down=g["w_down"])


COND_ROWS = 16


def _split_mods(m):
    d = m.shape[1] // 6
    return [m[:2, k * d:(k + 1) * d].reshape(2, 1, d) for k in range(6)]


def _hosted(hosts, box, key):
    fn = (hosts or {}).get(key)
    return fn(box) if fn else None


def _layer_fwd(l, x, cond_s, w, rows, n_ctx, pc, hosts=None, box=None):
    T, D = x.shape
    nt, nct, tm = rows.nt, rows.nct, rows.tm
    n = lambda s: f"l{l}_{s}"
    crow = Rows(1, 0, COND_ROWS)
    mraw = matmul_nn(n("ada_mm"), cond_s, w["w_ada"])
    (m,) = stage_fwd(n("ada_bias"), f_bias, crow, [crow.row(mraw, mraw.shape[1]), crow.vec(w["b_ada"])],
                     [(mraw.shape[1], F32, False)])
    sh1, sc1, ga1, sh2, sc2, ga2 = _split_mods(m)

    (h1,) = stage_fwd(n("norm1"), f_norm_mod, rows,
                      [rows.row(x, D), rows.vec(w["g_mix"]), rows.segvec(sh1), rows.segvec(sc1)],
                      [(D, ACT_DTYPE, False)])
    proj = matmul_nn(n("in_mm"), h1, w["w1"])
    xbc_w = w["conv_w"].shape[1]
    di = w["ssd_norm_w"].shape[1]
    pw = w["pool_scale"].shape[1]
    c_z, c_g, c_p, c_dt = xbc_w, xbc_w + di, xbc_w + di + 2 * pw, xbc_w + di + 3 * pw
    xbc = conv_fwd(n("conv"), proj, w["conv_w"], w["conv_b"], n_ctx, xbc_w)
    ex = _hosted(hosts, box, "ssd")
    y2, states, xres = ssd_fwd(n("ssd"), xbc, proj, c_dt // 128, w["dt_bias"], w["a_log"], n_ctx, ex)
    if ex is not None:
        box["ssd"] = xres

    G = SSD_GROUPS
    gw = di // G
    r8 = Rows(nt, nct, tm, G)
    gate_args = [r8.row(y2, gw, 0, True), r8.row(y2, gw, 0, True, roff=nt), r8.row(xbc, gw, 0, True, stride=2),
                 r8.row(proj, gw, c_z // gw, True), r8.vec(w["dskip"], True), r8.vec(w["ssd_norm_w"], True)]
    (ynw,) = stage_fwd(n("ssd_gate"), f_ssd_gate, r8, gate_args, [(gw, ACT_DTYPE, True)])
    o_ssd = matmul_nn(n("ssd_out_mm"), ynw, w["w_ssd_out"])

    nw = len(POOL_WINDOWS)
    pg = pw // nw
    r4 = Rows(nt, nct, tm, nw)
    pmat, pmat_t, inv_cnt = pc
    cblk = lambda a: Arg(a, (None, None) + a.shape[2:], lambda j, i, s: (s, j, 0, 0), "const")
    pool_args = [r4.row(proj, pg, c_p // pg, True), cblk(pmat), cblk(pmat_t), cblk(inv_cnt),
                 Arg(w["pool_w"], (None, pg, pg), lambda j, i, s: (j, 0, 0), "acc"), r4.vec(w["pool_scale"], True)]
    (ps,) = stage_fwd(n("pool"), f_pool, r4, pool_args, [(pg, ACT_DTYPE, True)])
    o_pool = matmul_nn(n("pool_out_mm"), ps, w["w_pool_out"])

    merge_args = [rows.row(o_ssd, D), rows.row(o_pool, D), rows.row(proj, pw, c_g // pw), rows.row(proj, pw, c_g // pw + 1)]
    (mg,) = stage_fwd(n("merge"), f_merge, rows, merge_args, [(D, ACT_DTYPE, False)])
    mo = matmul_nn(n("out_mm"), mg, w["w_out"])

    rn_args = [rows.row(x, D), rows.row(mo, D), rows.segvec(ga1), rows.vec(w["g_ffn"]), rows.segvec(sh2), rows.segvec(sc2)]
    x1, h2 = stage_fwd(n("norm2"), f_resid_norm_mod, rows, rn_args, [(D, F32, False), (D, ACT_DTYPE, False)])
    ex = _hosted(hosts, box, "gate_up_mm")
    gu = matmul_nn(n("gate_up_mm"), h2, w["w_gate_up"], ex=ex)
    if ex is not None:
        gu, box["gate_up_mm"] = gu
    fh = gu.shape[1] // 2
    sw_args = [rows.row(gu, fh, 0), rows.row(gu, fh, 1)]
    (act,) = stage_fwd(n("swiglu"), f_swiglu, rows, sw_args, [(fh, ACT_DTYPE, False)])
    dn = matmul_nn(n("down_mm"), act, w["w_down"])
    res_args = [rows.row(x1, D), rows.row(dn, D), rows.segvec(ga2)]
    (x2,) = stage_fwd(n("resid2"), f_resid, rows, res_args, [(D, F32, False)])
    saved = dict(x=x, mraw=mraw, mods=(sh1, sc1, ga1, sh2, sc2, ga2), h1=h1, proj=proj, xbc=xbc, y2=y2, states=states,
                 ynw=ynw, o_ssd=o_ssd, ps=ps, o_pool=o_pool, mg=mg, mo=mo, x1=x1, h2=h2, gu=gu, act=act, dn=dn,
                 cols=(c_z, c_g, c_p, c_dt))
    return x2, saved


def f_norm_mod_keep(x, g, sh, sc):
    return f_norm_mod(x, g, sh, sc)[0], x


def _layer_bwd(l, dx2, cond_s, w, s, rows, n_ctx, pc, hosts=None, box=None):
    T, D = dx2.shape
    nt, nct, tm = rows.nt, rows.nct, rows.tm
    n = lambda t: f"l{l}_{t}_bwd"
    sh1, sc1, ga1, sh2, sc2, ga2 = s["mods"]
    c_z, c_g, c_p, c_dt = s["cols"]
    x, proj, xbc, y2, gu = s["x"], s["proj"], s["xbc"], s["y2"], s["gu"]
    g = {}

    res_args = [rows.row(s["x1"], D), rows.row(s["dn"], D), rows.segvec(ga2)]
    dx1, ddn, dga2 = stage_bwd(n("resid2"), f_resid, rows, res_args, [rows.row(dx2, D)], [F32, ACT_DTYPE])
    ex = _hosted(hosts, box, "down_dx")
    dact = matmul_nt(n("down_dx"), ddn, w["w_down"], ex=ex)
    if ex is not None:
        dact, box["down_dx"] = dact
    g["w_down"] = matmul_tn(n("down_dw"), s["act"], ddn)
    fh = gu.shape[1] // 2
    sw_args = [rows.row(gu, fh, 0), rows.row(gu, fh, 1)]
    da, db = stage_bwd(n("swiglu"), f_swiglu, rows, sw_args, [rows.row(dact, fh)], [ACT_DTYPE, ACT_DTYPE])
    dgu = jnp.concatenate([da, db], axis=1)
    dh2 = matmul_nt(n("gate_up_dx"), dgu, w["w_gate_up"])
    g["w_gate_up"] = matmul_tn(n("gate_up_dw"), s["h2"], dgu)

    rn_args = [rows.row(x, D), rows.row(s["mo"], D), rows.segvec(ga1), rows.vec(w["g_ffn"]), rows.segvec(sh2), rows.segvec(sc2)]
    dxr, dmo, dga1, g["g_ffn"], dsh2, dsc2 = stage_bwd(
        n("norm2"), f_resid_norm_mod, rows, rn_args, [rows.row(dx1, D), rows.row(dh2, D)], [F32, ACT_DTYPE])
    dmg = matmul_nt(n("out_dx"), dmo, w["w_out"])
    g["w_out"] = matmul_tn(n("out_dw"), s["mg"], dmo)

    pw = w["pool_scale"].shape[1]
    merge_args = [rows.row(s["o_ssd"], D), rows.row(s["o_pool"], D), rows.row(proj, pw, c_g // pw), rows.row(proj, pw, c_g // pw + 1)]
    do_ssd, do_pool, dgl_s, dgl_p = stage_bwd(n("merge"), f_merge, rows, merge_args, [rows.row(dmg, D)], [ACT_DTYPE] * 4)
    dps = matmul_nt(n("pool_out_dx"), do_pool, w["w_pool_out"])
    g["w_pool_out"] = matmul_tn(n("pool_out_dw"), s["ps"], do_pool)

    nw = len(POOL_WINDOWS)
    pg = pw // nw
    r4 = Rows(nt, nct, tm, nw)
    pmat, pmat_t, inv_cnt = pc
    cblk = lambda a: Arg(a, (None, None) + a.shape[2:], lambda j, i, s_: (s_, j, 0, 0), "const")
    pool_args = [r4.row(proj, pg, c_p // pg, True), cblk(pmat), cblk(pmat_t), cblk(inv_cnt),
                 Arg(w["pool_w"], (None, pg, pg), lambda j, i, s_: (j, 0, 0), "acc"), r4.vec(w["pool_scale"], True)]
    du_pool, g["pool_w"], g["pool_scale"] = stage_bwd(n("pool"), f_pool, r4, pool_args, [r4.row(dps, pg, 0, True)], [ACT_DTYPE])

    dynw = matmul_nt(n("ssd_out_dx"), do_ssd, w["w_ssd_out"])
    g["w_ssd_out"] = matmul_tn(n("ssd_out_dw"), s["ynw"], do_ssd)
    G = SSD_GROUPS
    di = w["ssd_norm_w"].shape[1]
    gw = di // G
    r8 = Rows(nt, nct, tm, G)
    gate_args = [r8.row(y2, gw, 0, True), r8.row(y2, gw, 0, True, roff=nt), r8.row(xbc, gw, 0, True, stride=2),
                 r8.row(proj, gw, c_z // gw, True), r8.vec(w["dskip"], True), r8.vec(w["ssd_norm_w"], True)]
    dy, _, dxs_skip, dz, g["dskip"], g["ssd_norm_w"] = stage_bwd(
        n("ssd_gate"), f_ssd_gate, r8, gate_args, [r8.row(dynw, gw, 0, True)], [F32, F32, F32, ACT_DTYPE])

    ex = _hosted(hosts, box, "ssd")
    dxbc2, ddt, g["dt_bias"], g["a_log"], xres = ssd_bwd(n("ssd"), xbc, proj, c_dt // 128, w["dt_bias"], w["a_log"],
                                                         s["states"], dy, n_ctx, ex)
    if ex is not None:
        box["ssd"] = xres
    xbc_w = xbc.shape[1]
    dxbc_raw, g["conv_w"], g["conv_b"] = conv_bwd(n("conv"), proj, w["conv_w"], w["conv_b"], dxbc2, dxs_skip, n_ctx, xbc_w)
    dproj = jnp.concatenate([dxbc_raw, dz, dgl_s, dgl_p, du_pool, ddt.astype(ACT_DTYPE),
                             jnp.zeros((T, DT_PAD - 128), ACT_DTYPE)], axis=1)
    ex = _hosted(hosts, box, "in_dx")
    dh1 = matmul_nt(n("in_dx"), dproj, w["w1"], ex=ex)
    if ex is not None:
        dh1, box["in_dx"] = dh1
    g["w1"] = matmul_tn(n("in_dw"), s["h1"], dproj)

    n1_args = [rows.row(x, D), rows.vec(w["g_mix"]), rows.segvec(sh1), rows.segvec(sc1)]
    dx, g["g_mix"], dsh1, dsc1 = stage_bwd(n("norm1"), f_norm_mod_keep, rows, n1_args,
                                           [rows.row(dh1, D), rows.row(dxr, D)], [F32])

    dm = jnp.concatenate([v.reshape(2, D) for v in (dsh1, dsc1, dga1, dsh2, dsc2, dga2)], axis=1)
    dm = jnp.concatenate([dm, jnp.zeros((COND_ROWS - 2, dm.shape[1]), F32)], axis=0)
    crow = Rows(1, 0, COND_ROWS)
    dmraw, g["b_ada"] = stage_bwd(n("ada_bias"), f_bias, crow, [crow.row(s["mraw"], dm.shape[1]), crow.vec(w["b_ada"])],
                                  [crow.row(dm, dm.shape[1])], [ACT_DTYPE])
    dcs = matmul_nt(n("ada_dx"), dmraw, w["w_ada"])
    g["w_ada"] = matmul_tn(n("ada_dw"), cond_s, dmraw)
    return dx, dcs, g


def local_step(x, ctx, c, c_ctx, target, layer_w_fn, n_layers, g_final, fwd_hosts=None, bwd_hosts=None):
    L, D = x.shape
    n_ctx = ctx.shape[0]
    tm = ROW_TILE
    T = L + n_ctx
    rows = Rows(T // tm, n_ctx // tm, tm)
    pc = _pool_consts(tm, n_ctx)
    xa = jnp.concatenate([ctx, x], axis=0)
    cond = jnp.concatenate([c_ctx.reshape(1, D), c.reshape(1, D), jnp.zeros((COND_ROWS - 2, D), F32)], axis=0)
    crow = Rows(1, 0, COND_ROWS)
    (cond_s,) = stage_fwd("cond_silu", f_silu, crow, [crow.row(cond, D)], [(D, ACT_DTYPE, False)])

    saved, layer_w = [], []
    for l in range(n_layers):
        layer_w.append(layer_w_fn(l))
        box = {}
        xa, s = _layer_fwd(l, xa, cond_s, layer_w[l], rows, n_ctx, pc, fwd_hosts(l, box) if fwd_hosts else None, box)
        saved.append(s)

    rl = Rows(L // tm, 0, tm)
    gf = g_final.reshape(1, D)
    tgt = rl.row(target, D)
    tgt.kind = "const"
    loss_args = [rl.row(xa, D, roff=n_ctx // tm), tgt, rl.vec(gf)]
    (loss_rows,) = stage_fwd("loss", f_loss, rl, loss_args, [(1, F32, False)])
    ones = jnp.ones((L, 1), F32)
    dx_lat, dgf = stage_bwd("loss_bwd", f_loss, rl, loss_args, [rl.row(ones, 1)], [F32])
    loss = jnp.sum(loss_rows)
    dx = jnp.concatenate([jnp.zeros((n_ctx, D), F32), dx_lat], axis=0)

    grads = [None] * n_layers
    dcs = jnp.zeros((COND_ROWS, D), F32)
    for l in reversed(range(n_layers)):
        box = {}
        hosts = bwd_hosts(l, grads, box) if bwd_hosts else None
        dx, dcs_l, grads[l] = _layer_bwd(l, dx, cond_s, layer_w[l], saved[l], rows, n_ctx, pc, hosts, box)
        dcs = dcs + dcs_l
    (dcond,) = stage_bwd("cond_silu_bwd", f_silu, crow, [crow.row(cond, D)], [crow.row(dcs, D)], [F32])
    return loss, dx[n_ctx:], grads, dcond[0], dgf


def gather_chips(halves, conv=None):
    n = len(halves)
    ops = list(halves) + ([conv] if conv is not None else [])

    def copies(ins, outs, pos):
        c, me = pos[2], _chip_index(pos)
        pairs = [(s.at[c], o.at[me, c]) for s, o in zip(ins[:n], outs[:n])]
        pairs += [(s, o.at[me]) for s, o in zip(ins[n:], outs[n:])]
        return pairs, [(s, d, _flip(pos, rel)) for rel in PLANE for s, d in pairs]

    shapes = [jax.ShapeDtypeStruct((4,) + s.shape, s.dtype) for s in ops]
    return Exchange(copies, 3 * len(ops), len(ops), ops, shapes)


def gather_pair(gathered):
    n = len(gathered)

    def copies(ins, outs, pos):
        c = pos[2]
        return [], [(s.at[b, c], o.at[b, c], _flip(pos, PAIR[0])) for s, o in zip(ins, outs) for b in range(4)]

    shapes = [jax.ShapeDtypeStruct(g.shape, g.dtype) for g in gathered]
    return Exchange(copies, 4 * n, 0, gathered, shapes, aliases={k: k for k in range(n)})


def swap_halves(grads):
    n = len(grads)

    def copies(ins, outs, pos):
        c = pos[2]
        return [], [(g.at[b, 1 - c], o.at[b], _flip(pos, PAIR[0])) for g, o in zip(ins, outs) for b in range(4)]

    shapes = [jax.ShapeDtypeStruct((g.shape[0],) + g.shape[2:], g.dtype) for g in grads]
    return Exchange(copies, 4 * n, 0, grads, shapes)


def scatter_chips(sums):
    n = len(sums)

    def copies(ins, outs, pos):
        me = _chip_index(pos)
        local = [(p.at[me], o.at[me]) for p, o in zip(ins, outs)]
        remote = []
        for rel in PLANE:
            peer = _flip(pos, rel)
            remote += [(p.at[_chip_index(peer)], o.at[me], peer) for p, o in zip(ins, outs)]
        return local, remote

    shapes = [jax.ShapeDtypeStruct(p.shape, p.dtype) for p in sums]
    return Exchange(copies, 3 * n, n, sums, shapes)


def share_halves(finals):
    n = len(finals)

    def copies(ins, outs, pos):
        c = pos[2]
        return [], [(f.at[c], o.at[c], _flip(pos, PAIR[0])) for f, o in zip(ins, outs)]

    shapes = [jax.ShapeDtypeStruct(f.shape, f.dtype) for f in finals]
    return Exchange(copies, n, 0, finals, shapes, aliases={k: k for k in range(n)})


def gather_everyone(vec):
    def copies(ins, outs, pos):
        me = _device_index(pos)
        (v,), (o,) = ins, outs
        return [(v, o.at[me])], [(v, o.at[me], _flip(pos, rel)) for rel in EVERYONE]

    return Exchange(copies, len(EVERYONE), 1, [vec], [jax.ShapeDtypeStruct((8,) + vec.shape, vec.dtype)])


def _row_tile(rows, cols, n_bufs):
    cap = VMEM_LIMIT_BYTES // 4 // (2 * n_bufs * cols * 4)
    for t in (1024, 512, 256, 128, 64, 32, 16, 8):
        if t <= cap and rows % t == 0:
            return t
    return rows


WIRE_DTYPE = jnp.bfloat16


def add_own_half(name, grads, recv, c):
    nb, _, R, C = grads.shape
    tr = _row_tile(R, C, 3)

    def body(c_ref, g_ref, r_ref, o_ref):
        o_ref[...] = (g_ref[...] + r_ref[...]).astype(o_ref.dtype)

    spec = pl.BlockSpec((None, tr, C), lambda b, i, c_ref: (b, i, 0))
    return _pcall(
        body, name=name, out_shape=jax.ShapeDtypeStruct(recv.shape, WIRE_DTYPE),
        grid_spec=pltpu.PrefetchScalarGridSpec(
            num_scalar_prefetch=1, grid=(nb, R // tr),
            in_specs=[pl.BlockSpec((None, None, tr, C), lambda b, i, c_ref: (b, c_ref[0], i, 0)), spec],
            out_specs=spec),
        compiler_params=_params("parallel", "parallel"),
    )(c, grads, recv)


def sum_slots(name, a, c=None):
    n, R, C = a.shape
    tr = _row_tile(R, C, n + 1)

    def body(*refs):
        a_ref, o_ref = refs[-2:]
        acc = a_ref[0].astype(F32)
        for k in range(1, n):
            acc = acc + a_ref[k].astype(F32)
        o_ref[...] = acc

    if c is None:
        return _pcall(
            body, name=name, out_shape=jax.ShapeDtypeStruct((R, C), F32), grid=(R // tr,),
            in_specs=[pl.BlockSpec((n, tr, C), lambda i: (0, i, 0))], out_specs=pl.BlockSpec((tr, C), lambda i: (i, 0)),
            compiler_params=_params("parallel"),
        )(a)
    return _pcall(
        body, name=name, out_shape=jax.ShapeDtypeStruct((2, R, C), F32),
        grid_spec=pltpu.PrefetchScalarGridSpec(
            num_scalar_prefetch=1, grid=(R // tr,),
            in_specs=[pl.BlockSpec((n, tr, C), lambda i, c_ref: (0, i, 0))],
            out_specs=pl.BlockSpec((None, tr, C), lambda i, c_ref: (c_ref[0], i, 0))),
        compiler_params=_params("parallel"),
    )(c, a)


def adamw(name, w, g_layers, m, v):
    nl, R, C = w.shape
    assert len(g_layers) == nl
    tr = _row_tile(R, C, 8 + nl)
    nr = R // tr

    def body(*refs):
        w_ref, m_ref, v_ref = refs[:3]
        g_refs = refs[3:3 + nl]
        go_ref, d_ref, nm_ref, nv_ref = refs[3 + nl:]
        l = pl.program_id(0)
        gr = g_refs[0][...]
        for k in range(1, nl):
            gr = jnp.where(l == k, g_refs[k][...], gr)
        nm = ADAM_B1 * m_ref[...] + (1.0 - ADAM_B1) * gr
        nv = ADAM_B2 * v_ref[...] + (1.0 - ADAM_B2) * jnp.square(gr)
        m_hat = nm / (1.0 - ADAM_B1 ** ADAM_STEP)
        v_hat = nv / (1.0 - ADAM_B2 ** ADAM_STEP)
        d_ref[...] = -ADAM_LR * (m_hat / (jnp.sqrt(v_hat) + ADAM_EPS) + ADAM_WD * w_ref[...])
        go_ref[...] = gr
        nm_ref[...] = nm
        nv_ref[...] = nv

    spec = pl.BlockSpec((None, tr, C), lambda l, i: (l, i, 0))
    g_specs = [pl.BlockSpec((tr, C), (lambda l, i, k=k: (jnp.where(l == k, i, jnp.where(l < k, 0, nr - 1)), 0)))
               for k in range(nl)]
    return _pcall(
        body, name=name, out_shape=[jax.ShapeDtypeStruct((nl, R, C), F32)] * 4, grid=(nl, nr),
        in_specs=[spec] * 3 + g_specs, out_specs=[spec] * 4, compiler_params=_params("arbitrary", "arbitrary"),
    )(w, m, v, *g_layers)


BIG = ("w_ada", "w_in", "w_ssd_out", "pool_w", "w_pool_out", "w_out", "w_gate_up", "w_down")
COL_SHARDED = ("w_ada", "w_in", "w_gate_up")
SMALL = ("c_ctx", "b_ada", "g_mix", "conv_w", "conv_b", "dt_bias", "a_log", "d_skip", "ssd_norm_w", "pool_scale",
         "g_ffn", "g_final")
WEIGHTS = ("c_ctx", "w_ada", "b_ada", "g_mix", "w_in", "conv_w", "conv_b", "dt_bias", "a_log", "d_skip", "ssd_norm_w",
           "w_ssd_out", "pool_w", "pool_scale", "w_pool_out", "w_out", "g_ffn", "w_gate_up", "w_down", "g_final")
LAYER_KEYS = ("w_ada", "b_ada", "g_mix", "w_in", "conv_w", "conv_b", "dt_bias", "a_log", "d_skip", "ssd_norm_w",
              "w_ssd_out", "pool_w", "pool_scale", "w_pool_out", "w_out", "g_ffn", "w_gate_up", "w_down")


def _shard2d(name, a):
    if name == "pool_w":
        return a.reshape(a.shape[0], a.shape[1] * a.shape[2], a.shape[3])
    return a


def _full_from_blocks(name, a):
    nb, R, C = a.shape
    if name in COL_SHARDED:
        return jnp.transpose(a, (1, 0, 2)).reshape(R, nb * C)
    if name == "pool_w":
        nw = len(POOL_WINDOWS)
        return jnp.transpose(a.reshape(nb, nw, R // nw, C), (1, 0, 2, 3)).reshape(nw, nb * R // nw, C)
    return a.reshape(nb * R, C)


def _blocks_from_full(name, g):
    nb = 4
    if name in COL_SHARDED:
        K, N = g.shape
        return jnp.transpose(g.reshape(K, nb, N // nb), (1, 0, 2))
    if name == "pool_w":
        nw, r, C = g.shape
        return jnp.transpose(g.reshape(nw, nb, r // nb, C), (1, 0, 2, 3)).reshape(nb, nw * r // nb, C)
    return g.reshape(nb, g.shape[0] // nb, g.shape[1])


def _pack(arrs, rows):
    flat = jnp.concatenate([a.reshape(-1).astype(F32) for a in arrs])
    return jnp.concatenate([flat, jnp.zeros((rows * 128 - flat.size,), F32)]).reshape(rows, 128)


def _unpack(vec, shapes):
    flat = vec.reshape(-1)
    out, o = [], 0
    for s in shapes:
        n = int(np.prod(s))
        out.append(flat[o:o + n].reshape(s))
        o += n
    return out


def _rows_for(shapes):
    n = sum(int(np.prod(s)) for s in shapes)
    return -(-n // (8 * 128)) * 8


def kernel(x, c, ctx, c_ctx, w_ada, b_ada, g_mix, w_in, conv_w, conv_b, dt_bias, a_log, d_skip, ssd_norm_w, w_ssd_out, pool_w, pool_scale, w_pool_out, w_out, g_ffn, w_gate_up, w_down, g_final, loss_target, m_c_ctx, m_w_ada, m_b_ada, m_g_mix, m_w_in, m_conv_w, m_conv_b, m_dt_bias, m_a_log, m_d_skip, m_ssd_norm_w, m_w_ssd_out, m_pool_w, m_pool_scale, m_w_pool_out, m_w_out, m_g_ffn, m_w_gate_up, m_w_down, m_g_final, v_c_ctx, v_w_ada, v_b_ada, v_g_mix, v_w_in, v_conv_w, v_conv_b, v_dt_bias, v_a_log, v_d_skip, v_ssd_norm_w, v_w_ssd_out, v_pool_w, v_pool_scale, v_w_pool_out, v_w_out, v_g_ffn, v_w_gate_up, v_w_down, v_g_final):
    w = dict(c_ctx=c_ctx, w_ada=w_ada, b_ada=b_ada, g_mix=g_mix, w_in=w_in, conv_w=conv_w, conv_b=conv_b, dt_bias=dt_bias,
             a_log=a_log, d_skip=d_skip, ssd_norm_w=ssd_norm_w, w_ssd_out=w_ssd_out, pool_w=pool_w, pool_scale=pool_scale,
             w_pool_out=w_pool_out, w_out=w_out, g_ffn=g_ffn, w_gate_up=w_gate_up, w_down=w_down, g_final=g_final)
    m = dict(c_ctx=m_c_ctx, w_ada=m_w_ada, b_ada=m_b_ada, g_mix=m_g_mix, w_in=m_w_in, conv_w=m_conv_w, conv_b=m_conv_b,
             dt_bias=m_dt_bias, a_log=m_a_log, d_skip=m_d_skip, ssd_norm_w=m_ssd_norm_w, w_ssd_out=m_w_ssd_out,
             pool_w=m_pool_w, pool_scale=m_pool_scale, w_pool_out=m_w_pool_out, w_out=m_w_out, g_ffn=m_g_ffn,
             w_gate_up=m_w_gate_up, w_down=m_w_down, g_final=m_g_final)
    v = dict(c_ctx=v_c_ctx, w_ada=v_w_ada, b_ada=v_b_ada, g_mix=v_g_mix, w_in=v_w_in, conv_w=v_conv_w, conv_b=v_conv_b,
             dt_bias=v_dt_bias, a_log=v_a_log, d_skip=v_d_skip, ssd_norm_w=v_ssd_norm_w, w_ssd_out=v_w_ssd_out,
             pool_w=v_pool_w, pool_scale=v_pool_scale, w_pool_out=v_w_pool_out, w_out=v_w_out, g_ffn=v_g_ffn,
             w_gate_up=v_w_gate_up, w_down=v_w_down, g_final=v_g_final)
    assert x.shape[0] == 1, "one example per device"
    pos = _position()
    core = pos[2].astype(jnp.int32).reshape(1)
    n_layers = w_in.shape[0]
    assert n_layers == 2
    dims = (ssd_norm_w.shape[1], conv_w.shape[2] * 4, dt_bias[0].size, pool_scale.shape[1])
    shard = {k: _shard2d(k, w[k]) for k in BIG}

    def halves(a):
        return a.reshape(a.shape[:-2] + (2, a.shape[-2] // 2, a.shape[-1]))

    def whole(a):
        return a.reshape(a.shape[:-3] + (2 * a.shape[-2], a.shape[-1]))

    def wire_shards(l):
        return [halves(shard[k][l].astype(MXU_DTYPE)) for k in BIG]

    first = comm_call("gather0_chips", gather_chips(wire_shards(0), conv=conv_w))
    got = {0: comm_call("gather0_pair", gather_pair(first[:-1]))}
    conv_all = first[-1]
    conv_full = [jnp.transpose(conv_all[:, l], (1, 0, 2)).reshape(conv_all.shape[2], -1) for l in range(n_layers)]

    boxes = {}

    def layer_w_fn(l):
        if l == 1:
            got[1] = boxes[("fwd", 0)]["gate_up_mm"]
        full = {k: _full_from_blocks(k, whole(a)) for k, a in zip(BIG, got[l])}
        full["conv_w"] = conv_full[l]
        return _prep_layer_weights(*[full[k] if k in full else w[k][l] for k in LAYER_KEYS])

    def fwd_hosts(l, box):
        boxes[("fwd", l)] = box
        if l != 0:
            return None
        return {"ssd": lambda box: gather_chips(wire_shards(1)), "gate_up_mm": lambda box: gather_pair(box["ssd"])}

    def blocks(gl):
        return [halves(_blocks_from_full(k, gl[k])) for k in BIG]

    def reduce_now(tag, gl):
        G = blocks(gl)
        recv = comm_call(f"swap{tag}", swap_halves(G))
        pair = [add_own_half(f"pair_sum{tag}_{k}", g, r, core) for k, g, r in zip(BIG, G, recv)]
        parts = comm_call(f"scatter{tag}", scatter_chips(pair))
        fin = [sum_slots(f"chip_sum{tag}_{k}", p, core) for k, p in zip(BIG, parts)]
        return [whole(a) for a in comm_call(f"share{tag}", share_halves(fin))]

    small_layers = {}

    def bwd_hosts(l, grads, box):
        boxes[("bwd", l)] = box
        if l != 0:
            return None
        gl1 = _unprep_layer_grads(grads[1], dims)
        small_layers[1] = gl1
        G = blocks(gl1)

        def scatter(box):
            pair = [add_own_half(f"pair_sum1_{k}", g, r, core) for k, g, r in zip(BIG, G, box["down_dx"])]
            return scatter_chips(pair)

        def share(box):
            return share_halves([sum_slots(f"chip_sum1_{k}", p, core) for k, p in zip(BIG, box["ssd"])])

        return {"down_dx": lambda box: swap_halves(G), "ssd": scatter, "in_dx": share}

    loss, grad_x, grads, d_c_ctx, d_g_final = local_step(
        x[0], ctx[0], c[0], c_ctx, loss_target[0], layer_w_fn, n_layers, g_final, fwd_hosts, bwd_hosts)
    loss = lax.psum(loss, ("x", "y", "c"))
    reduced1 = [whole(a) for a in boxes[("bwd", 0)]["in_dx"]]
    gl0 = _unprep_layer_grads(grads[0], dims)
    small_layers[0] = gl0
    reduced0 = reduce_now("0", gl0)

    small_full = dict(c_ctx=d_c_ctx, g_final=d_g_final.reshape(-1))
    for k in SMALL:
        if k not in small_full:
            small_full[k] = jnp.stack([small_layers[l][k] for l in range(n_layers)])
    shapes = [small_full[k].shape for k in SMALL]
    packed = _pack([small_full[k] for k in SMALL], _rows_for(shapes))
    total = sum_slots("small_sum", comm_call("gather_small", gather_everyone(packed))[0])
    small_g = dict(zip(SMALL, _unpack(total, shapes)))
    cw = conv_w.shape[2]
    small_g["conv_w"] = lax.dynamic_slice_in_dim(small_g["conv_w"], _chip_index(pos) * cw, cw, axis=2)

    grad, delta, new_m, new_v = {}, {}, {}, {}
    for k, g0, g1 in zip(BIG, reduced0, reduced1):
        shp = w[k].shape
        flat = lambda a: _shard2d(k, a)
        outs = adamw(f"adamw_{k}", flat(w[k]), [g0, g1], flat(m[k]), flat(v[k]))
        grad[k], delta[k], new_m[k], new_v[k] = [a.reshape(shp) for a in outs]
    sshapes = [w[k].shape for k in SMALL]
    srows = _rows_for(sshapes)
    pk = lambda d: _pack([d[k] for k in SMALL], srows)[None]
    _, d_, m_, v_ = adamw("adamw_small", pk(w), [pk(small_g)[0]], pk(m), pk(v))
    for k, dd, mm, vv in zip(SMALL, _unpack(d_, sshapes), _unpack(m_, sshapes), _unpack(v_, sshapes)):
        grad[k], delta[k], new_m[k], new_v[k] = small_g[k], dd, mm, vv

    return (loss, grad_x[None], *[grad[k] for k in WEIGHTS], *[delta[k] for k in WEIGHTS],
            *[new_m[k] for k in WEIGHTS], *[new_v[k] for k in WEIGHTS])
```

```python
import functools

import jax
import jax.numpy as jnp
import numpy as np
from jax import lax
from jax.experimental import pallas as pl
from jax.experimental.pallas import tpu as pltpu

F32 = jnp.float32
MXU_DTYPE = jnp.bfloat16
ACT_DTYPE = jnp.bfloat16
VMEM_LIMIT_BYTES = 48 * 1024 * 1024
EPS = 1e-6
NEG = -1e30

SSD_HEADDIM = 64
SSD_GROUPS = 8
SSD_STATE = 128
SSD_CHUNK = 128
SSD_GROUPS_PER_STEP = 8
SSD_CONV = 5
GRID_W = 64
POOL_WINDOWS = (2, 4, 8, 16)
ROW_TILE = 256
DT_PAD = 512

ADAM_LR = 0.001
ADAM_B1 = 0.9
ADAM_B2 = 0.999
ADAM_EPS = 1e-08
ADAM_WD = 0.01
ADAM_STEP = 10

MESH = pl.DeviceIdType.MESH


def _pcall(body, **kw):
    return pl.pallas_call(body, **kw)


def _params(*sem):
    return pltpu.CompilerParams(dimension_semantics=tuple(sem), vmem_limit_bytes=VMEM_LIMIT_BYTES)


def _pick_tile(n, cands):
    for t in cands:
        if n % t == 0:
            return t
    return n


PLANE = ((1, 0, 0), (0, 1, 0), (1, 1, 0))
PAIR = ((0, 0, 1),)
EVERYONE = tuple((a, b, d) for a in (0, 1) for b in (0, 1) for d in (0, 1) if a + b + d)
HBM = pl.BlockSpec(memory_space=pl.ANY)


def _position():
    return lax.axis_index("x"), lax.axis_index("y"), lax.axis_index("c")


def _flip(pos, rel):
    return tuple(1 - p if r else p for p, r in zip(pos, rel))


def _chip_index(pos):
    return 2 * pos[0] + pos[1]


def _device_index(pos):
    return 4 * pos[0] + 2 * pos[1] + pos[2]


class Exchange:
    def __init__(self, copies, n_remote, n_local, operands, out_shapes, aliases=None):
        self.copies, self.n_remote, self.n_local = copies, n_remote, n_local
        self.operands, self.out_shapes, self.aliases = list(operands), list(out_shapes), dict(aliases or {})

    def scratch(self):
        return [pltpu.SemaphoreType.DMA((max(self.n_remote, 1),)), pltpu.SemaphoreType.DMA((max(self.n_remote, 1),)),
                pltpu.SemaphoreType.DMA((max(self.n_local, 1),))]

    def descriptors(self, ins, outs, sems):
        send_sems, recv_sems, local_sems = sems
        local, remote = self.copies(ins, outs, _position())
        assert len(local) == self.n_local and len(remote) == self.n_remote
        cps = [pltpu.make_async_copy(src, dst, local_sems.at[k]) for k, (src, dst) in enumerate(local)]
        cps += [pltpu.make_async_remote_copy(src_ref=src, dst_ref=dst, send_sem=send_sems.at[k], recv_sem=recv_sems.at[k],
                                             device_id=peer, device_id_type=MESH) for k, (src, dst, peer) in enumerate(remote)]
        return cps


def combine(a, b):
    na, nao = len(a.operands), len(a.out_shapes)

    def copies(ins, outs, pos):
        la, ra = a.copies(ins[:na], outs[:nao], pos)
        lb, rb = b.copies(ins[na:], outs[nao:], pos)
        return la + lb, ra + rb

    aliases = dict(a.aliases)
    aliases.update({na + k: nao + v for k, v in b.aliases.items()})
    return Exchange(copies, a.n_remote + b.n_remote, a.n_local + b.n_local, a.operands + b.operands,
                    a.out_shapes + b.out_shapes, aliases)


class LazyDict(dict):
    def __getitem__(self, key):
        v = dict.__getitem__(self, key)
        if callable(v):
            v = v()
            dict.__setitem__(self, key, v)
        return v


def comm_call(name, ex):
    n_in, n_out = len(ex.operands), len(ex.out_shapes)

    def body(*refs):
        cps = ex.descriptors(refs[:n_in], refs[n_in:n_in + n_out], refs[n_in + n_out:])
        for cp in cps:
            cp.start()
        for cp in cps:
            cp.wait()

    return _pcall(
        body, name=name, out_shape=ex.out_shapes, in_specs=[HBM] * n_in, out_specs=[HBM] * n_out,
        scratch_shapes=ex.scratch(), input_output_aliases=ex.aliases,
        compiler_params=pltpu.CompilerParams(has_side_effects=True),
    )(*ex.operands)


def hosted_call(body, ex, operands, *, name, out_shape, grid, in_specs, out_specs, scratch_shapes=()):
    n_in, n_out, n_scr = len(operands), len(out_shape), len(scratch_shapes)
    sem = ("arbitrary",) * len(grid)
    if ex is None:
        res = _pcall(body, name=name, out_shape=list(out_shape), grid=grid, in_specs=list(in_specs),
                     out_specs=list(out_specs), scratch_shapes=list(scratch_shapes), compiler_params=_params(*sem))(*operands)
        return res, []
    x_in, x_out = len(ex.operands), len(ex.out_shapes)

    def wrapped(*refs):
        o = 0
        ins = refs[o:o + n_in]; o += n_in
        xins = refs[o:o + x_in]; o += x_in
        outs = refs[o:o + n_out]; o += n_out
        xouts = refs[o:o + x_out]; o += x_out
        scr = refs[o:o + n_scr]; o += n_scr
        sems = refs[o:]
        first = last = None
        for a, n in enumerate(grid):
            i = pl.program_id(a)
            first = (i == 0) if first is None else first & (i == 0)
            last = (i == n - 1) if last is None else last & (i == n - 1)

        @pl.when(first)
        def _():
            for cp in ex.descriptors(xins, xouts, sems):
                cp.start()

        body(*ins, *outs, *scr)

        @pl.when(last)
        def _():
            for cp in ex.descriptors(xins, xouts, sems):
                cp.wait()

    aliases = {n_in + k: n_out + v for k, v in ex.aliases.items()}
    res = _pcall(
        wrapped, name=name, out_shape=list(out_shape) + ex.out_shapes, grid=grid,
        in_specs=list(in_specs) + [HBM] * x_in, out_specs=list(out_specs) + [HBM] * x_out,
        scratch_shapes=list(scratch_shapes) + ex.scratch(), input_output_aliases=aliases,
        compiler_params=pltpu.CompilerParams(dimension_semantics=sem, vmem_limit_bytes=VMEM_LIMIT_BYTES,
                                             has_side_effects=True),
    )(*operands, *ex.operands)
    return res[:n_out], res[n_out:]


def _dot(a, b, dims):
    return lax.dot_general(a.astype(MXU_DTYPE), b.astype(MXU_DTYPE), (dims, ((), ())), preferred_element_type=F32)


_NN = ((1,), (0,))
_NT = ((1,), (1,))
_TN = ((0,), (0,))


@jax.custom_vjp
def _mm(a, b):
    return _dot(a, b, _NN)


def _mm_fwd(a, b):
    return _mm(a, b), (a, b)


def _mm_bwd(res, g):
    a, b = res
    return _dot(g, b, _NT).astype(a.dtype), _dot(a, g, _TN).astype(b.dtype)


_mm.defvjp(_mm_fwd, _mm_bwd)


@jax.custom_vjp
def _mm_nt(a, b):
    return _dot(a, b, _NT)


def _mm_nt_fwd(a, b):
    return _mm_nt(a, b), (a, b)


def _mm_nt_bwd(res, g):
    a, b = res
    return _dot(g, b, _NN).astype(a.dtype), _dot(g, a, _TN).astype(b.dtype)


_mm_nt.defvjp(_mm_nt_fwd, _mm_nt_bwd)


@jax.custom_vjp
def _mm_tn(a, b):
    return _dot(a, b, _TN)


def _mm_tn_fwd(a, b):
    return _mm_tn(a, b), (a, b)


def _mm_tn_bwd(res, g):
    a, b = res
    return _dot(b, g, _NT).astype(a.dtype), _dot(a, g, _NN).astype(b.dtype)


_mm_tn.defvjp(_mm_tn_fwd, _mm_tn_bwd)


def _dot_exact(m01, v):
    m = m01.astype(jnp.bfloat16)
    hi = v.astype(jnp.bfloat16)
    r1 = v - hi.astype(F32)
    mid = r1.astype(jnp.bfloat16)
    lo = (r1 - mid.astype(F32)).astype(jnp.bfloat16)
    out = jnp.dot(m, hi, preferred_element_type=F32)
    out = out + jnp.dot(m, mid, preferred_element_type=F32)
    return out + jnp.dot(m, lo, preferred_element_type=F32)


@jax.custom_vjp
def _lin01(m, mt, v):
    return _dot_exact(m, v)


def _lin01_fwd(m, mt, v):
    return _dot_exact(m, v), (m, mt)


def _lin01_bwd(res, g):
    m, mt = res
    return jnp.zeros_like(m), jnp.zeros_like(mt), _dot_exact(mt, g)


_lin01.defvjp(_lin01_fwd, _lin01_bwd)


MATMUL_VMEM_BUDGET = VMEM_LIMIT_BYTES * 3 // 4


def _mm_tiles(m, n, k_bytes_a, k_bytes_b, out_bytes, cands_m, cands_n):
    best = None
    for tm in cands_m:
        if m % tm:
            continue
        for tn in cands_n:
            if n % tn:
                continue
            need = 2 * (tm * k_bytes_a + tn * k_bytes_b + tm * tn * out_bytes)
            if need <= MATMUL_VMEM_BUDGET and (best is None or tm * tn > best[0] * best[1]):
                best = (tm, tn)
    assert best is not None, (m, n)
    return best


_ROW_CANDS = (1088, 768, 544, 512, 272, 256, 128, 16)
_COL_CANDS = (2816, 2048, 1408, 1024, 512, 256, 128)


def _one(res, xres, ex):
    return res[0] if ex is None else (res[0], xres)


def matmul_nn(name, a, b, out_dtype=F32, ex=None):
    M, K = a.shape
    N = b.shape[1]
    tm, tn = _mm_tiles(M, N, K * a.dtype.itemsize, K * b.dtype.itemsize, jnp.dtype(out_dtype).itemsize,
                       _ROW_CANDS, (512, 256, 128))

    def body(a_ref, b_ref, o_ref):
        o_ref[...] = _dot(a_ref[...], b_ref[...], _NN).astype(o_ref.dtype)

    res, xres = hosted_call(
        body, ex, [a, b], name=name, out_shape=[jax.ShapeDtypeStruct((M, N), out_dtype)], grid=(N // tn, M // tm),
        in_specs=[pl.BlockSpec((tm, K), lambda j, i: (i, 0)), pl.BlockSpec((K, tn), lambda j, i: (0, j))],
        out_specs=[pl.BlockSpec((tm, tn), lambda j, i: (i, j))])
    return _one(res, xres, ex)


def matmul_nt(name, g, b, out_dtype=F32, ex=None, offsets=None):
    pieces = list(g) if isinstance(g, (list, tuple)) else [g]
    offsets = list(offsets) if offsets is not None else [0]
    M = pieces[0].shape[0]
    K, N = b.shape
    g_bytes = sum(p.shape[1] * p.dtype.itemsize for p in pieces)
    tm, tk = _mm_tiles(M, K, g_bytes, N * b.dtype.itemsize, jnp.dtype(out_dtype).itemsize, _ROW_CANDS, _COL_CANDS)

    def body(*refs):
        b_ref, o_ref = refs[-2:]
        acc = None
        for g_ref, off in zip(refs[:-2], offsets):
            part = _dot(g_ref[...], b_ref[:, off:off + g_ref.shape[1]], _NT)
            acc = part if acc is None else acc + part
        o_ref[...] = acc.astype(o_ref.dtype)

    res, xres = hosted_call(
        body, ex, pieces + [b], name=name, out_shape=[jax.ShapeDtypeStruct((M, K), out_dtype)], grid=(K // tk, M // tm),
        in_specs=[pl.BlockSpec((tm, p.shape[1]), lambda j, i: (i, 0)) for p in pieces]
        + [pl.BlockSpec((tk, N), lambda j, i: (j, 0))],
        out_specs=[pl.BlockSpec((tm, tk), lambda j, i: (i, j))])
    return _one(res, xres, ex)


def matmul_tn(name, a, g, ex=None):
    M, K = a.shape
    N = g.shape[1]
    tk, tn = _mm_tiles(K, N, M * a.dtype.itemsize, M * g.dtype.itemsize, 4, (512, 256, 128), (512, 256, 128))

    def body(a_ref, g_ref, o_ref):
        o_ref[...] = _dot(a_ref[...], g_ref[...], _TN)

    res, xres = hosted_call(
        body, ex, [a, g], name=name, out_shape=[jax.ShapeDtypeStruct((K, N), F32)], grid=(K // tk, N // tn),
        in_specs=[pl.BlockSpec((M, tk), lambda i, j: (0, i)), pl.BlockSpec((M, tn), lambda i, j: (0, j))],
        out_specs=[pl.BlockSpec((tk, tn), lambda i, j: (i, j))])
    return _one(res, xres, ex)


class Arg:
    def __init__(self, arr, block, imap, kind):
        self.arr, self.block, self.imap, self.kind = arr, block, imap, kind


class Rows:
    def __init__(self, nt, nct, tm, ncol=1):
        self.nt, self.nct, self.tm, self.ncol = nt, nct, tm, ncol

    def seg(self, i):
        return jnp.where(i >= self.nct, 1, 0)

    def spec(self, block, imap):
        return pl.BlockSpec(block, lambda j, i: imap(j, i, self.seg(i)))

    def row(self, arr, width, cb0=0, follow=False, roff=0, stride=1):
        f = stride if follow else 0
        return Arg(arr, (self.tm, width), lambda j, i, s: (i + roff, cb0 + f * j), "row")

    def vec(self, arr, follow=False, kind="acc"):
        w = arr.shape[1] // (self.ncol if follow else 1)
        f = 1 if follow else 0
        return Arg(arr, (1, w), lambda j, i, s: (0, f * j), kind)

    def segvec(self, arr, kind="seg"):
        return Arg(arr, (None, 1, arr.shape[2]), lambda j, i, s: (s, 0, 0), kind)


def _load(ref):
    return ref[...].astype(F32) if ref.dtype != F32 else ref[...]


def stage_fwd(name, f, rows, args, outs):
    n_in = len(args)

    def body(*refs):
        vals = [_load(r) for r in refs[:n_in]]
        res = f(*vals)
        for r, v in zip(refs[n_in:], res):
            r[...] = v.astype(r.dtype)

    T = rows.nt * rows.tm
    out_shape = [jax.ShapeDtypeStruct((T, w * (rows.ncol if fo else 1)), dt) for w, dt, fo in outs]
    out_specs = [pl.BlockSpec((rows.tm, w), (lambda j, i, fo=fo: (i, j if fo else 0))) for w, dt, fo in outs]
    res = _pcall(
        body, name=name, out_shape=out_shape, grid=(rows.ncol, rows.nt),
        in_specs=[rows.spec(a.block, a.imap) for a in args], out_specs=out_specs,
        compiler_params=_params("parallel", "parallel"),
    )(*[a.arr for a in args])
    return res


def stage_bwd(name, f, rows, args, cots, row_dtypes):
    n_in, n_ct = len(args), len(cots)
    diff = [k for k, a in enumerate(args) if a.kind != "const"]
    row_dt = {}
    for k in diff:
        if args[k].kind == "row":
            row_dt[k] = row_dtypes[len(row_dt)]

    def body(*refs):
        i = pl.program_id(1)
        vals = [_load(r) for r in refs[:n_in]]
        cts = tuple(_load(r) for r in refs[n_in:n_in + n_ct])
        outs = refs[n_in + n_ct:]

        def g(*dv):
            full = list(vals)
            for k, v in zip(diff, dv):
                full[k] = v
            return tuple(f(*full))

        _, vjp = jax.vjp(g, *[vals[k] for k in diff])
        grads = vjp(cts)
        for k, o, gr in zip(diff, outs, grads):
            kind = args[k].kind
            if kind == "row":
                o[...] = gr.astype(o.dtype)
            else:
                first = (i == 0) | (i == rows.nct) if kind == "seg" else (i == 0)

                @pl.when(first)
                def _():
                    o[...] = gr.astype(o.dtype)

                @pl.when(jnp.logical_not(first))
                def _():
                    o[...] += gr.astype(o.dtype)

    T = rows.nt * rows.tm
    out_shape, out_specs = [], []
    for k in diff:
        a = args[k]
        if a.kind == "row":
            out_shape.append(jax.ShapeDtypeStruct((T, a.block[1] * (rows.ncol if _follows(a) else 1)), row_dt[k]))
            fo = _follows(a)
            out_specs.append(pl.BlockSpec(a.block, (lambda j, i, fo=fo: (i, j if fo else 0))))
        else:
            out_shape.append(jax.ShapeDtypeStruct(a.arr.shape, F32))
            out_specs.append(rows.spec(a.block, a.imap))
    return _pcall(
        body, name=name, out_shape=out_shape, grid=(rows.ncol, rows.nt),
        in_specs=[rows.spec(a.block, a.imap) for a in list(args) + list(cots)], out_specs=out_specs,
        compiler_params=_params("arbitrary", "arbitrary"),
    )(*[a.arr for a in list(args) + list(cots)])


def _follows(a):
    return a.imap(1, 0, 0)[-1] != a.imap(0, 0, 0)[-1]


def _rms(x):
    return x * lax.rsqrt(jnp.mean(x * x, axis=-1, keepdims=True) + EPS)


def f_norm_mod(x, g, sh, sc):
    return ((_rms(x) * g) * (1.0 + sc) + sh,)


def f_resid_norm_mod(x, mo, ga, g, sh, sc):
    x1 = x + ga * mo
    return x1, (_rms(x1) * g) * (1.0 + sc) + sh


def f_resid(x, dn, ga):
    return (x + ga * dn,)


def f_silu(x):
    return (x * jax.nn.sigmoid(x),)


def f_bias(x, b):
    return (x + b,)


def f_ssd_gate(y0, y1, xs, z, dskip, nw):
    y = y0 + y1 + dskip * xs
    return (_rms(y * (z * jax.nn.sigmoid(z))) * nw,)


def f_pool(u, pmat, pmat_t, inv_cnt, pw, scale):
    pm = _lin01(pmat, pmat_t, u) * inv_cnt - u
    return (_mm(pm, pw) * scale,)


def f_merge(o_ssd, o_pool, gl_ssd, gl_pool):
    return (jax.nn.sigmoid(gl_ssd) * o_ssd + jax.nn.sigmoid(gl_pool) * o_pool,)


@jax.custom_vjp
def _halve_cols(x):
    h = x.shape[1] // 2
    return x[:, :h], x[:, h:]


def _halve_cols_fwd(x):
    return _halve_cols(x), None


def _halve_cols_bwd(_, g):
    return (jnp.concatenate(g, axis=1),)


_halve_cols.defvjp(_halve_cols_fwd, _halve_cols_bwd)


def f_swiglu(gu):
    a, b = _halve_cols(gu)
    return ((a * jax.nn.sigmoid(a)) * b,)


def f_loss(x, tgt, g):
    err = _rms(x) * g - tgt
    return (0.5 * jnp.mean(err * err, axis=-1, keepdims=True),)


CONV_TILE = 128


def _shift_rows(v, j, n_ctx):
    if j == 0:
        return v
    T = v.shape[0]
    r = lax.broadcasted_iota(jnp.int32, v.shape, 0)
    lo = jnp.where(r >= n_ctx, n_ctx, 0)
    hi = jnp.where(r >= n_ctx, T, n_ctx)
    ok = (r + j >= lo) & (r + j < hi)
    return jnp.where(ok, pltpu.roll(v, (-j) % T, 0), 0.0)


def conv_fwd(name, proj, conv_w, conv_b, n_ctx, width, ex=None):
    T = proj.shape[0]
    half = SSD_CONV // 2

    def body(u_ref, w_ref, b_ref, o_ref):
        u = u_ref[...]
        pre = jnp.broadcast_to(b_ref[...], u.shape)
        for k in range(SSD_CONV):
            pre = pre + w_ref[k:k + 1, :] * _shift_rows(u, k - half, n_ctx)
        o_ref[...] = pre * jax.nn.sigmoid(pre)

    col = lambda t: (0, t)
    res, xres = hosted_call(
        body, ex, [proj, conv_w, conv_b], name=name, out_shape=[jax.ShapeDtypeStruct((T, width), F32)],
        grid=(width // CONV_TILE,),
        in_specs=[pl.BlockSpec((T, CONV_TILE), col), pl.BlockSpec((SSD_CONV, CONV_TILE), col),
                  pl.BlockSpec((1, CONV_TILE), col)],
        out_specs=[pl.BlockSpec((T, CONV_TILE), col)])
    return res[0], xres


def conv_bwd(name, proj, conv_w, conv_b, d_act2, d_skip, n_ctx, width, ex=None):
    T = proj.shape[0]
    half = SSD_CONV // 2

    def body(u_ref, w_ref, b_ref, c0_ref, c1_ref, cs_ref, du_ref, dw_ref, db_ref):
        t = pl.program_id(0)
        u = u_ref[...]
        pre = jnp.broadcast_to(b_ref[...], u.shape)
        for k in range(SSD_CONV):
            pre = pre + w_ref[k:k + 1, :] * _shift_rows(u, k - half, n_ctx)
        sg = jax.nn.sigmoid(pre)
        ct = c0_ref[...] + c1_ref[...] + jnp.where(t % 4 < 2, cs_ref[...], 0.0)
        dpre = ct * (sg * (1.0 + pre * (1.0 - sg)))
        du = jnp.zeros_like(u)
        for k in range(SSD_CONV):
            du = du + w_ref[k:k + 1, :] * _shift_rows(dpre, half - k, n_ctx)
            dw_ref[k:k + 1, :] = jnp.sum(dpre * _shift_rows(u, k - half, n_ctx), axis=0, keepdims=True)
        du_ref[...] = du.astype(du_ref.dtype)
        db_ref[...] = jnp.sum(dpre, axis=0, keepdims=True)

    col = lambda t: (0, t)
    skip_col = lambda t: (0, (t // 4) * 2 + jnp.minimum(t % 4, 1))
    res, xres = hosted_call(
        body, ex, [proj, conv_w, conv_b, d_act2, d_act2, d_skip], name=name,
        out_shape=[jax.ShapeDtypeStruct((T, width), ACT_DTYPE), jax.ShapeDtypeStruct((SSD_CONV, width), F32),
                   jax.ShapeDtypeStruct((1, width), F32)],
        grid=(width // CONV_TILE,),
        in_specs=[pl.BlockSpec((T, CONV_TILE), col), pl.BlockSpec((SSD_CONV, CONV_TILE), col),
                  pl.BlockSpec((1, CONV_TILE), col), pl.BlockSpec((T, CONV_TILE), col),
                  pl.BlockSpec((T, CONV_TILE), lambda t: (1, t)), pl.BlockSpec((T, CONV_TILE), skip_col)],
        out_specs=[pl.BlockSpec((T, CONV_TILE), col), pl.BlockSpec((SSD_CONV, CONV_TILE), col),
                   pl.BlockSpec((1, CONV_TILE), col)])
    return res[0], res[1], res[2], xres


@jax.custom_vjp
def _cumsum_mat(tri, tri_t, a):
    return jnp.dot(tri, a, precision=lax.Precision.HIGHEST, preferred_element_type=F32)


def _cumsum_fwd(tri, tri_t, a):
    return _cumsum_mat(tri, tri_t, a), (tri, tri_t)


def _cumsum_bwd(res, g):
    tri, tri_t = res
    return (jnp.zeros_like(tri), jnp.zeros_like(tri_t),
            jnp.dot(tri_t, g, precision=lax.Precision.HIGHEST, preferred_element_type=F32))


_cumsum_mat.defvjp(_cumsum_fwd, _cumsum_bwd)


def _ssd_dt(dtraw, dt_bias, a_log, tri, tri_t):
    dt_all = jax.nn.softplus(dtraw + dt_bias)
    a_all = dt_all * (-jnp.exp(a_log))
    return dt_all, a_all, _cumsum_mat(tri, tri_t, a_all)


def _ssd_chunk(xs, bm, cm, dt_all, a_all, s_all, s_in, mask, idx0):
    Q = xs.shape[0]
    hpg = xs.shape[1] // SSD_HEADDIM
    lane = lax.broadcasted_iota(jnp.int32, dt_all.shape, 1)
    head = lax.broadcasted_iota(jnp.int32, xs.shape, 1) // SSD_HEADDIM
    head1 = lax.broadcasted_iota(jnp.int32, (1, xs.shape[1]), 1) // SSD_HEADDIM

    def pick(v, r):
        return jnp.sum(jnp.where(lane == idx0 + r, v, 0.0), axis=1, keepdims=True)

    def expand(cols, hd):
        out = cols[hpg - 1]
        for r in range(hpg - 2, -1, -1):
            out = jnp.where(hd == r, cols[r], out)
        return out

    dt_r = [pick(dt_all, r) for r in range(hpg)]
    s_r = [pick(s_all, r) for r in range(hpg)]
    stot_r = [jnp.sum(jnp.where(lane == idx0 + r, a_all, 0.0), keepdims=True).reshape(1, 1) for r in range(hpg)]

    xd = xs * expand([jnp.broadcast_to(c, xs.shape) for c in dt_r], head)
    cb = _mm_nt(cm, bm)
    y = expand([jnp.broadcast_to(jnp.exp(c), xs.shape) for c in s_r], head) * _mm(cm, s_in)
    for r in range(hpg):
        sm = jnp.broadcast_to(s_r[r], (Q, Q))
        decay = jnp.exp(jnp.where(mask, sm - sm.T, NEG))
        y = y + _mm(cb * decay, jnp.where(head == r, xd, 0.0))
    to_end = expand([jnp.broadcast_to(jnp.exp(t - c), xs.shape) for t, c in zip(stot_r, s_r)], head)
    carry = expand([jnp.broadcast_to(jnp.exp(t), (1, xs.shape[1])) for t in stot_r], head1)
    s_out = carry * s_in + _mm_tn(bm, xd * to_end)
    return y, s_out


def _scan_consts():
    q = SSD_CHUNK
    i = np.arange(q)[:, None]
    j = np.arange(q)[None, :]
    fwd = (j <= i).astype(np.float32)
    bwd = (j >= i).astype(np.float32)
    tri = np.stack([fwd, bwd])
    return jnp.asarray(tri), jnp.asarray(np.stack([fwd.T, bwd.T]))


def _chunk_of(d, k, ncc, nc):
    rev = jnp.where(k < ncc, ncc - 1 - k, nc - 1 + ncc - k)
    return jnp.where(d == 0, k, rev)


def ssd_fwd(name, xbc, proj, dt_cb, dt_bias, a_log, n_ctx, ex=None):
    T = xbc.shape[0]
    q, G = SSD_CHUNK, SSD_GROUPS
    nc, ncc = T // q, n_ctx // q
    gw = xbc.shape[1] // G
    xw = gw - 2 * SSD_STATE
    hpg = xw // SSD_HEADDIM
    nh = G * hpg
    tri, tri_t = _scan_consts()

    gs = SSD_GROUPS_PER_STEP

    def body(x_ref, dt_ref, bias_ref, alog_ref, tri_ref, trit_ref, y_ref, sin_ref, state):
        d, gb, k = pl.program_id(0), pl.program_id(1), pl.program_id(2)

        @pl.when(k == 0)
        def _():
            state[...] = jnp.zeros_like(state)

        tri_v = tri_ref[...]
        dt_all, a_all, s_all = _ssd_dt(dt_ref[...], bias_ref[...], alog_ref[...], tri_v, trit_ref[...])
        for j in range(gs):
            o = j * gw
            s_in = state[j]
            sin_ref[j] = s_in
            y, s_out = _ssd_chunk(
                x_ref[:, o:o + xw], x_ref[:, o + xw:o + xw + SSD_STATE], x_ref[:, o + xw + SSD_STATE:o + gw],
                dt_all, a_all, s_all, s_in, tri_v > 0.5, d * nh + (gb * gs + j) * hpg)
            y_ref[:, j * xw:(j + 1) * xw] = y
            state[j] = s_out

    ch = lambda d, g, k: _chunk_of(d, k, ncc, nc)
    res, xres = hosted_call(
        body, ex, [xbc, proj, dt_bias, a_log, tri, tri_t], name=name,
        out_shape=[jax.ShapeDtypeStruct((2 * T, G * xw), F32),
                   jax.ShapeDtypeStruct((2, nc, G, SSD_STATE, xw), F32)],
        grid=(2, G // gs, nc),
        in_specs=[pl.BlockSpec((q, gs * gw), lambda d, g, k: (ch(d, g, k), g)),
                  pl.BlockSpec((q, 128), lambda d, g, k: (ch(d, g, k), dt_cb)),
                  pl.BlockSpec((1, 128), lambda d, g, k: (0, 0)),
                  pl.BlockSpec((1, 128), lambda d, g, k: (0, 0)),
                  pl.BlockSpec((None, q, q), lambda d, g, k: (d, 0, 0)),
                  pl.BlockSpec((None, q, q), lambda d, g, k: (d, 0, 0))],
        out_specs=[pl.BlockSpec((q, gs * xw), lambda d, g, k: (d * nc + ch(d, g, k), g)),
                   pl.BlockSpec((None, None, gs, SSD_STATE, xw), lambda d, g, k: (d, k, g, 0, 0))],
        scratch_shapes=[pltpu.VMEM((gs, SSD_STATE, xw), F32)])
    return res[0], res[1], xres


def ssd_bwd(name, xbc, proj, dt_cb, dt_bias, a_log, states, dy, n_ctx, ex=None):
    T = xbc.shape[0]
    q, G = SSD_CHUNK, SSD_GROUPS
    nc, ncc = T // q, n_ctx // q
    gw = xbc.shape[1] // G
    xw = gw - 2 * SSD_STATE
    hpg = xw // SSD_HEADDIM
    nh = G * hpg
    tri, tri_t = _scan_consts()

    gs = SSD_GROUPS_PER_STEP

    def body(x_ref, dt_ref, bias_ref, alog_ref, tri_ref, trit_ref, sin_ref, dy_ref,
             dx_ref, ddt_ref, dbias_ref, dalog_ref, dstate):
        d, gb, k = pl.program_id(0), pl.program_id(1), pl.program_id(2)
        first = (d == 0) & (gb == 0) & (k == 0)

        @pl.when(first)
        def _():
            ddt_ref[...] = jnp.zeros_like(ddt_ref)
            dbias_ref[...] = jnp.zeros_like(dbias_ref)
            dalog_ref[...] = jnp.zeros_like(dalog_ref)

        @pl.when(k == 0)
        def _():
            dstate[...] = jnp.zeros_like(dstate)

        tri_v, trit_v = tri_ref[...], trit_ref[...]
        mask = tri_v > 0.5

        def fn(dtraw, bias, alog, *per_group):
            dt_all, a_all, s_all = _ssd_dt(dtraw, bias, alog, tri_v, trit_v)
            ys, s_outs = [], []
            for j in range(gs):
                xs, bm, cm, s_in = per_group[4 * j:4 * j + 4]
                y, s_out = _ssd_chunk(xs, bm, cm, dt_all, a_all, s_all, s_in, mask, d * nh + (gb * gs + j) * hpg)
                ys.append(y)
                s_outs.append(s_out)
            return ys, s_outs

        per_group = []
        for j in range(gs):
            o = j * gw
            per_group += [x_ref[:, o:o + xw], x_ref[:, o + xw:o + xw + SSD_STATE], x_ref[:, o + xw + SSD_STATE:o + gw],
                          sin_ref[j]]
        _, vjp = jax.vjp(fn, dt_ref[...], bias_ref[...], alog_ref[...], *per_group)
        cts = vjp(([dy_ref[:, j * xw:(j + 1) * xw] for j in range(gs)], [dstate[j] for j in range(gs)]))
        ddt, dbias, dalog = cts[:3]
        for j in range(gs):
            o = j * gw
            dxs, dbm, dcm, ds_in = cts[3 + 4 * j:7 + 4 * j]
            dx_ref[:, o:o + xw] = dxs
            dx_ref[:, o + xw:o + xw + SSD_STATE] = dbm
            dx_ref[:, o + xw + SSD_STATE:o + gw] = dcm
            dstate[j] = ds_in
        row0 = pl.multiple_of(_chunk_of(d, nc - 1 - k, ncc, nc) * q, q)
        ddt_ref[pl.ds(row0, q), :] += ddt
        dbias_ref[...] += dbias
        dalog_ref[...] += dalog

    ch = lambda d, g, k: _chunk_of(d, nc - 1 - k, ncc, nc)
    res, xres = hosted_call(
        body, ex, [xbc, proj, dt_bias, a_log, tri, tri_t, states, dy], name=name,
        out_shape=[jax.ShapeDtypeStruct((2 * T, G * gw), F32), jax.ShapeDtypeStruct((T, 128), F32),
                   jax.ShapeDtypeStruct((1, 128), F32), jax.ShapeDtypeStruct((1, 128), F32)],
        grid=(2, G // gs, nc),
        in_specs=[pl.BlockSpec((q, gs * gw), lambda d, g, k: (ch(d, g, k), g)),
                  pl.BlockSpec((q, 128), lambda d, g, k: (ch(d, g, k), dt_cb)),
                  pl.BlockSpec((1, 128), lambda d, g, k: (0, 0)),
                  pl.BlockSpec((1, 128), lambda d, g, k: (0, 0)),
                  pl.BlockSpec((None, q, q), lambda d, g, k: (d, 0, 0)),
                  pl.BlockSpec((None, q, q), lambda d, g, k: (d, 0, 0)),
                  pl.BlockSpec((None, None, gs, SSD_STATE, xw), lambda d, g, k: (d, nc - 1 - k, g, 0, 0)),
                  pl.BlockSpec((q, gs * xw), lambda d, g, k: (ch(d, g, k), g))],
        out_specs=[pl.BlockSpec((q, gs * gw), lambda d, g, k: (d * nc + ch(d, g, k), g)),
                   pl.BlockSpec((T, 128), lambda d, g, k: (0, 0)),
                   pl.BlockSpec((1, 128), lambda d, g, k: (0, 0)),
                   pl.BlockSpec((1, 128), lambda d, g, k: (0, 0))],
        scratch_shapes=[pltpu.VMEM((gs, SSD_STATE, xw), F32)])
    return res[0], res[1], res[2], res[3], xres


def _perm_xbc(a):
    G = SSD_GROUPS
    n = a.shape[-1]
    gn = G * SSD_STATE
    di = n - 2 * gn
    lead = a.shape[:-1]
    xs = a[..., :di].reshape(lead + (G, di // G))
    bm = a[..., di:di + gn].reshape(lead + (G, SSD_STATE))
    cm = a[..., di + gn:].reshape(lead + (G, SSD_STATE))
    return jnp.concatenate([xs, bm, cm], axis=-1).reshape(lead + (n,))


def _unperm_xbc(a):
    G = SSD_GROUPS
    n = a.shape[-1]
    gn = G * SSD_STATE
    di = n - 2 * gn
    lead = a.shape[:-1]
    r = a.reshape(lead + (G, n // G))
    xw = di // G
    return jnp.concatenate([r[..., :xw].reshape(lead + (di,)), r[..., xw:xw + SSD_STATE].reshape(lead + (gn,)),
                            r[..., xw + SSD_STATE:].reshape(lead + (gn,))], axis=-1)


def _pool_consts(tm, n_ctx):
    assert n_ctx == tm and tm % GRID_W == 0
    mats, cnts = [], []
    for seq in (n_ctx, GRID_W):
        t = np.arange(tm)
        tt = t % seq
        base = t - tt
        ms, cs = [], []
        for k in POOL_WINDOWS:
            lo = np.clip(tt - k // 2, 0, seq) + base
            hi = np.clip(tt + k // 2, 0, seq) + base
            m = ((t[None, :] >= lo[:, None]) & (t[None, :] < hi[:, None])).astype(np.float32)
            ms.append(m)
            cs.append((1.0 / (hi - lo).astype(np.float32))[:, None])
        mats.append(np.stack(ms))
        cnts.append(np.stack(cs))
    m = np.stack(mats)
    return jnp.asarray(m), jnp.asarray(np.swapaxes(m, -1, -2)), jnp.asarray(np.stack(cnts).astype(np.float32))


def _prep_layer_weights(w_ada, b_ada, g_mix, w_in, conv_w, conv_b, dt_bias, a_log, d_skip, ssd_norm_w, w_ssd_out,
                        pool_w, pool_scale, w_pool_out, w_out, g_ffn, w_gate_up, w_down):
    D = w_in.shape[0]
    di = ssd_norm_w.shape[0]
    xbc = conv_w.shape[1]
    nh2 = dt_bias.size
    pw = pool_scale.shape[0]
    o = 0
    wz = w_in[:, o:o + di]; o += di
    wx = w_in[:, o:o + xbc]; o += xbc
    wdt = w_in[:, o:o + nh2]; o += nh2
    wp = w_in[:, o:o + pw]; o += pw
    wg = w_in[:, o:]
    w1 = jnp.concatenate([_perm_xbc(wx), wz, wg, wp, wdt, jnp.zeros((D, DT_PAD - nh2), w_in.dtype)], axis=1)
    pad128 = lambda v: jnp.concatenate([v.reshape(1, -1), jnp.zeros((1, 128 - v.size), F32)], axis=1)
    return dict(
        w_ada=w_ada, b_ada=b_ada.reshape(1, -1), g_mix=g_mix.reshape(1, -1), w1=w1,
        conv_w=_perm_xbc(conv_w), conv_b=_perm_xbc(conv_b.reshape(1, -1)),
        dt_bias=pad128(dt_bias), a_log=pad128(a_log),
        dskip=jnp.repeat(d_skip[0] + d_skip[1], SSD_HEADDIM).reshape(1, -1),
        ssd_norm_w=ssd_norm_w.reshape(1, -1), w_ssd_out=w_ssd_out, pool_w=pool_w,
        pool_scale=pool_scale.reshape(1, -1), w_pool_out=w_pool_out, w_out=w_out, g_ffn=g_ffn.reshape(1, -1),
        w_gate_up=w_gate_up, w_down=w_down)


def _unprep_layer_grads(g, dims):
    di, xbc, nh2, pw = dims
    dxbc, dz, dgs, dgp, dp, ddt = g["w1"]
    dg = jnp.concatenate([dgs, dgp], axis=1)
    ddt = ddt[:, :nh2]
    nh = nh2 // 2
    dsk = g["dskip"].reshape(nh, SSD_HEADDIM).sum(axis=1)
    return dict(
        w_ada=g["w_ada"], b_ada=g["b_ada"].reshape(-1), g_mix=g["g_mix"].reshape(-1),
        w_in=jnp.concatenate([dz, _unperm_xbc(dxbc), ddt, dp, dg], axis=1),
        conv_w=_unperm_xbc(g["conv_w"]), conv_b=_unperm_xbc(g["conv_b"]).reshape(-1),
        dt_bias=g["dt_bias"][0, :nh2].reshape(2, nh), a_log=g["a_log"][0, :nh2].reshape(2, nh),
        d_skip=jnp.stack([dsk, dsk]), ssd_norm_w=g["ssd_norm_w"].reshape(-1), w_ssd_out=g["w_ssd_out"],
        pool_w=g["pool_w"], pool_scale=g["pool_scale"].reshape(-1), w_pool_out=g["w_pool_out"], w_out=g["w_out"],
        g_ffn=g["g_ffn"].reshape(-1), w_gate_up=g["w_gate_up"], w_down=g["w_down"])


COND_ROWS = 16


def _split_mods(m):
    d = m.shape[1] // 6
    return [m[:2, k * d:(k + 1) * d].reshape(2, 1, d) for k in range(6)]


def _hosted(hosts, box, key):
    fn = (hosts or {}).get(key)
    return fn(box) if fn else None


def _layer_fwd(l, x, cond_s, w, rows, n_ctx, pc, hosts=None, box=None):
    T, D = x.shape
    nt, nct, tm = rows.nt, rows.nct, rows.tm
    n = lambda s: f"l{l}_{s}"
    crow = Rows(1, 0, COND_ROWS)
    mraw = matmul_nn(n("ada_mm"), cond_s, w["w_ada"])
    (m,) = stage_fwd(n("ada_bias"), f_bias, crow, [crow.row(mraw, mraw.shape[1]), crow.vec(w["b_ada"])],
                     [(mraw.shape[1], F32, False)])
    sh1, sc1, ga1, sh2, sc2, ga2 = _split_mods(m)

    (h1,) = stage_fwd(n("norm1"), f_norm_mod, rows,
                      [rows.row(x, D), rows.vec(w["g_mix"]), rows.segvec(sh1), rows.segvec(sc1)],
                      [(D, ACT_DTYPE, False)])
    ex = _hosted(hosts, box, "in_mm")
    proj = matmul_nn(n("in_mm"), h1, w["w1"], ex=ex)
    if ex is not None:
        proj, box["in_mm"] = proj
    xbc_w = w["conv_w"].shape[1]
    di = w["ssd_norm_w"].shape[1]
    pw = w["pool_scale"].shape[1]
    c_z, c_g, c_p, c_dt = xbc_w, xbc_w + di, xbc_w + di + 2 * pw, xbc_w + di + 3 * pw
    ex = _hosted(hosts, box, "conv")
    xbc, xres = conv_fwd(n("conv"), proj, w["conv_w"], w["conv_b"], n_ctx, xbc_w, ex)
    if ex is not None:
        box["conv"] = xres
    ex = _hosted(hosts, box, "ssd")
    y2, states, xres = ssd_fwd(n("ssd"), xbc, proj, c_dt // 128, w["dt_bias"], w["a_log"], n_ctx, ex)
    if ex is not None:
        box["ssd"] = xres

    G = SSD_GROUPS
    gw = di // G
    r8 = Rows(nt, nct, tm, G)
    gate_args = [r8.row(y2, gw, 0, True), r8.row(y2, gw, 0, True, roff=nt), r8.row(xbc, gw, 0, True, stride=2),
                 r8.row(proj, gw, c_z // gw, True), r8.vec(w["dskip"], True), r8.vec(w["ssd_norm_w"], True)]
    (ynw,) = stage_fwd(n("ssd_gate"), f_ssd_gate, r8, gate_args, [(gw, ACT_DTYPE, True)])
    o_ssd = matmul_nn(n("ssd_out_mm"), ynw, w["w_ssd_out"])

    nw = len(POOL_WINDOWS)
    pg = pw // nw
    r4 = Rows(nt, nct, tm, nw)
    pmat, pmat_t, inv_cnt = pc
    cblk = lambda a: Arg(a, (None, None) + a.shape[2:], lambda j, i, s: (s, j, 0, 0), "const")
    pool_args = [r4.row(proj, pg, c_p // pg, True), cblk(pmat), cblk(pmat_t), cblk(inv_cnt),
                 Arg(w["pool_w"], (None, pg, pg), lambda j, i, s: (j, 0, 0), "acc"), r4.vec(w["pool_scale"], True)]
    (ps,) = stage_fwd(n("pool"), f_pool, r4, pool_args, [(pg, ACT_DTYPE, True)])
    o_pool = matmul_nn(n("pool_out_mm"), ps, w["w_pool_out"])

    merge_args = [rows.row(o_ssd, D), rows.row(o_pool, D), rows.row(proj, pw, c_g // pw), rows.row(proj, pw, c_g // pw + 1)]
    (mg,) = stage_fwd(n("merge"), f_merge, rows, merge_args, [(D, ACT_DTYPE, False)])
    mo = matmul_nn(n("out_mm"), mg, w["w_out"])

    rn_args = [rows.row(x, D), rows.row(mo, D), rows.segvec(ga1), rows.vec(w["g_ffn"]), rows.segvec(sh2), rows.segvec(sc2)]
    x1, h2 = stage_fwd(n("norm2"), f_resid_norm_mod, rows, rn_args, [(D, F32, False), (D, ACT_DTYPE, False)])
    ex = _hosted(hosts, box, "gate_up_mm")
    gu = matmul_nn(n("gate_up_mm"), h2, w["w_gate_up"], ex=ex)
    if ex is not None:
        gu, box["gate_up_mm"] = gu
    fh = gu.shape[1] // 2
    (act,) = stage_fwd(n("swiglu"), f_swiglu, rows, [rows.row(gu, 2 * fh)], [(fh, ACT_DTYPE, False)])
    dn = matmul_nn(n("down_mm"), act, w["w_down"])
    res_args = [rows.row(x1, D), rows.row(dn, D), rows.segvec(ga2)]
    (x2,) = stage_fwd(n("resid2"), f_resid, rows, res_args, [(D, F32, False)])
    saved = dict(x=x, mraw=mraw, mods=(sh1, sc1, ga1, sh2, sc2, ga2), h1=h1, proj=proj, xbc=xbc, y2=y2, states=states,
                 ynw=ynw, o_ssd=o_ssd, ps=ps, o_pool=o_pool, mg=mg, mo=mo, x1=x1, h2=h2, gu=gu, act=act, dn=dn,
                 cols=(c_z, c_g, c_p, c_dt))
    return x2, saved


def f_norm_mod_keep(x, g, sh, sc):
    return f_norm_mod(x, g, sh, sc)[0], x


def _layer_bwd(l, dx2, cond_s, w, s, rows, n_ctx, pc, hosts=None, box=None):
    T, D = dx2.shape
    nt, nct, tm = rows.nt, rows.nct, rows.tm
    n = lambda t: f"l{l}_{t}_bwd"
    sh1, sc1, ga1, sh2, sc2, ga2 = s["mods"]
    c_z, c_g, c_p, c_dt = s["cols"]
    x, proj, xbc, y2, gu = s["x"], s["proj"], s["xbc"], s["y2"], s["gu"]
    g = {}
    if box is not None:
        box["g"] = g

    res_args = [rows.row(s["x1"], D), rows.row(s["dn"], D), rows.segvec(ga2)]
    res_args[0].kind = "const"
    dx1 = dx2
    ddn, dga2 = stage_bwd(n("resid2"), f_resid, rows, res_args, [rows.row(dx2, D)], [ACT_DTYPE])
    ex = _hosted(hosts, box, "down_dx")
    dact = matmul_nt(n("down_dx"), ddn, w["w_down"], ex=ex)
    if ex is not None:
        dact, box["down_dx"] = dact
    g["w_down"] = matmul_tn(n("down_dw"), s["act"], ddn)
    fh = gu.shape[1] // 2
    (dgu,) = stage_bwd(n("swiglu"), f_swiglu, rows, [rows.row(gu, 2 * fh)], [rows.row(dact, fh)], [ACT_DTYPE])
    dh2 = matmul_nt(n("gate_up_dx"), dgu, w["w_gate_up"])
    g["w_gate_up"] = matmul_tn(n("gate_up_dw"), s["h2"], dgu)

    rn_args = [rows.row(x, D), rows.row(s["mo"], D), rows.segvec(ga1), rows.vec(w["g_ffn"]), rows.segvec(sh2), rows.segvec(sc2)]
    dxr, dmo, dga1, g["g_ffn"], dsh2, dsc2 = stage_bwd(
        n("norm2"), f_resid_norm_mod, rows, rn_args, [rows.row(dx1, D), rows.row(dh2, D)], [F32, ACT_DTYPE])
    dmg = matmul_nt(n("out_dx"), dmo, w["w_out"])
    g["w_out"] = matmul_tn(n("out_dw"), s["mg"], dmo)

    pw = w["pool_scale"].shape[1]
    merge_args = [rows.row(s["o_ssd"], D), rows.row(s["o_pool"], D), rows.row(proj, pw, c_g // pw), rows.row(proj, pw, c_g // pw + 1)]
    do_ssd, do_pool, dgl_s, dgl_p = stage_bwd(n("merge"), f_merge, rows, merge_args, [rows.row(dmg, D)], [ACT_DTYPE] * 4)
    dps = matmul_nt(n("pool_out_dx"), do_pool, w["w_pool_out"])
    g["w_pool_out"] = matmul_tn(n("pool_out_dw"), s["ps"], do_pool)

    nw = len(POOL_WINDOWS)
    pg = pw // nw
    r4 = Rows(nt, nct, tm, nw)
    pmat, pmat_t, inv_cnt = pc
    cblk = lambda a: Arg(a, (None, None) + a.shape[2:], lambda j, i, s_: (s_, j, 0, 0), "const")
    pool_args = [r4.row(proj, pg, c_p // pg, True), cblk(pmat), cblk(pmat_t), cblk(inv_cnt),
                 Arg(w["pool_w"], (None, pg, pg), lambda j, i, s_: (j, 0, 0), "acc"), r4.vec(w["pool_scale"], True)]
    du_pool, g["pool_w"], g["pool_scale"] = stage_bwd(n("pool"), f_pool, r4, pool_args, [r4.row(dps, pg, 0, True)], [ACT_DTYPE])

    dynw = matmul_nt(n("ssd_out_dx"), do_ssd, w["w_ssd_out"])
    g["w_ssd_out"] = matmul_tn(n("ssd_out_dw"), s["ynw"], do_ssd)
    G = SSD_GROUPS
    di = w["ssd_norm_w"].shape[1]
    gw = di // G
    r8 = Rows(nt, nct, tm, G)
    gate_args = [r8.row(y2, gw, 0, True), r8.row(y2, gw, 0, True, roff=nt), r8.row(xbc, gw, 0, True, stride=2),
                 r8.row(proj, gw, c_z // gw, True), r8.vec(w["dskip"], True), r8.vec(w["ssd_norm_w"], True)]
    gate_args[1].kind = "const"
    dy, dxs_skip, dz, g["dskip"], g["ssd_norm_w"] = stage_bwd(
        n("ssd_gate"), f_ssd_gate, r8, gate_args, [r8.row(dynw, gw, 0, True)], [F32, F32, ACT_DTYPE])

    ex = _hosted(hosts, box, "ssd")
    dxbc2, ddt, g["dt_bias"], g["a_log"], xres = ssd_bwd(n("ssd"), xbc, proj, c_dt // 128, w["dt_bias"], w["a_log"],
                                                         s["states"], dy, n_ctx, ex)
    if ex is not None:
        box["ssd"] = xres
    xbc_w = xbc.shape[1]
    ex = _hosted(hosts, box, "conv")
    dxbc_raw, g["conv_w"], g["conv_b"], xres = conv_bwd(n("conv"), proj, w["conv_w"], w["conv_b"], dxbc2, dxs_skip,
                                                         n_ctx, xbc_w, ex)
    if ex is not None:
        box["conv"] = xres
    pieces = [dxbc_raw, dz, dgl_s, dgl_p, du_pool, ddt]
    offsets = [0, c_z, c_g, c_g + pw, c_p, c_dt]
    ex = _hosted(hosts, box, "in_dx")
    dh1 = matmul_nt(n("in_dx"), pieces, w["w1"], ex=ex, offsets=offsets)
    if ex is not None:
        dh1, box["in_dx"] = dh1
    ex = _hosted(hosts, box, "in_dw")
    first = matmul_tn(n("in_dw0"), s["h1"], pieces[0], ex=ex)
    if ex is not None:
        first, box["in_dw"] = first
    g["w1"] = [first] + [matmul_tn(n(f"in_dw{k}"), s["h1"], p) for k, p in enumerate(pieces) if k]

    n1_args = [rows.row(x, D), rows.vec(w["g_mix"]), rows.segvec(sh1), rows.segvec(sc1)]
    dx, g["g_mix"], dsh1, dsc1 = stage_bwd(n("norm1"), f_norm_mod_keep, rows, n1_args,
                                           [rows.row(dh1, D), rows.row(dxr, D)], [F32])

    dm = jnp.concatenate([v.reshape(2, D) for v in (dsh1, dsc1, dga1, dsh2, dsc2, dga2)], axis=1)
    dm = jnp.concatenate([dm, jnp.zeros((COND_ROWS - 2, dm.shape[1]), F32)], axis=0)
    crow = Rows(1, 0, COND_ROWS)
    dmraw, g["b_ada"] = stage_bwd(n("ada_bias"), f_bias, crow, [crow.row(s["mraw"], dm.shape[1]), crow.vec(w["b_ada"])],
                                  [crow.row(dm, dm.shape[1])], [ACT_DTYPE])
    dcs = matmul_nt(n("ada_dx"), dmraw, w["w_ada"])
    g["w_ada"] = matmul_tn(n("ada_dw"), cond_s, dmraw)
    return dx, dcs, g


def local_step(x, ctx, c, c_ctx, target, layer_w_fn, n_layers, g_final, fwd_hosts=None, bwd_hosts=None):
    L, D = x.shape
    n_ctx = ctx.shape[0]
    tm = ROW_TILE
    T = L + n_ctx
    rows = Rows(T // tm, n_ctx // tm, tm)
    pc = _pool_consts(tm, n_ctx)
    xa = jnp.concatenate([ctx, x], axis=0)
    cond = jnp.concatenate([c_ctx.reshape(1, D), c.reshape(1, D), jnp.zeros((COND_ROWS - 2, D), F32)], axis=0)
    crow = Rows(1, 0, COND_ROWS)
    (cond_s,) = stage_fwd("cond_silu", f_silu, crow, [crow.row(cond, D)], [(D, ACT_DTYPE, False)])

    saved, layer_w = [], []
    for l in range(n_layers):
        layer_w.append(layer_w_fn(l))
        box = {}
        xa, s = _layer_fwd(l, xa, cond_s, layer_w[l], rows, n_ctx, pc, fwd_hosts(l, box) if fwd_hosts else None, box)
        saved.append(s)

    rl = Rows(L // tm, 0, tm)
    gf = g_final.reshape(1, D)
    tgt = rl.row(target, D)
    tgt.kind = "const"
    loss_args = [rl.row(xa, D, roff=n_ctx // tm), tgt, rl.vec(gf)]
    (loss_rows,) = stage_fwd("loss", f_loss, rl, loss_args, [(1, F32, False)])
    ones = jnp.ones((L, 1), F32)
    dx_lat, dgf = stage_bwd("loss_bwd", f_loss, rl, loss_args, [rl.row(ones, 1)], [F32])
    loss = jnp.sum(loss_rows)
    dx = jnp.concatenate([jnp.zeros((n_ctx, D), F32), dx_lat], axis=0)

    grads = [None] * n_layers
    dcs = jnp.zeros((COND_ROWS, D), F32)
    for l in reversed(range(n_layers)):
        box = {}
        hosts = bwd_hosts(l, grads, box) if bwd_hosts else None
        dx, dcs_l, grads[l] = _layer_bwd(l, dx, cond_s, layer_w[l], saved[l], rows, n_ctx, pc, hosts, box)
        dcs = dcs + dcs_l
    (dcond,) = stage_bwd("cond_silu_bwd", f_silu, crow, [crow.row(cond, D)], [crow.row(dcs, D)], [F32])
    return loss, dx[n_ctx:], grads, dcond[0], dgf


def gather_chips(halves, conv=None):
    n = len(halves)
    ops = list(halves) + ([conv] if conv is not None else [])

    def copies(ins, outs, pos):
        c, me = pos[2], _chip_index(pos)
        pairs = [(s.at[c], o.at[me, c]) for s, o in zip(ins[:n], outs[:n])]
        pairs += [(s, o.at[me]) for s, o in zip(ins[n:], outs[n:])]
        return pairs, [(s, d, _flip(pos, rel)) for rel in PLANE for s, d in pairs]

    shapes = [jax.ShapeDtypeStruct((4,) + s.shape, s.dtype) for s in ops]
    return Exchange(copies, 3 * len(ops), len(ops), ops, shapes)


def gather_pair(gathered):
    n = len(gathered)

    def copies(ins, outs, pos):
        c = pos[2]
        return [], [(s.at[b, c], o.at[b, c], _flip(pos, PAIR[0])) for s, o in zip(ins, outs) for b in range(4)]

    shapes = [jax.ShapeDtypeStruct(g.shape, g.dtype) for g in gathered]
    return Exchange(copies, 4 * n, 0, gathered, shapes, aliases={k: k for k in range(n)})


def swap_halves(grads):
    n = len(grads)

    def copies(ins, outs, pos):
        c = pos[2]
        return [], [(g.at[b, 1 - c], o.at[b], _flip(pos, PAIR[0])) for g, o in zip(ins, outs) for b in range(4)]

    shapes = [jax.ShapeDtypeStruct((g.shape[0],) + g.shape[2:], g.dtype) for g in grads]
    return Exchange(copies, 4 * n, 0, grads, shapes)


def scatter_chips(sums):
    n = len(sums)

    def copies(ins, outs, pos):
        me = _chip_index(pos)
        local = [(p.at[me], o.at[me]) for p, o in zip(ins, outs)]
        remote = []
        for rel in PLANE:
            peer = _flip(pos, rel)
            remote += [(p.at[_chip_index(peer)], o.at[me], peer) for p, o in zip(ins, outs)]
        return local, remote

    shapes = [jax.ShapeDtypeStruct(p.shape, p.dtype) for p in sums]
    return Exchange(copies, 3 * n, n, sums, shapes)


def share_halves(finals):
    n = len(finals)

    def copies(ins, outs, pos):
        c = pos[2]
        return [], [(f.at[c], o.at[c], _flip(pos, PAIR[0])) for f, o in zip(ins, outs)]

    shapes = [jax.ShapeDtypeStruct(f.shape, f.dtype) for f in finals]
    return Exchange(copies, n, 0, finals, shapes, aliases={k: k for k in range(n)})


def gather_everyone(vec):
    def copies(ins, outs, pos):
        me = _device_index(pos)
        (v,), (o,) = ins, outs
        return [(v, o.at[me])], [(v, o.at[me], _flip(pos, rel)) for rel in EVERYONE]

    return Exchange(copies, len(EVERYONE), 1, [vec], [jax.ShapeDtypeStruct((8,) + vec.shape, vec.dtype)])


def _row_tile(rows, cols, n_bufs):
    cap = VMEM_LIMIT_BYTES // 4 // (2 * n_bufs * cols * 4)
    for t in (1024, 512, 256, 128, 64, 32, 16, 8):
        if t <= cap and rows % t == 0:
            return t
    return rows


WIRE_DTYPE = jnp.bfloat16


def add_own_half(name, grads, recv, c):
    nb, _, R, C = grads.shape
    tr = _row_tile(R, C, 3)

    def body(c_ref, g_ref, r_ref, o_ref):
        o_ref[...] = (g_ref[...] + r_ref[...]).astype(o_ref.dtype)

    spec = pl.BlockSpec((None, tr, C), lambda b, i, c_ref: (b, i, 0))
    return _pcall(
        body, name=name, out_shape=jax.ShapeDtypeStruct(recv.shape, WIRE_DTYPE),
        grid_spec=pltpu.PrefetchScalarGridSpec(
            num_scalar_prefetch=1, grid=(nb, R // tr),
            in_specs=[pl.BlockSpec((None, None, tr, C), lambda b, i, c_ref: (b, c_ref[0], i, 0)), spec],
            out_specs=spec),
        compiler_params=_params("parallel", "parallel"),
    )(c, grads, recv)


def sum_slots(name, a, c=None):
    n, R, C = a.shape
    tr = _row_tile(R, C, n + 1)

    def body(*refs):
        a_ref, o_ref = refs[-2:]
        acc = a_ref[0].astype(F32)
        for k in range(1, n):
            acc = acc + a_ref[k].astype(F32)
        o_ref[...] = acc

    if c is None:
        return _pcall(
            body, name=name, out_shape=jax.ShapeDtypeStruct((R, C), F32), grid=(R // tr,),
            in_specs=[pl.BlockSpec((n, tr, C), lambda i: (0, i, 0))], out_specs=pl.BlockSpec((tr, C), lambda i: (i, 0)),
            compiler_params=_params("parallel"),
        )(a)
    return _pcall(
        body, name=name, out_shape=jax.ShapeDtypeStruct((2, R, C), F32),
        grid_spec=pltpu.PrefetchScalarGridSpec(
            num_scalar_prefetch=1, grid=(R // tr,),
            in_specs=[pl.BlockSpec((n, tr, C), lambda i, c_ref: (0, i, 0))],
            out_specs=pl.BlockSpec((None, tr, C), lambda i, c_ref: (c_ref[0], i, 0))),
        compiler_params=_params("parallel"),
    )(c, a)


def adamw(name, w, g_layers, m, v):
    nl, R, C = w.shape
    assert len(g_layers) == nl
    tr = _row_tile(R, C, 8 + nl)
    nr = R // tr

    def body(*refs):
        w_ref, m_ref, v_ref = refs[:3]
        g_refs = refs[3:3 + nl]
        go_ref, d_ref, nm_ref, nv_ref = refs[3 + nl:]
        l = pl.program_id(0)
        gr = g_refs[0][...]
        for k in range(1, nl):
            gr = jnp.where(l == k, g_refs[k][...], gr)
        nm = ADAM_B1 * m_ref[...] + (1.0 - ADAM_B1) * gr
        nv = ADAM_B2 * v_ref[...] + (1.0 - ADAM_B2) * jnp.square(gr)
        m_hat = nm / (1.0 - ADAM_B1 ** ADAM_STEP)
        v_hat = nv / (1.0 - ADAM_B2 ** ADAM_STEP)
        d_ref[...] = -ADAM_LR * (m_hat / (jnp.sqrt(v_hat) + ADAM_EPS) + ADAM_WD * w_ref[...])
        go_ref[...] = gr
        nm_ref[...] = nm
        nv_ref[...] = nv

    spec = pl.BlockSpec((None, tr, C), lambda l, i: (l, i, 0))
    g_specs = [pl.BlockSpec((tr, C), (lambda l, i, k=k: (jnp.where(l == k, i, jnp.where(l < k, 0, nr - 1)), 0)))
               for k in range(nl)]
    return _pcall(
        body, name=name, out_shape=[jax.ShapeDtypeStruct((nl, R, C), F32)] * 4, grid=(nl, nr),
        in_specs=[spec] * 3 + g_specs, out_specs=[spec] * 4, compiler_params=_params("arbitrary", "arbitrary"),
    )(w, m, v, *g_layers)


BIG = ("w_ada", "w_in", "w_ssd_out", "pool_w", "w_pool_out", "w_out", "w_gate_up", "w_down")
COL_SHARDED = ("w_ada", "w_in", "w_gate_up")
FIRST_USED = ("w_ada", "w_in")
LATER_USED = tuple(k for k in BIG if k not in FIRST_USED)
READY_LAST = FIRST_USED
READY_EARLY = LATER_USED
SMALL = ("c_ctx", "b_ada", "g_mix", "conv_w", "conv_b", "dt_bias", "a_log", "d_skip", "ssd_norm_w", "pool_scale",
         "g_ffn", "g_final")
WEIGHTS = ("c_ctx", "w_ada", "b_ada", "g_mix", "w_in", "conv_w", "conv_b", "dt_bias", "a_log", "d_skip", "ssd_norm_w",
           "w_ssd_out", "pool_w", "pool_scale", "w_pool_out", "w_out", "g_ffn", "w_gate_up", "w_down", "g_final")
LAYER_KEYS = ("w_ada", "b_ada", "g_mix", "w_in", "conv_w", "conv_b", "dt_bias", "a_log", "d_skip", "ssd_norm_w",
              "w_ssd_out", "pool_w", "pool_scale", "w_pool_out", "w_out", "g_ffn", "w_gate_up", "w_down")


def _shard2d(name, a):
    if name == "pool_w":
        return a.reshape(a.shape[0], a.shape[1] * a.shape[2], a.shape[3])
    return a


def _full_from_blocks(name, a):
    nb, R, C = a.shape
    if name in COL_SHARDED:
        return jnp.transpose(a, (1, 0, 2)).reshape(R, nb * C)
    if name == "pool_w":
        nw = len(POOL_WINDOWS)
        return jnp.transpose(a.reshape(nb, nw, R // nw, C), (1, 0, 2, 3)).reshape(nw, nb * R // nw, C)
    return a.reshape(nb * R, C)


def _blocks_from_full(name, g):
    nb = 4
    if name in COL_SHARDED:
        K, N = g.shape
        return jnp.transpose(g.reshape(K, nb, N // nb), (1, 0, 2))
    if name == "pool_w":
        nw, r, C = g.shape
        return jnp.transpose(g.reshape(nw, nb, r // nb, C), (1, 0, 2, 3)).reshape(nb, nw * r // nb, C)
    return g.reshape(nb, g.shape[0] // nb, g.shape[1])


def _pack(arrs, rows):
    flat = jnp.concatenate([a.reshape(-1).astype(F32) for a in arrs])
    return jnp.concatenate([flat, jnp.zeros((rows * 128 - flat.size,), F32)]).reshape(rows, 128)


def _unpack(vec, shapes):
    flat = vec.reshape(-1)
    out, o = [], 0
    for s in shapes:
        n = int(np.prod(s))
        out.append(flat[o:o + n].reshape(s))
        o += n
    return out


def _rows_for(shapes):
    n = sum(int(np.prod(s)) for s in shapes)
    return -(-n // (8 * 128)) * 8


def kernel(x, c, ctx, c_ctx, w_ada, b_ada, g_mix, w_in, conv_w, conv_b, dt_bias, a_log, d_skip, ssd_norm_w, w_ssd_out, pool_w, pool_scale, w_pool_out, w_out, g_ffn, w_gate_up, w_down, g_final, loss_target, m_c_ctx, m_w_ada, m_b_ada, m_g_mix, m_w_in, m_conv_w, m_conv_b, m_dt_bias, m_a_log, m_d_skip, m_ssd_norm_w, m_w_ssd_out, m_pool_w, m_pool_scale, m_w_pool_out, m_w_out, m_g_ffn, m_w_gate_up, m_w_down, m_g_final, v_c_ctx, v_w_ada, v_b_ada, v_g_mix, v_w_in, v_conv_w, v_conv_b, v_dt_bias, v_a_log, v_d_skip, v_ssd_norm_w, v_w_ssd_out, v_pool_w, v_pool_scale, v_w_pool_out, v_w_out, v_g_ffn, v_w_gate_up, v_w_down, v_g_final):
    w = dict(c_ctx=c_ctx, w_ada=w_ada, b_ada=b_ada, g_mix=g_mix, w_in=w_in, conv_w=conv_w, conv_b=conv_b, dt_bias=dt_bias,
             a_log=a_log, d_skip=d_skip, ssd_norm_w=ssd_norm_w, w_ssd_out=w_ssd_out, pool_w=pool_w, pool_scale=pool_scale,
             w_pool_out=w_pool_out, w_out=w_out, g_ffn=g_ffn, w_gate_up=w_gate_up, w_down=w_down, g_final=g_final)
    m = dict(c_ctx=m_c_ctx, w_ada=m_w_ada, b_ada=m_b_ada, g_mix=m_g_mix, w_in=m_w_in, conv_w=m_conv_w, conv_b=m_conv_b,
             dt_bias=m_dt_bias, a_log=m_a_log, d_skip=m_d_skip, ssd_norm_w=m_ssd_norm_w, w_ssd_out=m_w_ssd_out,
             pool_w=m_pool_w, pool_scale=m_pool_scale, w_pool_out=m_w_pool_out, w_out=m_w_out, g_ffn=m_g_ffn,
             w_gate_up=m_w_gate_up, w_down=m_w_down, g_final=m_g_final)
    v = dict(c_ctx=v_c_ctx, w_ada=v_w_ada, b_ada=v_b_ada, g_mix=v_g_mix, w_in=v_w_in, conv_w=v_conv_w, conv_b=v_conv_b,
             dt_bias=v_dt_bias, a_log=v_a_log, d_skip=v_d_skip, ssd_norm_w=v_ssd_norm_w, w_ssd_out=v_w_ssd_out,
             pool_w=v_pool_w, pool_scale=v_pool_scale, w_pool_out=v_w_pool_out, w_out=v_w_out, g_ffn=v_g_ffn,
             w_gate_up=v_w_gate_up, w_down=v_w_down, g_final=v_g_final)
    assert x.shape[0] == 1, "one example per device"
    pos = _position()
    core = pos[2].astype(jnp.int32).reshape(1)
    n_layers = w_in.shape[0]
    assert n_layers == 2
    dims = (ssd_norm_w.shape[1], conv_w.shape[2] * 4, dt_bias[0].size, pool_scale.shape[1])
    shard = {k: _shard2d(k, w[k]) for k in BIG}

    def halves(a):
        return a.reshape(a.shape[:-2] + (2, a.shape[-2] // 2, a.shape[-1]))

    def whole(a):
        return a.reshape(a.shape[:-3] + (2 * a.shape[-2], a.shape[-1]))

    def wire_shards(l, names):
        return [halves(shard[k][l].astype(MXU_DTYPE)) for k in names]

    def full_weights(names, gathered):
        return {k: _full_from_blocks(k, whole(a)) for k, a in zip(names, gathered)}

    first = comm_call("gather0_chips", gather_chips(wire_shards(0, FIRST_USED), conv=conv_w))
    got0 = full_weights(FIRST_USED, comm_call("gather0_pair", gather_pair(first[:-1])))
    conv_all = first[-1]
    conv_full = [jnp.transpose(conv_all[:, l], (1, 0, 2)).reshape(conv_all.shape[2], -1) for l in range(n_layers)]

    boxes = {}

    def layer_w_fn(l):
        if l == 0:
            full = dict(got0)
            late = {k: None for k in LATER_USED}
        else:
            full = full_weights(BIG, boxes[("fwd", 0)]["gate_up_mm"])
            late = {}
        full["conv_w"] = conv_full[l]
        lw = LazyDict(_prep_layer_weights(*[full[k] if k in full else (None if k in late else w[k][l]) for k in LAYER_KEYS]))
        for i, k in enumerate(late):
            lw[k] = (lambda i=i, k=k: _full_from_blocks(k, whole(boxes[("fwd", 0)]["conv"][i])))
        return lw

    def fwd_hosts(l, box):
        boxes[("fwd", l)] = box
        if l != 0:
            return None
        return {"in_mm": lambda box: gather_chips(wire_shards(0, LATER_USED)), "conv": lambda box: gather_pair(box["in_mm"]),
                "ssd": lambda box: gather_chips(wire_shards(1, BIG)), "gate_up_mm": lambda box: gather_pair(box["ssd"])}

    def blocks(gl, names):
        return [halves(_blocks_from_full(k, gl[k])) for k in names]

    def pair_sums(tag, names, G, recv):
        return [add_own_half(f"pair_sum{tag}_{k}", g, r, core) for k, g, r in zip(names, G, recv)]

    def chip_sums(tag, names, parts):
        return [sum_slots(f"chip_sum{tag}_{k}", p, core) for k, p in zip(names, parts)]

    def reduce_now(tag, gl, names):
        G = blocks(gl, names)
        pair = pair_sums(tag, names, G, comm_call(f"swap{tag}", swap_halves(G)))
        fin = chip_sums(tag, names, comm_call(f"scatter{tag}", scatter_chips(pair)))
        return [whole(a) for a in comm_call(f"share{tag}", share_halves(fin))]

    small_layers = {}
    n_big = len(BIG)

    def bwd_hosts(l, grads, box):
        boxes[("bwd", l)] = box
        if l != 0:
            return None
        gl1 = _unprep_layer_grads(grads[1], dims)
        small_layers[1] = gl1
        G1 = blocks(gl1, BIG)
        early = {}

        def scan_host(box):
            early["G"] = blocks(box["g"], READY_EARLY)
            return combine(scatter_chips(pair_sums("1", BIG, G1, box["down_dx"])), swap_halves(early["G"]))

        def conv_host(box):
            return scatter_chips(pair_sums("0e", READY_EARLY, early["G"], box["ssd"][n_big:]))

        return {"down_dx": lambda box: swap_halves(G1), "ssd": scan_host, "conv": conv_host,
                "in_dx": lambda box: share_halves(chip_sums("1", BIG, box["ssd"][:n_big])),
                "in_dw": lambda box: share_halves(chip_sums("0e", READY_EARLY, box["conv"]))}

    loss, grad_x, grads, d_c_ctx, d_g_final = local_step(
        x[0], ctx[0], c[0], c_ctx, loss_target[0], layer_w_fn, n_layers, g_final, fwd_hosts, bwd_hosts)
    loss = lax.psum(loss, ("x", "y", "c"))
    reduced1 = [whole(a) for a in boxes[("bwd", 0)]["in_dx"]]
    gl0 = _unprep_layer_grads(grads[0], dims)
    small_layers[0] = gl0
    red0 = dict(zip(READY_EARLY, [whole(a) for a in boxes[("bwd", 0)]["in_dw"]]))
    red0.update(zip(READY_LAST, reduce_now("0", gl0, READY_LAST)))
    reduced0 = [red0[k] for k in BIG]

    small_full = dict(c_ctx=d_c_ctx, g_final=d_g_final.reshape(-1))
    for k in SMALL:
        if k not in small_full:
            small_full[k] = jnp.stack([small_layers[l][k] for l in range(n_layers)])
    shapes = [small_full[k].shape for k in SMALL]
    packed = _pack([small_full[k] for k in SMALL], _rows_for(shapes))
    total = sum_slots("small_sum", comm_call("gather_small", gather_everyone(packed))[0])
    small_g = dict(zip(SMALL, _unpack(total, shapes)))
    cw = conv_w.shape[2]
    small_g["conv_w"] = lax.dynamic_slice_in_dim(small_g["conv_w"], _chip_index(pos) * cw, cw, axis=2)

    grad, delta, new_m, new_v = {}, {}, {}, {}
    for k, g0, g1 in zip(BIG, reduced0, reduced1):
        shp = w[k].shape
        flat = lambda a: _shard2d(k, a)
        outs = adamw(f"adamw_{k}", flat(w[k]), [g0, g1], flat(m[k]), flat(v[k]))
        grad[k], delta[k], new_m[k], new_v[k] = [a.reshape(shp) for a in outs]
    sshapes = [w[k].shape for k in SMALL]
    srows = _rows_for(sshapes)
    pk = lambda d: _pack([d[k] for k in SMALL], srows)[None]
    _, d_, m_, v_ = adamw("adamw_small", pk(w), [pk(small_g)[0]], pk(m), pk(v))
    for k, dd, mm, vv in zip(SMALL, _unpack(d_, sshapes), _unpack(m_, sshapes), _unpack(v_, sshapes)):
        grad[k], delta[k], new_m[k], new_v[k] = small_g[k], dd, mm, vv

    return (loss, grad_x[None], *[grad[k] for k in WEIGHTS], *[delta[k] for k in WEIGHTS],
            *[new_m[k] for k in WEIGHTS], *[new_v[k] for k in WEIGHTS])
```

```python
import functools

import jax
import jax.numpy as jnp
import numpy as np
from jax import lax
from jax.experimental import pallas as pl
from jax.experimental.pallas import tpu as pltpu

F32 = jnp.float32
MXU_DTYPE = jnp.bfloat16
ACT_DTYPE = jnp.bfloat16
VMEM_LIMIT_BYTES = 48 * 1024 * 1024
EPS = 1e-6
NEG = -1e30

SSD_HEADDIM = 64
SSD_GROUPS = 8
SSD_STATE = 128
SSD_CHUNK = 128
SSD_GROUPS_PER_STEP = 8
SSD_CONV = 5
GRID_W = 64
POOL_WINDOWS = (2, 4, 8, 16)
ROW_TILE = 256
DT_PAD = 512

ADAM_LR = 0.001
ADAM_B1 = 0.9
ADAM_B2 = 0.999
ADAM_EPS = 1e-08
ADAM_WD = 0.01
ADAM_STEP = 10

MESH = pl.DeviceIdType.MESH


def _pcall(body, **kw):
    return pl.pallas_call(body, **kw)


def _params(*sem):
    return pltpu.CompilerParams(dimension_semantics=tuple(sem), vmem_limit_bytes=VMEM_LIMIT_BYTES)


def _pick_tile(n, cands):
    for t in cands:
        if n % t == 0:
            return t
    return n


PLANE = ((1, 0, 0), (0, 1, 0), (1, 1, 0))
PAIR = ((0, 0, 1),)
EVERYONE = tuple((a, b, d) for a in (0, 1) for b in (0, 1) for d in (0, 1) if a + b + d)
HBM = pl.BlockSpec(memory_space=pl.ANY)


def _position():
    return lax.axis_index("x"), lax.axis_index("y"), lax.axis_index("c")


def _flip(pos, rel):
    return tuple(1 - p if r else p for p, r in zip(pos, rel))


def _chip_index(pos):
    return 2 * pos[0] + pos[1]


def _device_index(pos):
    return 4 * pos[0] + 2 * pos[1] + pos[2]


class Exchange:
    def __init__(self, copies, n_remote, n_local, operands, out_shapes, aliases=None):
        self.copies, self.n_remote, self.n_local = copies, n_remote, n_local
        self.operands, self.out_shapes, self.aliases = list(operands), list(out_shapes), dict(aliases or {})

    def scratch(self):
        return [pltpu.SemaphoreType.DMA((max(self.n_remote, 1),)), pltpu.SemaphoreType.DMA((max(self.n_remote, 1),)),
                pltpu.SemaphoreType.DMA((max(self.n_local, 1),))]

    def descriptors(self, ins, outs, sems):
        send_sems, recv_sems, local_sems = sems
        local, remote = self.copies(ins, outs, _position())
        assert len(local) == self.n_local and len(remote) == self.n_remote
        cps = [pltpu.make_async_copy(src, dst, local_sems.at[k]) for k, (src, dst) in enumerate(local)]
        cps += [pltpu.make_async_remote_copy(src_ref=src, dst_ref=dst, send_sem=send_sems.at[k], recv_sem=recv_sems.at[k],
                                             device_id=peer, device_id_type=MESH) for k, (src, dst, peer) in enumerate(remote)]
        return cps


def combine(a, b):
    na, nao = len(a.operands), len(a.out_shapes)

    def copies(ins, outs, pos):
        la, ra = a.copies(ins[:na], outs[:nao], pos)
        lb, rb = b.copies(ins[na:], outs[nao:], pos)
        return la + lb, ra + rb

    aliases = dict(a.aliases)
    aliases.update({na + k: nao + v for k, v in b.aliases.items()})
    return Exchange(copies, a.n_remote + b.n_remote, a.n_local + b.n_local, a.operands + b.operands,
                    a.out_shapes + b.out_shapes, aliases)


class LazyDict(dict):
    def __getitem__(self, key):
        v = dict.__getitem__(self, key)
        if callable(v):
            v = v()
            dict.__setitem__(self, key, v)
        return v


def comm_call(name, ex):
    n_in, n_out = len(ex.operands), len(ex.out_shapes)

    def body(*refs):
        cps = ex.descriptors(refs[:n_in], refs[n_in:n_in + n_out], refs[n_in + n_out:])
        for cp in cps:
            cp.start()
        for cp in cps:
            cp.wait()

    return _pcall(
        body, name=name, out_shape=ex.out_shapes, in_specs=[HBM] * n_in, out_specs=[HBM] * n_out,
        scratch_shapes=ex.scratch(), input_output_aliases=ex.aliases,
        compiler_params=pltpu.CompilerParams(has_side_effects=True),
    )(*ex.operands)


def hosted_call(body, ex, operands, *, name, out_shape, grid, in_specs, out_specs, scratch_shapes=()):
    n_in, n_out, n_scr = len(operands), len(out_shape), len(scratch_shapes)
    sem = ("arbitrary",) * len(grid)
    if ex is None:
        res = _pcall(body, name=name, out_shape=list(out_shape), grid=grid, in_specs=list(in_specs),
                     out_specs=list(out_specs), scratch_shapes=list(scratch_shapes), compiler_params=_params(*sem))(*operands)
        return res, []
    x_in, x_out = len(ex.operands), len(ex.out_shapes)

    def wrapped(*refs):
        o = 0
        ins = refs[o:o + n_in]; o += n_in
        xins = refs[o:o + x_in]; o += x_in
        outs = refs[o:o + n_out]; o += n_out
        xouts = refs[o:o + x_out]; o += x_out
        scr = refs[o:o + n_scr]; o += n_scr
        sems = refs[o:]
        first = last = None
        for a, n in enumerate(grid):
            i = pl.program_id(a)
            first = (i == 0) if first is None else first & (i == 0)
            last = (i == n - 1) if last is None else last & (i == n - 1)

        @pl.when(first)
        def _():
            for cp in ex.descriptors(xins, xouts, sems):
                cp.start()

        body(*ins, *outs, *scr)

        @pl.when(last)
        def _():
            for cp in ex.descriptors(xins, xouts, sems):
                cp.wait()

    aliases = {n_in + k: n_out + v for k, v in ex.aliases.items()}
    res = _pcall(
        wrapped, name=name, out_shape=list(out_shape) + ex.out_shapes, grid=grid,
        in_specs=list(in_specs) + [HBM] * x_in, out_specs=list(out_specs) + [HBM] * x_out,
        scratch_shapes=list(scratch_shapes) + ex.scratch(), input_output_aliases=aliases,
        compiler_params=pltpu.CompilerParams(dimension_semantics=sem, vmem_limit_bytes=VMEM_LIMIT_BYTES,
                                             has_side_effects=True),
    )(*operands, *ex.operands)
    return res[:n_out], res[n_out:]


def _dot(a, b, dims):
    return lax.dot_general(a.astype(MXU_DTYPE), b.astype(MXU_DTYPE), (dims, ((), ())), preferred_element_type=F32)


_NN = ((1,), (0,))
_NT = ((1,), (1,))
_TN = ((0,), (0,))


@jax.custom_vjp
def _mm(a, b):
    return _dot(a, b, _NN)


def _mm_fwd(a, b):
    return _mm(a, b), (a, b)


def _mm_bwd(res, g):
    a, b = res
    return _dot(g, b, _NT).astype(a.dtype), _dot(a, g, _TN).astype(b.dtype)


_mm.defvjp(_mm_fwd, _mm_bwd)


@jax.custom_vjp
def _mm_nt(a, b):
    return _dot(a, b, _NT)


def _mm_nt_fwd(a, b):
    return _mm_nt(a, b), (a, b)


def _mm_nt_bwd(res, g):
    a, b = res
    return _dot(g, b, _NN).astype(a.dtype), _dot(g, a, _TN).astype(b.dtype)


_mm_nt.defvjp(_mm_nt_fwd, _mm_nt_bwd)


@jax.custom_vjp
def _mm_tn(a, b):
    return _dot(a, b, _TN)


def _mm_tn_fwd(a, b):
    return _mm_tn(a, b), (a, b)


def _mm_tn_bwd(res, g):
    a, b = res
    return _dot(b, g, _NT).astype(a.dtype), _dot(a, g, _NN).astype(b.dtype)


_mm_tn.defvjp(_mm_tn_fwd, _mm_tn_bwd)


def _dot_exact(m01, v):
    m = m01.astype(jnp.bfloat16)
    hi = v.astype(jnp.bfloat16)
    r1 = v - hi.astype(F32)
    mid = r1.astype(jnp.bfloat16)
    lo = (r1 - mid.astype(F32)).astype(jnp.bfloat16)
    out = jnp.dot(m, hi, preferred_element_type=F32)
    out = out + jnp.dot(m, mid, preferred_element_type=F32)
    return out + jnp.dot(m, lo, preferred_element_type=F32)


@jax.custom_vjp
def _lin01(m, mt, v):
    return _dot_exact(m, v)


def _lin01_fwd(m, mt, v):
    return _dot_exact(m, v), (m, mt)


def _lin01_bwd(res, g):
    m, mt = res
    return jnp.zeros_like(m), jnp.zeros_like(mt), _dot_exact(mt, g)


_lin01.defvjp(_lin01_fwd, _lin01_bwd)


MATMUL_VMEM_BUDGET = VMEM_LIMIT_BYTES * 3 // 4


def _mm_tiles(m, n, k_bytes_a, k_bytes_b, out_bytes, cands_m, cands_n):
    best = None
    for tm in cands_m:
        if m % tm:
            continue
        for tn in cands_n:
            if n % tn:
                continue
            need = 2 * (tm * k_bytes_a + tn * k_bytes_b + tm * tn * out_bytes)
            if need <= MATMUL_VMEM_BUDGET and (best is None or tm * tn > best[0] * best[1]):
                best = (tm, tn)
    assert best is not None, (m, n)
    return best


_ROW_CANDS = (4352, 2176, 1088, 768, 544, 512, 272, 256, 128, 16)
_COL_CANDS = (2816, 2048, 1408, 1024, 512, 256, 128)


def _one(res, xres, ex):
    return res[0] if ex is None else (res[0], xres)


def matmul_nn(name, a, b, out_dtype=F32, ex=None):
    M, K = a.shape
    N = b.shape[1]
    tm, tn = _mm_tiles(M, N, K * a.dtype.itemsize, K * b.dtype.itemsize, jnp.dtype(out_dtype).itemsize,
                       _ROW_CANDS, (512, 256, 128))

    def body(a_ref, b_ref, o_ref):
        o_ref[...] = _dot(a_ref[...], b_ref[...], _NN).astype(o_ref.dtype)

    res, xres = hosted_call(
        body, ex, [a, b], name=name, out_shape=[jax.ShapeDtypeStruct((M, N), out_dtype)], grid=(N // tn, M // tm),
        in_specs=[pl.BlockSpec((tm, K), lambda j, i: (i, 0)), pl.BlockSpec((K, tn), lambda j, i: (0, j))],
        out_specs=[pl.BlockSpec((tm, tn), lambda j, i: (i, j))])
    return _one(res, xres, ex)


def matmul_nt(name, g, b, out_dtype=F32, ex=None, offsets=None):
    pieces = list(g) if isinstance(g, (list, tuple)) else [g]
    offsets = list(offsets) if offsets is not None else [0]
    M = pieces[0].shape[0]
    K, N = b.shape
    g_bytes = sum(p.shape[1] * p.dtype.itemsize for p in pieces)
    tm, tk = _mm_tiles(M, K, g_bytes, N * b.dtype.itemsize, jnp.dtype(out_dtype).itemsize, _ROW_CANDS, _COL_CANDS)

    def body(*refs):
        b_ref, o_ref = refs[-2:]
        acc = None
        for g_ref, off in zip(refs[:-2], offsets):
            part = _dot(g_ref[...], b_ref[:, off:off + g_ref.shape[1]], _NT)
            acc = part if acc is None else acc + part
        o_ref[...] = acc.astype(o_ref.dtype)

    res, xres = hosted_call(
        body, ex, pieces + [b], name=name, out_shape=[jax.ShapeDtypeStruct((M, K), out_dtype)], grid=(K // tk, M // tm),
        in_specs=[pl.BlockSpec((tm, p.shape[1]), lambda j, i: (i, 0)) for p in pieces]
        + [pl.BlockSpec((tk, N), lambda j, i: (j, 0))],
        out_specs=[pl.BlockSpec((tm, tk), lambda j, i: (i, j))])
    return _one(res, xres, ex)


def matmul_tn(name, a, g, ex=None):
    M, K = a.shape
    N = g.shape[1]
    tk, tn = _mm_tiles(K, N, M * a.dtype.itemsize, M * g.dtype.itemsize, 4, (512, 256, 128), (512, 256, 128))

    def body(a_ref, g_ref, o_ref):
        o_ref[...] = _dot(a_ref[...], g_ref[...], _TN)

    res, xres = hosted_call(
        body, ex, [a, g], name=name, out_shape=[jax.ShapeDtypeStruct((K, N), F32)], grid=(K // tk, N // tn),
        in_specs=[pl.BlockSpec((M, tk), lambda i, j: (0, i)), pl.BlockSpec((M, tn), lambda i, j: (0, j))],
        out_specs=[pl.BlockSpec((tk, tn), lambda i, j: (i, j))])
    return _one(res, xres, ex)


class Arg:
    def __init__(self, arr, block, imap, kind):
        self.arr, self.block, self.imap, self.kind = arr, block, imap, kind


class Rows:
    def __init__(self, nt, nct, tm, ncol=1):
        self.nt, self.nct, self.tm, self.ncol = nt, nct, tm, ncol

    def seg(self, i):
        return jnp.where(i >= self.nct, 1, 0)

    def spec(self, block, imap):
        return pl.BlockSpec(block, lambda j, i: imap(j, i, self.seg(i)))

    def row(self, arr, width, cb0=0, follow=False, roff=0, stride=1):
        f = stride if follow else 0
        return Arg(arr, (self.tm, width), lambda j, i, s: (i + roff, cb0 + f * j), "row")

    def vec(self, arr, follow=False, kind="acc"):
        w = arr.shape[1] // (self.ncol if follow else 1)
        f = 1 if follow else 0
        return Arg(arr, (1, w), lambda j, i, s: (0, f * j), kind)

    def segvec(self, arr, kind="seg"):
        return Arg(arr, (None, 1, arr.shape[2]), lambda j, i, s: (s, 0, 0), kind)


def _load(ref):
    return ref[...].astype(F32) if ref.dtype != F32 else ref[...]


def stage_fwd(name, f, rows, args, outs):
    n_in = len(args)

    def body(*refs):
        vals = [_load(r) for r in refs[:n_in]]
        res = f(*vals)
        for r, v in zip(refs[n_in:], res):
            r[...] = v.astype(r.dtype)

    T = rows.nt * rows.tm
    out_shape = [jax.ShapeDtypeStruct((T, w * (rows.ncol if fo else 1)), dt) for w, dt, fo in outs]
    out_specs = [pl.BlockSpec((rows.tm, w), (lambda j, i, fo=fo: (i, j if fo else 0))) for w, dt, fo in outs]
    res = _pcall(
        body, name=name, out_shape=out_shape, grid=(rows.ncol, rows.nt),
        in_specs=[rows.spec(a.block, a.imap) for a in args], out_specs=out_specs,
        compiler_params=_params("parallel", "parallel"),
    )(*[a.arr for a in args])
    return res


def stage_bwd(name, f, rows, args, cots, row_dtypes):
    n_in, n_ct = len(args), len(cots)
    diff = [k for k, a in enumerate(args) if a.kind != "const"]
    row_dt = {}
    for k in diff:
        if args[k].kind == "row":
            row_dt[k] = row_dtypes[len(row_dt)]

    def body(*refs):
        i = pl.program_id(1)
        vals = [_load(r) for r in refs[:n_in]]
        cts = tuple(_load(r) for r in refs[n_in:n_in + n_ct])
        outs = refs[n_in + n_ct:]

        def g(*dv):
            full = list(vals)
            for k, v in zip(diff, dv):
                full[k] = v
            return tuple(f(*full))

        _, vjp = jax.vjp(g, *[vals[k] for k in diff])
        grads = vjp(cts)
        for k, o, gr in zip(diff, outs, grads):
            kind = args[k].kind
            if kind == "row":
                o[...] = gr.astype(o.dtype)
            else:
                first = (i == 0) | (i == rows.nct) if kind == "seg" else (i == 0)

                @pl.when(first)
                def _():
                    o[...] = gr.astype(o.dtype)

                @pl.when(jnp.logical_not(first))
                def _():
                    o[...] += gr.astype(o.dtype)

    T = rows.nt * rows.tm
    out_shape, out_specs = [], []
    for k in diff:
        a = args[k]
        if a.kind == "row":
            out_shape.append(jax.ShapeDtypeStruct((T, a.block[1] * (rows.ncol if _follows(a) else 1)), row_dt[k]))
            fo = _follows(a)
            out_specs.append(pl.BlockSpec(a.block, (lambda j, i, fo=fo: (i, j if fo else 0))))
        else:
            out_shape.append(jax.ShapeDtypeStruct(a.arr.shape, F32))
            out_specs.append(rows.spec(a.block, a.imap))
    return _pcall(
        body, name=name, out_shape=out_shape, grid=(rows.ncol, rows.nt),
        in_specs=[rows.spec(a.block, a.imap) for a in list(args) + list(cots)], out_specs=out_specs,
        compiler_params=_params("arbitrary", "arbitrary"),
    )(*[a.arr for a in list(args) + list(cots)])


def _follows(a):
    return a.imap(1, 0, 0)[-1] != a.imap(0, 0, 0)[-1]


def _rms(x):
    return x * lax.rsqrt(jnp.mean(x * x, axis=-1, keepdims=True) + EPS)


def f_norm_mod(x, g, sh, sc):
    return ((_rms(x) * g) * (1.0 + sc) + sh,)


def f_resid_norm_mod(x, mo, ga, g, sh, sc):
    x1 = x + ga * mo
    return x1, (_rms(x1) * g) * (1.0 + sc) + sh


def f_resid(x, dn, ga):
    return (x + ga * dn,)


def f_silu(x):
    return (x * jax.nn.sigmoid(x),)


def f_bias(x, b):
    return (x + b,)


def f_ssd_gate(y0, y1, xs, z, dskip, nw):
    y = y0 + y1 + dskip * xs
    return (_rms(y * (z * jax.nn.sigmoid(z))) * nw,)


def f_pool(u, pmat, pmat_t, inv_cnt, pw, scale):
    pm = _lin01(pmat, pmat_t, u) * inv_cnt - u
    return (_mm(pm, pw) * scale,)


def f_merge(o_ssd, o_pool, gl_ssd, gl_pool):
    return (jax.nn.sigmoid(gl_ssd) * o_ssd + jax.nn.sigmoid(gl_pool) * o_pool,)


@jax.custom_vjp
def _halve_cols(x):
    h = x.shape[1] // 2
    return x[:, :h], x[:, h:]


def _halve_cols_fwd(x):
    return _halve_cols(x), None


def _halve_cols_bwd(_, g):
    return (jnp.concatenate(g, axis=1),)


_halve_cols.defvjp(_halve_cols_fwd, _halve_cols_bwd)


def f_swiglu(gu):
    a, b = _halve_cols(gu)
    return ((a * jax.nn.sigmoid(a)) * b,)


def f_loss(x, tgt, g):
    err = _rms(x) * g - tgt
    return (0.5 * jnp.mean(err * err, axis=-1, keepdims=True),)


CONV_TILE = 128


def _shift_rows(v, j, n_ctx):
    if j == 0:
        return v
    T = v.shape[0]
    r = lax.broadcasted_iota(jnp.int32, v.shape, 0)
    lo = jnp.where(r >= n_ctx, n_ctx, 0)
    hi = jnp.where(r >= n_ctx, T, n_ctx)
    ok = (r + j >= lo) & (r + j < hi)
    return jnp.where(ok, pltpu.roll(v, (-j) % T, 0), 0.0)


def conv_fwd(name, proj, conv_w, conv_b, n_ctx, width, ex=None):
    T = proj.shape[0]
    half = SSD_CONV // 2

    def body(u_ref, w_ref, b_ref, o_ref):
        u = u_ref[...]
        pre = jnp.broadcast_to(b_ref[...], u.shape)
        for k in range(SSD_CONV):
            pre = pre + w_ref[k:k + 1, :] * _shift_rows(u, k - half, n_ctx)
        o_ref[...] = pre * jax.nn.sigmoid(pre)

    col = lambda t: (0, t)
    res, xres = hosted_call(
        body, ex, [proj, conv_w, conv_b], name=name, out_shape=[jax.ShapeDtypeStruct((T, width), F32)],
        grid=(width // CONV_TILE,),
        in_specs=[pl.BlockSpec((T, CONV_TILE), col), pl.BlockSpec((SSD_CONV, CONV_TILE), col),
                  pl.BlockSpec((1, CONV_TILE), col)],
        out_specs=[pl.BlockSpec((T, CONV_TILE), col)])
    return res[0], xres


def conv_bwd(name, proj, conv_w, conv_b, d_act2, d_skip, n_ctx, width, ex=None):
    T = proj.shape[0]
    half = SSD_CONV // 2

    def body(u_ref, w_ref, b_ref, c0_ref, c1_ref, cs_ref, du_ref, dw_ref, db_ref):
        t = pl.program_id(0)
        u = u_ref[...]
        pre = jnp.broadcast_to(b_ref[...], u.shape)
        for k in range(SSD_CONV):
            pre = pre + w_ref[k:k + 1, :] * _shift_rows(u, k - half, n_ctx)
        sg = jax.nn.sigmoid(pre)
        ct = c0_ref[...] + c1_ref[...] + jnp.where(t % 4 < 2, cs_ref[...], 0.0)
        dpre = ct * (sg * (1.0 + pre * (1.0 - sg)))
        du = jnp.zeros_like(u)
        for k in range(SSD_CONV):
            du = du + w_ref[k:k + 1, :] * _shift_rows(dpre, half - k, n_ctx)
            dw_ref[k:k + 1, :] = jnp.sum(dpre * _shift_rows(u, k - half, n_ctx), axis=0, keepdims=True)
        du_ref[...] = du.astype(du_ref.dtype)
        db_ref[...] = jnp.sum(dpre, axis=0, keepdims=True)

    col = lambda t: (0, t)
    skip_col = lambda t: (0, (t // 4) * 2 + jnp.minimum(t % 4, 1))
    res, xres = hosted_call(
        body, ex, [proj, conv_w, conv_b, d_act2, d_act2, d_skip], name=name,
        out_shape=[jax.ShapeDtypeStruct((T, width), ACT_DTYPE), jax.ShapeDtypeStruct((SSD_CONV, width), F32),
                   jax.ShapeDtypeStruct((1, width), F32)],
        grid=(width // CONV_TILE,),
        in_specs=[pl.BlockSpec((T, CONV_TILE), col), pl.BlockSpec((SSD_CONV, CONV_TILE), col),
                  pl.BlockSpec((1, CONV_TILE), col), pl.BlockSpec((T, CONV_TILE), col),
                  pl.BlockSpec((T, CONV_TILE), lambda t: (1, t)), pl.BlockSpec((T, CONV_TILE), skip_col)],
        out_specs=[pl.BlockSpec((T, CONV_TILE), col), pl.BlockSpec((SSD_CONV, CONV_TILE), col),
                   pl.BlockSpec((1, CONV_TILE), col)])
    return res[0], res[1], res[2], xres


@jax.custom_vjp
def _cumsum_mat(tri, tri_t, a):
    return jnp.dot(tri, a, precision=lax.Precision.HIGHEST, preferred_element_type=F32)


def _cumsum_fwd(tri, tri_t, a):
    return _cumsum_mat(tri, tri_t, a), (tri, tri_t)


def _cumsum_bwd(res, g):
    tri, tri_t = res
    return (jnp.zeros_like(tri), jnp.zeros_like(tri_t),
            jnp.dot(tri_t, g, precision=lax.Precision.HIGHEST, preferred_element_type=F32))


_cumsum_mat.defvjp(_cumsum_fwd, _cumsum_bwd)


def _ssd_dt(dtraw, dt_bias, a_log, tri, tri_t):
    dt_all = jax.nn.softplus(dtraw + dt_bias)
    a_all = dt_all * (-jnp.exp(a_log))
    return dt_all, a_all, _cumsum_mat(tri, tri_t, a_all)


def _ssd_chunk(xs, bm, cm, dt_all, a_all, s_all, s_in, mask, idx0):
    (xs,), (s_in,) = xs, s_in
    Q = xs.shape[0]
    hpg = xs.shape[1] // SSD_HEADDIM
    lane = lax.broadcasted_iota(jnp.int32, dt_all.shape, 1)
    head = lax.broadcasted_iota(jnp.int32, xs.shape, 1) // SSD_HEADDIM
    head1 = lax.broadcasted_iota(jnp.int32, (1, xs.shape[1]), 1) // SSD_HEADDIM

    def pick(v, r):
        return jnp.sum(jnp.where(lane == idx0 + r, v, 0.0), axis=1, keepdims=True)

    def expand(cols, hd):
        out = cols[hpg - 1]
        for r in range(hpg - 2, -1, -1):
            out = jnp.where(hd == r, cols[r], out)
        return out

    dt_r = [pick(dt_all, r) for r in range(hpg)]
    s_r = [pick(s_all, r) for r in range(hpg)]
    stot_r = [jnp.sum(jnp.where(lane == idx0 + r, a_all, 0.0), keepdims=True).reshape(1, 1) for r in range(hpg)]

    xd = xs * expand([jnp.broadcast_to(c, xs.shape) for c in dt_r], head)
    cb = _mm_nt(cm, bm)
    weights, stacked = [], []
    for r in range(hpg):
        sm = jnp.broadcast_to(s_r[r], (Q, Q))
        weights.append(cb * jnp.exp(jnp.where(mask, sm - sm.T, NEG)))
        stacked.append(jnp.where(head == r, xd, 0.0))
    y = expand([jnp.broadcast_to(jnp.exp(c), xs.shape) for c in s_r], head) * _mm(cm, s_in)
    y = y + _mm(jnp.concatenate(weights, axis=1), jnp.concatenate(stacked, axis=0))
    to_end = expand([jnp.broadcast_to(jnp.exp(t - c), xs.shape) for t, c in zip(stot_r, s_r)], head)
    carry = expand([jnp.broadcast_to(jnp.exp(t), (1, xs.shape[1])) for t in stot_r], head1)
    s_out = carry * s_in + _mm_tn(bm, xd * to_end)
    return [y], [s_out]


def _scan_consts():
    q = SSD_CHUNK
    i = np.arange(q)[:, None]
    j = np.arange(q)[None, :]
    fwd = (j <= i).astype(np.float32)
    bwd = (j >= i).astype(np.float32)
    tri = np.stack([fwd, bwd])
    return jnp.asarray(tri), jnp.asarray(np.stack([fwd.T, bwd.T]))


def _chunk_of(d, k, ncc, nc):
    rev = jnp.where(k < ncc, ncc - 1 - k, nc - 1 + ncc - k)
    return jnp.where(d == 0, k, rev)


def ssd_fwd(name, xbc, proj, dt_cb, dt_bias, a_log, n_ctx, ex=None):
    T = xbc.shape[0]
    q, G = SSD_CHUNK, SSD_GROUPS
    nc, ncc = T // q, n_ctx // q
    gw = xbc.shape[1] // G
    xw = gw - 2 * SSD_STATE
    hpg = xw // SSD_HEADDIM
    nh = G * hpg
    tri, tri_t = _scan_consts()

    gs = SSD_GROUPS_PER_STEP

    def body(x_ref, dt_ref, bias_ref, alog_ref, tri_ref, trit_ref, y_ref, sin_ref, state):
        d, gb, k = pl.program_id(0), pl.program_id(1), pl.program_id(2)

        @pl.when(k == 0)
        def _():
            state[...] = jnp.zeros_like(state)

        tri_v = tri_ref[...]
        dt_all, a_all, s_all = _ssd_dt(dt_ref[...], bias_ref[...], alog_ref[...], tri_v, trit_ref[...])
        pairs = [(0, xw)]
        for j in range(gs):
            o = j * gw
            sin_ref[j] = state[j]
            ys, s_outs = _ssd_chunk(
                [x_ref[:, o + lo:o + hi] for lo, hi in pairs], x_ref[:, o + xw:o + xw + SSD_STATE],
                x_ref[:, o + xw + SSD_STATE:o + gw], dt_all, a_all, s_all, [state[j, :, lo:hi] for lo, hi in pairs],
                tri_v > 0.5, d * nh + (gb * gs + j) * hpg)
            for (lo, hi), y, s_out in zip(pairs, ys, s_outs):
                y_ref[:, j * xw + lo:j * xw + hi] = y
                state[j, :, lo:hi] = s_out

    ch = lambda d, g, k: _chunk_of(d, k, ncc, nc)
    res, xres = hosted_call(
        body, ex, [xbc, proj, dt_bias, a_log, tri, tri_t], name=name,
        out_shape=[jax.ShapeDtypeStruct((2 * T, G * xw), F32),
                   jax.ShapeDtypeStruct((2, nc, G, SSD_STATE, xw), F32)],
        grid=(2, G // gs, nc),
        in_specs=[pl.BlockSpec((q, gs * gw), lambda d, g, k: (ch(d, g, k), g)),
                  pl.BlockSpec((q, 128), lambda d, g, k: (ch(d, g, k), dt_cb)),
                  pl.BlockSpec((1, 128), lambda d, g, k: (0, 0)),
                  pl.BlockSpec((1, 128), lambda d, g, k: (0, 0)),
                  pl.BlockSpec((None, q, q), lambda d, g, k: (d, 0, 0)),
                  pl.BlockSpec((None, q, q), lambda d, g, k: (d, 0, 0))],
        out_specs=[pl.BlockSpec((q, gs * xw), lambda d, g, k: (d * nc + ch(d, g, k), g)),
                   pl.BlockSpec((None, None, gs, SSD_STATE, xw), lambda d, g, k: (d, k, g, 0, 0))],
        scratch_shapes=[pltpu.VMEM((gs, SSD_STATE, xw), F32)])
    return res[0], res[1], xres


def ssd_bwd(name, xbc, proj, dt_cb, dt_bias, a_log, states, dy, n_ctx, ex=None):
    T = xbc.shape[0]
    q, G = SSD_CHUNK, SSD_GROUPS
    nc, ncc = T // q, n_ctx // q
    gw = xbc.shape[1] // G
    xw = gw - 2 * SSD_STATE
    hpg = xw // SSD_HEADDIM
    nh = G * hpg
    tri, tri_t = _scan_consts()

    gs = SSD_GROUPS_PER_STEP

    def body(x_ref, dt_ref, bias_ref, alog_ref, tri_ref, trit_ref, sin_ref, dy_ref,
             dx_ref, ddt_ref, dbias_ref, dalog_ref, dstate):
        d, gb, k = pl.program_id(0), pl.program_id(1), pl.program_id(2)
        first = (d == 0) & (gb == 0) & (k == 0)

        @pl.when(first)
        def _():
            ddt_ref[...] = jnp.zeros_like(ddt_ref)
            dbias_ref[...] = jnp.zeros_like(dbias_ref)
            dalog_ref[...] = jnp.zeros_like(dalog_ref)

        @pl.when(k == 0)
        def _():
            dstate[...] = jnp.zeros_like(dstate)

        tri_v, trit_v = tri_ref[...], trit_ref[...]
        mask = tri_v > 0.5

        pairs = [(0, xw)]
        npair = len(pairs)
        per = 2 * npair + 2

        def fn(dtraw, bias, alog, *per_group):
            dt_all, a_all, s_all = _ssd_dt(dtraw, bias, alog, tri_v, trit_v)
            ys, s_outs = [], []
            for j in range(gs):
                grp = per_group[per * j:per * (j + 1)]
                y, s_out = _ssd_chunk(list(grp[:npair]), grp[npair], grp[npair + 1], dt_all, a_all, s_all,
                                      list(grp[npair + 2:]), mask, d * nh + (gb * gs + j) * hpg)
                ys += y
                s_outs += s_out
            return ys, s_outs

        per_group = []
        for j in range(gs):
            o = j * gw
            per_group += [x_ref[:, o + lo:o + hi] for lo, hi in pairs]
            per_group += [x_ref[:, o + xw:o + xw + SSD_STATE], x_ref[:, o + xw + SSD_STATE:o + gw]]
            per_group += [sin_ref[j, :, lo:hi] for lo, hi in pairs]
        _, vjp = jax.vjp(fn, dt_ref[...], bias_ref[...], alog_ref[...], *per_group)
        cts = vjp(([dy_ref[:, j * xw + lo:j * xw + hi] for j in range(gs) for lo, hi in pairs],
                   [dstate[j, :, lo:hi] for j in range(gs) for lo, hi in pairs]))
        ddt, dbias, dalog = cts[:3]
        for j in range(gs):
            o = j * gw
            grp = cts[3 + per * j:3 + per * (j + 1)]
            for (lo, hi), dxs, ds_in in zip(pairs, grp[:npair], grp[npair + 2:]):
                dx_ref[:, o + lo:o + hi] = dxs
                dstate[j, :, lo:hi] = ds_in
            dx_ref[:, o + xw:o + xw + SSD_STATE] = grp[npair]
            dx_ref[:, o + xw + SSD_STATE:o + gw] = grp[npair + 1]
        row0 = pl.multiple_of(_chunk_of(d, nc - 1 - k, ncc, nc) * q, q)
        ddt_ref[pl.ds(row0, q), :] += ddt
        dbias_ref[...] += dbias
        dalog_ref[...] += dalog

    ch = lambda d, g, k: _chunk_of(d, nc - 1 - k, ncc, nc)
    res, xres = hosted_call(
        body, ex, [xbc, proj, dt_bias, a_log, tri, tri_t, states, dy], name=name,
        out_shape=[jax.ShapeDtypeStruct((2 * T, G * gw), F32), jax.ShapeDtypeStruct((T, 128), F32),
                   jax.ShapeDtypeStruct((1, 128), F32), jax.ShapeDtypeStruct((1, 128), F32)],
        grid=(2, G // gs, nc),
        in_specs=[pl.BlockSpec((q, gs * gw), lambda d, g, k: (ch(d, g, k), g)),
                  pl.BlockSpec((q, 128), lambda d, g, k: (ch(d, g, k), dt_cb)),
                  pl.BlockSpec((1, 128), lambda d, g, k: (0, 0)),
                  pl.BlockSpec((1, 128), lambda d, g, k: (0, 0)),
                  pl.BlockSpec((None, q, q), lambda d, g, k: (d, 0, 0)),
                  pl.BlockSpec((None, q, q), lambda d, g, k: (d, 0, 0)),
                  pl.BlockSpec((None, None, gs, SSD_STATE, xw), lambda d, g, k: (d, nc - 1 - k, g, 0, 0)),
                  pl.BlockSpec((q, gs * xw), lambda d, g, k: (ch(d, g, k), g))],
        out_specs=[pl.BlockSpec((q, gs * gw), lambda d, g, k: (d * nc + ch(d, g, k), g)),
                   pl.BlockSpec((T, 128), lambda d, g, k: (0, 0)),
                   pl.BlockSpec((1, 128), lambda d, g, k: (0, 0)),
                   pl.BlockSpec((1, 128), lambda d, g, k: (0, 0))],
        scratch_shapes=[pltpu.VMEM((gs, SSD_STATE, xw), F32)])
    return res[0], res[1], res[2], res[3], xres


def _perm_xbc(a):
    G = SSD_GROUPS
    n = a.shape[-1]
    gn = G * SSD_STATE
    di = n - 2 * gn
    lead = a.shape[:-1]
    xs = a[..., :di].reshape(lead + (G, di // G))
    bm = a[..., di:di + gn].reshape(lead + (G, SSD_STATE))
    cm = a[..., di + gn:].reshape(lead + (G, SSD_STATE))
    return jnp.concatenate([xs, bm, cm], axis=-1).reshape(lead + (n,))


def _unperm_xbc(a):
    G = SSD_GROUPS
    n = a.shape[-1]
    gn = G * SSD_STATE
    di = n - 2 * gn
    lead = a.shape[:-1]
    r = a.reshape(lead + (G, n // G))
    xw = di // G
    return jnp.concatenate([r[..., :xw].reshape(lead + (di,)), r[..., xw:xw + SSD_STATE].reshape(lead + (gn,)),
                            r[..., xw + SSD_STATE:].reshape(lead + (gn,))], axis=-1)


def _pool_consts(tm, n_ctx):
    assert n_ctx == tm and tm % GRID_W == 0
    mats, cnts = [], []
    for seq in (n_ctx, GRID_W):
        t = np.arange(tm)
        tt = t % seq
        base = t - tt
        ms, cs = [], []
        for k in POOL_WINDOWS:
            lo = np.clip(tt - k // 2, 0, seq) + base
            hi = np.clip(tt + k // 2, 0, seq) + base
            m = ((t[None, :] >= lo[:, None]) & (t[None, :] < hi[:, None])).astype(np.float32)
            ms.append(m)
            cs.append((1.0 / (hi - lo).astype(np.float32))[:, None])
        mats.append(np.stack(ms))
        cnts.append(np.stack(cs))
    m = np.stack(mats)
    return jnp.asarray(m), jnp.asarray(np.swapaxes(m, -1, -2)), jnp.asarray(np.stack(cnts).astype(np.float32))


def _prep_layer_weights(w_ada, b_ada, g_mix, w_in, conv_w, conv_b, dt_bias, a_log, d_skip, ssd_norm_w, w_ssd_out,
                        pool_w, pool_scale, w_pool_out, w_out, g_ffn, w_gate_up, w_down):
    D = w_in.shape[0]
    di = ssd_norm_w.shape[0]
    xbc = conv_w.shape[1]
    nh2 = dt_bias.size
    pw = pool_scale.shape[0]
    o = 0
    wz = w_in[:, o:o + di]; o += di
    wx = w_in[:, o:o + xbc]; o += xbc
    wdt = w_in[:, o:o + nh2]; o += nh2
    wp = w_in[:, o:o + pw]; o += pw
    wg = w_in[:, o:]
    w1 = jnp.concatenate([_perm_xbc(wx), wz, wg, wp, wdt, jnp.zeros((D, DT_PAD - nh2), w_in.dtype)], axis=1)
    pad128 = lambda v: jnp.concatenate([v.reshape(1, -1), jnp.zeros((1, 128 - v.size), F32)], axis=1)
    return dict(
        w_ada=w_ada, b_ada=b_ada.reshape(1, -1), g_mix=g_mix.reshape(1, -1), w1=w1,
        conv_w=_perm_xbc(conv_w), conv_b=_perm_xbc(conv_b.reshape(1, -1)),
        dt_bias=pad128(dt_bias), a_log=pad128(a_log),
        dskip=jnp.repeat(d_skip[0] + d_skip[1], SSD_HEADDIM).reshape(1, -1),
        ssd_norm_w=ssd_norm_w.reshape(1, -1), w_ssd_out=w_ssd_out, pool_w=pool_w,
        pool_scale=pool_scale.reshape(1, -1), w_pool_out=w_pool_out, w_out=w_out, g_ffn=g_ffn.reshape(1, -1),
        w_gate_up=w_gate_up, w_down=w_down)


def _unprep_layer_grads(g, dims):
    di, xbc, nh2, pw = dims
    dxbc, dz, dgs, dgp, dp, ddt = g["w1"]
    r = dxbc.reshape(SSD_GROUPS, xbc // SSD_GROUPS, dxbc.shape[1])
    xw = di // SSD_GROUPS
    parts = [r[:, :xw], r[:, xw:xw + SSD_STATE], r[:, xw + SSD_STATE:]]
    w_in_t = jnp.concatenate([dz] + [p.reshape(-1, dxbc.shape[1]) for p in parts] + [ddt[:nh2], dp, dgs, dgp], axis=0)
    nh = nh2 // 2
    dsk = g["dskip"].reshape(nh, SSD_HEADDIM).sum(axis=1)
    return dict(
        w_ada=g["w_ada"], b_ada=g["b_ada"].reshape(-1), g_mix=g["g_mix"].reshape(-1),
        w_in=w_in_t,
        conv_w=_unperm_xbc(g["conv_w"]), conv_b=_unperm_xbc(g["conv_b"]).reshape(-1),
        dt_bias=g["dt_bias"][0, :nh2].reshape(2, nh), a_log=g["a_log"][0, :nh2].reshape(2, nh),
        d_skip=jnp.stack([dsk, dsk]), ssd_norm_w=g["ssd_norm_w"].reshape(-1), w_ssd_out=g["w_ssd_out"],
        pool_w=g["pool_w"], pool_scale=g["pool_scale"].reshape(-1), w_pool_out=g["w_pool_out"], w_out=g["w_out"],
        g_ffn=g["g_ffn"].reshape(-1), w_gate_up=g["w_gate_up"], w_down=g["w_down"])


COND_ROWS = 16


def _split_mods(m):
    d = m.shape[1] // 6
    return [m[:2, k * d:(k + 1) * d].reshape(2, 1, d) for k in range(6)]


def _hosted(hosts, box, key):
    fn = (hosts or {}).get(key)
    return fn(box) if fn else None


def _layer_fwd(l, x, cond_s, w, rows, n_ctx, pc, hosts=None, box=None):
    T, D = x.shape
    nt, nct, tm = rows.nt, rows.nct, rows.tm
    n = lambda s: f"l{l}_{s}"
    crow = Rows(1, 0, COND_ROWS)
    mraw = matmul_nn(n("ada_mm"), cond_s, w["w_ada"])
    (m,) = stage_fwd(n("ada_bias"), f_bias, crow, [crow.row(mraw, mraw.shape[1]), crow.vec(w["b_ada"])],
                     [(mraw.shape[1], F32, False)])
    sh1, sc1, ga1, sh2, sc2, ga2 = _split_mods(m)

    (h1,) = stage_fwd(n("norm1"), f_norm_mod, rows,
                      [rows.row(x, D), rows.vec(w["g_mix"]), rows.segvec(sh1), rows.segvec(sc1)],
                      [(D, ACT_DTYPE, False)])
    ex = _hosted(hosts, box, "in_mm")
    proj = matmul_nn(n("in_mm"), h1, w["w1"], ex=ex)
    if ex is not None:
        proj, box["in_mm"] = proj
    xbc_w = w["conv_w"].shape[1]
    di = w["ssd_norm_w"].shape[1]
    pw = w["pool_scale"].shape[1]
    c_z, c_g, c_p, c_dt = xbc_w, xbc_w + di, xbc_w + di + 2 * pw, xbc_w + di + 3 * pw
    ex = _hosted(hosts, box, "conv")
    xbc, xres = conv_fwd(n("conv"), proj, w["conv_w"], w["conv_b"], n_ctx, xbc_w, ex)
    if ex is not None:
        box["conv"] = xres
    ex = _hosted(hosts, box, "ssd")
    y2, states, xres = ssd_fwd(n("ssd"), xbc, proj, c_dt // 128, w["dt_bias"], w["a_log"], n_ctx, ex)
    if ex is not None:
        box["ssd"] = xres

    G = SSD_GROUPS
    gw = di // G
    r8 = Rows(nt, nct, tm, G)
    gate_args = [r8.row(y2, gw, 0, True), r8.row(y2, gw, 0, True, roff=nt), r8.row(xbc, gw, 0, True, stride=2),
                 r8.row(proj, gw, c_z // gw, True), r8.vec(w["dskip"], True), r8.vec(w["ssd_norm_w"], True)]
    (ynw,) = stage_fwd(n("ssd_gate"), f_ssd_gate, r8, gate_args, [(gw, ACT_DTYPE, True)])
    o_ssd = matmul_nn(n("ssd_out_mm"), ynw, w["w_ssd_out"])

    nw = len(POOL_WINDOWS)
    pg = pw // nw
    r4 = Rows(nt, nct, tm, nw)
    pmat, pmat_t, inv_cnt = pc
    cblk = lambda a: Arg(a, (None, None) + a.shape[2:], lambda j, i, s: (s, j, 0, 0), "const")
    pool_args = [r4.row(proj, pg, c_p // pg, True), cblk(pmat), cblk(pmat_t), cblk(inv_cnt),
                 Arg(w["pool_w"], (None, pg, pg), lambda j, i, s: (j, 0, 0), "acc"), r4.vec(w["pool_scale"], True)]
    (ps,) = stage_fwd(n("pool"), f_pool, r4, pool_args, [(pg, ACT_DTYPE, True)])
    o_pool = matmul_nn(n("pool_out_mm"), ps, w["w_pool_out"])

    merge_args = [rows.row(o_ssd, D), rows.row(o_pool, D), rows.row(proj, pw, c_g // pw), rows.row(proj, pw, c_g // pw + 1)]
    (mg,) = stage_fwd(n("merge"), f_merge, rows, merge_args, [(D, ACT_DTYPE, False)])
    mo = matmul_nn(n("out_mm"), mg, w["w_out"])

    rn_args = [rows.row(x, D), rows.row(mo, D), rows.segvec(ga1), rows.vec(w["g_ffn"]), rows.segvec(sh2), rows.segvec(sc2)]
    x1, h2 = stage_fwd(n("norm2"), f_resid_norm_mod, rows, rn_args, [(D, F32, False), (D, ACT_DTYPE, False)])
    ex = _hosted(hosts, box, "gate_up_mm")
    gu = matmul_nn(n("gate_up_mm"), h2, w["w_gate_up"], ex=ex)
    if ex is not None:
        gu, box["gate_up_mm"] = gu
    fh = gu.shape[1] // 2
    (act,) = stage_fwd(n("swiglu"), f_swiglu, rows, [rows.row(gu, 2 * fh)], [(fh, ACT_DTYPE, False)])
    dn = matmul_nn(n("down_mm"), act, w["w_down"])
    res_args = [rows.row(x1, D), rows.row(dn, D), rows.segvec(ga2)]
    (x2,) = stage_fwd(n("resid2"), f_resid, rows, res_args, [(D, F32, False)])
    saved = dict(x=x, mraw=mraw, mods=(sh1, sc1, ga1, sh2, sc2, ga2), h1=h1, proj=proj, xbc=xbc, y2=y2, states=states,
                 ynw=ynw, o_ssd=o_ssd, ps=ps, o_pool=o_pool, mg=mg, mo=mo, x1=x1, h2=h2, gu=gu, act=act, dn=dn,
                 cols=(c_z, c_g, c_p, c_dt))
    return x2, saved


def f_norm_mod_keep(x, g, sh, sc):
    return f_norm_mod(x, g, sh, sc)[0], x


def _layer_bwd(l, dx2, cond_s, w, s, rows, n_ctx, pc, hosts=None, box=None):
    T, D = dx2.shape
    nt, nct, tm = rows.nt, rows.nct, rows.tm
    n = lambda t: f"l{l}_{t}_bwd"
    sh1, sc1, ga1, sh2, sc2, ga2 = s["mods"]
    c_z, c_g, c_p, c_dt = s["cols"]
    x, proj, xbc, y2, gu = s["x"], s["proj"], s["xbc"], s["y2"], s["gu"]
    g = {}
    if box is not None:
        box["g"] = g

    res_args = [rows.row(s["x1"], D), rows.row(s["dn"], D), rows.segvec(ga2)]
    res_args[0].kind = "const"
    dx1 = dx2
    ddn, dga2 = stage_bwd(n("resid2"), f_resid, rows, res_args, [rows.row(dx2, D)], [ACT_DTYPE])
    ex = _hosted(hosts, box, "down_dx")
    dact = matmul_nt(n("down_dx"), ddn, w["w_down"], ex=ex)
    if ex is not None:
        dact, box["down_dx"] = dact
    g["w_down"] = matmul_tn(n("down_dw"), s["act"], ddn)
    fh = gu.shape[1] // 2
    (dgu,) = stage_bwd(n("swiglu"), f_swiglu, rows, [rows.row(gu, 2 * fh)], [rows.row(dact, fh)], [ACT_DTYPE])
    dh2 = matmul_nt(n("gate_up_dx"), dgu, w["w_gate_up"])
    g["w_gate_up"] = matmul_tn(n("gate_up_dw"), s["h2"], dgu)

    rn_args = [rows.row(x, D), rows.row(s["mo"], D), rows.segvec(ga1), rows.vec(w["g_ffn"]), rows.segvec(sh2), rows.segvec(sc2)]
    dxr, dmo, dga1, g["g_ffn"], dsh2, dsc2 = stage_bwd(
        n("norm2"), f_resid_norm_mod, rows, rn_args, [rows.row(dx1, D), rows.row(dh2, D)], [F32, ACT_DTYPE])
    dmg = matmul_nt(n("out_dx"), dmo, w["w_out"])
    g["w_out"] = matmul_tn(n("out_dw"), s["mg"], dmo)

    pw = w["pool_scale"].shape[1]
    merge_args = [rows.row(s["o_ssd"], D), rows.row(s["o_pool"], D), rows.row(proj, pw, c_g // pw), rows.row(proj, pw, c_g // pw + 1)]
    do_ssd, do_pool, dgl_s, dgl_p = stage_bwd(n("merge"), f_merge, rows, merge_args, [rows.row(dmg, D)], [ACT_DTYPE] * 4)
    dps = matmul_nt(n("pool_out_dx"), do_pool, w["w_pool_out"])
    g["w_pool_out"] = matmul_tn(n("pool_out_dw"), s["ps"], do_pool)

    nw = len(POOL_WINDOWS)
    pg = pw // nw
    r4 = Rows(nt, nct, tm, nw)
    pmat, pmat_t, inv_cnt = pc
    cblk = lambda a: Arg(a, (None, None) + a.shape[2:], lambda j, i, s_: (s_, j, 0, 0), "const")
    pool_args = [r4.row(proj, pg, c_p // pg, True), cblk(pmat), cblk(pmat_t), cblk(inv_cnt),
                 Arg(w["pool_w"], (None, pg, pg), lambda j, i, s_: (j, 0, 0), "acc"), r4.vec(w["pool_scale"], True)]
    du_pool, g["pool_w"], g["pool_scale"] = stage_bwd(n("pool"), f_pool, r4, pool_args, [r4.row(dps, pg, 0, True)], [ACT_DTYPE])

    dynw = matmul_nt(n("ssd_out_dx"), do_ssd, w["w_ssd_out"])
    g["w_ssd_out"] = matmul_tn(n("ssd_out_dw"), s["ynw"], do_ssd)
    G = SSD_GROUPS
    di = w["ssd_norm_w"].shape[1]
    gw = di // G
    r8 = Rows(nt, nct, tm, G)
    gate_args = [r8.row(y2, gw, 0, True), r8.row(y2, gw, 0, True, roff=nt), r8.row(xbc, gw, 0, True, stride=2),
                 r8.row(proj, gw, c_z // gw, True), r8.vec(w["dskip"], True), r8.vec(w["ssd_norm_w"], True)]
    gate_args[1].kind = "const"
    dy, dxs_skip, dz, g["dskip"], g["ssd_norm_w"] = stage_bwd(
        n("ssd_gate"), f_ssd_gate, r8, gate_args, [r8.row(dynw, gw, 0, True)], [F32, F32, ACT_DTYPE])

    ex = _hosted(hosts, box, "ssd")
    dxbc2, ddt, g["dt_bias"], g["a_log"], xres = ssd_bwd(n("ssd"), xbc, proj, c_dt // 128, w["dt_bias"], w["a_log"],
                                                         s["states"], dy, n_ctx, ex)
    if ex is not None:
        box["ssd"] = xres
    xbc_w = xbc.shape[1]
    ex = _hosted(hosts, box, "conv")
    dxbc_raw, g["conv_w"], g["conv_b"], xres = conv_bwd(n("conv"), proj, w["conv_w"], w["conv_b"], dxbc2, dxs_skip,
                                                         n_ctx, xbc_w, ex)
    if ex is not None:
        box["conv"] = xres
    pieces = [dxbc_raw, dz, dgl_s, dgl_p, du_pool, ddt]
    offsets = [0, c_z, c_g, c_g + pw, c_p, c_dt]
    ex = _hosted(hosts, box, "in_dx")
    dh1 = matmul_nt(n("in_dx"), pieces, w["w1"], ex=ex, offsets=offsets)
    if ex is not None:
        dh1, box["in_dx"] = dh1
    ex = _hosted(hosts, box, "in_dw")
    first = matmul_tn(n("in_dw0"), pieces[0], s["h1"], ex=ex)
    if ex is not None:
        first, box["in_dw"] = first
    g["w1"] = [first] + [matmul_tn(n(f"in_dw{k}"), p, s["h1"]) for k, p in enumerate(pieces) if k]

    n1_args = [rows.row(x, D), rows.vec(w["g_mix"]), rows.segvec(sh1), rows.segvec(sc1)]
    dx, g["g_mix"], dsh1, dsc1 = stage_bwd(n("norm1"), f_norm_mod_keep, rows, n1_args,
                                           [rows.row(dh1, D), rows.row(dxr, D)], [F32])

    dm = jnp.concatenate([v.reshape(2, D) for v in (dsh1, dsc1, dga1, dsh2, dsc2, dga2)], axis=1)
    dm = jnp.concatenate([dm, jnp.zeros((COND_ROWS - 2, dm.shape[1]), F32)], axis=0)
    crow = Rows(1, 0, COND_ROWS)
    dmraw, g["b_ada"] = stage_bwd(n("ada_bias"), f_bias, crow, [crow.row(s["mraw"], dm.shape[1]), crow.vec(w["b_ada"])],
                                  [crow.row(dm, dm.shape[1])], [ACT_DTYPE])
    dcs = matmul_nt(n("ada_dx"), dmraw, w["w_ada"])
    g["w_ada"] = matmul_tn(n("ada_dw"), cond_s, dmraw)
    return dx, dcs, g


def local_step(x, ctx, c, c_ctx, target, layer_w_fn, n_layers, g_final, fwd_hosts=None, bwd_hosts=None):
    L, D = x.shape
    n_ctx = ctx.shape[0]
    tm = ROW_TILE
    T = L + n_ctx
    rows = Rows(T // tm, n_ctx // tm, tm)
    pc = _pool_consts(tm, n_ctx)
    xa = jnp.concatenate([ctx, x], axis=0)
    cond = jnp.concatenate([c_ctx.reshape(1, D), c.reshape(1, D), jnp.zeros((COND_ROWS - 2, D), F32)], axis=0)
    crow = Rows(1, 0, COND_ROWS)
    (cond_s,) = stage_fwd("cond_silu", f_silu, crow, [crow.row(cond, D)], [(D, ACT_DTYPE, False)])

    saved, layer_w = [], []
    for l in range(n_layers):
        layer_w.append(layer_w_fn(l))
        box = {}
        xa, s = _layer_fwd(l, xa, cond_s, layer_w[l], rows, n_ctx, pc, fwd_hosts(l, box) if fwd_hosts else None, box)
        saved.append(s)

    rl = Rows(L // tm, 0, tm)
    gf = g_final.reshape(1, D)
    tgt = rl.row(target, D)
    tgt.kind = "const"
    loss_args = [rl.row(xa, D, roff=n_ctx // tm), tgt, rl.vec(gf)]
    (loss_rows,) = stage_fwd("loss", f_loss, rl, loss_args, [(1, F32, False)])
    ones = jnp.ones((L, 1), F32)
    dx_lat, dgf = stage_bwd("loss_bwd", f_loss, rl, loss_args, [rl.row(ones, 1)], [F32])
    loss = jnp.sum(loss_rows)
    dx = jnp.concatenate([jnp.zeros((n_ctx, D), F32), dx_lat], axis=0)

    grads = [None] * n_layers
    dcs = jnp.zeros((COND_ROWS, D), F32)
    for l in reversed(range(n_layers)):
        box = {}
        hosts = bwd_hosts(l, grads, box) if bwd_hosts else None
        dx, dcs_l, grads[l] = _layer_bwd(l, dx, cond_s, layer_w[l], saved[l], rows, n_ctx, pc, hosts, box)
        dcs = dcs + dcs_l
    (dcond,) = stage_bwd("cond_silu_bwd", f_silu, crow, [crow.row(cond, D)], [crow.row(dcs, D)], [F32])
    return loss, dx[n_ctx:], grads, dcond[0], dgf


def gather_chips(halves, conv=None):
    n = len(halves)
    ops = list(halves) + ([conv] if conv is not None else [])

    def copies(ins, outs, pos):
        c, me = pos[2], _chip_index(pos)
        pairs = [(s.at[c], o.at[me, c]) for s, o in zip(ins[:n], outs[:n])]
        pairs += [(s, o.at[me]) for s, o in zip(ins[n:], outs[n:])]
        return pairs, [(s, d, _flip(pos, rel)) for rel in PLANE for s, d in pairs]

    shapes = [jax.ShapeDtypeStruct((4,) + s.shape, s.dtype) for s in ops]
    return Exchange(copies, 3 * len(ops), len(ops), ops, shapes)


def gather_pair(gathered):
    n = len(gathered)

    def copies(ins, outs, pos):
        c = pos[2]
        return [], [(s.at[b, c], o.at[b, c], _flip(pos, PAIR[0])) for s, o in zip(ins, outs) for b in range(4)]

    shapes = [jax.ShapeDtypeStruct(g.shape, g.dtype) for g in gathered]
    return Exchange(copies, 4 * n, 0, gathered, shapes, aliases={k: k for k in range(n)})


def swap_halves(grads):
    n = len(grads)

    def copies(ins, outs, pos):
        c = pos[2]
        return [], [(g.at[b, 1 - c], o.at[b], _flip(pos, PAIR[0])) for g, o in zip(ins, outs) for b in range(4)]

    shapes = [jax.ShapeDtypeStruct((g.shape[0],) + g.shape[2:], g.dtype) for g in grads]
    return Exchange(copies, 4 * n, 0, grads, shapes)


def scatter_chips(sums):
    n = len(sums)

    def copies(ins, outs, pos):
        me = _chip_index(pos)
        local = [(p.at[me], o.at[me]) for p, o in zip(ins, outs)]
        remote = []
        for rel in PLANE:
            peer = _flip(pos, rel)
            remote += [(p.at[_chip_index(peer)], o.at[me], peer) for p, o in zip(ins, outs)]
        return local, remote

    shapes = [jax.ShapeDtypeStruct(p.shape, p.dtype) for p in sums]
    return Exchange(copies, 3 * n, n, sums, shapes)


def share_halves(finals):
    n = len(finals)

    def copies(ins, outs, pos):
        c = pos[2]
        return [], [(f.at[c], o.at[c], _flip(pos, PAIR[0])) for f, o in zip(ins, outs)]

    shapes = [jax.ShapeDtypeStruct(f.shape, f.dtype) for f in finals]
    return Exchange(copies, n, 0, finals, shapes, aliases={k: k for k in range(n)})


def gather_everyone(vec):
    def copies(ins, outs, pos):
        me = _device_index(pos)
        (v,), (o,) = ins, outs
        return [(v, o.at[me])], [(v, o.at[me], _flip(pos, rel)) for rel in EVERYONE]

    return Exchange(copies, len(EVERYONE), 1, [vec], [jax.ShapeDtypeStruct((8,) + vec.shape, vec.dtype)])


def _row_tile(rows, cols, n_bufs, mult=8):
    cap = VMEM_LIMIT_BYTES // 4 // (2 * n_bufs * cols * 4)
    for t in range(min(rows, cap) // mult * mult, 0, -mult):
        if rows % t == 0:
            return t
    return rows


WIRE_DTYPE = jnp.bfloat16


def add_own_half(name, grads, recv, c):
    nb, _, R, C = grads.shape
    tr = _row_tile(R, C, 3, mult=16)

    def body(c_ref, g_ref, r_ref, o_ref):
        o_ref[...] = (g_ref[...] + r_ref[...]).astype(o_ref.dtype)

    spec = pl.BlockSpec((None, tr, C), lambda b, i, c_ref: (b, i, 0))
    return _pcall(
        body, name=name, out_shape=jax.ShapeDtypeStruct(recv.shape, WIRE_DTYPE),
        grid_spec=pltpu.PrefetchScalarGridSpec(
            num_scalar_prefetch=1, grid=(nb, R // tr),
            in_specs=[pl.BlockSpec((None, None, tr, C), lambda b, i, c_ref: (b, c_ref[0], i, 0)), spec],
            out_specs=spec),
        compiler_params=_params("parallel", "parallel"),
    )(c, grads, recv)


def sum_slots(name, a, c=None):
    n, R, C = a.shape
    tr = _row_tile(R, C, n + 1, mult=16 if a.dtype.itemsize == 2 else 8)

    def body(*refs):
        a_ref, o_ref = refs[-2:]
        acc = a_ref[0].astype(F32)
        for k in range(1, n):
            acc = acc + a_ref[k].astype(F32)
        o_ref[...] = acc

    if c is None:
        return _pcall(
            body, name=name, out_shape=jax.ShapeDtypeStruct((R, C), F32), grid=(R // tr,),
            in_specs=[pl.BlockSpec((n, tr, C), lambda i: (0, i, 0))], out_specs=pl.BlockSpec((tr, C), lambda i: (i, 0)),
            compiler_params=_params("parallel"),
        )(a)
    return _pcall(
        body, name=name, out_shape=jax.ShapeDtypeStruct((2, R, C), F32),
        grid_spec=pltpu.PrefetchScalarGridSpec(
            num_scalar_prefetch=1, grid=(R // tr,),
            in_specs=[pl.BlockSpec((n, tr, C), lambda i, c_ref: (0, i, 0))],
            out_specs=pl.BlockSpec((None, tr, C), lambda i, c_ref: (c_ref[0], i, 0))),
        compiler_params=_params("parallel"),
    )(c, a)


def adamw(name, w, g_layers, m, v):
    nl, R, C = w.shape
    assert len(g_layers) == nl
    tr = _row_tile(R, C, 8 + nl)
    nr = R // tr

    def body(*refs):
        w_ref, m_ref, v_ref = refs[:3]
        g_refs = refs[3:3 + nl]
        go_ref, d_ref, nm_ref, nv_ref = refs[3 + nl:]
        l = pl.program_id(0)
        gr = g_refs[0][...]
        for k in range(1, nl):
            gr = jnp.where(l == k, g_refs[k][...], gr)
        nm = ADAM_B1 * m_ref[...] + (1.0 - ADAM_B1) * gr
        nv = ADAM_B2 * v_ref[...] + (1.0 - ADAM_B2) * jnp.square(gr)
        m_hat = nm / (1.0 - ADAM_B1 ** ADAM_STEP)
        v_hat = nv / (1.0 - ADAM_B2 ** ADAM_STEP)
        d_ref[...] = -ADAM_LR * (m_hat / (jnp.sqrt(v_hat) + ADAM_EPS) + ADAM_WD * w_ref[...])
        go_ref[...] = gr
        nm_ref[...] = nm
        nv_ref[...] = nv

    spec = pl.BlockSpec((None, tr, C), lambda l, i: (l, i, 0))
    g_specs = [pl.BlockSpec((tr, C), (lambda l, i, k=k: (jnp.where(l == k, i, jnp.where(l < k, 0, nr - 1)), 0)))
               for k in range(nl)]
    return _pcall(
        body, name=name, out_shape=[jax.ShapeDtypeStruct((nl, R, C), F32)] * 4, grid=(nl, nr),
        in_specs=[spec] * 3 + g_specs, out_specs=[spec] * 4, compiler_params=_params("arbitrary", "arbitrary"),
    )(w, m, v, *g_layers)


BIG = ("w_ada", "w_in", "w_ssd_out", "pool_w", "w_pool_out", "w_out", "w_gate_up", "w_down")
COL_SHARDED = ("w_ada", "w_in", "w_gate_up")
GRAD_TRANSPOSED = ("w_in",)
FIRST_USED = ("w_ada", "w_in")
LATER_USED = tuple(k for k in BIG if k not in FIRST_USED)
READY_LAST = FIRST_USED
READY_EARLY = LATER_USED
SMALL = ("c_ctx", "b_ada", "g_mix", "conv_w", "conv_b", "dt_bias", "a_log", "d_skip", "ssd_norm_w", "pool_scale",
         "g_ffn", "g_final")
WEIGHTS = ("c_ctx", "w_ada", "b_ada", "g_mix", "w_in", "conv_w", "conv_b", "dt_bias", "a_log", "d_skip", "ssd_norm_w",
           "w_ssd_out", "pool_w", "pool_scale", "w_pool_out", "w_out", "g_ffn", "w_gate_up", "w_down", "g_final")
LAYER_KEYS = ("w_ada", "b_ada", "g_mix", "w_in", "conv_w", "conv_b", "dt_bias", "a_log", "d_skip", "ssd_norm_w",
              "w_ssd_out", "pool_w", "pool_scale", "w_pool_out", "w_out", "g_ffn", "w_gate_up", "w_down")


def _shard2d(name, a):
    if name == "pool_w":
        return a.reshape(a.shape[0], a.shape[1] * a.shape[2], a.shape[3])
    return a


def _full_from_blocks(name, a):
    nb, R, C = a.shape
    if name in COL_SHARDED:
        return jnp.transpose(a, (1, 0, 2)).reshape(R, nb * C)
    if name == "pool_w":
        nw = len(POOL_WINDOWS)
        return jnp.transpose(a.reshape(nb, nw, R // nw, C), (1, 0, 2, 3)).reshape(nw, nb * R // nw, C)
    return a.reshape(nb * R, C)


def _blocks_from_full(name, g):
    nb = 4
    if name in COL_SHARDED and name not in GRAD_TRANSPOSED:
        K, N = g.shape
        return jnp.transpose(g.reshape(K, nb, N // nb), (1, 0, 2))
    if name == "pool_w":
        nw, r, C = g.shape
        return jnp.transpose(g.reshape(nw, nb, r // nb, C), (1, 0, 2, 3)).reshape(nb, nw * r // nb, C)
    return g.reshape(nb, g.shape[0] // nb, g.shape[1])


def _pack(arrs, rows):
    flat = jnp.concatenate([a.reshape(-1).astype(F32) for a in arrs])
    return jnp.concatenate([flat, jnp.zeros((rows * 128 - flat.size,), F32)]).reshape(rows, 128)


def _unpack(vec, shapes):
    flat = vec.reshape(-1)
    out, o = [], 0
    for s in shapes:
        n = int(np.prod(s))
        out.append(flat[o:o + n].reshape(s))
        o += n
    return out


def _rows_for(shapes):
    n = sum(int(np.prod(s)) for s in shapes)
    return -(-n // (8 * 128)) * 8


def kernel(x, c, ctx, c_ctx, w_ada, b_ada, g_mix, w_in, conv_w, conv_b, dt_bias, a_log, d_skip, ssd_norm_w, w_ssd_out, pool_w, pool_scale, w_pool_out, w_out, g_ffn, w_gate_up, w_down, g_final, loss_target, m_c_ctx, m_w_ada, m_b_ada, m_g_mix, m_w_in, m_conv_w, m_conv_b, m_dt_bias, m_a_log, m_d_skip, m_ssd_norm_w, m_w_ssd_out, m_pool_w, m_pool_scale, m_w_pool_out, m_w_out, m_g_ffn, m_w_gate_up, m_w_down, m_g_final, v_c_ctx, v_w_ada, v_b_ada, v_g_mix, v_w_in, v_conv_w, v_conv_b, v_dt_bias, v_a_log, v_d_skip, v_ssd_norm_w, v_w_ssd_out, v_pool_w, v_pool_scale, v_w_pool_out, v_w_out, v_g_ffn, v_w_gate_up, v_w_down, v_g_final):
    w = dict(c_ctx=c_ctx, w_ada=w_ada, b_ada=b_ada, g_mix=g_mix, w_in=w_in, conv_w=conv_w, conv_b=conv_b, dt_bias=dt_bias,
             a_log=a_log, d_skip=d_skip, ssd_norm_w=ssd_norm_w, w_ssd_out=w_ssd_out, pool_w=pool_w, pool_scale=pool_scale,
             w_pool_out=w_pool_out, w_out=w_out, g_ffn=g_ffn, w_gate_up=w_gate_up, w_down=w_down, g_final=g_final)
    m = dict(c_ctx=m_c_ctx, w_ada=m_w_ada, b_ada=m_b_ada, g_mix=m_g_mix, w_in=m_w_in, conv_w=m_conv_w, conv_b=m_conv_b,
             dt_bias=m_dt_bias, a_log=m_a_log, d_skip=m_d_skip, ssd_norm_w=m_ssd_norm_w, w_ssd_out=m_w_ssd_out,
             pool_w=m_pool_w, pool_scale=m_pool_scale, w_pool_out=m_w_pool_out, w_out=m_w_out, g_ffn=m_g_ffn,
             w_gate_up=m_w_gate_up, w_down=m_w_down, g_final=m_g_final)
    v = dict(c_ctx=v_c_ctx, w_ada=v_w_ada, b_ada=v_b_ada, g_mix=v_g_mix, w_in=v_w_in, conv_w=v_conv_w, conv_b=v_conv_b,
             dt_bias=v_dt_bias, a_log=v_a_log, d_skip=v_d_skip, ssd_norm_w=v_ssd_norm_w, w_ssd_out=v_w_ssd_out,
             pool_w=v_pool_w, pool_scale=v_pool_scale, w_pool_out=v_w_pool_out, w_out=v_w_out, g_ffn=v_g_ffn,
             w_gate_up=v_w_gate_up, w_down=v_w_down, g_final=v_g_final)
    assert x.shape[0] == 1, "one example per device"
    pos = _position()
    core = pos[2].astype(jnp.int32).reshape(1)
    n_layers = w_in.shape[0]
    assert n_layers == 2
    dims = (ssd_norm_w.shape[1], conv_w.shape[2] * 4, dt_bias[0].size, pool_scale.shape[1])
    shard = {k: _shard2d(k, w[k]) for k in BIG}

    def halves(a):
        return a.reshape(a.shape[:-2] + (2, a.shape[-2] // 2, a.shape[-1]))

    def whole(a):
        return a.reshape(a.shape[:-3] + (2 * a.shape[-2], a.shape[-1]))

    def wire_shards(l, names):
        return [halves(shard[k][l].astype(MXU_DTYPE)) for k in names]

    def full_weights(names, gathered):
        return {k: _full_from_blocks(k, whole(a)) for k, a in zip(names, gathered)}

    first = comm_call("gather0_chips", gather_chips(wire_shards(0, FIRST_USED), conv=conv_w))
    got0 = full_weights(FIRST_USED, comm_call("gather0_pair", gather_pair(first[:-1])))
    conv_all = first[-1]
    conv_full = [jnp.transpose(conv_all[:, l], (1, 0, 2)).reshape(conv_all.shape[2], -1) for l in range(n_layers)]

    boxes = {}

    def layer_w_fn(l):
        if l == 0:
            full = dict(got0)
            late = {k: None for k in LATER_USED}
        else:
            full = full_weights(BIG, boxes[("fwd", 0)]["gate_up_mm"])
            late = {}
        full["conv_w"] = conv_full[l]
        lw = LazyDict(_prep_layer_weights(*[full[k] if k in full else (None if k in late else w[k][l]) for k in LAYER_KEYS]))
        for i, k in enumerate(late):
            lw[k] = (lambda i=i, k=k: _full_from_blocks(k, whole(boxes[("fwd", 0)]["conv"][i])))
        return lw

    def fwd_hosts(l, box):
        boxes[("fwd", l)] = box
        if l != 0:
            return None
        return {"in_mm": lambda box: gather_chips(wire_shards(0, LATER_USED)), "conv": lambda box: gather_pair(box["in_mm"]),
                "ssd": lambda box: gather_chips(wire_shards(1, BIG)), "gate_up_mm": lambda box: gather_pair(box["ssd"])}

    def blocks(gl, names):
        return [halves(_blocks_from_full(k, gl[k])) for k in names]

    def pair_sums(tag, names, G, recv):
        return [add_own_half(f"pair_sum{tag}_{k}", g, r, core) for k, g, r in zip(names, G, recv)]

    def chip_sums(tag, names, parts):
        return [sum_slots(f"chip_sum{tag}_{k}", p, core) for k, p in zip(names, parts)]

    def reduce_now(tag, gl, names):
        G = blocks(gl, names)
        pair = pair_sums(tag, names, G, comm_call(f"swap{tag}", swap_halves(G)))
        fin = chip_sums(tag, names, comm_call(f"scatter{tag}", scatter_chips(pair)))
        return [whole(a) for a in comm_call(f"share{tag}", share_halves(fin))]

    small_layers = {}
    n_big = len(BIG)

    def bwd_hosts(l, grads, box):
        boxes[("bwd", l)] = box
        if l != 0:
            return None
        gl1 = _unprep_layer_grads(grads[1], dims)
        small_layers[1] = gl1
        G1 = blocks(gl1, BIG)
        early = {}

        def scan_host(box):
            early["G"] = blocks(box["g"], READY_EARLY)
            return combine(scatter_chips(pair_sums("1", BIG, G1, box["down_dx"])), swap_halves(early["G"]))

        def conv_host(box):
            return scatter_chips(pair_sums("0e", READY_EARLY, early["G"], box["ssd"][n_big:]))

        return {"down_dx": lambda box: swap_halves(G1), "ssd": scan_host, "conv": conv_host,
                "in_dx": lambda box: share_halves(chip_sums("1", BIG, box["ssd"][:n_big])),
                "in_dw": lambda box: share_halves(chip_sums("0e", READY_EARLY, box["conv"]))}

    loss, grad_x, grads, d_c_ctx, d_g_final = local_step(
        x[0], ctx[0], c[0], c_ctx, loss_target[0], layer_w_fn, n_layers, g_final, fwd_hosts, bwd_hosts)
    loss = lax.psum(loss, ("x", "y", "c"))
    reduced1 = [whole(a) for a in boxes[("bwd", 0)]["in_dx"]]
    gl0 = _unprep_layer_grads(grads[0], dims)
    small_layers[0] = gl0
    red0 = dict(zip(READY_EARLY, [whole(a) for a in boxes[("bwd", 0)]["in_dw"]]))
    red0.update(zip(READY_LAST, reduce_now("0", gl0, READY_LAST)))
    reduced0 = [red0[k] for k in BIG]

    small_full = dict(c_ctx=d_c_ctx, g_final=d_g_final.reshape(-1))
    for k in SMALL:
        if k not in small_full:
            small_full[k] = jnp.stack([small_layers[l][k] for l in range(n_layers)])
    shapes = [small_full[k].shape for k in SMALL]
    packed = _pack([small_full[k] for k in SMALL], _rows_for(shapes))
    total = sum_slots("small_sum", comm_call("gather_small", gather_everyone(packed))[0])
    small_g = dict(zip(SMALL, _unpack(total, shapes)))
    cw = conv_w.shape[2]
    small_g["conv_w"] = lax.dynamic_slice_in_dim(small_g["conv_w"], _chip_index(pos) * cw, cw, axis=2)

    grad, delta, new_m, new_v = {}, {}, {}, {}
    for k, g0, g1 in zip(BIG, reduced0, reduced1):
        shp = w[k].shape
        if k in GRAD_TRANSPOSED:
            flat = lambda a: jnp.swapaxes(a, 1, 2)
            back = lambda a: jnp.swapaxes(a, 1, 2)
        else:
            flat = lambda a: _shard2d(k, a)
            back = lambda a: a.reshape(shp)
        outs = adamw(f"adamw_{k}", flat(w[k]), [g0, g1], flat(m[k]), flat(v[k]))
        grad[k], delta[k], new_m[k], new_v[k] = [back(a) for a in outs]
    sshapes = [w[k].shape for k in SMALL]
    srows = _rows_for(sshapes)
    pk = lambda d: _pack([d[k] for k in SMALL], srows)[None]
    _, d_, m_, v_ = adamw("adamw_small", pk(w), [pk(small_g)[0]], pk(m), pk(v))
    for k, dd, mm, vv in zip(SMALL, _unpack(d_, sshapes), _unpack(m_, sshapes), _unpack(v_, sshapes)):
        grad[k], delta[k], new_m[k], new_v[k] = small_g[k], dd, mm, vv

    return (loss, grad_x[None], *[grad[k] for k in WEIGHTS], *[delta[k] for k in WEIGHTS],
            *[new_m[k] for k in WEIGHTS], *[new_v[k] for k in WEIGHTS])
```

```python
import functools

import jax
import jax.numpy as jnp
import numpy as np
from jax import lax
from jax.experimental import pallas as pl
from jax.experimental.pallas import tpu as pltpu

F32 = jnp.float32
MXU_DTYPE = jnp.bfloat16
ACT_DTYPE = jnp.bfloat16
VMEM_LIMIT_BYTES = 48 * 1024 * 1024
EPS = 1e-6
NEG = -1e30

SSD_HEADDIM = 64
SSD_GROUPS = 8
SSD_STATE = 128
SSD_CHUNK = 128
SSD_GROUPS_PER_STEP = 8
SSD_CONV = 5
GRID_W = 64
POOL_WINDOWS = (2, 4, 8, 16)
ROW_TILE = 256
DT_PAD = 512

ADAM_LR = 0.001
ADAM_B1 = 0.9
ADAM_B2 = 0.999
ADAM_EPS = 1e-08
ADAM_WD = 0.01
ADAM_STEP = 10

MESH = pl.DeviceIdType.MESH


def _pcall(body, **kw):
    return pl.pallas_call(body, **kw)


def _params(*sem):
    return pltpu.CompilerParams(dimension_semantics=tuple(sem), vmem_limit_bytes=VMEM_LIMIT_BYTES)


def _pick_tile(n, cands):
    for t in cands:
        if n % t == 0:
            return t
    return n


PLANE = ((1, 0, 0), (0, 1, 0), (1, 1, 0))
PAIR = ((0, 0, 1),)
EVERYONE = tuple((a, b, d) for a in (0, 1) for b in (0, 1) for d in (0, 1) if a + b + d)
HBM = pl.BlockSpec(memory_space=pl.ANY)


def _position():
    return lax.axis_index("x"), lax.axis_index("y"), lax.axis_index("c")


def _flip(pos, rel):
    return tuple(1 - p if r else p for p, r in zip(pos, rel))


def _chip_index(pos):
    return 2 * pos[0] + pos[1]


def _device_index(pos):
    return 4 * pos[0] + 2 * pos[1] + pos[2]


class Exchange:
    def __init__(self, copies, n_remote, n_local, operands, out_shapes, aliases=None):
        self.copies, self.n_remote, self.n_local = copies, n_remote, n_local
        self.operands, self.out_shapes, self.aliases = list(operands), list(out_shapes), dict(aliases or {})

    def scratch(self):
        return [pltpu.SemaphoreType.DMA((max(self.n_remote, 1),)), pltpu.SemaphoreType.DMA((max(self.n_remote, 1),)),
                pltpu.SemaphoreType.DMA((max(self.n_local, 1),))]

    def descriptors(self, ins, outs, sems):
        send_sems, recv_sems, local_sems = sems
        local, remote = self.copies(ins, outs, _position())
        assert len(local) == self.n_local and len(remote) == self.n_remote
        cps = [pltpu.make_async_copy(src, dst, local_sems.at[k]) for k, (src, dst) in enumerate(local)]
        cps += [pltpu.make_async_remote_copy(src_ref=src, dst_ref=dst, send_sem=send_sems.at[k], recv_sem=recv_sems.at[k],
                                             device_id=peer, device_id_type=MESH) for k, (src, dst, peer) in enumerate(remote)]
        return cps


def combine(a, b):
    na, nao = len(a.operands), len(a.out_shapes)

    def copies(ins, outs, pos):
        la, ra = a.copies(ins[:na], outs[:nao], pos)
        lb, rb = b.copies(ins[na:], outs[nao:], pos)
        return la + lb, ra + rb

    aliases = dict(a.aliases)
    aliases.update({na + k: nao + v for k, v in b.aliases.items()})
    return Exchange(copies, a.n_remote + b.n_remote, a.n_local + b.n_local, a.operands + b.operands,
                    a.out_shapes + b.out_shapes, aliases)


class LazyDict(dict):
    def __getitem__(self, key):
        v = dict.__getitem__(self, key)
        if callable(v):
            v = v()
            dict.__setitem__(self, key, v)
        return v


def comm_call(name, ex):
    n_in, n_out = len(ex.operands), len(ex.out_shapes)

    def body(*refs):
        cps = ex.descriptors(refs[:n_in], refs[n_in:n_in + n_out], refs[n_in + n_out:])
        for cp in cps:
            cp.start()
        for cp in cps:
            cp.wait()

    return _pcall(
        body, name=name, out_shape=ex.out_shapes, in_specs=[HBM] * n_in, out_specs=[HBM] * n_out,
        scratch_shapes=ex.scratch(), input_output_aliases=ex.aliases,
        compiler_params=pltpu.CompilerParams(has_side_effects=True),
    )(*ex.operands)


def hosted_call(body, ex, operands, *, name, out_shape, grid, in_specs, out_specs, scratch_shapes=()):
    n_in, n_out, n_scr = len(operands), len(out_shape), len(scratch_shapes)
    sem = ("arbitrary",) * len(grid)
    if ex is None:
        res = _pcall(body, name=name, out_shape=list(out_shape), grid=grid, in_specs=list(in_specs),
                     out_specs=list(out_specs), scratch_shapes=list(scratch_shapes), compiler_params=_params(*sem))(*operands)
        return res, []
    x_in, x_out = len(ex.operands), len(ex.out_shapes)

    def wrapped(*refs):
        o = 0
        ins = refs[o:o + n_in]; o += n_in
        xins = refs[o:o + x_in]; o += x_in
        outs = refs[o:o + n_out]; o += n_out
        xouts = refs[o:o + x_out]; o += x_out
        scr = refs[o:o + n_scr]; o += n_scr
        sems = refs[o:]
        first = last = None
        for a, n in enumerate(grid):
            i = pl.program_id(a)
            first = (i == 0) if first is None else first & (i == 0)
            last = (i == n - 1) if last is None else last & (i == n - 1)

        @pl.when(first)
        def _():
            for cp in ex.descriptors(xins, xouts, sems):
                cp.start()

        body(*ins, *outs, *scr)

        @pl.when(last)
        def _():
            for cp in ex.descriptors(xins, xouts, sems):
                cp.wait()

    aliases = {n_in + k: n_out + v for k, v in ex.aliases.items()}
    res = _pcall(
        wrapped, name=name, out_shape=list(out_shape) + ex.out_shapes, grid=grid,
        in_specs=list(in_specs) + [HBM] * x_in, out_specs=list(out_specs) + [HBM] * x_out,
        scratch_shapes=list(scratch_shapes) + ex.scratch(), input_output_aliases=aliases,
        compiler_params=pltpu.CompilerParams(dimension_semantics=sem, vmem_limit_bytes=VMEM_LIMIT_BYTES,
                                             has_side_effects=True),
    )(*operands, *ex.operands)
    return res[:n_out], res[n_out:]


def _dot(a, b, dims):
    return lax.dot_general(a.astype(MXU_DTYPE), b.astype(MXU_DTYPE), (dims, ((), ())), preferred_element_type=F32)


_NN = ((1,), (0,))
_NT = ((1,), (1,))
_TN = ((0,), (0,))


@jax.custom_vjp
def _mm(a, b):
    return _dot(a, b, _NN)


def _mm_fwd(a, b):
    return _mm(a, b), (a, b)


def _mm_bwd(res, g):
    a, b = res
    return _dot(g, b, _NT).astype(a.dtype), _dot(a, g, _TN).astype(b.dtype)


_mm.defvjp(_mm_fwd, _mm_bwd)


@jax.custom_vjp
def _mm_nt(a, b):
    return _dot(a, b, _NT)


def _mm_nt_fwd(a, b):
    return _mm_nt(a, b), (a, b)


def _mm_nt_bwd(res, g):
    a, b = res
    return _dot(g, b, _NN).astype(a.dtype), _dot(g, a, _TN).astype(b.dtype)


_mm_nt.defvjp(_mm_nt_fwd, _mm_nt_bwd)


@jax.custom_vjp
def _mm_tn(a, b):
    return _dot(a, b, _TN)


def _mm_tn_fwd(a, b):
    return _mm_tn(a, b), (a, b)


def _mm_tn_bwd(res, g):
    a, b = res
    return _dot(b, g, _NT).astype(a.dtype), _dot(a, g, _NN).astype(b.dtype)


_mm_tn.defvjp(_mm_tn_fwd, _mm_tn_bwd)


def _dot_exact(m01, v):
    m = m01.astype(jnp.bfloat16)
    hi = v.astype(jnp.bfloat16)
    r1 = v - hi.astype(F32)
    mid = r1.astype(jnp.bfloat16)
    lo = (r1 - mid.astype(F32)).astype(jnp.bfloat16)
    out = jnp.dot(m, hi, preferred_element_type=F32)
    out = out + jnp.dot(m, mid, preferred_element_type=F32)
    return out + jnp.dot(m, lo, preferred_element_type=F32)


@jax.custom_vjp
def _lin01(m, mt, v):
    return _dot_exact(m, v)


def _lin01_fwd(m, mt, v):
    return _dot_exact(m, v), (m, mt)


def _lin01_bwd(res, g):
    m, mt = res
    return jnp.zeros_like(m), jnp.zeros_like(mt), _dot_exact(mt, g)


_lin01.defvjp(_lin01_fwd, _lin01_bwd)


MATMUL_VMEM_BUDGET = VMEM_LIMIT_BYTES * 3 // 4
MATMUL_MIN_STEPS = 16


def _mm_tiles(m, n, k_bytes_a, k_bytes_b, out_bytes, cands_m, cands_n):
    best = deep = None
    for tm in cands_m:
        if m % tm:
            continue
        for tn in cands_n:
            if n % tn:
                continue
            need = 2 * (tm * k_bytes_a + tn * k_bytes_b + tm * tn * out_bytes)
            if need > MATMUL_VMEM_BUDGET:
                continue
            if best is None or tm * tn > best[0] * best[1]:
                best = (tm, tn)
            if (m // tm) * (n // tn) >= MATMUL_MIN_STEPS and (deep is None or tm * tn > deep[0] * deep[1]):
                deep = (tm, tn)
    assert best is not None, (m, n)
    return deep or best


_ROW_CANDS = (4352, 2176, 1088, 768, 544, 512, 272, 256, 128, 16)
_COL_CANDS = (2816, 2048, 1408, 1024, 512, 256, 128)


def _one(res, xres, ex):
    return res[0] if ex is None else (res[0], xres)


def matmul_nn(name, a, b, out_dtype=F32, ex=None):
    M, K = a.shape
    N = b.shape[1]
    tm, tn = _mm_tiles(M, N, K * a.dtype.itemsize, K * b.dtype.itemsize, jnp.dtype(out_dtype).itemsize,
                       _ROW_CANDS, (512, 256, 128))

    def body(a_ref, b_ref, o_ref):
        o_ref[...] = _dot(a_ref[...], b_ref[...], _NN).astype(o_ref.dtype)

    res, xres = hosted_call(
        body, ex, [a, b], name=name, out_shape=[jax.ShapeDtypeStruct((M, N), out_dtype)], grid=(N // tn, M // tm),
        in_specs=[pl.BlockSpec((tm, K), lambda j, i: (i, 0)), pl.BlockSpec((K, tn), lambda j, i: (0, j))],
        out_specs=[pl.BlockSpec((tm, tn), lambda j, i: (i, j))])
    return _one(res, xres, ex)


def matmul_nt(name, g, b, out_dtype=F32, ex=None, offsets=None):
    pieces = list(g) if isinstance(g, (list, tuple)) else [g]
    offsets = list(offsets) if offsets is not None else [0]
    M = pieces[0].shape[0]
    K, N = b.shape
    g_bytes = sum(p.shape[1] * p.dtype.itemsize for p in pieces)
    tm, tk = _mm_tiles(M, K, g_bytes, N * b.dtype.itemsize, jnp.dtype(out_dtype).itemsize, _ROW_CANDS, _COL_CANDS)

    def body(*refs):
        b_ref, o_ref = refs[-2:]
        acc = None
        for g_ref, off in zip(refs[:-2], offsets):
            part = _dot(g_ref[...], b_ref[:, off:off + g_ref.shape[1]], _NT)
            acc = part if acc is None else acc + part
        o_ref[...] = acc.astype(o_ref.dtype)

    res, xres = hosted_call(
        body, ex, pieces + [b], name=name, out_shape=[jax.ShapeDtypeStruct((M, K), out_dtype)], grid=(K // tk, M // tm),
        in_specs=[pl.BlockSpec((tm, p.shape[1]), lambda j, i: (i, 0)) for p in pieces]
        + [pl.BlockSpec((tk, N), lambda j, i: (j, 0))],
        out_specs=[pl.BlockSpec((tm, tk), lambda j, i: (i, j))])
    return _one(res, xres, ex)


def matmul_tn(name, a, g, ex=None):
    M, K = a.shape
    N = g.shape[1]
    tk, tn = _mm_tiles(K, N, M * a.dtype.itemsize, M * g.dtype.itemsize, 4, (512, 256, 128), (512, 256, 128))

    def body(a_ref, g_ref, o_ref):
        o_ref[...] = _dot(a_ref[...], g_ref[...], _TN)

    res, xres = hosted_call(
        body, ex, [a, g], name=name, out_shape=[jax.ShapeDtypeStruct((K, N), F32)], grid=(K // tk, N // tn),
        in_specs=[pl.BlockSpec((M, tk), lambda i, j: (0, i)), pl.BlockSpec((M, tn), lambda i, j: (0, j))],
        out_specs=[pl.BlockSpec((tk, tn), lambda i, j: (i, j))])
    return _one(res, xres, ex)


class Arg:
    def __init__(self, arr, block, imap, kind):
        self.arr, self.block, self.imap, self.kind = arr, block, imap, kind


class Rows:
    def __init__(self, nt, nct, tm, ncol=1):
        self.nt, self.nct, self.tm, self.ncol = nt, nct, tm, ncol

    def seg(self, i):
        return jnp.where(i >= self.nct, 1, 0)

    def spec(self, block, imap):
        return pl.BlockSpec(block, lambda j, i: imap(j, i, self.seg(i)))

    def row(self, arr, width, cb0=0, follow=False, roff=0, stride=1):
        f = stride if follow else 0
        return Arg(arr, (self.tm, width), lambda j, i, s: (i + roff, cb0 + f * j), "row")

    def vec(self, arr, follow=False, kind="acc"):
        w = arr.shape[1] // (self.ncol if follow else 1)
        f = 1 if follow else 0
        return Arg(arr, (1, w), lambda j, i, s: (0, f * j), kind)

    def segvec(self, arr, kind="seg"):
        return Arg(arr, (None, 1, arr.shape[2]), lambda j, i, s: (s, 0, 0), kind)


def _load(ref):
    return ref[...].astype(F32) if ref.dtype != F32 else ref[...]


def stage_fwd(name, f, rows, args, outs):
    n_in = len(args)

    def body(*refs):
        vals = [_load(r) for r in refs[:n_in]]
        res = f(*vals)
        for r, v in zip(refs[n_in:], res):
            r[...] = v.astype(r.dtype)

    T = rows.nt * rows.tm
    out_shape = [jax.ShapeDtypeStruct((T, w * (rows.ncol if fo else 1)), dt) for w, dt, fo in outs]
    out_specs = [pl.BlockSpec((rows.tm, w), (lambda j, i, fo=fo: (i, j if fo else 0))) for w, dt, fo in outs]
    res = _pcall(
        body, name=name, out_shape=out_shape, grid=(rows.ncol, rows.nt),
        in_specs=[rows.spec(a.block, a.imap) for a in args], out_specs=out_specs,
        compiler_params=_params("parallel", "parallel"),
    )(*[a.arr for a in args])
    return res


def stage_bwd(name, f, rows, args, cots, row_dtypes, ex=None):
    n_in, n_ct = len(args), len(cots)
    diff = [k for k, a in enumerate(args) if a.kind != "const"]
    row_dt = {}
    for k in diff:
        if args[k].kind == "row":
            row_dt[k] = row_dtypes[len(row_dt)]

    def body(*refs):
        i = pl.program_id(1)
        vals = [_load(r) for r in refs[:n_in]]
        cts = tuple(_load(r) for r in refs[n_in:n_in + n_ct])
        outs = refs[n_in + n_ct:]

        def g(*dv):
            full = list(vals)
            for k, v in zip(diff, dv):
                full[k] = v
            return tuple(f(*full))

        _, vjp = jax.vjp(g, *[vals[k] for k in diff])
        grads = vjp(cts)
        for k, o, gr in zip(diff, outs, grads):
            kind = args[k].kind
            if kind == "row":
                o[...] = gr.astype(o.dtype)
            else:
                first = (i == 0) | (i == rows.nct) if kind == "seg" else (i == 0)

                @pl.when(first)
                def _():
                    o[...] = gr.astype(o.dtype)

                @pl.when(jnp.logical_not(first))
                def _():
                    o[...] += gr.astype(o.dtype)

    T = rows.nt * rows.tm
    out_shape, out_specs = [], []
    for k in diff:
        a = args[k]
        if a.kind == "row":
            out_shape.append(jax.ShapeDtypeStruct((T, a.block[1] * (rows.ncol if _follows(a) else 1)), row_dt[k]))
            fo = _follows(a)
            out_specs.append(pl.BlockSpec(a.block, (lambda j, i, fo=fo: (i, j if fo else 0))))
        else:
            out_shape.append(jax.ShapeDtypeStruct(a.arr.shape, F32))
            out_specs.append(rows.spec(a.block, a.imap))
    res, xres = hosted_call(
        body, ex, [a.arr for a in list(args) + list(cots)], name=name, out_shape=out_shape, grid=(rows.ncol, rows.nt),
        in_specs=[rows.spec(a.block, a.imap) for a in list(args) + list(cots)], out_specs=out_specs)
    return res if ex is None else (res, xres)


def _follows(a):
    return a.imap(1, 0, 0)[-1] != a.imap(0, 0, 0)[-1]


def _rms(x):
    return x * lax.rsqrt(jnp.mean(x * x, axis=-1, keepdims=True) + EPS)


def f_norm_mod(x, g, sh, sc):
    return ((_rms(x) * g) * (1.0 + sc) + sh,)


def f_resid_norm_mod(x, mo, ga, g, sh, sc):
    x1 = x + ga * mo
    return x1, (_rms(x1) * g) * (1.0 + sc) + sh


def f_resid(x, dn, ga):
    return (x + ga * dn,)


def f_silu(x):
    return (x * jax.nn.sigmoid(x),)


def f_bias(x, b):
    return (x + b,)


def f_ssd_gate(y0, y1, xs, z, dskip, nw):
    y = y0 + y1 + dskip * xs
    return (_rms(y * (z * jax.nn.sigmoid(z))) * nw,)


def f_pool(u, pmat, pmat_t, inv_cnt, pw, scale):
    pm = _lin01(pmat, pmat_t, u) * inv_cnt - u
    return (_mm(pm, pw) * scale,)


def f_merge(o_ssd, o_pool, gl_ssd, gl_pool):
    return (jax.nn.sigmoid(gl_ssd) * o_ssd + jax.nn.sigmoid(gl_pool) * o_pool,)


@jax.custom_vjp
def _halve_cols(x):
    h = x.shape[1] // 2
    return x[:, :h], x[:, h:]


def _halve_cols_fwd(x):
    return _halve_cols(x), None


def _halve_cols_bwd(_, g):
    return (jnp.concatenate(g, axis=1),)


_halve_cols.defvjp(_halve_cols_fwd, _halve_cols_bwd)


def f_swiglu(gu):
    a, b = _halve_cols(gu)
    return ((a * jax.nn.sigmoid(a)) * b,)


def f_loss(x, tgt, g):
    err = _rms(x) * g - tgt
    return (0.5 * jnp.mean(err * err, axis=-1, keepdims=True),)


CONV_TILE = 128


def _shift_rows(v, j, n_ctx):
    if j == 0:
        return v
    T = v.shape[0]
    r = lax.broadcasted_iota(jnp.int32, v.shape, 0)
    lo = jnp.where(r >= n_ctx, n_ctx, 0)
    hi = jnp.where(r >= n_ctx, T, n_ctx)
    ok = (r + j >= lo) & (r + j < hi)
    return jnp.where(ok, pltpu.roll(v, (-j) % T, 0), 0.0)


def conv_fwd(name, proj, conv_w, conv_b, n_ctx, width, ex=None):
    T = proj.shape[0]
    half = SSD_CONV // 2

    def body(u_ref, w_ref, b_ref, o_ref):
        u = u_ref[...]
        pre = jnp.broadcast_to(b_ref[...], u.shape)
        for k in range(SSD_CONV):
            pre = pre + w_ref[k:k + 1, :] * _shift_rows(u, k - half, n_ctx)
        o_ref[...] = pre * jax.nn.sigmoid(pre)

    col = lambda t: (0, t)
    res, xres = hosted_call(
        body, ex, [proj, conv_w, conv_b], name=name, out_shape=[jax.ShapeDtypeStruct((T, width), F32)],
        grid=(width // CONV_TILE,),
        in_specs=[pl.BlockSpec((T, CONV_TILE), col), pl.BlockSpec((SSD_CONV, CONV_TILE), col),
                  pl.BlockSpec((1, CONV_TILE), col)],
        out_specs=[pl.BlockSpec((T, CONV_TILE), col)])
    return res[0], xres


def conv_bwd(name, proj, conv_w, conv_b, d_act2, d_skip, n_ctx, width, ex=None):
    T = proj.shape[0]
    half = SSD_CONV // 2

    def body(u_ref, w_ref, b_ref, c0_ref, c1_ref, cs_ref, du_ref, dw_ref, db_ref):
        t = pl.program_id(0)
        u = u_ref[...]
        pre = jnp.broadcast_to(b_ref[...], u.shape)
        for k in range(SSD_CONV):
            pre = pre + w_ref[k:k + 1, :] * _shift_rows(u, k - half, n_ctx)
        sg = jax.nn.sigmoid(pre)
        ct = c0_ref[...] + c1_ref[...] + jnp.where(t % 4 < 2, cs_ref[...], 0.0)
        dpre = ct * (sg * (1.0 + pre * (1.0 - sg)))
        du = jnp.zeros_like(u)
        for k in range(SSD_CONV):
            du = du + w_ref[k:k + 1, :] * _shift_rows(dpre, half - k, n_ctx)
            dw_ref[k:k + 1, :] = jnp.sum(dpre * _shift_rows(u, k - half, n_ctx), axis=0, keepdims=True)
        du_ref[...] = du.astype(du_ref.dtype)
        db_ref[...] = jnp.sum(dpre, axis=0, keepdims=True)

    col = lambda t: (0, t)
    skip_col = lambda t: (0, (t // 4) * 2 + jnp.minimum(t % 4, 1))
    res, xres = hosted_call(
        body, ex, [proj, conv_w, conv_b, d_act2, d_act2, d_skip], name=name,
        out_shape=[jax.ShapeDtypeStruct((T, width), ACT_DTYPE), jax.ShapeDtypeStruct((SSD_CONV, width), F32),
                   jax.ShapeDtypeStruct((1, width), F32)],
        grid=(width // CONV_TILE,),
        in_specs=[pl.BlockSpec((T, CONV_TILE), col), pl.BlockSpec((SSD_CONV, CONV_TILE), col),
                  pl.BlockSpec((1, CONV_TILE), col), pl.BlockSpec((T, CONV_TILE), col),
                  pl.BlockSpec((T, CONV_TILE), lambda t: (1, t)), pl.BlockSpec((T, CONV_TILE), skip_col)],
        out_specs=[pl.BlockSpec((T, CONV_TILE), col), pl.BlockSpec((SSD_CONV, CONV_TILE), col),
                   pl.BlockSpec((1, CONV_TILE), col)])
    return res[0], res[1], res[2], xres


@jax.custom_vjp
def _cumsum_mat(tri, tri_t, a):
    return jnp.dot(tri, a, precision=lax.Precision.HIGHEST, preferred_element_type=F32)


def _cumsum_fwd(tri, tri_t, a):
    return _cumsum_mat(tri, tri_t, a), (tri, tri_t)


def _cumsum_bwd(res, g):
    tri, tri_t = res
    return (jnp.zeros_like(tri), jnp.zeros_like(tri_t),
            jnp.dot(tri_t, g, precision=lax.Precision.HIGHEST, preferred_element_type=F32))


_cumsum_mat.defvjp(_cumsum_fwd, _cumsum_bwd)


def _ssd_dt(dtraw, dt_bias, a_log, tri, tri_t):
    dt_all = jax.nn.softplus(dtraw + dt_bias)
    a_all = dt_all * (-jnp.exp(a_log))
    return dt_all, a_all, _cumsum_mat(tri, tri_t, a_all)


def _ssd_chunk(xs, bm, cm, dt_all, a_all, s_all, s_in, mask, idx0):
    (xs,), (s_in,) = xs, s_in
    Q = xs.shape[0]
    hpg = xs.shape[1] // SSD_HEADDIM
    lane = lax.broadcasted_iota(jnp.int32, dt_all.shape, 1)
    head = lax.broadcasted_iota(jnp.int32, xs.shape, 1) // SSD_HEADDIM
    head1 = lax.broadcasted_iota(jnp.int32, (1, xs.shape[1]), 1) // SSD_HEADDIM

    def pick(v, r):
        return jnp.sum(jnp.where(lane == idx0 + r, v, 0.0), axis=1, keepdims=True)

    def expand(cols, hd):
        out = cols[hpg - 1]
        for r in range(hpg - 2, -1, -1):
            out = jnp.where(hd == r, cols[r], out)
        return out

    def spread(*cols):
        return expand([jnp.broadcast_to(c, xs.shape) for c in cols], head)

    dt_r = [pick(dt_all, r) for r in range(hpg)]
    s_r = [pick(s_all, r) for r in range(hpg)]
    stot_r = [jnp.sum(jnp.where(lane == idx0 + r, a_all, 0.0), keepdims=True).reshape(1, 1) for r in range(hpg)]

    xd = xs * spread(*dt_r)
    cb = _mm_nt(cm, bm)
    weights, stacked = [], []
    for r in range(hpg):
        sm = jnp.broadcast_to(s_r[r], (Q, Q))
        weights.append(cb * jnp.exp(jnp.where(mask, sm - sm.T, NEG)))
        stacked.append(jnp.where(head == r, xd, 0.0))
    y = spread(*[jnp.exp(c) for c in s_r]) * _mm(cm, s_in)
    y = y + _mm(jnp.concatenate(weights, axis=1), jnp.concatenate(stacked, axis=0))
    to_end = spread(*[jnp.exp(t - c) for t, c in zip(stot_r, s_r)])
    carry = expand([jnp.broadcast_to(jnp.exp(t), (1, xs.shape[1])) for t in stot_r], head1)
    s_out = carry * s_in + _mm_tn(bm, xd * to_end)
    return [y], [s_out]


def _scan_consts():
    q = SSD_CHUNK
    i = np.arange(q)[:, None]
    j = np.arange(q)[None, :]
    fwd = (j <= i).astype(np.float32)
    bwd = (j >= i).astype(np.float32)
    tri = np.stack([fwd, bwd])
    return jnp.asarray(tri), jnp.asarray(np.stack([fwd.T, bwd.T]))


def _chunk_of(d, k, ncc, nc):
    rev = jnp.where(k < ncc, ncc - 1 - k, nc - 1 + ncc - k)
    return jnp.where(d == 0, k, rev)


def ssd_fwd(name, xbc, proj, dt_cb, dt_bias, a_log, n_ctx, ex=None):
    T = xbc.shape[0]
    q, G = SSD_CHUNK, SSD_GROUPS
    nc, ncc = T // q, n_ctx // q
    gw = xbc.shape[1] // G
    xw = gw - 2 * SSD_STATE
    hpg = xw // SSD_HEADDIM
    nh = G * hpg
    tri, tri_t = _scan_consts()

    gs = SSD_GROUPS_PER_STEP

    def body(x_ref, dt_ref, bias_ref, alog_ref, tri_ref, trit_ref, y_ref, sin_ref, state):
        d, gb, k = pl.program_id(0), pl.program_id(1), pl.program_id(2)

        @pl.when(k == 0)
        def _():
            state[...] = jnp.zeros_like(state)

        tri_v = tri_ref[...]
        dt_all, a_all, s_all = _ssd_dt(dt_ref[...], bias_ref[...], alog_ref[...], tri_v, trit_ref[...])
        pairs = [(0, xw)]
        for j in range(gs):
            o = j * gw
            sin_ref[j] = state[j]
            ys, s_outs = _ssd_chunk(
                [x_ref[:, o + lo:o + hi] for lo, hi in pairs], x_ref[:, o + xw:o + xw + SSD_STATE],
                x_ref[:, o + xw + SSD_STATE:o + gw], dt_all, a_all, s_all, [state[j, :, lo:hi] for lo, hi in pairs],
                tri_v > 0.5, d * nh + (gb * gs + j) * hpg)
            for (lo, hi), y, s_out in zip(pairs, ys, s_outs):
                y_ref[:, j * xw + lo:j * xw + hi] = y
                state[j, :, lo:hi] = s_out

    ch = lambda d, g, k: _chunk_of(d, k, ncc, nc)
    res, xres = hosted_call(
        body, ex, [xbc, proj, dt_bias, a_log, tri, tri_t], name=name,
        out_shape=[jax.ShapeDtypeStruct((2 * T, G * xw), F32),
                   jax.ShapeDtypeStruct((2, nc, G, SSD_STATE, xw), F32)],
        grid=(2, G // gs, nc),
        in_specs=[pl.BlockSpec((q, gs * gw), lambda d, g, k: (ch(d, g, k), g)),
                  pl.BlockSpec((q, 128), lambda d, g, k: (ch(d, g, k), dt_cb)),
                  pl.BlockSpec((1, 128), lambda d, g, k: (0, 0)),
                  pl.BlockSpec((1, 128), lambda d, g, k: (0, 0)),
                  pl.BlockSpec((None, q, q), lambda d, g, k: (d, 0, 0)),
                  pl.BlockSpec((None, q, q), lambda d, g, k: (d, 0, 0))],
        out_specs=[pl.BlockSpec((q, gs * xw), lambda d, g, k: (d * nc + ch(d, g, k), g)),
                   pl.BlockSpec((None, None, gs, SSD_STATE, xw), lambda d, g, k: (d, k, g, 0, 0))],
        scratch_shapes=[pltpu.VMEM((gs, SSD_STATE, xw), F32)])
    return res[0], res[1], xres


def ssd_bwd(name, xbc, proj, dt_cb, dt_bias, a_log, states, dy, n_ctx, ex=None):
    T = xbc.shape[0]
    q, G = SSD_CHUNK, SSD_GROUPS
    nc, ncc = T // q, n_ctx // q
    gw = xbc.shape[1] // G
    xw = gw - 2 * SSD_STATE
    hpg = xw // SSD_HEADDIM
    nh = G * hpg
    tri, tri_t = _scan_consts()

    gs = SSD_GROUPS_PER_STEP

    def body(x_ref, dt_ref, bias_ref, alog_ref, tri_ref, trit_ref, sin_ref, dy_ref,
             dx_ref, ddt_ref, dbias_ref, dalog_ref, dstate):
        d, gb, k = pl.program_id(0), pl.program_id(1), pl.program_id(2)
        first = (d == 0) & (gb == 0) & (k == 0)

        @pl.when(first)
        def _():
            ddt_ref[...] = jnp.zeros_like(ddt_ref)
            dbias_ref[...] = jnp.zeros_like(dbias_ref)
            dalog_ref[...] = jnp.zeros_like(dalog_ref)

        @pl.when(k == 0)
        def _():
            dstate[...] = jnp.zeros_like(dstate)

        tri_v, trit_v = tri_ref[...], trit_ref[...]
        mask = tri_v > 0.5

        pairs = [(0, xw)]
        npair = len(pairs)
        per = 2 * npair + 2

        def fn(dtraw, bias, alog, *per_group):
            dt_all, a_all, s_all = _ssd_dt(dtraw, bias, alog, tri_v, trit_v)
            ys, s_outs = [], []
            for j in range(gs):
                grp = per_group[per * j:per * (j + 1)]
                y, s_out = _ssd_chunk(list(grp[:npair]), grp[npair], grp[npair + 1], dt_all, a_all, s_all,
                                      list(grp[npair + 2:]), mask, d * nh + (gb * gs + j) * hpg)
                ys += y
                s_outs += s_out
            return ys, s_outs

        per_group = []
        for j in range(gs):
            o = j * gw
            per_group += [x_ref[:, o + lo:o + hi] for lo, hi in pairs]
            per_group += [x_ref[:, o + xw:o + xw + SSD_STATE], x_ref[:, o + xw + SSD_STATE:o + gw]]
            per_group += [sin_ref[j, :, lo:hi] for lo, hi in pairs]
        _, vjp = jax.vjp(fn, dt_ref[...], bias_ref[...], alog_ref[...], *per_group)
        cts = vjp(([dy_ref[:, j * xw + lo:j * xw + hi] for j in range(gs) for lo, hi in pairs],
                   [dstate[j, :, lo:hi] for j in range(gs) for lo, hi in pairs]))
        ddt, dbias, dalog = cts[:3]
        for j in range(gs):
            o = j * gw
            grp = cts[3 + per * j:3 + per * (j + 1)]
            for (lo, hi), dxs, ds_in in zip(pairs, grp[:npair], grp[npair + 2:]):
                dx_ref[:, o + lo:o + hi] = dxs
                dstate[j, :, lo:hi] = ds_in
            dx_ref[:, o + xw:o + xw + SSD_STATE] = grp[npair]
            dx_ref[:, o + xw + SSD_STATE:o + gw] = grp[npair + 1]
        row0 = pl.multiple_of(_chunk_of(d, nc - 1 - k, ncc, nc) * q, q)
        ddt_ref[pl.ds(row0, q), :] += ddt
        dbias_ref[...] += dbias
        dalog_ref[...] += dalog

    ch = lambda d, g, k: _chunk_of(d, nc - 1 - k, ncc, nc)
    res, xres = hosted_call(
        body, ex, [xbc, proj, dt_bias, a_log, tri, tri_t, states, dy], name=name,
        out_shape=[jax.ShapeDtypeStruct((2 * T, G * gw), F32), jax.ShapeDtypeStruct((T, 128), F32),
                   jax.ShapeDtypeStruct((1, 128), F32), jax.ShapeDtypeStruct((1, 128), F32)],
        grid=(2, G // gs, nc),
        in_specs=[pl.BlockSpec((q, gs * gw), lambda d, g, k: (ch(d, g, k), g)),
                  pl.BlockSpec((q, 128), lambda d, g, k: (ch(d, g, k), dt_cb)),
                  pl.BlockSpec((1, 128), lambda d, g, k: (0, 0)),
                  pl.BlockSpec((1, 128), lambda d, g, k: (0, 0)),
                  pl.BlockSpec((None, q, q), lambda d, g, k: (d, 0, 0)),
                  pl.BlockSpec((None, q, q), lambda d, g, k: (d, 0, 0)),
                  pl.BlockSpec((None, None, gs, SSD_STATE, xw), lambda d, g, k: (d, nc - 1 - k, g, 0, 0)),
                  pl.BlockSpec((q, gs * xw), lambda d, g, k: (ch(d, g, k), g))],
        out_specs=[pl.BlockSpec((q, gs * gw), lambda d, g, k: (d * nc + ch(d, g, k), g)),
                   pl.BlockSpec((T, 128), lambda d, g, k: (0, 0)),
                   pl.BlockSpec((1, 128), lambda d, g, k: (0, 0)),
                   pl.BlockSpec((1, 128), lambda d, g, k: (0, 0))],
        scratch_shapes=[pltpu.VMEM((gs, SSD_STATE, xw), F32)])
    return res[0], res[1], res[2], res[3], xres


def _perm_xbc(a):
    G = SSD_GROUPS
    n = a.shape[-1]
    gn = G * SSD_STATE
    di = n - 2 * gn
    lead = a.shape[:-1]
    xs = a[..., :di].reshape(lead + (G, di // G))
    bm = a[..., di:di + gn].reshape(lead + (G, SSD_STATE))
    cm = a[..., di + gn:].reshape(lead + (G, SSD_STATE))
    return jnp.concatenate([xs, bm, cm], axis=-1).reshape(lead + (n,))


def _unperm_xbc(a):
    G = SSD_GROUPS
    n = a.shape[-1]
    gn = G * SSD_STATE
    di = n - 2 * gn
    lead = a.shape[:-1]
    r = a.reshape(lead + (G, n // G))
    xw = di // G
    return jnp.concatenate([r[..., :xw].reshape(lead + (di,)), r[..., xw:xw + SSD_STATE].reshape(lead + (gn,)),
                            r[..., xw + SSD_STATE:].reshape(lead + (gn,))], axis=-1)


def _pool_consts(tm, n_ctx):
    assert n_ctx == tm and tm % GRID_W == 0
    mats, cnts = [], []
    for seq in (n_ctx, GRID_W):
        t = np.arange(tm)
        tt = t % seq
        base = t - tt
        ms, cs = [], []
        for k in POOL_WINDOWS:
            lo = np.clip(tt - k // 2, 0, seq) + base
            hi = np.clip(tt + k // 2, 0, seq) + base
            m = ((t[None, :] >= lo[:, None]) & (t[None, :] < hi[:, None])).astype(np.float32)
            ms.append(m)
            cs.append((1.0 / (hi - lo).astype(np.float32))[:, None])
        mats.append(np.stack(ms))
        cnts.append(np.stack(cs))
    m = np.stack(mats)
    return jnp.asarray(m), jnp.asarray(np.swapaxes(m, -1, -2)), jnp.asarray(np.stack(cnts).astype(np.float32))


def _prep_layer_weights(w_ada, b_ada, g_mix, w_in, conv_w, conv_b, dt_bias, a_log, d_skip, ssd_norm_w, w_ssd_out,
                        pool_w, pool_scale, w_pool_out, w_out, g_ffn, w_gate_up, w_down):
    D = w_in.shape[0]
    di = ssd_norm_w.shape[0]
    xbc = conv_w.shape[1]
    nh2 = dt_bias.size
    pw = pool_scale.shape[0]
    o = 0
    wz = w_in[:, o:o + di]; o += di
    wx = w_in[:, o:o + xbc]; o += xbc
    wdt = w_in[:, o:o + nh2]; o += nh2
    wp = w_in[:, o:o + pw]; o += pw
    wg = w_in[:, o:]
    w1 = jnp.concatenate([_perm_xbc(wx), wz, wg, wp, wdt, jnp.zeros((D, DT_PAD - nh2), w_in.dtype)], axis=1)
    pad128 = lambda v: jnp.concatenate([v.reshape(1, -1), jnp.zeros((1, 128 - v.size), F32)], axis=1)
    return dict(
        w_ada=w_ada, b_ada=b_ada.reshape(1, -1), g_mix=g_mix.reshape(1, -1), w1=w1,
        conv_w=_perm_xbc(conv_w), conv_b=_perm_xbc(conv_b.reshape(1, -1)),
        dt_bias=pad128(dt_bias), a_log=pad128(a_log),
        dskip=jnp.repeat(d_skip[0] + d_skip[1], SSD_HEADDIM).reshape(1, -1),
        ssd_norm_w=ssd_norm_w.reshape(1, -1), w_ssd_out=w_ssd_out, pool_w=pool_w,
        pool_scale=pool_scale.reshape(1, -1), w_pool_out=w_pool_out, w_out=w_out, g_ffn=g_ffn.reshape(1, -1),
        w_gate_up=w_gate_up, w_down=w_down)


def _unprep_layer_grads(g, dims):
    di, xbc, nh2, pw = dims
    dxbc, dz, dgs, dgp, dp, ddt = g["w1"]
    r = dxbc.reshape(SSD_GROUPS, xbc // SSD_GROUPS, dxbc.shape[1])
    xw = di // SSD_GROUPS
    parts = [r[:, :xw], r[:, xw:xw + SSD_STATE], r[:, xw + SSD_STATE:]]
    w_in_t = jnp.concatenate([dz] + [p.reshape(-1, dxbc.shape[1]) for p in parts] + [ddt[:nh2], dp, dgs, dgp], axis=0)
    nh = nh2 // 2
    dsk = g["dskip"].reshape(nh, SSD_HEADDIM).sum(axis=1)
    return dict(
        w_ada=g["w_ada"], b_ada=g["b_ada"].reshape(-1), g_mix=g["g_mix"].reshape(-1),
        w_in=w_in_t,
        conv_w=_unperm_xbc(g["conv_w"]), conv_b=_unperm_xbc(g["conv_b"]).reshape(-1),
        dt_bias=g["dt_bias"][0, :nh2].reshape(2, nh), a_log=g["a_log"][0, :nh2].reshape(2, nh),
        d_skip=jnp.stack([dsk, dsk]), ssd_norm_w=g["ssd_norm_w"].reshape(-1), w_ssd_out=g["w_ssd_out"],
        pool_w=g["pool_w"], pool_scale=g["pool_scale"].reshape(-1), w_pool_out=g["w_pool_out"], w_out=g["w_out"],
        g_ffn=g["g_ffn"].reshape(-1), w_gate_up=g["w_gate_up"], w_down=g["w_down"])


COND_ROWS = 16


def _split_mods(m):
    d = m.shape[1] // 6
    return [m[:2, k * d:(k + 1) * d].reshape(2, 1, d) for k in range(6)]


TALL_ROW_TILE = 1088


def _tall_rows(T, ncol):
    tm = max(t for t in range(16, min(T, TALL_ROW_TILE) + 1, 16) if T % t == 0)
    return Rows(T // tm, 0, tm, ncol)


def _hosted(hosts, box, key):
    fn = (hosts or {}).get(key)
    return fn(box) if fn else None


def _layer_fwd(l, x, cond_s, w, rows, n_ctx, pc, hosts=None, box=None):
    T, D = x.shape
    nt, nct, tm = rows.nt, rows.nct, rows.tm
    n = lambda s: f"l{l}_{s}"
    crow = Rows(1, 0, COND_ROWS)
    mraw = matmul_nn(n("ada_mm"), cond_s, w["w_ada"])
    (m,) = stage_fwd(n("ada_bias"), f_bias, crow, [crow.row(mraw, mraw.shape[1]), crow.vec(w["b_ada"])],
                     [(mraw.shape[1], F32, False)])
    sh1, sc1, ga1, sh2, sc2, ga2 = _split_mods(m)

    (h1,) = stage_fwd(n("norm1"), f_norm_mod, rows,
                      [rows.row(x, D), rows.vec(w["g_mix"]), rows.segvec(sh1), rows.segvec(sc1)],
                      [(D, ACT_DTYPE, False)])
    ex = _hosted(hosts, box, "in_mm")
    proj = matmul_nn(n("in_mm"), h1, w["w1"], ex=ex)
    if ex is not None:
        proj, box["in_mm"] = proj
    xbc_w = w["conv_w"].shape[1]
    di = w["ssd_norm_w"].shape[1]
    pw = w["pool_scale"].shape[1]
    c_z, c_g, c_p, c_dt = xbc_w, xbc_w + di, xbc_w + di + 2 * pw, xbc_w + di + 3 * pw
    ex = _hosted(hosts, box, "conv")
    xbc, xres = conv_fwd(n("conv"), proj, w["conv_w"], w["conv_b"], n_ctx, xbc_w, ex)
    if ex is not None:
        box["conv"] = xres
    ex = _hosted(hosts, box, "ssd")
    y2, states, xres = ssd_fwd(n("ssd"), xbc, proj, c_dt // 128, w["dt_bias"], w["a_log"], n_ctx, ex)
    if ex is not None:
        box["ssd"] = xres

    G = SSD_GROUPS
    gw = di // G
    r8 = _tall_rows(T, G)
    gate_args = [r8.row(y2, gw, 0, True), r8.row(y2, gw, 0, True, roff=r8.nt), r8.row(xbc, gw, 0, True, stride=2),
                 r8.row(proj, gw, c_z // gw, True), r8.vec(w["dskip"], True), r8.vec(w["ssd_norm_w"], True)]
    (ynw,) = stage_fwd(n("ssd_gate"), f_ssd_gate, r8, gate_args, [(gw, ACT_DTYPE, True)])
    o_ssd = matmul_nn(n("ssd_out_mm"), ynw, w["w_ssd_out"])

    nw = len(POOL_WINDOWS)
    pg = pw // nw
    r4 = Rows(nt, nct, tm, nw)
    pmat, pmat_t, inv_cnt = pc
    cblk = lambda a: Arg(a, (None, None) + a.shape[2:], lambda j, i, s: (s, j, 0, 0), "const")
    pool_args = [r4.row(proj, pg, c_p // pg, True), cblk(pmat), cblk(pmat_t), cblk(inv_cnt),
                 Arg(w["pool_w"], (None, pg, pg), lambda j, i, s: (j, 0, 0), "acc"), r4.vec(w["pool_scale"], True)]
    (ps,) = stage_fwd(n("pool"), f_pool, r4, pool_args, [(pg, ACT_DTYPE, True)])
    o_pool = matmul_nn(n("pool_out_mm"), ps, w["w_pool_out"])

    merge_args = [rows.row(o_ssd, D), rows.row(o_pool, D), rows.row(proj, pw, c_g // pw), rows.row(proj, pw, c_g // pw + 1)]
    (mg,) = stage_fwd(n("merge"), f_merge, rows, merge_args, [(D, ACT_DTYPE, False)])
    mo = matmul_nn(n("out_mm"), mg, w["w_out"])

    rn_args = [rows.row(x, D), rows.row(mo, D), rows.segvec(ga1), rows.vec(w["g_ffn"]), rows.segvec(sh2), rows.segvec(sc2)]
    x1, h2 = stage_fwd(n("norm2"), f_resid_norm_mod, rows, rn_args, [(D, F32, False), (D, ACT_DTYPE, False)])
    ex = _hosted(hosts, box, "gate_up_mm")
    gu = matmul_nn(n("gate_up_mm"), h2, w["w_gate_up"], ex=ex)
    if ex is not None:
        gu, box["gate_up_mm"] = gu
    fh = gu.shape[1] // 2
    (act,) = stage_fwd(n("swiglu"), f_swiglu, rows, [rows.row(gu, 2 * fh)], [(fh, ACT_DTYPE, False)])
    dn = matmul_nn(n("down_mm"), act, w["w_down"])
    res_args = [rows.row(x1, D), rows.row(dn, D), rows.segvec(ga2)]
    (x2,) = stage_fwd(n("resid2"), f_resid, rows, res_args, [(D, F32, False)])
    saved = dict(x=x, mraw=mraw, mods=(sh1, sc1, ga1, sh2, sc2, ga2), h1=h1, proj=proj, xbc=xbc, y2=y2, states=states,
                 ynw=ynw, o_ssd=o_ssd, ps=ps, o_pool=o_pool, mg=mg, mo=mo, x1=x1, h2=h2, gu=gu, act=act, dn=dn,
                 cols=(c_z, c_g, c_p, c_dt))
    return x2, saved


def f_norm_mod_keep(x, g, sh, sc):
    return f_norm_mod(x, g, sh, sc)[0], x


def _layer_bwd(l, dx2, cond_s, w, s, rows, n_ctx, pc, hosts=None, box=None):
    T, D = dx2.shape
    nt, nct, tm = rows.nt, rows.nct, rows.tm
    n = lambda t: f"l{l}_{t}_bwd"
    sh1, sc1, ga1, sh2, sc2, ga2 = s["mods"]
    c_z, c_g, c_p, c_dt = s["cols"]
    x, proj, xbc, y2, gu = s["x"], s["proj"], s["xbc"], s["y2"], s["gu"]
    g = {}
    if box is not None:
        box["g"] = g

    res_args = [rows.row(s["x1"], D), rows.row(s["dn"], D), rows.segvec(ga2)]
    res_args[0].kind = "const"
    dx1 = dx2
    ddn, dga2 = stage_bwd(n("resid2"), f_resid, rows, res_args, [rows.row(dx2, D)], [ACT_DTYPE])
    ex = _hosted(hosts, box, "down_dx")
    dact = matmul_nt(n("down_dx"), ddn, w["w_down"], ex=ex)
    if ex is not None:
        dact, box["down_dx"] = dact
    g["w_down"] = matmul_tn(n("down_dw"), s["act"], ddn)
    fh = gu.shape[1] // 2
    (dgu,) = stage_bwd(n("swiglu"), f_swiglu, rows, [rows.row(gu, 2 * fh)], [rows.row(dact, fh)], [ACT_DTYPE])
    dh2 = matmul_nt(n("gate_up_dx"), dgu, w["w_gate_up"])
    g["w_gate_up"] = matmul_tn(n("gate_up_dw"), s["h2"], dgu)

    rn_args = [rows.row(x, D), rows.row(s["mo"], D), rows.segvec(ga1), rows.vec(w["g_ffn"]), rows.segvec(sh2), rows.segvec(sc2)]
    dxr, dmo, dga1, g["g_ffn"], dsh2, dsc2 = stage_bwd(
        n("norm2"), f_resid_norm_mod, rows, rn_args, [rows.row(dx1, D), rows.row(dh2, D)], [F32, ACT_DTYPE])
    dmg = matmul_nt(n("out_dx"), dmo, w["w_out"])
    g["w_out"] = matmul_tn(n("out_dw"), s["mg"], dmo)

    pw = w["pool_scale"].shape[1]
    merge_args = [rows.row(s["o_ssd"], D), rows.row(s["o_pool"], D), rows.row(proj, pw, c_g // pw), rows.row(proj, pw, c_g // pw + 1)]
    do_ssd, do_pool, dgl_s, dgl_p = stage_bwd(n("merge"), f_merge, rows, merge_args, [rows.row(dmg, D)], [ACT_DTYPE] * 4)
    dps = matmul_nt(n("pool_out_dx"), do_pool, w["w_pool_out"])
    g["w_pool_out"] = matmul_tn(n("pool_out_dw"), s["ps"], do_pool)

    nw = len(POOL_WINDOWS)
    pg = pw // nw
    r4 = Rows(nt, nct, tm, nw)
    pmat, pmat_t, inv_cnt = pc
    cblk = lambda a: Arg(a, (None, None) + a.shape[2:], lambda j, i, s_: (s_, j, 0, 0), "const")
    pool_args = [r4.row(proj, pg, c_p // pg, True), cblk(pmat), cblk(pmat_t), cblk(inv_cnt),
                 Arg(w["pool_w"], (None, pg, pg), lambda j, i, s_: (j, 0, 0), "acc"), r4.vec(w["pool_scale"], True)]
    du_pool, g["pool_w"], g["pool_scale"] = stage_bwd(n("pool"), f_pool, r4, pool_args, [r4.row(dps, pg, 0, True)], [ACT_DTYPE])

    dynw = matmul_nt(n("ssd_out_dx"), do_ssd, w["w_ssd_out"])
    g["w_ssd_out"] = matmul_tn(n("ssd_out_dw"), s["ynw"], do_ssd)
    G = SSD_GROUPS
    di = w["ssd_norm_w"].shape[1]
    gw = di // G
    r8 = _tall_rows(T, G)
    gate_args = [r8.row(y2, gw, 0, True), r8.row(y2, gw, 0, True, roff=r8.nt), r8.row(xbc, gw, 0, True, stride=2),
                 r8.row(proj, gw, c_z // gw, True), r8.vec(w["dskip"], True), r8.vec(w["ssd_norm_w"], True)]
    gate_args[1].kind = "const"
    ex = _hosted(hosts, box, "ssd_gate")
    res = stage_bwd(n("ssd_gate"), f_ssd_gate, r8, gate_args, [r8.row(dynw, gw, 0, True)], [F32, F32, ACT_DTYPE], ex)
    if ex is not None:
        res, box["ssd_gate"] = res
    dy, dxs_skip, dz, g["dskip"], g["ssd_norm_w"] = res

    ex = _hosted(hosts, box, "ssd")
    dxbc2, ddt, g["dt_bias"], g["a_log"], xres = ssd_bwd(n("ssd"), xbc, proj, c_dt // 128, w["dt_bias"], w["a_log"],
                                                         s["states"], dy, n_ctx, ex)
    if ex is not None:
        box["ssd"] = xres
    xbc_w = xbc.shape[1]
    ex = _hosted(hosts, box, "conv")
    dxbc_raw, g["conv_w"], g["conv_b"], xres = conv_bwd(n("conv"), proj, w["conv_w"], w["conv_b"], dxbc2, dxs_skip,
                                                         n_ctx, xbc_w, ex)
    if ex is not None:
        box["conv"] = xres
    pieces = [dxbc_raw, dz, dgl_s, dgl_p, du_pool, ddt]
    offsets = [0, c_z, c_g, c_g + pw, c_p, c_dt]
    ex = _hosted(hosts, box, "in_dx")
    dh1 = matmul_nt(n("in_dx"), pieces, w["w1"], ex=ex, offsets=offsets)
    if ex is not None:
        dh1, box["in_dx"] = dh1
    ex = _hosted(hosts, box, "in_dw")
    first = matmul_tn(n("in_dw0"), pieces[0], s["h1"], ex=ex)
    if ex is not None:
        first, box["in_dw"] = first
    g["w1"] = [first] + [matmul_tn(n(f"in_dw{k}"), p, s["h1"]) for k, p in enumerate(pieces) if k]

    n1_args = [rows.row(x, D), rows.vec(w["g_mix"]), rows.segvec(sh1), rows.segvec(sc1)]
    dx, g["g_mix"], dsh1, dsc1 = stage_bwd(n("norm1"), f_norm_mod_keep, rows, n1_args,
                                           [rows.row(dh1, D), rows.row(dxr, D)], [F32])

    dm = jnp.concatenate([v.reshape(2, D) for v in (dsh1, dsc1, dga1, dsh2, dsc2, dga2)], axis=1)
    dm = jnp.concatenate([dm, jnp.zeros((COND_ROWS - 2, dm.shape[1]), F32)], axis=0)
    crow = Rows(1, 0, COND_ROWS)
    dmraw, g["b_ada"] = stage_bwd(n("ada_bias"), f_bias, crow, [crow.row(s["mraw"], dm.shape[1]), crow.vec(w["b_ada"])],
                                  [crow.row(dm, dm.shape[1])], [ACT_DTYPE])
    dcs = matmul_nt(n("ada_dx"), dmraw, w["w_ada"])
    g["w_ada"] = matmul_tn(n("ada_dw"), cond_s, dmraw)
    return dx, dcs, g


def local_step(x, ctx, c, c_ctx, target, layer_w_fn, n_layers, g_final, fwd_hosts=None, bwd_hosts=None):
    L, D = x.shape
    n_ctx = ctx.shape[0]
    tm = ROW_TILE
    T = L + n_ctx
    rows = Rows(T // tm, n_ctx // tm, tm)
    pc = _pool_consts(tm, n_ctx)
    xa = jnp.concatenate([ctx, x], axis=0)
    cond = jnp.concatenate([c_ctx.reshape(1, D), c.reshape(1, D), jnp.zeros((COND_ROWS - 2, D), F32)], axis=0)
    crow = Rows(1, 0, COND_ROWS)
    (cond_s,) = stage_fwd("cond_silu", f_silu, crow, [crow.row(cond, D)], [(D, ACT_DTYPE, False)])

    saved, layer_w = [], []
    for l in range(n_layers):
        layer_w.append(layer_w_fn(l))
        box = {}
        xa, s = _layer_fwd(l, xa, cond_s, layer_w[l], rows, n_ctx, pc, fwd_hosts(l, box) if fwd_hosts else None, box)
        saved.append(s)

    rl = Rows(L // tm, 0, tm)
    gf = g_final.reshape(1, D)
    tgt = rl.row(target, D)
    tgt.kind = "const"
    loss_args = [rl.row(xa, D, roff=n_ctx // tm), tgt, rl.vec(gf)]
    (loss_rows,) = stage_fwd("loss", f_loss, rl, loss_args, [(1, F32, False)])
    ones = jnp.ones((L, 1), F32)
    dx_lat, dgf = stage_bwd("loss_bwd", f_loss, rl, loss_args, [rl.row(ones, 1)], [F32])
    loss = jnp.sum(loss_rows)
    dx = jnp.concatenate([jnp.zeros((n_ctx, D), F32), dx_lat], axis=0)

    grads = [None] * n_layers
    dcs = jnp.zeros((COND_ROWS, D), F32)
    for l in reversed(range(n_layers)):
        box = {}
        hosts = bwd_hosts(l, grads, box) if bwd_hosts else None
        dx, dcs_l, grads[l] = _layer_bwd(l, dx, cond_s, layer_w[l], saved[l], rows, n_ctx, pc, hosts, box)
        dcs = dcs + dcs_l
    (dcond,) = stage_bwd("cond_silu_bwd", f_silu, crow, [crow.row(cond, D)], [crow.row(dcs, D)], [F32])
    return loss, dx[n_ctx:], grads, dcond[0], dgf


def gather_chips(halves, conv=None):
    n = len(halves)
    ops = list(halves) + ([conv] if conv is not None else [])

    def copies(ins, outs, pos):
        c, me = pos[2], _chip_index(pos)
        pairs = [(s.at[c], o.at[me, c]) for s, o in zip(ins[:n], outs[:n])]
        pairs += [(s, o.at[me]) for s, o in zip(ins[n:], outs[n:])]
        return pairs, [(s, d, _flip(pos, rel)) for rel in PLANE for s, d in pairs]

    shapes = [jax.ShapeDtypeStruct((4,) + s.shape, s.dtype) for s in ops]
    return Exchange(copies, 3 * len(ops), len(ops), ops, shapes)


def gather_pair(gathered):
    n = len(gathered)

    def copies(ins, outs, pos):
        c = pos[2]
        return [], [(s.at[b, c], o.at[b, c], _flip(pos, PAIR[0])) for s, o in zip(ins, outs) for b in range(4)]

    shapes = [jax.ShapeDtypeStruct(g.shape, g.dtype) for g in gathered]
    return Exchange(copies, 4 * n, 0, gathered, shapes, aliases={k: k for k in range(n)})


def swap_halves(grads):
    n = len(grads)

    def copies(ins, outs, pos):
        c = pos[2]
        return [], [(g.at[b, 1 - c], o.at[b], _flip(pos, PAIR[0])) for g, o in zip(ins, outs) for b in range(4)]

    shapes = [jax.ShapeDtypeStruct((g.shape[0],) + g.shape[2:], g.dtype) for g in grads]
    return Exchange(copies, 4 * n, 0, grads, shapes)


def scatter_chips(sums):
    n = len(sums)

    def copies(ins, outs, pos):
        me = _chip_index(pos)
        local = [(p.at[me], o.at[me]) for p, o in zip(ins, outs)]
        remote = []
        for rel in PLANE:
            peer = _flip(pos, rel)
            remote += [(p.at[_chip_index(peer)], o.at[me], peer) for p, o in zip(ins, outs)]
        return local, remote

    shapes = [jax.ShapeDtypeStruct(p.shape, p.dtype) for p in sums]
    return Exchange(copies, 3 * n, n, sums, shapes)


def share_halves(finals):
    n = len(finals)

    def copies(ins, outs, pos):
        c = pos[2]
        return [], [(f.at[c], o.at[c], _flip(pos, PAIR[0])) for f, o in zip(ins, outs)]

    shapes = [jax.ShapeDtypeStruct(f.shape, f.dtype) for f in finals]
    return Exchange(copies, n, 0, finals, shapes, aliases={k: k for k in range(n)})


def gather_everyone(vec):
    def copies(ins, outs, pos):
        me = _device_index(pos)
        (v,), (o,) = ins, outs
        return [(v, o.at[me])], [(v, o.at[me], _flip(pos, rel)) for rel in EVERYONE]

    return Exchange(copies, len(EVERYONE), 1, [vec], [jax.ShapeDtypeStruct((8,) + vec.shape, vec.dtype)])


def _row_tile(rows, cols, n_bufs, mult=8):
    cap = VMEM_LIMIT_BYTES // 4 // (2 * n_bufs * cols * 4)
    for t in range(min(rows, cap) // mult * mult, 0, -mult):
        if rows % t == 0:
            return t
    return rows


WIRE_DTYPE = jnp.bfloat16


def add_own_half(name, grads, recv, c):
    nb, _, R, C = grads.shape
    tr = _row_tile(R, C, 3, mult=16)

    def body(c_ref, g_ref, r_ref, o_ref):
        o_ref[...] = (g_ref[...] + r_ref[...]).astype(o_ref.dtype)

    spec = pl.BlockSpec((None, tr, C), lambda b, i, c_ref: (b, i, 0))
    return _pcall(
        body, name=name, out_shape=jax.ShapeDtypeStruct(recv.shape, WIRE_DTYPE),
        grid_spec=pltpu.PrefetchScalarGridSpec(
            num_scalar_prefetch=1, grid=(nb, R // tr),
            in_specs=[pl.BlockSpec((None, None, tr, C), lambda b, i, c_ref: (b, c_ref[0], i, 0)), spec],
            out_specs=spec),
        compiler_params=_params("parallel", "parallel"),
    )(c, grads, recv)


def sum_slots(name, a, c=None):
    n, R, C = a.shape
    tr = _row_tile(R, C, n + 1, mult=16 if a.dtype.itemsize == 2 else 8)

    def body(*refs):
        a_ref, o_ref = refs[-2:]
        acc = a_ref[0].astype(F32)
        for k in range(1, n):
            acc = acc + a_ref[k].astype(F32)
        o_ref[...] = acc

    if c is None:
        return _pcall(
            body, name=name, out_shape=jax.ShapeDtypeStruct((R, C), F32), grid=(R // tr,),
            in_specs=[pl.BlockSpec((n, tr, C), lambda i: (0, i, 0))], out_specs=pl.BlockSpec((tr, C), lambda i: (i, 0)),
            compiler_params=_params("parallel"),
        )(a)
    return _pcall(
        body, name=name, out_shape=jax.ShapeDtypeStruct((2, R, C), F32),
        grid_spec=pltpu.PrefetchScalarGridSpec(
            num_scalar_prefetch=1, grid=(R // tr,),
            in_specs=[pl.BlockSpec((n, tr, C), lambda i, c_ref: (0, i, 0))],
            out_specs=pl.BlockSpec((None, tr, C), lambda i, c_ref: (c_ref[0], i, 0))),
        compiler_params=_params("parallel"),
    )(c, a)


def adamw(name, w, g_layers, m, v):
    nl, R, C = w.shape
    assert len(g_layers) == nl
    tr = _row_tile(R, C, 8 + nl)
    nr = R // tr

    def body(*refs):
        w_ref, m_ref, v_ref = refs[:3]
        g_refs = refs[3:3 + nl]
        go_ref, d_ref, nm_ref, nv_ref = refs[3 + nl:]
        l = pl.program_id(0)
        gr = g_refs[0][...]
        for k in range(1, nl):
            gr = jnp.where(l == k, g_refs[k][...], gr)
        nm = ADAM_B1 * m_ref[...] + (1.0 - ADAM_B1) * gr
        nv = ADAM_B2 * v_ref[...] + (1.0 - ADAM_B2) * jnp.square(gr)
        m_hat = nm / (1.0 - ADAM_B1 ** ADAM_STEP)
        v_hat = nv / (1.0 - ADAM_B2 ** ADAM_STEP)
        d_ref[...] = -ADAM_LR * (m_hat / (jnp.sqrt(v_hat) + ADAM_EPS) + ADAM_WD * w_ref[...])
        go_ref[...] = gr
        nm_ref[...] = nm
        nv_ref[...] = nv

    spec = pl.BlockSpec((None, tr, C), lambda l, i: (l, i, 0))
    g_specs = [pl.BlockSpec((tr, C), (lambda l, i, k=k: (jnp.where(l == k, i, jnp.where(l < k, 0, nr - 1)), 0)))
               for k in range(nl)]
    return _pcall(
        body, name=name, out_shape=[jax.ShapeDtypeStruct((nl, R, C), F32)] * 4, grid=(nl, nr),
        in_specs=[spec] * 3 + g_specs, out_specs=[spec] * 4, compiler_params=_params("arbitrary", "arbitrary"),
    )(w, m, v, *g_layers)


BIG = ("w_ada", "w_in", "w_ssd_out", "pool_w", "w_pool_out", "w_out", "w_gate_up", "w_down")
COL_SHARDED = ("w_ada", "w_in", "w_gate_up")
GRAD_TRANSPOSED = ("w_in",)
FIRST_USED = ("w_ada", "w_in")
LATER_USED = tuple(k for k in BIG if k not in FIRST_USED)
READY_LAST = FIRST_USED
READY_EARLY = LATER_USED
SMALL = ("c_ctx", "b_ada", "g_mix", "conv_w", "conv_b", "dt_bias", "a_log", "d_skip", "ssd_norm_w", "pool_scale",
         "g_ffn", "g_final")
WEIGHTS = ("c_ctx", "w_ada", "b_ada", "g_mix", "w_in", "conv_w", "conv_b", "dt_bias", "a_log", "d_skip", "ssd_norm_w",
           "w_ssd_out", "pool_w", "pool_scale", "w_pool_out", "w_out", "g_ffn", "w_gate_up", "w_down", "g_final")
LAYER_KEYS = ("w_ada", "b_ada", "g_mix", "w_in", "conv_w", "conv_b", "dt_bias", "a_log", "d_skip", "ssd_norm_w",
              "w_ssd_out", "pool_w", "pool_scale", "w_pool_out", "w_out", "g_ffn", "w_gate_up", "w_down")


def _shard2d(name, a):
    if name == "pool_w":
        return a.reshape(a.shape[0], a.shape[1] * a.shape[2], a.shape[3])
    return a


def _full_from_blocks(name, a):
    nb, R, C = a.shape
    if name in COL_SHARDED:
        return jnp.transpose(a, (1, 0, 2)).reshape(R, nb * C)
    if name == "pool_w":
        nw = len(POOL_WINDOWS)
        return jnp.transpose(a.reshape(nb, nw, R // nw, C), (1, 0, 2, 3)).reshape(nw, nb * R // nw, C)
    return a.reshape(nb * R, C)


def _blocks_from_full(name, g):
    nb = 4
    if name in COL_SHARDED and name not in GRAD_TRANSPOSED:
        K, N = g.shape
        return jnp.transpose(g.reshape(K, nb, N // nb), (1, 0, 2))
    if name == "pool_w":
        nw, r, C = g.shape
        return jnp.transpose(g.reshape(nw, nb, r // nb, C), (1, 0, 2, 3)).reshape(nb, nw * r // nb, C)
    return g.reshape(nb, g.shape[0] // nb, g.shape[1])


def _pack(arrs, rows):
    flat = jnp.concatenate([a.reshape(-1).astype(F32) for a in arrs])
    return jnp.concatenate([flat, jnp.zeros((rows * 128 - flat.size,), F32)]).reshape(rows, 128)


def _unpack(vec, shapes):
    flat = vec.reshape(-1)
    out, o = [], 0
    for s in shapes:
        n = int(np.prod(s))
        out.append(flat[o:o + n].reshape(s))
        o += n
    return out


def _rows_for(shapes):
    n = sum(int(np.prod(s)) for s in shapes)
    return -(-n // (8 * 128)) * 8


def kernel(x, c, ctx, c_ctx, w_ada, b_ada, g_mix, w_in, conv_w, conv_b, dt_bias, a_log, d_skip, ssd_norm_w, w_ssd_out, pool_w, pool_scale, w_pool_out, w_out, g_ffn, w_gate_up, w_down, g_final, loss_target, m_c_ctx, m_w_ada, m_b_ada, m_g_mix, m_w_in, m_conv_w, m_conv_b, m_dt_bias, m_a_log, m_d_skip, m_ssd_norm_w, m_w_ssd_out, m_pool_w, m_pool_scale, m_w_pool_out, m_w_out, m_g_ffn, m_w_gate_up, m_w_down, m_g_final, v_c_ctx, v_w_ada, v_b_ada, v_g_mix, v_w_in, v_conv_w, v_conv_b, v_dt_bias, v_a_log, v_d_skip, v_ssd_norm_w, v_w_ssd_out, v_pool_w, v_pool_scale, v_w_pool_out, v_w_out, v_g_ffn, v_w_gate_up, v_w_down, v_g_final):
    w = dict(c_ctx=c_ctx, w_ada=w_ada, b_ada=b_ada, g_mix=g_mix, w_in=w_in, conv_w=conv_w, conv_b=conv_b, dt_bias=dt_bias,
             a_log=a_log, d_skip=d_skip, ssd_norm_w=ssd_norm_w, w_ssd_out=w_ssd_out, pool_w=pool_w, pool_scale=pool_scale,
             w_pool_out=w_pool_out, w_out=w_out, g_ffn=g_ffn, w_gate_up=w_gate_up, w_down=w_down, g_final=g_final)
    m = dict(c_ctx=m_c_ctx, w_ada=m_w_ada, b_ada=m_b_ada, g_mix=m_g_mix, w_in=m_w_in, conv_w=m_conv_w, conv_b=m_conv_b,
             dt_bias=m_dt_bias, a_log=m_a_log, d_skip=m_d_skip, ssd_norm_w=m_ssd_norm_w, w_ssd_out=m_w_ssd_out,
             pool_w=m_pool_w, pool_scale=m_pool_scale, w_pool_out=m_w_pool_out, w_out=m_w_out, g_ffn=m_g_ffn,
             w_gate_up=m_w_gate_up, w_down=m_w_down, g_final=m_g_final)
    v = dict(c_ctx=v_c_ctx, w_ada=v_w_ada, b_ada=v_b_ada, g_mix=v_g_mix, w_in=v_w_in, conv_w=v_conv_w, conv_b=v_conv_b,
             dt_bias=v_dt_bias, a_log=v_a_log, d_skip=v_d_skip, ssd_norm_w=v_ssd_norm_w, w_ssd_out=v_w_ssd_out,
             pool_w=v_pool_w, pool_scale=v_pool_scale, w_pool_out=v_w_pool_out, w_out=v_w_out, g_ffn=v_g_ffn,
             w_gate_up=v_w_gate_up, w_down=v_w_down, g_final=v_g_final)
    assert x.shape[0] == 1, "one example per device"
    pos = _position()
    core = pos[2].astype(jnp.int32).reshape(1)
    n_layers = w_in.shape[0]
    assert n_layers == 2
    dims = (ssd_norm_w.shape[1], conv_w.shape[2] * 4, dt_bias[0].size, pool_scale.shape[1])
    shard = {k: _shard2d(k, w[k]) for k in BIG}

    def halves(a):
        return a.reshape(a.shape[:-2] + (2, a.shape[-2] // 2, a.shape[-1]))

    def whole(a):
        return a.reshape(a.shape[:-3] + (2 * a.shape[-2], a.shape[-1]))

    def wire_shards(l, names):
        return [halves(shard[k][l].astype(MXU_DTYPE)) for k in names]

    def full_weights(names, gathered):
        return {k: _full_from_blocks(k, whole(a)) for k, a in zip(names, gathered)}

    first = comm_call("gather0_chips", gather_chips(wire_shards(0, FIRST_USED), conv=conv_w))
    got0 = full_weights(FIRST_USED, comm_call("gather0_pair", gather_pair(first[:-1])))
    conv_all = first[-1]
    conv_full = [jnp.transpose(conv_all[:, l], (1, 0, 2)).reshape(conv_all.shape[2], -1) for l in range(n_layers)]

    boxes = {}

    def layer_w_fn(l):
        if l == 0:
            full = dict(got0)
            late = {k: None for k in LATER_USED}
        else:
            full = full_weights(BIG, boxes[("fwd", 0)]["gate_up_mm"])
            late = {}
        full["conv_w"] = conv_full[l]
        lw = LazyDict(_prep_layer_weights(*[full[k] if k in full else (None if k in late else w[k][l]) for k in LAYER_KEYS]))
        for i, k in enumerate(late):
            lw[k] = (lambda i=i, k=k: _full_from_blocks(k, whole(boxes[("fwd", 0)]["conv"][i])))
        return lw

    def fwd_hosts(l, box):
        boxes[("fwd", l)] = box
        if l != 0:
            return None
        return {"in_mm": lambda box: gather_chips(wire_shards(0, LATER_USED)), "conv": lambda box: gather_pair(box["in_mm"]),
                "ssd": lambda box: gather_chips(wire_shards(1, BIG)), "gate_up_mm": lambda box: gather_pair(box["ssd"])}

    def blocks(gl, names):
        return [halves(_blocks_from_full(k, gl[k])) for k in names]

    def pair_sums(tag, names, G, recv):
        return [add_own_half(f"pair_sum{tag}_{k}", g, r, core) for k, g, r in zip(names, G, recv)]

    def chip_sums(tag, names, parts):
        return [sum_slots(f"chip_sum{tag}_{k}", p, core) for k, p in zip(names, parts)]

    def reduce_now(tag, gl, names):
        G = blocks(gl, names)
        pair = pair_sums(tag, names, G, comm_call(f"swap{tag}", swap_halves(G)))
        fin = chip_sums(tag, names, comm_call(f"scatter{tag}", scatter_chips(pair)))
        return [whole(a) for a in comm_call(f"share{tag}", share_halves(fin))]

    small_layers = {}
    n_big = len(BIG)

    def bwd_hosts(l, grads, box):
        boxes[("bwd", l)] = box
        if l != 0:
            return None
        gl1 = _unprep_layer_grads(grads[1], dims)
        small_layers[1] = gl1
        G1 = blocks(gl1, BIG)
        early = {}

        def gate_host(box):
            early["G"] = blocks(box["g"], READY_EARLY)
            return swap_halves(early["G"])

        def scan_host(box):
            return combine(scatter_chips(pair_sums("1", BIG, G1, box["down_dx"])),
                           scatter_chips(pair_sums("0e", READY_EARLY, early["G"], box["ssd_gate"])))

        def conv_host(box):
            return combine(share_halves(chip_sums("1", BIG, box["ssd"][:n_big])),
                           share_halves(chip_sums("0e", READY_EARLY, box["ssd"][n_big:])))

        return {"down_dx": lambda box: swap_halves(G1), "ssd_gate": gate_host, "ssd": scan_host, "conv": conv_host}

    loss, grad_x, grads, d_c_ctx, d_g_final = local_step(
        x[0], ctx[0], c[0], c_ctx, loss_target[0], layer_w_fn, n_layers, g_final, fwd_hosts, bwd_hosts)
    loss = lax.psum(loss, ("x", "y", "c"))
    shared = [whole(a) for a in boxes[("bwd", 0)]["conv"]]
    reduced1 = shared[:n_big]
    gl0 = _unprep_layer_grads(grads[0], dims)
    small_layers[0] = gl0
    red0 = dict(zip(READY_EARLY, shared[n_big:]))
    red0.update(zip(READY_LAST, reduce_now("0", gl0, READY_LAST)))
    reduced0 = [red0[k] for k in BIG]

    small_full = dict(c_ctx=d_c_ctx, g_final=d_g_final.reshape(-1))
    for k in SMALL:
        if k not in small_full:
            small_full[k] = jnp.stack([small_layers[l][k] for l in range(n_layers)])
    shapes = [small_full[k].shape for k in SMALL]
    packed = _pack([small_full[k] for k in SMALL], _rows_for(shapes))
    total = sum_slots("small_sum", comm_call("gather_small", gather_everyone(packed))[0])
    small_g = dict(zip(SMALL, _unpack(total, shapes)))
    cw = conv_w.shape[2]
    small_g["conv_w"] = lax.dynamic_slice_in_dim(small_g["conv_w"], _chip_index(pos) * cw, cw, axis=2)

    grad, delta, new_m, new_v = {}, {}, {}, {}
    for k, g0, g1 in zip(BIG, reduced0, reduced1):
        shp = w[k].shape
        if k in GRAD_TRANSPOSED:
            flat = lambda a: jnp.swapaxes(a, 1, 2)
            back = lambda a: jnp.swapaxes(a, 1, 2)
        else:
            flat = lambda a: _shard2d(k, a)
            back = lambda a: a.reshape(shp)
        outs = adamw(f"adamw_{k}", flat(w[k]), [g0, g1], flat(m[k]), flat(v[k]))
        grad[k], delta[k], new_m[k], new_v[k] = [back(a) for a in outs]
    sshapes = [w[k].shape for k in SMALL]
    srows = _rows_for(sshapes)
    pk = lambda d: _pack([d[k] for k in SMALL], srows)[None]
    _, d_, m_, v_ = adamw("adamw_small", pk(w), [pk(small_g)[0]], pk(m), pk(v))
    for k, dd, mm, vv in zip(SMALL, _unpack(d_, sshapes), _unpack(m_, sshapes), _unpack(v_, sshapes)):
        grad[k], delta[k], new_m[k], new_v[k] = small_g[k], dd, mm, vv

    return (loss, grad_x[None], *[grad[k] for k in WEIGHTS], *[delta[k] for k in WEIGHTS],
            *[new_m[k] for k in WEIGHTS], *[new_v[k] for k in WEIGHTS])
```

```python
import functools

import jax
import jax.numpy as jnp
import numpy as np
from jax import lax
from jax.experimental import pallas as pl
from jax.experimental.pallas import tpu as pltpu

F32 = jnp.float32
MXU_DTYPE = jnp.bfloat16
ACT_DTYPE = jnp.bfloat16
VMEM_LIMIT_BYTES = 48 * 1024 * 1024
EPS = 1e-6
NEG = -1e30

SSD_HEADDIM = 64
SSD_GROUPS = 8
SSD_STATE = 128
SSD_CHUNK = 128
SSD_GROUPS_PER_STEP = 8
SSD_CONV = 5
GRID_W = 64
POOL_WINDOWS = (2, 4, 8, 16)
ROW_TILE = 256
DT_PAD = 512

ADAM_LR = 0.001
ADAM_B1 = 0.9
ADAM_B2 = 0.999
ADAM_EPS = 1e-08
ADAM_WD = 0.01
ADAM_STEP = 10

MESH = pl.DeviceIdType.MESH


def _pcall(body, **kw):
    return pl.pallas_call(body, **kw)


def _params(*sem):
    return pltpu.CompilerParams(dimension_semantics=tuple(sem), vmem_limit_bytes=VMEM_LIMIT_BYTES)


def _pick_tile(n, cands):
    for t in cands:
        if n % t == 0:
            return t
    return n


PLANE = ((1, 0, 0), (0, 1, 0), (1, 1, 0))
PAIR = ((0, 0, 1),)
EVERYONE = tuple((a, b, d) for a in (0, 1) for b in (0, 1) for d in (0, 1) if a + b + d)
HBM = pl.BlockSpec(memory_space=pl.ANY)


def _position():
    return lax.axis_index("x"), lax.axis_index("y"), lax.axis_index("c")


def _flip(pos, rel):
    return tuple(1 - p if r else p for p, r in zip(pos, rel))


def _chip_index(pos):
    return 2 * pos[0] + pos[1]


def _device_index(pos):
    return 4 * pos[0] + 2 * pos[1] + pos[2]


class Exchange:
    def __init__(self, copies, n_remote, n_local, operands, out_shapes, aliases=None):
        self.copies, self.n_remote, self.n_local = copies, n_remote, n_local
        self.operands, self.out_shapes, self.aliases = list(operands), list(out_shapes), dict(aliases or {})

    def scratch(self):
        return [pltpu.SemaphoreType.DMA((max(self.n_remote, 1),)), pltpu.SemaphoreType.DMA((max(self.n_remote, 1),)),
                pltpu.SemaphoreType.DMA((max(self.n_local, 1),))]

    def descriptors(self, ins, outs, sems):
        send_sems, recv_sems, local_sems = sems
        local, remote = self.copies(ins, outs, _position())
        assert len(local) == self.n_local and len(remote) == self.n_remote
        cps = [pltpu.make_async_copy(src, dst, local_sems.at[k]) for k, (src, dst) in enumerate(local)]
        cps += [pltpu.make_async_remote_copy(src_ref=src, dst_ref=dst, send_sem=send_sems.at[k], recv_sem=recv_sems.at[k],
                                             device_id=peer, device_id_type=MESH) for k, (src, dst, peer) in enumerate(remote)]
        return cps


def combine(a, b):
    na, nao = len(a.operands), len(a.out_shapes)

    def copies(ins, outs, pos):
        la, ra = a.copies(ins[:na], outs[:nao], pos)
        lb, rb = b.copies(ins[na:], outs[nao:], pos)
        return la + lb, ra + rb

    aliases = dict(a.aliases)
    aliases.update({na + k: nao + v for k, v in b.aliases.items()})
    return Exchange(copies, a.n_remote + b.n_remote, a.n_local + b.n_local, a.operands + b.operands,
                    a.out_shapes + b.out_shapes, aliases)


class LazyDict(dict):
    def __getitem__(self, key):
        v = dict.__getitem__(self, key)
        if callable(v):
            v = v()
            dict.__setitem__(self, key, v)
        return v


def comm_call(name, ex):
    n_in, n_out = len(ex.operands), len(ex.out_shapes)

    def body(*refs):
        cps = ex.descriptors(refs[:n_in], refs[n_in:n_in + n_out], refs[n_in + n_out:])
        for cp in cps:
            cp.start()
        for cp in cps:
            cp.wait()

    return _pcall(
        body, name=name, out_shape=ex.out_shapes, in_specs=[HBM] * n_in, out_specs=[HBM] * n_out,
        scratch_shapes=ex.scratch(), input_output_aliases=ex.aliases,
        compiler_params=pltpu.CompilerParams(has_side_effects=True),
    )(*ex.operands)


def hosted_call(body, ex, operands, *, name, out_shape, grid, in_specs, out_specs, scratch_shapes=()):
    n_in, n_out, n_scr = len(operands), len(out_shape), len(scratch_shapes)
    sem = ("arbitrary",) * len(grid)
    if ex is None:
        res = _pcall(body, name=name, out_shape=list(out_shape), grid=grid, in_specs=list(in_specs),
                     out_specs=list(out_specs), scratch_shapes=list(scratch_shapes), compiler_params=_params(*sem))(*operands)
        return res, []
    x_in, x_out = len(ex.operands), len(ex.out_shapes)

    def wrapped(*refs):
        o = 0
        ins = refs[o:o + n_in]; o += n_in
        xins = refs[o:o + x_in]; o += x_in
        outs = refs[o:o + n_out]; o += n_out
        xouts = refs[o:o + x_out]; o += x_out
        scr = refs[o:o + n_scr]; o += n_scr
        sems = refs[o:]
        first = last = None
        for a, n in enumerate(grid):
            i = pl.program_id(a)
            first = (i == 0) if first is None else first & (i == 0)
            last = (i == n - 1) if last is None else last & (i == n - 1)

        @pl.when(first)
        def _():
            for cp in ex.descriptors(xins, xouts, sems):
                cp.start()

        body(*ins, *outs, *scr)

        @pl.when(last)
        def _():
            for cp in ex.descriptors(xins, xouts, sems):
                cp.wait()

    aliases = {n_in + k: n_out + v for k, v in ex.aliases.items()}
    res = _pcall(
        wrapped, name=name, out_shape=list(out_shape) + ex.out_shapes, grid=grid,
        in_specs=list(in_specs) + [HBM] * x_in, out_specs=list(out_specs) + [HBM] * x_out,
        scratch_shapes=list(scratch_shapes) + ex.scratch(), input_output_aliases=aliases,
        compiler_params=pltpu.CompilerParams(dimension_semantics=sem, vmem_limit_bytes=VMEM_LIMIT_BYTES,
                                             has_side_effects=True),
    )(*operands, *ex.operands)
    return res[:n_out], res[n_out:]


def _dot(a, b, dims):
    return lax.dot_general(a.astype(MXU_DTYPE), b.astype(MXU_DTYPE), (dims, ((), ())), preferred_element_type=F32)


_NN = ((1,), (0,))
_NT = ((1,), (1,))
_TN = ((0,), (0,))


@jax.custom_vjp
def _mm(a, b):
    return _dot(a, b, _NN)


def _mm_fwd(a, b):
    return _mm(a, b), (a, b)


def _mm_bwd(res, g):
    a, b = res
    return _dot(g, b, _NT).astype(a.dtype), _dot(a, g, _TN).astype(b.dtype)


_mm.defvjp(_mm_fwd, _mm_bwd)


@jax.custom_vjp
def _mm_nt(a, b):
    return _dot(a, b, _NT)


def _mm_nt_fwd(a, b):
    return _mm_nt(a, b), (a, b)


def _mm_nt_bwd(res, g):
    a, b = res
    return _dot(g, b, _NN).astype(a.dtype), _dot(g, a, _TN).astype(b.dtype)


_mm_nt.defvjp(_mm_nt_fwd, _mm_nt_bwd)


@jax.custom_vjp
def _mm_tn(a, b):
    return _dot(a, b, _TN)


def _mm_tn_fwd(a, b):
    return _mm_tn(a, b), (a, b)


def _mm_tn_bwd(res, g):
    a, b = res
    return _dot(b, g, _NT).astype(a.dtype), _dot(a, g, _NN).astype(b.dtype)


_mm_tn.defvjp(_mm_tn_fwd, _mm_tn_bwd)


def _dot_exact(m01, v):
    m = m01.astype(jnp.bfloat16)
    hi = v.astype(jnp.bfloat16)
    r1 = v - hi.astype(F32)
    mid = r1.astype(jnp.bfloat16)
    lo = (r1 - mid.astype(F32)).astype(jnp.bfloat16)
    out = jnp.dot(m, hi, preferred_element_type=F32)
    out = out + jnp.dot(m, mid, preferred_element_type=F32)
    return out + jnp.dot(m, lo, preferred_element_type=F32)


@jax.custom_vjp
def _lin01(m, mt, v):
    return _dot_exact(m, v)


def _lin01_fwd(m, mt, v):
    return _dot_exact(m, v), (m, mt)


def _lin01_bwd(res, g):
    m, mt = res
    return jnp.zeros_like(m), jnp.zeros_like(mt), _dot_exact(mt, g)


_lin01.defvjp(_lin01_fwd, _lin01_bwd)


MATMUL_VMEM_BUDGET = VMEM_LIMIT_BYTES * 3 // 4


def _mm_tiles(m, n, k_bytes_a, k_bytes_b, out_bytes, cands_m, cands_n):
    best = None
    for tm in cands_m:
        if m % tm:
            continue
        for tn in cands_n:
            if n % tn:
                continue
            need = 2 * (tm * k_bytes_a + tn * k_bytes_b + tm * tn * out_bytes)
            if need <= MATMUL_VMEM_BUDGET and (best is None or tm * tn > best[0] * best[1]):
                best = (tm, tn)
    assert best is not None, (m, n)
    return best


_ROW_CANDS = (4352, 2176, 1088, 768, 544, 512, 272, 256, 128, 16)
_COL_CANDS = (2816, 2048, 1408, 1024, 512, 256, 128)


def _one(res, xres, ex):
    return res[0] if ex is None else (res[0], xres)


def matmul_nn(name, a, b, out_dtype=F32, ex=None):
    M, K = a.shape
    N = b.shape[1]
    tm, tn = _mm_tiles(M, N, K * a.dtype.itemsize, K * b.dtype.itemsize, jnp.dtype(out_dtype).itemsize,
                       _ROW_CANDS, (512, 256, 128))

    def body(a_ref, b_ref, o_ref):
        o_ref[...] = _dot(a_ref[...], b_ref[...], _NN).astype(o_ref.dtype)

    res, xres = hosted_call(
        body, ex, [a, b], name=name, out_shape=[jax.ShapeDtypeStruct((M, N), out_dtype)], grid=(N // tn, M // tm),
        in_specs=[pl.BlockSpec((tm, K), lambda j, i: (i, 0)), pl.BlockSpec((K, tn), lambda j, i: (0, j))],
        out_specs=[pl.BlockSpec((tm, tn), lambda j, i: (i, j))])
    return _one(res, xres, ex)


def matmul_nt(name, g, b, out_dtype=F32, ex=None, offsets=None):
    pieces = list(g) if isinstance(g, (list, tuple)) else [g]
    offsets = list(offsets) if offsets is not None else [0]
    M = pieces[0].shape[0]
    K, N = b.shape
    g_bytes = sum(p.shape[1] * p.dtype.itemsize for p in pieces)
    tm, tk = _mm_tiles(M, K, g_bytes, N * b.dtype.itemsize, jnp.dtype(out_dtype).itemsize, _ROW_CANDS, _COL_CANDS)

    def body(*refs):
        b_ref, o_ref = refs[-2:]
        acc = None
        for g_ref, off in zip(refs[:-2], offsets):
            part = _dot(g_ref[...], b_ref[:, off:off + g_ref.shape[1]], _NT)
            acc = part if acc is None else acc + part
        o_ref[...] = acc.astype(o_ref.dtype)

    res, xres = hosted_call(
        body, ex, pieces + [b], name=name, out_shape=[jax.ShapeDtypeStruct((M, K), out_dtype)], grid=(K // tk, M // tm),
        in_specs=[pl.BlockSpec((tm, p.shape[1]), lambda j, i: (i, 0)) for p in pieces]
        + [pl.BlockSpec((tk, N), lambda j, i: (j, 0))],
        out_specs=[pl.BlockSpec((tm, tk), lambda j, i: (i, j))])
    return _one(res, xres, ex)


def matmul_tn(name, a, g, ex=None):
    M, K = a.shape
    N = g.shape[1]
    tk, tn = _mm_tiles(K, N, M * a.dtype.itemsize, M * g.dtype.itemsize, 4, (512, 256, 128), (512, 256, 128))

    def body(a_ref, g_ref, o_ref):
        o_ref[...] = _dot(a_ref[...], g_ref[...], _TN)

    res, xres = hosted_call(
        body, ex, [a, g], name=name, out_shape=[jax.ShapeDtypeStruct((K, N), F32)], grid=(K // tk, N // tn),
        in_specs=[pl.BlockSpec((M, tk), lambda i, j: (0, i)), pl.BlockSpec((M, tn), lambda i, j: (0, j))],
        out_specs=[pl.BlockSpec((tk, tn), lambda i, j: (i, j))])
    return _one(res, xres, ex)


class Arg:
    def __init__(self, arr, block, imap, kind):
        self.arr, self.block, self.imap, self.kind = arr, block, imap, kind


class Rows:
    def __init__(self, nt, nct, tm, ncol=1):
        self.nt, self.nct, self.tm, self.ncol = nt, nct, tm, ncol

    def seg(self, i):
        return jnp.where(i >= self.nct, 1, 0)

    def spec(self, block, imap):
        return pl.BlockSpec(block, lambda j, i: imap(j, i, self.seg(i)))

    def row(self, arr, width, cb0=0, follow=False, roff=0, stride=1):
        f = stride if follow else 0
        return Arg(arr, (self.tm, width), lambda j, i, s: (i + roff, cb0 + f * j), "row")

    def vec(self, arr, follow=False, kind="acc"):
        w = arr.shape[1] // (self.ncol if follow else 1)
        f = 1 if follow else 0
        return Arg(arr, (1, w), lambda j, i, s: (0, f * j), kind)

    def segvec(self, arr, kind="seg"):
        return Arg(arr, (None, 1, arr.shape[2]), lambda j, i, s: (s, 0, 0), kind)


def _load(ref):
    return ref[...].astype(F32) if ref.dtype != F32 else ref[...]


def stage_fwd(name, f, rows, args, outs):
    n_in = len(args)

    def body(*refs):
        vals = [_load(r) for r in refs[:n_in]]
        res = f(*vals)
        for r, v in zip(refs[n_in:], res):
            r[...] = v.astype(r.dtype)

    T = rows.nt * rows.tm
    out_shape = [jax.ShapeDtypeStruct((T, w * (rows.ncol if fo else 1)), dt) for w, dt, fo in outs]
    out_specs = [pl.BlockSpec((rows.tm, w), (lambda j, i, fo=fo: (i, j if fo else 0))) for w, dt, fo in outs]
    res = _pcall(
        body, name=name, out_shape=out_shape, grid=(rows.ncol, rows.nt),
        in_specs=[rows.spec(a.block, a.imap) for a in args], out_specs=out_specs,
        compiler_params=_params("parallel", "parallel"),
    )(*[a.arr for a in args])
    return res


def stage_bwd(name, f, rows, args, cots, row_dtypes, ex=None):
    n_in, n_ct = len(args), len(cots)
    diff = [k for k, a in enumerate(args) if a.kind != "const"]
    row_dt = {}
    for k in diff:
        if args[k].kind == "row":
            row_dt[k] = row_dtypes[len(row_dt)]

    def body(*refs):
        i = pl.program_id(1)
        vals = [_load(r) for r in refs[:n_in]]
        cts = tuple(_load(r) for r in refs[n_in:n_in + n_ct])
        outs = refs[n_in + n_ct:]

        def g(*dv):
            full = list(vals)
            for k, v in zip(diff, dv):
                full[k] = v
            return tuple(f(*full))

        _, vjp = jax.vjp(g, *[vals[k] for k in diff])
        grads = vjp(cts)
        for k, o, gr in zip(diff, outs, grads):
            kind = args[k].kind
            if kind == "row":
                o[...] = gr.astype(o.dtype)
            else:
                first = (i == 0) | (i == rows.nct) if kind == "seg" else (i == 0)

                @pl.when(first)
                def _():
                    o[...] = gr.astype(o.dtype)

                @pl.when(jnp.logical_not(first))
                def _():
                    o[...] += gr.astype(o.dtype)

    T = rows.nt * rows.tm
    out_shape, out_specs = [], []
    for k in diff:
        a = args[k]
        if a.kind == "row":
            out_shape.append(jax.ShapeDtypeStruct((T, a.block[1] * (rows.ncol if _follows(a) else 1)), row_dt[k]))
            fo = _follows(a)
            out_specs.append(pl.BlockSpec(a.block, (lambda j, i, fo=fo: (i, j if fo else 0))))
        else:
            out_shape.append(jax.ShapeDtypeStruct(a.arr.shape, F32))
            out_specs.append(rows.spec(a.block, a.imap))
    res, xres = hosted_call(
        body, ex, [a.arr for a in list(args) + list(cots)], name=name, out_shape=out_shape, grid=(rows.ncol, rows.nt),
        in_specs=[rows.spec(a.block, a.imap) for a in list(args) + list(cots)], out_specs=out_specs)
    return res if ex is None else (res, xres)


def _follows(a):
    return a.imap(1, 0, 0)[-1] != a.imap(0, 0, 0)[-1]


def _rms(x):
    return x * lax.rsqrt(jnp.mean(x * x, axis=-1, keepdims=True) + EPS)


def f_norm_mod(x, g, sh, sc):
    return ((_rms(x) * g) * (1.0 + sc) + sh,)


def f_resid_norm_mod(x, mo, ga, g, sh, sc):
    x1 = x + ga * mo
    return x1, (_rms(x1) * g) * (1.0 + sc) + sh


def f_resid(x, dn, ga):
    return (x + ga * dn,)


def f_silu(x):
    return (x * jax.nn.sigmoid(x),)


def f_bias(x, b):
    return (x + b,)


def f_ssd_gate(y0, y1, xs, z, dskip, nw):
    y = y0 + y1 + dskip * xs
    return (_rms(y * (z * jax.nn.sigmoid(z))) * nw,)


def f_pool(u, pmat, pmat_t, inv_cnt, pw, scale):
    pm = _lin01(pmat, pmat_t, u) * inv_cnt - u
    return (_mm(pm, pw) * scale,)


def f_merge(o_ssd, o_pool, gl_ssd, gl_pool):
    return (jax.nn.sigmoid(gl_ssd) * o_ssd + jax.nn.sigmoid(gl_pool) * o_pool,)


@jax.custom_vjp
def _halve_cols(x):
    h = x.shape[1] // 2
    return x[:, :h], x[:, h:]


def _halve_cols_fwd(x):
    return _halve_cols(x), None


def _halve_cols_bwd(_, g):
    return (jnp.concatenate(g, axis=1),)


_halve_cols.defvjp(_halve_cols_fwd, _halve_cols_bwd)


def f_swiglu(gu):
    a, b = _halve_cols(gu)
    return ((a * jax.nn.sigmoid(a)) * b,)


def f_loss(x, tgt, g):
    err = _rms(x) * g - tgt
    return (0.5 * jnp.mean(err * err, axis=-1, keepdims=True),)


CONV_TILE = 128


def _shift_rows(v, j, n_ctx):
    if j == 0:
        return v
    T = v.shape[0]
    r = lax.broadcasted_iota(jnp.int32, v.shape, 0)
    lo = jnp.where(r >= n_ctx, n_ctx, 0)
    hi = jnp.where(r >= n_ctx, T, n_ctx)
    ok = (r + j >= lo) & (r + j < hi)
    return jnp.where(ok, pltpu.roll(v, (-j) % T, 0), 0.0)


def conv_fwd(name, proj, conv_w, conv_b, n_ctx, width, ex=None):
    T = proj.shape[0]
    half = SSD_CONV // 2

    def body(u_ref, w_ref, b_ref, o_ref):
        u = u_ref[...]
        pre = jnp.broadcast_to(b_ref[...], u.shape)
        for k in range(SSD_CONV):
            pre = pre + w_ref[k:k + 1, :] * _shift_rows(u, k - half, n_ctx)
        o_ref[...] = pre * jax.nn.sigmoid(pre)

    col = lambda t: (0, t)
    res, xres = hosted_call(
        body, ex, [proj, conv_w, conv_b], name=name, out_shape=[jax.ShapeDtypeStruct((T, width), F32)],
        grid=(width // CONV_TILE,),
        in_specs=[pl.BlockSpec((T, CONV_TILE), col), pl.BlockSpec((SSD_CONV, CONV_TILE), col),
                  pl.BlockSpec((1, CONV_TILE), col)],
        out_specs=[pl.BlockSpec((T, CONV_TILE), col)])
    return res[0], xres


def conv_bwd(name, proj, conv_w, conv_b, d_act2, d_skip, n_ctx, width, ex=None):
    T = proj.shape[0]
    half = SSD_CONV // 2

    def body(u_ref, w_ref, b_ref, c0_ref, c1_ref, cs_ref, du_ref, dw_ref, db_ref):
        t = pl.program_id(0)
        u = u_ref[...]
        pre = jnp.broadcast_to(b_ref[...], u.shape)
        for k in range(SSD_CONV):
            pre = pre + w_ref[k:k + 1, :] * _shift_rows(u, k - half, n_ctx)
        sg = jax.nn.sigmoid(pre)
        ct = c0_ref[...] + c1_ref[...] + jnp.where(t % 4 < 2, cs_ref[...], 0.0)
        dpre = ct * (sg * (1.0 + pre * (1.0 - sg)))
        du = jnp.zeros_like(u)
        for k in range(SSD_CONV):
            du = du + w_ref[k:k + 1, :] * _shift_rows(dpre, half - k, n_ctx)
            dw_ref[k:k + 1, :] = jnp.sum(dpre * _shift_rows(u, k - half, n_ctx), axis=0, keepdims=True)
        du_ref[...] = du.astype(du_ref.dtype)
        db_ref[...] = jnp.sum(dpre, axis=0, keepdims=True)

    col = lambda t: (0, t)
    skip_col = lambda t: (0, (t // 4) * 2 + jnp.minimum(t % 4, 1))
    res, xres = hosted_call(
        body, ex, [proj, conv_w, conv_b, d_act2, d_act2, d_skip], name=name,
        out_shape=[jax.ShapeDtypeStruct((T, width), ACT_DTYPE), jax.ShapeDtypeStruct((SSD_CONV, width), F32),
                   jax.ShapeDtypeStruct((1, width), F32)],
        grid=(width // CONV_TILE,),
        in_specs=[pl.BlockSpec((T, CONV_TILE), col), pl.BlockSpec((SSD_CONV, CONV_TILE), col),
                  pl.BlockSpec((1, CONV_TILE), col), pl.BlockSpec((T, CONV_TILE), col),
                  pl.BlockSpec((T, CONV_TILE), lambda t: (1, t)), pl.BlockSpec((T, CONV_TILE), skip_col)],
        out_specs=[pl.BlockSpec((T, CONV_TILE), col), pl.BlockSpec((SSD_CONV, CONV_TILE), col),
                   pl.BlockSpec((1, CONV_TILE), col)])
    return res[0], res[1], res[2], xres


@jax.custom_vjp
def _cumsum_mat(tri, tri_t, a):
    return jnp.dot(tri, a, precision=lax.Precision.HIGHEST, preferred_element_type=F32)


def _cumsum_fwd(tri, tri_t, a):
    return _cumsum_mat(tri, tri_t, a), (tri, tri_t)


def _cumsum_bwd(res, g):
    tri, tri_t = res
    return (jnp.zeros_like(tri), jnp.zeros_like(tri_t),
            jnp.dot(tri_t, g, precision=lax.Precision.HIGHEST, preferred_element_type=F32))


_cumsum_mat.defvjp(_cumsum_fwd, _cumsum_bwd)


def _ssd_dt(dtraw, dt_bias, a_log, tri, tri_t):
    dt_all = jax.nn.softplus(dtraw + dt_bias)
    a_all = dt_all * (-jnp.exp(a_log))
    return dt_all, a_all, _cumsum_mat(tri, tri_t, a_all)


def _ssd_chunk(xs, bm, cm, dt_all, a_all, s_all, s_in, mask, idx0):
    (xs,), (s_in,) = xs, s_in
    Q = xs.shape[0]
    hpg = xs.shape[1] // SSD_HEADDIM
    lane = lax.broadcasted_iota(jnp.int32, dt_all.shape, 1)
    head = lax.broadcasted_iota(jnp.int32, xs.shape, 1) // SSD_HEADDIM
    head1 = lax.broadcasted_iota(jnp.int32, (1, xs.shape[1]), 1) // SSD_HEADDIM

    def pick(v, r):
        return jnp.sum(jnp.where(lane == idx0 + r, v, 0.0), axis=1, keepdims=True)

    def expand(cols, hd):
        out = cols[hpg - 1]
        for r in range(hpg - 2, -1, -1):
            out = jnp.where(hd == r, cols[r], out)
        return out

    def spread(*cols):
        return expand([jnp.broadcast_to(c, xs.shape) for c in cols], head)

    dt_r = [pick(dt_all, r) for r in range(hpg)]
    s_r = [pick(s_all, r) for r in range(hpg)]
    stot_r = [jnp.sum(jnp.where(lane == idx0 + r, a_all, 0.0), keepdims=True).reshape(1, 1) for r in range(hpg)]

    xd = xs * spread(*dt_r)
    cb = _mm_nt(cm, bm)
    weights, stacked = [], []
    for r in range(hpg):
        sm = jnp.broadcast_to(s_r[r], (Q, Q))
        weights.append(cb * jnp.exp(jnp.where(mask, sm - sm.T, NEG)))
        stacked.append(jnp.where(head == r, xd, 0.0))
    y = spread(*[jnp.exp(c) for c in s_r]) * _mm(cm, s_in)
    y = y + _mm(jnp.concatenate(weights, axis=1), jnp.concatenate(stacked, axis=0))
    to_end = spread(*[jnp.exp(t - c) for t, c in zip(stot_r, s_r)])
    carry = expand([jnp.broadcast_to(jnp.exp(t), (1, xs.shape[1])) for t in stot_r], head1)
    s_out = carry * s_in + _mm_tn(bm, xd * to_end)
    return [y], [s_out]


def _scan_consts():
    q = SSD_CHUNK
    i = np.arange(q)[:, None]
    j = np.arange(q)[None, :]
    fwd = (j <= i).astype(np.float32)
    bwd = (j >= i).astype(np.float32)
    tri = np.stack([fwd, bwd])
    return jnp.asarray(tri), jnp.asarray(np.stack([fwd.T, bwd.T]))


def _chunk_of(d, k, ncc, nc):
    rev = jnp.where(k < ncc, ncc - 1 - k, nc - 1 + ncc - k)
    return jnp.where(d == 0, k, rev)


def ssd_fwd(name, xbc, proj, dt_cb, dt_bias, a_log, n_ctx, ex=None):
    T = xbc.shape[0]
    q, G = SSD_CHUNK, SSD_GROUPS
    nc, ncc = T // q, n_ctx // q
    gw = xbc.shape[1] // G
    xw = gw - 2 * SSD_STATE
    hpg = xw // SSD_HEADDIM
    nh = G * hpg
    tri, tri_t = _scan_consts()

    gs = SSD_GROUPS_PER_STEP

    def body(x_ref, dt_ref, bias_ref, alog_ref, tri_ref, trit_ref, y_ref, sin_ref, state):
        d, gb, k = pl.program_id(0), pl.program_id(1), pl.program_id(2)

        @pl.when(k == 0)
        def _():
            state[...] = jnp.zeros_like(state)

        tri_v = tri_ref[...]
        dt_all, a_all, s_all = _ssd_dt(dt_ref[...], bias_ref[...], alog_ref[...], tri_v, trit_ref[...])
        pairs = [(0, xw)]
        for j in range(gs):
            o = j * gw
            sin_ref[j] = state[j]
            ys, s_outs = _ssd_chunk(
                [x_ref[:, o + lo:o + hi] for lo, hi in pairs], x_ref[:, o + xw:o + xw + SSD_STATE],
                x_ref[:, o + xw + SSD_STATE:o + gw], dt_all, a_all, s_all, [state[j, :, lo:hi] for lo, hi in pairs],
                tri_v > 0.5, d * nh + (gb * gs + j) * hpg)
            for (lo, hi), y, s_out in zip(pairs, ys, s_outs):
                y_ref[:, j * xw + lo:j * xw + hi] = y
                state[j, :, lo:hi] = s_out

    ch = lambda d, g, k: _chunk_of(d, k, ncc, nc)
    res, xres = hosted_call(
        body, ex, [xbc, proj, dt_bias, a_log, tri, tri_t], name=name,
        out_shape=[jax.ShapeDtypeStruct((2 * T, G * xw), F32),
                   jax.ShapeDtypeStruct((2, nc, G, SSD_STATE, xw), F32)],
        grid=(2, G // gs, nc),
        in_specs=[pl.BlockSpec((q, gs * gw), lambda d, g, k: (ch(d, g, k), g)),
                  pl.BlockSpec((q, 128), lambda d, g, k: (ch(d, g, k), dt_cb)),
                  pl.BlockSpec((1, 128), lambda d, g, k: (0, 0)),
                  pl.BlockSpec((1, 128), lambda d, g, k: (0, 0)),
                  pl.BlockSpec((None, q, q), lambda d, g, k: (d, 0, 0)),
                  pl.BlockSpec((None, q, q), lambda d, g, k: (d, 0, 0))],
        out_specs=[pl.BlockSpec((q, gs * xw), lambda d, g, k: (d * nc + ch(d, g, k), g)),
                   pl.BlockSpec((None, None, gs, SSD_STATE, xw), lambda d, g, k: (d, k, g, 0, 0))],
        scratch_shapes=[pltpu.VMEM((gs, SSD_STATE, xw), F32)])
    return res[0], res[1], xres


def ssd_bwd(name, xbc, proj, dt_cb, dt_bias, a_log, states, dy, n_ctx, ex=None):
    T = xbc.shape[0]
    q, G = SSD_CHUNK, SSD_GROUPS
    nc, ncc = T // q, n_ctx // q
    gw = xbc.shape[1] // G
    xw = gw - 2 * SSD_STATE
    hpg = xw // SSD_HEADDIM
    nh = G * hpg
    tri, tri_t = _scan_consts()

    gs = SSD_GROUPS_PER_STEP

    def body(x_ref, dt_ref, bias_ref, alog_ref, tri_ref, trit_ref, sin_ref, dy_ref,
             dx_ref, ddt_ref, dbias_ref, dalog_ref, dstate):
        d, gb, k = pl.program_id(0), pl.program_id(1), pl.program_id(2)
        first = (d == 0) & (gb == 0) & (k == 0)

        @pl.when(first)
        def _():
            ddt_ref[...] = jnp.zeros_like(ddt_ref)
            dbias_ref[...] = jnp.zeros_like(dbias_ref)
            dalog_ref[...] = jnp.zeros_like(dalog_ref)

        @pl.when(k == 0)
        def _():
            dstate[...] = jnp.zeros_like(dstate)

        tri_v, trit_v = tri_ref[...], trit_ref[...]
        mask = tri_v > 0.5

        pairs = [(0, xw)]
        npair = len(pairs)
        per = 2 * npair + 2

        def fn(dtraw, bias, alog, *per_group):
            dt_all, a_all, s_all = _ssd_dt(dtraw, bias, alog, tri_v, trit_v)
            ys, s_outs = [], []
            for j in range(gs):
                grp = per_group[per * j:per * (j + 1)]
                y, s_out = _ssd_chunk(list(grp[:npair]), grp[npair], grp[npair + 1], dt_all, a_all, s_all,
                                      list(grp[npair + 2:]), mask, d * nh + (gb * gs + j) * hpg)
                ys += y
                s_outs += s_out
            return ys, s_outs

        per_group = []
        for j in range(gs):
            o = j * gw
            per_group += [x_ref[:, o + lo:o + hi] for lo, hi in pairs]
            per_group += [x_ref[:, o + xw:o + xw + SSD_STATE], x_ref[:, o + xw + SSD_STATE:o + gw]]
            per_group += [sin_ref[j, :, lo:hi] for lo, hi in pairs]
        _, vjp = jax.vjp(fn, dt_ref[...], bias_ref[...], alog_ref[...], *per_group)
        cts = vjp(([dy_ref[:, j * xw + lo:j * xw + hi] for j in range(gs) for lo, hi in pairs],
                   [dstate[j, :, lo:hi] for j in range(gs) for lo, hi in pairs]))
        ddt, dbias, dalog = cts[:3]
        for j in range(gs):
            o = j * gw
            grp = cts[3 + per * j:3 + per * (j + 1)]
            for (lo, hi), dxs, ds_in in zip(pairs, grp[:npair], grp[npair + 2:]):
                dx_ref[:, o + lo:o + hi] = dxs
                dstate[j, :, lo:hi] = ds_in
            dx_ref[:, o + xw:o + xw + SSD_STATE] = grp[npair]
            dx_ref[:, o + xw + SSD_STATE:o + gw] = grp[npair + 1]
        row0 = pl.multiple_of(_chunk_of(d, nc - 1 - k, ncc, nc) * q, q)
        ddt_ref[pl.ds(row0, q), :] += ddt
        dbias_ref[...] += dbias
        dalog_ref[...] += dalog

    ch = lambda d, g, k: _chunk_of(d, nc - 1 - k, ncc, nc)
    res, xres = hosted_call(
        body, ex, [xbc, proj, dt_bias, a_log, tri, tri_t, states, dy], name=name,
        out_shape=[jax.ShapeDtypeStruct((2 * T, G * gw), F32), jax.ShapeDtypeStruct((T, 128), F32),
                   jax.ShapeDtypeStruct((1, 128), F32), jax.ShapeDtypeStruct((1, 128), F32)],
        grid=(2, G // gs, nc),
        in_specs=[pl.BlockSpec((q, gs * gw), lambda d, g, k: (ch(d, g, k), g)),
                  pl.BlockSpec((q, 128), lambda d, g, k: (ch(d, g, k), dt_cb)),
                  pl.BlockSpec((1, 128), lambda d, g, k: (0, 0)),
                  pl.BlockSpec((1, 128), lambda d, g, k: (0, 0)),
                  pl.BlockSpec((None, q, q), lambda d, g, k: (d, 0, 0)),
                  pl.BlockSpec((None, q, q), lambda d, g, k: (d, 0, 0)),
                  pl.BlockSpec((None, None, gs, SSD_STATE, xw), lambda d, g, k: (d, nc - 1 - k, g, 0, 0)),
                  pl.BlockSpec((q, gs * xw), lambda d, g, k: (ch(d, g, k), g))],
        out_specs=[pl.BlockSpec((q, gs * gw), lambda d, g, k: (d * nc + ch(d, g, k), g)),
                   pl.BlockSpec((T, 128), lambda d, g, k: (0, 0)),
                   pl.BlockSpec((1, 128), lambda d, g, k: (0, 0)),
                   pl.BlockSpec((1, 128), lambda d, g, k: (0, 0))],
        scratch_shapes=[pltpu.VMEM((gs, SSD_STATE, xw), F32)])
    return res[0], res[1], res[2], res[3], xres


def _perm_xbc(a):
    G = SSD_GROUPS
    n = a.shape[-1]
    gn = G * SSD_STATE
    di = n - 2 * gn
    lead = a.shape[:-1]
    xs = a[..., :di].reshape(lead + (G, di // G))
    bm = a[..., di:di + gn].reshape(lead + (G, SSD_STATE))
    cm = a[..., di + gn:].reshape(lead + (G, SSD_STATE))
    return jnp.concatenate([xs, bm, cm], axis=-1).reshape(lead + (n,))


def _unperm_xbc(a):
    G = SSD_GROUPS
    n = a.shape[-1]
    gn = G * SSD_STATE
    di = n - 2 * gn
    lead = a.shape[:-1]
    r = a.reshape(lead + (G, n // G))
    xw = di // G
    return jnp.concatenate([r[..., :xw].reshape(lead + (di,)), r[..., xw:xw + SSD_STATE].reshape(lead + (gn,)),
                            r[..., xw + SSD_STATE:].reshape(lead + (gn,))], axis=-1)


def _pool_consts(tm, n_ctx):
    assert n_ctx == tm and tm % GRID_W == 0
    mats, cnts = [], []
    for seq in (n_ctx, GRID_W):
        t = np.arange(tm)
        tt = t % seq
        base = t - tt
        ms, cs = [], []
        for k in POOL_WINDOWS:
            lo = np.clip(tt - k // 2, 0, seq) + base
            hi = np.clip(tt + k // 2, 0, seq) + base
            m = ((t[None, :] >= lo[:, None]) & (t[None, :] < hi[:, None])).astype(np.float32)
            ms.append(m)
            cs.append((1.0 / (hi - lo).astype(np.float32))[:, None])
        mats.append(np.stack(ms))
        cnts.append(np.stack(cs))
    m = np.stack(mats)
    return jnp.asarray(m), jnp.asarray(np.swapaxes(m, -1, -2)), jnp.asarray(np.stack(cnts).astype(np.float32))


def _prep_layer_weights(w_ada, b_ada, g_mix, w_in, conv_w, conv_b, dt_bias, a_log, d_skip, ssd_norm_w, w_ssd_out,
                        pool_w, pool_scale, w_pool_out, w_out, g_ffn, w_gate_up, w_down):
    D = w_in.shape[0]
    di = ssd_norm_w.shape[0]
    xbc = conv_w.shape[1]
    nh2 = dt_bias.size
    pw = pool_scale.shape[0]
    o = 0
    wz = w_in[:, o:o + di]; o += di
    wx = w_in[:, o:o + xbc]; o += xbc
    wdt = w_in[:, o:o + nh2]; o += nh2
    wp = w_in[:, o:o + pw]; o += pw
    wg = w_in[:, o:]
    w1 = jnp.concatenate([_perm_xbc(wx), wz, wg, wp, wdt, jnp.zeros((D, DT_PAD - nh2), w_in.dtype)], axis=1)
    pad128 = lambda v: jnp.concatenate([v.reshape(1, -1), jnp.zeros((1, 128 - v.size), F32)], axis=1)
    return dict(
        w_ada=w_ada, b_ada=b_ada.reshape(1, -1), g_mix=g_mix.reshape(1, -1), w1=w1,
        conv_w=_perm_xbc(conv_w), conv_b=_perm_xbc(conv_b.reshape(1, -1)),
        dt_bias=pad128(dt_bias), a_log=pad128(a_log),
        dskip=jnp.repeat(d_skip[0] + d_skip[1], SSD_HEADDIM).reshape(1, -1),
        ssd_norm_w=ssd_norm_w.reshape(1, -1), w_ssd_out=w_ssd_out, pool_w=pool_w,
        pool_scale=pool_scale.reshape(1, -1), w_pool_out=w_pool_out, w_out=w_out, g_ffn=g_ffn.reshape(1, -1),
        w_gate_up=w_gate_up, w_down=w_down)


def _unprep_layer_grads(g, dims):
    di, xbc, nh2, pw = dims
    dxbc, dz, dgs, dgp, dp, ddt = g["w1"]
    r = dxbc.reshape(SSD_GROUPS, xbc // SSD_GROUPS, dxbc.shape[1])
    xw = di // SSD_GROUPS
    parts = [r[:, :xw], r[:, xw:xw + SSD_STATE], r[:, xw + SSD_STATE:]]
    w_in_t = jnp.concatenate([dz] + [p.reshape(-1, dxbc.shape[1]) for p in parts] + [ddt[:nh2], dp, dgs, dgp], axis=0)
    nh = nh2 // 2
    dsk = g["dskip"].reshape(nh, SSD_HEADDIM).sum(axis=1)
    return dict(
        w_ada=g["w_ada"], b_ada=g["b_ada"].reshape(-1), g_mix=g["g_mix"].reshape(-1),
        w_in=w_in_t,
        conv_w=_unperm_xbc(g["conv_w"]), conv_b=_unperm_xbc(g["conv_b"]).reshape(-1),
        dt_bias=g["dt_bias"][0, :nh2].reshape(2, nh), a_log=g["a_log"][0, :nh2].reshape(2, nh),
        d_skip=jnp.stack([dsk, dsk]), ssd_norm_w=g["ssd_norm_w"].reshape(-1), w_ssd_out=g["w_ssd_out"],
        pool_w=g["pool_w"], pool_scale=g["pool_scale"].reshape(-1), w_pool_out=g["w_pool_out"], w_out=g["w_out"],
        g_ffn=g["g_ffn"].reshape(-1), w_gate_up=g["w_gate_up"], w_down=g["w_down"])


COND_ROWS = 16


def _split_mods(m):
    d = m.shape[1] // 6
    return [m[:2, k * d:(k + 1) * d].reshape(2, 1, d) for k in range(6)]


TALL_ROW_TILE = 1088


def _tall_rows(T, ncol):
    tm = max(t for t in range(16, min(T, TALL_ROW_TILE) + 1, 16) if T % t == 0)
    return Rows(T // tm, 0, tm, ncol)


def _hosted(hosts, box, key):
    fn = (hosts or {}).get(key)
    return fn(box) if fn else None


def _layer_fwd(l, x, cond_s, w, rows, n_ctx, pc, hosts=None, box=None):
    T, D = x.shape
    nt, nct, tm = rows.nt, rows.nct, rows.tm
    n = lambda s: f"l{l}_{s}"
    crow = Rows(1, 0, COND_ROWS)
    mraw = matmul_nn(n("ada_mm"), cond_s, w["w_ada"])
    (m,) = stage_fwd(n("ada_bias"), f_bias, crow, [crow.row(mraw, mraw.shape[1]), crow.vec(w["b_ada"])],
                     [(mraw.shape[1], F32, False)])
    sh1, sc1, ga1, sh2, sc2, ga2 = _split_mods(m)

    (h1,) = stage_fwd(n("norm1"), f_norm_mod, rows,
                      [rows.row(x, D), rows.vec(w["g_mix"]), rows.segvec(sh1), rows.segvec(sc1)],
                      [(D, ACT_DTYPE, False)])
    ex = _hosted(hosts, box, "in_mm")
    proj = matmul_nn(n("in_mm"), h1, w["w1"], ex=ex)
    if ex is not None:
        proj, box["in_mm"] = proj
    xbc_w = w["conv_w"].shape[1]
    di = w["ssd_norm_w"].shape[1]
    pw = w["pool_scale"].shape[1]
    c_z, c_g, c_p, c_dt = xbc_w, xbc_w + di, xbc_w + di + 2 * pw, xbc_w + di + 3 * pw
    ex = _hosted(hosts, box, "conv")
    xbc, xres = conv_fwd(n("conv"), proj, w["conv_w"], w["conv_b"], n_ctx, xbc_w, ex)
    if ex is not None:
        box["conv"] = xres
    ex = _hosted(hosts, box, "ssd")
    y2, states, xres = ssd_fwd(n("ssd"), xbc, proj, c_dt // 128, w["dt_bias"], w["a_log"], n_ctx, ex)
    if ex is not None:
        box["ssd"] = xres

    G = SSD_GROUPS
    gw = di // G
    r8 = _tall_rows(T, G)
    gate_args = [r8.row(y2, gw, 0, True), r8.row(y2, gw, 0, True, roff=r8.nt), r8.row(xbc, gw, 0, True, stride=2),
                 r8.row(proj, gw, c_z // gw, True), r8.vec(w["dskip"], True), r8.vec(w["ssd_norm_w"], True)]
    (ynw,) = stage_fwd(n("ssd_gate"), f_ssd_gate, r8, gate_args, [(gw, ACT_DTYPE, True)])
    o_ssd = matmul_nn(n("ssd_out_mm"), ynw, w["w_ssd_out"])

    nw = len(POOL_WINDOWS)
    pg = pw // nw
    r4 = Rows(nt, nct, tm, nw)
    pmat, pmat_t, inv_cnt = pc
    cblk = lambda a: Arg(a, (None, None) + a.shape[2:], lambda j, i, s: (s, j, 0, 0), "const")
    pool_args = [r4.row(proj, pg, c_p // pg, True), cblk(pmat), cblk(pmat_t), cblk(inv_cnt),
                 Arg(w["pool_w"], (None, pg, pg), lambda j, i, s: (j, 0, 0), "acc"), r4.vec(w["pool_scale"], True)]
    (ps,) = stage_fwd(n("pool"), f_pool, r4, pool_args, [(pg, ACT_DTYPE, True)])
    o_pool = matmul_nn(n("pool_out_mm"), ps, w["w_pool_out"])

    merge_args = [rows.row(o_ssd, D), rows.row(o_pool, D), rows.row(proj, pw, c_g // pw), rows.row(proj, pw, c_g // pw + 1)]
    (mg,) = stage_fwd(n("merge"), f_merge, rows, merge_args, [(D, ACT_DTYPE, False)])
    mo = matmul_nn(n("out_mm"), mg, w["w_out"])

    rn_args = [rows.row(x, D), rows.row(mo, D), rows.segvec(ga1), rows.vec(w["g_ffn"]), rows.segvec(sh2), rows.segvec(sc2)]
    x1, h2 = stage_fwd(n("norm2"), f_resid_norm_mod, rows, rn_args, [(D, F32, False), (D, ACT_DTYPE, False)])
    ex = _hosted(hosts, box, "gate_up_mm")
    gu = matmul_nn(n("gate_up_mm"), h2, w["w_gate_up"], ex=ex)
    if ex is not None:
        gu, box["gate_up_mm"] = gu
    fh = gu.shape[1] // 2
    (act,) = stage_fwd(n("swiglu"), f_swiglu, rows, [rows.row(gu, 2 * fh)], [(fh, ACT_DTYPE, False)])
    dn = matmul_nn(n("down_mm"), act, w["w_down"])
    res_args = [rows.row(x1, D), rows.row(dn, D), rows.segvec(ga2)]
    (x2,) = stage_fwd(n("resid2"), f_resid, rows, res_args, [(D, F32, False)])
    saved = dict(x=x, mraw=mraw, mods=(sh1, sc1, ga1, sh2, sc2, ga2), h1=h1, proj=proj, xbc=xbc, y2=y2, states=states,
                 ynw=ynw, o_ssd=o_ssd, ps=ps, o_pool=o_pool, mg=mg, mo=mo, x1=x1, h2=h2, gu=gu, act=act, dn=dn,
                 cols=(c_z, c_g, c_p, c_dt))
    return x2, saved


def f_norm_mod_keep(x, g, sh, sc):
    return f_norm_mod(x, g, sh, sc)[0], x


def _layer_bwd(l, dx2, cond_s, w, s, rows, n_ctx, pc, hosts=None, box=None):
    T, D = dx2.shape
    nt, nct, tm = rows.nt, rows.nct, rows.tm
    n = lambda t: f"l{l}_{t}_bwd"
    sh1, sc1, ga1, sh2, sc2, ga2 = s["mods"]
    c_z, c_g, c_p, c_dt = s["cols"]
    x, proj, xbc, y2, gu = s["x"], s["proj"], s["xbc"], s["y2"], s["gu"]
    g = {}
    if box is not None:
        box["g"] = g

    res_args = [rows.row(s["x1"], D), rows.row(s["dn"], D), rows.segvec(ga2)]
    res_args[0].kind = "const"
    dx1 = dx2
    ddn, dga2 = stage_bwd(n("resid2"), f_resid, rows, res_args, [rows.row(dx2, D)], [ACT_DTYPE])
    ex = _hosted(hosts, box, "down_dx")
    dact = matmul_nt(n("down_dx"), ddn, w["w_down"], ex=ex)
    if ex is not None:
        dact, box["down_dx"] = dact
    g["w_down"] = matmul_tn(n("down_dw"), s["act"], ddn)
    fh = gu.shape[1] // 2
    (dgu,) = stage_bwd(n("swiglu"), f_swiglu, rows, [rows.row(gu, 2 * fh)], [rows.row(dact, fh)], [ACT_DTYPE])
    dh2 = matmul_nt(n("gate_up_dx"), dgu, w["w_gate_up"])
    g["w_gate_up"] = matmul_tn(n("gate_up_dw"), s["h2"], dgu)

    rn_args = [rows.row(x, D), rows.row(s["mo"], D), rows.segvec(ga1), rows.vec(w["g_ffn"]), rows.segvec(sh2), rows.segvec(sc2)]
    dxr, dmo, dga1, g["g_ffn"], dsh2, dsc2 = stage_bwd(
        n("norm2"), f_resid_norm_mod, rows, rn_args, [rows.row(dx1, D), rows.row(dh2, D)], [F32, ACT_DTYPE])
    dmg = matmul_nt(n("out_dx"), dmo, w["w_out"])
    g["w_out"] = matmul_tn(n("out_dw"), s["mg"], dmo)

    pw = w["pool_scale"].shape[1]
    merge_args = [rows.row(s["o_ssd"], D), rows.row(s["o_pool"], D), rows.row(proj, pw, c_g // pw), rows.row(proj, pw, c_g // pw + 1)]
    do_ssd, do_pool, dgl_s, dgl_p = stage_bwd(n("merge"), f_merge, rows, merge_args, [rows.row(dmg, D)], [ACT_DTYPE] * 4)
    dps = matmul_nt(n("pool_out_dx"), do_pool, w["w_pool_out"])
    g["w_pool_out"] = matmul_tn(n("pool_out_dw"), s["ps"], do_pool)

    nw = len(POOL_WINDOWS)
    pg = pw // nw
    r4 = Rows(nt, nct, tm, nw)
    pmat, pmat_t, inv_cnt = pc
    cblk = lambda a: Arg(a, (None, None) + a.shape[2:], lambda j, i, s_: (s_, j, 0, 0), "const")
    pool_args = [r4.row(proj, pg, c_p // pg, True), cblk(pmat), cblk(pmat_t), cblk(inv_cnt),
                 Arg(w["pool_w"], (None, pg, pg), lambda j, i, s_: (j, 0, 0), "acc"), r4.vec(w["pool_scale"], True)]
    du_pool, g["pool_w"], g["pool_scale"] = stage_bwd(n("pool"), f_pool, r4, pool_args, [r4.row(dps, pg, 0, True)], [ACT_DTYPE])

    dynw = matmul_nt(n("ssd_out_dx"), do_ssd, w["w_ssd_out"])
    g["w_ssd_out"] = matmul_tn(n("ssd_out_dw"), s["ynw"], do_ssd)
    G = SSD_GROUPS
    di = w["ssd_norm_w"].shape[1]
    gw = di // G
    r8 = _tall_rows(T, G)
    gate_args = [r8.row(y2, gw, 0, True), r8.row(y2, gw, 0, True, roff=r8.nt), r8.row(xbc, gw, 0, True, stride=2),
                 r8.row(proj, gw, c_z // gw, True), r8.vec(w["dskip"], True), r8.vec(w["ssd_norm_w"], True)]
    gate_args[1].kind = "const"
    ex = _hosted(hosts, box, "ssd_gate")
    res = stage_bwd(n("ssd_gate"), f_ssd_gate, r8, gate_args, [r8.row(dynw, gw, 0, True)], [F32, F32, ACT_DTYPE], ex)
    if ex is not None:
        res, box["ssd_gate"] = res
    dy, dxs_skip, dz, g["dskip"], g["ssd_norm_w"] = res

    ex = _hosted(hosts, box, "ssd")
    dxbc2, ddt, g["dt_bias"], g["a_log"], xres = ssd_bwd(n("ssd"), xbc, proj, c_dt // 128, w["dt_bias"], w["a_log"],
                                                         s["states"], dy, n_ctx, ex)
    if ex is not None:
        box["ssd"] = xres
    xbc_w = xbc.shape[1]
    ex = _hosted(hosts, box, "conv")
    dxbc_raw, g["conv_w"], g["conv_b"], xres = conv_bwd(n("conv"), proj, w["conv_w"], w["conv_b"], dxbc2, dxs_skip,
                                                         n_ctx, xbc_w, ex)
    if ex is not None:
        box["conv"] = xres
    pieces = [dxbc_raw, dz, dgl_s, dgl_p, du_pool, ddt]
    offsets = [0, c_z, c_g, c_g + pw, c_p, c_dt]
    ex = _hosted(hosts, box, "in_dx")
    dh1 = matmul_nt(n("in_dx"), pieces, w["w1"], ex=ex, offsets=offsets)
    if ex is not None:
        dh1, box["in_dx"] = dh1
    ex = _hosted(hosts, box, "in_dw")
    first = matmul_tn(n("in_dw0"), pieces[0], s["h1"], ex=ex)
    if ex is not None:
        first, box["in_dw"] = first
    g["w1"] = [first] + [matmul_tn(n(f"in_dw{k}"), p, s["h1"]) for k, p in enumerate(pieces) if k]

    n1_args = [rows.row(x, D), rows.vec(w["g_mix"]), rows.segvec(sh1), rows.segvec(sc1)]
    dx, g["g_mix"], dsh1, dsc1 = stage_bwd(n("norm1"), f_norm_mod_keep, rows, n1_args,
                                           [rows.row(dh1, D), rows.row(dxr, D)], [F32])

    dm = jnp.concatenate([v.reshape(2, D) for v in (dsh1, dsc1, dga1, dsh2, dsc2, dga2)], axis=1)
    dm = jnp.concatenate([dm, jnp.zeros((COND_ROWS - 2, dm.shape[1]), F32)], axis=0)
    crow = Rows(1, 0, COND_ROWS)
    dmraw, g["b_ada"] = stage_bwd(n("ada_bias"), f_bias, crow, [crow.row(s["mraw"], dm.shape[1]), crow.vec(w["b_ada"])],
                                  [crow.row(dm, dm.shape[1])], [ACT_DTYPE])
    dcs = matmul_nt(n("ada_dx"), dmraw, w["w_ada"])
    g["w_ada"] = matmul_tn(n("ada_dw"), cond_s, dmraw)
    return dx, dcs, g


def local_step(x, ctx, c, c_ctx, target, layer_w_fn, n_layers, g_final, fwd_hosts=None, bwd_hosts=None):
    L, D = x.shape
    n_ctx = ctx.shape[0]
    tm = ROW_TILE
    T = L + n_ctx
    rows = Rows(T // tm, n_ctx // tm, tm)
    pc = _pool_consts(tm, n_ctx)
    xa = jnp.concatenate([ctx, x], axis=0)
    cond = jnp.concatenate([c_ctx.reshape(1, D), c.reshape(1, D), jnp.zeros((COND_ROWS - 2, D), F32)], axis=0)
    crow = Rows(1, 0, COND_ROWS)
    (cond_s,) = stage_fwd("cond_silu", f_silu, crow, [crow.row(cond, D)], [(D, ACT_DTYPE, False)])

    saved, layer_w = [], []
    for l in range(n_layers):
        layer_w.append(layer_w_fn(l))
        box = {}
        xa, s = _layer_fwd(l, xa, cond_s, layer_w[l], rows, n_ctx, pc, fwd_hosts(l, box) if fwd_hosts else None, box)
        saved.append(s)

    rl = Rows(L // tm, 0, tm)
    gf = g_final.reshape(1, D)
    tgt = rl.row(target, D)
    tgt.kind = "const"
    loss_args = [rl.row(xa, D, roff=n_ctx // tm), tgt, rl.vec(gf)]
    (loss_rows,) = stage_fwd("loss", f_loss, rl, loss_args, [(1, F32, False)])
    ones = jnp.ones((L, 1), F32)
    dx_lat, dgf = stage_bwd("loss_bwd", f_loss, rl, loss_args, [rl.row(ones, 1)], [F32])
    loss = jnp.sum(loss_rows)
    dx = jnp.concatenate([jnp.zeros((n_ctx, D), F32), dx_lat], axis=0)

    grads = [None] * n_layers
    dcs = jnp.zeros((COND_ROWS, D), F32)
    for l in reversed(range(n_layers)):
        box = {}
        hosts = bwd_hosts(l, grads, box) if bwd_hosts else None
        dx, dcs_l, grads[l] = _layer_bwd(l, dx, cond_s, layer_w[l], saved[l], rows, n_ctx, pc, hosts, box)
        dcs = dcs + dcs_l
    (dcond,) = stage_bwd("cond_silu_bwd", f_silu, crow, [crow.row(cond, D)], [crow.row(dcs, D)], [F32])
    return loss, dx[n_ctx:], grads, dcond[0], dgf


def gather_chips(halves, conv=None):
    n = len(halves)
    ops = list(halves) + ([conv] if conv is not None else [])

    def copies(ins, outs, pos):
        c, me = pos[2], _chip_index(pos)
        pairs = [(s.at[c], o.at[me, c]) for s, o in zip(ins[:n], outs[:n])]
        pairs += [(s, o.at[me]) for s, o in zip(ins[n:], outs[n:])]
        return pairs, [(s, d, _flip(pos, rel)) for rel in PLANE for s, d in pairs]

    shapes = [jax.ShapeDtypeStruct((4,) + s.shape, s.dtype) for s in ops]
    return Exchange(copies, 3 * len(ops), len(ops), ops, shapes)


def gather_pair(gathered):
    n = len(gathered)

    def copies(ins, outs, pos):
        c = pos[2]
        return [], [(s.at[b, c], o.at[b, c], _flip(pos, PAIR[0])) for s, o in zip(ins, outs) for b in range(4)]

    shapes = [jax.ShapeDtypeStruct(g.shape, g.dtype) for g in gathered]
    return Exchange(copies, 4 * n, 0, gathered, shapes, aliases={k: k for k in range(n)})


def swap_halves(grads):
    n = len(grads)

    def copies(ins, outs, pos):
        c = pos[2]
        return [], [(g.at[b, 1 - c], o.at[b], _flip(pos, PAIR[0])) for g, o in zip(ins, outs) for b in range(4)]

    shapes = [jax.ShapeDtypeStruct((g.shape[0],) + g.shape[2:], g.dtype) for g in grads]
    return Exchange(copies, 4 * n, 0, grads, shapes)


def scatter_chips(sums):
    n = len(sums)

    def copies(ins, outs, pos):
        me = _chip_index(pos)
        local = [(p.at[me], o.at[me]) for p, o in zip(ins, outs)]
        remote = []
        for rel in PLANE:
            peer = _flip(pos, rel)
            remote += [(p.at[_chip_index(peer)], o.at[me], peer) for p, o in zip(ins, outs)]
        return local, remote

    shapes = [jax.ShapeDtypeStruct(p.shape, p.dtype) for p in sums]
    return Exchange(copies, 3 * n, n, sums, shapes)


def share_halves(finals):
    n = len(finals)

    def copies(ins, outs, pos):
        c = pos[2]
        return [], [(f.at[c], o.at[c], _flip(pos, PAIR[0])) for f, o in zip(ins, outs)]

    shapes = [jax.ShapeDtypeStruct(f.shape, f.dtype) for f in finals]
    return Exchange(copies, n, 0, finals, shapes, aliases={k: k for k in range(n)})


def gather_everyone(vec):
    def copies(ins, outs, pos):
        me = _device_index(pos)
        (v,), (o,) = ins, outs
        return [(v, o.at[me])], [(v, o.at[me], _flip(pos, rel)) for rel in EVERYONE]

    return Exchange(copies, len(EVERYONE), 1, [vec], [jax.ShapeDtypeStruct((8,) + vec.shape, vec.dtype)])


def _row_tile(rows, cols, n_bufs, mult=8):
    cap = VMEM_LIMIT_BYTES // 2 // (2 * n_bufs * cols * 4)
    for t in range(min(rows, cap) // mult * mult, 0, -mult):
        if rows % t == 0:
            return t
    return rows


WIRE_DTYPE = jnp.bfloat16


def add_own_half(name, grads, recv, c):
    nb, _, R, C = grads.shape
    tr = _row_tile(R, C, 3, mult=16)

    def body(c_ref, g_ref, r_ref, o_ref):
        o_ref[...] = (g_ref[...] + r_ref[...]).astype(o_ref.dtype)

    spec = pl.BlockSpec((None, tr, C), lambda b, i, c_ref: (b, i, 0))
    return _pcall(
        body, name=name, out_shape=jax.ShapeDtypeStruct(recv.shape, WIRE_DTYPE),
        grid_spec=pltpu.PrefetchScalarGridSpec(
            num_scalar_prefetch=1, grid=(nb, R // tr),
            in_specs=[pl.BlockSpec((None, None, tr, C), lambda b, i, c_ref: (b, c_ref[0], i, 0)), spec],
            out_specs=spec),
        compiler_params=_params("parallel", "parallel"),
    )(c, grads, recv)


def sum_slots(name, a, c=None):
    n, R, C = a.shape
    tr = _row_tile(R, C, n + 1, mult=16 if a.dtype.itemsize == 2 else 8)

    def body(*refs):
        a_ref, o_ref = refs[-2:]
        acc = a_ref[0].astype(F32)
        for k in range(1, n):
            acc = acc + a_ref[k].astype(F32)
        o_ref[...] = acc

    if c is None:
        return _pcall(
            body, name=name, out_shape=jax.ShapeDtypeStruct((R, C), F32), grid=(R // tr,),
            in_specs=[pl.BlockSpec((n, tr, C), lambda i: (0, i, 0))], out_specs=pl.BlockSpec((tr, C), lambda i: (i, 0)),
            compiler_params=_params("parallel"),
        )(a)
    return _pcall(
        body, name=name, out_shape=jax.ShapeDtypeStruct((2, R, C), F32),
        grid_spec=pltpu.PrefetchScalarGridSpec(
            num_scalar_prefetch=1, grid=(R // tr,),
            in_specs=[pl.BlockSpec((n, tr, C), lambda i, c_ref: (0, i, 0))],
            out_specs=pl.BlockSpec((None, tr, C), lambda i, c_ref: (c_ref[0], i, 0))),
        compiler_params=_params("parallel"),
    )(c, a)


def adamw(name, w, g_layers, m, v):
    nl, R, C = w.shape
    assert len(g_layers) == nl
    tr = _row_tile(R, C, 8 + nl)
    nr = R // tr

    def body(*refs):
        w_ref, m_ref, v_ref = refs[:3]
        g_refs = refs[3:3 + nl]
        go_ref, d_ref, nm_ref, nv_ref = refs[3 + nl:]
        l = pl.program_id(0)
        gr = g_refs[0][...]
        for k in range(1, nl):
            gr = jnp.where(l == k, g_refs[k][...], gr)
        nm = ADAM_B1 * m_ref[...] + (1.0 - ADAM_B1) * gr
        nv = ADAM_B2 * v_ref[...] + (1.0 - ADAM_B2) * jnp.square(gr)
        m_hat = nm / (1.0 - ADAM_B1 ** ADAM_STEP)
        v_hat = nv / (1.0 - ADAM_B2 ** ADAM_STEP)
        d_ref[...] = -ADAM_LR * (m_hat / (jnp.sqrt(v_hat) + ADAM_EPS) + ADAM_WD * w_ref[...])
        go_ref[...] = gr
        nm_ref[...] = nm
        nv_ref[...] = nv

    spec = pl.BlockSpec((None, tr, C), lambda l, i: (l, i, 0))
    g_specs = [pl.BlockSpec((tr, C), (lambda l, i, k=k: (jnp.where(l == k, i, jnp.where(l < k, 0, nr - 1)), 0)))
               for k in range(nl)]
    return _pcall(
        body, name=name, out_shape=[jax.ShapeDtypeStruct((nl, R, C), F32)] * 4, grid=(nl, nr),
        in_specs=[spec] * 3 + g_specs, out_specs=[spec] * 4, compiler_params=_params("arbitrary", "arbitrary"),
    )(w, m, v, *g_layers)


BIG = ("w_ada", "w_in", "w_ssd_out", "pool_w", "w_pool_out", "w_out", "w_gate_up", "w_down")
COL_SHARDED = ("w_ada", "w_in", "w_gate_up")
GRAD_TRANSPOSED = ("w_in",)
FIRST_USED = ("w_ada", "w_in")
LATER_USED = tuple(k for k in BIG if k not in FIRST_USED)
READY_LAST = FIRST_USED
READY_EARLY = LATER_USED
SMALL = ("c_ctx", "b_ada", "g_mix", "conv_w", "conv_b", "dt_bias", "a_log", "d_skip", "ssd_norm_w", "pool_scale",
         "g_ffn", "g_final")
WEIGHTS = ("c_ctx", "w_ada", "b_ada", "g_mix", "w_in", "conv_w", "conv_b", "dt_bias", "a_log", "d_skip", "ssd_norm_w",
           "w_ssd_out", "pool_w", "pool_scale", "w_pool_out", "w_out", "g_ffn", "w_gate_up", "w_down", "g_final")
LAYER_KEYS = ("w_ada", "b_ada", "g_mix", "w_in", "conv_w", "conv_b", "dt_bias", "a_log", "d_skip", "ssd_norm_w",
              "w_ssd_out", "pool_w", "pool_scale", "w_pool_out", "w_out", "g_ffn", "w_gate_up", "w_down")


def _shard2d(name, a):
    if name == "pool_w":
        return a.reshape(a.shape[0], a.shape[1] * a.shape[2], a.shape[3])
    return a


def _full_from_blocks(name, a):
    nb, R, C = a.shape
    if name in COL_SHARDED:
        return jnp.transpose(a, (1, 0, 2)).reshape(R, nb * C)
    if name == "pool_w":
        nw = len(POOL_WINDOWS)
        return jnp.transpose(a.reshape(nb, nw, R // nw, C), (1, 0, 2, 3)).reshape(nw, nb * R // nw, C)
    return a.reshape(nb * R, C)


def _blocks_from_full(name, g):
    nb = 4
    if name in COL_SHARDED and name not in GRAD_TRANSPOSED:
        K, N = g.shape
        return jnp.transpose(g.reshape(K, nb, N // nb), (1, 0, 2))
    if name == "pool_w":
        nw, r, C = g.shape
        return jnp.transpose(g.reshape(nw, nb, r // nb, C), (1, 0, 2, 3)).reshape(nb, nw * r // nb, C)
    return g.reshape(nb, g.shape[0] // nb, g.shape[1])


def _pack(arrs, rows):
    flat = jnp.concatenate([a.reshape(-1).astype(F32) for a in arrs])
    return jnp.concatenate([flat, jnp.zeros((rows * 128 - flat.size,), F32)]).reshape(rows, 128)


def _unpack(vec, shapes):
    flat = vec.reshape(-1)
    out, o = [], 0
    for s in shapes:
        n = int(np.prod(s))
        out.append(flat[o:o + n].reshape(s))
        o += n
    return out


def _rows_for(shapes):
    n = sum(int(np.prod(s)) for s in shapes)
    return -(-n // (8 * 128)) * 8


def kernel(x, c, ctx, c_ctx, w_ada, b_ada, g_mix, w_in, conv_w, conv_b, dt_bias, a_log, d_skip, ssd_norm_w, w_ssd_out, pool_w, pool_scale, w_pool_out, w_out, g_ffn, w_gate_up, w_down, g_final, loss_target, m_c_ctx, m_w_ada, m_b_ada, m_g_mix, m_w_in, m_conv_w, m_conv_b, m_dt_bias, m_a_log, m_d_skip, m_ssd_norm_w, m_w_ssd_out, m_pool_w, m_pool_scale, m_w_pool_out, m_w_out, m_g_ffn, m_w_gate_up, m_w_down, m_g_final, v_c_ctx, v_w_ada, v_b_ada, v_g_mix, v_w_in, v_conv_w, v_conv_b, v_dt_bias, v_a_log, v_d_skip, v_ssd_norm_w, v_w_ssd_out, v_pool_w, v_pool_scale, v_w_pool_out, v_w_out, v_g_ffn, v_w_gate_up, v_w_down, v_g_final):
    w = dict(c_ctx=c_ctx, w_ada=w_ada, b_ada=b_ada, g_mix=g_mix, w_in=w_in, conv_w=conv_w, conv_b=conv_b, dt_bias=dt_bias,
             a_log=a_log, d_skip=d_skip, ssd_norm_w=ssd_norm_w, w_ssd_out=w_ssd_out, pool_w=pool_w, pool_scale=pool_scale,
             w_pool_out=w_pool_out, w_out=w_out, g_ffn=g_ffn, w_gate_up=w_gate_up, w_down=w_down, g_final=g_final)
    m = dict(c_ctx=m_c_ctx, w_ada=m_w_ada, b_ada=m_b_ada, g_mix=m_g_mix, w_in=m_w_in, conv_w=m_conv_w, conv_b=m_conv_b,
             dt_bias=m_dt_bias, a_log=m_a_log, d_skip=m_d_skip, ssd_norm_w=m_ssd_norm_w, w_ssd_out=m_w_ssd_out,
             pool_w=m_pool_w, pool_scale=m_pool_scale, w_pool_out=m_w_pool_out, w_out=m_w_out, g_ffn=m_g_ffn,
             w_gate_up=m_w_gate_up, w_down=m_w_down, g_final=m_g_final)
    v = dict(c_ctx=v_c_ctx, w_ada=v_w_ada, b_ada=v_b_ada, g_mix=v_g_mix, w_in=v_w_in, conv_w=v_conv_w, conv_b=v_conv_b,
             dt_bias=v_dt_bias, a_log=v_a_log, d_skip=v_d_skip, ssd_norm_w=v_ssd_norm_w, w_ssd_out=v_w_ssd_out,
             pool_w=v_pool_w, pool_scale=v_pool_scale, w_pool_out=v_w_pool_out, w_out=v_w_out, g_ffn=v_g_ffn,
             w_gate_up=v_w_gate_up, w_down=v_w_down, g_final=v_g_final)
    assert x.shape[0] == 1, "one example per device"
    pos = _position()
    core = pos[2].astype(jnp.int32).reshape(1)
    n_layers = w_in.shape[0]
    assert n_layers == 2
    dims = (ssd_norm_w.shape[1], conv_w.shape[2] * 4, dt_bias[0].size, pool_scale.shape[1])
    shard = {k: _shard2d(k, w[k]) for k in BIG}

    def halves(a):
        return a.reshape(a.shape[:-2] + (2, a.shape[-2] // 2, a.shape[-1]))

    def whole(a):
        return a.reshape(a.shape[:-3] + (2 * a.shape[-2], a.shape[-1]))

    def wire_shards(l, names):
        return [halves(shard[k][l].astype(MXU_DTYPE)) for k in names]

    def full_weights(names, gathered):
        return {k: _full_from_blocks(k, whole(a)) for k, a in zip(names, gathered)}

    first = comm_call("gather0_chips", gather_chips(wire_shards(0, FIRST_USED), conv=conv_w))
    got0 = full_weights(FIRST_USED, comm_call("gather0_pair", gather_pair(first[:-1])))
    conv_all = first[-1]
    conv_full = [jnp.transpose(conv_all[:, l], (1, 0, 2)).reshape(conv_all.shape[2], -1) for l in range(n_layers)]

    boxes = {}

    def layer_w_fn(l):
        if l == 0:
            full = dict(got0)
            late = {k: None for k in LATER_USED}
        else:
            full = full_weights(BIG, boxes[("fwd", 0)]["gate_up_mm"])
            late = {}
        full["conv_w"] = conv_full[l]
        lw = LazyDict(_prep_layer_weights(*[full[k] if k in full else (None if k in late else w[k][l]) for k in LAYER_KEYS]))
        for i, k in enumerate(late):
            lw[k] = (lambda i=i, k=k: _full_from_blocks(k, whole(boxes[("fwd", 0)]["conv"][i])))
        return lw

    def fwd_hosts(l, box):
        boxes[("fwd", l)] = box
        if l != 0:
            return None
        return {"in_mm": lambda box: gather_chips(wire_shards(0, LATER_USED)), "conv": lambda box: gather_pair(box["in_mm"]),
                "ssd": lambda box: gather_chips(wire_shards(1, BIG)), "gate_up_mm": lambda box: gather_pair(box["ssd"])}

    def blocks(gl, names):
        return [halves(_blocks_from_full(k, gl[k])) for k in names]

    def pair_sums(tag, names, G, recv):
        return [add_own_half(f"pair_sum{tag}_{k}", g, r, core) for k, g, r in zip(names, G, recv)]

    def chip_sums(tag, names, parts):
        return [sum_slots(f"chip_sum{tag}_{k}", p, core) for k, p in zip(names, parts)]

    def reduce_now(tag, gl, names):
        G = blocks(gl, names)
        pair = pair_sums(tag, names, G, comm_call(f"swap{tag}", swap_halves(G)))
        fin = chip_sums(tag, names, comm_call(f"scatter{tag}", scatter_chips(pair)))
        return [whole(a) for a in comm_call(f"share{tag}", share_halves(fin))]

    small_layers = {}
    n_big = len(BIG)

    def bwd_hosts(l, grads, box):
        boxes[("bwd", l)] = box
        if l != 0:
            return None
        gl1 = _unprep_layer_grads(grads[1], dims)
        small_layers[1] = gl1
        G1 = blocks(gl1, BIG)
        early = {}

        def gate_host(box):
            early["G"] = blocks(box["g"], READY_EARLY)
            return swap_halves(early["G"])

        def scan_host(box):
            return combine(scatter_chips(pair_sums("1", BIG, G1, box["down_dx"])),
                           scatter_chips(pair_sums("0e", READY_EARLY, early["G"], box["ssd_gate"])))

        def conv_host(box):
            return combine(share_halves(chip_sums("1", BIG, box["ssd"][:n_big])),
                           share_halves(chip_sums("0e", READY_EARLY, box["ssd"][n_big:])))

        return {"down_dx": lambda box: swap_halves(G1), "ssd_gate": gate_host, "ssd": scan_host, "in_dx": conv_host}

    loss, grad_x, grads, d_c_ctx, d_g_final = local_step(
        x[0], ctx[0], c[0], c_ctx, loss_target[0], layer_w_fn, n_layers, g_final, fwd_hosts, bwd_hosts)
    loss = lax.psum(loss, ("x", "y", "c"))
    shared = [whole(a) for a in boxes[("bwd", 0)]["in_dx"]]
    reduced1 = shared[:n_big]
    gl0 = _unprep_layer_grads(grads[0], dims)
    small_layers[0] = gl0
    red0 = dict(zip(READY_EARLY, shared[n_big:]))
    red0.update(zip(READY_LAST, reduce_now("0", gl0, READY_LAST)))
    reduced0 = [red0[k] for k in BIG]

    small_full = dict(c_ctx=d_c_ctx, g_final=d_g_final.reshape(-1))
    for k in SMALL:
        if k not in small_full:
            small_full[k] = jnp.stack([small_layers[l][k] for l in range(n_layers)])
    shapes = [small_full[k].shape for k in SMALL]
    packed = _pack([small_full[k] for k in SMALL], _rows_for(shapes))
    total = sum_slots("small_sum", comm_call("gather_small", gather_everyone(packed))[0])
    small_g = dict(zip(SMALL, _unpack(total, shapes)))
    cw = conv_w.shape[2]
    small_g["conv_w"] = lax.dynamic_slice_in_dim(small_g["conv_w"], _chip_index(pos) * cw, cw, axis=2)

    grad, delta, new_m, new_v = {}, {}, {}, {}
    for k, g0, g1 in zip(BIG, reduced0, reduced1):
        shp = w[k].shape
        if k in GRAD_TRANSPOSED:
            flat = lambda a: jnp.swapaxes(a, 1, 2)
            back = lambda a: jnp.swapaxes(a, 1, 2)
        else:
            flat = lambda a: _shard2d(k, a)
            back = lambda a: a.reshape(shp)
        outs = adamw(f"adamw_{k}", flat(w[k]), [g0, g1], flat(m[k]), flat(v[k]))
        grad[k], delta[k], new_m[k], new_v[k] = [back(a) for a in outs]
    sshapes = [w[k].shape for k in SMALL]
    srows = _rows_for(sshapes)
    pk = lambda d: _pack([d[k] for k in SMALL], srows)[None]
    _, d_, m_, v_ = adamw("adamw_small", pk(w), [pk(small_g)[0]], pk(m), pk(v))
    for k, dd, mm, vv in zip(SMALL, _unpack(d_, sshapes), _unpack(m_, sshapes), _unpack(v_, sshapes)):
        grad[k], delta[k], new_m[k], new_v[k] = small_g[k], dd, mm, vv

    return (loss, grad_x[None], *[grad[k] for k in WEIGHTS], *[delta[k] for k in WEIGHTS],
            *[new_m[k] for k in WEIGHTS], *[new_v[k] for k in WEIGHTS])
```

```python
import functools

import jax
import jax.numpy as jnp
import numpy as np
from jax import lax
from jax.experimental import pallas as pl
from jax.experimental.pallas import tpu as pltpu

F32 = jnp.float32
MXU_DTYPE = jnp.bfloat16
ACT_DTYPE = jnp.bfloat16
VMEM_LIMIT_BYTES = 48 * 1024 * 1024
EPS = 1e-6
NEG = -1e30

SSD_HEADDIM = 64
SSD_GROUPS = 8
SSD_STATE = 128
SSD_CHUNK = 128
SSD_GROUPS_PER_STEP = 8
SSD_CONV = 5
GRID_W = 64
POOL_WINDOWS = (2, 4, 8, 16)
ROW_TILE = 256
DT_PAD = 512

ADAM_LR = 0.001
ADAM_B1 = 0.9
ADAM_B2 = 0.999
ADAM_EPS = 1e-08
ADAM_WD = 0.01
ADAM_STEP = 10

MESH = pl.DeviceIdType.MESH


def _pcall(body, **kw):
    return pl.pallas_call(body, **kw)


def _params(*sem):
    return pltpu.CompilerParams(dimension_semantics=tuple(sem), vmem_limit_bytes=VMEM_LIMIT_BYTES)


def _pick_tile(n, cands):
    for t in cands:
        if n % t == 0:
            return t
    return n


PLANE = ((1, 0, 0), (0, 1, 0), (1, 1, 0))
PAIR = ((0, 0, 1),)
EVERYONE = tuple((a, b, d) for a in (0, 1) for b in (0, 1) for d in (0, 1) if a + b + d)
HBM = pl.BlockSpec(memory_space=pl.ANY)


def _position():
    return lax.axis_index("x"), lax.axis_index("y"), lax.axis_index("c")


def _flip(pos, rel):
    return tuple(1 - p if r else p for p, r in zip(pos, rel))


def _chip_index(pos):
    return 2 * pos[0] + pos[1]


def _device_index(pos):
    return 4 * pos[0] + 2 * pos[1] + pos[2]


class Exchange:
    def __init__(self, copies, n_remote, n_local, operands, out_shapes, aliases=None):
        self.copies, self.n_remote, self.n_local = copies, n_remote, n_local
        self.operands, self.out_shapes, self.aliases = list(operands), list(out_shapes), dict(aliases or {})

    def scratch(self):
        return [pltpu.SemaphoreType.DMA((max(self.n_remote, 1),)), pltpu.SemaphoreType.DMA((max(self.n_remote, 1),)),
                pltpu.SemaphoreType.DMA((max(self.n_local, 1),))]

    def descriptors(self, ins, outs, sems):
        send_sems, recv_sems, local_sems = sems
        local, remote = self.copies(ins, outs, _position())
        assert len(local) == self.n_local and len(remote) == self.n_remote
        cps = [pltpu.make_async_copy(src, dst, local_sems.at[k]) for k, (src, dst) in enumerate(local)]
        cps += [pltpu.make_async_remote_copy(src_ref=src, dst_ref=dst, send_sem=send_sems.at[k], recv_sem=recv_sems.at[k],
                                             device_id=peer, device_id_type=MESH) for k, (src, dst, peer) in enumerate(remote)]
        return cps


def combine(a, b):
    na, nao = len(a.operands), len(a.out_shapes)

    def copies(ins, outs, pos):
        la, ra = a.copies(ins[:na], outs[:nao], pos)
        lb, rb = b.copies(ins[na:], outs[nao:], pos)
        return la + lb, ra + rb

    aliases = dict(a.aliases)
    aliases.update({na + k: nao + v for k, v in b.aliases.items()})
    return Exchange(copies, a.n_remote + b.n_remote, a.n_local + b.n_local, a.operands + b.operands,
                    a.out_shapes + b.out_shapes, aliases)


class LazyDict(dict):
    def __getitem__(self, key):
        v = dict.__getitem__(self, key)
        if callable(v):
            v = v()
            dict.__setitem__(self, key, v)
        return v


def comm_call(name, ex):
    n_in, n_out = len(ex.operands), len(ex.out_shapes)

    def body(*refs):
        cps = ex.descriptors(refs[:n_in], refs[n_in:n_in + n_out], refs[n_in + n_out:])
        for cp in cps:
            cp.start()
        for cp in cps:
            cp.wait()

    return _pcall(
        body, name=name, out_shape=ex.out_shapes, in_specs=[HBM] * n_in, out_specs=[HBM] * n_out,
        scratch_shapes=ex.scratch(), input_output_aliases=ex.aliases,
        compiler_params=pltpu.CompilerParams(has_side_effects=True),
    )(*ex.operands)


def hosted_call(body, ex, operands, *, name, out_shape, grid, in_specs, out_specs, scratch_shapes=()):
    n_in, n_out, n_scr = len(operands), len(out_shape), len(scratch_shapes)
    sem = ("arbitrary",) * len(grid)
    if ex is None:
        res = _pcall(body, name=name, out_shape=list(out_shape), grid=grid, in_specs=list(in_specs),
                     out_specs=list(out_specs), scratch_shapes=list(scratch_shapes), compiler_params=_params(*sem))(*operands)
        return res, []
    x_in, x_out = len(ex.operands), len(ex.out_shapes)

    def wrapped(*refs):
        o = 0
        ins = refs[o:o + n_in]; o += n_in
        xins = refs[o:o + x_in]; o += x_in
        outs = refs[o:o + n_out]; o += n_out
        xouts = refs[o:o + x_out]; o += x_out
        scr = refs[o:o + n_scr]; o += n_scr
        sems = refs[o:]
        first = last = None
        for a, n in enumerate(grid):
            i = pl.program_id(a)
            first = (i == 0) if first is None else first & (i == 0)
            last = (i == n - 1) if last is None else last & (i == n - 1)

        @pl.when(first)
        def _():
            for cp in ex.descriptors(xins, xouts, sems):
                cp.start()

        body(*ins, *outs, *scr)

        @pl.when(last)
        def _():
            for cp in ex.descriptors(xins, xouts, sems):
                cp.wait()

    aliases = {n_in + k: n_out + v for k, v in ex.aliases.items()}
    res = _pcall(
        wrapped, name=name, out_shape=list(out_shape) + ex.out_shapes, grid=grid,
        in_specs=list(in_specs) + [HBM] * x_in, out_specs=list(out_specs) + [HBM] * x_out,
        scratch_shapes=list(scratch_shapes) + ex.scratch(), input_output_aliases=aliases,
        compiler_params=pltpu.CompilerParams(dimension_semantics=sem, vmem_limit_bytes=VMEM_LIMIT_BYTES,
                                             has_side_effects=True),
    )(*operands, *ex.operands)
    return res[:n_out], res[n_out:]


def _dot(a, b, dims):
    return lax.dot_general(a.astype(MXU_DTYPE), b.astype(MXU_DTYPE), (dims, ((), ())), preferred_element_type=F32)


_NN = ((1,), (0,))
_NT = ((1,), (1,))
_TN = ((0,), (0,))


@jax.custom_vjp
def _mm(a, b):
    return _dot(a, b, _NN)


def _mm_fwd(a, b):
    return _mm(a, b), (a, b)


def _mm_bwd(res, g):
    a, b = res
    return _dot(g, b, _NT).astype(a.dtype), _dot(a, g, _TN).astype(b.dtype)


_mm.defvjp(_mm_fwd, _mm_bwd)


@jax.custom_vjp
def _mm_nt(a, b):
    return _dot(a, b, _NT)


def _mm_nt_fwd(a, b):
    return _mm_nt(a, b), (a, b)


def _mm_nt_bwd(res, g):
    a, b = res
    return _dot(g, b, _NN).astype(a.dtype), _dot(g, a, _TN).astype(b.dtype)


_mm_nt.defvjp(_mm_nt_fwd, _mm_nt_bwd)


@jax.custom_vjp
def _mm_tn(a, b):
    return _dot(a, b, _TN)


def _mm_tn_fwd(a, b):
    return _mm_tn(a, b), (a, b)


def _mm_tn_bwd(res, g):
    a, b = res
    return _dot(b, g, _NT).astype(a.dtype), _dot(a, g, _NN).astype(b.dtype)


_mm_tn.defvjp(_mm_tn_fwd, _mm_tn_bwd)


def _dot_exact(m01, v):
    m = m01.astype(jnp.bfloat16)
    hi = v.astype(jnp.bfloat16)
    r1 = v - hi.astype(F32)
    mid = r1.astype(jnp.bfloat16)
    lo = (r1 - mid.astype(F32)).astype(jnp.bfloat16)
    out = jnp.dot(m, hi, preferred_element_type=F32)
    out = out + jnp.dot(m, mid, preferred_element_type=F32)
    return out + jnp.dot(m, lo, preferred_element_type=F32)


@jax.custom_vjp
def _lin01(m, mt, v):
    return _dot_exact(m, v)


def _lin01_fwd(m, mt, v):
    return _dot_exact(m, v), (m, mt)


def _lin01_bwd(res, g):
    m, mt = res
    return jnp.zeros_like(m), jnp.zeros_like(mt), _dot_exact(mt, g)


_lin01.defvjp(_lin01_fwd, _lin01_bwd)


MATMUL_VMEM_BUDGET = VMEM_LIMIT_BYTES * 3 // 4


def _mm_tiles(m, n, k_bytes_a, k_bytes_b, out_bytes, cands_m, cands_n):
    best = None
    for tm in cands_m:
        if m % tm:
            continue
        for tn in cands_n:
            if n % tn:
                continue
            need = 2 * (tm * k_bytes_a + tn * k_bytes_b + tm * tn * out_bytes)
            if need <= MATMUL_VMEM_BUDGET and (best is None or tm * tn > best[0] * best[1]):
                best = (tm, tn)
    assert best is not None, (m, n)
    return best


_ROW_CANDS = (4352, 2176, 1088, 768, 544, 512, 272, 256, 128, 16)
_COL_CANDS = (2816, 2048, 1408, 1024, 512, 256, 128)


def _one(res, xres, ex):
    return res[0] if ex is None else (res[0], xres)


def matmul_nn(name, a, b, out_dtype=F32, ex=None):
    M, K = a.shape
    N = b.shape[1]
    tm, tn = _mm_tiles(M, N, K * a.dtype.itemsize, K * b.dtype.itemsize, jnp.dtype(out_dtype).itemsize,
                       _ROW_CANDS, (512, 256, 128))

    def body(a_ref, b_ref, o_ref):
        o_ref[...] = _dot(a_ref[...], b_ref[...], _NN).astype(o_ref.dtype)

    res, xres = hosted_call(
        body, ex, [a, b], name=name, out_shape=[jax.ShapeDtypeStruct((M, N), out_dtype)], grid=(N // tn, M // tm),
        in_specs=[pl.BlockSpec((tm, K), lambda j, i: (i, 0)), pl.BlockSpec((K, tn), lambda j, i: (0, j))],
        out_specs=[pl.BlockSpec((tm, tn), lambda j, i: (i, j))])
    return _one(res, xres, ex)


def matmul_nt(name, g, b, out_dtype=F32, ex=None, offsets=None):
    pieces = list(g) if isinstance(g, (list, tuple)) else [g]
    offsets = list(offsets) if offsets is not None else [0]
    M = pieces[0].shape[0]
    K, N = b.shape
    g_bytes = sum(p.shape[1] * p.dtype.itemsize for p in pieces)
    tm, tk = _mm_tiles(M, K, g_bytes, N * b.dtype.itemsize, jnp.dtype(out_dtype).itemsize, _ROW_CANDS, _COL_CANDS)

    def body(*refs):
        b_ref, o_ref = refs[-2:]
        acc = None
        for g_ref, off in zip(refs[:-2], offsets):
            part = _dot(g_ref[...], b_ref[:, off:off + g_ref.shape[1]], _NT)
            acc = part if acc is None else acc + part
        o_ref[...] = acc.astype(o_ref.dtype)

    res, xres = hosted_call(
        body, ex, pieces + [b], name=name, out_shape=[jax.ShapeDtypeStruct((M, K), out_dtype)], grid=(K // tk, M // tm),
        in_specs=[pl.BlockSpec((tm, p.shape[1]), lambda j, i: (i, 0)) for p in pieces]
        + [pl.BlockSpec((tk, N), lambda j, i: (j, 0))],
        out_specs=[pl.BlockSpec((tm, tk), lambda j, i: (i, j))])
    return _one(res, xres, ex)


def matmul_tn(name, a, g, ex=None):
    M, K = a.shape
    N = g.shape[1]
    tk, tn = _mm_tiles(K, N, M * a.dtype.itemsize, M * g.dtype.itemsize, 4, (512, 256, 128), (512, 256, 128))

    def body(a_ref, g_ref, o_ref):
        o_ref[...] = _dot(a_ref[...], g_ref[...], _TN)

    res, xres = hosted_call(
        body, ex, [a, g], name=name, out_shape=[jax.ShapeDtypeStruct((K, N), F32)], grid=(K // tk, N // tn),
        in_specs=[pl.BlockSpec((M, tk), lambda i, j: (0, i)), pl.BlockSpec((M, tn), lambda i, j: (0, j))],
        out_specs=[pl.BlockSpec((tk, tn), lambda i, j: (i, j))])
    return _one(res, xres, ex)


class Arg:
    def __init__(self, arr, block, imap, kind):
        self.arr, self.block, self.imap, self.kind = arr, block, imap, kind


class Rows:
    def __init__(self, nt, nct, tm, ncol=1):
        self.nt, self.nct, self.tm, self.ncol = nt, nct, tm, ncol

    def seg(self, i):
        return jnp.where(i >= self.nct, 1, 0)

    def spec(self, block, imap):
        return pl.BlockSpec(block, lambda j, i: imap(j, i, self.seg(i)))

    def row(self, arr, width, cb0=0, follow=False, roff=0, stride=1):
        f = stride if follow else 0
        return Arg(arr, (self.tm, width), lambda j, i, s: (i + roff, cb0 + f * j), "row")

    def vec(self, arr, follow=False, kind="acc"):
        w = arr.shape[1] // (self.ncol if follow else 1)
        f = 1 if follow else 0
        return Arg(arr, (1, w), lambda j, i, s: (0, f * j), kind)

    def segvec(self, arr, kind="seg"):
        return Arg(arr, (None, 1, arr.shape[2]), lambda j, i, s: (s, 0, 0), kind)


def _load(ref):
    return ref[...].astype(F32) if ref.dtype != F32 else ref[...]


def stage_fwd(name, f, rows, args, outs):
    n_in = len(args)

    def body(*refs):
        vals = [_load(r) for r in refs[:n_in]]
        res = f(*vals)
        for r, v in zip(refs[n_in:], res):
            r[...] = v.astype(r.dtype)

    T = rows.nt * rows.tm
    out_shape = [jax.ShapeDtypeStruct((T, w * (rows.ncol if fo else 1)), dt) for w, dt, fo in outs]
    out_specs = [pl.BlockSpec((rows.tm, w), (lambda j, i, fo=fo: (i, j if fo else 0))) for w, dt, fo in outs]
    res = _pcall(
        body, name=name, out_shape=out_shape, grid=(rows.ncol, rows.nt),
        in_specs=[rows.spec(a.block, a.imap) for a in args], out_specs=out_specs,
        compiler_params=_params("parallel", "parallel"),
    )(*[a.arr for a in args])
    return res


def stage_bwd(name, f, rows, args, cots, row_dtypes, ex=None, primal=()):
    n_in, n_ct = len(args), len(cots)
    diff = [k for k, a in enumerate(args) if a.kind != "const"]
    row_dt = {}
    for k in diff:
        if args[k].kind == "row":
            row_dt[k] = row_dtypes[len(row_dt)]

    def body(*refs):
        i = pl.program_id(1)
        vals = [_load(r) for r in refs[:n_in]]
        cts = tuple(_load(r) for r in refs[n_in:n_in + n_ct])
        outs = refs[n_in + n_ct:]

        def g(*dv):
            full = list(vals)
            for k, v in zip(diff, dv):
                full[k] = v
            return tuple(f(*full))

        prim, vjp = jax.vjp(g, *[vals[k] for k in diff])
        grads = vjp(cts)
        for o, v in zip(outs[len(diff):], prim):
            o[...] = v.astype(o.dtype)
        for k, o, gr in zip(diff, outs, grads):
            kind = args[k].kind
            if kind == "row":
                o[...] = gr.astype(o.dtype)
            else:
                first = (i == 0) | (i == rows.nct) if kind == "seg" else (i == 0)

                @pl.when(first)
                def _():
                    o[...] = gr.astype(o.dtype)

                @pl.when(jnp.logical_not(first))
                def _():
                    o[...] += gr.astype(o.dtype)

    T = rows.nt * rows.tm
    out_shape, out_specs = [], []
    for k in diff:
        a = args[k]
        if a.kind == "row":
            out_shape.append(jax.ShapeDtypeStruct((T, a.block[1] * (rows.ncol if _follows(a) else 1)), row_dt[k]))
            fo = _follows(a)
            out_specs.append(pl.BlockSpec(a.block, (lambda j, i, fo=fo: (i, j if fo else 0))))
        else:
            out_shape.append(jax.ShapeDtypeStruct(a.arr.shape, F32))
            out_specs.append(rows.spec(a.block, a.imap))
    for w, dt in primal:
        out_shape.append(jax.ShapeDtypeStruct((T, w), dt))
        out_specs.append(pl.BlockSpec((rows.tm, w), lambda j, i: (i, 0)))
    res, xres = hosted_call(
        body, ex, [a.arr for a in list(args) + list(cots)], name=name, out_shape=out_shape, grid=(rows.ncol, rows.nt),
        in_specs=[rows.spec(a.block, a.imap) for a in list(args) + list(cots)], out_specs=out_specs)
    return res if ex is None else (res, xres)


def _follows(a):
    return a.imap(1, 0, 0)[-1] != a.imap(0, 0, 0)[-1]


def _rms(x):
    return x * lax.rsqrt(jnp.mean(x * x, axis=-1, keepdims=True) + EPS)


def f_norm_mod(x, g, sh, sc):
    return ((_rms(x) * g) * (1.0 + sc) + sh,)


def f_resid_norm_mod(x, mo, ga, g, sh, sc):
    x1 = x + ga * mo
    return x1, (_rms(x1) * g) * (1.0 + sc) + sh


def f_resid(x, dn, ga):
    return (x + ga * dn,)


def f_silu(x):
    return (x * jax.nn.sigmoid(x),)


def f_bias(x, b):
    return (x + b,)


def f_ssd_gate(y0, y1, xs, z, dskip, nw):
    y = y0 + y1 + dskip * xs
    return (_rms(y * (z * jax.nn.sigmoid(z))) * nw,)


def f_pool(u, pmat, pmat_t, inv_cnt, pw, scale):
    pm = _lin01(pmat, pmat_t, u) * inv_cnt - u
    return (_mm(pm, pw) * scale,)


def f_merge(o_ssd, o_pool, gl_ssd, gl_pool):
    return (jax.nn.sigmoid(gl_ssd) * o_ssd + jax.nn.sigmoid(gl_pool) * o_pool,)


def _column_splitter(n):
    @jax.custom_vjp
    def split(x):
        w = x.shape[1] // n
        return tuple(x[:, k * w:(k + 1) * w] for k in range(n))

    def fwd(x):
        return split(x), None

    def bwd(_, g):
        return (jnp.concatenate(g, axis=1),)

    split.defvjp(fwd, bwd)
    return split


_halve_cols = _column_splitter(2)
_quarter_cols = _column_splitter(len(POOL_WINDOWS))


def f_swiglu(gu):
    a, b = _halve_cols(gu)
    return ((a * jax.nn.sigmoid(a)) * b,)


def f_pool_all(u, pmat, pmat_t, inv_cnt, scale, *pws):
    outs = [f_pool(part, pmat[k], pmat_t[k], inv_cnt[k], pws[k], 1.0)[0] for k, part in enumerate(_quarter_cols(u))]
    return (jnp.concatenate(outs, axis=1) * scale,)


def f_loss(x, tgt, g):
    err = _rms(x) * g - tgt
    return (0.5 * jnp.mean(err * err, axis=-1, keepdims=True),)


CONV_TILE = 128


def _shift_rows(v, j, n_ctx):
    if j == 0:
        return v
    T = v.shape[0]
    r = lax.broadcasted_iota(jnp.int32, v.shape, 0)
    lo = jnp.where(r >= n_ctx, n_ctx, 0)
    hi = jnp.where(r >= n_ctx, T, n_ctx)
    ok = (r + j >= lo) & (r + j < hi)
    return jnp.where(ok, pltpu.roll(v, (-j) % T, 0), 0.0)


def conv_fwd(name, proj, conv_w, conv_b, n_ctx, width, ex=None):
    T = proj.shape[0]
    half = SSD_CONV // 2

    def body(u_ref, w_ref, b_ref, o_ref):
        u = u_ref[...]
        pre = jnp.broadcast_to(b_ref[...], u.shape)
        for k in range(SSD_CONV):
            pre = pre + w_ref[k:k + 1, :] * _shift_rows(u, k - half, n_ctx)
        o_ref[...] = pre * jax.nn.sigmoid(pre)

    col = lambda t: (0, t)
    res, xres = hosted_call(
        body, ex, [proj, conv_w, conv_b], name=name, out_shape=[jax.ShapeDtypeStruct((T, width), F32)],
        grid=(width // CONV_TILE,),
        in_specs=[pl.BlockSpec((T, CONV_TILE), col), pl.BlockSpec((SSD_CONV, CONV_TILE), col),
                  pl.BlockSpec((1, CONV_TILE), col)],
        out_specs=[pl.BlockSpec((T, CONV_TILE), col)])
    return res[0], xres


def conv_bwd(name, proj, conv_w, conv_b, d_act2, d_skip, n_ctx, width, ex=None):
    T = proj.shape[0]
    half = SSD_CONV // 2

    def body(u_ref, w_ref, b_ref, c0_ref, c1_ref, cs_ref, du_ref, dw_ref, db_ref):
        t = pl.program_id(0)
        u = u_ref[...]
        pre = jnp.broadcast_to(b_ref[...], u.shape)
        for k in range(SSD_CONV):
            pre = pre + w_ref[k:k + 1, :] * _shift_rows(u, k - half, n_ctx)
        sg = jax.nn.sigmoid(pre)
        ct = c0_ref[...] + c1_ref[...] + jnp.where(t % 4 < 2, cs_ref[...], 0.0)
        dpre = ct * (sg * (1.0 + pre * (1.0 - sg)))
        du = jnp.zeros_like(u)
        for k in range(SSD_CONV):
            du = du + w_ref[k:k + 1, :] * _shift_rows(dpre, half - k, n_ctx)
            dw_ref[k:k + 1, :] = jnp.sum(dpre * _shift_rows(u, k - half, n_ctx), axis=0, keepdims=True)
        du_ref[...] = du.astype(du_ref.dtype)
        db_ref[...] = jnp.sum(dpre, axis=0, keepdims=True)

    col = lambda t: (0, t)
    skip_col = lambda t: (0, (t // 4) * 2 + jnp.minimum(t % 4, 1))
    res, xres = hosted_call(
        body, ex, [proj, conv_w, conv_b, d_act2, d_act2, d_skip], name=name,
        out_shape=[jax.ShapeDtypeStruct((T, width), ACT_DTYPE), jax.ShapeDtypeStruct((SSD_CONV, width), F32),
                   jax.ShapeDtypeStruct((1, width), F32)],
        grid=(width // CONV_TILE,),
        in_specs=[pl.BlockSpec((T, CONV_TILE), col), pl.BlockSpec((SSD_CONV, CONV_TILE), col),
                  pl.BlockSpec((1, CONV_TILE), col), pl.BlockSpec((T, CONV_TILE), col),
                  pl.BlockSpec((T, CONV_TILE), lambda t: (1, t)), pl.BlockSpec((T, CONV_TILE), skip_col)],
        out_specs=[pl.BlockSpec((T, CONV_TILE), col), pl.BlockSpec((SSD_CONV, CONV_TILE), col),
                   pl.BlockSpec((1, CONV_TILE), col)])
    return res[0], res[1], res[2], xres


@jax.custom_vjp
def _cumsum_mat(tri, tri_t, a):
    return jnp.dot(tri, a, precision=lax.Precision.HIGHEST, preferred_element_type=F32)


def _cumsum_fwd(tri, tri_t, a):
    return _cumsum_mat(tri, tri_t, a), (tri, tri_t)


def _cumsum_bwd(res, g):
    tri, tri_t = res
    return (jnp.zeros_like(tri), jnp.zeros_like(tri_t),
            jnp.dot(tri_t, g, precision=lax.Precision.HIGHEST, preferred_element_type=F32))


_cumsum_mat.defvjp(_cumsum_fwd, _cumsum_bwd)


def _ssd_dt(dtraw, dt_bias, a_log, tri, tri_t):
    dt_all = jax.nn.softplus(dtraw + dt_bias)
    a_all = dt_all * (-jnp.exp(a_log))
    return dt_all, a_all, _cumsum_mat(tri, tri_t, a_all)


def _ssd_chunk(xs, bm, cm, dt_all, a_all, s_all, s_in, mask, idx0):
    (xs,), (s_in,) = xs, s_in
    Q = xs.shape[0]
    hpg = xs.shape[1] // SSD_HEADDIM
    lane = lax.broadcasted_iota(jnp.int32, dt_all.shape, 1)
    head = lax.broadcasted_iota(jnp.int32, xs.shape, 1) // SSD_HEADDIM
    head1 = lax.broadcasted_iota(jnp.int32, (1, xs.shape[1]), 1) // SSD_HEADDIM

    def pick(v, r):
        return jnp.sum(jnp.where(lane == idx0 + r, v, 0.0), axis=1, keepdims=True)

    def expand(cols, hd):
        out = cols[hpg - 1]
        for r in range(hpg - 2, -1, -1):
            out = jnp.where(hd == r, cols[r], out)
        return out

    def spread(*cols):
        return expand([jnp.broadcast_to(c, xs.shape) for c in cols], head)

    dt_r = [pick(dt_all, r) for r in range(hpg)]
    s_r = [pick(s_all, r) for r in range(hpg)]
    stot_r = [jnp.sum(jnp.where(lane == idx0 + r, a_all, 0.0), keepdims=True).reshape(1, 1) for r in range(hpg)]

    xd = xs * spread(*dt_r)
    cb = _mm_nt(cm, bm)
    weights, stacked = [], []
    for r in range(hpg):
        sm = jnp.broadcast_to(s_r[r], (Q, Q))
        weights.append(cb * jnp.exp(jnp.where(mask, sm - sm.T, NEG)))
        stacked.append(jnp.where(head == r, xd, 0.0))
    y = spread(*[jnp.exp(c) for c in s_r]) * _mm(cm, s_in)
    y = y + _mm(jnp.concatenate(weights, axis=1), jnp.concatenate(stacked, axis=0))
    to_end = spread(*[jnp.exp(t - c) for t, c in zip(stot_r, s_r)])
    carry = expand([jnp.broadcast_to(jnp.exp(t), (1, xs.shape[1])) for t in stot_r], head1)
    s_out = carry * s_in + _mm_tn(bm, xd * to_end)
    return [y], [s_out]


def _scan_consts():
    q = SSD_CHUNK
    i = np.arange(q)[:, None]
    j = np.arange(q)[None, :]
    fwd = (j <= i).astype(np.float32)
    bwd = (j >= i).astype(np.float32)
    tri = np.stack([fwd, bwd])
    return jnp.asarray(tri), jnp.asarray(np.stack([fwd.T, bwd.T]))


def _chunk_of(d, k, ncc, nc):
    rev = jnp.where(k < ncc, ncc - 1 - k, nc - 1 + ncc - k)
    return jnp.where(d == 0, k, rev)


def ssd_fwd(name, xbc, proj, dt_cb, dt_bias, a_log, n_ctx, ex=None):
    T = xbc.shape[0]
    q, G = SSD_CHUNK, SSD_GROUPS
    nc, ncc = T // q, n_ctx // q
    gw = xbc.shape[1] // G
    xw = gw - 2 * SSD_STATE
    hpg = xw // SSD_HEADDIM
    nh = G * hpg
    tri, tri_t = _scan_consts()

    gs = SSD_GROUPS_PER_STEP

    def body(x_ref, dt_ref, bias_ref, alog_ref, tri_ref, trit_ref, y_ref, sin_ref, state):
        d, gb, k = pl.program_id(0), pl.program_id(1), pl.program_id(2)

        @pl.when(k == 0)
        def _():
            state[...] = jnp.zeros_like(state)

        tri_v = tri_ref[...]
        dt_all, a_all, s_all = _ssd_dt(dt_ref[...], bias_ref[...], alog_ref[...], tri_v, trit_ref[...])
        pairs = [(0, xw)]
        for j in range(gs):
            o = j * gw
            sin_ref[j] = state[j]
            ys, s_outs = _ssd_chunk(
                [x_ref[:, o + lo:o + hi] for lo, hi in pairs], x_ref[:, o + xw:o + xw + SSD_STATE],
                x_ref[:, o + xw + SSD_STATE:o + gw], dt_all, a_all, s_all, [state[j, :, lo:hi] for lo, hi in pairs],
                tri_v > 0.5, d * nh + (gb * gs + j) * hpg)
            for (lo, hi), y, s_out in zip(pairs, ys, s_outs):
                y_ref[:, j * xw + lo:j * xw + hi] = y
                state[j, :, lo:hi] = s_out

    ch = lambda d, g, k: _chunk_of(d, k, ncc, nc)
    res, xres = hosted_call(
        body, ex, [xbc, proj, dt_bias, a_log, tri, tri_t], name=name,
        out_shape=[jax.ShapeDtypeStruct((2 * T, G * xw), F32),
                   jax.ShapeDtypeStruct((2, nc, G, SSD_STATE, xw), F32)],
        grid=(2, G // gs, nc),
        in_specs=[pl.BlockSpec((q, gs * gw), lambda d, g, k: (ch(d, g, k), g)),
                  pl.BlockSpec((q, 128), lambda d, g, k: (ch(d, g, k), dt_cb)),
                  pl.BlockSpec((1, 128), lambda d, g, k: (0, 0)),
                  pl.BlockSpec((1, 128), lambda d, g, k: (0, 0)),
                  pl.BlockSpec((None, q, q), lambda d, g, k: (d, 0, 0)),
                  pl.BlockSpec((None, q, q), lambda d, g, k: (d, 0, 0))],
        out_specs=[pl.BlockSpec((q, gs * xw), lambda d, g, k: (d * nc + ch(d, g, k), g)),
                   pl.BlockSpec((None, None, gs, SSD_STATE, xw), lambda d, g, k: (d, k, g, 0, 0))],
        scratch_shapes=[pltpu.VMEM((gs, SSD_STATE, xw), F32)])
    return res[0], res[1], xres


def ssd_bwd(name, xbc, proj, dt_cb, dt_bias, a_log, states, dy, n_ctx, ex=None):
    T = xbc.shape[0]
    q, G = SSD_CHUNK, SSD_GROUPS
    nc, ncc = T // q, n_ctx // q
    gw = xbc.shape[1] // G
    xw = gw - 2 * SSD_STATE
    hpg = xw // SSD_HEADDIM
    nh = G * hpg
    tri, tri_t = _scan_consts()

    gs = SSD_GROUPS_PER_STEP

    def body(x_ref, dt_ref, bias_ref, alog_ref, tri_ref, trit_ref, sin_ref, dy_ref,
             dx_ref, ddt_ref, dbias_ref, dalog_ref, dstate):
        d, gb, k = pl.program_id(0), pl.program_id(1), pl.program_id(2)
        first = (d == 0) & (gb == 0) & (k == 0)

        @pl.when(first)
        def _():
            ddt_ref[...] = jnp.zeros_like(ddt_ref)
            dbias_ref[...] = jnp.zeros_like(dbias_ref)
            dalog_ref[...] = jnp.zeros_like(dalog_ref)

        @pl.when(k == 0)
        def _():
            dstate[...] = jnp.zeros_like(dstate)

        tri_v, trit_v = tri_ref[...], trit_ref[...]
        mask = tri_v > 0.5

        pairs = [(0, xw)]
        npair = len(pairs)
        per = 2 * npair + 2

        def fn(dtraw, bias, alog, *per_group):
            dt_all, a_all, s_all = _ssd_dt(dtraw, bias, alog, tri_v, trit_v)
            ys, s_outs = [], []
            for j in range(gs):
                grp = per_group[per * j:per * (j + 1)]
                y, s_out = _ssd_chunk(list(grp[:npair]), grp[npair], grp[npair + 1], dt_all, a_all, s_all,
                                      list(grp[npair + 2:]), mask, d * nh + (gb * gs + j) * hpg)
                ys += y
                s_outs += s_out
            return ys, s_outs

        per_group = []
        for j in range(gs):
            o = j * gw
            per_group += [x_ref[:, o + lo:o + hi] for lo, hi in pairs]
            per_group += [x_ref[:, o + xw:o + xw + SSD_STATE], x_ref[:, o + xw + SSD_STATE:o + gw]]
            per_group += [sin_ref[j, :, lo:hi] for lo, hi in pairs]
        _, vjp = jax.vjp(fn, dt_ref[...], bias_ref[...], alog_ref[...], *per_group)
        cts = vjp(([dy_ref[:, j * xw + lo:j * xw + hi] for j in range(gs) for lo, hi in pairs],
                   [dstate[j, :, lo:hi] for j in range(gs) for lo, hi in pairs]))
        ddt, dbias, dalog = cts[:3]
        for j in range(gs):
            o = j * gw
            grp = cts[3 + per * j:3 + per * (j + 1)]
            for (lo, hi), dxs, ds_in in zip(pairs, grp[:npair], grp[npair + 2:]):
                dx_ref[:, o + lo:o + hi] = dxs
                dstate[j, :, lo:hi] = ds_in
            dx_ref[:, o + xw:o + xw + SSD_STATE] = grp[npair]
            dx_ref[:, o + xw + SSD_STATE:o + gw] = grp[npair + 1]
        row0 = pl.multiple_of(_chunk_of(d, nc - 1 - k, ncc, nc) * q, q)
        ddt_ref[pl.ds(row0, q), :] += ddt
        dbias_ref[...] += dbias
        dalog_ref[...] += dalog

    ch = lambda d, g, k: _chunk_of(d, nc - 1 - k, ncc, nc)
    res, xres = hosted_call(
        body, ex, [xbc, proj, dt_bias, a_log, tri, tri_t, states, dy], name=name,
        out_shape=[jax.ShapeDtypeStruct((2 * T, G * gw), F32), jax.ShapeDtypeStruct((T, 128), F32),
                   jax.ShapeDtypeStruct((1, 128), F32), jax.ShapeDtypeStruct((1, 128), F32)],
        grid=(2, G // gs, nc),
        in_specs=[pl.BlockSpec((q, gs * gw), lambda d, g, k: (ch(d, g, k), g)),
                  pl.BlockSpec((q, 128), lambda d, g, k: (ch(d, g, k), dt_cb)),
                  pl.BlockSpec((1, 128), lambda d, g, k: (0, 0)),
                  pl.BlockSpec((1, 128), lambda d, g, k: (0, 0)),
                  pl.BlockSpec((None, q, q), lambda d, g, k: (d, 0, 0)),
                  pl.BlockSpec((None, q, q), lambda d, g, k: (d, 0, 0)),
                  pl.BlockSpec((None, None, gs, SSD_STATE, xw), lambda d, g, k: (d, nc - 1 - k, g, 0, 0)),
                  pl.BlockSpec((q, gs * xw), lambda d, g, k: (ch(d, g, k), g))],
        out_specs=[pl.BlockSpec((q, gs * gw), lambda d, g, k: (d * nc + ch(d, g, k), g)),
                   pl.BlockSpec((T, 128), lambda d, g, k: (0, 0)),
                   pl.BlockSpec((1, 128), lambda d, g, k: (0, 0)),
                   pl.BlockSpec((1, 128), lambda d, g, k: (0, 0))],
        scratch_shapes=[pltpu.VMEM((gs, SSD_STATE, xw), F32)])
    return res[0], res[1], res[2], res[3], xres


def _perm_xbc(a):
    G = SSD_GROUPS
    n = a.shape[-1]
    gn = G * SSD_STATE
    di = n - 2 * gn
    lead = a.shape[:-1]
    xs = a[..., :di].reshape(lead + (G, di // G))
    bm = a[..., di:di + gn].reshape(lead + (G, SSD_STATE))
    cm = a[..., di + gn:].reshape(lead + (G, SSD_STATE))
    return jnp.concatenate([xs, bm, cm], axis=-1).reshape(lead + (n,))


def _unperm_xbc(a):
    G = SSD_GROUPS
    n = a.shape[-1]
    gn = G * SSD_STATE
    di = n - 2 * gn
    lead = a.shape[:-1]
    r = a.reshape(lead + (G, n // G))
    xw = di // G
    return jnp.concatenate([r[..., :xw].reshape(lead + (di,)), r[..., xw:xw + SSD_STATE].reshape(lead + (gn,)),
                            r[..., xw + SSD_STATE:].reshape(lead + (gn,))], axis=-1)


def _pool_consts(tm, n_ctx):
    assert n_ctx == tm and tm % GRID_W == 0
    mats, cnts = [], []
    for seq in (n_ctx, GRID_W):
        t = np.arange(tm)
        tt = t % seq
        base = t - tt
        ms, cs = [], []
        for k in POOL_WINDOWS:
            lo = np.clip(tt - k // 2, 0, seq) + base
            hi = np.clip(tt + k // 2, 0, seq) + base
            m = ((t[None, :] >= lo[:, None]) & (t[None, :] < hi[:, None])).astype(np.float32)
            ms.append(m)
            cs.append((1.0 / (hi - lo).astype(np.float32))[:, None])
        mats.append(np.stack(ms))
        cnts.append(np.stack(cs))
    m = np.stack(mats)
    return jnp.asarray(m), jnp.asarray(np.swapaxes(m, -1, -2)), jnp.asarray(np.stack(cnts).astype(np.float32))


def _prep_layer_weights(w_ada, b_ada, g_mix, w_in, conv_w, conv_b, dt_bias, a_log, d_skip, ssd_norm_w, w_ssd_out,
                        pool_w, pool_scale, w_pool_out, w_out, g_ffn, w_gate_up, w_down):
    D = w_in.shape[0]
    di = ssd_norm_w.shape[0]
    xbc = conv_w.shape[1]
    nh2 = dt_bias.size
    pw = pool_scale.shape[0]
    o = 0
    wz = w_in[:, o:o + di]; o += di
    wx = w_in[:, o:o + xbc]; o += xbc
    wdt = w_in[:, o:o + nh2]; o += nh2
    wp = w_in[:, o:o + pw]; o += pw
    wg = w_in[:, o:]
    w1 = jnp.concatenate([_perm_xbc(wx), wz, wg, wp, wdt, jnp.zeros((D, DT_PAD - nh2), w_in.dtype)], axis=1)
    pad128 = lambda v: jnp.concatenate([v.reshape(1, -1), jnp.zeros((1, 128 - v.size), F32)], axis=1)
    return dict(
        w_ada=w_ada, b_ada=b_ada.reshape(1, -1), g_mix=g_mix.reshape(1, -1), w1=w1,
        conv_w=_perm_xbc(conv_w), conv_b=_perm_xbc(conv_b.reshape(1, -1)),
        dt_bias=pad128(dt_bias), a_log=pad128(a_log),
        dskip=jnp.repeat(d_skip[0] + d_skip[1], SSD_HEADDIM).reshape(1, -1),
        ssd_norm_w=ssd_norm_w.reshape(1, -1), w_ssd_out=w_ssd_out, pool_w=pool_w,
        pool_scale=pool_scale.reshape(1, -1), w_pool_out=w_pool_out, w_out=w_out, g_ffn=g_ffn.reshape(1, -1),
        w_gate_up=w_gate_up, w_down=w_down)


def _unprep_layer_grads(g, dims):
    di, xbc, nh2, pw = dims
    dxbc, dz, dgs, dgp, dp, ddt = g["w1"]
    r = dxbc.reshape(SSD_GROUPS, xbc // SSD_GROUPS, dxbc.shape[1])
    xw = di // SSD_GROUPS
    parts = [r[:, :xw], r[:, xw:xw + SSD_STATE], r[:, xw + SSD_STATE:]]
    w_in_t = jnp.concatenate([dz] + [p.reshape(-1, dxbc.shape[1]) for p in parts] + [ddt[:nh2], dp, dgs, dgp], axis=0)
    nh = nh2 // 2
    dsk = g["dskip"].reshape(nh, SSD_HEADDIM).sum(axis=1)
    return dict(
        w_ada=g["w_ada"], b_ada=g["b_ada"].reshape(-1), g_mix=g["g_mix"].reshape(-1),
        w_in=w_in_t,
        conv_w=_unperm_xbc(g["conv_w"]), conv_b=_unperm_xbc(g["conv_b"]).reshape(-1),
        dt_bias=g["dt_bias"][0, :nh2].reshape(2, nh), a_log=g["a_log"][0, :nh2].reshape(2, nh),
        d_skip=jnp.stack([dsk, dsk]), ssd_norm_w=g["ssd_norm_w"].reshape(-1), w_ssd_out=g["w_ssd_out"],
        pool_w=g["pool_w"], pool_scale=g["pool_scale"].reshape(-1), w_pool_out=g["w_pool_out"], w_out=g["w_out"],
        g_ffn=g["g_ffn"].reshape(-1), w_gate_up=g["w_gate_up"], w_down=g["w_down"])


COND_ROWS = 16


def _split_mods(m):
    d = m.shape[1] // 6
    return [m[:2, k * d:(k + 1) * d].reshape(2, 1, d) for k in range(6)]


def _pool_args(rows, proj, col_block, width, pc, w):
    seg_const = lambda a: Arg(a, (None,) + a.shape[1:], lambda j, i, s: (s, 0, 0, 0), "const")
    pws = [Arg(w["pool_w"][k], w["pool_w"].shape[1:], lambda j, i, s: (0, 0), "acc") for k in range(w["pool_w"].shape[0])]
    return [rows.row(proj, width, col_block)] + [seg_const(a) for a in pc] + [rows.vec(w["pool_scale"])] + pws


TALL_ROW_TILE = 1088


def _tall_rows(T, ncol):
    tm = max(t for t in range(16, min(T, TALL_ROW_TILE) + 1, 16) if T % t == 0)
    return Rows(T // tm, 0, tm, ncol)


def _hosted(hosts, box, key):
    fn = (hosts or {}).get(key)
    return fn(box) if fn else None


def _layer_fwd(l, x, cond_s, w, rows, n_ctx, pc, hosts=None, box=None):
    T, D = x.shape
    nt, nct, tm = rows.nt, rows.nct, rows.tm
    n = lambda s: f"l{l}_{s}"
    crow = Rows(1, 0, COND_ROWS)
    mraw = matmul_nn(n("ada_mm"), cond_s, w["w_ada"])
    (m,) = stage_fwd(n("ada_bias"), f_bias, crow, [crow.row(mraw, mraw.shape[1]), crow.vec(w["b_ada"])],
                     [(mraw.shape[1], F32, False)])
    sh1, sc1, ga1, sh2, sc2, ga2 = _split_mods(m)

    (h1,) = stage_fwd(n("norm1"), f_norm_mod, rows,
                      [rows.row(x, D), rows.vec(w["g_mix"]), rows.segvec(sh1), rows.segvec(sc1)],
                      [(D, ACT_DTYPE, False)])
    ex = _hosted(hosts, box, "in_mm")
    proj = matmul_nn(n("in_mm"), h1, w["w1"], ex=ex)
    if ex is not None:
        proj, box["in_mm"] = proj
    xbc_w = w["conv_w"].shape[1]
    di = w["ssd_norm_w"].shape[1]
    pw = w["pool_scale"].shape[1]
    c_z, c_g, c_p, c_dt = xbc_w, xbc_w + di, xbc_w + di + 2 * pw, xbc_w + di + 3 * pw
    ex = _hosted(hosts, box, "conv")
    xbc, xres = conv_fwd(n("conv"), proj, w["conv_w"], w["conv_b"], n_ctx, xbc_w, ex)
    if ex is not None:
        box["conv"] = xres
    ex = _hosted(hosts, box, "ssd")
    y2, states, xres = ssd_fwd(n("ssd"), xbc, proj, c_dt // 128, w["dt_bias"], w["a_log"], n_ctx, ex)
    if ex is not None:
        box["ssd"] = xres

    G = SSD_GROUPS
    gw = di // G
    r8 = _tall_rows(T, G)
    gate_args = [r8.row(y2, gw, 0, True), r8.row(y2, gw, 0, True, roff=r8.nt), r8.row(xbc, gw, 0, True, stride=2),
                 r8.row(proj, gw, c_z // gw, True), r8.vec(w["dskip"], True), r8.vec(w["ssd_norm_w"], True)]
    (ynw,) = stage_fwd(n("ssd_gate"), f_ssd_gate, r8, gate_args, [(gw, ACT_DTYPE, True)])
    o_ssd = matmul_nn(n("ssd_out_mm"), ynw, w["w_ssd_out"])

    nw = len(POOL_WINDOWS)
    pg = pw // nw
    (ps,) = stage_fwd(n("pool"), f_pool_all, rows, _pool_args(rows, proj, c_p // pw, pw, pc, w), [(pw, ACT_DTYPE, False)])
    o_pool = matmul_nn(n("pool_out_mm"), ps, w["w_pool_out"])

    merge_args = [rows.row(o_ssd, D), rows.row(o_pool, D), rows.row(proj, pw, c_g // pw), rows.row(proj, pw, c_g // pw + 1)]
    (mg,) = stage_fwd(n("merge"), f_merge, rows, merge_args, [(D, ACT_DTYPE, False)])
    mo = matmul_nn(n("out_mm"), mg, w["w_out"])

    rn_args = [rows.row(x, D), rows.row(mo, D), rows.segvec(ga1), rows.vec(w["g_ffn"]), rows.segvec(sh2), rows.segvec(sc2)]
    x1, h2 = stage_fwd(n("norm2"), f_resid_norm_mod, rows, rn_args, [(D, F32, False), (D, ACT_DTYPE, False)])
    ex = _hosted(hosts, box, "gate_up_mm")
    gu = matmul_nn(n("gate_up_mm"), h2, w["w_gate_up"], ex=ex)
    if ex is not None:
        gu, box["gate_up_mm"] = gu
    fh = gu.shape[1] // 2
    (act,) = stage_fwd(n("swiglu"), f_swiglu, rows, [rows.row(gu, 2 * fh)], [(fh, ACT_DTYPE, False)])
    dn = matmul_nn(n("down_mm"), act, w["w_down"])
    res_args = [rows.row(x1, D), rows.row(dn, D), rows.segvec(ga2)]
    (x2,) = stage_fwd(n("resid2"), f_resid, rows, res_args, [(D, F32, False)])
    saved = dict(x=x, mraw=mraw, mods=(sh1, sc1, ga1, sh2, sc2, ga2), h1=h1, proj=proj, xbc=xbc, y2=y2, states=states,
                 ynw=ynw, o_ssd=o_ssd, ps=ps, o_pool=o_pool, mg=mg, mo=mo, x1=x1, h2=h2, gu=gu, act=act, dn=dn,
                 cols=(c_z, c_g, c_p, c_dt))
    return x2, saved


def f_norm_mod_keep(x, g, sh, sc):
    return f_norm_mod(x, g, sh, sc)[0], x


def _layer_bwd(l, dx2, cond_s, w, s, rows, n_ctx, pc, hosts=None, box=None):
    T, D = dx2.shape
    nt, nct, tm = rows.nt, rows.nct, rows.tm
    n = lambda t: f"l{l}_{t}_bwd"
    sh1, sc1, ga1, sh2, sc2, ga2 = s["mods"]
    c_z, c_g, c_p, c_dt = s["cols"]
    x, proj, xbc, y2, gu = s["x"], s["proj"], s["xbc"], s["y2"], s["gu"]
    g = {}
    if box is not None:
        box["g"] = g

    res_args = [rows.row(s["x1"], D), rows.row(s["dn"], D), rows.segvec(ga2)]
    res_args[0].kind = "const"
    dx1 = dx2
    ddn, dga2 = stage_bwd(n("resid2"), f_resid, rows, res_args, [rows.row(dx2, D)], [ACT_DTYPE])
    ex = _hosted(hosts, box, "down_dx")
    dact = matmul_nt(n("down_dx"), ddn, w["w_down"], ex=ex)
    if ex is not None:
        dact, box["down_dx"] = dact
    g["w_down"] = matmul_tn(n("down_dw"), s["act"], ddn)
    fh = gu.shape[1] // 2
    (dgu,) = stage_bwd(n("swiglu"), f_swiglu, rows, [rows.row(gu, 2 * fh)], [rows.row(dact, fh)], [ACT_DTYPE])
    dh2 = matmul_nt(n("gate_up_dx"), dgu, w["w_gate_up"])
    g["w_gate_up"] = matmul_tn(n("gate_up_dw"), s["h2"], dgu)

    rn_args = [rows.row(x, D), rows.row(s["mo"], D), rows.segvec(ga1), rows.vec(w["g_ffn"]), rows.segvec(sh2), rows.segvec(sc2)]
    dxr, dmo, dga1, g["g_ffn"], dsh2, dsc2 = stage_bwd(
        n("norm2"), f_resid_norm_mod, rows, rn_args, [rows.row(dx1, D), rows.row(dh2, D)], [F32, ACT_DTYPE])
    dmg = matmul_nt(n("out_dx"), dmo, w["w_out"])
    g["w_out"] = matmul_tn(n("out_dw"), s["mg"], dmo)

    pw = w["pool_scale"].shape[1]
    merge_args = [rows.row(s["o_ssd"], D), rows.row(s["o_pool"], D), rows.row(proj, pw, c_g // pw), rows.row(proj, pw, c_g // pw + 1)]
    do_ssd, do_pool, dgl_s, dgl_p = stage_bwd(n("merge"), f_merge, rows, merge_args, [rows.row(dmg, D)], [ACT_DTYPE] * 4)
    dps = matmul_nt(n("pool_out_dx"), do_pool, w["w_pool_out"])
    g["w_pool_out"] = matmul_tn(n("pool_out_dw"), s["ps"], do_pool)

    nw = len(POOL_WINDOWS)
    pg = pw // nw
    du_pool, g["pool_scale"], *dpw = stage_bwd(n("pool"), f_pool_all, rows, _pool_args(rows, proj, c_p // pw, pw, pc, w),
                                               [rows.row(dps, pw)], [ACT_DTYPE])
    g["pool_w"] = jnp.stack(dpw)

    dynw = matmul_nt(n("ssd_out_dx"), do_ssd, w["w_ssd_out"])
    g["w_ssd_out"] = matmul_tn(n("ssd_out_dw"), s["ynw"], do_ssd)
    G = SSD_GROUPS
    di = w["ssd_norm_w"].shape[1]
    gw = di // G
    r8 = _tall_rows(T, G)
    gate_args = [r8.row(y2, gw, 0, True), r8.row(y2, gw, 0, True, roff=r8.nt), r8.row(xbc, gw, 0, True, stride=2),
                 r8.row(proj, gw, c_z // gw, True), r8.vec(w["dskip"], True), r8.vec(w["ssd_norm_w"], True)]
    gate_args[1].kind = "const"
    ex = _hosted(hosts, box, "ssd_gate")
    res = stage_bwd(n("ssd_gate"), f_ssd_gate, r8, gate_args, [r8.row(dynw, gw, 0, True)], [F32, F32, ACT_DTYPE], ex)
    if ex is not None:
        res, box["ssd_gate"] = res
    dy, dxs_skip, dz, g["dskip"], g["ssd_norm_w"] = res

    ex = _hosted(hosts, box, "ssd")
    dxbc2, ddt, g["dt_bias"], g["a_log"], xres = ssd_bwd(n("ssd"), xbc, proj, c_dt // 128, w["dt_bias"], w["a_log"],
                                                         s["states"], dy, n_ctx, ex)
    if ex is not None:
        box["ssd"] = xres
    xbc_w = xbc.shape[1]
    ex = _hosted(hosts, box, "conv")
    dxbc_raw, g["conv_w"], g["conv_b"], xres = conv_bwd(n("conv"), proj, w["conv_w"], w["conv_b"], dxbc2, dxs_skip,
                                                         n_ctx, xbc_w, ex)
    if ex is not None:
        box["conv"] = xres
    pieces = [dxbc_raw, dz, dgl_s, dgl_p, du_pool, ddt]
    offsets = [0, c_z, c_g, c_g + pw, c_p, c_dt]
    ex = _hosted(hosts, box, "in_dx")
    dh1 = matmul_nt(n("in_dx"), pieces, w["w1"], ex=ex, offsets=offsets)
    if ex is not None:
        dh1, box["in_dx"] = dh1
    ex = _hosted(hosts, box, "in_dw")
    first = matmul_tn(n("in_dw0"), pieces[0], s["h1"], ex=ex)
    if ex is not None:
        first, box["in_dw"] = first
    g["w1"] = [first] + [matmul_tn(n(f"in_dw{k}"), p, s["h1"]) for k, p in enumerate(pieces) if k]

    n1_args = [rows.row(x, D), rows.vec(w["g_mix"]), rows.segvec(sh1), rows.segvec(sc1)]
    dx, g["g_mix"], dsh1, dsc1 = stage_bwd(n("norm1"), f_norm_mod_keep, rows, n1_args,
                                           [rows.row(dh1, D), rows.row(dxr, D)], [F32])

    dm = jnp.concatenate([v.reshape(2, D) for v in (dsh1, dsc1, dga1, dsh2, dsc2, dga2)], axis=1)
    dm = jnp.concatenate([dm, jnp.zeros((COND_ROWS - 2, dm.shape[1]), F32)], axis=0)
    crow = Rows(1, 0, COND_ROWS)
    dmraw, g["b_ada"] = stage_bwd(n("ada_bias"), f_bias, crow, [crow.row(s["mraw"], dm.shape[1]), crow.vec(w["b_ada"])],
                                  [crow.row(dm, dm.shape[1])], [ACT_DTYPE])
    dcs = matmul_nt(n("ada_dx"), dmraw, w["w_ada"])
    g["w_ada"] = matmul_tn(n("ada_dw"), cond_s, dmraw)
    return dx, dcs, g


def local_step(x, ctx, c, c_ctx, target, layer_w_fn, n_layers, g_final, fwd_hosts=None, bwd_hosts=None):
    L, D = x.shape
    n_ctx = ctx.shape[0]
    tm = ROW_TILE
    T = L + n_ctx
    rows = Rows(T // tm, n_ctx // tm, tm)
    pc = _pool_consts(tm, n_ctx)
    xa = jnp.concatenate([ctx, x], axis=0)
    cond = jnp.concatenate([c_ctx.reshape(1, D), c.reshape(1, D), jnp.zeros((COND_ROWS - 2, D), F32)], axis=0)
    crow = Rows(1, 0, COND_ROWS)
    (cond_s,) = stage_fwd("cond_silu", f_silu, crow, [crow.row(cond, D)], [(D, ACT_DTYPE, False)])

    saved, layer_w = [], []
    for l in range(n_layers):
        layer_w.append(layer_w_fn(l))
        box = {}
        xa, s = _layer_fwd(l, xa, cond_s, layer_w[l], rows, n_ctx, pc, fwd_hosts(l, box) if fwd_hosts else None, box)
        saved.append(s)

    rl = Rows(L // tm, 0, tm)
    gf = g_final.reshape(1, D)
    tgt = rl.row(target, D)
    tgt.kind = "const"
    loss_args = [rl.row(xa, D, roff=n_ctx // tm), tgt, rl.vec(gf)]
    ones = jnp.ones((L, 1), F32)
    dx_lat, dgf, loss_rows = stage_bwd("loss", f_loss, rl, loss_args, [rl.row(ones, 1)], [F32], primal=[(1, F32)])
    loss = jnp.sum(loss_rows)
    dx = jnp.concatenate([jnp.zeros((n_ctx, D), F32), dx_lat], axis=0)

    grads = [None] * n_layers
    dcs = jnp.zeros((COND_ROWS, D), F32)
    for l in reversed(range(n_layers)):
        box = {}
        hosts = bwd_hosts(l, grads, box) if bwd_hosts else None
        dx, dcs_l, grads[l] = _layer_bwd(l, dx, cond_s, layer_w[l], saved[l], rows, n_ctx, pc, hosts, box)
        dcs = dcs + dcs_l
    (dcond,) = stage_bwd("cond_silu_bwd", f_silu, crow, [crow.row(cond, D)], [crow.row(dcs, D)], [F32])
    return loss, dx[n_ctx:], grads, dcond[0], dgf


def gather_chips(halves, conv=None):
    n = len(halves)
    ops = list(halves) + ([conv] if conv is not None else [])

    def copies(ins, outs, pos):
        c, me = pos[2], _chip_index(pos)
        pairs = [(s.at[c], o.at[me, c]) for s, o in zip(ins[:n], outs[:n])]
        pairs += [(s, o.at[me]) for s, o in zip(ins[n:], outs[n:])]
        return pairs, [(s, d, _flip(pos, rel)) for rel in PLANE for s, d in pairs]

    shapes = [jax.ShapeDtypeStruct((4,) + s.shape, s.dtype) for s in ops]
    return Exchange(copies, 3 * len(ops), len(ops), ops, shapes)


def gather_pair(gathered):
    n = len(gathered)

    def copies(ins, outs, pos):
        c = pos[2]
        return [], [(s.at[b, c], o.at[b, c], _flip(pos, PAIR[0])) for s, o in zip(ins, outs) for b in range(4)]

    shapes = [jax.ShapeDtypeStruct(g.shape, g.dtype) for g in gathered]
    return Exchange(copies, 4 * n, 0, gathered, shapes, aliases={k: k for k in range(n)})


def swap_halves(grads):
    n = len(grads)

    def copies(ins, outs, pos):
        c = pos[2]
        return [], [(g.at[b, 1 - c], o.at[b], _flip(pos, PAIR[0])) for g, o in zip(ins, outs) for b in range(4)]

    shapes = [jax.ShapeDtypeStruct((g.shape[0],) + g.shape[2:], g.dtype) for g in grads]
    return Exchange(copies, 4 * n, 0, grads, shapes)


def scatter_chips(sums):
    n = len(sums)

    def copies(ins, outs, pos):
        me = _chip_index(pos)
        local = [(p.at[me], o.at[me]) for p, o in zip(ins, outs)]
        remote = []
        for rel in PLANE:
            peer = _flip(pos, rel)
            remote += [(p.at[_chip_index(peer)], o.at[me], peer) for p, o in zip(ins, outs)]
        return local, remote

    shapes = [jax.ShapeDtypeStruct(p.shape, p.dtype) for p in sums]
    return Exchange(copies, 3 * n, n, sums, shapes)


def share_halves(finals):
    n = len(finals)

    def copies(ins, outs, pos):
        c = pos[2]
        return [], [(f.at[c], o.at[c], _flip(pos, PAIR[0])) for f, o in zip(ins, outs)]

    shapes = [jax.ShapeDtypeStruct(f.shape, f.dtype) for f in finals]
    return Exchange(copies, n, 0, finals, shapes, aliases={k: k for k in range(n)})


def gather_everyone(vec):
    def copies(ins, outs, pos):
        me = _device_index(pos)
        (v,), (o,) = ins, outs
        return [(v, o.at[me])], [(v, o.at[me], _flip(pos, rel)) for rel in EVERYONE]

    return Exchange(copies, len(EVERYONE), 1, [vec], [jax.ShapeDtypeStruct((8,) + vec.shape, vec.dtype)])


def _row_tile(rows, cols, n_bufs, mult=8):
    cap = VMEM_LIMIT_BYTES // 2 // (2 * n_bufs * cols * 4)
    for t in range(min(rows, cap) // mult * mult, 0, -mult):
        if rows % t == 0:
            return t
    return rows


def _adamw_update(w, g, m, v):
    nm = ADAM_B1 * m + (1.0 - ADAM_B1) * g
    nv = ADAM_B2 * v + (1.0 - ADAM_B2) * jnp.square(g)
    m_hat = nm / (1.0 - ADAM_B1 ** ADAM_STEP)
    v_hat = nv / (1.0 - ADAM_B2 ** ADAM_STEP)
    return -ADAM_LR * (m_hat / (jnp.sqrt(v_hat) + ADAM_EPS) + ADAM_WD * w), nm, nv


def adamw_small(name, ws, gs, ms, vs):
    n = len(ws)

    def body(*refs):
        ins, outs = refs[:4 * n], refs[4 * n:]
        for k in range(n):
            d, nm, nv = _adamw_update(ins[k][...], ins[n + k][...], ins[2 * n + k][...], ins[3 * n + k][...])
            outs[k][...] = d
            outs[n + k][...] = nm
            outs[2 * n + k][...] = nv

    shapes = [jax.ShapeDtypeStruct(a.shape, F32) for a in ws]
    vmem = pl.BlockSpec(memory_space=pltpu.VMEM)
    res = _pcall(body, name=name, out_shape=shapes * 3, in_specs=[vmem] * (4 * n), out_specs=[vmem] * (3 * n),
                 compiler_params=pltpu.CompilerParams(vmem_limit_bytes=VMEM_LIMIT_BYTES))(*ws, *gs, *ms, *vs)
    return res[:n], res[n:2 * n], res[2 * n:]


WIRE_DTYPE = jnp.bfloat16


def add_own_half(name, grads, recv, c):
    nb, _, R, C = grads.shape
    tr = _row_tile(R, C, 3, mult=16)

    def body(c_ref, g_ref, r_ref, o_ref):
        o_ref[...] = (g_ref[...] + r_ref[...]).astype(o_ref.dtype)

    spec = pl.BlockSpec((None, tr, C), lambda b, i, c_ref: (b, i, 0))
    return _pcall(
        body, name=name, out_shape=jax.ShapeDtypeStruct(recv.shape, WIRE_DTYPE),
        grid_spec=pltpu.PrefetchScalarGridSpec(
            num_scalar_prefetch=1, grid=(nb, R // tr),
            in_specs=[pl.BlockSpec((None, None, tr, C), lambda b, i, c_ref: (b, c_ref[0], i, 0)), spec],
            out_specs=spec),
        compiler_params=_params("parallel", "parallel"),
    )(c, grads, recv)


def sum_slots(name, a, c=None):
    n, R, C = a.shape
    tr = _row_tile(R, C, n + 1, mult=16 if a.dtype.itemsize == 2 else 8)

    def body(*refs):
        a_ref, o_ref = refs[-2:]
        acc = a_ref[0].astype(F32)
        for k in range(1, n):
            acc = acc + a_ref[k].astype(F32)
        o_ref[...] = acc

    if c is None:
        return _pcall(
            body, name=name, out_shape=jax.ShapeDtypeStruct((R, C), F32), grid=(R // tr,),
            in_specs=[pl.BlockSpec((n, tr, C), lambda i: (0, i, 0))], out_specs=pl.BlockSpec((tr, C), lambda i: (i, 0)),
            compiler_params=_params("parallel"),
        )(a)
    return _pcall(
        body, name=name, out_shape=jax.ShapeDtypeStruct((2, R, C), F32),
        grid_spec=pltpu.PrefetchScalarGridSpec(
            num_scalar_prefetch=1, grid=(R // tr,),
            in_specs=[pl.BlockSpec((n, tr, C), lambda i, c_ref: (0, i, 0))],
            out_specs=pl.BlockSpec((None, tr, C), lambda i, c_ref: (c_ref[0], i, 0))),
        compiler_params=_params("parallel"),
    )(c, a)


def adamw(name, w, g_layers, m, v):
    nl, R, C = w.shape
    assert len(g_layers) == nl
    tr = _row_tile(R, C, 8 + nl)
    nr = R // tr

    def body(*refs):
        w_ref, m_ref, v_ref = refs[:3]
        g_refs = refs[3:3 + nl]
        go_ref, d_ref, nm_ref, nv_ref = refs[3 + nl:]
        l = pl.program_id(0)
        gr = g_refs[0][...]
        for k in range(1, nl):
            gr = jnp.where(l == k, g_refs[k][...], gr)
        d_ref[...], nm_ref[...], nv_ref[...] = _adamw_update(w_ref[...], gr, m_ref[...], v_ref[...])
        go_ref[...] = gr

    spec = pl.BlockSpec((None, tr, C), lambda l, i: (l, i, 0))
    g_specs = [pl.BlockSpec((tr, C), (lambda l, i, k=k: (jnp.where(l == k, i, jnp.where(l < k, 0, nr - 1)), 0)))
               for k in range(nl)]
    return _pcall(
        body, name=name, out_shape=[jax.ShapeDtypeStruct((nl, R, C), F32)] * 4, grid=(nl, nr),
        in_specs=[spec] * 3 + g_specs, out_specs=[spec] * 4, compiler_params=_params("arbitrary", "arbitrary"),
    )(w, m, v, *g_layers)


BIG = ("w_ada", "w_in", "w_ssd_out", "pool_w", "w_pool_out", "w_out", "w_gate_up", "w_down")
COL_SHARDED = ("w_ada", "w_in", "w_gate_up")
GRAD_TRANSPOSED = ("w_in",)
FIRST_USED = ("w_ada", "w_in")
LATER_USED = tuple(k for k in BIG if k not in FIRST_USED)
READY_LAST = FIRST_USED
READY_EARLY = LATER_USED
SMALL = ("c_ctx", "b_ada", "g_mix", "conv_w", "conv_b", "dt_bias", "a_log", "d_skip", "ssd_norm_w", "pool_scale",
         "g_ffn", "g_final")
WEIGHTS = ("c_ctx", "w_ada", "b_ada", "g_mix", "w_in", "conv_w", "conv_b", "dt_bias", "a_log", "d_skip", "ssd_norm_w",
           "w_ssd_out", "pool_w", "pool_scale", "w_pool_out", "w_out", "g_ffn", "w_gate_up", "w_down", "g_final")
LAYER_KEYS = ("w_ada", "b_ada", "g_mix", "w_in", "conv_w", "conv_b", "dt_bias", "a_log", "d_skip", "ssd_norm_w",
              "w_ssd_out", "pool_w", "pool_scale", "w_pool_out", "w_out", "g_ffn", "w_gate_up", "w_down")


def _shard2d(name, a):
    if name == "pool_w":
        return a.reshape(a.shape[0], a.shape[1] * a.shape[2], a.shape[3])
    return a


def _full_from_blocks(name, a):
    nb, R, C = a.shape
    if name in COL_SHARDED:
        return jnp.transpose(a, (1, 0, 2)).reshape(R, nb * C)
    if name == "pool_w":
        nw = len(POOL_WINDOWS)
        return jnp.transpose(a.reshape(nb, nw, R // nw, C), (1, 0, 2, 3)).reshape(nw, nb * R // nw, C)
    return a.reshape(nb * R, C)


def _blocks_from_full(name, g):
    nb = 4
    if name in COL_SHARDED and name not in GRAD_TRANSPOSED:
        K, N = g.shape
        return jnp.transpose(g.reshape(K, nb, N // nb), (1, 0, 2))
    if name == "pool_w":
        nw, r, C = g.shape
        return jnp.transpose(g.reshape(nw, nb, r // nb, C), (1, 0, 2, 3)).reshape(nb, nw * r // nb, C)
    return g.reshape(nb, g.shape[0] // nb, g.shape[1])


def _pack(arrs, rows):
    flat = jnp.concatenate([a.reshape(-1).astype(F32) for a in arrs])
    return jnp.concatenate([flat, jnp.zeros((rows * 128 - flat.size,), F32)]).reshape(rows, 128)


def _unpack(vec, shapes):
    flat = vec.reshape(-1)
    out, o = [], 0
    for s in shapes:
        n = int(np.prod(s))
        out.append(flat[o:o + n].reshape(s))
        o += n
    return out


def _rows_for(shapes):
    n = sum(int(np.prod(s)) for s in shapes)
    return -(-n // (8 * 128)) * 8


def kernel(x, c, ctx, c_ctx, w_ada, b_ada, g_mix, w_in, conv_w, conv_b, dt_bias, a_log, d_skip, ssd_norm_w, w_ssd_out, pool_w, pool_scale, w_pool_out, w_out, g_ffn, w_gate_up, w_down, g_final, loss_target, m_c_ctx, m_w_ada, m_b_ada, m_g_mix, m_w_in, m_conv_w, m_conv_b, m_dt_bias, m_a_log, m_d_skip, m_ssd_norm_w, m_w_ssd_out, m_pool_w, m_pool_scale, m_w_pool_out, m_w_out, m_g_ffn, m_w_gate_up, m_w_down, m_g_final, v_c_ctx, v_w_ada, v_b_ada, v_g_mix, v_w_in, v_conv_w, v_conv_b, v_dt_bias, v_a_log, v_d_skip, v_ssd_norm_w, v_w_ssd_out, v_pool_w, v_pool_scale, v_w_pool_out, v_w_out, v_g_ffn, v_w_gate_up, v_w_down, v_g_final):
    w = dict(c_ctx=c_ctx, w_ada=w_ada, b_ada=b_ada, g_mix=g_mix, w_in=w_in, conv_w=conv_w, conv_b=conv_b, dt_bias=dt_bias,
             a_log=a_log, d_skip=d_skip, ssd_norm_w=ssd_norm_w, w_ssd_out=w_ssd_out, pool_w=pool_w, pool_scale=pool_scale,
             w_pool_out=w_pool_out, w_out=w_out, g_ffn=g_ffn, w_gate_up=w_gate_up, w_down=w_down, g_final=g_final)
    m = dict(c_ctx=m_c_ctx, w_ada=m_w_ada, b_ada=m_b_ada, g_mix=m_g_mix, w_in=m_w_in, conv_w=m_conv_w, conv_b=m_conv_b,
             dt_bias=m_dt_bias, a_log=m_a_log, d_skip=m_d_skip, ssd_norm_w=m_ssd_norm_w, w_ssd_out=m_w_ssd_out,
             pool_w=m_pool_w, pool_scale=m_pool_scale, w_pool_out=m_w_pool_out, w_out=m_w_out, g_ffn=m_g_ffn,
             w_gate_up=m_w_gate_up, w_down=m_w_down, g_final=m_g_final)
    v = dict(c_ctx=v_c_ctx, w_ada=v_w_ada, b_ada=v_b_ada, g_mix=v_g_mix, w_in=v_w_in, conv_w=v_conv_w, conv_b=v_conv_b,
             dt_bias=v_dt_bias, a_log=v_a_log, d_skip=v_d_skip, ssd_norm_w=v_ssd_norm_w, w_ssd_out=v_w_ssd_out,
             pool_w=v_pool_w, pool_scale=v_pool_scale, w_pool_out=v_w_pool_out, w_out=v_w_out, g_ffn=v_g_ffn,
             w_gate_up=v_w_gate_up, w_down=v_w_down, g_final=v_g_final)
    assert x.shape[0] == 1, "one example per device"
    pos = _position()
    core = pos[2].astype(jnp.int32).reshape(1)
    n_layers = w_in.shape[0]
    assert n_layers == 2
    dims = (ssd_norm_w.shape[1], conv_w.shape[2] * 4, dt_bias[0].size, pool_scale.shape[1])
    shard = {k: _shard2d(k, w[k]) for k in BIG}

    def halves(a):
        return a.reshape(a.shape[:-2] + (2, a.shape[-2] // 2, a.shape[-1]))

    def whole(a):
        return a.reshape(a.shape[:-3] + (2 * a.shape[-2], a.shape[-1]))

    def wire_shards(l, names):
        return [halves(shard[k][l].astype(MXU_DTYPE)) for k in names]

    def full_weights(names, gathered):
        return {k: _full_from_blocks(k, whole(a)) for k, a in zip(names, gathered)}

    first = comm_call("gather0_chips", gather_chips(wire_shards(0, FIRST_USED), conv=conv_w))
    got0 = full_weights(FIRST_USED, comm_call("gather0_pair", gather_pair(first[:-1])))
    conv_all = first[-1]
    conv_full = [jnp.transpose(conv_all[:, l], (1, 0, 2)).reshape(conv_all.shape[2], -1) for l in range(n_layers)]

    boxes = {}

    def layer_w_fn(l):
        if l == 0:
            full = dict(got0)
            late = {k: None for k in LATER_USED}
        else:
            full = full_weights(BIG, boxes[("fwd", 0)]["gate_up_mm"])
            late = {}
        full["conv_w"] = conv_full[l]
        lw = LazyDict(_prep_layer_weights(*[full[k] if k in full else (None if k in late else w[k][l]) for k in LAYER_KEYS]))
        for i, k in enumerate(late):
            lw[k] = (lambda i=i, k=k: _full_from_blocks(k, whole(boxes[("fwd", 0)]["conv"][i])))
        return lw

    def fwd_hosts(l, box):
        boxes[("fwd", l)] = box
        if l != 0:
            return None
        return {"in_mm": lambda box: gather_chips(wire_shards(0, LATER_USED)), "conv": lambda box: gather_pair(box["in_mm"]),
                "ssd": lambda box: gather_chips(wire_shards(1, BIG)), "gate_up_mm": lambda box: gather_pair(box["ssd"])}

    def blocks(gl, names):
        return [halves(_blocks_from_full(k, gl[k])) for k in names]

    def pair_sums(tag, names, G, recv):
        return [add_own_half(f"pair_sum{tag}_{k}", g, r, core) for k, g, r in zip(names, G, recv)]

    def chip_sums(tag, names, parts):
        return [sum_slots(f"chip_sum{tag}_{k}", p, core) for k, p in zip(names, parts)]

    def reduce_now(tag, gl, names):
        G = blocks(gl, names)
        pair = pair_sums(tag, names, G, comm_call(f"swap{tag}", swap_halves(G)))
        fin = chip_sums(tag, names, comm_call(f"scatter{tag}", scatter_chips(pair)))
        return [whole(a) for a in comm_call(f"share{tag}", share_halves(fin))]

    small_layers = {}
    n_big = len(BIG)

    def bwd_hosts(l, grads, box):
        boxes[("bwd", l)] = box
        if l != 0:
            return None
        gl1 = _unprep_layer_grads(grads[1], dims)
        small_layers[1] = gl1
        G1 = blocks(gl1, BIG)
        early = {}

        def gate_host(box):
            early["G"] = blocks(box["g"], READY_EARLY)
            return swap_halves(early["G"])

        def scan_host(box):
            return combine(scatter_chips(pair_sums("1", BIG, G1, box["down_dx"])),
                           scatter_chips(pair_sums("0e", READY_EARLY, early["G"], box["ssd_gate"])))

        def conv_host(box):
            return combine(share_halves(chip_sums("1", BIG, box["ssd"][:n_big])),
                           share_halves(chip_sums("0e", READY_EARLY, box["ssd"][n_big:])))

        return {"down_dx": lambda box: swap_halves(G1), "ssd_gate": gate_host, "ssd": scan_host, "in_dx": conv_host}

    loss, grad_x, grads, d_c_ctx, d_g_final = local_step(
        x[0], ctx[0], c[0], c_ctx, loss_target[0], layer_w_fn, n_layers, g_final, fwd_hosts, bwd_hosts)
    shared =[whole(a) for a in boxes[("bwd", 0)]["in_dx"]]
    reduced1 = shared[:n_big]
    gl0 = _unprep_layer_grads(grads[0], dims)
    small_layers[0] = gl0
    red0 = dict(zip(READY_EARLY, shared[n_big:]))
    red0.update(zip(READY_LAST, reduce_now("0", gl0, READY_LAST)))
    reduced0 = [red0[k] for k in BIG]

    small_full = dict(c_ctx=d_c_ctx, g_final=d_g_final.reshape(-1))
    for k in SMALL:
        if k not in small_full:
            small_full[k] = jnp.stack([small_layers[l][k] for l in range(n_layers)])
    shapes = [small_full[k].shape for k in SMALL] + [(1,)]
    packed = _pack([small_full[k] for k in SMALL] + [loss.reshape(1)], _rows_for(shapes))
    total = sum_slots("small_sum", comm_call("gather_small", gather_everyone(packed))[0])
    *small_vals, loss = _unpack(total, shapes)
    loss = loss.reshape(())
    small_g = dict(zip(SMALL, small_vals))
    cw = conv_w.shape[2]
    small_g["conv_w"] = lax.dynamic_slice_in_dim(small_g["conv_w"], _chip_index(pos) * cw, cw, axis=2)

    grad, delta, new_m, new_v = {}, {}, {}, {}
    for k, g0, g1 in zip(BIG, reduced0, reduced1):
        shp = w[k].shape
        if k in GRAD_TRANSPOSED:
            flat = lambda a: jnp.swapaxes(a, 1, 2)
            back = lambda a: jnp.swapaxes(a, 1, 2)
        else:
            flat = lambda a: _shard2d(k, a)
            back = lambda a: a.reshape(shp)
        outs = adamw(f"adamw_{k}", flat(w[k]), [g0, g1], flat(m[k]), flat(v[k]))
        grad[k], delta[k], new_m[k], new_v[k] = [back(a) for a in outs]
    flat2 = lambda d: [d[k].reshape(-1, d[k].shape[-1]) for k in SMALL]
    d_, m_, v_ = adamw_small("adamw_small", flat2(w), flat2(small_g), flat2(m), flat2(v))
    for k, dd, mm, vv in zip(SMALL, d_, m_, v_):
        shp = w[k].shape
        grad[k], delta[k], new_m[k], new_v[k] = small_g[k], dd.reshape(shp), mm.reshape(shp), vv.reshape(shp)

    return (loss, grad_x[None], *[grad[k] for k in WEIGHTS], *[delta[k] for k in WEIGHTS],
            *[new_m[k] for k in WEIGHTS], *[new_v[k] for k in WEIGHTS])
```

```python
import functools

import jax
import jax.numpy as jnp
import numpy as np
from jax import lax
from jax.experimental import pallas as pl
from jax.experimental.pallas import tpu as pltpu

F32 = jnp.float32
MXU_DTYPE = jnp.bfloat16
ACT_DTYPE = jnp.bfloat16
VMEM_LIMIT_BYTES = 48 * 1024 * 1024
EPS = 1e-6
NEG = -1e30

SSD_HEADDIM = 64
SSD_GROUPS = 8
SSD_STATE = 128
SSD_CHUNK = 128
SSD_GROUPS_PER_STEP = 8
SSD_CONV = 5
GRID_W = 64
POOL_WINDOWS = (2, 4, 8, 16)
ROW_TILE = 256
DT_PAD = 512

ADAM_LR = 0.001
ADAM_B1 = 0.9
ADAM_B2 = 0.999
ADAM_EPS = 1e-08
ADAM_WD = 0.01
ADAM_STEP = 10

MESH = pl.DeviceIdType.MESH


def _pcall(body, **kw):
    return pl.pallas_call(body, **kw)


def _params(*sem):
    return pltpu.CompilerParams(dimension_semantics=tuple(sem), vmem_limit_bytes=VMEM_LIMIT_BYTES)


def _pick_tile(n, cands):
    for t in cands:
        if n % t == 0:
            return t
    return n


PLANE = ((1, 0, 0), (0, 1, 0), (1, 1, 0))
PAIR = ((0, 0, 1),)
EVERYONE = tuple((a, b, d) for a in (0, 1) for b in (0, 1) for d in (0, 1) if a + b + d)
HBM = pl.BlockSpec(memory_space=pl.ANY)


def _position():
    return lax.axis_index("x"), lax.axis_index("y"), lax.axis_index("c")


def _flip(pos, rel):
    return tuple(1 - p if r else p for p, r in zip(pos, rel))


def _chip_index(pos):
    return 2 * pos[0] + pos[1]


def _device_index(pos):
    return 4 * pos[0] + 2 * pos[1] + pos[2]


class Exchange:
    def __init__(self, copies, n_remote, n_local, operands, out_shapes, aliases=None):
        self.copies, self.n_remote, self.n_local = copies, n_remote, n_local
        self.operands, self.out_shapes, self.aliases = list(operands), list(out_shapes), dict(aliases or {})

    def scratch(self):
        return [pltpu.SemaphoreType.DMA((max(self.n_remote, 1),)), pltpu.SemaphoreType.DMA((max(self.n_remote, 1),)),
                pltpu.SemaphoreType.DMA((max(self.n_local, 1),))]

    def descriptors(self, ins, outs, sems):
        send_sems, recv_sems, local_sems = sems
        local, remote = self.copies(ins, outs, _position())
        assert len(local) == self.n_local and len(remote) == self.n_remote
        cps = [pltpu.make_async_copy(src, dst, local_sems.at[k]) for k, (src, dst) in enumerate(local)]
        cps += [pltpu.make_async_remote_copy(src_ref=src, dst_ref=dst, send_sem=send_sems.at[k], recv_sem=recv_sems.at[k],
                                             device_id=peer, device_id_type=MESH) for k, (src, dst, peer) in enumerate(remote)]
        return cps


def combine(a, b):
    na, nao = len(a.operands), len(a.out_shapes)

    def copies(ins, outs, pos):
        la, ra = a.copies(ins[:na], outs[:nao], pos)
        lb, rb = b.copies(ins[na:], outs[nao:], pos)
        return la + lb, ra + rb

    aliases = dict(a.aliases)
    aliases.update({na + k: nao + v for k, v in b.aliases.items()})
    return Exchange(copies, a.n_remote + b.n_remote, a.n_local + b.n_local, a.operands + b.operands,
                    a.out_shapes + b.out_shapes, aliases)


class LazyDict(dict):
    def __getitem__(self, key):
        v = dict.__getitem__(self, key)
        if callable(v):
            v = v()
            dict.__setitem__(self, key, v)
        return v


def comm_call(name, ex):
    n_in, n_out = len(ex.operands), len(ex.out_shapes)

    def body(*refs):
        cps = ex.descriptors(refs[:n_in], refs[n_in:n_in + n_out], refs[n_in + n_out:])
        for cp in cps:
            cp.start()
        for cp in cps:
            cp.wait()

    return _pcall(
        body, name=name, out_shape=ex.out_shapes, in_specs=[HBM] * n_in, out_specs=[HBM] * n_out,
        scratch_shapes=ex.scratch(), input_output_aliases=ex.aliases,
        compiler_params=pltpu.CompilerParams(has_side_effects=True),
    )(*ex.operands)


def hosted_call(body, ex, operands, *, name, out_shape, grid, in_specs, out_specs, scratch_shapes=()):
    n_in, n_out, n_scr = len(operands), len(out_shape), len(scratch_shapes)
    sem = ("arbitrary",) * len(grid)
    if ex is None:
        res = _pcall(body, name=name, out_shape=list(out_shape), grid=grid, in_specs=list(in_specs),
                     out_specs=list(out_specs), scratch_shapes=list(scratch_shapes), compiler_params=_params(*sem))(*operands)
        return res, []
    x_in, x_out = len(ex.operands), len(ex.out_shapes)

    def wrapped(*refs):
        o = 0
        ins = refs[o:o + n_in]; o += n_in
        xins = refs[o:o + x_in]; o += x_in
        outs = refs[o:o + n_out]; o += n_out
        xouts = refs[o:o + x_out]; o += x_out
        scr = refs[o:o + n_scr]; o += n_scr
        sems = refs[o:]
        first = last = None
        for a, n in enumerate(grid):
            i = pl.program_id(a)
            first = (i == 0) if first is None else first & (i == 0)
            last = (i == n - 1) if last is None else last & (i == n - 1)

        @pl.when(first)
        def _():
            for cp in ex.descriptors(xins, xouts, sems):
                cp.start()

        body(*ins, *outs, *scr)

        @pl.when(last)
        def _():
            for cp in ex.descriptors(xins, xouts, sems):
                cp.wait()

    aliases = {n_in + k: n_out + v for k, v in ex.aliases.items()}
    res = _pcall(
        wrapped, name=name, out_shape=list(out_shape) + ex.out_shapes, grid=grid,
        in_specs=list(in_specs) + [HBM] * x_in, out_specs=list(out_specs) + [HBM] * x_out,
        scratch_shapes=list(scratch_shapes) + ex.scratch(), input_output_aliases=aliases,
        compiler_params=pltpu.CompilerParams(dimension_semantics=sem, vmem_limit_bytes=VMEM_LIMIT_BYTES,
                                             has_side_effects=True),
    )(*operands, *ex.operands)
    return res[:n_out], res[n_out:]


def _dot(a, b, dims):
    return lax.dot_general(a.astype(MXU_DTYPE), b.astype(MXU_DTYPE), (dims, ((), ())), preferred_element_type=F32)


_NN = ((1,), (0,))
_NT = ((1,), (1,))
_TN = ((0,), (0,))


@jax.custom_vjp
def _mm(a, b):
    return _dot(a, b, _NN)


def _mm_fwd(a, b):
    return _mm(a, b), (a, b)


def _mm_bwd(res, g):
    a, b = res
    return _dot(g, b, _NT).astype(a.dtype), _dot(a, g, _TN).astype(b.dtype)


_mm.defvjp(_mm_fwd, _mm_bwd)


@jax.custom_vjp
def _mm_nt(a, b):
    return _dot(a, b, _NT)


def _mm_nt_fwd(a, b):
    return _mm_nt(a, b), (a, b)


def _mm_nt_bwd(res, g):
    a, b = res
    return _dot(g, b, _NN).astype(a.dtype), _dot(g, a, _TN).astype(b.dtype)


_mm_nt.defvjp(_mm_nt_fwd, _mm_nt_bwd)


@jax.custom_vjp
def _mm_tn(a, b):
    return _dot(a, b, _TN)


def _mm_tn_fwd(a, b):
    return _mm_tn(a, b), (a, b)


def _mm_tn_bwd(res, g):
    a, b = res
    return _dot(b, g, _NT).astype(a.dtype), _dot(a, g, _NN).astype(b.dtype)


_mm_tn.defvjp(_mm_tn_fwd, _mm_tn_bwd)


def _dot_exact(m01, v):
    m = m01.astype(jnp.bfloat16)
    hi = v.astype(jnp.bfloat16)
    r1 = v - hi.astype(F32)
    mid = r1.astype(jnp.bfloat16)
    lo = (r1 - mid.astype(F32)).astype(jnp.bfloat16)
    out = jnp.dot(m, hi, preferred_element_type=F32)
    out = out + jnp.dot(m, mid, preferred_element_type=F32)
    return out + jnp.dot(m, lo, preferred_element_type=F32)


@jax.custom_vjp
def _lin01(m, mt, v):
    return _dot_exact(m, v)


def _lin01_fwd(m, mt, v):
    return _dot_exact(m, v), (m, mt)


def _lin01_bwd(res, g):
    m, mt = res
    return jnp.zeros_like(m), jnp.zeros_like(mt), _dot_exact(mt, g)


_lin01.defvjp(_lin01_fwd, _lin01_bwd)


MATMUL_VMEM_BUDGET = VMEM_LIMIT_BYTES * 3 // 4


def _mm_tiles(m, n, k_bytes_a, k_bytes_b, out_bytes, cands_m, cands_n):
    best = None
    for tm in cands_m:
        if m % tm:
            continue
        for tn in cands_n:
            if n % tn:
                continue
            need = 2 * (tm * k_bytes_a + tn * k_bytes_b + tm * tn * out_bytes)
            if need <= MATMUL_VMEM_BUDGET and (best is None or tm * tn > best[0] * best[1]):
                best = (tm, tn)
    assert best is not None, (m, n)
    return best


_ROW_CANDS = (4352, 2176, 1088, 768, 544, 512, 272, 256, 128, 16)
_COL_CANDS = (2816, 2048, 1408, 1024, 512, 256, 128)


def _one(res, xres, ex):
    return res[0] if ex is None else (res[0], xres)


def matmul_nn(name, a, b, out_dtype=F32, ex=None, col0=0, ncols=None):
    M, K = a.shape
    N = b.shape[1] - col0 if ncols is None else ncols
    tm, tn = _mm_tiles(M, N, K * a.dtype.itemsize, K * b.dtype.itemsize, jnp.dtype(out_dtype).itemsize,
                       _ROW_CANDS, (512, 256, 128))
    assert col0 % tn == 0
    first = col0 // tn

    def body(a_ref, b_ref, o_ref):
        o_ref[...] = _dot(a_ref[...], b_ref[...], _NN).astype(o_ref.dtype)

    res, xres = hosted_call(
        body, ex, [a, b], name=name, out_shape=[jax.ShapeDtypeStruct((M, N), out_dtype)], grid=(N // tn, M // tm),
        in_specs=[pl.BlockSpec((tm, K), lambda j, i: (i, 0)), pl.BlockSpec((K, tn), lambda j, i: (0, first + j))],
        out_specs=[pl.BlockSpec((tm, tn), lambda j, i: (i, j))])
    return _one(res, xres, ex)


def matmul_nt(name, g, b, out_dtype=F32, ex=None, offsets=None):
    pieces = list(g) if isinstance(g, (list, tuple)) else [g]
    offsets = list(offsets) if offsets is not None else [0]
    M = pieces[0].shape[0]
    K, N = b.shape
    g_bytes = sum(p.shape[1] * p.dtype.itemsize for p in pieces)
    tm, tk = _mm_tiles(M, K, g_bytes, N * b.dtype.itemsize, jnp.dtype(out_dtype).itemsize, _ROW_CANDS, _COL_CANDS)

    def body(*refs):
        b_ref, o_ref = refs[-2:]
        acc = None
        for g_ref, off in zip(refs[:-2], offsets):
            part = _dot(g_ref[...], b_ref[:, off:off + g_ref.shape[1]], _NT)
            acc = part if acc is None else acc + part
        o_ref[...] = acc.astype(o_ref.dtype)

    res, xres = hosted_call(
        body, ex, pieces + [b], name=name, out_shape=[jax.ShapeDtypeStruct((M, K), out_dtype)], grid=(K // tk, M // tm),
        in_specs=[pl.BlockSpec((tm, p.shape[1]), lambda j, i: (i, 0)) for p in pieces]
        + [pl.BlockSpec((tk, N), lambda j, i: (j, 0))],
        out_specs=[pl.BlockSpec((tm, tk), lambda j, i: (i, j))])
    return _one(res, xres, ex)


def matmul_tn(name, a, g, ex=None):
    M, K = a.shape
    N = g.shape[1]
    tk, tn = _mm_tiles(K, N, M * a.dtype.itemsize, M * g.dtype.itemsize, 4, (512, 256, 128), (512, 256, 128))

    def body(a_ref, g_ref, o_ref):
        o_ref[...] = _dot(a_ref[...], g_ref[...], _TN)

    res, xres = hosted_call(
        body, ex, [a, g], name=name, out_shape=[jax.ShapeDtypeStruct((K, N), F32)], grid=(K // tk, N // tn),
        in_specs=[pl.BlockSpec((M, tk), lambda i, j: (0, i)), pl.BlockSpec((M, tn), lambda i, j: (0, j))],
        out_specs=[pl.BlockSpec((tk, tn), lambda i, j: (i, j))])
    return _one(res, xres, ex)


class Arg:
    def __init__(self, arr, block, imap, kind):
        self.arr, self.block, self.imap, self.kind = arr, block, imap, kind


class Rows:
    def __init__(self, nt, nct, tm, ncol=1):
        self.nt, self.nct, self.tm, self.ncol = nt, nct, tm, ncol

    def seg(self, i):
        return jnp.where(i >= self.nct, 1, 0)

    def spec(self, block, imap):
        return pl.BlockSpec(block, lambda j, i: imap(j, i, self.seg(i)))

    def row(self, arr, width, cb0=0, follow=False, roff=0, stride=1):
        f = stride if follow else 0
        return Arg(arr, (self.tm, width), lambda j, i, s: (i + roff, cb0 + f * j), "row")

    def vec(self, arr, follow=False, kind="acc"):
        w = arr.shape[1] // (self.ncol if follow else 1)
        f = 1 if follow else 0
        return Arg(arr, (1, w), lambda j, i, s: (0, f * j), kind)

    def segvec(self, arr, kind="seg"):
        return Arg(arr, (None, 1, arr.shape[2]), lambda j, i, s: (s, 0, 0), kind)


def _load(ref):
    return ref[...].astype(F32) if ref.dtype != F32 else ref[...]


def stage_fwd(name, f, rows, args, outs):
    n_in = len(args)

    def body(*refs):
        vals = [_load(r) for r in refs[:n_in]]
        res = f(*vals)
        for r, v in zip(refs[n_in:], res):
            r[...] = v.astype(r.dtype)

    T = rows.nt * rows.tm
    out_shape = [jax.ShapeDtypeStruct((T, w * (rows.ncol if fo else 1)), dt) for w, dt, fo in outs]
    out_specs = [pl.BlockSpec((rows.tm, w), (lambda j, i, fo=fo: (i, j if fo else 0))) for w, dt, fo in outs]
    res = _pcall(
        body, name=name, out_shape=out_shape, grid=(rows.ncol, rows.nt),
        in_specs=[rows.spec(a.block, a.imap) for a in args], out_specs=out_specs,
        compiler_params=_params("parallel", "parallel"),
    )(*[a.arr for a in args])
    return res


def stage_bwd(name, f, rows, args, cots, row_dtypes, ex=None, primal=()):
    n_in, n_ct = len(args), len(cots)
    diff = [k for k, a in enumerate(args) if a.kind != "const"]
    row_dt = {}
    for k in diff:
        if args[k].kind == "row":
            row_dt[k] = row_dtypes[len(row_dt)]

    def body(*refs):
        i = pl.program_id(1)
        vals = [_load(r) for r in refs[:n_in]]
        cts = tuple(_load(r) for r in refs[n_in:n_in + n_ct])
        outs = refs[n_in + n_ct:]

        def g(*dv):
            full = list(vals)
            for k, v in zip(diff, dv):
                full[k] = v
            return tuple(f(*full))

        prim, vjp = jax.vjp(g, *[vals[k] for k in diff])
        grads = vjp(cts)
        for o, v in zip(outs[len(diff):], prim):
            o[...] = v.astype(o.dtype)
        for k, o, gr in zip(diff, outs, grads):
            kind = args[k].kind
            if kind == "row":
                o[...] = gr.astype(o.dtype)
            else:
                first = (i == 0) | (i == rows.nct) if kind == "seg" else (i == 0)

                @pl.when(first)
                def _():
                    o[...] = gr.astype(o.dtype)

                @pl.when(jnp.logical_not(first))
                def _():
                    o[...] += gr.astype(o.dtype)

    T = rows.nt * rows.tm
    out_shape, out_specs = [], []
    for k in diff:
        a = args[k]
        if a.kind == "row":
            out_shape.append(jax.ShapeDtypeStruct((T, a.block[1] * (rows.ncol if _follows(a) else 1)), row_dt[k]))
            fo = _follows(a)
            out_specs.append(pl.BlockSpec(a.block, (lambda j, i, fo=fo: (i, j if fo else 0))))
        else:
            out_shape.append(jax.ShapeDtypeStruct(a.arr.shape, F32))
            out_specs.append(rows.spec(a.block, a.imap))
    for w, dt in primal:
        out_shape.append(jax.ShapeDtypeStruct((T, w), dt))
        out_specs.append(pl.BlockSpec((rows.tm, w), lambda j, i: (i, 0)))
    res, xres = hosted_call(
        body, ex, [a.arr for a in list(args) + list(cots)], name=name, out_shape=out_shape, grid=(rows.ncol, rows.nt),
        in_specs=[rows.spec(a.block, a.imap) for a in list(args) + list(cots)], out_specs=out_specs)
    return res if ex is None else (res, xres)


def _follows(a):
    return a.imap(1, 0, 0)[-1] != a.imap(0, 0, 0)[-1]


def _rms(x):
    return x * lax.rsqrt(jnp.mean(x * x, axis=-1, keepdims=True) + EPS)


def f_norm_mod(x, g, sh, sc):
    return ((_rms(x) * g) * (1.0 + sc) + sh,)


def f_resid_norm_mod(x, mo, ga, g, sh, sc):
    x1 = x + ga * mo
    return x1, (_rms(x1) * g) * (1.0 + sc) + sh


def f_resid(x, dn, ga):
    return (x + ga * dn,)


def f_silu(x):
    return (x * jax.nn.sigmoid(x),)


def f_bias(x, b):
    return (x + b,)


def f_ssd_gate(y0, y1, xs, z, dskip, nw):
    y = y0 + y1 + dskip * xs
    return (_rms(y * (z * jax.nn.sigmoid(z))) * nw,)


def f_pool(u, pmat, pmat_t, inv_cnt, pw, scale):
    pm = _lin01(pmat, pmat_t, u) * inv_cnt - u
    return (_mm(pm, pw) * scale,)


def f_merge(o_ssd, o_pool, gl_ssd, gl_pool):
    return (jax.nn.sigmoid(gl_ssd) * o_ssd + jax.nn.sigmoid(gl_pool) * o_pool,)


def _column_splitter(n):
    @jax.custom_vjp
    def split(x):
        w = x.shape[1] // n
        return tuple(x[:, k * w:(k + 1) * w] for k in range(n))

    def fwd(x):
        return split(x), None

    def bwd(_, g):
        return (jnp.concatenate(g, axis=1),)

    split.defvjp(fwd, bwd)
    return split


_halve_cols = _column_splitter(2)
_quarter_cols = _column_splitter(len(POOL_WINDOWS))


def f_swiglu(gu):
    a, b = _halve_cols(gu)
    return ((a * jax.nn.sigmoid(a)) * b,)


def f_pool_all(u, pmat, pmat_t, inv_cnt, scale, *pws):
    outs = [f_pool(part, pmat[k], pmat_t[k], inv_cnt[k], pws[k], 1.0)[0] for k, part in enumerate(_quarter_cols(u))]
    return (jnp.concatenate(outs, axis=1) * scale,)


def f_loss(x, tgt, g):
    err = _rms(x) * g - tgt
    return (0.5 * jnp.mean(err * err, axis=-1, keepdims=True),)


CONV_TILE = 128


def _shift_rows(v, j, n_ctx):
    if j == 0:
        return v
    T = v.shape[0]
    r = lax.broadcasted_iota(jnp.int32, v.shape, 0)
    lo = jnp.where(r >= n_ctx, n_ctx, 0)
    hi = jnp.where(r >= n_ctx, T, n_ctx)
    ok = (r + j >= lo) & (r + j < hi)
    return jnp.where(ok, pltpu.roll(v, (-j) % T, 0), 0.0)


def conv_fwd(name, proj, conv_w, conv_b, n_ctx, width, ex=None):
    T = proj.shape[0]
    half = SSD_CONV // 2

    def body(u_ref, w_ref, b_ref, o_ref):
        u = u_ref[...].astype(F32)
        pre = jnp.broadcast_to(b_ref[...], u.shape)
        for k in range(SSD_CONV):
            pre = pre + w_ref[k:k + 1, :] * _shift_rows(u, k - half, n_ctx)
        o_ref[...] = pre * jax.nn.sigmoid(pre)

    col = lambda t: (0, t)
    res, xres = hosted_call(
        body, ex, [proj, conv_w, conv_b], name=name, out_shape=[jax.ShapeDtypeStruct((T, width), F32)],
        grid=(width // CONV_TILE,),
        in_specs=[pl.BlockSpec((T, CONV_TILE), col), pl.BlockSpec((SSD_CONV, CONV_TILE), col),
                  pl.BlockSpec((1, CONV_TILE), col)],
        out_specs=[pl.BlockSpec((T, CONV_TILE), col)])
    return res[0], xres


def conv_bwd(name, proj, conv_w, conv_b, d_act2, d_skip, n_ctx, width, ex=None):
    T = proj.shape[0]
    half = SSD_CONV // 2

    def body(u_ref, w_ref, b_ref, c0_ref, c1_ref, cs_ref, du_ref, dw_ref, db_ref):
        t = pl.program_id(0)
        u = u_ref[...].astype(F32)
        pre = jnp.broadcast_to(b_ref[...], u.shape)
        for k in range(SSD_CONV):
            pre = pre + w_ref[k:k + 1, :] * _shift_rows(u, k - half, n_ctx)
        sg = jax.nn.sigmoid(pre)
        ct = c0_ref[...].astype(F32) + c1_ref[...].astype(F32) + jnp.where(t % 4 < 2, cs_ref[...].astype(F32), 0.0)
        dpre = ct * (sg * (1.0 + pre * (1.0 - sg)))
        du = jnp.zeros_like(u)
        for k in range(SSD_CONV):
            du = du + w_ref[k:k + 1, :] * _shift_rows(dpre, half - k, n_ctx)
            dw_ref[k:k + 1, :] = jnp.sum(dpre * _shift_rows(u, k - half, n_ctx), axis=0, keepdims=True)
        du_ref[...] = du.astype(du_ref.dtype)
        db_ref[...] = jnp.sum(dpre, axis=0, keepdims=True)

    col = lambda t: (0, t)
    skip_col = lambda t: (0, (t // 4) * 2 + jnp.minimum(t % 4, 1))
    res, xres = hosted_call(
        body, ex, [proj, conv_w, conv_b, d_act2, d_act2, d_skip], name=name,
        out_shape=[jax.ShapeDtypeStruct((T, width), ACT_DTYPE), jax.ShapeDtypeStruct((SSD_CONV, width), F32),
                   jax.ShapeDtypeStruct((1, width), F32)],
        grid=(width // CONV_TILE,),
        in_specs=[pl.BlockSpec((T, CONV_TILE), col), pl.BlockSpec((SSD_CONV, CONV_TILE), col),
                  pl.BlockSpec((1, CONV_TILE), col), pl.BlockSpec((T, CONV_TILE), col),
                  pl.BlockSpec((T, CONV_TILE), lambda t: (1, t)), pl.BlockSpec((T, CONV_TILE), skip_col)],
        out_specs=[pl.BlockSpec((T, CONV_TILE), col), pl.BlockSpec((SSD_CONV, CONV_TILE), col),
                   pl.BlockSpec((1, CONV_TILE), col)])
    return res[0], res[1], res[2], xres


@jax.custom_vjp
def _cumsum_mat(tri, tri_t, a):
    return jnp.dot(tri, a, precision=lax.Precision.HIGHEST, preferred_element_type=F32)


def _cumsum_fwd(tri, tri_t, a):
    return _cumsum_mat(tri, tri_t, a), (tri, tri_t)


def _cumsum_bwd(res, g):
    tri, tri_t = res
    return (jnp.zeros_like(tri), jnp.zeros_like(tri_t),
            jnp.dot(tri_t, g, precision=lax.Precision.HIGHEST, preferred_element_type=F32))


_cumsum_mat.defvjp(_cumsum_fwd, _cumsum_bwd)


def _ssd_dt(dtraw, dt_bias, a_log, tri, tri_t):
    dt_all = jax.nn.softplus(dtraw + dt_bias)
    a_all = dt_all * (-jnp.exp(a_log))
    return dt_all, a_all, _cumsum_mat(tri, tri_t, a_all)


def _ssd_chunk(xs, bm, cm, dt_all, a_all, s_all, s_in, mask, idx0):
    (xs,), (s_in,) = xs, s_in
    Q = xs.shape[0]
    hpg = xs.shape[1] // SSD_HEADDIM
    lane = lax.broadcasted_iota(jnp.int32, dt_all.shape, 1)
    head = lax.broadcasted_iota(jnp.int32, xs.shape, 1) // SSD_HEADDIM
    head1 = lax.broadcasted_iota(jnp.int32, (1, xs.shape[1]), 1) // SSD_HEADDIM

    def pick(v, r):
        return jnp.sum(jnp.where(lane == idx0 + r, v, 0.0), axis=1, keepdims=True)

    def expand(cols, hd):
        out = cols[hpg - 1]
        for r in range(hpg - 2, -1, -1):
            out = jnp.where(hd == r, cols[r], out)
        return out

    def spread(*cols):
        return expand([jnp.broadcast_to(c, xs.shape) for c in cols], head)

    dt_r = [pick(dt_all, r) for r in range(hpg)]
    s_r = [pick(s_all, r) for r in range(hpg)]
    stot_r = [jnp.sum(jnp.where(lane == idx0 + r, a_all, 0.0), keepdims=True).reshape(1, 1) for r in range(hpg)]

    xd = xs * spread(*dt_r)
    cb = _mm_nt(cm, bm)
    weights, stacked = [], []
    for r in range(hpg):
        sm = jnp.broadcast_to(s_r[r], (Q, Q))
        weights.append(cb * jnp.exp(jnp.where(mask, sm - sm.T, NEG)))
        stacked.append(jnp.where(head == r, xd, 0.0))
    y = spread(*[jnp.exp(c) for c in s_r]) * _mm(cm, s_in)
    y = y + _mm(jnp.concatenate(weights, axis=1), jnp.concatenate(stacked, axis=0))
    to_end = spread(*[jnp.exp(t - c) for t, c in zip(stot_r, s_r)])
    carry = expand([jnp.broadcast_to(jnp.exp(t), (1, xs.shape[1])) for t in stot_r], head1)
    s_out = carry * s_in + _mm_tn(bm, xd * to_end)
    return [y], [s_out]


def _scan_consts():
    q = SSD_CHUNK
    i = np.arange(q)[:, None]
    j = np.arange(q)[None, :]
    fwd = (j <= i).astype(np.float32)
    bwd = (j >= i).astype(np.float32)
    tri = np.stack([fwd, bwd])
    return jnp.asarray(tri), jnp.asarray(np.stack([fwd.T, bwd.T]))


def _chunk_of(d, k, ncc, nc):
    rev = jnp.where(k < ncc, ncc - 1 - k, nc - 1 + ncc - k)
    return jnp.where(d == 0, k, rev)


def ssd_fwd(name, xbc, proj, dt_cb, dt_bias, a_log, n_ctx, ex=None):
    T = xbc.shape[0]
    q, G = SSD_CHUNK, SSD_GROUPS
    nc, ncc = T // q, n_ctx // q
    gw = xbc.shape[1] // G
    xw = gw - 2 * SSD_STATE
    hpg = xw // SSD_HEADDIM
    nh = G * hpg
    tri, tri_t = _scan_consts()

    gs = SSD_GROUPS_PER_STEP

    def body(x_ref, dt_ref, bias_ref, alog_ref, tri_ref, trit_ref, y_ref, sin_ref, state):
        d, gb, k = pl.program_id(0), pl.program_id(1), pl.program_id(2)

        @pl.when(k == 0)
        def _():
            state[...] = jnp.zeros_like(state)

        tri_v = tri_ref[...]
        dt_all, a_all, s_all = _ssd_dt(dt_ref[...], bias_ref[...], alog_ref[...], tri_v, trit_ref[...])
        pairs = [(0, xw)]
        for j in range(gs):
            o = j * gw
            sin_ref[j] = state[j]
            ys, s_outs = _ssd_chunk(
                [x_ref[:, o + lo:o + hi] for lo, hi in pairs], x_ref[:, o + xw:o + xw + SSD_STATE],
                x_ref[:, o + xw + SSD_STATE:o + gw], dt_all, a_all, s_all, [state[j, :, lo:hi] for lo, hi in pairs],
                tri_v > 0.5, d * nh + (gb * gs + j) * hpg)
            for (lo, hi), y, s_out in zip(pairs, ys, s_outs):
                y_ref[:, j * xw + lo:j * xw + hi] = y.astype(y_ref.dtype)
                state[j, :, lo:hi] = s_out

    ch = lambda d, g, k: _chunk_of(d, k, ncc, nc)
    res, xres = hosted_call(
        body, ex, [xbc, proj, dt_bias, a_log, tri, tri_t], name=name,
        out_shape=[jax.ShapeDtypeStruct((2 * T, G * xw), ACT_DTYPE),
                   jax.ShapeDtypeStruct((2, nc, G, SSD_STATE, xw), F32)],
        grid=(2, G // gs, nc),
        in_specs=[pl.BlockSpec((q, gs * gw), lambda d, g, k: (ch(d, g, k), g)),
                  pl.BlockSpec((q, 128), lambda d, g, k: (ch(d, g, k), dt_cb)),
                  pl.BlockSpec((1, 128), lambda d, g, k: (0, 0)),
                  pl.BlockSpec((1, 128), lambda d, g, k: (0, 0)),
                  pl.BlockSpec((None, q, q), lambda d, g, k: (d, 0, 0)),
                  pl.BlockSpec((None, q, q), lambda d, g, k: (d, 0, 0))],
        out_specs=[pl.BlockSpec((q, gs * xw), lambda d, g, k: (d * nc + ch(d, g, k), g)),
                   pl.BlockSpec((None, None, gs, SSD_STATE, xw), lambda d, g, k: (d, k, g, 0, 0))],
        scratch_shapes=[pltpu.VMEM((gs, SSD_STATE, xw), F32)])
    return res[0], res[1], xres


def ssd_bwd(name, xbc, proj, dt_cb, dt_bias, a_log, states, dy, n_ctx, ex=None):
    T = xbc.shape[0]
    q, G = SSD_CHUNK, SSD_GROUPS
    nc, ncc = T // q, n_ctx // q
    gw = xbc.shape[1] // G
    xw = gw - 2 * SSD_STATE
    hpg = xw // SSD_HEADDIM
    nh = G * hpg
    tri, tri_t = _scan_consts()

    gs = SSD_GROUPS_PER_STEP

    def body(x_ref, dt_ref, bias_ref, alog_ref, tri_ref, trit_ref, sin_ref, dy_ref,
             dx_ref, ddt_ref, dbias_ref, dalog_ref, dstate):
        d, gb, k = pl.program_id(0), pl.program_id(1), pl.program_id(2)
        first = (d == 0) & (gb == 0) & (k == 0)

        @pl.when(first)
        def _():
            ddt_ref[...] = jnp.zeros_like(ddt_ref)
            dbias_ref[...] = jnp.zeros_like(dbias_ref)
            dalog_ref[...] = jnp.zeros_like(dalog_ref)

        @pl.when(k == 0)
        def _():
            dstate[...] = jnp.zeros_like(dstate)

        tri_v, trit_v = tri_ref[...], trit_ref[...]
        mask = tri_v > 0.5

        pairs = [(0, xw)]
        npair = len(pairs)
        per = 2 * npair + 2

        def fn(dtraw, bias, alog, *per_group):
            dt_all, a_all, s_all = _ssd_dt(dtraw, bias, alog, tri_v, trit_v)
            ys, s_outs = [], []
            for j in range(gs):
                grp = per_group[per * j:per * (j + 1)]
                y, s_out = _ssd_chunk(list(grp[:npair]), grp[npair], grp[npair + 1], dt_all, a_all, s_all,
                                      list(grp[npair + 2:]), mask, d * nh + (gb * gs + j) * hpg)
                ys += y
                s_outs += s_out
            return ys, s_outs

        per_group = []
        for j in range(gs):
            o = j * gw
            per_group += [x_ref[:, o + lo:o + hi] for lo, hi in pairs]
            per_group += [x_ref[:, o + xw:o + xw + SSD_STATE], x_ref[:, o + xw + SSD_STATE:o + gw]]
            per_group += [sin_ref[j, :, lo:hi] for lo, hi in pairs]
        _, vjp = jax.vjp(fn, dt_ref[...], bias_ref[...], alog_ref[...], *per_group)
        cts = vjp(([dy_ref[:, j * xw + lo:j * xw + hi].astype(F32) for j in range(gs) for lo, hi in pairs],
                   [dstate[j, :, lo:hi] for j in range(gs) for lo, hi in pairs]))
        ddt, dbias, dalog = cts[:3]
        for j in range(gs):
            o = j * gw
            grp = cts[3 + per * j:3 + per * (j + 1)]
            for (lo, hi), dxs, ds_in in zip(pairs, grp[:npair], grp[npair + 2:]):
                dx_ref[:, o + lo:o + hi] = dxs.astype(dx_ref.dtype)
                dstate[j, :, lo:hi] = ds_in
            dx_ref[:, o + xw:o + xw + SSD_STATE] = grp[npair].astype(dx_ref.dtype)
            dx_ref[:, o + xw + SSD_STATE:o + gw] = grp[npair + 1].astype(dx_ref.dtype)
        row0 = pl.multiple_of(_chunk_of(d, nc - 1 - k, ncc, nc) * q, q)
        ddt_ref[pl.ds(row0, q), :] += ddt
        dbias_ref[...] += dbias
        dalog_ref[...] += dalog

    ch = lambda d, g, k: _chunk_of(d, nc - 1 - k, ncc, nc)
    res, xres = hosted_call(
        body, ex, [xbc, proj, dt_bias, a_log, tri, tri_t, states, dy], name=name,
        out_shape=[jax.ShapeDtypeStruct((2 * T, G * gw), ACT_DTYPE), jax.ShapeDtypeStruct((T, 128), F32),
                   jax.ShapeDtypeStruct((1, 128), F32), jax.ShapeDtypeStruct((1, 128), F32)],
        grid=(2, G // gs, nc),
        in_specs=[pl.BlockSpec((q, gs * gw), lambda d, g, k: (ch(d, g, k), g)),
                  pl.BlockSpec((q, 128), lambda d, g, k: (ch(d, g, k), dt_cb)),
                  pl.BlockSpec((1, 128), lambda d, g, k: (0, 0)),
                  pl.BlockSpec((1, 128), lambda d, g, k: (0, 0)),
                  pl.BlockSpec((None, q, q), lambda d, g, k: (d, 0, 0)),
                  pl.BlockSpec((None, q, q), lambda d, g, k: (d, 0, 0)),
                  pl.BlockSpec((None, None, gs, SSD_STATE, xw), lambda d, g, k: (d, nc - 1 - k, g, 0, 0)),
                  pl.BlockSpec((q, gs * xw), lambda d, g, k: (ch(d, g, k), g))],
        out_specs=[pl.BlockSpec((q, gs * gw), lambda d, g, k: (d * nc + ch(d, g, k), g)),
                   pl.BlockSpec((T, 128), lambda d, g, k: (0, 0)),
                   pl.BlockSpec((1, 128), lambda d, g, k: (0, 0)),
                   pl.BlockSpec((1, 128), lambda d, g, k: (0, 0))],
        scratch_shapes=[pltpu.VMEM((gs, SSD_STATE, xw), F32)])
    return res[0], res[1], res[2], res[3], xres


def _perm_xbc(a):
    G = SSD_GROUPS
    n = a.shape[-1]
    gn = G * SSD_STATE
    di = n - 2 * gn
    lead = a.shape[:-1]
    xs = a[..., :di].reshape(lead + (G, di // G))
    bm = a[..., di:di + gn].reshape(lead + (G, SSD_STATE))
    cm = a[..., di + gn:].reshape(lead + (G, SSD_STATE))
    return jnp.concatenate([xs, bm, cm], axis=-1).reshape(lead + (n,))


def _unperm_xbc(a):
    G = SSD_GROUPS
    n = a.shape[-1]
    gn = G * SSD_STATE
    di = n - 2 * gn
    lead = a.shape[:-1]
    r = a.reshape(lead + (G, n // G))
    xw = di // G
    return jnp.concatenate([r[..., :xw].reshape(lead + (di,)), r[..., xw:xw + SSD_STATE].reshape(lead + (gn,)),
                            r[..., xw + SSD_STATE:].reshape(lead + (gn,))], axis=-1)


def _pool_consts(tm, n_ctx):
    assert n_ctx == tm and tm % GRID_W == 0
    mats, cnts = [], []
    for seq in (n_ctx, GRID_W):
        t = np.arange(tm)
        tt = t % seq
        base = t - tt
        ms, cs = [], []
        for k in POOL_WINDOWS:
            lo = np.clip(tt - k // 2, 0, seq) + base
            hi = np.clip(tt + k // 2, 0, seq) + base
            m = ((t[None, :] >= lo[:, None]) & (t[None, :] < hi[:, None])).astype(np.float32)
            ms.append(m)
            cs.append((1.0 / (hi - lo).astype(np.float32))[:, None])
        mats.append(np.stack(ms))
        cnts.append(np.stack(cs))
    m = np.stack(mats)
    return jnp.asarray(m), jnp.asarray(np.swapaxes(m, -1, -2)), jnp.asarray(np.stack(cnts).astype(np.float32))


def _prep_layer_weights(w_ada, b_ada, g_mix, w_in, conv_w, conv_b, dt_bias, a_log, d_skip, ssd_norm_w, w_ssd_out,
                        pool_w, pool_scale, w_pool_out, w_out, g_ffn, w_gate_up, w_down):
    D = w_in.shape[0]
    di = ssd_norm_w.shape[0]
    xbc = conv_w.shape[1]
    nh2 = dt_bias.size
    pw = pool_scale.shape[0]
    o = 0
    wz = w_in[:, o:o + di]; o += di
    wx = w_in[:, o:o + xbc]; o += xbc
    wdt = w_in[:, o:o + nh2]; o += nh2
    wp = w_in[:, o:o + pw]; o += pw
    wg = w_in[:, o:]
    w1 = jnp.concatenate([_perm_xbc(wx), wz, wg, wp, wdt, jnp.zeros((D, DT_PAD - nh2), w_in.dtype)], axis=1)
    pad128 = lambda v: jnp.concatenate([v.reshape(1, -1), jnp.zeros((1, 128 - v.size), F32)], axis=1)
    return dict(
        w_ada=w_ada, b_ada=b_ada.reshape(1, -1), g_mix=g_mix.reshape(1, -1), w1=w1,
        conv_w=_perm_xbc(conv_w), conv_b=_perm_xbc(conv_b.reshape(1, -1)),
        dt_bias=pad128(dt_bias), a_log=pad128(a_log),
        dskip=jnp.repeat(d_skip[0] + d_skip[1], SSD_HEADDIM).reshape(1, -1),
        ssd_norm_w=ssd_norm_w.reshape(1, -1), w_ssd_out=w_ssd_out, pool_w=pool_w,
        pool_scale=pool_scale.reshape(1, -1), w_pool_out=w_pool_out, w_out=w_out, g_ffn=g_ffn.reshape(1, -1),
        w_gate_up=w_gate_up, w_down=w_down)


def _unprep_layer_grads(g, dims):
    di, xbc, nh2, pw = dims
    dxbc, dz, dgs, dgp, dp, ddt = g["w1"]
    r = dxbc.reshape(SSD_GROUPS, xbc // SSD_GROUPS, dxbc.shape[1])
    xw = di // SSD_GROUPS
    parts = [r[:, :xw], r[:, xw:xw + SSD_STATE], r[:, xw + SSD_STATE:]]
    w_in_t = jnp.concatenate([dz] + [p.reshape(-1, dxbc.shape[1]) for p in parts] + [ddt[:nh2], dp, dgs, dgp], axis=0)
    nh = nh2 // 2
    dsk = g["dskip"].reshape(nh, SSD_HEADDIM).sum(axis=1)
    return dict(
        w_ada=g["w_ada"], b_ada=g["b_ada"].reshape(-1), g_mix=g["g_mix"].reshape(-1),
        w_in=w_in_t,
        conv_w=_unperm_xbc(g["conv_w"]), conv_b=_unperm_xbc(g["conv_b"]).reshape(-1),
        dt_bias=g["dt_bias"][0, :nh2].reshape(2, nh), a_log=g["a_log"][0, :nh2].reshape(2, nh),
        d_skip=jnp.stack([dsk, dsk]), ssd_norm_w=g["ssd_norm_w"].reshape(-1), w_ssd_out=g["w_ssd_out"],
        pool_w=g["pool_w"], pool_scale=g["pool_scale"].reshape(-1), w_pool_out=g["w_pool_out"], w_out=g["w_out"],
        g_ffn=g["g_ffn"].reshape(-1), w_gate_up=g["w_gate_up"], w_down=g["w_down"])


COND_ROWS = 16


def _split_mods(m):
    d = m.shape[1] // 6
    return [m[:2, k * d:(k + 1) * d].reshape(2, 1, d) for k in range(6)]


def _pool_args(rows, proj, col_block, width, pc, w):
    seg_const = lambda a: Arg(a, (None,) + a.shape[1:], lambda j, i, s: (s, 0, 0, 0), "const")
    pws = [Arg(w["pool_w"][k], w["pool_w"].shape[1:], lambda j, i, s: (0, 0), "acc") for k in range(w["pool_w"].shape[0])]
    return [rows.row(proj, width, col_block)] + [seg_const(a) for a in pc] + [rows.vec(w["pool_scale"])] + pws


TALL_ROW_TILE = 1088


def _tall_rows(T, ncol):
    tm = max(t for t in range(16, min(T, TALL_ROW_TILE) + 1, 16) if T % t == 0)
    return Rows(T // tm, 0, tm, ncol)


def _hosted(hosts, box, key):
    fn = (hosts or {}).get(key)
    return fn(box) if fn else None


def _layer_fwd(l, x, cond_s, w, rows, n_ctx, pc, hosts=None, box=None):
    T, D = x.shape
    nt, nct, tm = rows.nt, rows.nct, rows.tm
    n = lambda s: f"l{l}_{s}"
    crow = Rows(1, 0, COND_ROWS)
    mraw = matmul_nn(n("ada_mm"), cond_s, w["w_ada"])
    (m,) = stage_fwd(n("ada_bias"), f_bias, crow, [crow.row(mraw, mraw.shape[1]), crow.vec(w["b_ada"])],
                     [(mraw.shape[1], F32, False)])
    sh1, sc1, ga1, sh2, sc2, ga2 = _split_mods(m)

    (h1,) = stage_fwd(n("norm1"), f_norm_mod, rows,
                      [rows.row(x, D), rows.vec(w["g_mix"]), rows.segvec(sh1), rows.segvec(sc1)],
                      [(D, ACT_DTYPE, False)])
    xbc_w = w["conv_w"].shape[1]
    di = w["ssd_norm_w"].shape[1]
    pw = w["pool_scale"].shape[1]
    c_z, c_g, c_p, c_dt = xbc_w, xbc_w + di, xbc_w + di + 2 * pw, xbc_w + di + 3 * pw
    ex = _hosted(hosts, box, "in_mm")
    proj = matmul_nn(n("in_mm"), h1, w["w1"], out_dtype=ACT_DTYPE, ex=ex, ncols=c_dt)
    if ex is not None:
        proj, box["in_mm"] = proj
    dtraw = matmul_nn(n("in_dt_mm"), h1, w["w1"], col0=c_dt, ncols=128)
    ex = _hosted(hosts, box, "conv")
    xbc, xres = conv_fwd(n("conv"), proj, w["conv_w"], w["conv_b"], n_ctx, xbc_w, ex)
    if ex is not None:
        box["conv"] = xres
    ex = _hosted(hosts, box, "ssd")
    y2, states, xres = ssd_fwd(n("ssd"), xbc, dtraw, 0, w["dt_bias"], w["a_log"], n_ctx, ex)
    if ex is not None:
        box["ssd"] = xres

    G = SSD_GROUPS
    gw = di // G
    r8 = _tall_rows(T, G)
    gate_args = [r8.row(y2, gw, 0, True), r8.row(y2, gw, 0, True, roff=r8.nt), r8.row(xbc, gw, 0, True, stride=2),
                 r8.row(proj, gw, c_z // gw, True), r8.vec(w["dskip"], True), r8.vec(w["ssd_norm_w"], True)]
    (ynw,) = stage_fwd(n("ssd_gate"), f_ssd_gate, r8, gate_args, [(gw, ACT_DTYPE, True)])
    o_ssd = matmul_nn(n("ssd_out_mm"), ynw, w["w_ssd_out"])

    nw = len(POOL_WINDOWS)
    pg = pw // nw
    (ps,) = stage_fwd(n("pool"), f_pool_all, rows, _pool_args(rows, proj, c_p // pw, pw, pc, w), [(pw, ACT_DTYPE, False)])
    o_pool = matmul_nn(n("pool_out_mm"), ps, w["w_pool_out"])

    merge_args = [rows.row(o_ssd, D), rows.row(o_pool, D), rows.row(proj, pw, c_g // pw), rows.row(proj, pw, c_g // pw + 1)]
    (mg,) = stage_fwd(n("merge"), f_merge, rows, merge_args, [(D, ACT_DTYPE, False)])
    mo = matmul_nn(n("out_mm"), mg, w["w_out"])

    rn_args = [rows.row(x, D), rows.row(mo, D), rows.segvec(ga1), rows.vec(w["g_ffn"]), rows.segvec(sh2), rows.segvec(sc2)]
    x1, h2 = stage_fwd(n("norm2"), f_resid_norm_mod, rows, rn_args, [(D, F32, False), (D, ACT_DTYPE, False)])
    ex = _hosted(hosts, box, "gate_up_mm")
    gu = matmul_nn(n("gate_up_mm"), h2, w["w_gate_up"], ex=ex)
    if ex is not None:
        gu, box["gate_up_mm"] = gu
    fh = gu.shape[1] // 2
    (act,) = stage_fwd(n("swiglu"), f_swiglu, rows, [rows.row(gu, 2 * fh)], [(fh, ACT_DTYPE, False)])
    dn = matmul_nn(n("down_mm"), act, w["w_down"])
    res_args = [rows.row(x1, D), rows.row(dn, D), rows.segvec(ga2)]
    (x2,) = stage_fwd(n("resid2"), f_resid, rows, res_args, [(D, F32, False)])
    saved = dict(x=x, mraw=mraw, mods=(sh1, sc1, ga1, sh2, sc2, ga2), h1=h1, proj=proj, dtraw=dtraw, xbc=xbc, y2=y2,
                 states=states,
                 ynw=ynw, o_ssd=o_ssd, ps=ps, o_pool=o_pool, mg=mg, mo=mo, x1=x1, h2=h2, gu=gu, act=act, dn=dn,
                 cols=(c_z, c_g, c_p, c_dt))
    return x2, saved


def f_norm_mod_keep(x, g, sh, sc):
    return f_norm_mod(x, g, sh, sc)[0], x


def _layer_bwd(l, dx2, cond_s, w, s, rows, n_ctx, pc, hosts=None, box=None):
    T, D = dx2.shape
    nt, nct, tm = rows.nt, rows.nct, rows.tm
    n = lambda t: f"l{l}_{t}_bwd"
    sh1, sc1, ga1, sh2, sc2, ga2 = s["mods"]
    c_z, c_g, c_p, c_dt = s["cols"]
    x, proj, xbc, y2, gu = s["x"], s["proj"], s["xbc"], s["y2"], s["gu"]
    g = {}
    if box is not None:
        box["g"] = g

    res_args = [rows.row(s["x1"], D), rows.row(s["dn"], D), rows.segvec(ga2)]
    res_args[0].kind = "const"
    dx1 = dx2
    ddn, dga2 = stage_bwd(n("resid2"), f_resid, rows, res_args, [rows.row(dx2, D)], [ACT_DTYPE])
    ex = _hosted(hosts, box, "down_dx")
    dact = matmul_nt(n("down_dx"), ddn, w["w_down"], ex=ex)
    if ex is not None:
        dact, box["down_dx"] = dact
    g["w_down"] = matmul_tn(n("down_dw"), s["act"], ddn)
    fh = gu.shape[1] // 2
    (dgu,) = stage_bwd(n("swiglu"), f_swiglu, rows, [rows.row(gu, 2 * fh)], [rows.row(dact, fh)], [ACT_DTYPE])
    dh2 = matmul_nt(n("gate_up_dx"), dgu, w["w_gate_up"])
    g["w_gate_up"] = matmul_tn(n("gate_up_dw"), s["h2"], dgu)

    rn_args = [rows.row(x, D), rows.row(s["mo"], D), rows.segvec(ga1), rows.vec(w["g_ffn"]), rows.segvec(sh2), rows.segvec(sc2)]
    dxr, dmo, dga1, g["g_ffn"], dsh2, dsc2 = stage_bwd(
        n("norm2"), f_resid_norm_mod, rows, rn_args, [rows.row(dx1, D), rows.row(dh2, D)], [F32, ACT_DTYPE])
    dmg = matmul_nt(n("out_dx"), dmo, w["w_out"])
    g["w_out"] = matmul_tn(n("out_dw"), s["mg"], dmo)

    pw = w["pool_scale"].shape[1]
    merge_args = [rows.row(s["o_ssd"], D), rows.row(s["o_pool"], D), rows.row(proj, pw, c_g // pw), rows.row(proj, pw, c_g // pw + 1)]
    do_ssd, do_pool, dgl_s, dgl_p = stage_bwd(n("merge"), f_merge, rows, merge_args, [rows.row(dmg, D)], [ACT_DTYPE] * 4)
    dps = matmul_nt(n("pool_out_dx"), do_pool, w["w_pool_out"])
    g["w_pool_out"] = matmul_tn(n("pool_out_dw"), s["ps"], do_pool)

    nw = len(POOL_WINDOWS)
    pg = pw // nw
    du_pool, g["pool_scale"], *dpw = stage_bwd(n("pool"), f_pool_all, rows, _pool_args(rows, proj, c_p // pw, pw, pc, w),
                                               [rows.row(dps, pw)], [ACT_DTYPE])
    g["pool_w"] = jnp.stack(dpw)

    dynw = matmul_nt(n("ssd_out_dx"), do_ssd, w["w_ssd_out"])
    g["w_ssd_out"] = matmul_tn(n("ssd_out_dw"), s["ynw"], do_ssd)
    G = SSD_GROUPS
    di = w["ssd_norm_w"].shape[1]
    gw = di // G
    r8 = _tall_rows(T, G)
    gate_args = [r8.row(y2, gw, 0, True), r8.row(y2, gw, 0, True, roff=r8.nt), r8.row(xbc, gw, 0, True, stride=2),
                 r8.row(proj, gw, c_z // gw, True), r8.vec(w["dskip"], True), r8.vec(w["ssd_norm_w"], True)]
    gate_args[1].kind = "const"
    ex = _hosted(hosts, box, "ssd_gate")
    res = stage_bwd(n("ssd_gate"), f_ssd_gate, r8, gate_args, [r8.row(dynw, gw, 0, True)], [ACT_DTYPE] * 3, ex)
    if ex is not None:
        res, box["ssd_gate"] = res
    dy, dxs_skip, dz, g["dskip"], g["ssd_norm_w"] = res

    ex = _hosted(hosts, box, "ssd")
    dxbc2, ddt, g["dt_bias"], g["a_log"], xres = ssd_bwd(n("ssd"), xbc, s["dtraw"], 0, w["dt_bias"], w["a_log"],
                                                         s["states"], dy, n_ctx, ex)
    if ex is not None:
        box["ssd"] = xres
    xbc_w = xbc.shape[1]
    ex = _hosted(hosts, box, "conv")
    dxbc_raw, g["conv_w"], g["conv_b"], xres = conv_bwd(n("conv"), proj, w["conv_w"], w["conv_b"], dxbc2, dxs_skip,
                                                         n_ctx, xbc_w, ex)
    if ex is not None:
        box["conv"] = xres
    pieces = [dxbc_raw, dz, dgl_s, dgl_p, du_pool, ddt]
    offsets = [0, c_z, c_g, c_g + pw, c_p, c_dt]
    ex = _hosted(hosts, box, "in_dx")
    dh1 = matmul_nt(n("in_dx"), pieces, w["w1"], ex=ex, offsets=offsets)
    if ex is not None:
        dh1, box["in_dx"] = dh1
    ex = _hosted(hosts, box, "in_dw")
    first = matmul_tn(n("in_dw0"), pieces[0], s["h1"], ex=ex)
    if ex is not None:
        first, box["in_dw"] = first
    g["w1"] = [first] + [matmul_tn(n(f"in_dw{k}"), p, s["h1"]) for k, p in enumerate(pieces) if k]

    n1_args = [rows.row(x, D), rows.vec(w["g_mix"]), rows.segvec(sh1), rows.segvec(sc1)]
    dx, g["g_mix"], dsh1, dsc1 = stage_bwd(n("norm1"), f_norm_mod_keep, rows, n1_args,
                                           [rows.row(dh1, D), rows.row(dxr, D)], [F32])

    dm = jnp.concatenate([v.reshape(2, D) for v in (dsh1, dsc1, dga1, dsh2, dsc2, dga2)], axis=1)
    dm = jnp.concatenate([dm, jnp.zeros((COND_ROWS - 2, dm.shape[1]), F32)], axis=0)
    crow = Rows(1, 0, COND_ROWS)
    dmraw, g["b_ada"] = stage_bwd(n("ada_bias"), f_bias, crow, [crow.row(s["mraw"], dm.shape[1]), crow.vec(w["b_ada"])],
                                  [crow.row(dm, dm.shape[1])], [ACT_DTYPE])
    dcs = matmul_nt(n("ada_dx"), dmraw, w["w_ada"])
    g["w_ada"] = matmul_tn(n("ada_dw"), cond_s, dmraw)
    return dx, dcs, g


def local_step(x, ctx, c, c_ctx, target, layer_w_fn, n_layers, g_final, fwd_hosts=None, bwd_hosts=None):
    L, D = x.shape
    n_ctx = ctx.shape[0]
    tm = ROW_TILE
    T = L + n_ctx
    rows = Rows(T // tm, n_ctx // tm, tm)
    pc = _pool_consts(tm, n_ctx)
    xa = jnp.concatenate([ctx, x], axis=0)
    cond = jnp.concatenate([c_ctx.reshape(1, D), c.reshape(1, D), jnp.zeros((COND_ROWS - 2, D), F32)], axis=0)
    crow = Rows(1, 0, COND_ROWS)
    (cond_s,) = stage_fwd("cond_silu", f_silu, crow, [crow.row(cond, D)], [(D, ACT_DTYPE, False)])

    saved, layer_w = [], []
    for l in range(n_layers):
        layer_w.append(layer_w_fn(l))
        box = {}
        xa, s = _layer_fwd(l, xa, cond_s, layer_w[l], rows, n_ctx, pc, fwd_hosts(l, box) if fwd_hosts else None, box)
        saved.append(s)

    rl = Rows(L // tm, 0, tm)
    gf = g_final.reshape(1, D)
    tgt = rl.row(target, D)
    tgt.kind = "const"
    loss_args = [rl.row(xa, D, roff=n_ctx // tm), tgt, rl.vec(gf)]
    ones = jnp.ones((L, 1), F32)
    dx_lat, dgf, loss_rows = stage_bwd("loss", f_loss, rl, loss_args, [rl.row(ones, 1)], [F32], primal=[(1, F32)])
    loss = jnp.sum(loss_rows)
    dx = jnp.concatenate([jnp.zeros((n_ctx, D), F32), dx_lat], axis=0)

    grads = [None] * n_layers
    dcs = jnp.zeros((COND_ROWS, D), F32)
    for l in reversed(range(n_layers)):
        box = {}
        hosts = bwd_hosts(l, grads, box) if bwd_hosts else None
        dx, dcs_l, grads[l] = _layer_bwd(l, dx, cond_s, layer_w[l], saved[l], rows, n_ctx, pc, hosts, box)
        dcs = dcs + dcs_l
    (dcond,) = stage_bwd("cond_silu_bwd", f_silu, crow, [crow.row(cond, D)], [crow.row(dcs, D)], [F32])
    return loss, dx[n_ctx:], grads, dcond[0], dgf


def gather_chips(halves, conv=None):
    n = len(halves)
    ops = list(halves) + ([conv] if conv is not None else [])

    def copies(ins, outs, pos):
        c, me = pos[2], _chip_index(pos)
        pairs = [(s.at[c], o.at[me, c]) for s, o in zip(ins[:n], outs[:n])]
        pairs += [(s, o.at[me]) for s, o in zip(ins[n:], outs[n:])]
        return pairs, [(s, d, _flip(pos, rel)) for rel in PLANE for s, d in pairs]

    shapes = [jax.ShapeDtypeStruct((4,) + s.shape, s.dtype) for s in ops]
    return Exchange(copies, 3 * len(ops), len(ops), ops, shapes)


def gather_pair(gathered):
    n = len(gathered)

    def copies(ins, outs, pos):
        c = pos[2]
        return [], [(s.at[b, c], o.at[b, c], _flip(pos, PAIR[0])) for s, o in zip(ins, outs) for b in range(4)]

    shapes = [jax.ShapeDtypeStruct(g.shape, g.dtype) for g in gathered]
    return Exchange(copies, 4 * n, 0, gathered, shapes, aliases={k: k for k in range(n)})


def swap_halves(grads):
    n = len(grads)

    def copies(ins, outs, pos):
        c = pos[2]
        return [], [(g.at[b, 1 - c], o.at[b], _flip(pos, PAIR[0])) for g, o in zip(ins, outs) for b in range(4)]

    shapes = [jax.ShapeDtypeStruct((g.shape[0],) + g.shape[2:], g.dtype) for g in grads]
    return Exchange(copies, 4 * n, 0, grads, shapes)


def scatter_chips(sums):
    n = len(sums)

    def copies(ins, outs, pos):
        me = _chip_index(pos)
        local = [(p.at[me], o.at[me]) for p, o in zip(ins, outs)]
        remote = []
        for rel in PLANE:
            peer = _flip(pos, rel)
            remote += [(p.at[_chip_index(peer)], o.at[me], peer) for p, o in zip(ins, outs)]
        return local, remote

    shapes = [jax.ShapeDtypeStruct(p.shape, p.dtype) for p in sums]
    return Exchange(copies, 3 * n, n, sums, shapes)


def share_halves(finals):
    n = len(finals)

    def copies(ins, outs, pos):
        c = pos[2]
        return [], [(f.at[c], o.at[c], _flip(pos, PAIR[0])) for f, o in zip(ins, outs)]

    shapes = [jax.ShapeDtypeStruct(f.shape, f.dtype) for f in finals]
    return Exchange(copies, n, 0, finals, shapes, aliases={k: k for k in range(n)})


def gather_everyone(vec):
    def copies(ins, outs, pos):
        me = _device_index(pos)
        (v,), (o,) = ins, outs
        return [(v, o.at[me])], [(v, o.at[me], _flip(pos, rel)) for rel in EVERYONE]

    return Exchange(copies, len(EVERYONE), 1, [vec], [jax.ShapeDtypeStruct((8,) + vec.shape, vec.dtype)])


def _row_tile(rows, cols, n_bufs, mult=8):
    cap = VMEM_LIMIT_BYTES // 2 // (2 * n_bufs * cols * 4)
    for t in range(min(rows, cap) // mult * mult, 0, -mult):
        if rows % t == 0:
            return t
    return rows


def _adamw_update(w, g, m, v):
    nm = ADAM_B1 * m + (1.0 - ADAM_B1) * g
    nv = ADAM_B2 * v + (1.0 - ADAM_B2) * jnp.square(g)
    m_hat = nm / (1.0 - ADAM_B1 ** ADAM_STEP)
    v_hat = nv / (1.0 - ADAM_B2 ** ADAM_STEP)
    return -ADAM_LR * (m_hat / (jnp.sqrt(v_hat) + ADAM_EPS) + ADAM_WD * w), nm, nv


def adamw_small(name, ws, gs, ms, vs):
    n = len(ws)

    def body(*refs):
        ins, outs = refs[:4 * n], refs[4 * n:]
        for k in range(n):
            d, nm, nv = _adamw_update(ins[k][...], ins[n + k][...], ins[2 * n + k][...], ins[3 * n + k][...])
            outs[k][...] = d
            outs[n + k][...] = nm
            outs[2 * n + k][...] = nv

    shapes = [jax.ShapeDtypeStruct(a.shape, F32) for a in ws]
    vmem = pl.BlockSpec(memory_space=pltpu.VMEM)
    res = _pcall(body, name=name, out_shape=shapes * 3, in_specs=[vmem] * (4 * n), out_specs=[vmem] * (3 * n),
                 compiler_params=pltpu.CompilerParams(vmem_limit_bytes=VMEM_LIMIT_BYTES))(*ws, *gs, *ms, *vs)
    return res[:n], res[n:2 * n], res[2 * n:]


WIRE_DTYPE = jnp.bfloat16


def add_own_half(name, grads, recv, c):
    nb, _, R, C = grads.shape
    tr = _row_tile(R, C, 3, mult=16)

    def body(c_ref, g_ref, r_ref, o_ref):
        o_ref[...] = (g_ref[...] + r_ref[...]).astype(o_ref.dtype)

    spec = pl.BlockSpec((None, tr, C), lambda b, i, c_ref: (b, i, 0))
    return _pcall(
        body, name=name, out_shape=jax.ShapeDtypeStruct(recv.shape, WIRE_DTYPE),
        grid_spec=pltpu.PrefetchScalarGridSpec(
            num_scalar_prefetch=1, grid=(nb, R // tr),
            in_specs=[pl.BlockSpec((None, None, tr, C), lambda b, i, c_ref: (b, c_ref[0], i, 0)), spec],
            out_specs=spec),
        compiler_params=_params("parallel", "parallel"),
    )(c, grads, recv)


def sum_slots(name, a, c=None):
    n, R, C = a.shape
    tr = _row_tile(R, C, n + 1, mult=16 if a.dtype.itemsize == 2 else 8)

    def body(*refs):
        a_ref, o_ref = refs[-2:]
        acc = a_ref[0].astype(F32)
        for k in range(1, n):
            acc = acc + a_ref[k].astype(F32)
        o_ref[...] = acc

    if c is None:
        return _pcall(
            body, name=name, out_shape=jax.ShapeDtypeStruct((R, C), F32), grid=(R // tr,),
            in_specs=[pl.BlockSpec((n, tr, C), lambda i: (0, i, 0))], out_specs=pl.BlockSpec((tr, C), lambda i: (i, 0)),
            compiler_params=_params("parallel"),
        )(a)
    return _pcall(
        body, name=name, out_shape=jax.ShapeDtypeStruct((2, R, C), F32),
        grid_spec=pltpu.PrefetchScalarGridSpec(
            num_scalar_prefetch=1, grid=(R // tr,),
            in_specs=[pl.BlockSpec((n, tr, C), lambda i, c_ref: (0, i, 0))],
            out_specs=pl.BlockSpec((None, tr, C), lambda i, c_ref: (c_ref[0], i, 0))),
        compiler_params=_params("parallel"),
    )(c, a)


def adamw(name, w, g_layers, m, v):
    nl, R, C = w.shape
    assert len(g_layers) == nl
    tr = _row_tile(R, C, 8 + nl)
    nr = R // tr

    def body(*refs):
        w_ref, m_ref, v_ref = refs[:3]
        g_refs = refs[3:3 + nl]
        go_ref, d_ref, nm_ref, nv_ref = refs[3 + nl:]
        l = pl.program_id(0)
        gr = g_refs[0][...]
        for k in range(1, nl):
            gr = jnp.where(l == k, g_refs[k][...], gr)
        d_ref[...], nm_ref[...], nv_ref[...] = _adamw_update(w_ref[...], gr, m_ref[...], v_ref[...])
        go_ref[...] = gr

    spec = pl.BlockSpec((None, tr, C), lambda l, i: (l, i, 0))
    g_specs = [pl.BlockSpec((tr, C), (lambda l, i, k=k: (jnp.where(l == k, i, jnp.where(l < k, 0, nr - 1)), 0)))
               for k in range(nl)]
    return _pcall(
        body, name=name, out_shape=[jax.ShapeDtypeStruct((nl, R, C), F32)] * 4, grid=(nl, nr),
        in_specs=[spec] * 3 + g_specs, out_specs=[spec] * 4, compiler_params=_params("arbitrary", "arbitrary"),
    )(w, m, v, *g_layers)


BIG = ("w_ada", "w_in", "w_ssd_out", "pool_w", "w_pool_out", "w_out", "w_gate_up", "w_down")
COL_SHARDED = ("w_ada", "w_in", "w_gate_up")
GRAD_TRANSPOSED = ("w_in",)
FIRST_USED = ("w_ada", "w_in")
LATER_USED = tuple(k for k in BIG if k not in FIRST_USED)
READY_LAST = FIRST_USED
READY_EARLY = LATER_USED
SMALL = ("c_ctx", "b_ada", "g_mix", "conv_w", "conv_b", "dt_bias", "a_log", "d_skip", "ssd_norm_w", "pool_scale",
         "g_ffn", "g_final")
WEIGHTS = ("c_ctx", "w_ada", "b_ada", "g_mix", "w_in", "conv_w", "conv_b", "dt_bias", "a_log", "d_skip", "ssd_norm_w",
           "w_ssd_out", "pool_w", "pool_scale", "w_pool_out", "w_out", "g_ffn", "w_gate_up", "w_down", "g_final")
LAYER_KEYS = ("w_ada", "b_ada", "g_mix", "w_in", "conv_w", "conv_b", "dt_bias", "a_log", "d_skip", "ssd_norm_w",
              "w_ssd_out", "pool_w", "pool_scale", "w_pool_out", "w_out", "g_ffn", "w_gate_up", "w_down")


def _shard2d(name, a):
    if name == "pool_w":
        return a.reshape(a.shape[0], a.shape[1] * a.shape[2], a.shape[3])
    return a


def _full_from_blocks(name, a):
    nb, R, C = a.shape
    if name in COL_SHARDED:
        return jnp.transpose(a, (1, 0, 2)).reshape(R, nb * C)
    if name == "pool_w":
        nw = len(POOL_WINDOWS)
        return jnp.transpose(a.reshape(nb, nw, R // nw, C), (1, 0, 2, 3)).reshape(nw, nb * R // nw, C)
    return a.reshape(nb * R, C)


def _blocks_from_full(name, g):
    nb = 4
    if name in COL_SHARDED and name not in GRAD_TRANSPOSED:
        K, N = g.shape
        return jnp.transpose(g.reshape(K, nb, N // nb), (1, 0, 2))
    if name == "pool_w":
        nw, r, C = g.shape
        return jnp.transpose(g.reshape(nw, nb, r // nb, C), (1, 0, 2, 3)).reshape(nb, nw * r // nb, C)
    return g.reshape(nb, g.shape[0] // nb, g.shape[1])


def _pack(arrs, rows):
    flat = jnp.concatenate([a.reshape(-1).astype(F32) for a in arrs])
    return jnp.concatenate([flat, jnp.zeros((rows * 128 - flat.size,), F32)]).reshape(rows, 128)


def _unpack(vec, shapes):
    flat = vec.reshape(-1)
    out, o = [], 0
    for s in shapes:
        n = int(np.prod(s))
        out.append(flat[o:o + n].reshape(s))
        o += n
    return out


def _rows_for(shapes):
    n = sum(int(np.prod(s)) for s in shapes)
    return -(-n // (8 * 128)) * 8


def kernel(x, c, ctx, c_ctx, w_ada, b_ada, g_mix, w_in, conv_w, conv_b, dt_bias, a_log, d_skip, ssd_norm_w, w_ssd_out, pool_w, pool_scale, w_pool_out, w_out, g_ffn, w_gate_up, w_down, g_final, loss_target, m_c_ctx, m_w_ada, m_b_ada, m_g_mix, m_w_in, m_conv_w, m_conv_b, m_dt_bias, m_a_log, m_d_skip, m_ssd_norm_w, m_w_ssd_out, m_pool_w, m_pool_scale, m_w_pool_out, m_w_out, m_g_ffn, m_w_gate_up, m_w_down, m_g_final, v_c_ctx, v_w_ada, v_b_ada, v_g_mix, v_w_in, v_conv_w, v_conv_b, v_dt_bias, v_a_log, v_d_skip, v_ssd_norm_w, v_w_ssd_out, v_pool_w, v_pool_scale, v_w_pool_out, v_w_out, v_g_ffn, v_w_gate_up, v_w_down, v_g_final):
    w = dict(c_ctx=c_ctx, w_ada=w_ada, b_ada=b_ada, g_mix=g_mix, w_in=w_in, conv_w=conv_w, conv_b=conv_b, dt_bias=dt_bias,
             a_log=a_log, d_skip=d_skip, ssd_norm_w=ssd_norm_w, w_ssd_out=w_ssd_out, pool_w=pool_w, pool_scale=pool_scale,
             w_pool_out=w_pool_out, w_out=w_out, g_ffn=g_ffn, w_gate_up=w_gate_up, w_down=w_down, g_final=g_final)
    m = dict(c_ctx=m_c_ctx, w_ada=m_w_ada, b_ada=m_b_ada, g_mix=m_g_mix, w_in=m_w_in, conv_w=m_conv_w, conv_b=m_conv_b,
             dt_bias=m_dt_bias, a_log=m_a_log, d_skip=m_d_skip, ssd_norm_w=m_ssd_norm_w, w_ssd_out=m_w_ssd_out,
             pool_w=m_pool_w, pool_scale=m_pool_scale, w_pool_out=m_w_pool_out, w_out=m_w_out, g_ffn=m_g_ffn,
             w_gate_up=m_w_gate_up, w_down=m_w_down, g_final=m_g_final)
    v = dict(c_ctx=v_c_ctx, w_ada=v_w_ada, b_ada=v_b_ada, g_mix=v_g_mix, w_in=v_w_in, conv_w=v_conv_w, conv_b=v_conv_b,
             dt_bias=v_dt_bias, a_log=v_a_log, d_skip=v_d_skip, ssd_norm_w=v_ssd_norm_w, w_ssd_out=v_w_ssd_out,
             pool_w=v_pool_w, pool_scale=v_pool_scale, w_pool_out=v_w_pool_out, w_out=v_w_out, g_ffn=v_g_ffn,
             w_gate_up=v_w_gate_up, w_down=v_w_down, g_final=v_g_final)
    assert x.shape[0] == 1, "one example per device"
    pos = _position()
    core = pos[2].astype(jnp.int32).reshape(1)
    n_layers = w_in.shape[0]
    assert n_layers == 2
    dims = (ssd_norm_w.shape[1], conv_w.shape[2] * 4, dt_bias[0].size, pool_scale.shape[1])
    shard = {k: _shard2d(k, w[k]) for k in BIG}

    def halves(a):
        return a.reshape(a.shape[:-2] + (2, a.shape[-2] // 2, a.shape[-1]))

    def whole(a):
        return a.reshape(a.shape[:-3] + (2 * a.shape[-2], a.shape[-1]))

    def wire_shards(l, names):
        return [halves(shard[k][l].astype(MXU_DTYPE)) for k in names]

    def full_weights(names, gathered):
        return {k: _full_from_blocks(k, whole(a)) for k, a in zip(names, gathered)}

    first = comm_call("gather0_chips", gather_chips(wire_shards(0, FIRST_USED), conv=conv_w))
    got0 = full_weights(FIRST_USED, comm_call("gather0_pair", gather_pair(first[:-1])))
    conv_all = first[-1]
    conv_full = [jnp.transpose(conv_all[:, l], (1, 0, 2)).reshape(conv_all.shape[2], -1) for l in range(n_layers)]

    boxes = {}

    def layer_w_fn(l):
        if l == 0:
            full = dict(got0)
            late = {k: None for k in LATER_USED}
        else:
            full = full_weights(BIG, boxes[("fwd", 0)]["gate_up_mm"])
            late = {}
        full["conv_w"] = conv_full[l]
        lw = LazyDict(_prep_layer_weights(*[full[k] if k in full else (None if k in late else w[k][l]) for k in LAYER_KEYS]))
        for i, k in enumerate(late):
            lw[k] = (lambda i=i, k=k: _full_from_blocks(k, whole(boxes[("fwd", 0)]["conv"][i])))
        return lw

    def fwd_hosts(l, box):
        boxes[("fwd", l)] = box
        if l != 0:
            return None
        return {"in_mm": lambda box: gather_chips(wire_shards(0, LATER_USED)), "conv": lambda box: gather_pair(box["in_mm"]),
                "ssd": lambda box: gather_chips(wire_shards(1, BIG)), "gate_up_mm": lambda box: gather_pair(box["ssd"])}

    def blocks(gl, names):
        return [halves(_blocks_from_full(k, gl[k])) for k in names]

    def pair_sums(tag, names, G, recv):
        return [add_own_half(f"pair_sum{tag}_{k}", g, r, core) for k, g, r in zip(names, G, recv)]

    def chip_sums(tag, names, parts):
        return [sum_slots(f"chip_sum{tag}_{k}", p, core) for k, p in zip(names, parts)]

    def reduce_now(tag, gl, names):
        G = blocks(gl, names)
        pair = pair_sums(tag, names, G, comm_call(f"swap{tag}", swap_halves(G)))
        fin = chip_sums(tag, names, comm_call(f"scatter{tag}", scatter_chips(pair)))
        return [whole(a) for a in comm_call(f"share{tag}", share_halves(fin))]

    small_layers = {}
    n_big = len(BIG)

    def bwd_hosts(l, grads, box):
        boxes[("bwd", l)] = box
        if l != 0:
            return None
        gl1 = _unprep_layer_grads(grads[1], dims)
        small_layers[1] = gl1
        G1 = blocks(gl1, BIG)
        early = {}

        def gate_host(box):
            early["G"] = blocks(box["g"], READY_EARLY)
            return swap_halves(early["G"])

        def scan_host(box):
            return combine(scatter_chips(pair_sums("1", BIG, G1, box["down_dx"])),
                           scatter_chips(pair_sums("0e", READY_EARLY, early["G"], box["ssd_gate"])))

        def conv_host(box):
            return combine(share_halves(chip_sums("1", BIG, box["ssd"][:n_big])),
                           share_halves(chip_sums("0e", READY_EARLY, box["ssd"][n_big:])))

        return {"down_dx": lambda box: swap_halves(G1), "ssd_gate": gate_host, "ssd": scan_host, "in_dx": conv_host}

    loss, grad_x, grads, d_c_ctx, d_g_final = local_step(
        x[0], ctx[0], c[0], c_ctx, loss_target[0], layer_w_fn, n_layers, g_final, fwd_hosts, bwd_hosts)
    shared =[whole(a) for a in boxes[("bwd", 0)]["in_dx"]]
    reduced1 = shared[:n_big]
    gl0 = _unprep_layer_grads(grads[0], dims)
    small_layers[0] = gl0
    red0 = dict(zip(READY_EARLY, shared[n_big:]))
    red0.update(zip(READY_LAST, reduce_now("0", gl0, READY_LAST)))
    reduced0 = [red0[k] for k in BIG]

    small_full = dict(c_ctx=d_c_ctx, g_final=d_g_final.reshape(-1))
    for k in SMALL:
        if k not in small_full:
            small_full[k] = jnp.stack([small_layers[l][k] for l in range(n_layers)])
    shapes = [small_full[k].shape for k in SMALL] + [(1,)]
    packed = _pack([small_full[k] for k in SMALL] + [loss.reshape(1)], _rows_for(shapes))
    total = sum_slots("small_sum", comm_call("gather_small", gather_everyone(packed))[0])
    *small_vals, loss = _unpack(total, shapes)
    loss = loss.reshape(())
    small_g = dict(zip(SMALL, small_vals))
    cw = conv_w.shape[2]
    small_g["conv_w"] = lax.dynamic_slice_in_dim(small_g["conv_w"], _chip_index(pos) * cw, cw, axis=2)

    grad, delta, new_m, new_v = {}, {}, {}, {}
    for k, g0, g1 in zip(BIG, reduced0, reduced1):
        shp = w[k].shape
        if k in GRAD_TRANSPOSED:
            flat = lambda a: jnp.swapaxes(a, 1, 2)
            back = lambda a: jnp.swapaxes(a, 1, 2)
        else:
            flat = lambda a: _shard2d(k, a)
            back = lambda a: a.reshape(shp)
        outs = adamw(f"adamw_{k}", flat(w[k]), [g0, g1], flat(m[k]), flat(v[k]))
        grad[k], delta[k], new_m[k], new_v[k] = [back(a) for a in outs]
    flat2 = lambda d: [d[k].reshape(-1, d[k].shape[-1]) for k in SMALL]
    d_, m_, v_ = adamw_small("adamw_small", flat2(w), flat2(small_g), flat2(m), flat2(v))
    for k, dd, mm, vv in zip(SMALL, d_, m_, v_):
        shp = w[k].shape
        grad[k], delta[k], new_m[k], new_v[k] = small_g[k], dd.reshape(shp), mm.reshape(shp), vv.reshape(shp)

    return (loss, grad_x[None], *[grad[k] for k in WEIGHTS], *[delta[k] for k in WEIGHTS],
            *[new_m[k] for k in WEIGHTS], *[new_v[k] for k in WEIGHTS])
```

```python
import functools

import jax
import jax.numpy as jnp
import numpy as np
from jax import lax
from jax.experimental import pallas as pl
from jax.experimental.pallas import tpu as pltpu

F32 = jnp.float32
MXU_DTYPE = jnp.bfloat16
ACT_DTYPE = jnp.bfloat16
VMEM_LIMIT_BYTES = 48 * 1024 * 1024
EPS = 1e-6
NEG = -1e30

SSD_HEADDIM = 64
SSD_GROUPS = 8
SSD_STATE = 128
SSD_CHUNK = 128
SSD_GROUPS_PER_STEP = 8
SSD_CONV = 5
GRID_W = 64
POOL_WINDOWS = (2, 4, 8, 16)
ROW_TILE = 256
DT_PAD = 512

ADAM_LR = 0.001
ADAM_B1 = 0.9
ADAM_B2 = 0.999
ADAM_EPS = 1e-08
ADAM_WD = 0.01
ADAM_STEP = 10

MESH = pl.DeviceIdType.MESH


def _pcall(body, **kw):
    return pl.pallas_call(body, **kw)


def _params(*sem):
    return pltpu.CompilerParams(dimension_semantics=tuple(sem), vmem_limit_bytes=VMEM_LIMIT_BYTES)


def _pick_tile(n, cands):
    for t in cands:
        if n % t == 0:
            return t
    return n


PLANE = ((1, 0, 0), (0, 1, 0), (1, 1, 0))
PAIR = ((0, 0, 1),)
EVERYONE = tuple((a, b, d) for a in (0, 1) for b in (0, 1) for d in (0, 1) if a + b + d)
HBM = pl.BlockSpec(memory_space=pl.ANY)


def _position():
    return lax.axis_index("x"), lax.axis_index("y"), lax.axis_index("c")


def _flip(pos, rel):
    return tuple(1 - p if r else p for p, r in zip(pos, rel))


def _chip_index(pos):
    return 2 * pos[0] + pos[1]


def _device_index(pos):
    return 4 * pos[0] + 2 * pos[1] + pos[2]


class Exchange:
    def __init__(self, copies, n_remote, n_local, operands, out_shapes, aliases=None):
        self.copies, self.n_remote, self.n_local = copies, n_remote, n_local
        self.operands, self.out_shapes, self.aliases = list(operands), list(out_shapes), dict(aliases or {})

    def scratch(self):
        return [pltpu.SemaphoreType.DMA((max(self.n_remote, 1),)), pltpu.SemaphoreType.DMA((max(self.n_remote, 1),)),
                pltpu.SemaphoreType.DMA((max(self.n_local, 1),))]

    def descriptors(self, ins, outs, sems):
        send_sems, recv_sems, local_sems = sems
        local, remote = self.copies(ins, outs, _position())
        assert len(local) == self.n_local and len(remote) == self.n_remote
        cps = [pltpu.make_async_copy(src, dst, local_sems.at[k]) for k, (src, dst) in enumerate(local)]
        cps += [pltpu.make_async_remote_copy(src_ref=src, dst_ref=dst, send_sem=send_sems.at[k], recv_sem=recv_sems.at[k],
                                             device_id=peer, device_id_type=MESH) for k, (src, dst, peer) in enumerate(remote)]
        return cps


def combine(a, b):
    na, nao = len(a.operands), len(a.out_shapes)

    def copies(ins, outs, pos):
        la, ra = a.copies(ins[:na], outs[:nao], pos)
        lb, rb = b.copies(ins[na:], outs[nao:], pos)
        return la + lb, ra + rb

    aliases = dict(a.aliases)
    aliases.update({na + k: nao + v for k, v in b.aliases.items()})
    return Exchange(copies, a.n_remote + b.n_remote, a.n_local + b.n_local, a.operands + b.operands,
                    a.out_shapes + b.out_shapes, aliases)


class LazyDict(dict):
    def __getitem__(self, key):
        v = dict.__getitem__(self, key)
        if callable(v):
            v = v()
            dict.__setitem__(self, key, v)
        return v


def comm_call(name, ex):
    n_in, n_out = len(ex.operands), len(ex.out_shapes)

    def body(*refs):
        cps = ex.descriptors(refs[:n_in], refs[n_in:n_in + n_out], refs[n_in + n_out:])
        for cp in cps:
            cp.start()
        for cp in cps:
            cp.wait()

    return _pcall(
        body, name=name, out_shape=ex.out_shapes, in_specs=[HBM] * n_in, out_specs=[HBM] * n_out,
        scratch_shapes=ex.scratch(), input_output_aliases=ex.aliases,
        compiler_params=pltpu.CompilerParams(has_side_effects=True),
    )(*ex.operands)


def hosted_call(body, ex, operands, *, name, out_shape, grid, in_specs, out_specs, scratch_shapes=()):
    n_in, n_out, n_scr = len(operands), len(out_shape), len(scratch_shapes)
    sem = ("arbitrary",) * len(grid)
    if ex is None:
        res = _pcall(body, name=name, out_shape=list(out_shape), grid=grid, in_specs=list(in_specs),
                     out_specs=list(out_specs), scratch_shapes=list(scratch_shapes), compiler_params=_params(*sem))(*operands)
        return res, []
    x_in, x_out = len(ex.operands), len(ex.out_shapes)

    def wrapped(*refs):
        o = 0
        ins = refs[o:o + n_in]; o += n_in
        xins = refs[o:o + x_in]; o += x_in
        outs = refs[o:o + n_out]; o += n_out
        xouts = refs[o:o + x_out]; o += x_out
        scr = refs[o:o + n_scr]; o += n_scr
        sems = refs[o:]
        first = last = None
        for a, n in enumerate(grid):
            i = pl.program_id(a)
            first = (i == 0) if first is None else first & (i == 0)
            last = (i == n - 1) if last is None else last & (i == n - 1)

        @pl.when(first)
        def _():
            for cp in ex.descriptors(xins, xouts, sems):
                cp.start()

        body(*ins, *outs, *scr)

        @pl.when(last)
        def _():
            for cp in ex.descriptors(xins, xouts, sems):
                cp.wait()

    aliases = {n_in + k: n_out + v for k, v in ex.aliases.items()}
    res = _pcall(
        wrapped, name=name, out_shape=list(out_shape) + ex.out_shapes, grid=grid,
        in_specs=list(in_specs) + [HBM] * x_in, out_specs=list(out_specs) + [HBM] * x_out,
        scratch_shapes=list(scratch_shapes) + ex.scratch(), input_output_aliases=aliases,
        compiler_params=pltpu.CompilerParams(dimension_semantics=sem, vmem_limit_bytes=VMEM_LIMIT_BYTES,
                                             has_side_effects=True),
    )(*operands, *ex.operands)
    return res[:n_out], res[n_out:]


def _dot(a, b, dims):
    return lax.dot_general(a.astype(MXU_DTYPE), b.astype(MXU_DTYPE), (dims, ((), ())), preferred_element_type=F32)


_NN = ((1,), (0,))
_NT = ((1,), (1,))
_TN = ((0,), (0,))


@jax.custom_vjp
def _mm(a, b):
    return _dot(a, b, _NN)


def _mm_fwd(a, b):
    return _mm(a, b), (a, b)


def _mm_bwd(res, g):
    a, b = res
    return _dot(g, b, _NT).astype(a.dtype), _dot(a, g, _TN).astype(b.dtype)


_mm.defvjp(_mm_fwd, _mm_bwd)


@jax.custom_vjp
def _mm_nt(a, b):
    return _dot(a, b, _NT)


def _mm_nt_fwd(a, b):
    return _mm_nt(a, b), (a, b)


def _mm_nt_bwd(res, g):
    a, b = res
    return _dot(g, b, _NN).astype(a.dtype), _dot(g, a, _TN).astype(b.dtype)


_mm_nt.defvjp(_mm_nt_fwd, _mm_nt_bwd)


@jax.custom_vjp
def _mm_tn(a, b):
    return _dot(a, b, _TN)


def _mm_tn_fwd(a, b):
    return _mm_tn(a, b), (a, b)


def _mm_tn_bwd(res, g):
    a, b = res
    return _dot(b, g, _NT).astype(a.dtype), _dot(a, g, _NN).astype(b.dtype)


_mm_tn.defvjp(_mm_tn_fwd, _mm_tn_bwd)


def _dot_exact(m01, v):
    m = m01.astype(jnp.bfloat16)
    hi = v.astype(jnp.bfloat16)
    r1 = v - hi.astype(F32)
    mid = r1.astype(jnp.bfloat16)
    lo = (r1 - mid.astype(F32)).astype(jnp.bfloat16)
    out = jnp.dot(m, hi, preferred_element_type=F32)
    out = out + jnp.dot(m, mid, preferred_element_type=F32)
    return out + jnp.dot(m, lo, preferred_element_type=F32)


@jax.custom_vjp
def _lin01(m, mt, v):
    return _dot_exact(m, v)


def _lin01_fwd(m, mt, v):
    return _dot_exact(m, v), (m, mt)


def _lin01_bwd(res, g):
    m, mt = res
    return jnp.zeros_like(m), jnp.zeros_like(mt), _dot_exact(mt, g)


_lin01.defvjp(_lin01_fwd, _lin01_bwd)


MATMUL_VMEM_BUDGET = VMEM_LIMIT_BYTES * 3 // 4


def _mm_tiles(m, n, k_bytes_a, k_bytes_b, out_bytes, cands_m, cands_n):
    best = None
    for tm in cands_m:
        if m % tm:
            continue
        for tn in cands_n:
            if n % tn:
                continue
            need = 2 * (tm * k_bytes_a + tn * k_bytes_b + tm * tn * out_bytes)
            if need <= MATMUL_VMEM_BUDGET and (best is None or tm * tn > best[0] * best[1]):
                best = (tm, tn)
    assert best is not None, (m, n)
    return best


_ROW_CANDS = (4352, 2176, 1088, 768, 544, 512, 272, 256, 128, 16)
_COL_CANDS = (2816, 2048, 1408, 1024, 512, 256, 128)


def _one(res, xres, ex):
    return res[0] if ex is None else (res[0], xres)


def matmul_nn(name, a, b, out_dtype=F32, ex=None, col0=0, ncols=None):
    M, K = a.shape
    N = b.shape[1] - col0 if ncols is None else ncols
    tm, tn = _mm_tiles(M, N, K * a.dtype.itemsize, K * b.dtype.itemsize, jnp.dtype(out_dtype).itemsize,
                       _ROW_CANDS, (512, 256, 128))
    assert col0 % tn == 0
    first = col0 // tn

    def body(a_ref, b_ref, o_ref):
        o_ref[...] = _dot(a_ref[...], b_ref[...], _NN).astype(o_ref.dtype)

    res, xres = hosted_call(
        body, ex, [a, b], name=name, out_shape=[jax.ShapeDtypeStruct((M, N), out_dtype)], grid=(N // tn, M // tm),
        in_specs=[pl.BlockSpec((tm, K), lambda j, i: (i, 0)), pl.BlockSpec((K, tn), lambda j, i: (0, first + j))],
        out_specs=[pl.BlockSpec((tm, tn), lambda j, i: (i, j))])
    return _one(res, xres, ex)


def matmul_nt(name, g, b, out_dtype=F32, ex=None, offsets=None):
    pieces = list(g) if isinstance(g, (list, tuple)) else [g]
    offsets = list(offsets) if offsets is not None else [0]
    M = pieces[0].shape[0]
    K, N = b.shape
    g_bytes = sum(p.shape[1] * p.dtype.itemsize for p in pieces)
    tm, tk = _mm_tiles(M, K, g_bytes, N * b.dtype.itemsize, jnp.dtype(out_dtype).itemsize, _ROW_CANDS, _COL_CANDS)

    def body(*refs):
        b_ref, o_ref = refs[-2:]
        acc = None
        for g_ref, off in zip(refs[:-2], offsets):
            part = _dot(g_ref[...], b_ref[:, off:off + g_ref.shape[1]], _NT)
            acc = part if acc is None else acc + part
        o_ref[...] = acc.astype(o_ref.dtype)

    res, xres = hosted_call(
        body, ex, pieces + [b], name=name, out_shape=[jax.ShapeDtypeStruct((M, K), out_dtype)], grid=(K // tk, M // tm),
        in_specs=[pl.BlockSpec((tm, p.shape[1]), lambda j, i: (i, 0)) for p in pieces]
        + [pl.BlockSpec((tk, N), lambda j, i: (j, 0))],
        out_specs=[pl.BlockSpec((tm, tk), lambda j, i: (i, j))])
    return _one(res, xres, ex)


def matmul_tn(name, a, g, ex=None):
    M, K = a.shape
    N = g.shape[1]
    tk, tn = _mm_tiles(K, N, M * a.dtype.itemsize, M * g.dtype.itemsize, 4, (512, 256, 128), (512, 256, 128))

    def body(a_ref, g_ref, o_ref):
        o_ref[...] = _dot(a_ref[...], g_ref[...], _TN)

    res, xres = hosted_call(
        body, ex, [a, g], name=name, out_shape=[jax.ShapeDtypeStruct((K, N), F32)], grid=(K // tk, N // tn),
        in_specs=[pl.BlockSpec((M, tk), lambda i, j: (0, i)), pl.BlockSpec((M, tn), lambda i, j: (0, j))],
        out_specs=[pl.BlockSpec((tk, tn), lambda i, j: (i, j))])
    return _one(res, xres, ex)


class Arg:
    def __init__(self, arr, block, imap, kind):
        self.arr, self.block, self.imap, self.kind = arr, block, imap, kind


class Rows:
    def __init__(self, nt, nct, tm, ncol=1):
        self.nt, self.nct, self.tm, self.ncol = nt, nct, tm, ncol

    def seg(self, i):
        return jnp.where(i >= self.nct, 1, 0)

    def spec(self, block, imap):
        return pl.BlockSpec(block, lambda j, i: imap(j, i, self.seg(i)))

    def row(self, arr, width, cb0=0, follow=False, roff=0, stride=1):
        f = stride if follow else 0
        return Arg(arr, (self.tm, width), lambda j, i, s: (i + roff, cb0 + f * j), "row")

    def vec(self, arr, follow=False, kind="acc"):
        w = arr.shape[1] // (self.ncol if follow else 1)
        f = 1 if follow else 0
        return Arg(arr, (1, w), lambda j, i, s: (0, f * j), kind)

    def segvec(self, arr, kind="seg"):
        return Arg(arr, (None, 1, arr.shape[2]), lambda j, i, s: (s, 0, 0), kind)


def _load(ref):
    return ref[...].astype(F32) if ref.dtype != F32 else ref[...]


def stage_fwd(name, f, rows, args, outs):
    n_in = len(args)

    def body(*refs):
        vals = [_load(r) for r in refs[:n_in]]
        res = f(*vals)
        for r, v in zip(refs[n_in:], res):
            r[...] = v.astype(r.dtype)

    T = rows.nt * rows.tm
    out_shape = [jax.ShapeDtypeStruct((T, w * (rows.ncol if fo else 1)), dt) for w, dt, fo in outs]
    out_specs = [pl.BlockSpec((rows.tm, w), (lambda j, i, fo=fo: (i, j if fo else 0))) for w, dt, fo in outs]
    res = _pcall(
        body, name=name, out_shape=out_shape, grid=(rows.ncol, rows.nt),
        in_specs=[rows.spec(a.block, a.imap) for a in args], out_specs=out_specs,
        compiler_params=_params("parallel", "parallel"),
    )(*[a.arr for a in args])
    return res


def stage_bwd(name, f, rows, args, cots, row_dtypes, ex=None, primal=()):
    n_in, n_ct = len(args), len(cots)
    diff = [k for k, a in enumerate(args) if a.kind != "const"]
    row_dt = {}
    for k in diff:
        if args[k].kind == "row":
            row_dt[k] = row_dtypes[len(row_dt)]

    def body(*refs):
        i = pl.program_id(1)
        vals = [_load(r) for r in refs[:n_in]]
        cts = tuple(_load(r) for r in refs[n_in:n_in + n_ct])
        outs = refs[n_in + n_ct:]

        def g(*dv):
            full = list(vals)
            for k, v in zip(diff, dv):
                full[k] = v
            return tuple(f(*full))

        prim, vjp = jax.vjp(g, *[vals[k] for k in diff])
        grads = vjp(cts)
        for o, v in zip(outs[len(diff):], prim):
            o[...] = v.astype(o.dtype)
        for k, o, gr in zip(diff, outs, grads):
            kind = args[k].kind
            if kind == "row":
                o[...] = gr.astype(o.dtype)
            else:
                first = (i == 0) | (i == rows.nct) if kind == "seg" else (i == 0)

                @pl.when(first)
                def _():
                    o[...] = gr.astype(o.dtype)

                @pl.when(jnp.logical_not(first))
                def _():
                    o[...] += gr.astype(o.dtype)

    T = rows.nt * rows.tm
    out_shape, out_specs = [], []
    for k in diff:
        a = args[k]
        if a.kind == "row":
            out_shape.append(jax.ShapeDtypeStruct((T, a.block[1] * (rows.ncol if _follows(a) else 1)), row_dt[k]))
            fo = _follows(a)
            out_specs.append(pl.BlockSpec(a.block, (lambda j, i, fo=fo: (i, j if fo else 0))))
        else:
            out_shape.append(jax.ShapeDtypeStruct(a.arr.shape, F32))
            out_specs.append(rows.spec(a.block, a.imap))
    for w, dt in primal:
        out_shape.append(jax.ShapeDtypeStruct((T, w), dt))
        out_specs.append(pl.BlockSpec((rows.tm, w), lambda j, i: (i, 0)))
    res, xres = hosted_call(
        body, ex, [a.arr for a in list(args) + list(cots)], name=name, out_shape=out_shape, grid=(rows.ncol, rows.nt),
        in_specs=[rows.spec(a.block, a.imap) for a in list(args) + list(cots)], out_specs=out_specs)
    return res if ex is None else (res, xres)


def _follows(a):
    return a.imap(1, 0, 0)[-1] != a.imap(0, 0, 0)[-1]


def _rms(x):
    return x * lax.rsqrt(jnp.mean(x * x, axis=-1, keepdims=True) + EPS)


def f_norm_mod(x, g, sh, sc):
    return ((_rms(x) * g) * (1.0 + sc) + sh,)


def f_resid_norm_mod(x, mo, ga, g, sh, sc):
    x1 = x + ga * mo
    return x1, (_rms(x1) * g) * (1.0 + sc) + sh


def f_resid(x, dn, ga):
    return (x + ga * dn,)


def f_silu(x):
    return (x * jax.nn.sigmoid(x),)


def f_bias(x, b):
    return (x + b,)


def f_ssd_gate(y0, y1, xs, z, dskip, nw):
    y = y0 + y1 + dskip * xs
    return (_rms(y * (z * jax.nn.sigmoid(z))) * nw,)


def f_pool(u, pmat, pmat_t, inv_cnt, pw, scale):
    pm = _lin01(pmat, pmat_t, u) * inv_cnt - u
    return (_mm(pm, pw) * scale,)


def f_merge(o_ssd, o_pool, gl_ssd, gl_pool):
    return (jax.nn.sigmoid(gl_ssd) * o_ssd + jax.nn.sigmoid(gl_pool) * o_pool,)


def _column_splitter(n):
    @jax.custom_vjp
    def split(x):
        w = x.shape[1] // n
        return tuple(x[:, k * w:(k + 1) * w] for k in range(n))

    def fwd(x):
        return split(x), None

    def bwd(_, g):
        return (jnp.concatenate(g, axis=1),)

    split.defvjp(fwd, bwd)
    return split


_halve_cols = _column_splitter(2)
_quarter_cols = _column_splitter(len(POOL_WINDOWS))


def f_swiglu(gu):
    a, b = _halve_cols(gu)
    return ((a * jax.nn.sigmoid(a)) * b,)


def f_pool_all(u, pmat, pmat_t, inv_cnt, scale, *pws):
    outs = [f_pool(part, pmat[k], pmat_t[k], inv_cnt[k], pws[k], 1.0)[0] for k, part in enumerate(_quarter_cols(u))]
    return (jnp.concatenate(outs, axis=1) * scale,)


def f_loss_resid(x1, dn, ga, tgt, g):
    err = _rms(x1 + ga * dn) * g - tgt
    return (0.5 * jnp.mean(err * err, axis=-1, keepdims=True),)


CONV_TILE = 128


def _shift_rows(v, j, n_ctx):
    if j == 0:
        return v
    T = v.shape[0]
    r = lax.broadcasted_iota(jnp.int32, v.shape, 0)
    lo = jnp.where(r >= n_ctx, n_ctx, 0)
    hi = jnp.where(r >= n_ctx, T, n_ctx)
    ok = (r + j >= lo) & (r + j < hi)
    return jnp.where(ok, pltpu.roll(v, (-j) % T, 0), 0.0)


def conv_fwd(name, proj, conv_w, conv_b, n_ctx, width, ex=None):
    T = proj.shape[0]
    half = SSD_CONV // 2

    def body(u_ref, w_ref, b_ref, o_ref):
        u = u_ref[...].astype(F32)
        pre = jnp.broadcast_to(b_ref[...], u.shape)
        for k in range(SSD_CONV):
            pre = pre + w_ref[k:k + 1, :] * _shift_rows(u, k - half, n_ctx)
        o_ref[...] = pre * jax.nn.sigmoid(pre)

    col = lambda t: (0, t)
    res, xres = hosted_call(
        body, ex, [proj, conv_w, conv_b], name=name, out_shape=[jax.ShapeDtypeStruct((T, width), F32)],
        grid=(width // CONV_TILE,),
        in_specs=[pl.BlockSpec((T, CONV_TILE), col), pl.BlockSpec((SSD_CONV, CONV_TILE), col),
                  pl.BlockSpec((1, CONV_TILE), col)],
        out_specs=[pl.BlockSpec((T, CONV_TILE), col)])
    return res[0], xres


def conv_bwd(name, proj, conv_w, conv_b, d_act2, d_skip, n_ctx, width, ex=None):
    T = proj.shape[0]
    half = SSD_CONV // 2

    def body(u_ref, w_ref, b_ref, c0_ref, c1_ref, cs_ref, du_ref, dw_ref, db_ref):
        t = pl.program_id(0)
        u = u_ref[...].astype(F32)
        pre = jnp.broadcast_to(b_ref[...], u.shape)
        for k in range(SSD_CONV):
            pre = pre + w_ref[k:k + 1, :] * _shift_rows(u, k - half, n_ctx)
        sg = jax.nn.sigmoid(pre)
        ct = c0_ref[...].astype(F32) + c1_ref[...].astype(F32) + jnp.where(t % 4 < 2, cs_ref[...].astype(F32), 0.0)
        dpre = ct * (sg * (1.0 + pre * (1.0 - sg)))
        du = jnp.zeros_like(u)
        for k in range(SSD_CONV):
            du = du + w_ref[k:k + 1, :] * _shift_rows(dpre, half - k, n_ctx)
            dw_ref[k:k + 1, :] = jnp.sum(dpre * _shift_rows(u, k - half, n_ctx), axis=0, keepdims=True)
        du_ref[...] = du.astype(du_ref.dtype)
        db_ref[...] = jnp.sum(dpre, axis=0, keepdims=True)

    col = lambda t: (0, t)
    skip_col = lambda t: (0, (t // 4) * 2 + jnp.minimum(t % 4, 1))
    res, xres = hosted_call(
        body, ex, [proj, conv_w, conv_b, d_act2, d_act2, d_skip], name=name,
        out_shape=[jax.ShapeDtypeStruct((T, width), ACT_DTYPE), jax.ShapeDtypeStruct((SSD_CONV, width), F32),
                   jax.ShapeDtypeStruct((1, width), F32)],
        grid=(width // CONV_TILE,),
        in_specs=[pl.BlockSpec((T, CONV_TILE), col), pl.BlockSpec((SSD_CONV, CONV_TILE), col),
                  pl.BlockSpec((1, CONV_TILE), col), pl.BlockSpec((T, CONV_TILE), col),
                  pl.BlockSpec((T, CONV_TILE), lambda t: (1, t)), pl.BlockSpec((T, CONV_TILE), skip_col)],
        out_specs=[pl.BlockSpec((T, CONV_TILE), col), pl.BlockSpec((SSD_CONV, CONV_TILE), col),
                   pl.BlockSpec((1, CONV_TILE), col)])
    return res[0], res[1], res[2], xres


@jax.custom_vjp
def _cumsum_mat(tri, tri_t, a):
    return jnp.dot(tri, a, precision=lax.Precision.HIGHEST, preferred_element_type=F32)


def _cumsum_fwd(tri, tri_t, a):
    return _cumsum_mat(tri, tri_t, a), (tri, tri_t)


def _cumsum_bwd(res, g):
    tri, tri_t = res
    return (jnp.zeros_like(tri), jnp.zeros_like(tri_t),
            jnp.dot(tri_t, g, precision=lax.Precision.HIGHEST, preferred_element_type=F32))


_cumsum_mat.defvjp(_cumsum_fwd, _cumsum_bwd)


def _ssd_dt(dtraw, dt_bias, a_log, tri, tri_t):
    dt_all = jax.nn.softplus(dtraw + dt_bias)
    a_all = dt_all * (-jnp.exp(a_log))
    return dt_all, a_all, _cumsum_mat(tri, tri_t, a_all)


def _ssd_chunk(xs, bm, cm, dt_all, a_all, s_all, s_in, mask, idx0):
    (xs,), (s_in,) = xs, s_in
    Q = xs.shape[0]
    hpg = xs.shape[1] // SSD_HEADDIM
    lane = lax.broadcasted_iota(jnp.int32, dt_all.shape, 1)
    head = lax.broadcasted_iota(jnp.int32, xs.shape, 1) // SSD_HEADDIM
    head1 = lax.broadcasted_iota(jnp.int32, (1, xs.shape[1]), 1) // SSD_HEADDIM

    def pick(v, r):
        return jnp.sum(jnp.where(lane == idx0 + r, v, 0.0), axis=1, keepdims=True)

    def expand(cols, hd):
        out = cols[hpg - 1]
        for r in range(hpg - 2, -1, -1):
            out = jnp.where(hd == r, cols[r], out)
        return out

    def spread(*cols):
        return expand([jnp.broadcast_to(c, xs.shape) for c in cols], head)

    dt_r = [pick(dt_all, r) for r in range(hpg)]
    s_r = [pick(s_all, r) for r in range(hpg)]
    stot_r = [jnp.sum(jnp.where(lane == idx0 + r, a_all, 0.0), keepdims=True).reshape(1, 1) for r in range(hpg)]

    xd = xs * spread(*dt_r)
    cb = _mm_nt(cm, bm)
    weights, stacked = [], []
    for r in range(hpg):
        sm = jnp.broadcast_to(s_r[r], (Q, Q))
        weights.append(cb * jnp.exp(jnp.where(mask, sm - sm.T, NEG)))
        stacked.append(jnp.where(head == r, xd, 0.0))
    y = spread(*[jnp.exp(c) for c in s_r]) * _mm(cm, s_in)
    y = y + _mm(jnp.concatenate(weights, axis=1), jnp.concatenate(stacked, axis=0))
    to_end = spread(*[jnp.exp(t - c) for t, c in zip(stot_r, s_r)])
    carry = expand([jnp.broadcast_to(jnp.exp(t), (1, xs.shape[1])) for t in stot_r], head1)
    s_out = carry * s_in + _mm_tn(bm, xd * to_end)
    return [y], [s_out]


def _scan_consts():
    q = SSD_CHUNK
    i = np.arange(q)[:, None]
    j = np.arange(q)[None, :]
    fwd = (j <= i).astype(np.float32)
    bwd = (j >= i).astype(np.float32)
    tri = np.stack([fwd, bwd])
    return jnp.asarray(tri), jnp.asarray(np.stack([fwd.T, bwd.T]))


def _chunk_of(d, k, ncc, nc):
    rev = jnp.where(k < ncc, ncc - 1 - k, nc - 1 + ncc - k)
    return jnp.where(d == 0, k, rev)


def ssd_fwd(name, xbc, proj, dt_cb, dt_bias, a_log, n_ctx, ex=None):
    T = xbc.shape[0]
    q, G = SSD_CHUNK, SSD_GROUPS
    nc, ncc = T // q, n_ctx // q
    gw = xbc.shape[1] // G
    xw = gw - 2 * SSD_STATE
    hpg = xw // SSD_HEADDIM
    nh = G * hpg
    tri, tri_t = _scan_consts()

    gs = SSD_GROUPS_PER_STEP

    def body(x_ref, dt_ref, bias_ref, alog_ref, tri_ref, trit_ref, y_ref, sin_ref, state):
        d, gb, k = pl.program_id(0), pl.program_id(1), pl.program_id(2)

        @pl.when(k == 0)
        def _():
            state[...] = jnp.zeros_like(state)

        tri_v = tri_ref[...]
        dt_all, a_all, s_all = _ssd_dt(dt_ref[...], bias_ref[...], alog_ref[...], tri_v, trit_ref[...])
        pairs = [(0, xw)]
        for j in range(gs):
            o = j * gw
            sin_ref[j] = state[j]
            ys, s_outs = _ssd_chunk(
                [x_ref[:, o + lo:o + hi] for lo, hi in pairs], x_ref[:, o + xw:o + xw + SSD_STATE],
                x_ref[:, o + xw + SSD_STATE:o + gw], dt_all, a_all, s_all, [state[j, :, lo:hi] for lo, hi in pairs],
                tri_v > 0.5, d * nh + (gb * gs + j) * hpg)
            for (lo, hi), y, s_out in zip(pairs, ys, s_outs):
                y_ref[:, j * xw + lo:j * xw + hi] = y.astype(y_ref.dtype)
                state[j, :, lo:hi] = s_out

    ch = lambda d, g, k: _chunk_of(d, k, ncc, nc)
    res, xres = hosted_call(
        body, ex, [xbc, proj, dt_bias, a_log, tri, tri_t], name=name,
        out_shape=[jax.ShapeDtypeStruct((2 * T, G * xw), ACT_DTYPE),
                   jax.ShapeDtypeStruct((2, nc, G, SSD_STATE, xw), F32)],
        grid=(2, G // gs, nc),
        in_specs=[pl.BlockSpec((q, gs * gw), lambda d, g, k: (ch(d, g, k), g)),
                  pl.BlockSpec((q, 128), lambda d, g, k: (ch(d, g, k), dt_cb)),
                  pl.BlockSpec((1, 128), lambda d, g, k: (0, 0)),
                  pl.BlockSpec((1, 128), lambda d, g, k: (0, 0)),
                  pl.BlockSpec((None, q, q), lambda d, g, k: (d, 0, 0)),
                  pl.BlockSpec((None, q, q), lambda d, g, k: (d, 0, 0))],
        out_specs=[pl.BlockSpec((q, gs * xw), lambda d, g, k: (d * nc + ch(d, g, k), g)),
                   pl.BlockSpec((None, None, gs, SSD_STATE, xw), lambda d, g, k: (d, k, g, 0, 0))],
        scratch_shapes=[pltpu.VMEM((gs, SSD_STATE, xw), F32)])
    return res[0], res[1], xres


def ssd_bwd(name, xbc, proj, dt_cb, dt_bias, a_log, states, dy, n_ctx, ex=None):
    T = xbc.shape[0]
    q, G = SSD_CHUNK, SSD_GROUPS
    nc, ncc = T // q, n_ctx // q
    gw = xbc.shape[1] // G
    xw = gw - 2 * SSD_STATE
    hpg = xw // SSD_HEADDIM
    nh = G * hpg
    tri, tri_t = _scan_consts()

    gs = SSD_GROUPS_PER_STEP

    def body(x_ref, dt_ref, bias_ref, alog_ref, tri_ref, trit_ref, sin_ref, dy_ref,
             dx_ref, ddt_ref, dbias_ref, dalog_ref, dstate):
        d, gb, k = pl.program_id(0), pl.program_id(1), pl.program_id(2)
        first = (d == 0) & (gb == 0) & (k == 0)

        @pl.when(first)
        def _():
            ddt_ref[...] = jnp.zeros_like(ddt_ref)
            dbias_ref[...] = jnp.zeros_like(dbias_ref)
            dalog_ref[...] = jnp.zeros_like(dalog_ref)

        @pl.when(k == 0)
        def _():
            dstate[...] = jnp.zeros_like(dstate)

        tri_v, trit_v = tri_ref[...], trit_ref[...]
        mask = tri_v > 0.5

        pairs = [(0, xw)]
        npair = len(pairs)
        per = 2 * npair + 2

        def fn(dtraw, bias, alog, *per_group):
            dt_all, a_all, s_all = _ssd_dt(dtraw, bias, alog, tri_v, trit_v)
            ys, s_outs = [], []
            for j in range(gs):
                grp = per_group[per * j:per * (j + 1)]
                y, s_out = _ssd_chunk(list(grp[:npair]), grp[npair], grp[npair + 1], dt_all, a_all, s_all,
                                      list(grp[npair + 2:]), mask, d * nh + (gb * gs + j) * hpg)
                ys += y
                s_outs += s_out
            return ys, s_outs

        per_group = []
        for j in range(gs):
            o = j * gw
            per_group += [x_ref[:, o + lo:o + hi] for lo, hi in pairs]
            per_group += [x_ref[:, o + xw:o + xw + SSD_STATE], x_ref[:, o + xw + SSD_STATE:o + gw]]
            per_group += [sin_ref[j, :, lo:hi] for lo, hi in pairs]
        _, vjp = jax.vjp(fn, dt_ref[...], bias_ref[...], alog_ref[...], *per_group)
        cts = vjp(([dy_ref[:, j * xw + lo:j * xw + hi].astype(F32) for j in range(gs) for lo, hi in pairs],
                   [dstate[j, :, lo:hi] for j in range(gs) for lo, hi in pairs]))
        ddt, dbias, dalog = cts[:3]
        for j in range(gs):
            o = j * gw
            grp = cts[3 + per * j:3 + per * (j + 1)]
            for (lo, hi), dxs, ds_in in zip(pairs, grp[:npair], grp[npair + 2:]):
                dx_ref[:, o + lo:o + hi] = dxs.astype(dx_ref.dtype)
                dstate[j, :, lo:hi] = ds_in
            dx_ref[:, o + xw:o + xw + SSD_STATE] = grp[npair].astype(dx_ref.dtype)
            dx_ref[:, o + xw + SSD_STATE:o + gw] = grp[npair + 1].astype(dx_ref.dtype)
        row0 = pl.multiple_of(_chunk_of(d, nc - 1 - k, ncc, nc) * q, q)
        ddt_ref[pl.ds(row0, q), :] += ddt
        dbias_ref[...] += dbias
        dalog_ref[...] += dalog

    ch = lambda d, g, k: _chunk_of(d, nc - 1 - k, ncc, nc)
    res, xres = hosted_call(
        body, ex, [xbc, proj, dt_bias, a_log, tri, tri_t, states, dy], name=name,
        out_shape=[jax.ShapeDtypeStruct((2 * T, G * gw), ACT_DTYPE), jax.ShapeDtypeStruct((T, 128), F32),
                   jax.ShapeDtypeStruct((1, 128), F32), jax.ShapeDtypeStruct((1, 128), F32)],
        grid=(2, G // gs, nc),
        in_specs=[pl.BlockSpec((q, gs * gw), lambda d, g, k: (ch(d, g, k), g)),
                  pl.BlockSpec((q, 128), lambda d, g, k: (ch(d, g, k), dt_cb)),
                  pl.BlockSpec((1, 128), lambda d, g, k: (0, 0)),
                  pl.BlockSpec((1, 128), lambda d, g, k: (0, 0)),
                  pl.BlockSpec((None, q, q), lambda d, g, k: (d, 0, 0)),
                  pl.BlockSpec((None, q, q), lambda d, g, k: (d, 0, 0)),
                  pl.BlockSpec((None, None, gs, SSD_STATE, xw), lambda d, g, k: (d, nc - 1 - k, g, 0, 0)),
                  pl.BlockSpec((q, gs * xw), lambda d, g, k: (ch(d, g, k), g))],
        out_specs=[pl.BlockSpec((q, gs * gw), lambda d, g, k: (d * nc + ch(d, g, k), g)),
                   pl.BlockSpec((T, 128), lambda d, g, k: (0, 0)),
                   pl.BlockSpec((1, 128), lambda d, g, k: (0, 0)),
                   pl.BlockSpec((1, 128), lambda d, g, k: (0, 0))],
        scratch_shapes=[pltpu.VMEM((gs, SSD_STATE, xw), F32)])
    return res[0], res[1], res[2], res[3], xres


def _perm_xbc(a):
    G = SSD_GROUPS
    n = a.shape[-1]
    gn = G * SSD_STATE
    di = n - 2 * gn
    lead = a.shape[:-1]
    xs = a[..., :di].reshape(lead + (G, di // G))
    bm = a[..., di:di + gn].reshape(lead + (G, SSD_STATE))
    cm = a[..., di + gn:].reshape(lead + (G, SSD_STATE))
    return jnp.concatenate([xs, bm, cm], axis=-1).reshape(lead + (n,))


def _unperm_xbc(a):
    G = SSD_GROUPS
    n = a.shape[-1]
    gn = G * SSD_STATE
    di = n - 2 * gn
    lead = a.shape[:-1]
    r = a.reshape(lead + (G, n // G))
    xw = di // G
    return jnp.concatenate([r[..., :xw].reshape(lead + (di,)), r[..., xw:xw + SSD_STATE].reshape(lead + (gn,)),
                            r[..., xw + SSD_STATE:].reshape(lead + (gn,))], axis=-1)


def _pool_consts(tm, n_ctx):
    assert n_ctx == tm and tm % GRID_W == 0
    mats, cnts = [], []
    for seq in (n_ctx, GRID_W):
        t = np.arange(tm)
        tt = t % seq
        base = t - tt
        ms, cs = [], []
        for k in POOL_WINDOWS:
            lo = np.clip(tt - k // 2, 0, seq) + base
            hi = np.clip(tt + k // 2, 0, seq) + base
            m = ((t[None, :] >= lo[:, None]) & (t[None, :] < hi[:, None])).astype(np.float32)
            ms.append(m)
            cs.append((1.0 / (hi - lo).astype(np.float32))[:, None])
        mats.append(np.stack(ms))
        cnts.append(np.stack(cs))
    m = np.stack(mats)
    return jnp.asarray(m), jnp.asarray(np.swapaxes(m, -1, -2)), jnp.asarray(np.stack(cnts).astype(np.float32))


def _prep_layer_weights(w_ada, b_ada, g_mix, w_in, conv_w, conv_b, dt_bias, a_log, d_skip, ssd_norm_w, w_ssd_out,
                        pool_w, pool_scale, w_pool_out, w_out, g_ffn, w_gate_up, w_down):
    D = w_in.shape[0]
    di = ssd_norm_w.shape[0]
    xbc = conv_w.shape[1]
    nh2 = dt_bias.size
    pw = pool_scale.shape[0]
    o = 0
    wz = w_in[:, o:o + di]; o += di
    wx = w_in[:, o:o + xbc]; o += xbc
    wdt = w_in[:, o:o + nh2]; o += nh2
    wp = w_in[:, o:o + pw]; o += pw
    wg = w_in[:, o:]
    w1 = jnp.concatenate([_perm_xbc(wx), wz, wg, wp, wdt, jnp.zeros((D, DT_PAD - nh2), w_in.dtype)], axis=1)
    pad128 = lambda v: jnp.concatenate([v.reshape(1, -1), jnp.zeros((1, 128 - v.size), F32)], axis=1)
    return dict(
        w_ada=w_ada, b_ada=b_ada.reshape(1, -1), g_mix=g_mix.reshape(1, -1), w1=w1,
        conv_w=_perm_xbc(conv_w), conv_b=_perm_xbc(conv_b.reshape(1, -1)),
        dt_bias=pad128(dt_bias), a_log=pad128(a_log),
        dskip=jnp.repeat(d_skip[0] + d_skip[1], SSD_HEADDIM).reshape(1, -1),
        ssd_norm_w=ssd_norm_w.reshape(1, -1), w_ssd_out=w_ssd_out, pool_w=pool_w,
        pool_scale=pool_scale.reshape(1, -1), w_pool_out=w_pool_out, w_out=w_out, g_ffn=g_ffn.reshape(1, -1),
        w_gate_up=w_gate_up, w_down=w_down)


def _unprep_layer_grads(g, dims):
    di, xbc, nh2, pw = dims
    dxbc, dz, dgs, dgp, dp, ddt = g["w1"]
    r = dxbc.reshape(SSD_GROUPS, xbc // SSD_GROUPS, dxbc.shape[1])
    xw = di // SSD_GROUPS
    parts = [r[:, :xw], r[:, xw:xw + SSD_STATE], r[:, xw + SSD_STATE:]]
    w_in_t = jnp.concatenate([dz] + [p.reshape(-1, dxbc.shape[1]) for p in parts] + [ddt[:nh2], dp, dgs, dgp], axis=0)
    nh = nh2 // 2
    dsk = g["dskip"].reshape(nh, SSD_HEADDIM).sum(axis=1)
    return dict(
        w_ada=g["w_ada"], b_ada=g["b_ada"].reshape(-1), g_mix=g["g_mix"].reshape(-1),
        w_in=w_in_t,
        conv_w=_unperm_xbc(g["conv_w"]), conv_b=_unperm_xbc(g["conv_b"]).reshape(-1),
        dt_bias=g["dt_bias"][0, :nh2].reshape(2, nh), a_log=g["a_log"][0, :nh2].reshape(2, nh),
        d_skip=jnp.stack([dsk, dsk]), ssd_norm_w=g["ssd_norm_w"].reshape(-1), w_ssd_out=g["w_ssd_out"],
        pool_w=g["pool_w"], pool_scale=g["pool_scale"].reshape(-1), w_pool_out=g["w_pool_out"], w_out=g["w_out"],
        g_ffn=g["g_ffn"].reshape(-1), w_gate_up=g["w_gate_up"], w_down=g["w_down"])


COND_ROWS = 16


def _split_mods(m):
    d = m.shape[1] // 6
    return [m[:2, k * d:(k + 1) * d].reshape(2, 1, d) for k in range(6)]


def _pool_args(rows, proj, col_block, width, pc, w):
    seg_const = lambda a: Arg(a, (None,) + a.shape[1:], lambda j, i, s: (s, 0, 0, 0), "const")
    pws = [Arg(w["pool_w"][k], w["pool_w"].shape[1:], lambda j, i, s: (0, 0), "acc") for k in range(w["pool_w"].shape[0])]
    return [rows.row(proj, width, col_block)] + [seg_const(a) for a in pc] + [rows.vec(w["pool_scale"])] + pws


TALL_ROW_TILE = 1088


def _tall_rows(T, ncol):
    tm = max(t for t in range(16, min(T, TALL_ROW_TILE) + 1, 16) if T % t == 0)
    return Rows(T // tm, 0, tm, ncol)


def _hosted(hosts, box, key):
    fn = (hosts or {}).get(key)
    return fn(box) if fn else None


def _layer_fwd(l, pre, cond_s, w, rows, n_ctx, pc, hosts=None, box=None):
    T, D = pre[0].shape if isinstance(pre, tuple) else pre.shape
    nt, nct, tm = rows.nt, rows.nct, rows.tm
    n = lambda s: f"l{l}_{s}"
    crow = Rows(1, 0, COND_ROWS)
    mraw = matmul_nn(n("ada_mm"), cond_s, w["w_ada"])
    (m,) = stage_fwd(n("ada_bias"), f_bias, crow, [crow.row(mraw, mraw.shape[1]), crow.vec(w["b_ada"])],
                     [(mraw.shape[1], F32, False)])
    sh1, sc1, ga1, sh2, sc2, ga2 = _split_mods(m)

    if isinstance(pre, tuple):
        x, h1 = stage_fwd(n("norm1"), f_resid_norm_mod, rows, _resid_norm_args(rows, pre, w["g_mix"], sh1, sc1, D),
                          [(D, F32, False), (D, ACT_DTYPE, False)])
    else:
        x = pre
        (h1,) = stage_fwd(n("norm1"), f_norm_mod, rows,
                          [rows.row(x, D), rows.vec(w["g_mix"]), rows.segvec(sh1), rows.segvec(sc1)],
                          [(D, ACT_DTYPE, False)])
    xbc_w = w["conv_w"].shape[1]
    di = w["ssd_norm_w"].shape[1]
    pw = w["pool_scale"].shape[1]
    c_z, c_g, c_p, c_dt = xbc_w, xbc_w + di, xbc_w + di + 2 * pw, xbc_w + di + 3 * pw
    ex = _hosted(hosts, box, "in_mm")
    proj = matmul_nn(n("in_mm"), h1, w["w1"], out_dtype=ACT_DTYPE, ex=ex, ncols=c_dt)
    if ex is not None:
        proj, box["in_mm"] = proj
    dtraw = matmul_nn(n("in_dt_mm"), h1, w["w1"], col0=c_dt, ncols=128)
    ex = _hosted(hosts, box, "conv")
    xbc, xres = conv_fwd(n("conv"), proj, w["conv_w"], w["conv_b"], n_ctx, xbc_w, ex)
    if ex is not None:
        box["conv"] = xres
    ex = _hosted(hosts, box, "ssd")
    y2, states, xres = ssd_fwd(n("ssd"), xbc, dtraw, 0, w["dt_bias"], w["a_log"], n_ctx, ex)
    if ex is not None:
        box["ssd"] = xres

    G = SSD_GROUPS
    gw = di // G
    r8 = _tall_rows(T, G)
    gate_args = [r8.row(y2, gw, 0, True), r8.row(y2, gw, 0, True, roff=r8.nt), r8.row(xbc, gw, 0, True, stride=2),
                 r8.row(proj, gw, c_z // gw, True), r8.vec(w["dskip"], True), r8.vec(w["ssd_norm_w"], True)]
    (ynw,) = stage_fwd(n("ssd_gate"), f_ssd_gate, r8, gate_args, [(gw, ACT_DTYPE, True)])
    o_ssd = matmul_nn(n("ssd_out_mm"), ynw, w["w_ssd_out"])

    nw = len(POOL_WINDOWS)
    pg = pw // nw
    (ps,) = stage_fwd(n("pool"), f_pool_all, rows, _pool_args(rows, proj, c_p // pw, pw, pc, w), [(pw, ACT_DTYPE, False)])
    o_pool = matmul_nn(n("pool_out_mm"), ps, w["w_pool_out"])

    merge_args = [rows.row(o_ssd, D), rows.row(o_pool, D), rows.row(proj, pw, c_g // pw), rows.row(proj, pw, c_g // pw + 1)]
    (mg,) = stage_fwd(n("merge"), f_merge, rows, merge_args, [(D, ACT_DTYPE, False)])
    mo = matmul_nn(n("out_mm"), mg, w["w_out"])

    rn_args = [rows.row(x, D), rows.row(mo, D), rows.segvec(ga1), rows.vec(w["g_ffn"]), rows.segvec(sh2), rows.segvec(sc2)]
    x1, h2 = stage_fwd(n("norm2"), f_resid_norm_mod, rows, rn_args, [(D, F32, False), (D, ACT_DTYPE, False)])
    ex = _hosted(hosts, box, "gate_up_mm")
    gu = matmul_nn(n("gate_up_mm"), h2, w["w_gate_up"], ex=ex)
    if ex is not None:
        gu, box["gate_up_mm"] = gu
    fh = gu.shape[1] // 2
    (act,) = stage_fwd(n("swiglu"), f_swiglu, rows, [rows.row(gu, 2 * fh)], [(fh, ACT_DTYPE, False)])
    dn = matmul_nn(n("down_mm"), act, w["w_down"])
    saved = dict(x=x, pre=pre, mraw=mraw, mods=(sh1, sc1, ga1, sh2, sc2, ga2), h1=h1, proj=proj, dtraw=dtraw, xbc=xbc, y2=y2,
                 states=states,
                 ynw=ynw, o_ssd=o_ssd, ps=ps, o_pool=o_pool, mg=mg, mo=mo, x1=x1, h2=h2, gu=gu, act=act, dn=dn,
                 cols=(c_z, c_g, c_p, c_dt))
    return (x1, dn, ga2), saved


def _resid_norm_args(rows, pre, g, sh, sc, D):
    x1, dn, ga2 = pre
    return [rows.row(x1, D), rows.row(dn, D), rows.segvec(ga2), rows.vec(g), rows.segvec(sh), rows.segvec(sc)]


def f_norm_mod_keep(x, g, sh, sc):
    return f_norm_mod(x, g, sh, sc)[0], x


def _layer_bwd(l, cot, cond_s, w, s, rows, n_ctx, pc, hosts=None, box=None):
    dx1, ddn, dga2 = cot
    T, D = dx1.shape
    nt, nct, tm = rows.nt, rows.nct, rows.tm
    n = lambda t: f"l{l}_{t}_bwd"
    sh1, sc1, ga1, sh2, sc2, ga2 = s["mods"]
    c_z, c_g, c_p, c_dt = s["cols"]
    x, proj, xbc, y2, gu = s["x"], s["proj"], s["xbc"], s["y2"], s["gu"]
    g = {}
    if box is not None:
        box["g"] = g

    ex = _hosted(hosts, box, "down_dx")
    dact = matmul_nt(n("down_dx"), ddn, w["w_down"], ex=ex)
    if ex is not None:
        dact, box["down_dx"] = dact
    g["w_down"] = matmul_tn(n("down_dw"), s["act"], ddn)
    fh = gu.shape[1] // 2
    (dgu,) = stage_bwd(n("swiglu"), f_swiglu, rows, [rows.row(gu, 2 * fh)], [rows.row(dact, fh)], [ACT_DTYPE])
    dh2 = matmul_nt(n("gate_up_dx"), dgu, w["w_gate_up"])
    g["w_gate_up"] = matmul_tn(n("gate_up_dw"), s["h2"], dgu)

    rn_args = [rows.row(x, D), rows.row(s["mo"], D), rows.segvec(ga1), rows.vec(w["g_ffn"]), rows.segvec(sh2), rows.segvec(sc2)]
    dxr, dmo, dga1, g["g_ffn"], dsh2, dsc2 = stage_bwd(
        n("norm2"), f_resid_norm_mod, rows, rn_args, [rows.row(dx1, D), rows.row(dh2, D)], [F32, ACT_DTYPE])
    dmg = matmul_nt(n("out_dx"), dmo, w["w_out"])
    g["w_out"] = matmul_tn(n("out_dw"), s["mg"], dmo)

    pw = w["pool_scale"].shape[1]
    merge_args = [rows.row(s["o_ssd"], D), rows.row(s["o_pool"], D), rows.row(proj, pw, c_g // pw), rows.row(proj, pw, c_g // pw + 1)]
    do_ssd, do_pool, dgl_s, dgl_p = stage_bwd(n("merge"), f_merge, rows, merge_args, [rows.row(dmg, D)], [ACT_DTYPE] * 4)
    dps = matmul_nt(n("pool_out_dx"), do_pool, w["w_pool_out"])
    g["w_pool_out"] = matmul_tn(n("pool_out_dw"), s["ps"], do_pool)

    nw = len(POOL_WINDOWS)
    pg = pw // nw
    du_pool, g["pool_scale"], *dpw = stage_bwd(n("pool"), f_pool_all, rows, _pool_args(rows, proj, c_p // pw, pw, pc, w),
                                               [rows.row(dps, pw)], [ACT_DTYPE])
    g["pool_w"] = jnp.stack(dpw)

    dynw = matmul_nt(n("ssd_out_dx"), do_ssd, w["w_ssd_out"])
    g["w_ssd_out"] = matmul_tn(n("ssd_out_dw"), s["ynw"], do_ssd)
    G = SSD_GROUPS
    di = w["ssd_norm_w"].shape[1]
    gw = di // G
    r8 = _tall_rows(T, G)
    gate_args = [r8.row(y2, gw, 0, True), r8.row(y2, gw, 0, True, roff=r8.nt), r8.row(xbc, gw, 0, True, stride=2),
                 r8.row(proj, gw, c_z // gw, True), r8.vec(w["dskip"], True), r8.vec(w["ssd_norm_w"], True)]
    gate_args[1].kind = "const"
    ex = _hosted(hosts, box, "ssd_gate")
    res = stage_bwd(n("ssd_gate"), f_ssd_gate, r8, gate_args, [r8.row(dynw, gw, 0, True)], [ACT_DTYPE] * 3, ex)
    if ex is not None:
        res, box["ssd_gate"] = res
    dy, dxs_skip, dz, g["dskip"], g["ssd_norm_w"] = res

    ex = _hosted(hosts, box, "ssd")
    dxbc2, ddt, g["dt_bias"], g["a_log"], xres = ssd_bwd(n("ssd"), xbc, s["dtraw"], 0, w["dt_bias"], w["a_log"],
                                                         s["states"], dy, n_ctx, ex)
    if ex is not None:
        box["ssd"] = xres
    xbc_w = xbc.shape[1]
    ex = _hosted(hosts, box, "conv")
    dxbc_raw, g["conv_w"], g["conv_b"], xres = conv_bwd(n("conv"), proj, w["conv_w"], w["conv_b"], dxbc2, dxs_skip,
                                                         n_ctx, xbc_w, ex)
    if ex is not None:
        box["conv"] = xres
    pieces = [dxbc_raw, dz, dgl_s, dgl_p, du_pool, ddt]
    offsets = [0, c_z, c_g, c_g + pw, c_p, c_dt]
    ex = _hosted(hosts, box, "in_dx")
    dh1 = matmul_nt(n("in_dx"), pieces, w["w1"], ex=ex, offsets=offsets)
    if ex is not None:
        dh1, box["in_dx"] = dh1
    ex = _hosted(hosts, box, "in_dw")
    first = matmul_tn(n("in_dw0"), pieces[0], s["h1"], ex=ex)
    if ex is not None:
        first, box["in_dw"] = first
    g["w1"] = [first] + [matmul_tn(n(f"in_dw{k}"), p, s["h1"]) for k, p in enumerate(pieces) if k]

    if isinstance(s["pre"], tuple):
        dx1p, ddnp, dga2p, g["g_mix"], dsh1, dsc1 = stage_bwd(
            n("norm1"), f_resid_norm_mod, rows, _resid_norm_args(rows, s["pre"], w["g_mix"], sh1, sc1, D),
            [rows.row(dxr, D), rows.row(dh1, D)], [F32, ACT_DTYPE])
        dx = (dx1p, ddnp, dga2p)
    else:
        n1_args = [rows.row(x, D), rows.vec(w["g_mix"]), rows.segvec(sh1), rows.segvec(sc1)]
        dx, g["g_mix"], dsh1, dsc1 = stage_bwd(n("norm1"), f_norm_mod_keep, rows, n1_args,
                                               [rows.row(dh1, D), rows.row(dxr, D)], [F32])

    dm = jnp.concatenate([v.reshape(2, D) for v in (dsh1, dsc1, dga1, dsh2, dsc2, dga2)], axis=1)
    dm = jnp.concatenate([dm, jnp.zeros((COND_ROWS - 2, dm.shape[1]), F32)], axis=0)
    crow = Rows(1, 0, COND_ROWS)
    dmraw, g["b_ada"] = stage_bwd(n("ada_bias"), f_bias, crow, [crow.row(s["mraw"], dm.shape[1]), crow.vec(w["b_ada"])],
                                  [crow.row(dm, dm.shape[1])], [ACT_DTYPE])
    dcs = matmul_nt(n("ada_dx"), dmraw, w["w_ada"])
    g["w_ada"] = matmul_tn(n("ada_dw"), cond_s, dmraw)
    return dx, dcs, g


def local_step(x, ctx, c, c_ctx, target, layer_w_fn, n_layers, g_final, fwd_hosts=None, bwd_hosts=None):
    L, D = x.shape
    n_ctx = ctx.shape[0]
    tm = ROW_TILE
    T = L + n_ctx
    rows = Rows(T // tm, n_ctx // tm, tm)
    pc = _pool_consts(tm, n_ctx)
    xa = jnp.concatenate([ctx, x], axis=0)
    cond = jnp.concatenate([c_ctx.reshape(1, D), c.reshape(1, D), jnp.zeros((COND_ROWS - 2, D), F32)], axis=0)
    crow = Rows(1, 0, COND_ROWS)
    (cond_s,) = stage_fwd("cond_silu", f_silu, crow, [crow.row(cond, D)], [(D, ACT_DTYPE, False)])

    saved, layer_w = [], []
    for l in range(n_layers):
        layer_w.append(layer_w_fn(l))
        box = {}
        xa, s = _layer_fwd(l, xa, cond_s, layer_w[l], rows, n_ctx, pc, fwd_hosts(l, box) if fwd_hosts else None, box)
        saved.append(s)

    x1, dn, ga2 = xa
    rl = Rows(L // tm, 0, tm)
    gf = g_final.reshape(1, D)
    tgt = rl.row(target, D)
    tgt.kind = "const"
    off = n_ctx // tm
    loss_args = [rl.row(x1, D, roff=off), rl.row(dn, D, roff=off), rl.vec(ga2[1]), tgt, rl.vec(gf)]
    ones = jnp.ones((L, 1), F32)
    dx1_lat, ddn_lat, dga2_lat, dgf, loss_rows = stage_bwd("loss", f_loss_resid, rl, loss_args, [rl.row(ones, 1)],
                                                           [F32, ACT_DTYPE], primal=[(1, F32)])
    loss = jnp.sum(loss_rows)
    cot = (jnp.concatenate([jnp.zeros((n_ctx, D), F32), dx1_lat], axis=0),
           jnp.concatenate([jnp.zeros((n_ctx, D), ACT_DTYPE), ddn_lat], axis=0),
           jnp.stack([jnp.zeros((1, D), F32), dga2_lat]))

    grads = [None] * n_layers
    dcs = jnp.zeros((COND_ROWS, D), F32)
    for l in reversed(range(n_layers)):
        box = {}
        hosts = bwd_hosts(l, grads, box) if bwd_hosts else None
        cot, dcs_l, grads[l] = _layer_bwd(l, cot, cond_s, layer_w[l], saved[l], rows, n_ctx, pc, hosts, box)
        dcs = dcs + dcs_l
    dx = cot
    (dcond,) = stage_bwd("cond_silu_bwd", f_silu, crow, [crow.row(cond, D)], [crow.row(dcs, D)], [F32])
    return loss, dx[n_ctx:], grads, dcond[0], dgf


def gather_chips(halves, conv=None):
    n = len(halves)
    ops = list(halves) + ([conv] if conv is not None else [])

    def copies(ins, outs, pos):
        c, me = pos[2], _chip_index(pos)
        pairs = [(s.at[c], o.at[me, c]) for s, o in zip(ins[:n], outs[:n])]
        pairs += [(s, o.at[me]) for s, o in zip(ins[n:], outs[n:])]
        return pairs, [(s, d, _flip(pos, rel)) for rel in PLANE for s, d in pairs]

    shapes = [jax.ShapeDtypeStruct((4,) + s.shape, s.dtype) for s in ops]
    return Exchange(copies, 3 * len(ops), len(ops), ops, shapes)


def gather_pair(gathered):
    n = len(gathered)

    def copies(ins, outs, pos):
        c = pos[2]
        return [], [(s.at[b, c], o.at[b, c], _flip(pos, PAIR[0])) for s, o in zip(ins, outs) for b in range(4)]

    shapes = [jax.ShapeDtypeStruct(g.shape, g.dtype) for g in gathered]
    return Exchange(copies, 4 * n, 0, gathered, shapes, aliases={k: k for k in range(n)})


def swap_halves(grads):
    n = len(grads)

    def copies(ins, outs, pos):
        c = pos[2]
        return [], [(g.at[b, 1 - c], o.at[b], _flip(pos, PAIR[0])) for g, o in zip(ins, outs) for b in range(4)]

    shapes = [jax.ShapeDtypeStruct((g.shape[0],) + g.shape[2:], g.dtype) for g in grads]
    return Exchange(copies, 4 * n, 0, grads, shapes)


def scatter_chips(sums):
    n = len(sums)

    def copies(ins, outs, pos):
        me = _chip_index(pos)
        local = [(p.at[me], o.at[me]) for p, o in zip(ins, outs)]
        remote = []
        for rel in PLANE:
            peer = _flip(pos, rel)
            remote += [(p.at[_chip_index(peer)], o.at[me], peer) for p, o in zip(ins, outs)]
        return local, remote

    shapes = [jax.ShapeDtypeStruct(p.shape, p.dtype) for p in sums]
    return Exchange(copies, 3 * n, n, sums, shapes)


def share_halves(finals):
    n = len(finals)

    def copies(ins, outs, pos):
        c = pos[2]
        return [], [(f.at[c], o.at[c], _flip(pos, PAIR[0])) for f, o in zip(ins, outs)]

    shapes = [jax.ShapeDtypeStruct(f.shape, f.dtype) for f in finals]
    return Exchange(copies, n, 0, finals, shapes, aliases={k: k for k in range(n)})


def gather_everyone(vec):
    def copies(ins, outs, pos):
        me = _device_index(pos)
        (v,), (o,) = ins, outs
        return [(v, o.at[me])], [(v, o.at[me], _flip(pos, rel)) for rel in EVERYONE]

    return Exchange(copies, len(EVERYONE), 1, [vec], [jax.ShapeDtypeStruct((8,) + vec.shape, vec.dtype)])


def _row_tile(rows, cols, n_bufs, mult=8):
    cap = VMEM_LIMIT_BYTES // 2 // (2 * n_bufs * cols * 4)
    for t in range(min(rows, cap) // mult * mult, 0, -mult):
        if rows % t == 0:
            return t
    return rows


def _adamw_update(w, g, m, v):
    nm = ADAM_B1 * m + (1.0 - ADAM_B1) * g
    nv = ADAM_B2 * v + (1.0 - ADAM_B2) * jnp.square(g)
    m_hat = nm / (1.0 - ADAM_B1 ** ADAM_STEP)
    v_hat = nv / (1.0 - ADAM_B2 ** ADAM_STEP)
    return -ADAM_LR * (m_hat / (jnp.sqrt(v_hat) + ADAM_EPS) + ADAM_WD * w), nm, nv


def adamw_small(name, ws, gs, ms, vs):
    n = len(ws)

    def body(*refs):
        ins, outs = refs[:4 * n], refs[4 * n:]
        for k in range(n):
            d, nm, nv = _adamw_update(ins[k][...], ins[n + k][...], ins[2 * n + k][...], ins[3 * n + k][...])
            outs[k][...] = d
            outs[n + k][...] = nm
            outs[2 * n + k][...] = nv

    shapes = [jax.ShapeDtypeStruct(a.shape, F32) for a in ws]
    vmem = pl.BlockSpec(memory_space=pltpu.VMEM)
    res = _pcall(body, name=name, out_shape=shapes * 3, in_specs=[vmem] * (4 * n), out_specs=[vmem] * (3 * n),
                 compiler_params=pltpu.CompilerParams(vmem_limit_bytes=VMEM_LIMIT_BYTES))(*ws, *gs, *ms, *vs)
    return res[:n], res[n:2 * n], res[2 * n:]


WIRE_DTYPE = jnp.bfloat16


def add_own_half(name, grads, recv, c):
    nb, _, R, C = grads.shape
    tr = _row_tile(R, C, 3, mult=16)

    def body(c_ref, g_ref, r_ref, o_ref):
        o_ref[...] = (g_ref[...] + r_ref[...]).astype(o_ref.dtype)

    spec = pl.BlockSpec((None, tr, C), lambda b, i, c_ref: (b, i, 0))
    return _pcall(
        body, name=name, out_shape=jax.ShapeDtypeStruct(recv.shape, WIRE_DTYPE),
        grid_spec=pltpu.PrefetchScalarGridSpec(
            num_scalar_prefetch=1, grid=(nb, R // tr),
            in_specs=[pl.BlockSpec((None, None, tr, C), lambda b, i, c_ref: (b, c_ref[0], i, 0)), spec],
            out_specs=spec),
        compiler_params=_params("parallel", "parallel"),
    )(c, grads, recv)


def sum_slots(name, a, c=None):
    n, R, C = a.shape
    tr = _row_tile(R, C, n + 1, mult=16 if a.dtype.itemsize == 2 else 8)

    def body(*refs):
        a_ref, o_ref = refs[-2:]
        acc = a_ref[0].astype(F32)
        for k in range(1, n):
            acc = acc + a_ref[k].astype(F32)
        o_ref[...] = acc

    if c is None:
        return _pcall(
            body, name=name, out_shape=jax.ShapeDtypeStruct((R, C), F32), grid=(R // tr,),
            in_specs=[pl.BlockSpec((n, tr, C), lambda i: (0, i, 0))], out_specs=pl.BlockSpec((tr, C), lambda i: (i, 0)),
            compiler_params=_params("parallel"),
        )(a)
    return _pcall(
        body, name=name, out_shape=jax.ShapeDtypeStruct((2, R, C), F32),
        grid_spec=pltpu.PrefetchScalarGridSpec(
            num_scalar_prefetch=1, grid=(R // tr,),
            in_specs=[pl.BlockSpec((n, tr, C), lambda i, c_ref: (0, i, 0))],
            out_specs=pl.BlockSpec((None, tr, C), lambda i, c_ref: (c_ref[0], i, 0))),
        compiler_params=_params("parallel"),
    )(c, a)


def adamw(name, w, g_layers, m, v):
    nl, R, C = w.shape
    assert len(g_layers) == nl
    tr = _row_tile(R, C, 8 + nl)
    nr = R // tr

    def body(*refs):
        w_ref, m_ref, v_ref = refs[:3]
        g_refs = refs[3:3 + nl]
        go_ref, d_ref, nm_ref, nv_ref = refs[3 + nl:]
        l = pl.program_id(0)
        gr = g_refs[0][...]
        for k in range(1, nl):
            gr = jnp.where(l == k, g_refs[k][...], gr)
        d_ref[...], nm_ref[...], nv_ref[...] = _adamw_update(w_ref[...], gr, m_ref[...], v_ref[...])
        go_ref[...] = gr

    spec = pl.BlockSpec((None, tr, C), lambda l, i: (l, i, 0))
    g_specs = [pl.BlockSpec((tr, C), (lambda l, i, k=k: (jnp.where(l == k, i, jnp.where(l < k, 0, nr - 1)), 0)))
               for k in range(nl)]
    return _pcall(
        body, name=name, out_shape=[jax.ShapeDtypeStruct((nl, R, C), F32)] * 4, grid=(nl, nr),
        in_specs=[spec] * 3 + g_specs, out_specs=[spec] * 4, compiler_params=_params("arbitrary", "arbitrary"),
    )(w, m, v, *g_layers)


BIG = ("w_ada", "w_in", "w_ssd_out", "pool_w", "w_pool_out", "w_out", "w_gate_up", "w_down")
COL_SHARDED = ("w_ada", "w_in", "w_gate_up")
GRAD_TRANSPOSED = ("w_in",)
FIRST_USED = ("w_ada", "w_in")
LATER_USED = tuple(k for k in BIG if k not in FIRST_USED)
READY_LAST = FIRST_USED
READY_EARLY = LATER_USED
SMALL = ("c_ctx", "b_ada", "g_mix", "conv_w", "conv_b", "dt_bias", "a_log", "d_skip", "ssd_norm_w", "pool_scale",
         "g_ffn", "g_final")
WEIGHTS = ("c_ctx", "w_ada", "b_ada", "g_mix", "w_in", "conv_w", "conv_b", "dt_bias", "a_log", "d_skip", "ssd_norm_w",
           "w_ssd_out", "pool_w", "pool_scale", "w_pool_out", "w_out", "g_ffn", "w_gate_up", "w_down", "g_final")
LAYER_KEYS = ("w_ada", "b_ada", "g_mix", "w_in", "conv_w", "conv_b", "dt_bias", "a_log", "d_skip", "ssd_norm_w",
              "w_ssd_out", "pool_w", "pool_scale", "w_pool_out", "w_out", "g_ffn", "w_gate_up", "w_down")


def _shard2d(name, a):
    if name == "pool_w":
        return a.reshape(a.shape[0], a.shape[1] * a.shape[2], a.shape[3])
    return a


def _full_from_blocks(name, a):
    nb, R, C = a.shape
    if name in COL_SHARDED:
        return jnp.transpose(a, (1, 0, 2)).reshape(R, nb * C)
    if name == "pool_w":
        nw = len(POOL_WINDOWS)
        return jnp.transpose(a.reshape(nb, nw, R // nw, C), (1, 0, 2, 3)).reshape(nw, nb * R // nw, C)
    return a.reshape(nb * R, C)


def _blocks_from_full(name, g):
    nb = 4
    if name in COL_SHARDED and name not in GRAD_TRANSPOSED:
        K, N = g.shape
        return jnp.transpose(g.reshape(K, nb, N // nb), (1, 0, 2))
    if name == "pool_w":
        nw, r, C = g.shape
        return jnp.transpose(g.reshape(nw, nb, r // nb, C), (1, 0, 2, 3)).reshape(nb, nw * r // nb, C)
    return g.reshape(nb, g.shape[0] // nb, g.shape[1])


def _pack(arrs, rows):
    flat = jnp.concatenate([a.reshape(-1).astype(F32) for a in arrs])
    return jnp.concatenate([flat, jnp.zeros((rows * 128 - flat.size,), F32)]).reshape(rows, 128)


def _unpack(vec, shapes):
    flat = vec.reshape(-1)
    out, o = [], 0
    for s in shapes:
        n = int(np.prod(s))
        out.append(flat[o:o + n].reshape(s))
        o += n
    return out


def _rows_for(shapes):
    n = sum(int(np.prod(s)) for s in shapes)
    return -(-n // (8 * 128)) * 8


def kernel(x, c, ctx, c_ctx, w_ada, b_ada, g_mix, w_in, conv_w, conv_b, dt_bias, a_log, d_skip, ssd_norm_w, w_ssd_out, pool_w, pool_scale, w_pool_out, w_out, g_ffn, w_gate_up, w_down, g_final, loss_target, m_c_ctx, m_w_ada, m_b_ada, m_g_mix, m_w_in, m_conv_w, m_conv_b, m_dt_bias, m_a_log, m_d_skip, m_ssd_norm_w, m_w_ssd_out, m_pool_w, m_pool_scale, m_w_pool_out, m_w_out, m_g_ffn, m_w_gate_up, m_w_down, m_g_final, v_c_ctx, v_w_ada, v_b_ada, v_g_mix, v_w_in, v_conv_w, v_conv_b, v_dt_bias, v_a_log, v_d_skip, v_ssd_norm_w, v_w_ssd_out, v_pool_w, v_pool_scale, v_w_pool_out, v_w_out, v_g_ffn, v_w_gate_up, v_w_down, v_g_final):
    w = dict(c_ctx=c_ctx, w_ada=w_ada, b_ada=b_ada, g_mix=g_mix, w_in=w_in, conv_w=conv_w, conv_b=conv_b, dt_bias=dt_bias,
             a_log=a_log, d_skip=d_skip, ssd_norm_w=ssd_norm_w, w_ssd_out=w_ssd_out, pool_w=pool_w, pool_scale=pool_scale,
             w_pool_out=w_pool_out, w_out=w_out, g_ffn=g_ffn, w_gate_up=w_gate_up, w_down=w_down, g_final=g_final)
    m = dict(c_ctx=m_c_ctx, w_ada=m_w_ada, b_ada=m_b_ada, g_mix=m_g_mix, w_in=m_w_in, conv_w=m_conv_w, conv_b=m_conv_b,
             dt_bias=m_dt_bias, a_log=m_a_log, d_skip=m_d_skip, ssd_norm_w=m_ssd_norm_w, w_ssd_out=m_w_ssd_out,
             pool_w=m_pool_w, pool_scale=m_pool_scale, w_pool_out=m_w_pool_out, w_out=m_w_out, g_ffn=m_g_ffn,
             w_gate_up=m_w_gate_up, w_down=m_w_down, g_final=m_g_final)
    v = dict(c_ctx=v_c_ctx, w_ada=v_w_ada, b_ada=v_b_ada, g_mix=v_g_mix, w_in=v_w_in, conv_w=v_conv_w, conv_b=v_conv_b,
             dt_bias=v_dt_bias, a_log=v_a_log, d_skip=v_d_skip, ssd_norm_w=v_ssd_norm_w, w_ssd_out=v_w_ssd_out,
             pool_w=v_pool_w, pool_scale=v_pool_scale, w_pool_out=v_w_pool_out, w_out=v_w_out, g_ffn=v_g_ffn,
             w_gate_up=v_w_gate_up, w_down=v_w_down, g_final=v_g_final)
    assert x.shape[0] == 1, "one example per device"
    pos = _position()
    core = pos[2].astype(jnp.int32).reshape(1)
    n_layers = w_in.shape[0]
    assert n_layers == 2
    dims = (ssd_norm_w.shape[1], conv_w.shape[2] * 4, dt_bias[0].size, pool_scale.shape[1])
    shard = {k: _shard2d(k, w[k]) for k in BIG}

    def halves(a):
        return a.reshape(a.shape[:-2] + (2, a.shape[-2] // 2, a.shape[-1]))

    def whole(a):
        return a.reshape(a.shape[:-3] + (2 * a.shape[-2], a.shape[-1]))

    def wire_shards(l, names):
        return [halves(shard[k][l].astype(MXU_DTYPE)) for k in names]

    def full_weights(names, gathered):
        return {k: _full_from_blocks(k, whole(a)) for k, a in zip(names, gathered)}

    first = comm_call("gather0_chips", gather_chips(wire_shards(0, FIRST_USED), conv=conv_w))
    got0 = full_weights(FIRST_USED, comm_call("gather0_pair", gather_pair(first[:-1])))
    conv_all = first[-1]
    conv_full = [jnp.transpose(conv_all[:, l], (1, 0, 2)).reshape(conv_all.shape[2], -1) for l in range(n_layers)]

    boxes = {}

    def layer_w_fn(l):
        if l == 0:
            full = dict(got0)
            late = {k: None for k in LATER_USED}
        else:
            full = full_weights(BIG, boxes[("fwd", 0)]["gate_up_mm"])
            late = {}
        full["conv_w"] = conv_full[l]
        lw = LazyDict(_prep_layer_weights(*[full[k] if k in full else (None if k in late else w[k][l]) for k in LAYER_KEYS]))
        for i, k in enumerate(late):
            lw[k] = (lambda i=i, k=k: _full_from_blocks(k, whole(boxes[("fwd", 0)]["conv"][i])))
        return lw

    def fwd_hosts(l, box):
        boxes[("fwd", l)] = box
        if l != 0:
            return None
        return {"in_mm": lambda box: gather_chips(wire_shards(0, LATER_USED)), "conv": lambda box: gather_pair(box["in_mm"]),
                "ssd": lambda box: gather_chips(wire_shards(1, BIG)), "gate_up_mm": lambda box: gather_pair(box["ssd"])}

    def blocks(gl, names):
        return [halves(_blocks_from_full(k, gl[k])) for k in names]

    def pair_sums(tag, names, G, recv):
        return [add_own_half(f"pair_sum{tag}_{k}", g, r, core) for k, g, r in zip(names, G, recv)]

    def chip_sums(tag, names, parts):
        return [sum_slots(f"chip_sum{tag}_{k}", p, core) for k, p in zip(names, parts)]

    def reduce_now(tag, gl, names):
        G = blocks(gl, names)
        pair = pair_sums(tag, names, G, comm_call(f"swap{tag}", swap_halves(G)))
        fin = chip_sums(tag, names, comm_call(f"scatter{tag}", scatter_chips(pair)))
        return [whole(a) for a in comm_call(f"share{tag}", share_halves(fin))]

    small_layers = {}
    n_big = len(BIG)

    def bwd_hosts(l, grads, box):
        boxes[("bwd", l)] = box
        if l != 0:
            return None
        gl1 = _unprep_layer_grads(grads[1], dims)
        small_layers[1] = gl1
        G1 = blocks(gl1, BIG)
        early = {}

        def gate_host(box):
            early["G"] = blocks(box["g"], READY_EARLY)
            return swap_halves(early["G"])

        def scan_host(box):
            return combine(scatter_chips(pair_sums("1", BIG, G1, box["down_dx"])),
                           scatter_chips(pair_sums("0e", READY_EARLY, early["G"], box["ssd_gate"])))

        def conv_host(box):
            return combine(share_halves(chip_sums("1", BIG, box["ssd"][:n_big])),
                           share_halves(chip_sums("0e", READY_EARLY, box["ssd"][n_big:])))

        return {"down_dx": lambda box: swap_halves(G1), "ssd_gate": gate_host, "ssd": scan_host, "in_dx": conv_host}

    loss, grad_x, grads, d_c_ctx, d_g_final = local_step(
        x[0], ctx[0], c[0], c_ctx, loss_target[0], layer_w_fn, n_layers, g_final, fwd_hosts, bwd_hosts)
    shared =[whole(a) for a in boxes[("bwd", 0)]["in_dx"]]
    reduced1 = shared[:n_big]
    gl0 = _unprep_layer_grads(grads[0], dims)
    small_layers[0] = gl0
    red0 = dict(zip(READY_EARLY, shared[n_big:]))
    red0.update(zip(READY_LAST, reduce_now("0", gl0, READY_LAST)))
    reduced0 = [red0[k] for k in BIG]

    small_full = dict(c_ctx=d_c_ctx, g_final=d_g_final.reshape(-1))
    for k in SMALL:
        if k not in small_full:
            small_full[k] = jnp.stack([small_layers[l][k] for l in range(n_layers)])
    shapes = [small_full[k].shape for k in SMALL] + [(1,)]
    packed = _pack([small_full[k] for k in SMALL] + [loss.reshape(1)], _rows_for(shapes))
    total = sum_slots("small_sum", comm_call("gather_small", gather_everyone(packed))[0])
    *small_vals, loss = _unpack(total, shapes)
    loss = loss.reshape(())
    small_g = dict(zip(SMALL, small_vals))
    cw = conv_w.shape[2]
    small_g["conv_w"] = lax.dynamic_slice_in_dim(small_g["conv_w"], _chip_index(pos) * cw, cw, axis=2)

    grad, delta, new_m, new_v = {}, {}, {}, {}
    for k, g0, g1 in zip(BIG, reduced0, reduced1):
        shp = w[k].shape
        if k in GRAD_TRANSPOSED:
            flat = lambda a: jnp.swapaxes(a, 1, 2)
            back = lambda a: jnp.swapaxes(a, 1, 2)
        else:
            flat = lambda a: _shard2d(k, a)
            back = lambda a: a.reshape(shp)
        outs = adamw(f"adamw_{k}", flat(w[k]), [g0, g1], flat(m[k]), flat(v[k]))
        grad[k], delta[k], new_m[k], new_v[k] = [back(a) for a in outs]
    flat2 = lambda d: [d[k].reshape(-1, d[k].shape[-1]) for k in SMALL]
    d_, m_, v_ = adamw_small("adamw_small", flat2(w), flat2(small_g), flat2(m), flat2(v))
    for k, dd, mm, vv in zip(SMALL, d_, m_, v_):
        shp = w[k].shape
        grad[k], delta[k], new_m[k], new_v[k] = small_g[k], dd.reshape(shp), mm.reshape(shp), vv.reshape(shp)

    return (loss, grad_x[None], *[grad[k] for k in WEIGHTS], *[delta[k] for k in WEIGHTS],
            *[new_m[k] for k in WEIGHTS], *[new_v[k] for k in WEIGHTS])
```

```python
import functools

import jax
import jax.numpy as jnp
import numpy as np
from jax import lax
from jax.experimental import pallas as pl
from jax.experimental.pallas import tpu as pltpu

F32 = jnp.float32
MXU_DTYPE = jnp.bfloat16
ACT_DTYPE = jnp.bfloat16
VMEM_LIMIT_BYTES = 48 * 1024 * 1024
EPS = 1e-6
NEG = -1e30

SSD_HEADDIM = 64
SSD_GROUPS = 8
SSD_STATE = 128
SSD_CHUNK = 128
SSD_GROUPS_PER_STEP = 8
SSD_CONV = 5
GRID_W = 64
POOL_WINDOWS = (2, 4, 8, 16)
ROW_TILE = 256
DT_PAD = 512

ADAM_LR = 0.001
ADAM_B1 = 0.9
ADAM_B2 = 0.999
ADAM_EPS = 1e-08
ADAM_WD = 0.01
ADAM_STEP = 10

MESH = pl.DeviceIdType.MESH


def _pcall(body, **kw):
    return pl.pallas_call(body, **kw)


def _params(*sem):
    return pltpu.CompilerParams(dimension_semantics=tuple(sem), vmem_limit_bytes=VMEM_LIMIT_BYTES)


def _pick_tile(n, cands):
    for t in cands:
        if n % t == 0:
            return t
    return n


PLANE = ((1, 0, 0), (0, 1, 0), (1, 1, 0))
PAIR = ((0, 0, 1),)
EVERYONE = tuple((a, b, d) for a in (0, 1) for b in (0, 1) for d in (0, 1) if a + b + d)
HBM = pl.BlockSpec(memory_space=pl.ANY)


def _position():
    return lax.axis_index("x"), lax.axis_index("y"), lax.axis_index("c")


def _flip(pos, rel):
    return tuple(1 - p if r else p for p, r in zip(pos, rel))


def _chip_index(pos):
    return 2 * pos[0] + pos[1]


def _device_index(pos):
    return 4 * pos[0] + 2 * pos[1] + pos[2]


class Exchange:
    def __init__(self, copies, n_remote, n_local, operands, out_shapes, aliases=None):
        self.copies, self.n_remote, self.n_local = copies, n_remote, n_local
        self.operands, self.out_shapes, self.aliases = list(operands), list(out_shapes), dict(aliases or {})

    def scratch(self):
        return [pltpu.SemaphoreType.DMA((max(self.n_remote, 1),)), pltpu.SemaphoreType.DMA((max(self.n_remote, 1),)),
                pltpu.SemaphoreType.DMA((max(self.n_local, 1),))]

    def descriptors(self, ins, outs, sems):
        send_sems, recv_sems, local_sems = sems
        local, remote = self.copies(ins, outs, _position())
        assert len(local) == self.n_local and len(remote) == self.n_remote
        cps = [pltpu.make_async_copy(src, dst, local_sems.at[k]) for k, (src, dst) in enumerate(local)]
        cps += [pltpu.make_async_remote_copy(src_ref=src, dst_ref=dst, send_sem=send_sems.at[k], recv_sem=recv_sems.at[k],
                                             device_id=peer, device_id_type=MESH) for k, (src, dst, peer) in enumerate(remote)]
        return cps


def combine(a, b):
    na, nao = len(a.operands), len(a.out_shapes)

    def copies(ins, outs, pos):
        la, ra = a.copies(ins[:na], outs[:nao], pos)
        lb, rb = b.copies(ins[na:], outs[nao:], pos)
        return la + lb, ra + rb

    aliases = dict(a.aliases)
    aliases.update({na + k: nao + v for k, v in b.aliases.items()})
    return Exchange(copies, a.n_remote + b.n_remote, a.n_local + b.n_local, a.operands + b.operands,
                    a.out_shapes + b.out_shapes, aliases)


class LazyDict(dict):
    def __getitem__(self, key):
        v = dict.__getitem__(self, key)
        if callable(v):
            v = v()
            dict.__setitem__(self, key, v)
        return v


def comm_call(name, ex):
    n_in, n_out = len(ex.operands), len(ex.out_shapes)

    def body(*refs):
        cps = ex.descriptors(refs[:n_in], refs[n_in:n_in + n_out], refs[n_in + n_out:])
        for cp in cps:
            cp.start()
        for cp in cps:
            cp.wait()

    return _pcall(
        body, name=name, out_shape=ex.out_shapes, in_specs=[HBM] * n_in, out_specs=[HBM] * n_out,
        scratch_shapes=ex.scratch(), input_output_aliases=ex.aliases,
        compiler_params=pltpu.CompilerParams(has_side_effects=True),
    )(*ex.operands)


def hosted_call(body, ex, operands, *, name, out_shape, grid, in_specs, out_specs, scratch_shapes=()):
    n_in, n_out, n_scr = len(operands), len(out_shape), len(scratch_shapes)
    sem = ("arbitrary",) * len(grid)
    if ex is None:
        res = _pcall(body, name=name, out_shape=list(out_shape), grid=grid, in_specs=list(in_specs),
                     out_specs=list(out_specs), scratch_shapes=list(scratch_shapes), compiler_params=_params(*sem))(*operands)
        return res, []
    x_in, x_out = len(ex.operands), len(ex.out_shapes)

    def wrapped(*refs):
        o = 0
        ins = refs[o:o + n_in]; o += n_in
        xins = refs[o:o + x_in]; o += x_in
        outs = refs[o:o + n_out]; o += n_out
        xouts = refs[o:o + x_out]; o += x_out
        scr = refs[o:o + n_scr]; o += n_scr
        sems = refs[o:]
        first = last = None
        for a, n in enumerate(grid):
            i = pl.program_id(a)
            first = (i == 0) if first is None else first & (i == 0)
            last = (i == n - 1) if last is None else last & (i == n - 1)

        @pl.when(first)
        def _():
            for cp in ex.descriptors(xins, xouts, sems):
                cp.start()

        body(*ins, *outs, *scr)

        @pl.when(last)
        def _():
            for cp in ex.descriptors(xins, xouts, sems):
                cp.wait()

    aliases = {n_in + k: n_out + v for k, v in ex.aliases.items()}
    res = _pcall(
        wrapped, name=name, out_shape=list(out_shape) + ex.out_shapes, grid=grid,
        in_specs=list(in_specs) + [HBM] * x_in, out_specs=list(out_specs) + [HBM] * x_out,
        scratch_shapes=list(scratch_shapes) + ex.scratch(), input_output_aliases=aliases,
        compiler_params=pltpu.CompilerParams(dimension_semantics=sem, vmem_limit_bytes=VMEM_LIMIT_BYTES,
                                             has_side_effects=True),
    )(*operands, *ex.operands)
    return res[:n_out], res[n_out:]


def _dot(a, b, dims):
    return lax.dot_general(a.astype(MXU_DTYPE), b.astype(MXU_DTYPE), (dims, ((), ())), preferred_element_type=F32)


_NN = ((1,), (0,))
_NT = ((1,), (1,))
_TN = ((0,), (0,))


@jax.custom_vjp
def _mm(a, b):
    return _dot(a, b, _NN)


def _mm_fwd(a, b):
    return _mm(a, b), (a, b)


def _mm_bwd(res, g):
    a, b = res
    return _dot(g, b, _NT).astype(a.dtype), _dot(a, g, _TN).astype(b.dtype)


_mm.defvjp(_mm_fwd, _mm_bwd)


@jax.custom_vjp
def _mm_nt(a, b):
    return _dot(a, b, _NT)


def _mm_nt_fwd(a, b):
    return _mm_nt(a, b), (a, b)


def _mm_nt_bwd(res, g):
    a, b = res
    return _dot(g, b, _NN).astype(a.dtype), _dot(g, a, _TN).astype(b.dtype)


_mm_nt.defvjp(_mm_nt_fwd, _mm_nt_bwd)


@jax.custom_vjp
def _mm_tn(a, b):
    return _dot(a, b, _TN)


def _mm_tn_fwd(a, b):
    return _mm_tn(a, b), (a, b)


def _mm_tn_bwd(res, g):
    a, b = res
    return _dot(b, g, _NT).astype(a.dtype), _dot(a, g, _NN).astype(b.dtype)


_mm_tn.defvjp(_mm_tn_fwd, _mm_tn_bwd)


def _dot_exact(m01, v):
    m = m01.astype(jnp.bfloat16)
    hi = v.astype(jnp.bfloat16)
    r1 = v - hi.astype(F32)
    mid = r1.astype(jnp.bfloat16)
    lo = (r1 - mid.astype(F32)).astype(jnp.bfloat16)
    out = jnp.dot(m, hi, preferred_element_type=F32)
    out = out + jnp.dot(m, mid, preferred_element_type=F32)
    return out + jnp.dot(m, lo, preferred_element_type=F32)


@jax.custom_vjp
def _lin01(m, mt, v):
    return _dot_exact(m, v)


def _lin01_fwd(m, mt, v):
    return _dot_exact(m, v), (m, mt)


def _lin01_bwd(res, g):
    m, mt = res
    return jnp.zeros_like(m), jnp.zeros_like(mt), _dot_exact(mt, g)


_lin01.defvjp(_lin01_fwd, _lin01_bwd)


MATMUL_VMEM_BUDGET = VMEM_LIMIT_BYTES * 3 // 4


def _mm_tiles(m, n, k_bytes_a, k_bytes_b, out_bytes, cands_m, cands_n):
    best = None
    for tm in cands_m:
        if m % tm:
            continue
        for tn in cands_n:
            if n % tn:
                continue
            need = 2 * (tm * k_bytes_a + tn * k_bytes_b + tm * tn * out_bytes)
            if need <= MATMUL_VMEM_BUDGET and (best is None or tm * tn > best[0] * best[1]):
                best = (tm, tn)
    assert best is not None, (m, n)
    return best


_ROW_CANDS = (4352, 2176, 1088, 768, 544, 512, 272, 256, 128, 16)
_COL_CANDS = (2816, 2048, 1408, 1024, 512, 256, 128)


def _one(res, xres, ex):
    return res[0] if ex is None else (res[0], xres)


def matmul_nn(name, a, b, out_dtype=F32, ex=None, col0=0, ncols=None):
    M, K = a.shape
    N = b.shape[1] - col0 if ncols is None else ncols
    tm, tn = _mm_tiles(M, N, K * a.dtype.itemsize, K * b.dtype.itemsize, jnp.dtype(out_dtype).itemsize,
                       _ROW_CANDS, (512, 256, 128))
    assert col0 % tn == 0
    first = col0 // tn

    def body(a_ref, b_ref, o_ref):
        o_ref[...] = _dot(a_ref[...], b_ref[...], _NN).astype(o_ref.dtype)

    res, xres = hosted_call(
        body, ex, [a, b], name=name, out_shape=[jax.ShapeDtypeStruct((M, N), out_dtype)], grid=(N // tn, M // tm),
        in_specs=[pl.BlockSpec((tm, K), lambda j, i: (i, 0)), pl.BlockSpec((K, tn), lambda j, i: (0, first + j))],
        out_specs=[pl.BlockSpec((tm, tn), lambda j, i: (i, j))])
    return _one(res, xres, ex)


def matmul_nt(name, g, b, out_dtype=F32, ex=None, offsets=None):
    pieces = list(g) if isinstance(g, (list, tuple)) else [g]
    offsets = list(offsets) if offsets is not None else [0]
    M = pieces[0].shape[0]
    K, N = b.shape
    g_bytes = sum(p.shape[1] * p.dtype.itemsize for p in pieces)
    tm, tk = _mm_tiles(M, K, g_bytes, N * b.dtype.itemsize, jnp.dtype(out_dtype).itemsize, _ROW_CANDS, _COL_CANDS)

    def body(*refs):
        b_ref, o_ref = refs[-2:]
        acc = None
        for g_ref, off in zip(refs[:-2], offsets):
            part = _dot(g_ref[...], b_ref[:, off:off + g_ref.shape[1]], _NT)
            acc = part if acc is None else acc + part
        o_ref[...] = acc.astype(o_ref.dtype)

    res, xres = hosted_call(
        body, ex, pieces + [b], name=name, out_shape=[jax.ShapeDtypeStruct((M, K), out_dtype)], grid=(K // tk, M // tm),
        in_specs=[pl.BlockSpec((tm, p.shape[1]), lambda j, i: (i, 0)) for p in pieces]
        + [pl.BlockSpec((tk, N), lambda j, i: (j, 0))],
        out_specs=[pl.BlockSpec((tm, tk), lambda j, i: (i, j))])
    return _one(res, xres, ex)


def matmul_tn(name, a, g, ex=None):
    M, K = a.shape
    N = g.shape[1]
    tk, tn = _mm_tiles(K, N, M * a.dtype.itemsize, M * g.dtype.itemsize, 4, (512, 256, 128), (512, 256, 128))

    def body(a_ref, g_ref, o_ref):
        o_ref[...] = _dot(a_ref[...], g_ref[...], _TN)

    res, xres = hosted_call(
        body, ex, [a, g], name=name, out_shape=[jax.ShapeDtypeStruct((K, N), F32)], grid=(K // tk, N // tn),
        in_specs=[pl.BlockSpec((M, tk), lambda i, j: (0, i)), pl.BlockSpec((M, tn), lambda i, j: (0, j))],
        out_specs=[pl.BlockSpec((tk, tn), lambda i, j: (i, j))])
    return _one(res, xres, ex)


class Arg:
    def __init__(self, arr, block, imap, kind):
        self.arr, self.block, self.imap, self.kind = arr, block, imap, kind


class Rows:
    def __init__(self, nt, nct, tm, ncol=1):
        self.nt, self.nct, self.tm, self.ncol = nt, nct, tm, ncol

    def seg(self, i):
        return jnp.where(i >= self.nct, 1, 0)

    def spec(self, block, imap):
        return pl.BlockSpec(block, lambda j, i: imap(j, i, self.seg(i)))

    def row(self, arr, width, cb0=0, follow=False, roff=0, stride=1):
        f = stride if follow else 0
        return Arg(arr, (self.tm, width), lambda j, i, s: (i + roff, cb0 + f * j), "row")

    def vec(self, arr, follow=False, kind="acc"):
        w = arr.shape[1] // (self.ncol if follow else 1)
        f = 1 if follow else 0
        return Arg(arr, (1, w), lambda j, i, s: (0, f * j), kind)

    def segvec(self, arr, kind="seg"):
        return Arg(arr, (None, 1, arr.shape[2]), lambda j, i, s: (s, 0, 0), kind)


def _load(ref):
    return ref[...].astype(F32) if ref.dtype != F32 else ref[...]


def stage_fwd(name, f, rows, args, outs):
    n_in = len(args)

    def body(*refs):
        vals = [_load(r) for r in refs[:n_in]]
        res = f(*vals)
        for r, v in zip(refs[n_in:], res):
            r[...] = v.astype(r.dtype)

    T = rows.nt * rows.tm
    out_shape = [jax.ShapeDtypeStruct((T, w * (rows.ncol if fo else 1)), dt) for w, dt, fo in outs]
    out_specs = [pl.BlockSpec((rows.tm, w), (lambda j, i, fo=fo: (i, j if fo else 0))) for w, dt, fo in outs]
    res = _pcall(
        body, name=name, out_shape=out_shape, grid=(rows.ncol, rows.nt),
        in_specs=[rows.spec(a.block, a.imap) for a in args], out_specs=out_specs,
        compiler_params=_params("parallel", "parallel"),
    )(*[a.arr for a in args])
    return res


def stage_bwd(name, f, rows, args, cots, row_dtypes, ex=None, primal=()):
    n_in, n_ct = len(args), len(cots)
    diff = [k for k, a in enumerate(args) if a.kind != "const"]
    row_dt = {}
    for k in diff:
        if args[k].kind == "row":
            row_dt[k] = row_dtypes[len(row_dt)]

    def body(*refs):
        i = pl.program_id(1)
        vals = [_load(r) for r in refs[:n_in]]
        cts = tuple(_load(r) for r in refs[n_in:n_in + n_ct])
        outs = refs[n_in + n_ct:]

        def g(*dv):
            full = list(vals)
            for k, v in zip(diff, dv):
                full[k] = v
            return tuple(f(*full))

        prim, vjp = jax.vjp(g, *[vals[k] for k in diff])
        grads = vjp(cts)
        for o, v in zip(outs[len(diff):], prim):
            o[...] = v.astype(o.dtype)
        for k, o, gr in zip(diff, outs, grads):
            kind = args[k].kind
            if kind == "row":
                o[...] = gr.astype(o.dtype)
            else:
                first = (i == 0) | (i == rows.nct) if kind == "seg" else (i == 0)

                @pl.when(first)
                def _():
                    o[...] = gr.astype(o.dtype)

                @pl.when(jnp.logical_not(first))
                def _():
                    o[...] += gr.astype(o.dtype)

    T = rows.nt * rows.tm
    out_shape, out_specs = [], []
    for k in diff:
        a = args[k]
        if a.kind == "row":
            out_shape.append(jax.ShapeDtypeStruct((T, a.block[1] * (rows.ncol if _follows(a) else 1)), row_dt[k]))
            fo = _follows(a)
            out_specs.append(pl.BlockSpec(a.block, (lambda j, i, fo=fo: (i, j if fo else 0))))
        else:
            out_shape.append(jax.ShapeDtypeStruct(a.arr.shape, F32))
            out_specs.append(rows.spec(a.block, a.imap))
    for w, dt in primal:
        out_shape.append(jax.ShapeDtypeStruct((T, w), dt))
        out_specs.append(pl.BlockSpec((rows.tm, w), lambda j, i: (i, 0)))
    res, xres = hosted_call(
        body, ex, [a.arr for a in list(args) + list(cots)], name=name, out_shape=out_shape, grid=(rows.ncol, rows.nt),
        in_specs=[rows.spec(a.block, a.imap) for a in list(args) + list(cots)], out_specs=out_specs)
    return res if ex is None else (res, xres)


def _follows(a):
    return a.imap(1, 0, 0)[-1] != a.imap(0, 0, 0)[-1]


def _rms(x):
    return x * lax.rsqrt(jnp.mean(x * x, axis=-1, keepdims=True) + EPS)


def f_norm_mod(x, g, sh, sc):
    return ((_rms(x) * g) * (1.0 + sc) + sh,)


def f_resid_norm_mod(x, mo, ga, g, sh, sc):
    x1 = x + ga * mo
    return x1, (_rms(x1) * g) * (1.0 + sc) + sh


def f_resid(x, dn, ga):
    return (x + ga * dn,)


def f_silu(x):
    return (x * jax.nn.sigmoid(x),)


def f_bias(x, b):
    return (x + b,)


def f_ssd_gate(y0, y1, xs, z, dskip, nw):
    y = y0 + y1 + dskip * xs
    return (_rms(y * (z * jax.nn.sigmoid(z))) * nw,)


def f_pool(u, pmat, pmat_t, inv_cnt, pw, scale):
    pm = _lin01(pmat, pmat_t, u) * inv_cnt - u
    return (_mm(pm, pw) * scale,)


def f_merge(o_ssd, o_pool, gl_ssd, gl_pool):
    return (jax.nn.sigmoid(gl_ssd) * o_ssd + jax.nn.sigmoid(gl_pool) * o_pool,)


def _column_splitter(n):
    @jax.custom_vjp
    def split(x):
        w = x.shape[1] // n
        return tuple(x[:, k * w:(k + 1) * w] for k in range(n))

    def fwd(x):
        return split(x), None

    def bwd(_, g):
        return (jnp.concatenate(g, axis=1),)

    split.defvjp(fwd, bwd)
    return split


_halve_cols = _column_splitter(2)
_quarter_cols = _column_splitter(len(POOL_WINDOWS))


def f_swiglu(gu):
    a, b = _halve_cols(gu)
    return ((a * jax.nn.sigmoid(a)) * b,)


def f_pool_all(u, pmat, pmat_t, inv_cnt, scale, *pws):
    outs = [f_pool(part, pmat[k], pmat_t[k], inv_cnt[k], pws[k], 1.0)[0] for k, part in enumerate(_quarter_cols(u))]
    return (jnp.concatenate(outs, axis=1) * scale,)


def f_loss_resid(x1, dn, ga, tgt, g):
    err = _rms(x1 + ga * dn) * g - tgt
    return (0.5 * jnp.mean(err * err, axis=-1, keepdims=True),)


CONV_TILE = 128


def _shift_rows(v, j, n_ctx):
    if j == 0:
        return v
    T = v.shape[0]
    r = lax.broadcasted_iota(jnp.int32, v.shape, 0)
    lo = jnp.where(r >= n_ctx, n_ctx, 0)
    hi = jnp.where(r >= n_ctx, T, n_ctx)
    ok = (r + j >= lo) & (r + j < hi)
    return jnp.where(ok, pltpu.roll(v, (-j) % T, 0), 0.0)


def conv_fwd(name, proj, conv_w, conv_b, n_ctx, width, ex=None):
    T = proj.shape[0]
    half = SSD_CONV // 2

    def body(u_ref, w_ref, b_ref, o_ref):
        u = u_ref[...].astype(F32)
        pre = jnp.broadcast_to(b_ref[...], u.shape)
        for k in range(SSD_CONV):
            pre = pre + w_ref[k:k + 1, :] * _shift_rows(u, k - half, n_ctx)
        o_ref[...] = pre * jax.nn.sigmoid(pre)

    col = lambda t: (0, t)
    res, xres = hosted_call(
        body, ex, [proj, conv_w, conv_b], name=name, out_shape=[jax.ShapeDtypeStruct((T, width), F32)],
        grid=(width // CONV_TILE,),
        in_specs=[pl.BlockSpec((T, CONV_TILE), col), pl.BlockSpec((SSD_CONV, CONV_TILE), col),
                  pl.BlockSpec((1, CONV_TILE), col)],
        out_specs=[pl.BlockSpec((T, CONV_TILE), col)])
    return res[0], xres


def conv_bwd(name, proj, conv_w, conv_b, d_act2, d_skip, n_ctx, width, ex=None):
    T = proj.shape[0]
    half = SSD_CONV // 2

    def body(u_ref, w_ref, b_ref, c0_ref, c1_ref, cs_ref, du_ref, dw_ref, db_ref):
        t = pl.program_id(0)
        u = u_ref[...].astype(F32)
        pre = jnp.broadcast_to(b_ref[...], u.shape)
        for k in range(SSD_CONV):
            pre = pre + w_ref[k:k + 1, :] * _shift_rows(u, k - half, n_ctx)
        sg = jax.nn.sigmoid(pre)
        ct = c0_ref[...].astype(F32) + c1_ref[...].astype(F32) + jnp.where(t % 4 < 2, cs_ref[...].astype(F32), 0.0)
        dpre = ct * (sg * (1.0 + pre * (1.0 - sg)))
        du = jnp.zeros_like(u)
        for k in range(SSD_CONV):
            du = du + w_ref[k:k + 1, :] * _shift_rows(dpre, half - k, n_ctx)
            dw_ref[k:k + 1, :] = jnp.sum(dpre * _shift_rows(u, k - half, n_ctx), axis=0, keepdims=True)
        du_ref[...] = du.astype(du_ref.dtype)
        db_ref[...] = jnp.sum(dpre, axis=0, keepdims=True)

    col = lambda t: (0, t)
    skip_col = lambda t: (0, (t // 4) * 2 + jnp.minimum(t % 4, 1))
    res, xres = hosted_call(
        body, ex, [proj, conv_w, conv_b, d_act2[0], d_act2[1], d_skip], name=name,
        out_shape=[jax.ShapeDtypeStruct((T, width), ACT_DTYPE), jax.ShapeDtypeStruct((SSD_CONV, width), F32),
                   jax.ShapeDtypeStruct((1, width), F32)],
        grid=(width // CONV_TILE,),
        in_specs=[pl.BlockSpec((T, CONV_TILE), col), pl.BlockSpec((SSD_CONV, CONV_TILE), col),
                  pl.BlockSpec((1, CONV_TILE), col), pl.BlockSpec((T, CONV_TILE), col),
                  pl.BlockSpec((T, CONV_TILE), col), pl.BlockSpec((T, CONV_TILE), skip_col)],
        out_specs=[pl.BlockSpec((T, CONV_TILE), col), pl.BlockSpec((SSD_CONV, CONV_TILE), col),
                   pl.BlockSpec((1, CONV_TILE), col)])
    return res[0], res[1], res[2], xres


@jax.custom_vjp
def _cumsum_mat(tri, tri_t, a):
    return jnp.dot(tri, a, precision=lax.Precision.HIGHEST, preferred_element_type=F32)


def _cumsum_fwd(tri, tri_t, a):
    return _cumsum_mat(tri, tri_t, a), (tri, tri_t)


def _cumsum_bwd(res, g):
    tri, tri_t = res
    return (jnp.zeros_like(tri), jnp.zeros_like(tri_t),
            jnp.dot(tri_t, g, precision=lax.Precision.HIGHEST, preferred_element_type=F32))


_cumsum_mat.defvjp(_cumsum_fwd, _cumsum_bwd)


def _ssd_dt(dtraw, dt_bias, a_log, tri, tri_t):
    dt_all = jax.nn.softplus(dtraw + dt_bias)
    a_all = dt_all * (-jnp.exp(a_log))
    return dt_all, a_all, _cumsum_mat(tri, tri_t, a_all)


def _ssd_chunk(xs, bm, cm, dt_all, a_all, s_all, s_in, mask, idx0):
    (xs,), (s_in,) = xs, s_in
    Q = xs.shape[0]
    hpg = xs.shape[1] // SSD_HEADDIM
    lane = lax.broadcasted_iota(jnp.int32, dt_all.shape, 1)
    head = lax.broadcasted_iota(jnp.int32, xs.shape, 1) // SSD_HEADDIM
    head1 = lax.broadcasted_iota(jnp.int32, (1, xs.shape[1]), 1) // SSD_HEADDIM

    def pick(v, r):
        return jnp.sum(jnp.where(lane == idx0 + r, v, 0.0), axis=1, keepdims=True)

    def expand(cols, hd):
        out = cols[hpg - 1]
        for r in range(hpg - 2, -1, -1):
            out = jnp.where(hd == r, cols[r], out)
        return out

    def spread(*cols):
        return expand([jnp.broadcast_to(c, xs.shape) for c in cols], head)

    dt_r = [pick(dt_all, r) for r in range(hpg)]
    s_r = [pick(s_all, r) for r in range(hpg)]
    stot_r = [jnp.sum(jnp.where(lane == idx0 + r, a_all, 0.0), keepdims=True).reshape(1, 1) for r in range(hpg)]

    xd = xs * spread(*dt_r)
    cb = _mm_nt(cm, bm)
    weights, stacked = [], []
    for r in range(hpg):
        sm = jnp.broadcast_to(s_r[r], (Q, Q))
        weights.append(cb * jnp.exp(jnp.where(mask, sm - sm.T, NEG)))
        stacked.append(jnp.where(head == r, xd, 0.0))
    y = spread(*[jnp.exp(c) for c in s_r]) * _mm(cm, s_in)
    y = y + _mm(jnp.concatenate(weights, axis=1), jnp.concatenate(stacked, axis=0))
    to_end = spread(*[jnp.exp(t - c) for t, c in zip(stot_r, s_r)])
    carry = expand([jnp.broadcast_to(jnp.exp(t), (1, xs.shape[1])) for t in stot_r], head1)
    s_out = carry * s_in + _mm_tn(bm, xd * to_end)
    return [y], [s_out]


def _scan_consts():
    q = SSD_CHUNK
    i = np.arange(q)[:, None]
    j = np.arange(q)[None, :]
    fwd = (j <= i).astype(np.float32)
    bwd = (j >= i).astype(np.float32)
    tri = np.stack([fwd, bwd])
    return jnp.asarray(tri), jnp.asarray(np.stack([fwd.T, bwd.T]))


def _chunk_of(d, k, ncc, nc):
    rev = jnp.where(k < ncc, ncc - 1 - k, nc - 1 + ncc - k)
    return jnp.where(d == 0, k, rev)


def ssd_fwd(name, xbc, dtraw, dt_bias, a_log, n_ctx, ex=None):
    T = xbc.shape[0]
    q, G = SSD_CHUNK, SSD_GROUPS
    nc, ncc = T // q, n_ctx // q
    gw = xbc.shape[1] // G
    xw = gw - 2 * SSD_STATE
    hpg = xw // SSD_HEADDIM
    nh = G * hpg
    tri, tri_t = _scan_consts()

    gs = SSD_GROUPS_PER_STEP

    def body(x0_ref, x1_ref, dt0_ref, dt1_ref, bias_ref, alog_ref, tri_ref, trit_ref, y0_ref, y1_ref, sin_ref, state):
        gb, k = pl.program_id(0), pl.program_id(1)

        @pl.when(k == 0)
        def _():
            state[...] = jnp.zeros_like(state)

        for d, (x_ref, dt_ref, y_ref) in enumerate(((x0_ref, dt0_ref, y0_ref), (x1_ref, dt1_ref, y1_ref))):
            tri_v = tri_ref[d]
            dt_all, a_all, s_all = _ssd_dt(dt_ref[...], bias_ref[...], alog_ref[...], tri_v, trit_ref[d])
            for j in range(gs):
                o = j * gw
                sin_ref[d, j] = state[d, j]
                (y,), (s_out,) = _ssd_chunk(
                    [x_ref[:, o:o + xw]], x_ref[:, o + xw:o + xw + SSD_STATE], x_ref[:, o + xw + SSD_STATE:o + gw],
                    dt_all, a_all, s_all, [state[d, j]], tri_v > 0.5, d * nh + (gb * gs + j) * hpg)
                y_ref[:, j * xw:(j + 1) * xw] = y.astype(y_ref.dtype)
                state[d, j] = s_out

    ch = lambda d, k: _chunk_of(d, k, ncc, nc)
    y_shape = jax.ShapeDtypeStruct((T, G * xw), ACT_DTYPE)
    res, xres = hosted_call(
        body, ex, [xbc, xbc, dtraw, dtraw, dt_bias, a_log, tri, tri_t], name=name,
        out_shape=[y_shape, y_shape, jax.ShapeDtypeStruct((2, nc, G, SSD_STATE, xw), F32)],
        grid=(G // gs, nc),
        in_specs=[pl.BlockSpec((q, gs * gw), lambda g, k: (ch(0, k), g)),
                  pl.BlockSpec((q, gs * gw), lambda g, k: (ch(1, k), g)),
                  pl.BlockSpec((q, 128), lambda g, k: (ch(0, k), 0)),
                  pl.BlockSpec((q, 128), lambda g, k: (ch(1, k), 0)),
                  pl.BlockSpec((1, 128), lambda g, k: (0, 0)),
                  pl.BlockSpec((1, 128), lambda g, k: (0, 0)),
                  pl.BlockSpec((2, q, q), lambda g, k: (0, 0, 0)),
                  pl.BlockSpec((2, q, q), lambda g, k: (0, 0, 0))],
        out_specs=[pl.BlockSpec((q, gs * xw), lambda g, k: (ch(0, k), g)),
                   pl.BlockSpec((q, gs * xw), lambda g, k: (ch(1, k), g)),
                   pl.BlockSpec((2, None, gs, SSD_STATE, xw), lambda g, k: (0, k, g, 0, 0))],
        scratch_shapes=[pltpu.VMEM((2, gs, SSD_STATE, xw), F32)])
    return res[0], res[1], res[2], xres


def ssd_bwd(name, xbc, dtraw, dt_bias, a_log, states, dy, n_ctx, ex=None):
    T = xbc.shape[0]
    q, G = SSD_CHUNK, SSD_GROUPS
    nc, ncc = T // q, n_ctx // q
    gw = xbc.shape[1] // G
    xw = gw - 2 * SSD_STATE
    hpg = xw // SSD_HEADDIM
    nh = G * hpg
    tri, tri_t = _scan_consts()

    gs = SSD_GROUPS_PER_STEP

    def body(x0_ref, x1_ref, dt0_ref, dt1_ref, bias_ref, alog_ref, tri_ref, trit_ref, sin_ref, dy0_ref, dy1_ref,
             dx0_ref, dx1_ref, ddt_ref, dbias_ref, dalog_ref, dstate):
        gb, k = pl.program_id(0), pl.program_id(1)

        @pl.when((gb == 0) & (k == 0))
        def _():
            ddt_ref[...] = jnp.zeros_like(ddt_ref)
            dbias_ref[...] = jnp.zeros_like(dbias_ref)
            dalog_ref[...] = jnp.zeros_like(dalog_ref)

        @pl.when(k == 0)
        def _():
            dstate[...] = jnp.zeros_like(dstate)

        tris = [(tri_ref[d], trit_ref[d]) for d in range(2)]
        per = 4

        def fn(bias, alog, dtraw0, dtraw1, *per_group):
            ys, s_outs = [], []
            for d, dtraw in enumerate((dtraw0, dtraw1)):
                tri_v, trit_v = tris[d]
                dt_all, a_all, s_all = _ssd_dt(dtraw, bias, alog, tri_v, trit_v)
                for j in range(gs):
                    xs, bm, cm, s_in = per_group[per * (d * gs + j):per * (d * gs + j + 1)]
                    y, s_out = _ssd_chunk([xs], bm, cm, dt_all, a_all, s_all, [s_in], tri_v > 0.5,
                                          d * nh + (gb * gs + j) * hpg)
                    ys += y
                    s_outs += s_out
            return ys, s_outs

        per_group, dys, dss = [], [], []
        for d, (x_ref, dy_ref) in enumerate(((x0_ref, dy0_ref), (x1_ref, dy1_ref))):
            for j in range(gs):
                o = j * gw
                per_group += [x_ref[:, o:o + xw], x_ref[:, o + xw:o + xw + SSD_STATE], x_ref[:, o + xw + SSD_STATE:o + gw],
                              sin_ref[d, j]]
                dys.append(dy_ref[:, j * xw:(j + 1) * xw].astype(F32))
                dss.append(dstate[d, j])
        _, vjp = jax.vjp(fn, bias_ref[...], alog_ref[...], dt0_ref[...], dt1_ref[...], *per_group)
        cts = vjp((dys, dss))
        dbias, dalog, ddt0, ddt1 = cts[:4]
        for d, dx_ref in enumerate((dx0_ref, dx1_ref)):
            for j in range(gs):
                o = j * gw
                dxs, dbm, dcm, ds_in = cts[4 + per * (d * gs + j):4 + per * (d * gs + j + 1)]
                dx_ref[:, o:o + xw] = dxs.astype(dx_ref.dtype)
                dx_ref[:, o + xw:o + xw + SSD_STATE] = dbm.astype(dx_ref.dtype)
                dx_ref[:, o + xw + SSD_STATE:o + gw] = dcm.astype(dx_ref.dtype)
                dstate[d, j] = ds_in
        for d, ddt in enumerate((ddt0, ddt1)):
            row0 = pl.multiple_of(_chunk_of(d, nc - 1 - k, ncc, nc) * q, q)
            ddt_ref[pl.ds(row0, q), :] += ddt
        dbias_ref[...] += dbias
        dalog_ref[...] += dalog

    ch = lambda d, k: _chunk_of(d, nc - 1 - k, ncc, nc)
    dx_shape = jax.ShapeDtypeStruct((T, G * gw), ACT_DTYPE)
    res, xres = hosted_call(
        body, ex, [xbc, xbc, dtraw, dtraw, dt_bias, a_log, tri, tri_t, states, dy, dy], name=name,
        out_shape=[dx_shape, dx_shape, jax.ShapeDtypeStruct((T, 128), F32),
                   jax.ShapeDtypeStruct((1, 128), F32), jax.ShapeDtypeStruct((1, 128), F32)],
        grid=(G // gs, nc),
        in_specs=[pl.BlockSpec((q, gs * gw), lambda g, k: (ch(0, k), g)),
                  pl.BlockSpec((q, gs * gw), lambda g, k: (ch(1, k), g)),
                  pl.BlockSpec((q, 128), lambda g, k: (ch(0, k), 0)),
                  pl.BlockSpec((q, 128), lambda g, k: (ch(1, k), 0)),
                  pl.BlockSpec((1, 128), lambda g, k: (0, 0)),
                  pl.BlockSpec((1, 128), lambda g, k: (0, 0)),
                  pl.BlockSpec((2, q, q), lambda g, k: (0, 0, 0)),
                  pl.BlockSpec((2, q, q), lambda g, k: (0, 0, 0)),
                  pl.BlockSpec((2, None, gs, SSD_STATE, xw), lambda g, k: (0, nc - 1 - k, g, 0, 0)),
                  pl.BlockSpec((q, gs * xw), lambda g, k: (ch(0, k), g)),
                  pl.BlockSpec((q, gs * xw), lambda g, k: (ch(1, k), g))],
        out_specs=[pl.BlockSpec((q, gs * gw), lambda g, k: (ch(0, k), g)),
                   pl.BlockSpec((q, gs * gw), lambda g, k: (ch(1, k), g)),
                   pl.BlockSpec((T, 128), lambda g, k: (0, 0)),
                   pl.BlockSpec((1, 128), lambda g, k: (0, 0)),
                   pl.BlockSpec((1, 128), lambda g, k: (0, 0))],
        scratch_shapes=[pltpu.VMEM((2, gs, SSD_STATE, xw), F32)])
    return res[0], res[1], res[2], res[3], res[4], xres


def _perm_xbc(a):
    G = SSD_GROUPS
    n = a.shape[-1]
    gn = G * SSD_STATE
    di = n - 2 * gn
    lead = a.shape[:-1]
    xs = a[..., :di].reshape(lead + (G, di // G))
    bm = a[..., di:di + gn].reshape(lead + (G, SSD_STATE))
    cm = a[..., di + gn:].reshape(lead + (G, SSD_STATE))
    return jnp.concatenate([xs, bm, cm], axis=-1).reshape(lead + (n,))


def _unperm_xbc(a):
    G = SSD_GROUPS
    n = a.shape[-1]
    gn = G * SSD_STATE
    di = n - 2 * gn
    lead = a.shape[:-1]
    r = a.reshape(lead + (G, n // G))
    xw = di // G
    return jnp.concatenate([r[..., :xw].reshape(lead + (di,)), r[..., xw:xw + SSD_STATE].reshape(lead + (gn,)),
                            r[..., xw + SSD_STATE:].reshape(lead + (gn,))], axis=-1)


def _pool_consts(tm, n_ctx):
    assert n_ctx == tm and tm % GRID_W == 0
    mats, cnts = [], []
    for seq in (n_ctx, GRID_W):
        t = np.arange(tm)
        tt = t % seq
        base = t - tt
        ms, cs = [], []
        for k in POOL_WINDOWS:
            lo = np.clip(tt - k // 2, 0, seq) + base
            hi = np.clip(tt + k // 2, 0, seq) + base
            m = ((t[None, :] >= lo[:, None]) & (t[None, :] < hi[:, None])).astype(np.float32)
            ms.append(m)
            cs.append((1.0 / (hi - lo).astype(np.float32))[:, None])
        mats.append(np.stack(ms))
        cnts.append(np.stack(cs))
    m = np.stack(mats)
    return jnp.asarray(m), jnp.asarray(np.swapaxes(m, -1, -2)), jnp.asarray(np.stack(cnts).astype(np.float32))


def _prep_layer_weights(w_ada, b_ada, g_mix, w_in, conv_w, conv_b, dt_bias, a_log, d_skip, ssd_norm_w, w_ssd_out,
                        pool_w, pool_scale, w_pool_out, w_out, g_ffn, w_gate_up, w_down):
    D = w_in.shape[0]
    di = ssd_norm_w.shape[0]
    xbc = conv_w.shape[1]
    nh2 = dt_bias.size
    pw = pool_scale.shape[0]
    o = 0
    wz = w_in[:, o:o + di]; o += di
    wx = w_in[:, o:o + xbc]; o += xbc
    wdt = w_in[:, o:o + nh2]; o += nh2
    wp = w_in[:, o:o + pw]; o += pw
    wg = w_in[:, o:]
    w1 = jnp.concatenate([_perm_xbc(wx), wz, wg, wp, wdt, jnp.zeros((D, DT_PAD - nh2), w_in.dtype)], axis=1)
    pad128 = lambda v: jnp.concatenate([v.reshape(1, -1), jnp.zeros((1, 128 - v.size), F32)], axis=1)
    return dict(
        w_ada=w_ada, b_ada=b_ada.reshape(1, -1), g_mix=g_mix.reshape(1, -1), w1=w1,
        conv_w=_perm_xbc(conv_w), conv_b=_perm_xbc(conv_b.reshape(1, -1)),
        dt_bias=pad128(dt_bias), a_log=pad128(a_log),
        dskip=jnp.repeat(d_skip[0] + d_skip[1], SSD_HEADDIM).reshape(1, -1),
        ssd_norm_w=ssd_norm_w.reshape(1, -1), w_ssd_out=w_ssd_out, pool_w=pool_w,
        pool_scale=pool_scale.reshape(1, -1), w_pool_out=w_pool_out, w_out=w_out, g_ffn=g_ffn.reshape(1, -1),
        w_gate_up=w_gate_up, w_down=w_down)


def _unprep_layer_grads(g, dims):
    di, xbc, nh2, pw = dims
    dxbc, dz, dgs, dgp, dp, ddt = g["w1"]
    r = dxbc.reshape(SSD_GROUPS, xbc // SSD_GROUPS, dxbc.shape[1])
    xw = di // SSD_GROUPS
    parts = [r[:, :xw], r[:, xw:xw + SSD_STATE], r[:, xw + SSD_STATE:]]
    w_in_t = jnp.concatenate([dz] + [p.reshape(-1, dxbc.shape[1]) for p in parts] + [ddt[:nh2], dp, dgs, dgp], axis=0)
    nh = nh2 // 2
    dsk = g["dskip"].reshape(nh, SSD_HEADDIM).sum(axis=1)
    return dict(
        w_ada=g["w_ada"], b_ada=g["b_ada"].reshape(-1), g_mix=g["g_mix"].reshape(-1),
        w_in=w_in_t,
        conv_w=_unperm_xbc(g["conv_w"]), conv_b=_unperm_xbc(g["conv_b"]).reshape(-1),
        dt_bias=g["dt_bias"][0, :nh2].reshape(2, nh), a_log=g["a_log"][0, :nh2].reshape(2, nh),
        d_skip=jnp.stack([dsk, dsk]), ssd_norm_w=g["ssd_norm_w"].reshape(-1), w_ssd_out=g["w_ssd_out"],
        pool_w=g["pool_w"], pool_scale=g["pool_scale"].reshape(-1), w_pool_out=g["w_pool_out"], w_out=g["w_out"],
        g_ffn=g["g_ffn"].reshape(-1), w_gate_up=g["w_gate_up"], w_down=g["w_down"])


COND_ROWS = 16


def _split_mods(m):
    d = m.shape[1] // 6
    return [m[:2, k * d:(k + 1) * d].reshape(2, 1, d) for k in range(6)]


def _pool_args(rows, proj, col_block, width, pc, w):
    seg_const = lambda a: Arg(a, (None,) + a.shape[1:], lambda j, i, s: (s, 0, 0, 0), "const")
    pws = [Arg(w["pool_w"][k], w["pool_w"].shape[1:], lambda j, i, s: (0, 0), "acc") for k in range(w["pool_w"].shape[0])]
    return [rows.row(proj, width, col_block)] + [seg_const(a) for a in pc] + [rows.vec(w["pool_scale"])] + pws


TALL_ROW_TILE = 1088


def _tall_rows(T, ncol):
    tm = max(t for t in range(16, min(T, TALL_ROW_TILE) + 1, 16) if T % t == 0)
    return Rows(T // tm, 0, tm, ncol)


def _hosted(hosts, box, key):
    fn = (hosts or {}).get(key)
    return fn(box) if fn else None


def _layer_fwd(l, pre, cond_s, w, rows, n_ctx, pc, hosts=None, box=None):
    T, D = pre[0].shape if isinstance(pre, tuple) else pre.shape
    nt, nct, tm = rows.nt, rows.nct, rows.tm
    n = lambda s: f"l{l}_{s}"
    crow = Rows(1, 0, COND_ROWS)
    mraw = matmul_nn(n("ada_mm"), cond_s, w["w_ada"])
    (m,) = stage_fwd(n("ada_bias"), f_bias, crow, [crow.row(mraw, mraw.shape[1]), crow.vec(w["b_ada"])],
                     [(mraw.shape[1], F32, False)])
    sh1, sc1, ga1, sh2, sc2, ga2 = _split_mods(m)

    if isinstance(pre, tuple):
        x, h1 = stage_fwd(n("norm1"), f_resid_norm_mod, rows, _resid_norm_args(rows, pre, w["g_mix"], sh1, sc1, D),
                          [(D, F32, False), (D, ACT_DTYPE, False)])
    else:
        x = pre
        (h1,) = stage_fwd(n("norm1"), f_norm_mod, rows,
                          [rows.row(x, D), rows.vec(w["g_mix"]), rows.segvec(sh1), rows.segvec(sc1)],
                          [(D, ACT_DTYPE, False)])
    xbc_w = w["conv_w"].shape[1]
    di = w["ssd_norm_w"].shape[1]
    pw = w["pool_scale"].shape[1]
    c_z, c_g, c_p, c_dt = xbc_w, xbc_w + di, xbc_w + di + 2 * pw, xbc_w + di + 3 * pw
    ex = _hosted(hosts, box, "in_mm")
    proj = matmul_nn(n("in_mm"), h1, w["w1"], out_dtype=ACT_DTYPE, ex=ex, ncols=c_dt)
    if ex is not None:
        proj, box["in_mm"] = proj
    dtraw = matmul_nn(n("in_dt_mm"), h1, w["w1"], col0=c_dt, ncols=128)
    ex = _hosted(hosts, box, "conv")
    xbc, xres = conv_fwd(n("conv"), proj, w["conv_w"], w["conv_b"], n_ctx, xbc_w, ex)
    if ex is not None:
        box["conv"] = xres
    ex = _hosted(hosts, box, "ssd")
    y0, y1, states, xres = ssd_fwd(n("ssd"), xbc, dtraw, w["dt_bias"], w["a_log"], n_ctx, ex)
    y2 = (y0, y1)
    if ex is not None:
        box["ssd"] = xres

    G = SSD_GROUPS
    gw = di // G
    r8 = _tall_rows(T, G)
    gate_args = [r8.row(y2[0], gw, 0, True), r8.row(y2[1], gw, 0, True), r8.row(xbc, gw, 0, True, stride=2),
                 r8.row(proj, gw, c_z // gw, True), r8.vec(w["dskip"], True), r8.vec(w["ssd_norm_w"], True)]
    (ynw,) = stage_fwd(n("ssd_gate"), f_ssd_gate, r8, gate_args, [(gw, ACT_DTYPE, True)])
    o_ssd = matmul_nn(n("ssd_out_mm"), ynw, w["w_ssd_out"])

    nw = len(POOL_WINDOWS)
    pg = pw // nw
    (ps,) = stage_fwd(n("pool"), f_pool_all, rows, _pool_args(rows, proj, c_p // pw, pw, pc, w), [(pw, ACT_DTYPE, False)])
    o_pool = matmul_nn(n("pool_out_mm"), ps, w["w_pool_out"])

    merge_args = [rows.row(o_ssd, D), rows.row(o_pool, D), rows.row(proj, pw, c_g // pw), rows.row(proj, pw, c_g // pw + 1)]
    (mg,) = stage_fwd(n("merge"), f_merge, rows, merge_args, [(D, ACT_DTYPE, False)])
    mo = matmul_nn(n("out_mm"), mg, w["w_out"])

    rn_args = [rows.row(x, D), rows.row(mo, D), rows.segvec(ga1), rows.vec(w["g_ffn"]), rows.segvec(sh2), rows.segvec(sc2)]
    x1, h2 = stage_fwd(n("norm2"), f_resid_norm_mod, rows, rn_args, [(D, F32, False), (D, ACT_DTYPE, False)])
    ex = _hosted(hosts, box, "gate_up_mm")
    gu = matmul_nn(n("gate_up_mm"), h2, w["w_gate_up"], ex=ex)
    if ex is not None:
        gu, box["gate_up_mm"] = gu
    fh = gu.shape[1] // 2
    (act,) = stage_fwd(n("swiglu"), f_swiglu, rows, [rows.row(gu, 2 * fh)], [(fh, ACT_DTYPE, False)])
    dn = matmul_nn(n("down_mm"), act, w["w_down"])
    saved = dict(x=x, pre=pre, mraw=mraw, mods=(sh1, sc1, ga1, sh2, sc2, ga2), h1=h1, proj=proj, dtraw=dtraw, xbc=xbc, y2=y2,
                 states=states,
                 ynw=ynw, o_ssd=o_ssd, ps=ps, o_pool=o_pool, mg=mg, mo=mo, x1=x1, h2=h2, gu=gu, act=act, dn=dn,
                 cols=(c_z, c_g, c_p, c_dt))
    return (x1, dn, ga2), saved


def _resid_norm_args(rows, pre, g, sh, sc, D):
    x1, dn, ga2 = pre
    return [rows.row(x1, D), rows.row(dn, D), rows.segvec(ga2), rows.vec(g), rows.segvec(sh), rows.segvec(sc)]


def f_norm_mod_keep(x, g, sh, sc):
    return f_norm_mod(x, g, sh, sc)[0], x


def _layer_bwd(l, cot, cond_s, w, s, rows, n_ctx, pc, hosts=None, box=None):
    dx1, ddn, dga2 = cot
    T, D = dx1.shape
    nt, nct, tm = rows.nt, rows.nct, rows.tm
    n = lambda t: f"l{l}_{t}_bwd"
    sh1, sc1, ga1, sh2, sc2, ga2 = s["mods"]
    c_z, c_g, c_p, c_dt = s["cols"]
    x, proj, xbc, y2, gu = s["x"], s["proj"], s["xbc"], s["y2"], s["gu"]
    g = {}
    if box is not None:
        box["g"] = g

    ex = _hosted(hosts, box, "down_dx")
    dact = matmul_nt(n("down_dx"), ddn, w["w_down"], ex=ex)
    if ex is not None:
        dact, box["down_dx"] = dact
    g["w_down"] = matmul_tn(n("down_dw"), s["act"], ddn)
    fh = gu.shape[1] // 2
    (dgu,) = stage_bwd(n("swiglu"), f_swiglu, rows, [rows.row(gu, 2 * fh)], [rows.row(dact, fh)], [ACT_DTYPE])
    dh2 = matmul_nt(n("gate_up_dx"), dgu, w["w_gate_up"])
    g["w_gate_up"] = matmul_tn(n("gate_up_dw"), s["h2"], dgu)

    rn_args = [rows.row(x, D), rows.row(s["mo"], D), rows.segvec(ga1), rows.vec(w["g_ffn"]), rows.segvec(sh2), rows.segvec(sc2)]
    dxr, dmo, dga1, g["g_ffn"], dsh2, dsc2 = stage_bwd(
        n("norm2"), f_resid_norm_mod, rows, rn_args, [rows.row(dx1, D), rows.row(dh2, D)], [F32, ACT_DTYPE])
    dmg = matmul_nt(n("out_dx"), dmo, w["w_out"])
    g["w_out"] = matmul_tn(n("out_dw"), s["mg"], dmo)

    pw = w["pool_scale"].shape[1]
    merge_args = [rows.row(s["o_ssd"], D), rows.row(s["o_pool"], D), rows.row(proj, pw, c_g // pw), rows.row(proj, pw, c_g // pw + 1)]
    do_ssd, do_pool, dgl_s, dgl_p = stage_bwd(n("merge"), f_merge, rows, merge_args, [rows.row(dmg, D)], [ACT_DTYPE] * 4)
    dps = matmul_nt(n("pool_out_dx"), do_pool, w["w_pool_out"])
    g["w_pool_out"] = matmul_tn(n("pool_out_dw"), s["ps"], do_pool)

    nw = len(POOL_WINDOWS)
    pg = pw // nw
    du_pool, g["pool_scale"], *dpw = stage_bwd(n("pool"), f_pool_all, rows, _pool_args(rows, proj, c_p // pw, pw, pc, w),
                                               [rows.row(dps, pw)], [ACT_DTYPE])
    g["pool_w"] = jnp.stack(dpw)

    dynw = matmul_nt(n("ssd_out_dx"), do_ssd, w["w_ssd_out"])
    g["w_ssd_out"] = matmul_tn(n("ssd_out_dw"), s["ynw"], do_ssd)
    G = SSD_GROUPS
    di = w["ssd_norm_w"].shape[1]
    gw = di // G
    r8 = _tall_rows(T, G)
    gate_args = [r8.row(y2[0], gw, 0, True), r8.row(y2[1], gw, 0, True), r8.row(xbc, gw, 0, True, stride=2),
                 r8.row(proj, gw, c_z // gw, True), r8.vec(w["dskip"], True), r8.vec(w["ssd_norm_w"], True)]
    gate_args[1].kind = "const"
    ex = _hosted(hosts, box, "ssd_gate")
    res = stage_bwd(n("ssd_gate"), f_ssd_gate, r8, gate_args, [r8.row(dynw, gw, 0, True)], [ACT_DTYPE] * 3, ex)
    if ex is not None:
        res, box["ssd_gate"] = res
    dy, dxs_skip, dz, g["dskip"], g["ssd_norm_w"] = res

    ex = _hosted(hosts, box, "ssd")
    dxbc0, dxbc1, ddt, g["dt_bias"], g["a_log"], xres = ssd_bwd(n("ssd"), xbc, s["dtraw"], w["dt_bias"], w["a_log"],
                                                                s["states"], dy, n_ctx, ex)
    dxbc2 = (dxbc0, dxbc1)
    if ex is not None:
        box["ssd"] = xres
    xbc_w = xbc.shape[1]
    ex = _hosted(hosts, box, "conv")
    dxbc_raw, g["conv_w"], g["conv_b"], xres = conv_bwd(n("conv"), proj, w["conv_w"], w["conv_b"], dxbc2, dxs_skip,
                                                         n_ctx, xbc_w, ex)
    if ex is not None:
        box["conv"] = xres
    pieces = [dxbc_raw, dz, dgl_s, dgl_p, du_pool, ddt]
    offsets = [0, c_z, c_g, c_g + pw, c_p, c_dt]
    ex = _hosted(hosts, box, "in_dx")
    dh1 = matmul_nt(n("in_dx"), pieces, w["w1"], ex=ex, offsets=offsets)
    if ex is not None:
        dh1, box["in_dx"] = dh1
    ex = _hosted(hosts, box, "in_dw")
    first = matmul_tn(n("in_dw0"), pieces[0], s["h1"], ex=ex)
    if ex is not None:
        first, box["in_dw"] = first
    g["w1"] = [first] + [matmul_tn(n(f"in_dw{k}"), p, s["h1"]) for k, p in enumerate(pieces) if k]

    if isinstance(s["pre"], tuple):
        dx1p, ddnp, dga2p, g["g_mix"], dsh1, dsc1 = stage_bwd(
            n("norm1"), f_resid_norm_mod, rows, _resid_norm_args(rows, s["pre"], w["g_mix"], sh1, sc1, D),
            [rows.row(dxr, D), rows.row(dh1, D)], [F32, ACT_DTYPE])
        dx = (dx1p, ddnp, dga2p)
    else:
        n1_args = [rows.row(x, D), rows.vec(w["g_mix"]), rows.segvec(sh1), rows.segvec(sc1)]
        dx, g["g_mix"], dsh1, dsc1 = stage_bwd(n("norm1"), f_norm_mod_keep, rows, n1_args,
                                               [rows.row(dh1, D), rows.row(dxr, D)], [F32])

    dm = jnp.concatenate([v.reshape(2, D) for v in (dsh1, dsc1, dga1, dsh2, dsc2, dga2)], axis=1)
    dm = jnp.concatenate([dm, jnp.zeros((COND_ROWS - 2, dm.shape[1]), F32)], axis=0)
    crow = Rows(1, 0, COND_ROWS)
    dmraw, g["b_ada"] = stage_bwd(n("ada_bias"), f_bias, crow, [crow.row(s["mraw"], dm.shape[1]), crow.vec(w["b_ada"])],
                                  [crow.row(dm, dm.shape[1])], [ACT_DTYPE])
    dcs = matmul_nt(n("ada_dx"), dmraw, w["w_ada"])
    g["w_ada"] = matmul_tn(n("ada_dw"), cond_s, dmraw)
    return dx, dcs, g


def local_step(x, ctx, c, c_ctx, target, layer_w_fn, n_layers, g_final, fwd_hosts=None, bwd_hosts=None):
    L, D = x.shape
    n_ctx = ctx.shape[0]
    tm = ROW_TILE
    T = L + n_ctx
    rows = Rows(T // tm, n_ctx // tm, tm)
    pc = _pool_consts(tm, n_ctx)
    xa = jnp.concatenate([ctx, x], axis=0)
    cond = jnp.concatenate([c_ctx.reshape(1, D), c.reshape(1, D), jnp.zeros((COND_ROWS - 2, D), F32)], axis=0)
    crow = Rows(1, 0, COND_ROWS)
    (cond_s,) = stage_fwd("cond_silu", f_silu, crow, [crow.row(cond, D)], [(D, ACT_DTYPE, False)])

    saved, layer_w = [], []
    for l in range(n_layers):
        layer_w.append(layer_w_fn(l))
        box = {}
        xa, s = _layer_fwd(l, xa, cond_s, layer_w[l], rows, n_ctx, pc, fwd_hosts(l, box) if fwd_hosts else None, box)
        saved.append(s)

    x1, dn, ga2 = xa
    rl = Rows(L // tm, 0, tm)
    gf = g_final.reshape(1, D)
    tgt = rl.row(target, D)
    tgt.kind = "const"
    off = n_ctx // tm
    loss_args = [rl.row(x1, D, roff=off), rl.row(dn, D, roff=off), rl.vec(ga2[1]), tgt, rl.vec(gf)]
    ones = jnp.ones((L, 1), F32)
    dx1_lat, ddn_lat, dga2_lat, dgf, loss_rows = stage_bwd("loss", f_loss_resid, rl, loss_args, [rl.row(ones, 1)],
                                                           [F32, ACT_DTYPE], primal=[(1, F32)])
    loss = jnp.sum(loss_rows)
    cot = (jnp.concatenate([jnp.zeros((n_ctx, D), F32), dx1_lat], axis=0),
           jnp.concatenate([jnp.zeros((n_ctx, D), ACT_DTYPE), ddn_lat], axis=0),
           jnp.stack([jnp.zeros((1, D), F32), dga2_lat]))

    grads = [None] * n_layers
    dcs = jnp.zeros((COND_ROWS, D), F32)
    for l in reversed(range(n_layers)):
        box = {}
        hosts = bwd_hosts(l, grads, box) if bwd_hosts else None
        cot, dcs_l, grads[l] = _layer_bwd(l, cot, cond_s, layer_w[l], saved[l], rows, n_ctx, pc, hosts, box)
        dcs = dcs + dcs_l
    dx = cot
    (dcond,) = stage_bwd("cond_silu_bwd", f_silu, crow, [crow.row(cond, D)], [crow.row(dcs, D)], [F32])
    return loss, dx[n_ctx:], grads, dcond[0], dgf


def gather_chips(halves, conv=None):
    n = len(halves)
    ops = list(halves) + ([conv] if conv is not None else [])

    def copies(ins, outs, pos):
        c, me = pos[2], _chip_index(pos)
        pairs = [(s.at[c], o.at[me, c]) for s, o in zip(ins[:n], outs[:n])]
        pairs += [(s, o.at[me]) for s, o in zip(ins[n:], outs[n:])]
        return pairs, [(s, d, _flip(pos, rel)) for rel in PLANE for s, d in pairs]

    shapes = [jax.ShapeDtypeStruct((4,) + s.shape, s.dtype) for s in ops]
    return Exchange(copies, 3 * len(ops), len(ops), ops, shapes)


def gather_pair(gathered):
    n = len(gathered)

    def copies(ins, outs, pos):
        c = pos[2]
        return [], [(s.at[b, c], o.at[b, c], _flip(pos, PAIR[0])) for s, o in zip(ins, outs) for b in range(4)]

    shapes = [jax.ShapeDtypeStruct(g.shape, g.dtype) for g in gathered]
    return Exchange(copies, 4 * n, 0, gathered, shapes, aliases={k: k for k in range(n)})


def swap_halves(grads):
    n = len(grads)

    def copies(ins, outs, pos):
        c = pos[2]
        return [], [(g.at[b, 1 - c], o.at[b], _flip(pos, PAIR[0])) for g, o in zip(ins, outs) for b in range(4)]

    shapes = [jax.ShapeDtypeStruct((g.shape[0],) + g.shape[2:], g.dtype) for g in grads]
    return Exchange(copies, 4 * n, 0, grads, shapes)


def scatter_chips(sums):
    n = len(sums)

    def copies(ins, outs, pos):
        me = _chip_index(pos)
        local = [(p.at[me], o.at[me]) for p, o in zip(ins, outs)]
        remote = []
        for rel in PLANE:
            peer = _flip(pos, rel)
            remote += [(p.at[_chip_index(peer)], o.at[me], peer) for p, o in zip(ins, outs)]
        return local, remote

    shapes = [jax.ShapeDtypeStruct(p.shape, p.dtype) for p in sums]
    return Exchange(copies, 3 * n, n, sums, shapes)


def share_halves(finals):
    n = len(finals)

    def copies(ins, outs, pos):
        c = pos[2]
        return [], [(f.at[c], o.at[c], _flip(pos, PAIR[0])) for f, o in zip(ins, outs)]

    shapes = [jax.ShapeDtypeStruct(f.shape, f.dtype) for f in finals]
    return Exchange(copies, n, 0, finals, shapes, aliases={k: k for k in range(n)})


def gather_everyone(vec):
    def copies(ins, outs, pos):
        me = _device_index(pos)
        (v,), (o,) = ins, outs
        return [(v, o.at[me])], [(v, o.at[me], _flip(pos, rel)) for rel in EVERYONE]

    return Exchange(copies, len(EVERYONE), 1, [vec], [jax.ShapeDtypeStruct((8,) + vec.shape, vec.dtype)])


def _row_tile(rows, cols, n_bufs, mult=8):
    cap = VMEM_LIMIT_BYTES // 2 // (2 * n_bufs * cols * 4)
    for t in range(min(rows, cap) // mult * mult, 0, -mult):
        if rows % t == 0:
            return t
    return rows


def _adamw_update(w, g, m, v):
    nm = ADAM_B1 * m + (1.0 - ADAM_B1) * g
    nv = ADAM_B2 * v + (1.0 - ADAM_B2) * jnp.square(g)
    m_hat = nm / (1.0 - ADAM_B1 ** ADAM_STEP)
    v_hat = nv / (1.0 - ADAM_B2 ** ADAM_STEP)
    return -ADAM_LR * (m_hat / (jnp.sqrt(v_hat) + ADAM_EPS) + ADAM_WD * w), nm, nv


def adamw_small(name, ws, gs, ms, vs):
    n = len(ws)

    def body(*refs):
        ins, outs = refs[:4 * n], refs[4 * n:]
        for k in range(n):
            d, nm, nv = _adamw_update(ins[k][...], ins[n + k][...], ins[2 * n + k][...], ins[3 * n + k][...])
            outs[k][...] = d
            outs[n + k][...] = nm
            outs[2 * n + k][...] = nv

    shapes = [jax.ShapeDtypeStruct(a.shape, F32) for a in ws]
    vmem = pl.BlockSpec(memory_space=pltpu.VMEM)
    res = _pcall(body, name=name, out_shape=shapes * 3, in_specs=[vmem] * (4 * n), out_specs=[vmem] * (3 * n),
                 compiler_params=pltpu.CompilerParams(vmem_limit_bytes=VMEM_LIMIT_BYTES))(*ws, *gs, *ms, *vs)
    return res[:n], res[n:2 * n], res[2 * n:]


WIRE_DTYPE = jnp.bfloat16


def add_own_half(name, grads, recv, c):
    nb, _, R, C = grads.shape
    tr = _row_tile(R, C, 3, mult=16)

    def body(c_ref, g_ref, r_ref, o_ref):
        o_ref[...] = (g_ref[...] + r_ref[...]).astype(o_ref.dtype)

    spec = pl.BlockSpec((None, tr, C), lambda b, i, c_ref: (b, i, 0))
    return _pcall(
        body, name=name, out_shape=jax.ShapeDtypeStruct(recv.shape, WIRE_DTYPE),
        grid_spec=pltpu.PrefetchScalarGridSpec(
            num_scalar_prefetch=1, grid=(nb, R // tr),
            in_specs=[pl.BlockSpec((None, None, tr, C), lambda b, i, c_ref: (b, c_ref[0], i, 0)), spec],
            out_specs=spec),
        compiler_params=_params("parallel", "parallel"),
    )(c, grads, recv)


def sum_slots(name, a, c=None):
    n, R, C = a.shape
    tr = _row_tile(R, C, n + 1, mult=16 if a.dtype.itemsize == 2 else 8)

    def body(*refs):
        a_ref, o_ref = refs[-2:]
        acc = a_ref[0].astype(F32)
        for k in range(1, n):
            acc = acc + a_ref[k].astype(F32)
        o_ref[...] = acc

    if c is None:
        return _pcall(
            body, name=name, out_shape=jax.ShapeDtypeStruct((R, C), F32), grid=(R // tr,),
            in_specs=[pl.BlockSpec((n, tr, C), lambda i: (0, i, 0))], out_specs=pl.BlockSpec((tr, C), lambda i: (i, 0)),
            compiler_params=_params("parallel"),
        )(a)
    return _pcall(
        body, name=name, out_shape=jax.ShapeDtypeStruct((2, R, C), F32),
        grid_spec=pltpu.PrefetchScalarGridSpec(
            num_scalar_prefetch=1, grid=(R // tr,),
            in_specs=[pl.BlockSpec((n, tr, C), lambda i, c_ref: (0, i, 0))],
            out_specs=pl.BlockSpec((None, tr, C), lambda i, c_ref: (c_ref[0], i, 0))),
        compiler_params=_params("parallel"),
    )(c, a)


def adamw(name, w, g_layers, m, v):
    nl, R, C = w.shape
    assert len(g_layers) == nl
    tr = _row_tile(R, C, 8 + nl)
    nr = R // tr

    def body(*refs):
        w_ref, m_ref, v_ref = refs[:3]
        g_refs = refs[3:3 + nl]
        go_ref, d_ref, nm_ref, nv_ref = refs[3 + nl:]
        l = pl.program_id(0)
        gr = g_refs[0][...]
        for k in range(1, nl):
            gr = jnp.where(l == k, g_refs[k][...], gr)
        d_ref[...], nm_ref[...], nv_ref[...] = _adamw_update(w_ref[...], gr, m_ref[...], v_ref[...])
        go_ref[...] = gr

    spec = pl.BlockSpec((None, tr, C), lambda l, i: (l, i, 0))
    g_specs = [pl.BlockSpec((tr, C), (lambda l, i, k=k: (jnp.where(l == k, i, jnp.where(l < k, 0, nr - 1)), 0)))
               for k in range(nl)]
    return _pcall(
        body, name=name, out_shape=[jax.ShapeDtypeStruct((nl, R, C), F32)] * 4, grid=(nl, nr),
        in_specs=[spec] * 3 + g_specs, out_specs=[spec] * 4, compiler_params=_params("arbitrary", "arbitrary"),
    )(w, m, v, *g_layers)


BIG = ("w_ada", "w_in", "w_ssd_out", "pool_w", "w_pool_out", "w_out", "w_gate_up", "w_down")
COL_SHARDED = ("w_ada", "w_in", "w_gate_up")
GRAD_TRANSPOSED = ("w_in",)
FIRST_USED = ("w_ada", "w_in")
LATER_USED = tuple(k for k in BIG if k not in FIRST_USED)
READY_LAST = FIRST_USED
READY_EARLY = LATER_USED
SMALL = ("c_ctx", "b_ada", "g_mix", "conv_w", "conv_b", "dt_bias", "a_log", "d_skip", "ssd_norm_w", "pool_scale",
         "g_ffn", "g_final")
WEIGHTS = ("c_ctx", "w_ada", "b_ada", "g_mix", "w_in", "conv_w", "conv_b", "dt_bias", "a_log", "d_skip", "ssd_norm_w",
           "w_ssd_out", "pool_w", "pool_scale", "w_pool_out", "w_out", "g_ffn", "w_gate_up", "w_down", "g_final")
LAYER_KEYS = ("w_ada", "b_ada", "g_mix", "w_in", "conv_w", "conv_b", "dt_bias", "a_log", "d_skip", "ssd_norm_w",
              "w_ssd_out", "pool_w", "pool_scale", "w_pool_out", "w_out", "g_ffn", "w_gate_up", "w_down")


def _shard2d(name, a):
    if name == "pool_w":
        return a.reshape(a.shape[0], a.shape[1] * a.shape[2], a.shape[3])
    return a


def _full_from_blocks(name, a):
    nb, R, C = a.shape
    if name in COL_SHARDED:
        return jnp.transpose(a, (1, 0, 2)).reshape(R, nb * C)
    if name == "pool_w":
        nw = len(POOL_WINDOWS)
        return jnp.transpose(a.reshape(nb, nw, R // nw, C), (1, 0, 2, 3)).reshape(nw, nb * R // nw, C)
    return a.reshape(nb * R, C)


def _blocks_from_full(name, g):
    nb = 4
    if name in COL_SHARDED and name not in GRAD_TRANSPOSED:
        K, N = g.shape
        return jnp.transpose(g.reshape(K, nb, N // nb), (1, 0, 2))
    if name == "pool_w":
        nw, r, C = g.shape
        return jnp.transpose(g.reshape(nw, nb, r // nb, C), (1, 0, 2, 3)).reshape(nb, nw * r // nb, C)
    return g.reshape(nb, g.shape[0] // nb, g.shape[1])


def _pack(arrs, rows):
    flat = jnp.concatenate([a.reshape(-1).astype(F32) for a in arrs])
    return jnp.concatenate([flat, jnp.zeros((rows * 128 - flat.size,), F32)]).reshape(rows, 128)


def _unpack(vec, shapes):
    flat = vec.reshape(-1)
    out, o = [], 0
    for s in shapes:
        n = int(np.prod(s))
        out.append(flat[o:o + n].reshape(s))
        o += n
    return out


def _rows_for(shapes):
    n = sum(int(np.prod(s)) for s in shapes)
    return -(-n // (8 * 128)) * 8


def kernel(x, c, ctx, c_ctx, w_ada, b_ada, g_mix, w_in, conv_w, conv_b, dt_bias, a_log, d_skip, ssd_norm_w, w_ssd_out, pool_w, pool_scale, w_pool_out, w_out, g_ffn, w_gate_up, w_down, g_final, loss_target, m_c_ctx, m_w_ada, m_b_ada, m_g_mix, m_w_in, m_conv_w, m_conv_b, m_dt_bias, m_a_log, m_d_skip, m_ssd_norm_w, m_w_ssd_out, m_pool_w, m_pool_scale, m_w_pool_out, m_w_out, m_g_ffn, m_w_gate_up, m_w_down, m_g_final, v_c_ctx, v_w_ada, v_b_ada, v_g_mix, v_w_in, v_conv_w, v_conv_b, v_dt_bias, v_a_log, v_d_skip, v_ssd_norm_w, v_w_ssd_out, v_pool_w, v_pool_scale, v_w_pool_out, v_w_out, v_g_ffn, v_w_gate_up, v_w_down, v_g_final):
    w = dict(c_ctx=c_ctx, w_ada=w_ada, b_ada=b_ada, g_mix=g_mix, w_in=w_in, conv_w=conv_w, conv_b=conv_b, dt_bias=dt_bias,
             a_log=a_log, d_skip=d_skip, ssd_norm_w=ssd_norm_w, w_ssd_out=w_ssd_out, pool_w=pool_w, pool_scale=pool_scale,
             w_pool_out=w_pool_out, w_out=w_out, g_ffn=g_ffn, w_gate_up=w_gate_up, w_down=w_down, g_final=g_final)
    m = dict(c_ctx=m_c_ctx, w_ada=m_w_ada, b_ada=m_b_ada, g_mix=m_g_mix, w_in=m_w_in, conv_w=m_conv_w, conv_b=m_conv_b,
             dt_bias=m_dt_bias, a_log=m_a_log, d_skip=m_d_skip, ssd_norm_w=m_ssd_norm_w, w_ssd_out=m_w_ssd_out,
             pool_w=m_pool_w, pool_scale=m_pool_scale, w_pool_out=m_w_pool_out, w_out=m_w_out, g_ffn=m_g_ffn,
             w_gate_up=m_w_gate_up, w_down=m_w_down, g_final=m_g_final)
    v = dict(c_ctx=v_c_ctx, w_ada=v_w_ada, b_ada=v_b_ada, g_mix=v_g_mix, w_in=v_w_in, conv_w=v_conv_w, conv_b=v_conv_b,
             dt_bias=v_dt_bias, a_log=v_a_log, d_skip=v_d_skip, ssd_norm_w=v_ssd_norm_w, w_ssd_out=v_w_ssd_out,
             pool_w=v_pool_w, pool_scale=v_pool_scale, w_pool_out=v_w_pool_out, w_out=v_w_out, g_ffn=v_g_ffn,
             w_gate_up=v_w_gate_up, w_down=v_w_down, g_final=v_g_final)
    assert x.shape[0] == 1, "one example per device"
    pos = _position()
    core = pos[2].astype(jnp.int32).reshape(1)
    n_layers = w_in.shape[0]
    assert n_layers == 2
    dims = (ssd_norm_w.shape[1], conv_w.shape[2] * 4, dt_bias[0].size, pool_scale.shape[1])
    shard = {k: _shard2d(k, w[k]) for k in BIG}

    def halves(a):
        return a.reshape(a.shape[:-2] + (2, a.shape[-2] // 2, a.shape[-1]))

    def whole(a):
        return a.reshape(a.shape[:-3] + (2 * a.shape[-2], a.shape[-1]))

    def wire_shards(l, names):
        return [halves(shard[k][l].astype(MXU_DTYPE)) for k in names]

    def full_weights(names, gathered):
        return {k: _full_from_blocks(k, whole(a)) for k, a in zip(names, gathered)}

    first = comm_call("gather0_chips", gather_chips(wire_shards(0, FIRST_USED), conv=conv_w))
    got0 = full_weights(FIRST_USED, comm_call("gather0_pair", gather_pair(first[:-1])))
    conv_all = first[-1]
    conv_full = [jnp.transpose(conv_all[:, l], (1, 0, 2)).reshape(conv_all.shape[2], -1) for l in range(n_layers)]

    boxes = {}

    def layer_w_fn(l):
        if l == 0:
            full = dict(got0)
            late = {k: None for k in LATER_USED}
        else:
            full = full_weights(BIG, boxes[("fwd", 0)]["gate_up_mm"])
            late = {}
        full["conv_w"] = conv_full[l]
        lw = LazyDict(_prep_layer_weights(*[full[k] if k in full else (None if k in late else w[k][l]) for k in LAYER_KEYS]))
        for i, k in enumerate(late):
            lw[k] = (lambda i=i, k=k: _full_from_blocks(k, whole(boxes[("fwd", 0)]["conv"][i])))
        return lw

    def fwd_hosts(l, box):
        boxes[("fwd", l)] = box
        if l != 0:
            return None
        return {"in_mm": lambda box: gather_chips(wire_shards(0, LATER_USED)), "conv": lambda box: gather_pair(box["in_mm"]),
                "ssd": lambda box: gather_chips(wire_shards(1, BIG)), "gate_up_mm": lambda box: gather_pair(box["ssd"])}

    def blocks(gl, names):
        return [halves(_blocks_from_full(k, gl[k])) for k in names]

    def pair_sums(tag, names, G, recv):
        return [add_own_half(f"pair_sum{tag}_{k}", g, r, core) for k, g, r in zip(names, G, recv)]

    def chip_sums(tag, names, parts):
        return [sum_slots(f"chip_sum{tag}_{k}", p, core) for k, p in zip(names, parts)]

    def reduce_now(tag, gl, names):
        G = blocks(gl, names)
        pair = pair_sums(tag, names, G, comm_call(f"swap{tag}", swap_halves(G)))
        fin = chip_sums(tag, names, comm_call(f"scatter{tag}", scatter_chips(pair)))
        return [whole(a) for a in comm_call(f"share{tag}", share_halves(fin))]

    small_layers = {}
    n_big = len(BIG)

    def bwd_hosts(l, grads, box):
        boxes[("bwd", l)] = box
        if l != 0:
            return None
        gl1 = _unprep_layer_grads(grads[1], dims)
        small_layers[1] = gl1
        G1 = blocks(gl1, BIG)
        early = {}

        def gate_host(box):
            early["G"] = blocks(box["g"], READY_EARLY)
            return swap_halves(early["G"])

        def scan_host(box):
            return combine(scatter_chips(pair_sums("1", BIG, G1, box["down_dx"])),
                           scatter_chips(pair_sums("0e", READY_EARLY, early["G"], box["ssd_gate"])))

        def conv_host(box):
            return combine(share_halves(chip_sums("1", BIG, box["ssd"][:n_big])),
                           share_halves(chip_sums("0e", READY_EARLY, box["ssd"][n_big:])))

        return {"down_dx": lambda box: swap_halves(G1), "ssd_gate": gate_host, "ssd": scan_host, "in_dx": conv_host}

    loss, grad_x, grads, d_c_ctx, d_g_final = local_step(
        x[0], ctx[0], c[0], c_ctx, loss_target[0], layer_w_fn, n_layers, g_final, fwd_hosts, bwd_hosts)
    shared =[whole(a) for a in boxes[("bwd", 0)]["in_dx"]]
    reduced1 = shared[:n_big]
    gl0 = _unprep_layer_grads(grads[0], dims)
    small_layers[0] = gl0
    red0 = dict(zip(READY_EARLY, shared[n_big:]))
    red0.update(zip(READY_LAST, reduce_now("0", gl0, READY_LAST)))
    reduced0 = [red0[k] for k in BIG]

    small_full = dict(c_ctx=d_c_ctx, g_final=d_g_final.reshape(-1))
    for k in SMALL:
        if k not in small_full:
            small_full[k] = jnp.stack([small_layers[l][k] for l in range(n_layers)])
    shapes = [small_full[k].shape for k in SMALL] + [(1,)]
    packed = _pack([small_full[k] for k in SMALL] + [loss.reshape(1)], _rows_for(shapes))
    total = sum_slots("small_sum", comm_call("gather_small", gather_everyone(packed))[0])
    *small_vals, loss = _unpack(total, shapes)
    loss = loss.reshape(())
    small_g = dict(zip(SMALL, small_vals))
    cw = conv_w.shape[2]
    small_g["conv_w"] = lax.dynamic_slice_in_dim(small_g["conv_w"], _chip_index(pos) * cw, cw, axis=2)

    grad, delta, new_m, new_v = {}, {}, {}, {}
    for k, g0, g1 in zip(BIG, reduced0, reduced1):
        shp = w[k].shape
        if k in GRAD_TRANSPOSED:
            flat = lambda a: jnp.swapaxes(a, 1, 2)
            back = lambda a: jnp.swapaxes(a, 1, 2)
        else:
            flat = lambda a: _shard2d(k, a)
            back = lambda a: a.reshape(shp)
        outs = adamw(f"adamw_{k}", flat(w[k]), [g0, g1], flat(m[k]), flat(v[k]))
        grad[k], delta[k], new_m[k], new_v[k] = [back(a) for a in outs]
    flat2 = lambda d: [d[k].reshape(-1, d[k].shape[-1]) for k in SMALL]
    d_, m_, v_ = adamw_small("adamw_small", flat2(w), flat2(small_g), flat2(m), flat2(v))
    for k, dd, mm, vv in zip(SMALL, d_, m_, v_):
        shp = w[k].shape
        grad[k], delta[k], new_m[k], new_v[k] = small_g[k], dd.reshape(shp), mm.reshape(shp), vv.reshape(shp)

    return (loss, grad_x[None], *[grad[k] for k in WEIGHTS], *[delta[k] for k in WEIGHTS],
            *[new_m[k] for k in WEIGHTS], *[new_v[k] for k in WEIGHTS])
```

```python
import functools

import jax
import jax.numpy as jnp
import numpy as np
from jax import lax
from jax.experimental import pallas as pl
from jax.experimental.pallas import tpu as pltpu

F32 = jnp.float32
MXU_DTYPE = jnp.bfloat16
ACT_DTYPE = jnp.bfloat16
VMEM_LIMIT_BYTES = 48 * 1024 * 1024
EPS = 1e-6
NEG = -1e30

SSD_HEADDIM = 64
SSD_GROUPS = 8
SSD_STATE = 128
SSD_CHUNK = 128
SSD_GROUPS_PER_STEP = 8
SSD_CONV = 5
GRID_W = 64
POOL_WINDOWS = (2, 4, 8, 16)
ROW_TILE = 256
DT_PAD = 512

ADAM_LR = 0.001
ADAM_B1 = 0.9
ADAM_B2 = 0.999
ADAM_EPS = 1e-08
ADAM_WD = 0.01
ADAM_STEP = 10

MESH = pl.DeviceIdType.MESH


def _pcall(body, **kw):
    return pl.pallas_call(body, **kw)


def _params(*sem):
    return pltpu.CompilerParams(dimension_semantics=tuple(sem), vmem_limit_bytes=VMEM_LIMIT_BYTES)


def _pick_tile(n, cands):
    for t in cands:
        if n % t == 0:
            return t
    return n


PLANE = ((1, 0, 0), (0, 1, 0), (1, 1, 0))
PAIR = ((0, 0, 1),)
EVERYONE = tuple((a, b, d) for a in (0, 1) for b in (0, 1) for d in (0, 1) if a + b + d)
HBM = pl.BlockSpec(memory_space=pl.ANY)


def _position():
    return lax.axis_index("x"), lax.axis_index("y"), lax.axis_index("c")


def _flip(pos, rel):
    return tuple(1 - p if r else p for p, r in zip(pos, rel))


def _chip_index(pos):
    return 2 * pos[0] + pos[1]


def _device_index(pos):
    return 4 * pos[0] + 2 * pos[1] + pos[2]


class Exchange:
    def __init__(self, copies, n_remote, n_local, operands, out_shapes, aliases=None):
        self.copies, self.n_remote, self.n_local = copies, n_remote, n_local
        self.operands, self.out_shapes, self.aliases = list(operands), list(out_shapes), dict(aliases or {})

    def scratch(self):
        return [pltpu.SemaphoreType.DMA((max(self.n_remote, 1),)), pltpu.SemaphoreType.DMA((max(self.n_remote, 1),)),
                pltpu.SemaphoreType.DMA((max(self.n_local, 1),))]

    def descriptors(self, ins, outs, sems):
        send_sems, recv_sems, local_sems = sems
        local, remote = self.copies(ins, outs, _position())
        assert len(local) == self.n_local and len(remote) == self.n_remote
        cps = [pltpu.make_async_copy(src, dst, local_sems.at[k]) for k, (src, dst) in enumerate(local)]
        cps += [pltpu.make_async_remote_copy(src_ref=src, dst_ref=dst, send_sem=send_sems.at[k], recv_sem=recv_sems.at[k],
                                             device_id=peer, device_id_type=MESH) for k, (src, dst, peer) in enumerate(remote)]
        return cps


def combine(a, b):
    na, nao = len(a.operands), len(a.out_shapes)

    def copies(ins, outs, pos):
        la, ra = a.copies(ins[:na], outs[:nao], pos)
        lb, rb = b.copies(ins[na:], outs[nao:], pos)
        return la + lb, ra + rb

    aliases = dict(a.aliases)
    aliases.update({na + k: nao + v for k, v in b.aliases.items()})
    return Exchange(copies, a.n_remote + b.n_remote, a.n_local + b.n_local, a.operands + b.operands,
                    a.out_shapes + b.out_shapes, aliases)


class LazyDict(dict):
    def __getitem__(self, key):
        v = dict.__getitem__(self, key)
        if callable(v):
            v = v()
            dict.__setitem__(self, key, v)
        return v


def comm_call(name, ex):
    n_in, n_out = len(ex.operands), len(ex.out_shapes)

    def body(*refs):
        cps = ex.descriptors(refs[:n_in], refs[n_in:n_in + n_out], refs[n_in + n_out:])
        for cp in cps:
            cp.start()
        for cp in cps:
            cp.wait()

    return _pcall(
        body, name=name, out_shape=ex.out_shapes, in_specs=[HBM] * n_in, out_specs=[HBM] * n_out,
        scratch_shapes=ex.scratch(), input_output_aliases=ex.aliases,
        compiler_params=pltpu.CompilerParams(has_side_effects=True),
    )(*ex.operands)


def hosted_call(body, ex, operands, *, name, out_shape, grid, in_specs, out_specs, scratch_shapes=()):
    n_in, n_out, n_scr = len(operands), len(out_shape), len(scratch_shapes)
    sem = ("arbitrary",) * len(grid)
    if ex is None:
        res = _pcall(body, name=name, out_shape=list(out_shape), grid=grid, in_specs=list(in_specs),
                     out_specs=list(out_specs), scratch_shapes=list(scratch_shapes), compiler_params=_params(*sem))(*operands)
        return res, []
    x_in, x_out = len(ex.operands), len(ex.out_shapes)

    def wrapped(*refs):
        o = 0
        ins = refs[o:o + n_in]; o += n_in
        xins = refs[o:o + x_in]; o += x_in
        outs = refs[o:o + n_out]; o += n_out
        xouts = refs[o:o + x_out]; o += x_out
        scr = refs[o:o + n_scr]; o += n_scr
        sems = refs[o:]
        first = last = None
        for a, n in enumerate(grid):
            i = pl.program_id(a)
            first = (i == 0) if first is None else first & (i == 0)
            last = (i == n - 1) if last is None else last & (i == n - 1)

        @pl.when(first)
        def _():
            for cp in ex.descriptors(xins, xouts, sems):
                cp.start()

        body(*ins, *outs, *scr)

        @pl.when(last)
        def _():
            for cp in ex.descriptors(xins, xouts, sems):
                cp.wait()

    aliases = {n_in + k: n_out + v for k, v in ex.aliases.items()}
    res = _pcall(
        wrapped, name=name, out_shape=list(out_shape) + ex.out_shapes, grid=grid,
        in_specs=list(in_specs) + [HBM] * x_in, out_specs=list(out_specs) + [HBM] * x_out,
        scratch_shapes=list(scratch_shapes) + ex.scratch(), input_output_aliases=aliases,
        compiler_params=pltpu.CompilerParams(dimension_semantics=sem, vmem_limit_bytes=VMEM_LIMIT_BYTES,
                                             has_side_effects=True),
    )(*operands, *ex.operands)
    return res[:n_out], res[n_out:]


def _dot(a, b, dims):
    return lax.dot_general(a.astype(MXU_DTYPE), b.astype(MXU_DTYPE), (dims, ((), ())), preferred_element_type=F32)


_NN = ((1,), (0,))
_NT = ((1,), (1,))
_TN = ((0,), (0,))


@jax.custom_vjp
def _mm(a, b):
    return _dot(a, b, _NN)


def _mm_fwd(a, b):
    return _mm(a, b), (a, b)


def _mm_bwd(res, g):
    a, b = res
    return _dot(g, b, _NT).astype(a.dtype), _dot(a, g, _TN).astype(b.dtype)


_mm.defvjp(_mm_fwd, _mm_bwd)


@jax.custom_vjp
def _mm_nt(a, b):
    return _dot(a, b, _NT)


def _mm_nt_fwd(a, b):
    return _mm_nt(a, b), (a, b)


def _mm_nt_bwd(res, g):
    a, b = res
    return _dot(g, b, _NN).astype(a.dtype), _dot(g, a, _TN).astype(b.dtype)


_mm_nt.defvjp(_mm_nt_fwd, _mm_nt_bwd)


@jax.custom_vjp
def _mm_tn(a, b):
    return _dot(a, b, _TN)


def _mm_tn_fwd(a, b):
    return _mm_tn(a, b), (a, b)


def _mm_tn_bwd(res, g):
    a, b = res
    return _dot(b, g, _NT).astype(a.dtype), _dot(a, g, _NN).astype(b.dtype)


_mm_tn.defvjp(_mm_tn_fwd, _mm_tn_bwd)


def _dot_exact(m01, v):
    m = m01.astype(jnp.bfloat16)
    hi = v.astype(jnp.bfloat16)
    r1 = v - hi.astype(F32)
    mid = r1.astype(jnp.bfloat16)
    lo = (r1 - mid.astype(F32)).astype(jnp.bfloat16)
    out = jnp.dot(m, hi, preferred_element_type=F32)
    out = out + jnp.dot(m, mid, preferred_element_type=F32)
    return out + jnp.dot(m, lo, preferred_element_type=F32)


@jax.custom_vjp
def _lin01(m, mt, v):
    return _dot_exact(m, v)


def _lin01_fwd(m, mt, v):
    return _dot_exact(m, v), (m, mt)


def _lin01_bwd(res, g):
    m, mt = res
    return jnp.zeros_like(m), jnp.zeros_like(mt), _dot_exact(mt, g)


_lin01.defvjp(_lin01_fwd, _lin01_bwd)


MATMUL_VMEM_BUDGET = VMEM_LIMIT_BYTES * 3 // 4


def _mm_tiles(m, n, k_bytes_a, k_bytes_b, out_bytes, cands_m, cands_n):
    best = None
    for tm in cands_m:
        if m % tm:
            continue
        for tn in cands_n:
            if n % tn:
                continue
            need = 2 * (tm * k_bytes_a + tn * k_bytes_b + tm * tn * out_bytes)
            if need <= MATMUL_VMEM_BUDGET and (best is None or tm * tn > best[0] * best[1]):
                best = (tm, tn)
    assert best is not None, (m, n)
    return best


_ROW_CANDS = (4352, 2176, 1088, 768, 544, 512, 272, 256, 128, 16)
_COL_CANDS = (2816, 2048, 1408, 1024, 512, 256, 128)


def _one(res, xres, ex):
    return res[0] if ex is None else (res[0], xres)


def _block_cands(c):
    return (c,) + tuple(t for t in (512, 256, 128) if c % t == 0)


def matmul_nn(name, a, b, out_dtype=F32, ex=None, col0=0, ncols=None):
    M, K = a.shape
    if b.ndim == 3:
        nb, _, C = b.shape
        N, cands = nb * C, _block_cands(C)
    else:
        N, cands = (b.shape[1] - col0 if ncols is None else ncols), (512, 256, 128)
    tm, tn = _mm_tiles(M, N, K * a.dtype.itemsize, K * b.dtype.itemsize, jnp.dtype(out_dtype).itemsize,
                       _ROW_CANDS, cands)
    if b.ndim == 3:
        per = C // tn
        b_spec = pl.BlockSpec((None, K, tn), lambda j, i: (j // per, 0, j % per))
    else:
        assert col0 % tn == 0
        first = col0 // tn
        b_spec = pl.BlockSpec((K, tn), lambda j, i: (0, first + j))

    def body(a_ref, b_ref, o_ref):
        o_ref[...] = _dot(a_ref[...], b_ref[...], _NN).astype(o_ref.dtype)

    res, xres = hosted_call(
        body, ex, [a, b], name=name, out_shape=[jax.ShapeDtypeStruct((M, N), out_dtype)], grid=(N // tn, M // tm),
        in_specs=[pl.BlockSpec((tm, K), lambda j, i: (i, 0)), b_spec],
        out_specs=[pl.BlockSpec((tm, tn), lambda j, i: (i, j))])
    return _one(res, xres, ex)


def matmul_nt(name, g, b, out_dtype=F32, ex=None, offsets=None):
    pieces = list(g) if isinstance(g, (list, tuple)) else [g]
    offsets = list(offsets) if offsets is not None else [0]
    M = pieces[0].shape[0]
    if b.ndim == 3:
        nb, K, C = b.shape
        N = nb * C
        assert len(pieces) == 1
    else:
        K, N = b.shape
    g_bytes = sum(p.shape[1] * p.dtype.itemsize for p in pieces)
    tm, tk = _mm_tiles(M, K, g_bytes, N * b.dtype.itemsize, jnp.dtype(out_dtype).itemsize, _ROW_CANDS, _COL_CANDS)

    def body(*refs):
        b_ref, o_ref = refs[-2:]
        acc = None
        if b.ndim == 3:
            parts = [_dot(refs[0][:, k * C:(k + 1) * C], b_ref[k], _NT) for k in range(nb)]
        else:
            parts = [_dot(g_ref[...], b_ref[:, off:off + g_ref.shape[1]], _NT) for g_ref, off in zip(refs[:-2], offsets)]
        for part in parts:
            acc = part if acc is None else acc + part
        o_ref[...] = acc.astype(o_ref.dtype)

    b_spec = (pl.BlockSpec((nb, tk, C), lambda j, i: (0, j, 0)) if b.ndim == 3
              else pl.BlockSpec((tk, N), lambda j, i: (j, 0)))
    res, xres = hosted_call(
        body, ex, pieces + [b], name=name, out_shape=[jax.ShapeDtypeStruct((M, K), out_dtype)], grid=(K // tk, M // tm),
        in_specs=[pl.BlockSpec((tm, p.shape[1]), lambda j, i: (i, 0)) for p in pieces] + [b_spec],
        out_specs=[pl.BlockSpec((tm, tk), lambda j, i: (i, j))])
    return _one(res, xres, ex)


def matmul_tn(name, a, g, ex=None, blocks=1):
    M, K = a.shape
    N = g.shape[1]
    C = N // blocks
    tk, tn = _mm_tiles(K, N, M * a.dtype.itemsize, M * g.dtype.itemsize, 4, (512, 256, 128),
                       (512, 256, 128) if blocks == 1 else _block_cands(C))

    def body(a_ref, g_ref, o_ref):
        o_ref[...] = _dot(a_ref[...], g_ref[...], _TN)

    if blocks == 1:
        out_shape, out_spec = jax.ShapeDtypeStruct((K, N), F32), pl.BlockSpec((tk, tn), lambda i, j: (i, j))
    else:
        per = C // tn
        out_shape = jax.ShapeDtypeStruct((blocks, K, C), F32)
        out_spec = pl.BlockSpec((None, tk, tn), lambda i, j: (j // per, i, j % per))
    res, xres = hosted_call(
        body, ex, [a, g], name=name, out_shape=[out_shape], grid=(K // tk, N // tn),
        in_specs=[pl.BlockSpec((M, tk), lambda i, j: (0, i)), pl.BlockSpec((M, tn), lambda i, j: (0, j))],
        out_specs=[out_spec])
    return _one(res, xres, ex)


class Arg:
    def __init__(self, arr, block, imap, kind):
        self.arr, self.block, self.imap, self.kind = arr, block, imap, kind


class Rows:
    def __init__(self, nt, nct, tm, ncol=1):
        self.nt, self.nct, self.tm, self.ncol = nt, nct, tm, ncol

    def seg(self, i):
        return jnp.where(i >= self.nct, 1, 0)

    def spec(self, block, imap):
        return pl.BlockSpec(block, lambda j, i: imap(j, i, self.seg(i)))

    def row(self, arr, width, cb0=0, follow=False, roff=0, stride=1):
        f = stride if follow else 0
        return Arg(arr, (self.tm, width), lambda j, i, s: (i + roff, cb0 + f * j), "row")

    def vec(self, arr, follow=False, kind="acc"):
        w = arr.shape[1] // (self.ncol if follow else 1)
        f = 1 if follow else 0
        return Arg(arr, (1, w), lambda j, i, s: (0, f * j), kind)

    def segvec(self, arr, kind="seg"):
        return Arg(arr, (None, 1, arr.shape[2]), lambda j, i, s: (s, 0, 0), kind)


def _load(ref):
    return ref[...].astype(F32) if ref.dtype != F32 else ref[...]


def stage_fwd(name, f, rows, args, outs):
    n_in = len(args)

    def body(*refs):
        vals = [_load(r) for r in refs[:n_in]]
        res = f(*vals)
        for r, v in zip(refs[n_in:], res):
            r[...] = v.astype(r.dtype)

    T = rows.nt * rows.tm
    out_shape = [jax.ShapeDtypeStruct((T, w * (rows.ncol if fo else 1)), dt) for w, dt, fo in outs]
    out_specs = [pl.BlockSpec((rows.tm, w), (lambda j, i, fo=fo: (i, j if fo else 0))) for w, dt, fo in outs]
    res = _pcall(
        body, name=name, out_shape=out_shape, grid=(rows.ncol, rows.nt),
        in_specs=[rows.spec(a.block, a.imap) for a in args], out_specs=out_specs,
        compiler_params=_params("parallel", "parallel"),
    )(*[a.arr for a in args])
    return res


def stage_bwd(name, f, rows, args, cots, row_dtypes, ex=None, primal=()):
    n_in, n_ct = len(args), len(cots)
    diff = [k for k, a in enumerate(args) if a.kind != "const"]
    row_dt = {}
    for k in diff:
        if args[k].kind == "row":
            row_dt[k] = row_dtypes[len(row_dt)]

    def body(*refs):
        i = pl.program_id(1)
        vals = [_load(r) for r in refs[:n_in]]
        cts = tuple(_load(r) for r in refs[n_in:n_in + n_ct])
        outs = refs[n_in + n_ct:]

        def g(*dv):
            full = list(vals)
            for k, v in zip(diff, dv):
                full[k] = v
            return tuple(f(*full))

        prim, vjp = jax.vjp(g, *[vals[k] for k in diff])
        grads = vjp(cts)
        for o, v in zip(outs[len(diff):], prim):
            o[...] = v.astype(o.dtype)
        for k, o, gr in zip(diff, outs, grads):
            kind = args[k].kind
            if kind == "row":
                o[...] = gr.astype(o.dtype)
            else:
                first = (i == 0) | (i == rows.nct) if kind == "seg" else (i == 0)

                @pl.when(first)
                def _():
                    o[...] = gr.astype(o.dtype)

                @pl.when(jnp.logical_not(first))
                def _():
                    o[...] += gr.astype(o.dtype)

    T = rows.nt * rows.tm
    out_shape, out_specs = [], []
    for k in diff:
        a = args[k]
        if a.kind == "row":
            out_shape.append(jax.ShapeDtypeStruct((T, a.block[1] * (rows.ncol if _follows(a) else 1)), row_dt[k]))
            fo = _follows(a)
            out_specs.append(pl.BlockSpec(a.block, (lambda j, i, fo=fo: (i, j if fo else 0))))
        else:
            out_shape.append(jax.ShapeDtypeStruct(a.arr.shape, F32))
            out_specs.append(rows.spec(a.block, a.imap))
    for w, dt in primal:
        out_shape.append(jax.ShapeDtypeStruct((T, w), dt))
        out_specs.append(pl.BlockSpec((rows.tm, w), lambda j, i: (i, 0)))
    res, xres = hosted_call(
        body, ex, [a.arr for a in list(args) + list(cots)], name=name, out_shape=out_shape, grid=(rows.ncol, rows.nt),
        in_specs=[rows.spec(a.block, a.imap) for a in list(args) + list(cots)], out_specs=out_specs)
    return res if ex is None else (res, xres)


def _follows(a):
    return a.imap(1, 0, 0)[-1] != a.imap(0, 0, 0)[-1]


def _rms(x):
    return x * lax.rsqrt(jnp.mean(x * x, axis=-1, keepdims=True) + EPS)


def f_norm_mod(x, g, sh, sc):
    return ((_rms(x) * g) * (1.0 + sc) + sh,)


def f_resid_norm_mod(x, mo, ga, g, sh, sc):
    x1 = x + ga * mo
    return x1, (_rms(x1) * g) * (1.0 + sc) + sh


def f_resid(x, dn, ga):
    return (x + ga * dn,)


def f_silu(x):
    return (x * jax.nn.sigmoid(x),)


def f_bias(x, b):
    return (x + b,)


def f_ssd_gate(y0, y1, xs, z, dskip, nw):
    y = y0 + y1 + dskip * xs
    return (_rms(y * (z * jax.nn.sigmoid(z))) * nw,)


def f_pool(u, pmat, pmat_t, inv_cnt, pw, scale):
    pm = _lin01(pmat, pmat_t, u) * inv_cnt - u
    return (_mm(pm, pw) * scale,)


def f_merge(o_ssd, o_pool, gl_ssd, gl_pool):
    return (jax.nn.sigmoid(gl_ssd) * o_ssd + jax.nn.sigmoid(gl_pool) * o_pool,)


def _column_splitter(n):
    @jax.custom_vjp
    def split(x):
        w = x.shape[1] // n
        return tuple(x[:, k * w:(k + 1) * w] for k in range(n))

    def fwd(x):
        return split(x), None

    def bwd(_, g):
        return (jnp.concatenate(g, axis=1),)

    split.defvjp(fwd, bwd)
    return split


_halve_cols = _column_splitter(2)
_quarter_cols = _column_splitter(len(POOL_WINDOWS))


def f_swiglu(gu):
    a, b = _halve_cols(gu)
    return ((a * jax.nn.sigmoid(a)) * b,)


def f_pool_all(u, pmat, pmat_t, inv_cnt, scale, *pws):
    outs = [f_pool(part, pmat[k], pmat_t[k], inv_cnt[k], pws[k], 1.0)[0] for k, part in enumerate(_quarter_cols(u))]
    return (jnp.concatenate(outs, axis=1) * scale,)


def f_loss_resid(x1, dn, ga, tgt, g):
    err = _rms(x1 + ga * dn) * g - tgt
    return (0.5 * jnp.mean(err * err, axis=-1, keepdims=True),)


CONV_TILE = 128


def _shift_rows(v, j, n_ctx):
    if j == 0:
        return v
    T = v.shape[0]
    r = lax.broadcasted_iota(jnp.int32, v.shape, 0)
    lo = jnp.where(r >= n_ctx, n_ctx, 0)
    hi = jnp.where(r >= n_ctx, T, n_ctx)
    ok = (r + j >= lo) & (r + j < hi)
    return jnp.where(ok, pltpu.roll(v, (-j) % T, 0), 0.0)


def conv_fwd(name, proj, conv_w, conv_b, n_ctx, width, ex=None):
    T = proj.shape[0]
    half = SSD_CONV // 2

    def body(u_ref, w_ref, b_ref, o_ref):
        u = u_ref[...].astype(F32)
        pre = jnp.broadcast_to(b_ref[...], u.shape)
        for k in range(SSD_CONV):
            pre = pre + w_ref[k:k + 1, :] * _shift_rows(u, k - half, n_ctx)
        o_ref[...] = pre * jax.nn.sigmoid(pre)

    col = lambda t: (0, t)
    res, xres = hosted_call(
        body, ex, [proj, conv_w, conv_b], name=name, out_shape=[jax.ShapeDtypeStruct((T, width), F32)],
        grid=(width // CONV_TILE,),
        in_specs=[pl.BlockSpec((T, CONV_TILE), col), pl.BlockSpec((SSD_CONV, CONV_TILE), col),
                  pl.BlockSpec((1, CONV_TILE), col)],
        out_specs=[pl.BlockSpec((T, CONV_TILE), col)])
    return res[0], xres


def conv_bwd(name, proj, conv_w, conv_b, d_act2, d_skip, n_ctx, width, ex=None):
    T = proj.shape[0]
    half = SSD_CONV // 2

    def body(u_ref, w_ref, b_ref, c0_ref, c1_ref, cs_ref, du_ref, dw_ref, db_ref):
        t = pl.program_id(0)
        u = u_ref[...].astype(F32)
        pre = jnp.broadcast_to(b_ref[...], u.shape)
        for k in range(SSD_CONV):
            pre = pre + w_ref[k:k + 1, :] * _shift_rows(u, k - half, n_ctx)
        sg = jax.nn.sigmoid(pre)
        ct = c0_ref[...].astype(F32) + c1_ref[...].astype(F32) + jnp.where(t % 4 < 2, cs_ref[...].astype(F32), 0.0)
        dpre = ct * (sg * (1.0 + pre * (1.0 - sg)))
        du = jnp.zeros_like(u)
        for k in range(SSD_CONV):
            du = du + w_ref[k:k + 1, :] * _shift_rows(dpre, half - k, n_ctx)
            dw_ref[k:k + 1, :] = jnp.sum(dpre * _shift_rows(u, k - half, n_ctx), axis=0, keepdims=True)
        du_ref[...] = du.astype(du_ref.dtype)
        db_ref[...] = jnp.sum(dpre, axis=0, keepdims=True)

    col = lambda t: (0, t)
    skip_col = lambda t: (0, (t // 4) * 2 + jnp.minimum(t % 4, 1))
    res, xres = hosted_call(
        body, ex, [proj, conv_w, conv_b, d_act2[0], d_act2[1], d_skip], name=name,
        out_shape=[jax.ShapeDtypeStruct((T, width), ACT_DTYPE), jax.ShapeDtypeStruct((SSD_CONV, width), F32),
                   jax.ShapeDtypeStruct((1, width), F32)],
        grid=(width // CONV_TILE,),
        in_specs=[pl.BlockSpec((T, CONV_TILE), col), pl.BlockSpec((SSD_CONV, CONV_TILE), col),
                  pl.BlockSpec((1, CONV_TILE), col), pl.BlockSpec((T, CONV_TILE), col),
                  pl.BlockSpec((T, CONV_TILE), col), pl.BlockSpec((T, CONV_TILE), skip_col)],
        out_specs=[pl.BlockSpec((T, CONV_TILE), col), pl.BlockSpec((SSD_CONV, CONV_TILE), col),
                   pl.BlockSpec((1, CONV_TILE), col)])
    return res[0], res[1], res[2], xres


@jax.custom_vjp
def _cumsum_mat(tri, tri_t, a):
    return jnp.dot(tri, a, precision=lax.Precision.HIGHEST, preferred_element_type=F32)


def _cumsum_fwd(tri, tri_t, a):
    return _cumsum_mat(tri, tri_t, a), (tri, tri_t)


def _cumsum_bwd(res, g):
    tri, tri_t = res
    return (jnp.zeros_like(tri), jnp.zeros_like(tri_t),
            jnp.dot(tri_t, g, precision=lax.Precision.HIGHEST, preferred_element_type=F32))


_cumsum_mat.defvjp(_cumsum_fwd, _cumsum_bwd)


def _ssd_dt(dtraw, dt_bias, a_log, tri, tri_t):
    dt_all = jax.nn.softplus(dtraw + dt_bias)
    a_all = dt_all * (-jnp.exp(a_log))
    return dt_all, a_all, _cumsum_mat(tri, tri_t, a_all)


def _ssd_chunk(xs, bm, cm, dt_all, a_all, s_all, s_in, mask, idx0):
    (xs,), (s_in,) = xs, s_in
    Q = xs.shape[0]
    hpg = xs.shape[1] // SSD_HEADDIM
    lane = lax.broadcasted_iota(jnp.int32, dt_all.shape, 1)
    head = lax.broadcasted_iota(jnp.int32, xs.shape, 1) // SSD_HEADDIM
    head1 = lax.broadcasted_iota(jnp.int32, (1, xs.shape[1]), 1) // SSD_HEADDIM

    def pick(v, r):
        return jnp.sum(jnp.where(lane == idx0 + r, v, 0.0), axis=1, keepdims=True)

    def expand(cols, hd):
        out = cols[hpg - 1]
        for r in range(hpg - 2, -1, -1):
            out = jnp.where(hd == r, cols[r], out)
        return out

    def spread(*cols):
        return expand([jnp.broadcast_to(c, xs.shape) for c in cols], head)

    dt_r = [pick(dt_all, r) for r in range(hpg)]
    s_r = [pick(s_all, r) for r in range(hpg)]
    stot_r = [jnp.sum(jnp.where(lane == idx0 + r, a_all, 0.0), keepdims=True).reshape(1, 1) for r in range(hpg)]

    xd = xs * spread(*dt_r)
    cb = _mm_nt(cm, bm)
    weights, stacked = [], []
    for r in range(hpg):
        sm = jnp.broadcast_to(s_r[r], (Q, Q))
        weights.append(cb * jnp.exp(jnp.where(mask, sm - sm.T, NEG)))
        stacked.append(jnp.where(head == r, xd, 0.0))
    y = spread(*[jnp.exp(c) for c in s_r]) * _mm(cm, s_in)
    y = y + _mm(jnp.concatenate(weights, axis=1), jnp.concatenate(stacked, axis=0))
    to_end = spread(*[jnp.exp(t - c) for t, c in zip(stot_r, s_r)])
    carry = expand([jnp.broadcast_to(jnp.exp(t), (1, xs.shape[1])) for t in stot_r], head1)
    s_out = carry * s_in + _mm_tn(bm, xd * to_end)
    return [y], [s_out]


def _scan_consts():
    q = SSD_CHUNK
    i = np.arange(q)[:, None]
    j = np.arange(q)[None, :]
    fwd = (j <= i).astype(np.float32)
    bwd = (j >= i).astype(np.float32)
    tri = np.stack([fwd, bwd])
    return jnp.asarray(tri), jnp.asarray(np.stack([fwd.T, bwd.T]))


def _chunk_of(d, k, ncc, nc):
    rev = jnp.where(k < ncc, ncc - 1 - k, nc - 1 + ncc - k)
    return jnp.where(d == 0, k, rev)


def ssd_fwd(name, xbc, dtraw, dt_bias, a_log, n_ctx, ex=None):
    T = xbc.shape[0]
    q, G = SSD_CHUNK, SSD_GROUPS
    nc, ncc = T // q, n_ctx // q
    gw = xbc.shape[1] // G
    xw = gw - 2 * SSD_STATE
    hpg = xw // SSD_HEADDIM
    nh = G * hpg
    tri, tri_t = _scan_consts()

    gs = SSD_GROUPS_PER_STEP

    def body(x0_ref, x1_ref, dt0_ref, dt1_ref, bias_ref, alog_ref, tri_ref, trit_ref, y0_ref, y1_ref, sin_ref, state):
        gb, k = pl.program_id(0), pl.program_id(1)

        @pl.when(k == 0)
        def _():
            state[...] = jnp.zeros_like(state)

        for d, (x_ref, dt_ref, y_ref) in enumerate(((x0_ref, dt0_ref, y0_ref), (x1_ref, dt1_ref, y1_ref))):
            tri_v = tri_ref[d]
            dt_all, a_all, s_all = _ssd_dt(dt_ref[...], bias_ref[...], alog_ref[...], tri_v, trit_ref[d])
            for j in range(gs):
                o = j * gw
                sin_ref[d, j] = state[d, j]
                (y,), (s_out,) = _ssd_chunk(
                    [x_ref[:, o:o + xw]], x_ref[:, o + xw:o + xw + SSD_STATE], x_ref[:, o + xw + SSD_STATE:o + gw],
                    dt_all, a_all, s_all, [state[d, j]], tri_v > 0.5, d * nh + (gb * gs + j) * hpg)
                y_ref[:, j * xw:(j + 1) * xw] = y.astype(y_ref.dtype)
                state[d, j] = s_out

    ch = lambda d, k: _chunk_of(d, k, ncc, nc)
    y_shape = jax.ShapeDtypeStruct((T, G * xw), ACT_DTYPE)
    res, xres = hosted_call(
        body, ex, [xbc, xbc, dtraw, dtraw, dt_bias, a_log, tri, tri_t], name=name,
        out_shape=[y_shape, y_shape, jax.ShapeDtypeStruct((2, nc, G, SSD_STATE, xw), F32)],
        grid=(G // gs, nc),
        in_specs=[pl.BlockSpec((q, gs * gw), lambda g, k: (ch(0, k), g)),
                  pl.BlockSpec((q, gs * gw), lambda g, k: (ch(1, k), g)),
                  pl.BlockSpec((q, 128), lambda g, k: (ch(0, k), 0)),
                  pl.BlockSpec((q, 128), lambda g, k: (ch(1, k), 0)),
                  pl.BlockSpec((1, 128), lambda g, k: (0, 0)),
                  pl.BlockSpec((1, 128), lambda g, k: (0, 0)),
                  pl.BlockSpec((2, q, q), lambda g, k: (0, 0, 0)),
                  pl.BlockSpec((2, q, q), lambda g, k: (0, 0, 0))],
        out_specs=[pl.BlockSpec((q, gs * xw), lambda g, k: (ch(0, k), g)),
                   pl.BlockSpec((q, gs * xw), lambda g, k: (ch(1, k), g)),
                   pl.BlockSpec((2, None, gs, SSD_STATE, xw), lambda g, k: (0, k, g, 0, 0))],
        scratch_shapes=[pltpu.VMEM((2, gs, SSD_STATE, xw), F32)])
    return res[0], res[1], res[2], xres


def ssd_bwd(name, xbc, dtraw, dt_bias, a_log, states, dy, n_ctx, ex=None):
    T = xbc.shape[0]
    q, G = SSD_CHUNK, SSD_GROUPS
    nc, ncc = T // q, n_ctx // q
    gw = xbc.shape[1] // G
    xw = gw - 2 * SSD_STATE
    hpg = xw // SSD_HEADDIM
    nh = G * hpg
    tri, tri_t = _scan_consts()

    gs = SSD_GROUPS_PER_STEP

    def body(x0_ref, x1_ref, dt0_ref, dt1_ref, bias_ref, alog_ref, tri_ref, trit_ref, sin_ref, dy0_ref, dy1_ref,
             dx0_ref, dx1_ref, ddt_ref, dbias_ref, dalog_ref, dstate):
        gb, k = pl.program_id(0), pl.program_id(1)

        @pl.when((gb == 0) & (k == 0))
        def _():
            ddt_ref[...] = jnp.zeros_like(ddt_ref)
            dbias_ref[...] = jnp.zeros_like(dbias_ref)
            dalog_ref[...] = jnp.zeros_like(dalog_ref)

        @pl.when(k == 0)
        def _():
            dstate[...] = jnp.zeros_like(dstate)

        tris = [(tri_ref[d], trit_ref[d]) for d in range(2)]
        per = 4

        def fn(bias, alog, dtraw0, dtraw1, *per_group):
            ys, s_outs = [], []
            for d, dtraw in enumerate((dtraw0, dtraw1)):
                tri_v, trit_v = tris[d]
                dt_all, a_all, s_all = _ssd_dt(dtraw, bias, alog, tri_v, trit_v)
                for j in range(gs):
                    xs, bm, cm, s_in = per_group[per * (d * gs + j):per * (d * gs + j + 1)]
                    y, s_out = _ssd_chunk([xs], bm, cm, dt_all, a_all, s_all, [s_in], tri_v > 0.5,
                                          d * nh + (gb * gs + j) * hpg)
                    ys += y
                    s_outs += s_out
            return ys, s_outs

        per_group, dys, dss = [], [], []
        for d, (x_ref, dy_ref) in enumerate(((x0_ref, dy0_ref), (x1_ref, dy1_ref))):
            for j in range(gs):
                o = j * gw
                per_group += [x_ref[:, o:o + xw], x_ref[:, o + xw:o + xw + SSD_STATE], x_ref[:, o + xw + SSD_STATE:o + gw],
                              sin_ref[d, j]]
                dys.append(dy_ref[:, j * xw:(j + 1) * xw].astype(F32))
                dss.append(dstate[d, j])
        _, vjp = jax.vjp(fn, bias_ref[...], alog_ref[...], dt0_ref[...], dt1_ref[...], *per_group)
        cts = vjp((dys, dss))
        dbias, dalog, ddt0, ddt1 = cts[:4]
        for d, dx_ref in enumerate((dx0_ref, dx1_ref)):
            for j in range(gs):
                o = j * gw
                dxs, dbm, dcm, ds_in = cts[4 + per * (d * gs + j):4 + per * (d * gs + j + 1)]
                dx_ref[:, o:o + xw] = dxs.astype(dx_ref.dtype)
                dx_ref[:, o + xw:o + xw + SSD_STATE] = dbm.astype(dx_ref.dtype)
                dx_ref[:, o + xw + SSD_STATE:o + gw] = dcm.astype(dx_ref.dtype)
                dstate[d, j] = ds_in
        for d, ddt in enumerate((ddt0, ddt1)):
            row0 = pl.multiple_of(_chunk_of(d, nc - 1 - k, ncc, nc) * q, q)
            ddt_ref[pl.ds(row0, q), :] += ddt
        dbias_ref[...] += dbias
        dalog_ref[...] += dalog

    ch = lambda d, k: _chunk_of(d, nc - 1 - k, ncc, nc)
    dx_shape = jax.ShapeDtypeStruct((T, G * gw), ACT_DTYPE)
    res, xres = hosted_call(
        body, ex, [xbc, xbc, dtraw, dtraw, dt_bias, a_log, tri, tri_t, states, dy, dy], name=name,
        out_shape=[dx_shape, dx_shape, jax.ShapeDtypeStruct((T, 128), F32),
                   jax.ShapeDtypeStruct((1, 128), F32), jax.ShapeDtypeStruct((1, 128), F32)],
        grid=(G // gs, nc),
        in_specs=[pl.BlockSpec((q, gs * gw), lambda g, k: (ch(0, k), g)),
                  pl.BlockSpec((q, gs * gw), lambda g, k: (ch(1, k), g)),
                  pl.BlockSpec((q, 128), lambda g, k: (ch(0, k), 0)),
                  pl.BlockSpec((q, 128), lambda g, k: (ch(1, k), 0)),
                  pl.BlockSpec((1, 128), lambda g, k: (0, 0)),
                  pl.BlockSpec((1, 128), lambda g, k: (0, 0)),
                  pl.BlockSpec((2, q, q), lambda g, k: (0, 0, 0)),
                  pl.BlockSpec((2, q, q), lambda g, k: (0, 0, 0)),
                  pl.BlockSpec((2, None, gs, SSD_STATE, xw), lambda g, k: (0, nc - 1 - k, g, 0, 0)),
                  pl.BlockSpec((q, gs * xw), lambda g, k: (ch(0, k), g)),
                  pl.BlockSpec((q, gs * xw), lambda g, k: (ch(1, k), g))],
        out_specs=[pl.BlockSpec((q, gs * gw), lambda g, k: (ch(0, k), g)),
                   pl.BlockSpec((q, gs * gw), lambda g, k: (ch(1, k), g)),
                   pl.BlockSpec((T, 128), lambda g, k: (0, 0)),
                   pl.BlockSpec((1, 128), lambda g, k: (0, 0)),
                   pl.BlockSpec((1, 128), lambda g, k: (0, 0))],
        scratch_shapes=[pltpu.VMEM((2, gs, SSD_STATE, xw), F32)])
    return res[0], res[1], res[2], res[3], res[4], xres


def _perm_xbc(a):
    G = SSD_GROUPS
    n = a.shape[-1]
    gn = G * SSD_STATE
    di = n - 2 * gn
    lead = a.shape[:-1]
    xs = a[..., :di].reshape(lead + (G, di // G))
    bm = a[..., di:di + gn].reshape(lead + (G, SSD_STATE))
    cm = a[..., di + gn:].reshape(lead + (G, SSD_STATE))
    return jnp.concatenate([xs, bm, cm], axis=-1).reshape(lead + (n,))


def _unperm_xbc(a):
    G = SSD_GROUPS
    n = a.shape[-1]
    gn = G * SSD_STATE
    di = n - 2 * gn
    lead = a.shape[:-1]
    r = a.reshape(lead + (G, n // G))
    xw = di // G
    return jnp.concatenate([r[..., :xw].reshape(lead + (di,)), r[..., xw:xw + SSD_STATE].reshape(lead + (gn,)),
                            r[..., xw + SSD_STATE:].reshape(lead + (gn,))], axis=-1)


def _pool_consts(tm, n_ctx):
    assert n_ctx == tm and tm % GRID_W == 0
    mats, cnts = [], []
    for seq in (n_ctx, GRID_W):
        t = np.arange(tm)
        tt = t % seq
        base = t - tt
        ms, cs = [], []
        for k in POOL_WINDOWS:
            lo = np.clip(tt - k // 2, 0, seq) + base
            hi = np.clip(tt + k // 2, 0, seq) + base
            m = ((t[None, :] >= lo[:, None]) & (t[None, :] < hi[:, None])).astype(np.float32)
            ms.append(m)
            cs.append((1.0 / (hi - lo).astype(np.float32))[:, None])
        mats.append(np.stack(ms))
        cnts.append(np.stack(cs))
    m = np.stack(mats)
    return jnp.asarray(m), jnp.asarray(np.swapaxes(m, -1, -2)), jnp.asarray(np.stack(cnts).astype(np.float32))


def _prep_layer_weights(w_ada, b_ada, g_mix, w_in, conv_w, conv_b, dt_bias, a_log, d_skip, ssd_norm_w, w_ssd_out,
                        pool_w, pool_scale, w_pool_out, w_out, g_ffn, w_gate_up, w_down):
    D = w_in.shape[0]
    di = ssd_norm_w.shape[0]
    xbc = conv_w.shape[1]
    nh2 = dt_bias.size
    pw = pool_scale.shape[0]
    o = 0
    wz = w_in[:, o:o + di]; o += di
    wx = w_in[:, o:o + xbc]; o += xbc
    wdt = w_in[:, o:o + nh2]; o += nh2
    wp = w_in[:, o:o + pw]; o += pw
    wg = w_in[:, o:]
    w1 = jnp.concatenate([_perm_xbc(wx), wz, wg, wp, wdt, jnp.zeros((D, DT_PAD - nh2), w_in.dtype)], axis=1)
    pad128 = lambda v: jnp.concatenate([v.reshape(1, -1), jnp.zeros((1, 128 - v.size), F32)], axis=1)
    return dict(
        w_ada=w_ada, b_ada=b_ada.reshape(1, -1), g_mix=g_mix.reshape(1, -1), w1=w1,
        conv_w=_perm_xbc(conv_w), conv_b=_perm_xbc(conv_b.reshape(1, -1)),
        dt_bias=pad128(dt_bias), a_log=pad128(a_log),
        dskip=jnp.repeat(d_skip[0] + d_skip[1], SSD_HEADDIM).reshape(1, -1),
        ssd_norm_w=ssd_norm_w.reshape(1, -1), w_ssd_out=w_ssd_out, pool_w=pool_w,
        pool_scale=pool_scale.reshape(1, -1), w_pool_out=w_pool_out, w_out=w_out, g_ffn=g_ffn.reshape(1, -1),
        w_gate_up=w_gate_up, w_down=w_down)


def _unprep_layer_grads(g, dims):
    di, xbc, nh2, pw = dims
    dxbc, dz, dgs, dgp, dp, ddt = g["w1"]
    r = dxbc.reshape(SSD_GROUPS, xbc // SSD_GROUPS, dxbc.shape[1])
    xw = di // SSD_GROUPS
    parts = [r[:, :xw], r[:, xw:xw + SSD_STATE], r[:, xw + SSD_STATE:]]
    w_in_t = jnp.concatenate([dz] + [p.reshape(-1, dxbc.shape[1]) for p in parts] + [ddt[:nh2], dp, dgs, dgp], axis=0)
    nh = nh2 // 2
    dsk = g["dskip"].reshape(nh, SSD_HEADDIM).sum(axis=1)
    return dict(
        w_ada=g["w_ada"], b_ada=g["b_ada"].reshape(-1), g_mix=g["g_mix"].reshape(-1),
        w_in=w_in_t,
        conv_w=_unperm_xbc(g["conv_w"]), conv_b=_unperm_xbc(g["conv_b"]).reshape(-1),
        dt_bias=g["dt_bias"][0, :nh2].reshape(2, nh), a_log=g["a_log"][0, :nh2].reshape(2, nh),
        d_skip=jnp.stack([dsk, dsk]), ssd_norm_w=g["ssd_norm_w"].reshape(-1), w_ssd_out=g["w_ssd_out"],
        pool_w=g["pool_w"], pool_scale=g["pool_scale"].reshape(-1), w_pool_out=g["w_pool_out"], w_out=g["w_out"],
        g_ffn=g["g_ffn"].reshape(-1), w_gate_up=g["w_gate_up"], w_down=g["w_down"])


COND_ROWS = 16


def _split_mods(m):
    d = m.shape[1] // 6
    return [m[:2, k * d:(k + 1) * d].reshape(2, 1, d) for k in range(6)]


def _pool_args(rows, proj, col_block, width, pc, w):
    seg_const = lambda a: Arg(a, (None,) + a.shape[1:], lambda j, i, s: (s, 0, 0, 0), "const")
    pws = [Arg(w["pool_w"][k], w["pool_w"].shape[1:], lambda j, i, s: (0, 0), "acc") for k in range(w["pool_w"].shape[0])]
    return [rows.row(proj, width, col_block)] + [seg_const(a) for a in pc] + [rows.vec(w["pool_scale"])] + pws


TALL_ROW_TILE = 1088


def _tall_rows(T, ncol):
    tm = max(t for t in range(16, min(T, TALL_ROW_TILE) + 1, 16) if T % t == 0)
    return Rows(T // tm, 0, tm, ncol)


def _hosted(hosts, box, key):
    fn = (hosts or {}).get(key)
    return fn(box) if fn else None


def _layer_fwd(l, pre, cond_s, w, rows, n_ctx, pc, hosts=None, box=None):
    T, D = pre[0].shape if isinstance(pre, tuple) else pre.shape
    nt, nct, tm = rows.nt, rows.nct, rows.tm
    n = lambda s: f"l{l}_{s}"
    crow = Rows(1, 0, COND_ROWS)
    mraw = matmul_nn(n("ada_mm"), cond_s, w["w_ada"])
    (m,) = stage_fwd(n("ada_bias"), f_bias, crow, [crow.row(mraw, mraw.shape[1]), crow.vec(w["b_ada"])],
                     [(mraw.shape[1], F32, False)])
    sh1, sc1, ga1, sh2, sc2, ga2 = _split_mods(m)

    if isinstance(pre, tuple):
        x, h1 = stage_fwd(n("norm1"), f_resid_norm_mod, rows, _resid_norm_args(rows, pre, w["g_mix"], sh1, sc1, D),
                          [(D, F32, False), (D, ACT_DTYPE, False)])
    else:
        x = pre
        (h1,) = stage_fwd(n("norm1"), f_norm_mod, rows,
                          [rows.row(x, D), rows.vec(w["g_mix"]), rows.segvec(sh1), rows.segvec(sc1)],
                          [(D, ACT_DTYPE, False)])
    xbc_w = w["conv_w"].shape[1]
    di = w["ssd_norm_w"].shape[1]
    pw = w["pool_scale"].shape[1]
    c_z, c_g, c_p, c_dt = xbc_w, xbc_w + di, xbc_w + di + 2 * pw, xbc_w + di + 3 * pw
    ex = _hosted(hosts, box, "in_mm")
    proj = matmul_nn(n("in_mm"), h1, w["w1"], out_dtype=ACT_DTYPE, ex=ex, ncols=c_dt)
    if ex is not None:
        proj, box["in_mm"] = proj
    dtraw = matmul_nn(n("in_dt_mm"), h1, w["w1"], col0=c_dt, ncols=128)
    ex = _hosted(hosts, box, "conv")
    xbc, xres = conv_fwd(n("conv"), proj, w["conv_w"], w["conv_b"], n_ctx, xbc_w, ex)
    if ex is not None:
        box["conv"] = xres
    ex = _hosted(hosts, box, "ssd")
    y0, y1, states, xres = ssd_fwd(n("ssd"), xbc, dtraw, w["dt_bias"], w["a_log"], n_ctx, ex)
    y2 = (y0, y1)
    if ex is not None:
        box["ssd"] = xres

    G = SSD_GROUPS
    gw = di // G
    r8 = _tall_rows(T, G)
    gate_args = [r8.row(y2[0], gw, 0, True), r8.row(y2[1], gw, 0, True), r8.row(xbc, gw, 0, True, stride=2),
                 r8.row(proj, gw, c_z // gw, True), r8.vec(w["dskip"], True), r8.vec(w["ssd_norm_w"], True)]
    (ynw,) = stage_fwd(n("ssd_gate"), f_ssd_gate, r8, gate_args, [(gw, ACT_DTYPE, True)])
    o_ssd = matmul_nn(n("ssd_out_mm"), ynw, w["w_ssd_out"])

    nw = len(POOL_WINDOWS)
    pg = pw // nw
    (ps,) = stage_fwd(n("pool"), f_pool_all, rows, _pool_args(rows, proj, c_p // pw, pw, pc, w), [(pw, ACT_DTYPE, False)])
    o_pool = matmul_nn(n("pool_out_mm"), ps, w["w_pool_out"])

    merge_args = [rows.row(o_ssd, D), rows.row(o_pool, D), rows.row(proj, pw, c_g // pw), rows.row(proj, pw, c_g // pw + 1)]
    (mg,) = stage_fwd(n("merge"), f_merge, rows, merge_args, [(D, ACT_DTYPE, False)])
    mo = matmul_nn(n("out_mm"), mg, w["w_out"])

    rn_args = [rows.row(x, D), rows.row(mo, D), rows.segvec(ga1), rows.vec(w["g_ffn"]), rows.segvec(sh2), rows.segvec(sc2)]
    x1, h2 = stage_fwd(n("norm2"), f_resid_norm_mod, rows, rn_args, [(D, F32, False), (D, ACT_DTYPE, False)])
    ex = _hosted(hosts, box, "gate_up_mm")
    gu = matmul_nn(n("gate_up_mm"), h2, w["w_gate_up"], ex=ex)
    if ex is not None:
        gu, box["gate_up_mm"] = gu
    fh = gu.shape[1] // 2
    (act,) = stage_fwd(n("swiglu"), f_swiglu, rows, [rows.row(gu, 2 * fh)], [(fh, ACT_DTYPE, False)])
    dn = matmul_nn(n("down_mm"), act, w["w_down"])
    saved = dict(x=x, pre=pre, mraw=mraw, mods=(sh1, sc1, ga1, sh2, sc2, ga2), h1=h1, proj=proj, dtraw=dtraw, xbc=xbc, y2=y2,
                 states=states,
                 ynw=ynw, o_ssd=o_ssd, ps=ps, o_pool=o_pool, mg=mg, mo=mo, x1=x1, h2=h2, gu=gu, act=act, dn=dn,
                 cols=(c_z, c_g, c_p, c_dt))
    return (x1, dn, ga2), saved


def _resid_norm_args(rows, pre, g, sh, sc, D):
    x1, dn, ga2 = pre
    return [rows.row(x1, D), rows.row(dn, D), rows.segvec(ga2), rows.vec(g), rows.segvec(sh), rows.segvec(sc)]


def f_norm_mod_keep(x, g, sh, sc):
    return f_norm_mod(x, g, sh, sc)[0], x


def _layer_bwd(l, cot, cond_s, w, s, rows, n_ctx, pc, hosts=None, box=None):
    dx1, ddn, dga2 = cot
    T, D = dx1.shape
    nt, nct, tm = rows.nt, rows.nct, rows.tm
    n = lambda t: f"l{l}_{t}_bwd"
    sh1, sc1, ga1, sh2, sc2, ga2 = s["mods"]
    c_z, c_g, c_p, c_dt = s["cols"]
    x, proj, xbc, y2, gu = s["x"], s["proj"], s["xbc"], s["y2"], s["gu"]
    g = {}
    if box is not None:
        box["g"] = g

    ex = _hosted(hosts, box, "down_dx")
    dact = matmul_nt(n("down_dx"), ddn, w["w_down"], ex=ex)
    if ex is not None:
        dact, box["down_dx"] = dact
    g["w_down"] = matmul_tn(n("down_dw"), s["act"], ddn)
    fh = gu.shape[1] // 2
    (dgu,) = stage_bwd(n("swiglu"), f_swiglu, rows, [rows.row(gu, 2 * fh)], [rows.row(dact, fh)], [ACT_DTYPE])
    dh2 = matmul_nt(n("gate_up_dx"), dgu, w["w_gate_up"])
    g["w_gate_up"] = matmul_tn(n("gate_up_dw"), s["h2"], dgu, blocks=w["w_gate_up"].shape[0])

    rn_args = [rows.row(x, D), rows.row(s["mo"], D), rows.segvec(ga1), rows.vec(w["g_ffn"]), rows.segvec(sh2), rows.segvec(sc2)]
    dxr, dmo, dga1, g["g_ffn"], dsh2, dsc2 = stage_bwd(
        n("norm2"), f_resid_norm_mod, rows, rn_args, [rows.row(dx1, D), rows.row(dh2, D)], [F32, ACT_DTYPE])
    dmg = matmul_nt(n("out_dx"), dmo, w["w_out"])
    g["w_out"] = matmul_tn(n("out_dw"), s["mg"], dmo)

    pw = w["pool_scale"].shape[1]
    merge_args = [rows.row(s["o_ssd"], D), rows.row(s["o_pool"], D), rows.row(proj, pw, c_g // pw), rows.row(proj, pw, c_g // pw + 1)]
    do_ssd, do_pool, dgl_s, dgl_p = stage_bwd(n("merge"), f_merge, rows, merge_args, [rows.row(dmg, D)], [ACT_DTYPE] * 4)
    dps = matmul_nt(n("pool_out_dx"), do_pool, w["w_pool_out"])
    g["w_pool_out"] = matmul_tn(n("pool_out_dw"), s["ps"], do_pool)

    nw = len(POOL_WINDOWS)
    pg = pw // nw
    du_pool, g["pool_scale"], *dpw = stage_bwd(n("pool"), f_pool_all, rows, _pool_args(rows, proj, c_p // pw, pw, pc, w),
                                               [rows.row(dps, pw)], [ACT_DTYPE])
    g["pool_w"] = jnp.stack(dpw)

    dynw = matmul_nt(n("ssd_out_dx"), do_ssd, w["w_ssd_out"])
    g["w_ssd_out"] = matmul_tn(n("ssd_out_dw"), s["ynw"], do_ssd)
    G = SSD_GROUPS
    di = w["ssd_norm_w"].shape[1]
    gw = di // G
    r8 = _tall_rows(T, G)
    gate_args = [r8.row(y2[0], gw, 0, True), r8.row(y2[1], gw, 0, True), r8.row(xbc, gw, 0, True, stride=2),
                 r8.row(proj, gw, c_z // gw, True), r8.vec(w["dskip"], True), r8.vec(w["ssd_norm_w"], True)]
    gate_args[1].kind = "const"
    ex = _hosted(hosts, box, "ssd_gate")
    res = stage_bwd(n("ssd_gate"), f_ssd_gate, r8, gate_args, [r8.row(dynw, gw, 0, True)], [ACT_DTYPE] * 3, ex)
    if ex is not None:
        res, box["ssd_gate"] = res
    dy, dxs_skip, dz, g["dskip"], g["ssd_norm_w"] = res

    ex = _hosted(hosts, box, "ssd")
    dxbc0, dxbc1, ddt, g["dt_bias"], g["a_log"], xres = ssd_bwd(n("ssd"), xbc, s["dtraw"], w["dt_bias"], w["a_log"],
                                                                s["states"], dy, n_ctx, ex)
    dxbc2 = (dxbc0, dxbc1)
    if ex is not None:
        box["ssd"] = xres
    xbc_w = xbc.shape[1]
    ex = _hosted(hosts, box, "conv")
    dxbc_raw, g["conv_w"], g["conv_b"], xres = conv_bwd(n("conv"), proj, w["conv_w"], w["conv_b"], dxbc2, dxs_skip,
                                                         n_ctx, xbc_w, ex)
    if ex is not None:
        box["conv"] = xres
    pieces = [dxbc_raw, dz, dgl_s, dgl_p, du_pool, ddt]
    offsets = [0, c_z, c_g, c_g + pw, c_p, c_dt]
    ex = _hosted(hosts, box, "in_dx")
    dh1 = matmul_nt(n("in_dx"), pieces, w["w1"], ex=ex, offsets=offsets)
    if ex is not None:
        dh1, box["in_dx"] = dh1
    ex = _hosted(hosts, box, "in_dw")
    first = matmul_tn(n("in_dw0"), pieces[0], s["h1"], ex=ex)
    if ex is not None:
        first, box["in_dw"] = first
    g["w1"] = [first] + [matmul_tn(n(f"in_dw{k}"), p, s["h1"]) for k, p in enumerate(pieces) if k]

    if isinstance(s["pre"], tuple):
        dx1p, ddnp, dga2p, g["g_mix"], dsh1, dsc1 = stage_bwd(
            n("norm1"), f_resid_norm_mod, rows, _resid_norm_args(rows, s["pre"], w["g_mix"], sh1, sc1, D),
            [rows.row(dxr, D), rows.row(dh1, D)], [F32, ACT_DTYPE])
        dx = (dx1p, ddnp, dga2p)
    else:
        n1_args = [rows.row(x, D), rows.vec(w["g_mix"]), rows.segvec(sh1), rows.segvec(sc1)]
        dx, g["g_mix"], dsh1, dsc1 = stage_bwd(n("norm1"), f_norm_mod_keep, rows, n1_args,
                                               [rows.row(dh1, D), rows.row(dxr, D)], [F32])

    dm = jnp.concatenate([v.reshape(2, D) for v in (dsh1, dsc1, dga1, dsh2, dsc2, dga2)], axis=1)
    dm = jnp.concatenate([dm, jnp.zeros((COND_ROWS - 2, dm.shape[1]), F32)], axis=0)
    crow = Rows(1, 0, COND_ROWS)
    dmraw, g["b_ada"] = stage_bwd(n("ada_bias"), f_bias, crow, [crow.row(s["mraw"], dm.shape[1]), crow.vec(w["b_ada"])],
                                  [crow.row(dm, dm.shape[1])], [ACT_DTYPE])
    dcs = matmul_nt(n("ada_dx"), dmraw, w["w_ada"])
    g["w_ada"] = matmul_tn(n("ada_dw"), cond_s, dmraw, blocks=w["w_ada"].shape[0])
    return dx, dcs, g


def local_step(x, ctx, c, c_ctx, target, layer_w_fn, n_layers, g_final, fwd_hosts=None, bwd_hosts=None):
    L, D = x.shape
    n_ctx = ctx.shape[0]
    tm = ROW_TILE
    T = L + n_ctx
    rows = Rows(T // tm, n_ctx // tm, tm)
    pc = _pool_consts(tm, n_ctx)
    xa = jnp.concatenate([ctx, x], axis=0)
    cond = jnp.concatenate([c_ctx.reshape(1, D), c.reshape(1, D), jnp.zeros((COND_ROWS - 2, D), F32)], axis=0)
    crow = Rows(1, 0, COND_ROWS)
    (cond_s,) = stage_fwd("cond_silu", f_silu, crow, [crow.row(cond, D)], [(D, ACT_DTYPE, False)])

    saved, layer_w = [], []
    for l in range(n_layers):
        layer_w.append(layer_w_fn(l))
        box = {}
        xa, s = _layer_fwd(l, xa, cond_s, layer_w[l], rows, n_ctx, pc, fwd_hosts(l, box) if fwd_hosts else None, box)
        saved.append(s)

    x1, dn, ga2 = xa
    rl = Rows(L // tm, 0, tm)
    gf = g_final.reshape(1, D)
    tgt = rl.row(target, D)
    tgt.kind = "const"
    off = n_ctx // tm
    loss_args = [rl.row(x1, D, roff=off), rl.row(dn, D, roff=off), rl.vec(ga2[1]), tgt, rl.vec(gf)]
    ones = jnp.ones((L, 1), F32)
    dx1_lat, ddn_lat, dga2_lat, dgf, loss_rows = stage_bwd("loss", f_loss_resid, rl, loss_args, [rl.row(ones, 1)],
                                                           [F32, ACT_DTYPE], primal=[(1, F32)])
    loss = jnp.sum(loss_rows)
    cot = (jnp.concatenate([jnp.zeros((n_ctx, D), F32), dx1_lat], axis=0),
           jnp.concatenate([jnp.zeros((n_ctx, D), ACT_DTYPE), ddn_lat], axis=0),
           jnp.stack([jnp.zeros((1, D), F32), dga2_lat]))

    grads = [None] * n_layers
    dcs = jnp.zeros((COND_ROWS, D), F32)
    for l in reversed(range(n_layers)):
        box = {}
        hosts = bwd_hosts(l, grads, box) if bwd_hosts else None
        cot, dcs_l, grads[l] = _layer_bwd(l, cot, cond_s, layer_w[l], saved[l], rows, n_ctx, pc, hosts, box)
        dcs = dcs + dcs_l
    dx = cot
    (dcond,) = stage_bwd("cond_silu_bwd", f_silu, crow, [crow.row(cond, D)], [crow.row(dcs, D)], [F32])
    return loss, dx[n_ctx:], grads, dcond[0], dgf


def gather_chips(halves, conv=None):
    n = len(halves)
    ops = list(halves) + ([conv] if conv is not None else [])

    def copies(ins, outs, pos):
        c, me = pos[2], _chip_index(pos)
        pairs = [(s.at[c], o.at[me, c]) for s, o in zip(ins[:n], outs[:n])]
        pairs += [(s, o.at[me]) for s, o in zip(ins[n:], outs[n:])]
        return pairs, [(s, d, _flip(pos, rel)) for rel in PLANE for s, d in pairs]

    shapes = [jax.ShapeDtypeStruct((4,) + s.shape, s.dtype) for s in ops]
    return Exchange(copies, 3 * len(ops), len(ops), ops, shapes)


def gather_pair(gathered):
    n = len(gathered)

    def copies(ins, outs, pos):
        c = pos[2]
        return [], [(s.at[b, c], o.at[b, c], _flip(pos, PAIR[0])) for s, o in zip(ins, outs) for b in range(4)]

    shapes = [jax.ShapeDtypeStruct(g.shape, g.dtype) for g in gathered]
    return Exchange(copies, 4 * n, 0, gathered, shapes, aliases={k: k for k in range(n)})


def swap_halves(grads):
    n = len(grads)

    def copies(ins, outs, pos):
        c = pos[2]
        return [], [(g.at[b, 1 - c], o.at[b], _flip(pos, PAIR[0])) for g, o in zip(ins, outs) for b in range(4)]

    shapes = [jax.ShapeDtypeStruct((g.shape[0],) + g.shape[2:], g.dtype) for g in grads]
    return Exchange(copies, 4 * n, 0, grads, shapes)


def scatter_chips(sums):
    n = len(sums)

    def copies(ins, outs, pos):
        me = _chip_index(pos)
        local = [(p.at[me], o.at[me]) for p, o in zip(ins, outs)]
        remote = []
        for rel in PLANE:
            peer = _flip(pos, rel)
            remote += [(p.at[_chip_index(peer)], o.at[me], peer) for p, o in zip(ins, outs)]
        return local, remote

    shapes = [jax.ShapeDtypeStruct(p.shape, p.dtype) for p in sums]
    return Exchange(copies, 3 * n, n, sums, shapes)


def share_halves(finals):
    n = len(finals)

    def copies(ins, outs, pos):
        c = pos[2]
        return [], [(f.at[c], o.at[c], _flip(pos, PAIR[0])) for f, o in zip(ins, outs)]

    shapes = [jax.ShapeDtypeStruct(f.shape, f.dtype) for f in finals]
    return Exchange(copies, n, 0, finals, shapes, aliases={k: k for k in range(n)})


def gather_everyone(vec):
    def copies(ins, outs, pos):
        me = _device_index(pos)
        (v,), (o,) = ins, outs
        return [(v, o.at[me])], [(v, o.at[me], _flip(pos, rel)) for rel in EVERYONE]

    return Exchange(copies, len(EVERYONE), 1, [vec], [jax.ShapeDtypeStruct((8,) + vec.shape, vec.dtype)])


def _row_tile(rows, cols, n_bufs, mult=8):
    cap = VMEM_LIMIT_BYTES // 2 // (2 * n_bufs * cols * 4)
    for t in range(min(rows, cap) // mult * mult, 0, -mult):
        if rows % t == 0:
            return t
    return rows


def _adamw_update(w, g, m, v):
    nm = ADAM_B1 * m + (1.0 - ADAM_B1) * g
    nv = ADAM_B2 * v + (1.0 - ADAM_B2) * jnp.square(g)
    m_hat = nm / (1.0 - ADAM_B1 ** ADAM_STEP)
    v_hat = nv / (1.0 - ADAM_B2 ** ADAM_STEP)
    return -ADAM_LR * (m_hat / (jnp.sqrt(v_hat) + ADAM_EPS) + ADAM_WD * w), nm, nv


def adamw_small(name, ws, gs, ms, vs):
    n = len(ws)

    def body(*refs):
        ins, outs = refs[:4 * n], refs[4 * n:]
        for k in range(n):
            d, nm, nv = _adamw_update(ins[k][...], ins[n + k][...], ins[2 * n + k][...], ins[3 * n + k][...])
            outs[k][...] = d
            outs[n + k][...] = nm
            outs[2 * n + k][...] = nv

    shapes = [jax.ShapeDtypeStruct(a.shape, F32) for a in ws]
    vmem = pl.BlockSpec(memory_space=pltpu.VMEM)
    res = _pcall(body, name=name, out_shape=shapes * 3, in_specs=[vmem] * (4 * n), out_specs=[vmem] * (3 * n),
                 compiler_params=pltpu.CompilerParams(vmem_limit_bytes=VMEM_LIMIT_BYTES))(*ws, *gs, *ms, *vs)
    return res[:n], res[n:2 * n], res[2 * n:]


WIRE_DTYPE = jnp.bfloat16


def add_own_half(name, grads, recv, c):
    nb, _, R, C = grads.shape
    tr = _row_tile(R, C, 3, mult=16)

    def body(c_ref, g_ref, r_ref, o_ref):
        o_ref[...] = (g_ref[...] + r_ref[...]).astype(o_ref.dtype)

    spec = pl.BlockSpec((None, tr, C), lambda b, i, c_ref: (b, i, 0))
    return _pcall(
        body, name=name, out_shape=jax.ShapeDtypeStruct(recv.shape, WIRE_DTYPE),
        grid_spec=pltpu.PrefetchScalarGridSpec(
            num_scalar_prefetch=1, grid=(nb, R // tr),
            in_specs=[pl.BlockSpec((None, None, tr, C), lambda b, i, c_ref: (b, c_ref[0], i, 0)), spec],
            out_specs=spec),
        compiler_params=_params("parallel", "parallel"),
    )(c, grads, recv)


def sum_slots(name, a, c=None):
    n, R, C = a.shape
    tr = _row_tile(R, C, n + 1, mult=16 if a.dtype.itemsize == 2 else 8)

    def body(*refs):
        a_ref, o_ref = refs[-2:]
        acc = a_ref[0].astype(F32)
        for k in range(1, n):
            acc = acc + a_ref[k].astype(F32)
        o_ref[...] = acc

    if c is None:
        return _pcall(
            body, name=name, out_shape=jax.ShapeDtypeStruct((R, C), F32), grid=(R // tr,),
            in_specs=[pl.BlockSpec((n, tr, C), lambda i: (0, i, 0))], out_specs=pl.BlockSpec((tr, C), lambda i: (i, 0)),
            compiler_params=_params("parallel"),
        )(a)
    return _pcall(
        body, name=name, out_shape=jax.ShapeDtypeStruct((2, R, C), F32),
        grid_spec=pltpu.PrefetchScalarGridSpec(
            num_scalar_prefetch=1, grid=(R // tr,),
            in_specs=[pl.BlockSpec((n, tr, C), lambda i, c_ref: (0, i, 0))],
            out_specs=pl.BlockSpec((None, tr, C), lambda i, c_ref: (c_ref[0], i, 0))),
        compiler_params=_params("parallel"),
    )(c, a)


def adamw(name, w, g_layers, m, v):
    nl, R, C = w.shape
    assert len(g_layers) == nl
    tr = _row_tile(R, C, 8 + nl)
    nr = R // tr

    def body(*refs):
        w_ref, m_ref, v_ref = refs[:3]
        g_refs = refs[3:3 + nl]
        go_ref, d_ref, nm_ref, nv_ref = refs[3 + nl:]
        l = pl.program_id(0)
        gr = g_refs[0][...]
        for k in range(1, nl):
            gr = jnp.where(l == k, g_refs[k][...], gr)
        d_ref[...], nm_ref[...], nv_ref[...] = _adamw_update(w_ref[...], gr, m_ref[...], v_ref[...])
        go_ref[...] = gr

    spec = pl.BlockSpec((None, tr, C), lambda l, i: (l, i, 0))
    g_specs = [pl.BlockSpec((tr, C), (lambda l, i, k=k: (jnp.where(l == k, i, jnp.where(l < k, 0, nr - 1)), 0)))
               for k in range(nl)]
    return _pcall(
        body, name=name, out_shape=[jax.ShapeDtypeStruct((nl, R, C), F32)] * 4, grid=(nl, nr),
        in_specs=[spec] * 3 + g_specs, out_specs=[spec] * 4, compiler_params=_params("arbitrary", "arbitrary"),
    )(w, m, v, *g_layers)


BIG = ("w_ada", "w_in", "w_ssd_out", "pool_w", "w_pool_out", "w_out", "w_gate_up", "w_down")
COL_SHARDED = ("w_ada", "w_in", "w_gate_up")
BLOCK_LAYOUT = ("w_ada", "w_gate_up")
GRAD_TRANSPOSED = ("w_in",)
FIRST_USED = ("w_ada", "w_in")
LATER_USED = tuple(k for k in BIG if k not in FIRST_USED)
READY_LAST = FIRST_USED
READY_EARLY = LATER_USED
SMALL = ("c_ctx", "b_ada", "g_mix", "conv_w", "conv_b", "dt_bias", "a_log", "d_skip", "ssd_norm_w", "pool_scale",
         "g_ffn", "g_final")
WEIGHTS = ("c_ctx", "w_ada", "b_ada", "g_mix", "w_in", "conv_w", "conv_b", "dt_bias", "a_log", "d_skip", "ssd_norm_w",
           "w_ssd_out", "pool_w", "pool_scale", "w_pool_out", "w_out", "g_ffn", "w_gate_up", "w_down", "g_final")
LAYER_KEYS = ("w_ada", "b_ada", "g_mix", "w_in", "conv_w", "conv_b", "dt_bias", "a_log", "d_skip", "ssd_norm_w",
              "w_ssd_out", "pool_w", "pool_scale", "w_pool_out", "w_out", "g_ffn", "w_gate_up", "w_down")


def _shard2d(name, a):
    if name == "pool_w":
        return a.reshape(a.shape[0], a.shape[1] * a.shape[2], a.shape[3])
    return a


def _full_from_blocks(name, a):
    nb, R, C = a.shape
    if name in BLOCK_LAYOUT:
        return a
    if name in COL_SHARDED:
        return jnp.transpose(a, (1, 0, 2)).reshape(R, nb * C)
    if name == "pool_w":
        nw = len(POOL_WINDOWS)
        return jnp.transpose(a.reshape(nb, nw, R // nw, C), (1, 0, 2, 3)).reshape(nw, nb * R // nw, C)
    return a.reshape(nb * R, C)


def _blocks_from_full(name, g):
    nb = 4
    if name in BLOCK_LAYOUT:
        return g
    if name in COL_SHARDED and name not in GRAD_TRANSPOSED:
        K, N = g.shape
        return jnp.transpose(g.reshape(K, nb, N // nb), (1, 0, 2))
    if name == "pool_w":
        nw, r, C = g.shape
        return jnp.transpose(g.reshape(nw, nb, r // nb, C), (1, 0, 2, 3)).reshape(nb, nw * r // nb, C)
    return g.reshape(nb, g.shape[0] // nb, g.shape[1])


def _pack(arrs, rows):
    flat = jnp.concatenate([a.reshape(-1).astype(F32) for a in arrs])
    return jnp.concatenate([flat, jnp.zeros((rows * 128 - flat.size,), F32)]).reshape(rows, 128)


def _unpack(vec, shapes):
    flat = vec.reshape(-1)
    out, o = [], 0
    for s in shapes:
        n = int(np.prod(s))
        out.append(flat[o:o + n].reshape(s))
        o += n
    return out


def _rows_for(shapes):
    n = sum(int(np.prod(s)) for s in shapes)
    return -(-n // (8 * 128)) * 8


def kernel(x, c, ctx, c_ctx, w_ada, b_ada, g_mix, w_in, conv_w, conv_b, dt_bias, a_log, d_skip, ssd_norm_w, w_ssd_out, pool_w, pool_scale, w_pool_out, w_out, g_ffn, w_gate_up, w_down, g_final, loss_target, m_c_ctx, m_w_ada, m_b_ada, m_g_mix, m_w_in, m_conv_w, m_conv_b, m_dt_bias, m_a_log, m_d_skip, m_ssd_norm_w, m_w_ssd_out, m_pool_w, m_pool_scale, m_w_pool_out, m_w_out, m_g_ffn, m_w_gate_up, m_w_down, m_g_final, v_c_ctx, v_w_ada, v_b_ada, v_g_mix, v_w_in, v_conv_w, v_conv_b, v_dt_bias, v_a_log, v_d_skip, v_ssd_norm_w, v_w_ssd_out, v_pool_w, v_pool_scale, v_w_pool_out, v_w_out, v_g_ffn, v_w_gate_up, v_w_down, v_g_final):
    w = dict(c_ctx=c_ctx, w_ada=w_ada, b_ada=b_ada, g_mix=g_mix, w_in=w_in, conv_w=conv_w, conv_b=conv_b, dt_bias=dt_bias,
             a_log=a_log, d_skip=d_skip, ssd_norm_w=ssd_norm_w, w_ssd_out=w_ssd_out, pool_w=pool_w, pool_scale=pool_scale,
             w_pool_out=w_pool_out, w_out=w_out, g_ffn=g_ffn, w_gate_up=w_gate_up, w_down=w_down, g_final=g_final)
    m = dict(c_ctx=m_c_ctx, w_ada=m_w_ada, b_ada=m_b_ada, g_mix=m_g_mix, w_in=m_w_in, conv_w=m_conv_w, conv_b=m_conv_b,
             dt_bias=m_dt_bias, a_log=m_a_log, d_skip=m_d_skip, ssd_norm_w=m_ssd_norm_w, w_ssd_out=m_w_ssd_out,
             pool_w=m_pool_w, pool_scale=m_pool_scale, w_pool_out=m_w_pool_out, w_out=m_w_out, g_ffn=m_g_ffn,
             w_gate_up=m_w_gate_up, w_down=m_w_down, g_final=m_g_final)
    v = dict(c_ctx=v_c_ctx, w_ada=v_w_ada, b_ada=v_b_ada, g_mix=v_g_mix, w_in=v_w_in, conv_w=v_conv_w, conv_b=v_conv_b,
             dt_bias=v_dt_bias, a_log=v_a_log, d_skip=v_d_skip, ssd_norm_w=v_ssd_norm_w, w_ssd_out=v_w_ssd_out,
             pool_w=v_pool_w, pool_scale=v_pool_scale, w_pool_out=v_w_pool_out, w_out=v_w_out, g_ffn=v_g_ffn,
             w_gate_up=v_w_gate_up, w_down=v_w_down, g_final=v_g_final)
    assert x.shape[0] == 1, "one example per device"
    pos = _position()
    core = pos[2].astype(jnp.int32).reshape(1)
    n_layers = w_in.shape[0]
    assert n_layers == 2
    dims = (ssd_norm_w.shape[1], conv_w.shape[2] * 4, dt_bias[0].size, pool_scale.shape[1])
    shard = {k: _shard2d(k, w[k]) for k in BIG}

    def halves(a):
        return a.reshape(a.shape[:-2] + (2, a.shape[-2] // 2, a.shape[-1]))

    def whole(a):
        return a.reshape(a.shape[:-3] + (2 * a.shape[-2], a.shape[-1]))

    def wire_shards(l, names):
        return [halves(shard[k][l].astype(MXU_DTYPE)) for k in names]

    def full_weights(names, gathered):
        return {k: _full_from_blocks(k, whole(a)) for k, a in zip(names, gathered)}

    first = comm_call("gather0_chips", gather_chips(wire_shards(0, FIRST_USED), conv=conv_w))
    got0 = full_weights(FIRST_USED, comm_call("gather0_pair", gather_pair(first[:-1])))
    conv_all = first[-1]
    conv_full = [jnp.transpose(conv_all[:, l], (1, 0, 2)).reshape(conv_all.shape[2], -1) for l in range(n_layers)]

    boxes = {}

    def layer_w_fn(l):
        if l == 0:
            full = dict(got0)
            late = {k: None for k in LATER_USED}
        else:
            full = full_weights(BIG, boxes[("fwd", 0)]["gate_up_mm"])
            late = {}
        full["conv_w"] = conv_full[l]
        lw = LazyDict(_prep_layer_weights(*[full[k] if k in full else (None if k in late else w[k][l]) for k in LAYER_KEYS]))
        for i, k in enumerate(late):
            lw[k] = (lambda i=i, k=k: _full_from_blocks(k, whole(boxes[("fwd", 0)]["conv"][i])))
        return lw

    def fwd_hosts(l, box):
        boxes[("fwd", l)] = box
        if l != 0:
            return None
        return {"in_mm": lambda box: gather_chips(wire_shards(0, LATER_USED)), "conv": lambda box: gather_pair(box["in_mm"]),
                "ssd": lambda box: gather_chips(wire_shards(1, BIG)), "gate_up_mm": lambda box: gather_pair(box["ssd"])}

    def blocks(gl, names):
        return [halves(_blocks_from_full(k, gl[k])) for k in names]

    def pair_sums(tag, names, G, recv):
        return [add_own_half(f"pair_sum{tag}_{k}", g, r, core) for k, g, r in zip(names, G, recv)]

    def chip_sums(tag, names, parts):
        return [sum_slots(f"chip_sum{tag}_{k}", p, core) for k, p in zip(names, parts)]

    def reduce_now(tag, gl, names):
        G = blocks(gl, names)
        pair = pair_sums(tag, names, G, comm_call(f"swap{tag}", swap_halves(G)))
        fin = chip_sums(tag, names, comm_call(f"scatter{tag}", scatter_chips(pair)))
        return [whole(a) for a in comm_call(f"share{tag}", share_halves(fin))]

    small_layers = {}
    n_big = len(BIG)

    def bwd_hosts(l, grads, box):
        boxes[("bwd", l)] = box
        if l != 0:
            return None
        gl1 = _unprep_layer_grads(grads[1], dims)
        small_layers[1] = gl1
        G1 = blocks(gl1, BIG)
        early = {}

        def gate_host(box):
            early["G"] = blocks(box["g"], READY_EARLY)
            return swap_halves(early["G"])

        def scan_host(box):
            return combine(scatter_chips(pair_sums("1", BIG, G1, box["down_dx"])),
                           scatter_chips(pair_sums("0e", READY_EARLY, early["G"], box["ssd_gate"])))

        def conv_host(box):
            return combine(share_halves(chip_sums("1", BIG, box["ssd"][:n_big])),
                           share_halves(chip_sums("0e", READY_EARLY, box["ssd"][n_big:])))

        return {"down_dx": lambda box: swap_halves(G1), "ssd_gate": gate_host, "ssd": scan_host, "in_dx": conv_host}

    loss, grad_x, grads, d_c_ctx, d_g_final = local_step(
        x[0], ctx[0], c[0], c_ctx, loss_target[0], layer_w_fn, n_layers, g_final, fwd_hosts, bwd_hosts)
    shared =[whole(a) for a in boxes[("bwd", 0)]["in_dx"]]
    reduced1 = shared[:n_big]
    gl0 = _unprep_layer_grads(grads[0], dims)
    small_layers[0] = gl0
    red0 = dict(zip(READY_EARLY, shared[n_big:]))
    red0.update(zip(READY_LAST, reduce_now("0", gl0, READY_LAST)))
    reduced0 = [red0[k] for k in BIG]

    small_full = dict(c_ctx=d_c_ctx, g_final=d_g_final.reshape(-1))
    for k in SMALL:
        if k not in small_full:
            small_full[k] = jnp.stack([small_layers[l][k] for l in range(n_layers)])
    shapes = [small_full[k].shape for k in SMALL] + [(1,)]
    packed = _pack([small_full[k] for k in SMALL] + [loss.reshape(1)], _rows_for(shapes))
    total = sum_slots("small_sum", comm_call("gather_small", gather_everyone(packed))[0])
    *small_vals, loss = _unpack(total, shapes)
    loss = loss.reshape(())
    small_g = dict(zip(SMALL, small_vals))
    cw = conv_w.shape[2]
    small_g["conv_w"] = lax.dynamic_slice_in_dim(small_g["conv_w"], _chip_index(pos) * cw, cw, axis=2)

    grad, delta, new_m, new_v = {}, {}, {}, {}
    for k, g0, g1 in zip(BIG, reduced0, reduced1):
        shp = w[k].shape
        if k in GRAD_TRANSPOSED:
            flat = lambda a: jnp.swapaxes(a, 1, 2)
            back = lambda a: jnp.swapaxes(a, 1, 2)
        else:
            flat = lambda a: _shard2d(k, a)
            back = lambda a: a.reshape(shp)
        outs = adamw(f"adamw_{k}", flat(w[k]), [g0, g1], flat(m[k]), flat(v[k]))
        grad[k], delta[k], new_m[k], new_v[k] = [back(a) for a in outs]
    flat2 = lambda d: [d[k].reshape(-1, d[k].shape[-1]) for k in SMALL]
    d_, m_, v_ = adamw_small("adamw_small", flat2(w), flat2(small_g), flat2(m), flat2(v))
    for k, dd, mm, vv in zip(SMALL, d_, m_, v_):
        shp = w[k].shape
        grad[k], delta[k], new_m[k], new_v[k] = small_g[k], dd.reshape(shp), mm.reshape(shp), vv.reshape(shp)

    return (loss, grad_x[None], *[grad[k] for k in WEIGHTS], *[delta[k] for k in WEIGHTS],
            *[new_m[k] for k in WEIGHTS], *[new_v[k] for k in WEIGHTS])
```

```python
import functools

import jax
import jax.numpy as jnp
import numpy as np
from jax import lax
from jax.experimental import pallas as pl
from jax.experimental.pallas import tpu as pltpu

F32 = jnp.float32
MXU_DTYPE = jnp.bfloat16
ACT_DTYPE = jnp.bfloat16
VMEM_LIMIT_BYTES = 48 * 1024 * 1024
EPS = 1e-6
NEG = -1e30

SSD_HEADDIM = 64
SSD_GROUPS = 8
SSD_STATE = 128
SSD_CHUNK = 128
SSD_GROUPS_PER_STEP = 8
SSD_CONV = 5
GRID_W = 64
POOL_WINDOWS = (2, 4, 8, 16)
ROW_TILE = 256
DT_PAD = 512

ADAM_LR = 0.001
ADAM_B1 = 0.9
ADAM_B2 = 0.999
ADAM_EPS = 1e-08
ADAM_WD = 0.01
ADAM_STEP = 10

MESH = pl.DeviceIdType.MESH


def _pcall(body, **kw):
    return pl.pallas_call(body, **kw)


def _params(*sem):
    return pltpu.CompilerParams(dimension_semantics=tuple(sem), vmem_limit_bytes=VMEM_LIMIT_BYTES)


def _pick_tile(n, cands):
    for t in cands:
        if n % t == 0:
            return t
    return n


PLANE = ((1, 0, 0), (0, 1, 0), (1, 1, 0))
PAIR = ((0, 0, 1),)
EVERYONE = tuple((a, b, d) for a in (0, 1) for b in (0, 1) for d in (0, 1) if a + b + d)
HBM = pl.BlockSpec(memory_space=pl.ANY)


def _position():
    return lax.axis_index("x"), lax.axis_index("y"), lax.axis_index("c")


def _flip(pos, rel):
    return tuple(1 - p if r else p for p, r in zip(pos, rel))


def _chip_index(pos):
    return 2 * pos[0] + pos[1]


def _device_index(pos):
    return 4 * pos[0] + 2 * pos[1] + pos[2]


class Exchange:
    def __init__(self, copies, n_remote, n_local, operands, out_shapes, aliases=None):
        self.copies, self.n_remote, self.n_local = copies, n_remote, n_local
        self.operands, self.out_shapes, self.aliases = list(operands), list(out_shapes), dict(aliases or {})

    def scratch(self):
        return [pltpu.SemaphoreType.DMA((max(self.n_remote, 1),)), pltpu.SemaphoreType.DMA((max(self.n_remote, 1),)),
                pltpu.SemaphoreType.DMA((max(self.n_local, 1),))]

    def descriptors(self, ins, outs, sems):
        send_sems, recv_sems, local_sems = sems
        local, remote = self.copies(ins, outs, _position())
        assert len(local) == self.n_local and len(remote) == self.n_remote
        cps = [pltpu.make_async_copy(src, dst, local_sems.at[k]) for k, (src, dst) in enumerate(local)]
        cps += [pltpu.make_async_remote_copy(src_ref=src, dst_ref=dst, send_sem=send_sems.at[k], recv_sem=recv_sems.at[k],
                                             device_id=peer, device_id_type=MESH) for k, (src, dst, peer) in enumerate(remote)]
        return cps


def combine(a, b):
    na, nao = len(a.operands), len(a.out_shapes)

    def copies(ins, outs, pos):
        la, ra = a.copies(ins[:na], outs[:nao], pos)
        lb, rb = b.copies(ins[na:], outs[nao:], pos)
        return la + lb, ra + rb

    aliases = dict(a.aliases)
    aliases.update({na + k: nao + v for k, v in b.aliases.items()})
    return Exchange(copies, a.n_remote + b.n_remote, a.n_local + b.n_local, a.operands + b.operands,
                    a.out_shapes + b.out_shapes, aliases)


class LazyDict(dict):
    def __getitem__(self, key):
        v = dict.__getitem__(self, key)
        if callable(v):
            v = v()
            dict.__setitem__(self, key, v)
        return v


def comm_call(name, ex):
    n_in, n_out = len(ex.operands), len(ex.out_shapes)

    def body(*refs):
        cps = ex.descriptors(refs[:n_in], refs[n_in:n_in + n_out], refs[n_in + n_out:])
        for cp in cps:
            cp.start()
        for cp in cps:
            cp.wait()

    return _pcall(
        body, name=name, out_shape=ex.out_shapes, in_specs=[HBM] * n_in, out_specs=[HBM] * n_out,
        scratch_shapes=ex.scratch(), input_output_aliases=ex.aliases,
        compiler_params=pltpu.CompilerParams(has_side_effects=True),
    )(*ex.operands)


def hosted_call(body, ex, operands, *, name, out_shape, grid, in_specs, out_specs, scratch_shapes=()):
    n_in, n_out, n_scr = len(operands), len(out_shape), len(scratch_shapes)
    sem = ("arbitrary",) * len(grid)
    if ex is None:
        res = _pcall(body, name=name, out_shape=list(out_shape), grid=grid, in_specs=list(in_specs),
                     out_specs=list(out_specs), scratch_shapes=list(scratch_shapes), compiler_params=_params(*sem))(*operands)
        return res, []
    x_in, x_out = len(ex.operands), len(ex.out_shapes)

    def wrapped(*refs):
        o = 0
        ins = refs[o:o + n_in]; o += n_in
        xins = refs[o:o + x_in]; o += x_in
        outs = refs[o:o + n_out]; o += n_out
        xouts = refs[o:o + x_out]; o += x_out
        scr = refs[o:o + n_scr]; o += n_scr
        sems = refs[o:]
        first = last = None
        for a, n in enumerate(grid):
            i = pl.program_id(a)
            first = (i == 0) if first is None else first & (i == 0)
            last = (i == n - 1) if last is None else last & (i == n - 1)

        @pl.when(first)
        def _():
            for cp in ex.descriptors(xins, xouts, sems):
                cp.start()

        body(*ins, *outs, *scr)

        @pl.when(last)
        def _():
            for cp in ex.descriptors(xins, xouts, sems):
                cp.wait()

    aliases = {n_in + k: n_out + v for k, v in ex.aliases.items()}
    res = _pcall(
        wrapped, name=name, out_shape=list(out_shape) + ex.out_shapes, grid=grid,
        in_specs=list(in_specs) + [HBM] * x_in, out_specs=list(out_specs) + [HBM] * x_out,
        scratch_shapes=list(scratch_shapes) + ex.scratch(), input_output_aliases=aliases,
        compiler_params=pltpu.CompilerParams(dimension_semantics=sem, vmem_limit_bytes=VMEM_LIMIT_BYTES,
                                             has_side_effects=True),
    )(*operands, *ex.operands)
    return res[:n_out], res[n_out:]


def _dot(a, b, dims):
    return lax.dot_general(a.astype(MXU_DTYPE), b.astype(MXU_DTYPE), (dims, ((), ())), preferred_element_type=F32)


_NN = ((1,), (0,))
_NT = ((1,), (1,))
_TN = ((0,), (0,))


@jax.custom_vjp
def _mm(a, b):
    return _dot(a, b, _NN)


def _mm_fwd(a, b):
    return _mm(a, b), (a, b)


def _mm_bwd(res, g):
    a, b = res
    return _dot(g, b, _NT).astype(a.dtype), _dot(a, g, _TN).astype(b.dtype)


_mm.defvjp(_mm_fwd, _mm_bwd)


@jax.custom_vjp
def _mm_nt(a, b):
    return _dot(a, b, _NT)


def _mm_nt_fwd(a, b):
    return _mm_nt(a, b), (a, b)


def _mm_nt_bwd(res, g):
    a, b = res
    return _dot(g, b, _NN).astype(a.dtype), _dot(g, a, _TN).astype(b.dtype)


_mm_nt.defvjp(_mm_nt_fwd, _mm_nt_bwd)


@jax.custom_vjp
def _mm_tn(a, b):
    return _dot(a, b, _TN)


def _mm_tn_fwd(a, b):
    return _mm_tn(a, b), (a, b)


def _mm_tn_bwd(res, g):
    a, b = res
    return _dot(b, g, _NT).astype(a.dtype), _dot(a, g, _NN).astype(b.dtype)


_mm_tn.defvjp(_mm_tn_fwd, _mm_tn_bwd)


def _dot_exact(m01, v):
    m = m01.astype(jnp.bfloat16)
    hi = v.astype(jnp.bfloat16)
    r1 = v - hi.astype(F32)
    mid = r1.astype(jnp.bfloat16)
    lo = (r1 - mid.astype(F32)).astype(jnp.bfloat16)
    out = jnp.dot(m, hi, preferred_element_type=F32)
    out = out + jnp.dot(m, mid, preferred_element_type=F32)
    return out + jnp.dot(m, lo, preferred_element_type=F32)


@jax.custom_vjp
def _lin01(m, mt, v):
    return _dot_exact(m, v)


def _lin01_fwd(m, mt, v):
    return _dot_exact(m, v), (m, mt)


def _lin01_bwd(res, g):
    m, mt = res
    return jnp.zeros_like(m), jnp.zeros_like(mt), _dot_exact(mt, g)


_lin01.defvjp(_lin01_fwd, _lin01_bwd)


MATMUL_VMEM_BUDGET = VMEM_LIMIT_BYTES * 5 // 6


def _mm_tiles(m, n, k_bytes_a, k_bytes_b, out_bytes, cands_m, cands_n):
    best = None
    for tm in cands_m:
        if m % tm:
            continue
        for tn in cands_n:
            if n % tn:
                continue
            need = 2 * (tm * k_bytes_a + tn * k_bytes_b + tm * tn * out_bytes)
            if need <= MATMUL_VMEM_BUDGET and (best is None or tm * tn > best[0] * best[1]):
                best = (tm, tn)
    assert best is not None, (m, n)
    return best


_ROW_CANDS = (4352, 2176, 1088, 768, 544, 512, 272, 256, 128, 16)
_COL_CANDS = (2816, 2048, 1408, 1024, 512, 256, 128)


def _one(res, xres, ex):
    return res[0] if ex is None else (res[0], xres)


def _block_cands(c):
    return (c,) + tuple(t for t in (512, 256, 128) if c % t == 0)


def matmul_nn(name, a, b, out_dtype=F32, ex=None, col0=0, ncols=None):
    M, K = a.shape
    if b.ndim == 3:
        nb, _, C = b.shape
        N, cands = nb * C, _block_cands(C)
    else:
        N, cands = (b.shape[1] - col0 if ncols is None else ncols), (512, 256, 128)
    tm, tn = _mm_tiles(M, N, K * a.dtype.itemsize, K * b.dtype.itemsize, jnp.dtype(out_dtype).itemsize,
                       _ROW_CANDS, cands)
    if b.ndim == 3:
        per = C // tn
        b_spec = pl.BlockSpec((None, K, tn), lambda j, i: (j // per, 0, j % per))
    else:
        assert col0 % tn == 0
        first = col0 // tn
        b_spec = pl.BlockSpec((K, tn), lambda j, i: (0, first + j))

    def body(a_ref, b_ref, o_ref):
        o_ref[...] = _dot(a_ref[...], b_ref[...], _NN).astype(o_ref.dtype)

    res, xres = hosted_call(
        body, ex, [a, b], name=name, out_shape=[jax.ShapeDtypeStruct((M, N), out_dtype)], grid=(N // tn, M // tm),
        in_specs=[pl.BlockSpec((tm, K), lambda j, i: (i, 0)), b_spec],
        out_specs=[pl.BlockSpec((tm, tn), lambda j, i: (i, j))])
    return _one(res, xres, ex)


def matmul_nt(name, g, b, out_dtype=F32, ex=None, offsets=None):
    pieces = list(g) if isinstance(g, (list, tuple)) else [g]
    offsets = list(offsets) if offsets is not None else [0]
    M = pieces[0].shape[0]
    if b.ndim == 3:
        nb, K, C = b.shape
        N = nb * C
        assert len(pieces) == 1
    else:
        K, N = b.shape
    g_bytes = sum(p.shape[1] * p.dtype.itemsize for p in pieces)
    tm, tk = _mm_tiles(M, K, g_bytes, N * b.dtype.itemsize, jnp.dtype(out_dtype).itemsize, _ROW_CANDS, _COL_CANDS)

    def body(*refs):
        b_ref, o_ref = refs[-2:]
        acc = None
        if b.ndim == 3:
            parts = [_dot(refs[0][:, k * C:(k + 1) * C], b_ref[k], _NT) for k in range(nb)]
        else:
            parts = [_dot(g_ref[...], b_ref[:, off:off + g_ref.shape[1]], _NT) for g_ref, off in zip(refs[:-2], offsets)]
        for part in parts:
            acc = part if acc is None else acc + part
        o_ref[...] = acc.astype(o_ref.dtype)

    b_spec = (pl.BlockSpec((nb, tk, C), lambda j, i: (0, j, 0)) if b.ndim == 3
              else pl.BlockSpec((tk, N), lambda j, i: (j, 0)))
    res, xres = hosted_call(
        body, ex, pieces + [b], name=name, out_shape=[jax.ShapeDtypeStruct((M, K), out_dtype)], grid=(K // tk, M // tm),
        in_specs=[pl.BlockSpec((tm, p.shape[1]), lambda j, i: (i, 0)) for p in pieces] + [b_spec],
        out_specs=[pl.BlockSpec((tm, tk), lambda j, i: (i, j))])
    return _one(res, xres, ex)


def matmul_tn(name, a, g, ex=None, blocks=1):
    M, K = a.shape
    N = g.shape[1]
    C = N // blocks
    tk, tn = _mm_tiles(K, N, M * a.dtype.itemsize, M * g.dtype.itemsize, 4, (512, 256, 128),
                       (512, 256, 128) if blocks == 1 else _block_cands(C))

    def body(a_ref, g_ref, o_ref):
        o_ref[...] = _dot(a_ref[...], g_ref[...], _TN)

    if blocks == 1:
        out_shape, out_spec = jax.ShapeDtypeStruct((K, N), F32), pl.BlockSpec((tk, tn), lambda i, j: (i, j))
    else:
        per = C // tn
        out_shape = jax.ShapeDtypeStruct((blocks, K, C), F32)
        out_spec = pl.BlockSpec((None, tk, tn), lambda i, j: (j // per, i, j % per))
    res, xres = hosted_call(
        body, ex, [a, g], name=name, out_shape=[out_shape], grid=(K // tk, N // tn),
        in_specs=[pl.BlockSpec((M, tk), lambda i, j: (0, i)), pl.BlockSpec((M, tn), lambda i, j: (0, j))],
        out_specs=[out_spec])
    return _one(res, xres, ex)


class Arg:
    def __init__(self, arr, block, imap, kind):
        self.arr, self.block, self.imap, self.kind = arr, block, imap, kind


class Rows:
    def __init__(self, nt, nct, tm, ncol=1):
        self.nt, self.nct, self.tm, self.ncol = nt, nct, tm, ncol

    def seg(self, i):
        return jnp.where(i >= self.nct, 1, 0)

    def spec(self, block, imap):
        return pl.BlockSpec(block, lambda j, i: imap(j, i, self.seg(i)))

    def row(self, arr, width, cb0=0, follow=False, roff=0, stride=1):
        f = stride if follow else 0
        return Arg(arr, (self.tm, width), lambda j, i, s: (i + roff, cb0 + f * j), "row")

    def vec(self, arr, follow=False, kind="acc"):
        w = arr.shape[1] // (self.ncol if follow else 1)
        f = 1 if follow else 0
        return Arg(arr, (1, w), lambda j, i, s: (0, f * j), kind)

    def segvec(self, arr, kind="seg"):
        return Arg(arr, (None, 1, arr.shape[2]), lambda j, i, s: (s, 0, 0), kind)


def _load(ref):
    return ref[...].astype(F32) if ref.dtype != F32 else ref[...]


def stage_fwd(name, f, rows, args, outs):
    n_in = len(args)

    def body(*refs):
        vals = [_load(r) for r in refs[:n_in]]
        res = f(*vals)
        for r, v in zip(refs[n_in:], res):
            r[...] = v.astype(r.dtype)

    T = rows.nt * rows.tm
    out_shape = [jax.ShapeDtypeStruct((T, w * (rows.ncol if fo else 1)), dt) for w, dt, fo in outs]
    out_specs = [pl.BlockSpec((rows.tm, w), (lambda j, i, fo=fo: (i, j if fo else 0))) for w, dt, fo in outs]
    res = _pcall(
        body, name=name, out_shape=out_shape, grid=(rows.ncol, rows.nt),
        in_specs=[rows.spec(a.block, a.imap) for a in args], out_specs=out_specs,
        compiler_params=_params("parallel", "parallel"),
    )(*[a.arr for a in args])
    return res


def stage_bwd(name, f, rows, args, cots, row_dtypes, ex=None, primal=()):
    n_in, n_ct = len(args), len(cots)
    diff = [k for k, a in enumerate(args) if a.kind != "const"]
    row_dt = {}
    for k in diff:
        if args[k].kind == "row":
            row_dt[k] = row_dtypes[len(row_dt)]

    def body(*refs):
        i = pl.program_id(1)
        vals = [_load(r) for r in refs[:n_in]]
        cts = tuple(_load(r) for r in refs[n_in:n_in + n_ct])
        outs = refs[n_in + n_ct:]

        def g(*dv):
            full = list(vals)
            for k, v in zip(diff, dv):
                full[k] = v
            return tuple(f(*full))

        prim, vjp = jax.vjp(g, *[vals[k] for k in diff])
        grads = vjp(cts)
        for o, v in zip(outs[len(diff):], prim):
            o[...] = v.astype(o.dtype)
        for k, o, gr in zip(diff, outs, grads):
            kind = args[k].kind
            if kind == "row":
                o[...] = gr.astype(o.dtype)
            else:
                first = (i == 0) | (i == rows.nct) if kind == "seg" else (i == 0)

                @pl.when(first)
                def _():
                    o[...] = gr.astype(o.dtype)

                @pl.when(jnp.logical_not(first))
                def _():
                    o[...] += gr.astype(o.dtype)

    T = rows.nt * rows.tm
    out_shape, out_specs = [], []
    for k in diff:
        a = args[k]
        if a.kind == "row":
            out_shape.append(jax.ShapeDtypeStruct((T, a.block[1] * (rows.ncol if _follows(a) else 1)), row_dt[k]))
            fo = _follows(a)
            out_specs.append(pl.BlockSpec(a.block, (lambda j, i, fo=fo: (i, j if fo else 0))))
        else:
            out_shape.append(jax.ShapeDtypeStruct(a.arr.shape, F32))
            out_specs.append(rows.spec(a.block, a.imap))
    for w, dt in primal:
        out_shape.append(jax.ShapeDtypeStruct((T, w), dt))
        out_specs.append(pl.BlockSpec((rows.tm, w), lambda j, i: (i, 0)))
    res, xres = hosted_call(
        body, ex, [a.arr for a in list(args) + list(cots)], name=name, out_shape=out_shape, grid=(rows.ncol, rows.nt),
        in_specs=[rows.spec(a.block, a.imap) for a in list(args) + list(cots)], out_specs=out_specs)
    return res if ex is None else (res, xres)


def _follows(a):
    return a.imap(1, 0, 0)[-1] != a.imap(0, 0, 0)[-1]


def _rms(x):
    return x * lax.rsqrt(jnp.mean(x * x, axis=-1, keepdims=True) + EPS)


def f_norm_mod(x, g, sh, sc):
    return ((_rms(x) * g) * (1.0 + sc) + sh,)


def f_resid_norm_mod(x, mo, ga, g, sh, sc):
    x1 = x + ga * mo
    return x1, (_rms(x1) * g) * (1.0 + sc) + sh


def f_resid(x, dn, ga):
    return (x + ga * dn,)


def f_silu(x):
    return (x * jax.nn.sigmoid(x),)


def f_bias(x, b):
    return (x + b,)


def f_ssd_gate(y0, y1, xs, z, dskip, nw):
    y = y0 + y1 + dskip * xs
    return (_rms(y * (z * jax.nn.sigmoid(z))) * nw,)


def f_pool(u, pmat, pmat_t, inv_cnt, pw, scale):
    pm = _lin01(pmat, pmat_t, u) * inv_cnt - u
    return (_mm(pm, pw) * scale,)


def f_merge(o_ssd, o_pool, gl_ssd, gl_pool):
    return (jax.nn.sigmoid(gl_ssd) * o_ssd + jax.nn.sigmoid(gl_pool) * o_pool,)


def _column_splitter(n):
    @jax.custom_vjp
    def split(x):
        w = x.shape[1] // n
        return tuple(x[:, k * w:(k + 1) * w] for k in range(n))

    def fwd(x):
        return split(x), None

    def bwd(_, g):
        return (jnp.concatenate(g, axis=1),)

    split.defvjp(fwd, bwd)
    return split


_halve_cols = _column_splitter(2)
_quarter_cols = _column_splitter(len(POOL_WINDOWS))


def f_swiglu(gu):
    a, b = _halve_cols(gu)
    return ((a * jax.nn.sigmoid(a)) * b,)


def f_pool_all(u, pmat, pmat_t, inv_cnt, scale, *pws):
    outs = [f_pool(part, pmat[k], pmat_t[k], inv_cnt[k], pws[k], 1.0)[0] for k, part in enumerate(_quarter_cols(u))]
    return (jnp.concatenate(outs, axis=1) * scale,)


def f_loss_resid(x1, dn, ga, tgt, g):
    err = _rms(x1 + ga * dn) * g - tgt
    return (0.5 * jnp.mean(err * err, axis=-1, keepdims=True),)


CONV_TILE = 128


CONV_GAP = 8


def _gapped(v, n_ctx):
    z = jnp.zeros((CONV_GAP, v.shape[1]), v.dtype)
    return jnp.concatenate([v[:n_ctx], z, v[n_ctx:], z], axis=0)


def _ungapped(v, n_ctx):
    return jnp.concatenate([v[:n_ctx], v[n_ctx + CONV_GAP:v.shape[0] - CONV_GAP]], axis=0)


def _shift_rows(v, j):
    return v if j == 0 else pltpu.roll(v, (-j) % v.shape[0], 0)


def conv_fwd(name, proj, conv_w, conv_b, n_ctx, width, ex=None):
    T = proj.shape[0]
    half = SSD_CONV // 2

    def body(u_ref, w_ref, b_ref, o_ref):
        u = _gapped(u_ref[...].astype(F32), n_ctx)
        pre = jnp.broadcast_to(b_ref[...], u.shape)
        for k in range(SSD_CONV):
            pre = pre + w_ref[k:k + 1, :] * _shift_rows(u, k - half)
        o_ref[...] = _ungapped(pre * jax.nn.sigmoid(pre), n_ctx)

    col = lambda t: (0, t)
    res, xres = hosted_call(
        body, ex, [proj, conv_w, conv_b], name=name, out_shape=[jax.ShapeDtypeStruct((T, width), F32)],
        grid=(width // CONV_TILE,),
        in_specs=[pl.BlockSpec((T, CONV_TILE), col), pl.BlockSpec((SSD_CONV, CONV_TILE), col),
                  pl.BlockSpec((1, CONV_TILE), col)],
        out_specs=[pl.BlockSpec((T, CONV_TILE), col)])
    return res[0], xres


def conv_bwd(name, proj, conv_w, conv_b, d_act2, d_skip, n_ctx, width, ex=None):
    T = proj.shape[0]
    half = SSD_CONV // 2

    def body(u_ref, w_ref, b_ref, c0_ref, c1_ref, cs_ref, du_ref, dw_ref, db_ref):
        t = pl.program_id(0)
        u = _gapped(u_ref[...].astype(F32), n_ctx)
        pre = jnp.broadcast_to(b_ref[...], u.shape)
        for k in range(SSD_CONV):
            pre = pre + w_ref[k:k + 1, :] * _shift_rows(u, k - half)
        sg = jax.nn.sigmoid(pre)
        ct = c0_ref[...].astype(F32) + c1_ref[...].astype(F32) + jnp.where(t % 4 < 2, cs_ref[...].astype(F32), 0.0)
        dpre = _gapped(ct, n_ctx) * (sg * (1.0 + pre * (1.0 - sg)))
        du = jnp.zeros_like(u)
        for k in range(SSD_CONV):
            du = du + w_ref[k:k + 1, :] * _shift_rows(dpre, half - k)
            dw_ref[k:k + 1, :] = jnp.sum(dpre * _shift_rows(u, k - half), axis=0, keepdims=True)
        du_ref[...] = _ungapped(du, n_ctx).astype(du_ref.dtype)
        db_ref[...] = jnp.sum(dpre, axis=0, keepdims=True)

    col = lambda t: (0, t)
    skip_col = lambda t: (0, (t // 4) * 2 + jnp.minimum(t % 4, 1))
    res, xres = hosted_call(
        body, ex, [proj, conv_w, conv_b, d_act2[0], d_act2[1], d_skip], name=name,
        out_shape=[jax.ShapeDtypeStruct((T, width), ACT_DTYPE), jax.ShapeDtypeStruct((SSD_CONV, width), F32),
                   jax.ShapeDtypeStruct((1, width), F32)],
        grid=(width // CONV_TILE,),
        in_specs=[pl.BlockSpec((T, CONV_TILE), col), pl.BlockSpec((SSD_CONV, CONV_TILE), col),
                  pl.BlockSpec((1, CONV_TILE), col), pl.BlockSpec((T, CONV_TILE), col),
                  pl.BlockSpec((T, CONV_TILE), col), pl.BlockSpec((T, CONV_TILE), skip_col)],
        out_specs=[pl.BlockSpec((T, CONV_TILE), col), pl.BlockSpec((SSD_CONV, CONV_TILE), col),
                   pl.BlockSpec((1, CONV_TILE), col)])
    return res[0], res[1], res[2], xres


@jax.custom_vjp
def _cumsum_mat(tri, tri_t, a):
    return jnp.dot(tri, a, precision=lax.Precision.HIGHEST, preferred_element_type=F32)


def _cumsum_fwd(tri, tri_t, a):
    return _cumsum_mat(tri, tri_t, a), (tri, tri_t)


def _cumsum_bwd(res, g):
    tri, tri_t = res
    return (jnp.zeros_like(tri), jnp.zeros_like(tri_t),
            jnp.dot(tri_t, g, precision=lax.Precision.HIGHEST, preferred_element_type=F32))


_cumsum_mat.defvjp(_cumsum_fwd, _cumsum_bwd)


def _ssd_dt(dtraw, dt_bias, a_log, tri, tri_t):
    dt_all = jax.nn.softplus(dtraw + dt_bias)
    a_all = dt_all * (-jnp.exp(a_log))
    return dt_all, a_all, _cumsum_mat(tri, tri_t, a_all)


def _ssd_chunk(xs, bm, cm, dt_all, a_all, s_all, s_in, mask, idx0):
    (xs,), (s_in,) = xs, s_in
    Q = xs.shape[0]
    hpg = xs.shape[1] // SSD_HEADDIM
    lane = lax.broadcasted_iota(jnp.int32, dt_all.shape, 1)
    head = lax.broadcasted_iota(jnp.int32, xs.shape, 1) // SSD_HEADDIM
    head1 = lax.broadcasted_iota(jnp.int32, (1, xs.shape[1]), 1) // SSD_HEADDIM

    def pick(v, r):
        return jnp.sum(jnp.where(lane == idx0 + r, v, 0.0), axis=1, keepdims=True)

    def expand(cols, hd):
        out = cols[hpg - 1]
        for r in range(hpg - 2, -1, -1):
            out = jnp.where(hd == r, cols[r], out)
        return out

    def spread(*cols):
        return expand([jnp.broadcast_to(c, xs.shape) for c in cols], head)

    dt_r = [pick(dt_all, r) for r in range(hpg)]
    s_r = [pick(s_all, r) for r in range(hpg)]
    stot_r = [jnp.sum(jnp.where(lane == idx0 + r, a_all, 0.0), keepdims=True).reshape(1, 1) for r in range(hpg)]

    xd = xs * spread(*dt_r)
    cb = _mm_nt(cm, bm)
    weights, stacked = [], []
    for r in range(hpg):
        sm = jnp.broadcast_to(s_r[r], (Q, Q))
        weights.append(cb * jnp.exp(jnp.where(mask, sm - sm.T, NEG)))
        stacked.append(jnp.where(head == r, xd, 0.0))
    y = spread(*[jnp.exp(c) for c in s_r]) * _mm(cm, s_in)
    y = y + _mm(jnp.concatenate(weights, axis=1), jnp.concatenate(stacked, axis=0))
    to_end = spread(*[jnp.exp(t - c) for t, c in zip(stot_r, s_r)])
    carry = expand([jnp.broadcast_to(jnp.exp(t), (1, xs.shape[1])) for t in stot_r], head1)
    s_out = carry * s_in + _mm_tn(bm, xd * to_end)
    return [y], [s_out]


def _scan_consts():
    q = SSD_CHUNK
    i = np.arange(q)[:, None]
    j = np.arange(q)[None, :]
    fwd = (j <= i).astype(np.float32)
    bwd = (j >= i).astype(np.float32)
    tri = np.stack([fwd, bwd])
    return jnp.asarray(tri), jnp.asarray(np.stack([fwd.T, bwd.T]))


def _chunk_of(d, k, ncc, nc):
    rev = jnp.where(k < ncc, ncc - 1 - k, nc - 1 + ncc - k)
    return jnp.where(d == 0, k, rev)


def ssd_fwd(name, xbc, dtraw, dt_bias, a_log, n_ctx, ex=None):
    T = xbc.shape[0]
    q, G = SSD_CHUNK, SSD_GROUPS
    nc, ncc = T // q, n_ctx // q
    gw = xbc.shape[1] // G
    xw = gw - 2 * SSD_STATE
    hpg = xw // SSD_HEADDIM
    nh = G * hpg
    tri, tri_t = _scan_consts()

    gs = SSD_GROUPS_PER_STEP

    def body(x0_ref, x1_ref, dt0_ref, dt1_ref, bias_ref, alog_ref, tri_ref, trit_ref, y0_ref, y1_ref, sin_ref, state):
        gb, k = pl.program_id(0), pl.program_id(1)

        @pl.when(k == 0)
        def _():
            state[...] = jnp.zeros_like(state)

        for d, (x_ref, dt_ref, y_ref) in enumerate(((x0_ref, dt0_ref, y0_ref), (x1_ref, dt1_ref, y1_ref))):
            tri_v = tri_ref[d]
            dt_all, a_all, s_all = _ssd_dt(dt_ref[...], bias_ref[...], alog_ref[...], tri_v, trit_ref[d])
            for j in range(gs):
                o = j * gw
                sin_ref[d, j] = state[d, j]
                (y,), (s_out,) = _ssd_chunk(
                    [x_ref[:, o:o + xw]], x_ref[:, o + xw:o + xw + SSD_STATE], x_ref[:, o + xw + SSD_STATE:o + gw],
                    dt_all, a_all, s_all, [state[d, j]], tri_v > 0.5, d * nh + (gb * gs + j) * hpg)
                y_ref[:, j * xw:(j + 1) * xw] = y.astype(y_ref.dtype)
                state[d, j] = s_out

    ch = lambda d, k: _chunk_of(d, k, ncc, nc)
    y_shape = jax.ShapeDtypeStruct((T, G * xw), ACT_DTYPE)
    res, xres = hosted_call(
        body, ex, [xbc, xbc, dtraw, dtraw, dt_bias, a_log, tri, tri_t], name=name,
        out_shape=[y_shape, y_shape, jax.ShapeDtypeStruct((2, nc, G, SSD_STATE, xw), F32)],
        grid=(G // gs, nc),
        in_specs=[pl.BlockSpec((q, gs * gw), lambda g, k: (ch(0, k), g)),
                  pl.BlockSpec((q, gs * gw), lambda g, k: (ch(1, k), g)),
                  pl.BlockSpec((q, 128), lambda g, k: (ch(0, k), 0)),
                  pl.BlockSpec((q, 128), lambda g, k: (ch(1, k), 0)),
                  pl.BlockSpec((1, 128), lambda g, k: (0, 0)),
                  pl.BlockSpec((1, 128), lambda g, k: (0, 0)),
                  pl.BlockSpec((2, q, q), lambda g, k: (0, 0, 0)),
                  pl.BlockSpec((2, q, q), lambda g, k: (0, 0, 0))],
        out_specs=[pl.BlockSpec((q, gs * xw), lambda g, k: (ch(0, k), g)),
                   pl.BlockSpec((q, gs * xw), lambda g, k: (ch(1, k), g)),
                   pl.BlockSpec((2, None, gs, SSD_STATE, xw), lambda g, k: (0, k, g, 0, 0))],
        scratch_shapes=[pltpu.VMEM((2, gs, SSD_STATE, xw), F32)])
    return res[0], res[1], res[2], xres


def ssd_bwd(name, xbc, dtraw, dt_bias, a_log, states, dy, n_ctx, ex=None):
    T = xbc.shape[0]
    q, G = SSD_CHUNK, SSD_GROUPS
    nc, ncc = T // q, n_ctx // q
    gw = xbc.shape[1] // G
    xw = gw - 2 * SSD_STATE
    hpg = xw // SSD_HEADDIM
    nh = G * hpg
    tri, tri_t = _scan_consts()

    gs = SSD_GROUPS_PER_STEP

    def body(x0_ref, x1_ref, dt0_ref, dt1_ref, bias_ref, alog_ref, tri_ref, trit_ref, sin_ref, dy0_ref, dy1_ref,
             dx0_ref, dx1_ref, ddt_ref, dbias_ref, dalog_ref, dstate):
        gb, k = pl.program_id(0), pl.program_id(1)

        @pl.when((gb == 0) & (k == 0))
        def _():
            ddt_ref[...] = jnp.zeros_like(ddt_ref)
            dbias_ref[...] = jnp.zeros_like(dbias_ref)
            dalog_ref[...] = jnp.zeros_like(dalog_ref)

        @pl.when(k == 0)
        def _():
            dstate[...] = jnp.zeros_like(dstate)

        tris = [(tri_ref[d], trit_ref[d]) for d in range(2)]
        per = 4

        def fn(bias, alog, dtraw0, dtraw1, *per_group):
            ys, s_outs = [], []
            for d, dtraw in enumerate((dtraw0, dtraw1)):
                tri_v, trit_v = tris[d]
                dt_all, a_all, s_all = _ssd_dt(dtraw, bias, alog, tri_v, trit_v)
                for j in range(gs):
                    xs, bm, cm, s_in = per_group[per * (d * gs + j):per * (d * gs + j + 1)]
                    y, s_out = _ssd_chunk([xs], bm, cm, dt_all, a_all, s_all, [s_in], tri_v > 0.5,
                                          d * nh + (gb * gs + j) * hpg)
                    ys += y
                    s_outs += s_out
            return ys, s_outs

        per_group, dys, dss = [], [], []
        for d, (x_ref, dy_ref) in enumerate(((x0_ref, dy0_ref), (x1_ref, dy1_ref))):
            for j in range(gs):
                o = j * gw
                per_group += [x_ref[:, o:o + xw], x_ref[:, o + xw:o + xw + SSD_STATE], x_ref[:, o + xw + SSD_STATE:o + gw],
                              sin_ref[d, j]]
                dys.append(dy_ref[:, j * xw:(j + 1) * xw].astype(F32))
                dss.append(dstate[d, j])
        _, vjp = jax.vjp(fn, bias_ref[...], alog_ref[...], dt0_ref[...], dt1_ref[...], *per_group)
        cts = vjp((dys, dss))
        dbias, dalog, ddt0, ddt1 = cts[:4]
        for d, dx_ref in enumerate((dx0_ref, dx1_ref)):
            for j in range(gs):
                o = j * gw
                dxs, dbm, dcm, ds_in = cts[4 + per * (d * gs + j):4 + per * (d * gs + j + 1)]
                dx_ref[:, o:o + xw] = dxs.astype(dx_ref.dtype)
                dx_ref[:, o + xw:o + xw + SSD_STATE] = dbm.astype(dx_ref.dtype)
                dx_ref[:, o + xw + SSD_STATE:o + gw] = dcm.astype(dx_ref.dtype)
                dstate[d, j] = ds_in
        for d, ddt in enumerate((ddt0, ddt1)):
            row0 = pl.multiple_of(_chunk_of(d, nc - 1 - k, ncc, nc) * q, q)
            ddt_ref[pl.ds(row0, q), :] += ddt
        dbias_ref[...] += dbias
        dalog_ref[...] += dalog

    ch = lambda d, k: _chunk_of(d, nc - 1 - k, ncc, nc)
    dx_shape = jax.ShapeDtypeStruct((T, G * gw), ACT_DTYPE)
    res, xres = hosted_call(
        body, ex, [xbc, xbc, dtraw, dtraw, dt_bias, a_log, tri, tri_t, states, dy, dy], name=name,
        out_shape=[dx_shape, dx_shape, jax.ShapeDtypeStruct((T, 128), F32),
                   jax.ShapeDtypeStruct((1, 128), F32), jax.ShapeDtypeStruct((1, 128), F32)],
        grid=(G // gs, nc),
        in_specs=[pl.BlockSpec((q, gs * gw), lambda g, k: (ch(0, k), g)),
                  pl.BlockSpec((q, gs * gw), lambda g, k: (ch(1, k), g)),
                  pl.BlockSpec((q, 128), lambda g, k: (ch(0, k), 0)),
                  pl.BlockSpec((q, 128), lambda g, k: (ch(1, k), 0)),
                  pl.BlockSpec((1, 128), lambda g, k: (0, 0)),
                  pl.BlockSpec((1, 128), lambda g, k: (0, 0)),
                  pl.BlockSpec((2, q, q), lambda g, k: (0, 0, 0)),
                  pl.BlockSpec((2, q, q), lambda g, k: (0, 0, 0)),
                  pl.BlockSpec((2, None, gs, SSD_STATE, xw), lambda g, k: (0, nc - 1 - k, g, 0, 0)),
                  pl.BlockSpec((q, gs * xw), lambda g, k: (ch(0, k), g)),
                  pl.BlockSpec((q, gs * xw), lambda g, k: (ch(1, k), g))],
        out_specs=[pl.BlockSpec((q, gs * gw), lambda g, k: (ch(0, k), g)),
                   pl.BlockSpec((q, gs * gw), lambda g, k: (ch(1, k), g)),
                   pl.BlockSpec((T, 128), lambda g, k: (0, 0)),
                   pl.BlockSpec((1, 128), lambda g, k: (0, 0)),
                   pl.BlockSpec((1, 128), lambda g, k: (0, 0))],
        scratch_shapes=[pltpu.VMEM((2, gs, SSD_STATE, xw), F32)])
    return res[0], res[1], res[2], res[3], res[4], xres


def _perm_xbc(a):
    G = SSD_GROUPS
    n = a.shape[-1]
    gn = G * SSD_STATE
    di = n - 2 * gn
    lead = a.shape[:-1]
    xs = a[..., :di].reshape(lead + (G, di // G))
    bm = a[..., di:di + gn].reshape(lead + (G, SSD_STATE))
    cm = a[..., di + gn:].reshape(lead + (G, SSD_STATE))
    return jnp.concatenate([xs, bm, cm], axis=-1).reshape(lead + (n,))


def _unperm_xbc(a):
    G = SSD_GROUPS
    n = a.shape[-1]
    gn = G * SSD_STATE
    di = n - 2 * gn
    lead = a.shape[:-1]
    r = a.reshape(lead + (G, n // G))
    xw = di // G
    return jnp.concatenate([r[..., :xw].reshape(lead + (di,)), r[..., xw:xw + SSD_STATE].reshape(lead + (gn,)),
                            r[..., xw + SSD_STATE:].reshape(lead + (gn,))], axis=-1)


def _pool_consts(tm, n_ctx):
    assert n_ctx == tm and tm % GRID_W == 0
    mats, cnts = [], []
    for seq in (n_ctx, GRID_W):
        t = np.arange(tm)
        tt = t % seq
        base = t - tt
        ms, cs = [], []
        for k in POOL_WINDOWS:
            lo = np.clip(tt - k // 2, 0, seq) + base
            hi = np.clip(tt + k // 2, 0, seq) + base
            m = ((t[None, :] >= lo[:, None]) & (t[None, :] < hi[:, None])).astype(np.float32)
            ms.append(m)
            cs.append((1.0 / (hi - lo).astype(np.float32))[:, None])
        mats.append(np.stack(ms))
        cnts.append(np.stack(cs))
    m = np.stack(mats)
    return jnp.asarray(m), jnp.asarray(np.swapaxes(m, -1, -2)), jnp.asarray(np.stack(cnts).astype(np.float32))


def _prep_layer_weights(w_ada, b_ada, g_mix, w_in, conv_w, conv_b, dt_bias, a_log, d_skip, ssd_norm_w, w_ssd_out,
                        pool_w, pool_scale, w_pool_out, w_out, g_ffn, w_gate_up, w_down):
    D = w_in.shape[0]
    di = ssd_norm_w.shape[0]
    xbc = conv_w.shape[1]
    nh2 = dt_bias.size
    pw = pool_scale.shape[0]
    o = 0
    wz = w_in[:, o:o + di]; o += di
    wx = w_in[:, o:o + xbc]; o += xbc
    wdt = w_in[:, o:o + nh2]; o += nh2
    wp = w_in[:, o:o + pw]; o += pw
    wg = w_in[:, o:]
    w1 = jnp.concatenate([_perm_xbc(wx), wz, wg, wp, wdt, jnp.zeros((D, DT_PAD - nh2), w_in.dtype)], axis=1)
    pad128 = lambda v: jnp.concatenate([v.reshape(1, -1), jnp.zeros((1, 128 - v.size), F32)], axis=1)
    return dict(
        w_ada=w_ada, b_ada=b_ada.reshape(1, -1), g_mix=g_mix.reshape(1, -1), w1=w1,
        conv_w=_perm_xbc(conv_w), conv_b=_perm_xbc(conv_b.reshape(1, -1)),
        dt_bias=pad128(dt_bias), a_log=pad128(a_log),
        dskip=jnp.repeat(d_skip[0] + d_skip[1], SSD_HEADDIM).reshape(1, -1),
        ssd_norm_w=ssd_norm_w.reshape(1, -1), w_ssd_out=w_ssd_out, pool_w=pool_w,
        pool_scale=pool_scale.reshape(1, -1), w_pool_out=w_pool_out, w_out=w_out, g_ffn=g_ffn.reshape(1, -1),
        w_gate_up=w_gate_up, w_down=w_down)


def _unprep_layer_grads(g, dims):
    di, xbc, nh2, pw = dims
    dxbc, dz, dgs, dgp, dp, ddt = g["w1"]
    r = dxbc.reshape(SSD_GROUPS, xbc // SSD_GROUPS, dxbc.shape[1])
    xw = di // SSD_GROUPS
    parts = [r[:, :xw], r[:, xw:xw + SSD_STATE], r[:, xw + SSD_STATE:]]
    w_in_t = jnp.concatenate([dz] + [p.reshape(-1, dxbc.shape[1]) for p in parts] + [ddt[:nh2], dp, dgs, dgp], axis=0)
    nh = nh2 // 2
    dsk = g["dskip"].reshape(nh, SSD_HEADDIM).sum(axis=1)
    return dict(
        w_ada=g["w_ada"], b_ada=g["b_ada"].reshape(-1), g_mix=g["g_mix"].reshape(-1),
        w_in=w_in_t,
        conv_w=_unperm_xbc(g["conv_w"]), conv_b=_unperm_xbc(g["conv_b"]).reshape(-1),
        dt_bias=g["dt_bias"][0, :nh2].reshape(2, nh), a_log=g["a_log"][0, :nh2].reshape(2, nh),
        d_skip=jnp.stack([dsk, dsk]), ssd_norm_w=g["ssd_norm_w"].reshape(-1), w_ssd_out=g["w_ssd_out"],
        pool_w=g["pool_w"], pool_scale=g["pool_scale"].reshape(-1), w_pool_out=g["w_pool_out"], w_out=g["w_out"],
        g_ffn=g["g_ffn"].reshape(-1), w_gate_up=g["w_gate_up"], w_down=g["w_down"])


COND_ROWS = 16


def _split_mods(m):
    d = m.shape[1] // 6
    return [m[:2, k * d:(k + 1) * d].reshape(2, 1, d) for k in range(6)]


def _pool_args(rows, proj, col_block, width, pc, w):
    seg_const = lambda a: Arg(a, (None,) + a.shape[1:], lambda j, i, s: (s, 0, 0, 0), "const")
    pws = [Arg(w["pool_w"][k], w["pool_w"].shape[1:], lambda j, i, s: (0, 0), "acc") for k in range(w["pool_w"].shape[0])]
    return [rows.row(proj, width, col_block)] + [seg_const(a) for a in pc] + [rows.vec(w["pool_scale"])] + pws


TALL_ROW_TILE = 1088


def _tall_rows(T, ncol):
    tm = max(t for t in range(16, min(T, TALL_ROW_TILE) + 1, 16) if T % t == 0)
    return Rows(T // tm, 0, tm, ncol)


def _hosted(hosts, box, key):
    fn = (hosts or {}).get(key)
    return fn(box) if fn else None


def _layer_fwd(l, pre, cond_s, w, rows, n_ctx, pc, hosts=None, box=None):
    T, D = pre[0].shape if isinstance(pre, tuple) else pre.shape
    nt, nct, tm = rows.nt, rows.nct, rows.tm
    n = lambda s: f"l{l}_{s}"
    crow = Rows(1, 0, COND_ROWS)
    mraw = matmul_nn(n("ada_mm"), cond_s, w["w_ada"])
    (m,) = stage_fwd(n("ada_bias"), f_bias, crow, [crow.row(mraw, mraw.shape[1]), crow.vec(w["b_ada"])],
                     [(mraw.shape[1], F32, False)])
    sh1, sc1, ga1, sh2, sc2, ga2 = _split_mods(m)

    if isinstance(pre, tuple):
        x, h1 = stage_fwd(n("norm1"), f_resid_norm_mod, rows, _resid_norm_args(rows, pre, w["g_mix"], sh1, sc1, D),
                          [(D, F32, False), (D, ACT_DTYPE, False)])
    else:
        x = pre
        (h1,) = stage_fwd(n("norm1"), f_norm_mod, rows,
                          [rows.row(x, D), rows.vec(w["g_mix"]), rows.segvec(sh1), rows.segvec(sc1)],
                          [(D, ACT_DTYPE, False)])
    xbc_w = w["conv_w"].shape[1]
    di = w["ssd_norm_w"].shape[1]
    pw = w["pool_scale"].shape[1]
    c_z, c_g, c_p, c_dt = xbc_w, xbc_w + di, xbc_w + di + 2 * pw, xbc_w + di + 3 * pw
    ex = _hosted(hosts, box, "in_mm")
    proj = matmul_nn(n("in_mm"), h1, w["w1"], out_dtype=ACT_DTYPE, ex=ex, ncols=c_dt)
    if ex is not None:
        proj, box["in_mm"] = proj
    dtraw = matmul_nn(n("in_dt_mm"), h1, w["w1"], col0=c_dt, ncols=128)
    ex = _hosted(hosts, box, "conv")
    xbc, xres = conv_fwd(n("conv"), proj, w["conv_w"], w["conv_b"], n_ctx, xbc_w, ex)
    if ex is not None:
        box["conv"] = xres
    ex = _hosted(hosts, box, "ssd")
    y0, y1, states, xres = ssd_fwd(n("ssd"), xbc, dtraw, w["dt_bias"], w["a_log"], n_ctx, ex)
    y2 = (y0, y1)
    if ex is not None:
        box["ssd"] = xres

    G = SSD_GROUPS
    gw = di // G
    r8 = _tall_rows(T, G)
    gate_args = [r8.row(y2[0], gw, 0, True), r8.row(y2[1], gw, 0, True), r8.row(xbc, gw, 0, True, stride=2),
                 r8.row(proj, gw, c_z // gw, True), r8.vec(w["dskip"], True), r8.vec(w["ssd_norm_w"], True)]
    (ynw,) = stage_fwd(n("ssd_gate"), f_ssd_gate, r8, gate_args, [(gw, ACT_DTYPE, True)])
    o_ssd = matmul_nn(n("ssd_out_mm"), ynw, w["w_ssd_out"])

    nw = len(POOL_WINDOWS)
    pg = pw // nw
    (ps,) = stage_fwd(n("pool"), f_pool_all, rows, _pool_args(rows, proj, c_p // pw, pw, pc, w), [(pw, ACT_DTYPE, False)])
    o_pool = matmul_nn(n("pool_out_mm"), ps, w["w_pool_out"])

    merge_args = [rows.row(o_ssd, D), rows.row(o_pool, D), rows.row(proj, pw, c_g // pw), rows.row(proj, pw, c_g // pw + 1)]
    (mg,) = stage_fwd(n("merge"), f_merge, rows, merge_args, [(D, ACT_DTYPE, False)])
    mo = matmul_nn(n("out_mm"), mg, w["w_out"])

    rn_args = [rows.row(x, D), rows.row(mo, D), rows.segvec(ga1), rows.vec(w["g_ffn"]), rows.segvec(sh2), rows.segvec(sc2)]
    x1, h2 = stage_fwd(n("norm2"), f_resid_norm_mod, rows, rn_args, [(D, F32, False), (D, ACT_DTYPE, False)])
    ex = _hosted(hosts, box, "gate_up_mm")
    gu = matmul_nn(n("gate_up_mm"), h2, w["w_gate_up"], ex=ex)
    if ex is not None:
        gu, box["gate_up_mm"] = gu
    fh = gu.shape[1] // 2
    (act,) = stage_fwd(n("swiglu"), f_swiglu, rows, [rows.row(gu, 2 * fh)], [(fh, ACT_DTYPE, False)])
    dn = matmul_nn(n("down_mm"), act, w["w_down"])
    saved = dict(x=x, pre=pre, mraw=mraw, mods=(sh1, sc1, ga1, sh2, sc2, ga2), h1=h1, proj=proj, dtraw=dtraw, xbc=xbc, y2=y2,
                 states=states,
                 ynw=ynw, o_ssd=o_ssd, ps=ps, o_pool=o_pool, mg=mg, mo=mo, x1=x1, h2=h2, gu=gu, act=act, dn=dn,
                 cols=(c_z, c_g, c_p, c_dt))
    return (x1, dn, ga2), saved


def _resid_norm_args(rows, pre, g, sh, sc, D):
    x1, dn, ga2 = pre
    return [rows.row(x1, D), rows.row(dn, D), rows.segvec(ga2), rows.vec(g), rows.segvec(sh), rows.segvec(sc)]


def f_norm_mod_keep(x, g, sh, sc):
    return f_norm_mod(x, g, sh, sc)[0], x


def _layer_bwd(l, cot, cond_s, w, s, rows, n_ctx, pc, hosts=None, box=None):
    dx1, ddn, dga2 = cot
    T, D = dx1.shape
    nt, nct, tm = rows.nt, rows.nct, rows.tm
    n = lambda t: f"l{l}_{t}_bwd"
    sh1, sc1, ga1, sh2, sc2, ga2 = s["mods"]
    c_z, c_g, c_p, c_dt = s["cols"]
    x, proj, xbc, y2, gu = s["x"], s["proj"], s["xbc"], s["y2"], s["gu"]
    g = {}
    if box is not None:
        box["g"] = g

    ex = _hosted(hosts, box, "down_dx")
    dact = matmul_nt(n("down_dx"), ddn, w["w_down"], ex=ex)
    if ex is not None:
        dact, box["down_dx"] = dact
    g["w_down"] = matmul_tn(n("down_dw"), s["act"], ddn)
    fh = gu.shape[1] // 2
    (dgu,) = stage_bwd(n("swiglu"), f_swiglu, rows, [rows.row(gu, 2 * fh)], [rows.row(dact, fh)], [ACT_DTYPE])
    dh2 = matmul_nt(n("gate_up_dx"), dgu, w["w_gate_up"])
    g["w_gate_up"] = matmul_tn(n("gate_up_dw"), s["h2"], dgu, blocks=w["w_gate_up"].shape[0])

    rn_args = [rows.row(x, D), rows.row(s["mo"], D), rows.segvec(ga1), rows.vec(w["g_ffn"]), rows.segvec(sh2), rows.segvec(sc2)]
    dxr, dmo, dga1, g["g_ffn"], dsh2, dsc2 = stage_bwd(
        n("norm2"), f_resid_norm_mod, rows, rn_args, [rows.row(dx1, D), rows.row(dh2, D)], [F32, ACT_DTYPE])
    dmg = matmul_nt(n("out_dx"), dmo, w["w_out"])
    g["w_out"] = matmul_tn(n("out_dw"), s["mg"], dmo)

    pw = w["pool_scale"].shape[1]
    merge_args = [rows.row(s["o_ssd"], D), rows.row(s["o_pool"], D), rows.row(proj, pw, c_g // pw), rows.row(proj, pw, c_g // pw + 1)]
    do_ssd, do_pool, dgl_s, dgl_p = stage_bwd(n("merge"), f_merge, rows, merge_args, [rows.row(dmg, D)], [ACT_DTYPE] * 4)
    dps = matmul_nt(n("pool_out_dx"), do_pool, w["w_pool_out"])
    g["w_pool_out"] = matmul_tn(n("pool_out_dw"), s["ps"], do_pool)

    nw = len(POOL_WINDOWS)
    pg = pw // nw
    du_pool, g["pool_scale"], *dpw = stage_bwd(n("pool"), f_pool_all, rows, _pool_args(rows, proj, c_p // pw, pw, pc, w),
                                               [rows.row(dps, pw)], [ACT_DTYPE])
    g["pool_w"] = jnp.stack(dpw)

    dynw = matmul_nt(n("ssd_out_dx"), do_ssd, w["w_ssd_out"])
    g["w_ssd_out"] = matmul_tn(n("ssd_out_dw"), s["ynw"], do_ssd)
    G = SSD_GROUPS
    di = w["ssd_norm_w"].shape[1]
    gw = di // G
    r8 = _tall_rows(T, G)
    gate_args = [r8.row(y2[0], gw, 0, True), r8.row(y2[1], gw, 0, True), r8.row(xbc, gw, 0, True, stride=2),
                 r8.row(proj, gw, c_z // gw, True), r8.vec(w["dskip"], True), r8.vec(w["ssd_norm_w"], True)]
    gate_args[1].kind = "const"
    ex = _hosted(hosts, box, "ssd_gate")
    res = stage_bwd(n("ssd_gate"), f_ssd_gate, r8, gate_args, [r8.row(dynw, gw, 0, True)], [ACT_DTYPE] * 3, ex)
    if ex is not None:
        res, box["ssd_gate"] = res
    dy, dxs_skip, dz, g["dskip"], g["ssd_norm_w"] = res

    ex = _hosted(hosts, box, "ssd")
    dxbc0, dxbc1, ddt, g["dt_bias"], g["a_log"], xres = ssd_bwd(n("ssd"), xbc, s["dtraw"], w["dt_bias"], w["a_log"],
                                                                s["states"], dy, n_ctx, ex)
    dxbc2 = (dxbc0, dxbc1)
    if ex is not None:
        box["ssd"] = xres
    xbc_w = xbc.shape[1]
    ex = _hosted(hosts, box, "conv")
    dxbc_raw, g["conv_w"], g["conv_b"], xres = conv_bwd(n("conv"), proj, w["conv_w"], w["conv_b"], dxbc2, dxs_skip,
                                                         n_ctx, xbc_w, ex)
    if ex is not None:
        box["conv"] = xres
    pieces = [dxbc_raw, dz, dgl_s, dgl_p, du_pool, ddt]
    offsets = [0, c_z, c_g, c_g + pw, c_p, c_dt]
    ex = _hosted(hosts, box, "in_dx")
    dh1 = matmul_nt(n("in_dx"), pieces, w["w1"], ex=ex, offsets=offsets)
    if ex is not None:
        dh1, box["in_dx"] = dh1
    ex = _hosted(hosts, box, "in_dw")
    first = matmul_tn(n("in_dw0"), pieces[0], s["h1"], ex=ex)
    if ex is not None:
        first, box["in_dw"] = first
    g["w1"] = [first] + [matmul_tn(n(f"in_dw{k}"), p, s["h1"]) for k, p in enumerate(pieces) if k]

    if isinstance(s["pre"], tuple):
        dx1p, ddnp, dga2p, g["g_mix"], dsh1, dsc1 = stage_bwd(
            n("norm1"), f_resid_norm_mod, rows, _resid_norm_args(rows, s["pre"], w["g_mix"], sh1, sc1, D),
            [rows.row(dxr, D), rows.row(dh1, D)], [F32, ACT_DTYPE])
        dx = (dx1p, ddnp, dga2p)
    else:
        n1_args = [rows.row(x, D), rows.vec(w["g_mix"]), rows.segvec(sh1), rows.segvec(sc1)]
        dx, g["g_mix"], dsh1, dsc1 = stage_bwd(n("norm1"), f_norm_mod_keep, rows, n1_args,
                                               [rows.row(dh1, D), rows.row(dxr, D)], [F32])

    dm = jnp.concatenate([v.reshape(2, D) for v in (dsh1, dsc1, dga1, dsh2, dsc2, dga2)], axis=1)
    dm = jnp.concatenate([dm, jnp.zeros((COND_ROWS - 2, dm.shape[1]), F32)], axis=0)
    crow = Rows(1, 0, COND_ROWS)
    dmraw, g["b_ada"] = stage_bwd(n("ada_bias"), f_bias, crow, [crow.row(s["mraw"], dm.shape[1]), crow.vec(w["b_ada"])],
                                  [crow.row(dm, dm.shape[1])], [ACT_DTYPE])
    dcs = matmul_nt(n("ada_dx"), dmraw, w["w_ada"])
    g["w_ada"] = matmul_tn(n("ada_dw"), cond_s, dmraw, blocks=w["w_ada"].shape[0])
    return dx, dcs, g


def local_step(x, ctx, c, c_ctx, target, layer_w_fn, n_layers, g_final, fwd_hosts=None, bwd_hosts=None):
    L, D = x.shape
    n_ctx = ctx.shape[0]
    tm = ROW_TILE
    T = L + n_ctx
    rows = Rows(T // tm, n_ctx // tm, tm)
    pc = _pool_consts(tm, n_ctx)
    xa = jnp.concatenate([ctx, x], axis=0)
    cond = jnp.concatenate([c_ctx.reshape(1, D), c.reshape(1, D), jnp.zeros((COND_ROWS - 2, D), F32)], axis=0)
    crow = Rows(1, 0, COND_ROWS)
    (cond_s,) = stage_fwd("cond_silu", f_silu, crow, [crow.row(cond, D)], [(D, ACT_DTYPE, False)])

    saved, layer_w = [], []
    for l in range(n_layers):
        layer_w.append(layer_w_fn(l))
        box = {}
        xa, s = _layer_fwd(l, xa, cond_s, layer_w[l], rows, n_ctx, pc, fwd_hosts(l, box) if fwd_hosts else None, box)
        saved.append(s)

    x1, dn, ga2 = xa
    rl = Rows(L // tm, 0, tm)
    gf = g_final.reshape(1, D)
    tgt = rl.row(target, D)
    tgt.kind = "const"
    off = n_ctx // tm
    loss_args = [rl.row(x1, D, roff=off), rl.row(dn, D, roff=off), rl.vec(ga2[1]), tgt, rl.vec(gf)]
    ones = jnp.ones((L, 1), F32)
    dx1_lat, ddn_lat, dga2_lat, dgf, loss_rows = stage_bwd("loss", f_loss_resid, rl, loss_args, [rl.row(ones, 1)],
                                                           [F32, ACT_DTYPE], primal=[(1, F32)])
    loss = jnp.sum(loss_rows)
    cot = (jnp.concatenate([jnp.zeros((n_ctx, D), F32), dx1_lat], axis=0),
           jnp.concatenate([jnp.zeros((n_ctx, D), ACT_DTYPE), ddn_lat], axis=0),
           jnp.stack([jnp.zeros((1, D), F32), dga2_lat]))

    grads = [None] * n_layers
    dcs = jnp.zeros((COND_ROWS, D), F32)
    for l in reversed(range(n_layers)):
        box = {}
        hosts = bwd_hosts(l, grads, box) if bwd_hosts else None
        cot, dcs_l, grads[l] = _layer_bwd(l, cot, cond_s, layer_w[l], saved[l], rows, n_ctx, pc, hosts, box)
        dcs = dcs + dcs_l
    dx = cot
    (dcond,) = stage_bwd("cond_silu_bwd", f_silu, crow, [crow.row(cond, D)], [crow.row(dcs, D)], [F32])
    return loss, dx[n_ctx:], grads, dcond[0], dgf


def gather_chips(halves, conv=None):
    n = len(halves)
    ops = list(halves) + ([conv] if conv is not None else [])

    def copies(ins, outs, pos):
        c, me = pos[2], _chip_index(pos)
        pairs = [(s.at[c], o.at[me, c]) for s, o in zip(ins[:n], outs[:n])]
        pairs += [(s, o.at[me]) for s, o in zip(ins[n:], outs[n:])]
        return pairs, [(s, d, _flip(pos, rel)) for rel in PLANE for s, d in pairs]

    shapes = [jax.ShapeDtypeStruct((4,) + s.shape, s.dtype) for s in ops]
    return Exchange(copies, 3 * len(ops), len(ops), ops, shapes)


def gather_pair(gathered):
    n = len(gathered)

    def copies(ins, outs, pos):
        c = pos[2]
        return [], [(s.at[b, c], o.at[b, c], _flip(pos, PAIR[0])) for s, o in zip(ins, outs) for b in range(4)]

    shapes = [jax.ShapeDtypeStruct(g.shape, g.dtype) for g in gathered]
    return Exchange(copies, 4 * n, 0, gathered, shapes, aliases={k: k for k in range(n)})


def swap_halves(grads):
    n = len(grads)

    def copies(ins, outs, pos):
        c = pos[2]
        return [], [(g.at[b, 1 - c], o.at[b], _flip(pos, PAIR[0])) for g, o in zip(ins, outs) for b in range(4)]

    shapes = [jax.ShapeDtypeStruct((g.shape[0],) + g.shape[2:], g.dtype) for g in grads]
    return Exchange(copies, 4 * n, 0, grads, shapes)


def scatter_chips(sums):
    n = len(sums)

    def copies(ins, outs, pos):
        me = _chip_index(pos)
        local = [(p.at[me], o.at[me]) for p, o in zip(ins, outs)]
        remote = []
        for rel in PLANE:
            peer = _flip(pos, rel)
            remote += [(p.at[_chip_index(peer)], o.at[me], peer) for p, o in zip(ins, outs)]
        return local, remote

    shapes = [jax.ShapeDtypeStruct(p.shape, p.dtype) for p in sums]
    return Exchange(copies, 3 * n, n, sums, shapes)


def share_halves(finals):
    n = len(finals)

    def copies(ins, outs, pos):
        c = pos[2]
        return [], [(f.at[c], o.at[c], _flip(pos, PAIR[0])) for f, o in zip(ins, outs)]

    shapes = [jax.ShapeDtypeStruct(f.shape, f.dtype) for f in finals]
    return Exchange(copies, n, 0, finals, shapes, aliases={k: k for k in range(n)})


def gather_everyone(vec):
    def copies(ins, outs, pos):
        me = _device_index(pos)
        (v,), (o,) = ins, outs
        return [(v, o.at[me])], [(v, o.at[me], _flip(pos, rel)) for rel in EVERYONE]

    return Exchange(copies, len(EVERYONE), 1, [vec], [jax.ShapeDtypeStruct((8,) + vec.shape, vec.dtype)])


def _row_tile(rows, cols, n_bufs, mult=8):
    cap = VMEM_LIMIT_BYTES // 2 // (2 * n_bufs * cols * 4)
    for t in range(min(rows, cap) // mult * mult, 0, -mult):
        if rows % t == 0:
            return t
    return rows


def _adamw_update(w, g, m, v):
    nm = ADAM_B1 * m + (1.0 - ADAM_B1) * g
    nv = ADAM_B2 * v + (1.0 - ADAM_B2) * jnp.square(g)
    m_hat = nm / (1.0 - ADAM_B1 ** ADAM_STEP)
    v_hat = nv / (1.0 - ADAM_B2 ** ADAM_STEP)
    return -ADAM_LR * (m_hat / (jnp.sqrt(v_hat) + ADAM_EPS) + ADAM_WD * w), nm, nv


def adamw_small(name, ws, gs, ms, vs):
    n = len(ws)

    def body(*refs):
        ins, outs = refs[:4 * n], refs[4 * n:]
        for k in range(n):
            d, nm, nv = _adamw_update(ins[k][...], ins[n + k][...], ins[2 * n + k][...], ins[3 * n + k][...])
            outs[k][...] = d
            outs[n + k][...] = nm
            outs[2 * n + k][...] = nv

    shapes = [jax.ShapeDtypeStruct(a.shape, F32) for a in ws]
    vmem = pl.BlockSpec(memory_space=pltpu.VMEM)
    res = _pcall(body, name=name, out_shape=shapes * 3, in_specs=[vmem] * (4 * n), out_specs=[vmem] * (3 * n),
                 compiler_params=pltpu.CompilerParams(vmem_limit_bytes=VMEM_LIMIT_BYTES))(*ws, *gs, *ms, *vs)
    return res[:n], res[n:2 * n], res[2 * n:]


WIRE_DTYPE = jnp.bfloat16


def add_own_half(name, grads, recv, c):
    nb, _, R, C = grads.shape
    tr = _row_tile(R, C, 3, mult=16)

    def body(c_ref, g_ref, r_ref, o_ref):
        o_ref[...] = (g_ref[...] + r_ref[...]).astype(o_ref.dtype)

    spec = pl.BlockSpec((None, tr, C), lambda b, i, c_ref: (b, i, 0))
    return _pcall(
        body, name=name, out_shape=jax.ShapeDtypeStruct(recv.shape, WIRE_DTYPE),
        grid_spec=pltpu.PrefetchScalarGridSpec(
            num_scalar_prefetch=1, grid=(nb, R // tr),
            in_specs=[pl.BlockSpec((None, None, tr, C), lambda b, i, c_ref: (b, c_ref[0], i, 0)), spec],
            out_specs=spec),
        compiler_params=_params("parallel", "parallel"),
    )(c, grads, recv)


def sum_slots(name, a, c=None):
    n, R, C = a.shape
    tr = _row_tile(R, C, n + 1, mult=16 if a.dtype.itemsize == 2 else 8)

    def body(*refs):
        a_ref, o_ref = refs[-2:]
        acc = a_ref[0].astype(F32)
        for k in range(1, n):
            acc = acc + a_ref[k].astype(F32)
        o_ref[...] = acc

    if c is None:
        return _pcall(
            body, name=name, out_shape=jax.ShapeDtypeStruct((R, C), F32), grid=(R // tr,),
            in_specs=[pl.BlockSpec((n, tr, C), lambda i: (0, i, 0))], out_specs=pl.BlockSpec((tr, C), lambda i: (i, 0)),
            compiler_params=_params("parallel"),
        )(a)
    return _pcall(
        body, name=name, out_shape=jax.ShapeDtypeStruct((2, R, C), F32),
        grid_spec=pltpu.PrefetchScalarGridSpec(
            num_scalar_prefetch=1, grid=(R // tr,),
            in_specs=[pl.BlockSpec((n, tr, C), lambda i, c_ref: (0, i, 0))],
            out_specs=pl.BlockSpec((None, tr, C), lambda i, c_ref: (c_ref[0], i, 0))),
        compiler_params=_params("parallel"),
    )(c, a)


def adamw(name, w, g_layers, m, v):
    nl, R, C = w.shape
    assert len(g_layers) == nl
    tr = _row_tile(R, C, 8 + nl)
    nr = R // tr

    def body(*refs):
        w_ref, m_ref, v_ref = refs[:3]
        g_refs = refs[3:3 + nl]
        go_ref, d_ref, nm_ref, nv_ref = refs[3 + nl:]
        l = pl.program_id(0)
        gr = g_refs[0][...]
        for k in range(1, nl):
            gr = jnp.where(l == k, g_refs[k][...], gr)
        d_ref[...], nm_ref[...], nv_ref[...] = _adamw_update(w_ref[...], gr, m_ref[...], v_ref[...])
        go_ref[...] = gr

    spec = pl.BlockSpec((None, tr, C), lambda l, i: (l, i, 0))
    g_specs = [pl.BlockSpec((tr, C), (lambda l, i, k=k: (jnp.where(l == k, i, jnp.where(l < k, 0, nr - 1)), 0)))
               for k in range(nl)]
    return _pcall(
        body, name=name, out_shape=[jax.ShapeDtypeStruct((nl, R, C), F32)] * 4, grid=(nl, nr),
        in_specs=[spec] * 3 + g_specs, out_specs=[spec] * 4, compiler_params=_params("arbitrary", "arbitrary"),
    )(w, m, v, *g_layers)


BIG = ("w_ada", "w_in", "w_ssd_out", "pool_w", "w_pool_out", "w_out", "w_gate_up", "w_down")
COL_SHARDED = ("w_ada", "w_in", "w_gate_up")
BLOCK_LAYOUT = ("w_ada", "w_gate_up")
GRAD_TRANSPOSED = ("w_in",)
FIRST_USED = ("w_ada", "w_in")
LATER_USED = tuple(k for k in BIG if k not in FIRST_USED)
READY_LAST = FIRST_USED
READY_EARLY = LATER_USED
SMALL = ("c_ctx", "b_ada", "g_mix", "conv_w", "conv_b", "dt_bias", "a_log", "d_skip", "ssd_norm_w", "pool_scale",
         "g_ffn", "g_final")
WEIGHTS = ("c_ctx", "w_ada", "b_ada", "g_mix", "w_in", "conv_w", "conv_b", "dt_bias", "a_log", "d_skip", "ssd_norm_w",
           "w_ssd_out", "pool_w", "pool_scale", "w_pool_out", "w_out", "g_ffn", "w_gate_up", "w_down", "g_final")
LAYER_KEYS = ("w_ada", "b_ada", "g_mix", "w_in", "conv_w", "conv_b", "dt_bias", "a_log", "d_skip", "ssd_norm_w",
              "w_ssd_out", "pool_w", "pool_scale", "w_pool_out", "w_out", "g_ffn", "w_gate_up", "w_down")


def _shard2d(name, a):
    if name == "pool_w":
        return a.reshape(a.shape[0], a.shape[1] * a.shape[2], a.shape[3])
    return a


def _full_from_blocks(name, a):
    nb, R, C = a.shape
    if name in BLOCK_LAYOUT:
        return a
    if name in COL_SHARDED:
        return jnp.transpose(a, (1, 0, 2)).reshape(R, nb * C)
    if name == "pool_w":
        nw = len(POOL_WINDOWS)
        return jnp.transpose(a.reshape(nb, nw, R // nw, C), (1, 0, 2, 3)).reshape(nw, nb * R // nw, C)
    return a.reshape(nb * R, C)


def _blocks_from_full(name, g):
    nb = 4
    if name in BLOCK_LAYOUT:
        return g
    if name in COL_SHARDED and name not in GRAD_TRANSPOSED:
        K, N = g.shape
        return jnp.transpose(g.reshape(K, nb, N // nb), (1, 0, 2))
    if name == "pool_w":
        nw, r, C = g.shape
        return jnp.transpose(g.reshape(nw, nb, r // nb, C), (1, 0, 2, 3)).reshape(nb, nw * r // nb, C)
    return g.reshape(nb, g.shape[0] // nb, g.shape[1])


def _pack(arrs, rows):
    flat = jnp.concatenate([a.reshape(-1).astype(F32) for a in arrs])
    return jnp.concatenate([flat, jnp.zeros((rows * 128 - flat.size,), F32)]).reshape(rows, 128)


def _unpack(vec, shapes):
    flat = vec.reshape(-1)
    out, o = [], 0
    for s in shapes:
        n = int(np.prod(s))
        out.append(flat[o:o + n].reshape(s))
        o += n
    return out


def _rows_for(shapes):
    n = sum(int(np.prod(s)) for s in shapes)
    return -(-n // (8 * 128)) * 8


def kernel(x, c, ctx, c_ctx, w_ada, b_ada, g_mix, w_in, conv_w, conv_b, dt_bias, a_log, d_skip, ssd_norm_w, w_ssd_out, pool_w, pool_scale, w_pool_out, w_out, g_ffn, w_gate_up, w_down, g_final, loss_target, m_c_ctx, m_w_ada, m_b_ada, m_g_mix, m_w_in, m_conv_w, m_conv_b, m_dt_bias, m_a_log, m_d_skip, m_ssd_norm_w, m_w_ssd_out, m_pool_w, m_pool_scale, m_w_pool_out, m_w_out, m_g_ffn, m_w_gate_up, m_w_down, m_g_final, v_c_ctx, v_w_ada, v_b_ada, v_g_mix, v_w_in, v_conv_w, v_conv_b, v_dt_bias, v_a_log, v_d_skip, v_ssd_norm_w, v_w_ssd_out, v_pool_w, v_pool_scale, v_w_pool_out, v_w_out, v_g_ffn, v_w_gate_up, v_w_down, v_g_final):
    w = dict(c_ctx=c_ctx, w_ada=w_ada, b_ada=b_ada, g_mix=g_mix, w_in=w_in, conv_w=conv_w, conv_b=conv_b, dt_bias=dt_bias,
             a_log=a_log, d_skip=d_skip, ssd_norm_w=ssd_norm_w, w_ssd_out=w_ssd_out, pool_w=pool_w, pool_scale=pool_scale,
             w_pool_out=w_pool_out, w_out=w_out, g_ffn=g_ffn, w_gate_up=w_gate_up, w_down=w_down, g_final=g_final)
    m = dict(c_ctx=m_c_ctx, w_ada=m_w_ada, b_ada=m_b_ada, g_mix=m_g_mix, w_in=m_w_in, conv_w=m_conv_w, conv_b=m_conv_b,
             dt_bias=m_dt_bias, a_log=m_a_log, d_skip=m_d_skip, ssd_norm_w=m_ssd_norm_w, w_ssd_out=m_w_ssd_out,
             pool_w=m_pool_w, pool_scale=m_pool_scale, w_pool_out=m_w_pool_out, w_out=m_w_out, g_ffn=m_g_ffn,
             w_gate_up=m_w_gate_up, w_down=m_w_down, g_final=m_g_final)
    v = dict(c_ctx=v_c_ctx, w_ada=v_w_ada, b_ada=v_b_ada, g_mix=v_g_mix, w_in=v_w_in, conv_w=v_conv_w, conv_b=v_conv_b,
             dt_bias=v_dt_bias, a_log=v_a_log, d_skip=v_d_skip, ssd_norm_w=v_ssd_norm_w, w_ssd_out=v_w_ssd_out,
             pool_w=v_pool_w, pool_scale=v_pool_scale, w_pool_out=v_w_pool_out, w_out=v_w_out, g_ffn=v_g_ffn,
             w_gate_up=v_w_gate_up, w_down=v_w_down, g_final=v_g_final)
    assert x.shape[0] == 1, "one example per device"
    pos = _position()
    core = pos[2].astype(jnp.int32).reshape(1)
    n_layers = w_in.shape[0]
    assert n_layers == 2
    dims = (ssd_norm_w.shape[1], conv_w.shape[2] * 4, dt_bias[0].size, pool_scale.shape[1])
    shard = {k: _shard2d(k, w[k]) for k in BIG}

    def halves(a):
        return a.reshape(a.shape[:-2] + (2, a.shape[-2] // 2, a.shape[-1]))

    def whole(a):
        return a.reshape(a.shape[:-3] + (2 * a.shape[-2], a.shape[-1]))

    def wire_shards(l, names):
        return [halves(shard[k][l].astype(MXU_DTYPE)) for k in names]

    def full_weights(names, gathered):
        return {k: _full_from_blocks(k, whole(a)) for k, a in zip(names, gathered)}

    first = comm_call("gather0_chips", gather_chips(wire_shards(0, FIRST_USED), conv=conv_w))
    got0 = full_weights(FIRST_USED, comm_call("gather0_pair", gather_pair(first[:-1])))
    conv_all = first[-1]
    conv_full = [jnp.transpose(conv_all[:, l], (1, 0, 2)).reshape(conv_all.shape[2], -1) for l in range(n_layers)]

    boxes = {}

    def layer_w_fn(l):
        if l == 0:
            full = dict(got0)
            late = {k: None for k in LATER_USED}
        else:
            full = full_weights(BIG, boxes[("fwd", 0)]["gate_up_mm"])
            late = {}
        full["conv_w"] = conv_full[l]
        lw = LazyDict(_prep_layer_weights(*[full[k] if k in full else (None if k in late else w[k][l]) for k in LAYER_KEYS]))
        for i, k in enumerate(late):
            lw[k] = (lambda i=i, k=k: _full_from_blocks(k, whole(boxes[("fwd", 0)]["conv"][i])))
        return lw

    def fwd_hosts(l, box):
        boxes[("fwd", l)] = box
        if l != 0:
            return None
        return {"in_mm": lambda box: gather_chips(wire_shards(0, LATER_USED)), "conv": lambda box: gather_pair(box["in_mm"]),
                "ssd": lambda box: gather_chips(wire_shards(1, BIG)), "gate_up_mm": lambda box: gather_pair(box["ssd"])}

    def blocks(gl, names):
        return [halves(_blocks_from_full(k, gl[k])) for k in names]

    def pair_sums(tag, names, G, recv):
        return [add_own_half(f"pair_sum{tag}_{k}", g, r, core) for k, g, r in zip(names, G, recv)]

    def chip_sums(tag, names, parts):
        return [sum_slots(f"chip_sum{tag}_{k}", p, core) for k, p in zip(names, parts)]

    def reduce_now(tag, gl, names):
        G = blocks(gl, names)
        pair = pair_sums(tag, names, G, comm_call(f"swap{tag}", swap_halves(G)))
        fin = chip_sums(tag, names, comm_call(f"scatter{tag}", scatter_chips(pair)))
        return [whole(a) for a in comm_call(f"share{tag}", share_halves(fin))]

    small_layers = {}
    n_big = len(BIG)

    def bwd_hosts(l, grads, box):
        boxes[("bwd", l)] = box
        if l != 0:
            return None
        gl1 = _unprep_layer_grads(grads[1], dims)
        small_layers[1] = gl1
        G1 = blocks(gl1, BIG)
        early = {}

        def gate_host(box):
            early["G"] = blocks(box["g"], READY_EARLY)
            return swap_halves(early["G"])

        def scan_host(box):
            return combine(scatter_chips(pair_sums("1", BIG, G1, box["down_dx"])),
                           scatter_chips(pair_sums("0e", READY_EARLY, early["G"], box["ssd_gate"])))

        def conv_host(box):
            return combine(share_halves(chip_sums("1", BIG, box["ssd"][:n_big])),
                           share_halves(chip_sums("0e", READY_EARLY, box["ssd"][n_big:])))

        return {"down_dx": lambda box: swap_halves(G1), "ssd_gate": gate_host, "ssd": scan_host, "in_dx": conv_host}

    loss, grad_x, grads, d_c_ctx, d_g_final = local_step(
        x[0], ctx[0], c[0], c_ctx, loss_target[0], layer_w_fn, n_layers, g_final, fwd_hosts, bwd_hosts)
    shared =[whole(a) for a in boxes[("bwd", 0)]["in_dx"]]
    reduced1 = shared[:n_big]
    gl0 = _unprep_layer_grads(grads[0], dims)
    small_layers[0] = gl0
    red0 = dict(zip(READY_EARLY, shared[n_big:]))
    red0.update(zip(READY_LAST, reduce_now("0", gl0, READY_LAST)))
    reduced0 = [red0[k] for k in BIG]

    small_full = dict(c_ctx=d_c_ctx, g_final=d_g_final.reshape(-1))
    for k in SMALL:
        if k not in small_full:
            small_full[k] = jnp.stack([small_layers[l][k] for l in range(n_layers)])
    shapes = [small_full[k].shape for k in SMALL] + [(1,)]
    packed = _pack([small_full[k] for k in SMALL] + [loss.reshape(1)], _rows_for(shapes))
    total = sum_slots("small_sum", comm_call("gather_small", gather_everyone(packed))[0])
    *small_vals, loss = _unpack(total, shapes)
    loss = loss.reshape(())
    small_g = dict(zip(SMALL, small_vals))
    cw = conv_w.shape[2]
    small_g["conv_w"] = lax.dynamic_slice_in_dim(small_g["conv_w"], _chip_index(pos) * cw, cw, axis=2)

    grad, delta, new_m, new_v = {}, {}, {}, {}
    for k, g0, g1 in zip(BIG, reduced0, reduced1):
        shp = w[k].shape
        if k in GRAD_TRANSPOSED:
            flat = lambda a: jnp.swapaxes(a, 1, 2)
            back = lambda a: jnp.swapaxes(a, 1, 2)
        else:
            flat = lambda a: _shard2d(k, a)
            back = lambda a: a.reshape(shp)
        outs = adamw(f"adamw_{k}", flat(w[k]), [g0, g1], flat(m[k]), flat(v[k]))
        grad[k], delta[k], new_m[k], new_v[k] = [back(a) for a in outs]
    flat2 = lambda d: [d[k].reshape(-1, d[k].shape[-1]) for k in SMALL]
    d_, m_, v_ = adamw_small("adamw_small", flat2(w), flat2(small_g), flat2(m), flat2(v))
    for k, dd, mm, vv in zip(SMALL, d_, m_, v_):
        shp = w[k].shape
        grad[k], delta[k], new_m[k], new_v[k] = small_g[k], dd.reshape(shp), mm.reshape(shp), vv.reshape(shp)

    return (loss, grad_x[None], *[grad[k] for k in WEIGHTS], *[delta[k] for k in WEIGHTS],
            *[new_m[k] for k in WEIGHTS], *[new_v[k] for k in WEIGHTS])
```

```python
import functools

import jax
import jax.numpy as jnp
import numpy as np
from jax import lax
from jax.experimental import pallas as pl
from jax.experimental.pallas import tpu as pltpu

F32 = jnp.float32
MXU_DTYPE = jnp.bfloat16
ACT_DTYPE = jnp.bfloat16
VMEM_LIMIT_BYTES = 48 * 1024 * 1024
EPS = 1e-6
NEG = -1e30

SSD_HEADDIM = 64
SSD_GROUPS = 8
SSD_STATE = 128
SSD_CHUNK = 128
SSD_GROUPS_PER_STEP = 8
SSD_CONV = 5
GRID_W = 64
POOL_WINDOWS = (2, 4, 8, 16)
ROW_TILE = 256
DT_PAD = 512

ADAM_LR = 0.001
ADAM_B1 = 0.9
ADAM_B2 = 0.999
ADAM_EPS = 1e-08
ADAM_WD = 0.01
ADAM_STEP = 10

MESH = pl.DeviceIdType.MESH


def _pcall(body, **kw):
    return pl.pallas_call(body, **kw)


def _params(*sem):
    return pltpu.CompilerParams(dimension_semantics=tuple(sem), vmem_limit_bytes=VMEM_LIMIT_BYTES)


def _pick_tile(n, cands):
    for t in cands:
        if n % t == 0:
            return t
    return n


PLANE = ((1, 0, 0), (0, 1, 0), (1, 1, 0))
PAIR = ((0, 0, 1),)
EVERYONE = tuple((a, b, d) for a in (0, 1) for b in (0, 1) for d in (0, 1) if a + b + d)
HBM = pl.BlockSpec(memory_space=pl.ANY)


def _position():
    return lax.axis_index("x"), lax.axis_index("y"), lax.axis_index("c")


def _flip(pos, rel):
    return tuple(1 - p if r else p for p, r in zip(pos, rel))


def _chip_index(pos):
    return 2 * pos[0] + pos[1]


def _device_index(pos):
    return 4 * pos[0] + 2 * pos[1] + pos[2]


class Exchange:
    def __init__(self, copies, n_remote, n_local, operands, out_shapes, aliases=None):
        self.copies, self.n_remote, self.n_local = copies, n_remote, n_local
        self.operands, self.out_shapes, self.aliases = list(operands), list(out_shapes), dict(aliases or {})

    def scratch(self):
        return [pltpu.SemaphoreType.DMA((max(self.n_remote, 1),)), pltpu.SemaphoreType.DMA((max(self.n_remote, 1),)),
                pltpu.SemaphoreType.DMA((max(self.n_local, 1),))]

    def descriptors(self, ins, outs, sems):
        send_sems, recv_sems, local_sems = sems
        local, remote = self.copies(ins, outs, _position())
        assert len(local) == self.n_local and len(remote) == self.n_remote
        cps = [pltpu.make_async_copy(src, dst, local_sems.at[k]) for k, (src, dst) in enumerate(local)]
        cps += [pltpu.make_async_remote_copy(src_ref=src, dst_ref=dst, send_sem=send_sems.at[k], recv_sem=recv_sems.at[k],
                                             device_id=peer, device_id_type=MESH) for k, (src, dst, peer) in enumerate(remote)]
        return cps


def combine(a, b):
    na, nao = len(a.operands), len(a.out_shapes)

    def copies(ins, outs, pos):
        la, ra = a.copies(ins[:na], outs[:nao], pos)
        lb, rb = b.copies(ins[na:], outs[nao:], pos)
        return la + lb, ra + rb

    aliases = dict(a.aliases)
    aliases.update({na + k: nao + v for k, v in b.aliases.items()})
    return Exchange(copies, a.n_remote + b.n_remote, a.n_local + b.n_local, a.operands + b.operands,
                    a.out_shapes + b.out_shapes, aliases)


class LazyDict(dict):
    def __getitem__(self, key):
        v = dict.__getitem__(self, key)
        if callable(v):
            v = v()
            dict.__setitem__(self, key, v)
        return v


def comm_call(name, ex):
    n_in, n_out = len(ex.operands), len(ex.out_shapes)

    def body(*refs):
        cps = ex.descriptors(refs[:n_in], refs[n_in:n_in + n_out], refs[n_in + n_out:])
        for cp in cps:
            cp.start()
        for cp in cps:
            cp.wait()

    return _pcall(
        body, name=name, out_shape=ex.out_shapes, in_specs=[HBM] * n_in, out_specs=[HBM] * n_out,
        scratch_shapes=ex.scratch(), input_output_aliases=ex.aliases,
        compiler_params=pltpu.CompilerParams(has_side_effects=True),
    )(*ex.operands)


def hosted_call(body, ex, operands, *, name, out_shape, grid, in_specs, out_specs, scratch_shapes=()):
    n_in, n_out, n_scr = len(operands), len(out_shape), len(scratch_shapes)
    sem = ("arbitrary",) * len(grid)
    if ex is None:
        res = _pcall(body, name=name, out_shape=list(out_shape), grid=grid, in_specs=list(in_specs),
                     out_specs=list(out_specs), scratch_shapes=list(scratch_shapes), compiler_params=_params(*sem))(*operands)
        return res, []
    x_in, x_out = len(ex.operands), len(ex.out_shapes)

    def wrapped(*refs):
        o = 0
        ins = refs[o:o + n_in]; o += n_in
        xins = refs[o:o + x_in]; o += x_in
        outs = refs[o:o + n_out]; o += n_out
        xouts = refs[o:o + x_out]; o += x_out
        scr = refs[o:o + n_scr]; o += n_scr
        sems = refs[o:]
        first = last = None
        for a, n in enumerate(grid):
            i = pl.program_id(a)
            first = (i == 0) if first is None else first & (i == 0)
            last = (i == n - 1) if last is None else last & (i == n - 1)

        @pl.when(first)
        def _():
            for cp in ex.descriptors(xins, xouts, sems):
                cp.start()

        body(*ins, *outs, *scr)

        @pl.when(last)
        def _():
            for cp in ex.descriptors(xins, xouts, sems):
                cp.wait()

    aliases = {n_in + k: n_out + v for k, v in ex.aliases.items()}
    res = _pcall(
        wrapped, name=name, out_shape=list(out_shape) + ex.out_shapes, grid=grid,
        in_specs=list(in_specs) + [HBM] * x_in, out_specs=list(out_specs) + [HBM] * x_out,
        scratch_shapes=list(scratch_shapes) + ex.scratch(), input_output_aliases=aliases,
        compiler_params=pltpu.CompilerParams(dimension_semantics=sem, vmem_limit_bytes=VMEM_LIMIT_BYTES,
                                             has_side_effects=True),
    )(*operands, *ex.operands)
    return res[:n_out], res[n_out:]


def _dot(a, b, dims):
    return lax.dot_general(a.astype(MXU_DTYPE), b.astype(MXU_DTYPE), (dims, ((), ())), preferred_element_type=F32)


_NN = ((1,), (0,))
_NT = ((1,), (1,))
_TN = ((0,), (0,))


@jax.custom_vjp
def _mm(a, b):
    return _dot(a, b, _NN)


def _mm_fwd(a, b):
    return _mm(a, b), (a, b)


def _mm_bwd(res, g):
    a, b = res
    return _dot(g, b, _NT).astype(a.dtype), _dot(a, g, _TN).astype(b.dtype)


_mm.defvjp(_mm_fwd, _mm_bwd)


@jax.custom_vjp
def _mm_nt(a, b):
    return _dot(a, b, _NT)


def _mm_nt_fwd(a, b):
    return _mm_nt(a, b), (a, b)


def _mm_nt_bwd(res, g):
    a, b = res
    return _dot(g, b, _NN).astype(a.dtype), _dot(g, a, _TN).astype(b.dtype)


_mm_nt.defvjp(_mm_nt_fwd, _mm_nt_bwd)


@jax.custom_vjp
def _mm_tn(a, b):
    return _dot(a, b, _TN)


def _mm_tn_fwd(a, b):
    return _mm_tn(a, b), (a, b)


def _mm_tn_bwd(res, g):
    a, b = res
    return _dot(b, g, _NT).astype(a.dtype), _dot(a, g, _NN).astype(b.dtype)


_mm_tn.defvjp(_mm_tn_fwd, _mm_tn_bwd)


def _dot_exact(m01, v):
    m = m01.astype(jnp.bfloat16)
    hi = v.astype(jnp.bfloat16)
    r1 = v - hi.astype(F32)
    mid = r1.astype(jnp.bfloat16)
    lo = (r1 - mid.astype(F32)).astype(jnp.bfloat16)
    out = jnp.dot(m, hi, preferred_element_type=F32)
    out = out + jnp.dot(m, mid, preferred_element_type=F32)
    return out + jnp.dot(m, lo, preferred_element_type=F32)


@jax.custom_vjp
def _lin01(m, mt, v):
    return _dot_exact(m, v)


def _lin01_fwd(m, mt, v):
    return _dot_exact(m, v), (m, mt)


def _lin01_bwd(res, g):
    m, mt = res
    return jnp.zeros_like(m), jnp.zeros_like(mt), _dot_exact(mt, g)


_lin01.defvjp(_lin01_fwd, _lin01_bwd)


MATMUL_VMEM_BUDGET = VMEM_LIMIT_BYTES * 5 // 6


def _mm_tiles(m, n, k_bytes_a, k_bytes_b, out_bytes, cands_m, cands_n):
    best = None
    for tm in cands_m:
        if m % tm:
            continue
        for tn in cands_n:
            if n % tn:
                continue
            need = 2 * (tm * k_bytes_a + tn * k_bytes_b + tm * tn * out_bytes)
            if need <= MATMUL_VMEM_BUDGET and (best is None or tm * tn > best[0] * best[1]):
                best = (tm, tn)
    assert best is not None, (m, n)
    return best


_ROW_CANDS = (4352, 2176, 1088, 768, 544, 512, 272, 256, 128, 16)
_COL_CANDS = (2816, 2048, 1408, 1024, 512, 256, 128)


def _one(res, xres, ex):
    return res[0] if ex is None else (res[0], xres)


def _block_cands(c):
    return (c,) + tuple(t for t in (512, 256, 128) if c % t == 0)


def matmul_nn(name, a, b, out_dtype=F32, ex=None, col0=0, ncols=None):
    M, K = a.shape
    if b.ndim == 3:
        nb, _, C = b.shape
        N, cands = nb * C, _block_cands(C)
    else:
        N, cands = (b.shape[1] - col0 if ncols is None else ncols), (512, 256, 128)
    tm, tn = _mm_tiles(M, N, K * a.dtype.itemsize, K * b.dtype.itemsize, jnp.dtype(out_dtype).itemsize,
                       _ROW_CANDS, cands)
    if b.ndim == 3:
        per = C // tn
        b_spec = pl.BlockSpec((None, K, tn), lambda j, i: (j // per, 0, j % per))
    else:
        assert col0 % tn == 0
        first = col0 // tn
        b_spec = pl.BlockSpec((K, tn), lambda j, i: (0, first + j))

    def body(a_ref, b_ref, o_ref):
        o_ref[...] = _dot(a_ref[...], b_ref[...], _NN).astype(o_ref.dtype)

    res, xres = hosted_call(
        body, ex, [a, b], name=name, out_shape=[jax.ShapeDtypeStruct((M, N), out_dtype)], grid=(N // tn, M // tm),
        in_specs=[pl.BlockSpec((tm, K), lambda j, i: (i, 0)), b_spec],
        out_specs=[pl.BlockSpec((tm, tn), lambda j, i: (i, j))])
    return _one(res, xres, ex)


def matmul_nt(name, g, b, out_dtype=F32, ex=None, offsets=None):
    pieces = list(g) if isinstance(g, (list, tuple)) else [g]
    offsets = list(offsets) if offsets is not None else [0]
    M = pieces[0].shape[0]
    if b.ndim == 3:
        nb, K, C = b.shape
        N = nb * C
        assert len(pieces) == 1
    else:
        K, N = b.shape
    g_bytes = sum(p.shape[1] * p.dtype.itemsize for p in pieces)
    tm, tk = _mm_tiles(M, K, g_bytes, N * b.dtype.itemsize, jnp.dtype(out_dtype).itemsize, _ROW_CANDS, _COL_CANDS)

    def body(*refs):
        b_ref, o_ref = refs[-2:]
        acc = None
        if b.ndim == 3:
            parts = [_dot(refs[0][:, k * C:(k + 1) * C], b_ref[k], _NT) for k in range(nb)]
        else:
            parts = [_dot(g_ref[...], b_ref[:, off:off + g_ref.shape[1]], _NT) for g_ref, off in zip(refs[:-2], offsets)]
        for part in parts:
            acc = part if acc is None else acc + part
        o_ref[...] = acc.astype(o_ref.dtype)

    b_spec = (pl.BlockSpec((nb, tk, C), lambda j, i: (0, j, 0)) if b.ndim == 3
              else pl.BlockSpec((tk, N), lambda j, i: (j, 0)))
    res, xres = hosted_call(
        body, ex, pieces + [b], name=name, out_shape=[jax.ShapeDtypeStruct((M, K), out_dtype)], grid=(K // tk, M // tm),
        in_specs=[pl.BlockSpec((tm, p.shape[1]), lambda j, i: (i, 0)) for p in pieces] + [b_spec],
        out_specs=[pl.BlockSpec((tm, tk), lambda j, i: (i, j))])
    return _one(res, xres, ex)


def matmul_tn(name, a, g, ex=None, blocks=1):
    M, K = a.shape
    N = g.shape[1]
    C = N // blocks
    tk, tn = _mm_tiles(K, N, M * a.dtype.itemsize, M * g.dtype.itemsize, 4, (512, 256, 128),
                       (512, 256, 128) if blocks == 1 else _block_cands(C))

    def body(a_ref, g_ref, o_ref):
        o_ref[...] = _dot(a_ref[...], g_ref[...], _TN)

    if blocks == 1:
        out_shape, out_spec = jax.ShapeDtypeStruct((K, N), F32), pl.BlockSpec((tk, tn), lambda i, j: (i, j))
    else:
        per = C // tn
        out_shape = jax.ShapeDtypeStruct((blocks, K, C), F32)
        out_spec = pl.BlockSpec((None, tk, tn), lambda i, j: (j // per, i, j % per))
    res, xres = hosted_call(
        body, ex, [a, g], name=name, out_shape=[out_shape], grid=(K // tk, N // tn),
        in_specs=[pl.BlockSpec((M, tk), lambda i, j: (0, i)), pl.BlockSpec((M, tn), lambda i, j: (0, j))],
        out_specs=[out_spec])
    return _one(res, xres, ex)


class Arg:
    def __init__(self, arr, block, imap, kind):
        self.arr, self.block, self.imap, self.kind = arr, block, imap, kind


class Rows:
    def __init__(self, nt, nct, tm, ncol=1):
        self.nt, self.nct, self.tm, self.ncol = nt, nct, tm, ncol

    def seg(self, i):
        return jnp.where(i >= self.nct, 1, 0)

    def spec(self, block, imap):
        return pl.BlockSpec(block, lambda j, i: imap(j, i, self.seg(i)))

    def row(self, arr, width, cb0=0, follow=False, roff=0, stride=1):
        f = stride if follow else 0
        return Arg(arr, (self.tm, width), lambda j, i, s: (i + roff, cb0 + f * j), "row")

    def vec(self, arr, follow=False, kind="acc"):
        w = arr.shape[1] // (self.ncol if follow else 1)
        f = 1 if follow else 0
        return Arg(arr, (1, w), lambda j, i, s: (0, f * j), kind)

    def segvec(self, arr, kind="seg"):
        return Arg(arr, (None, 1, arr.shape[2]), lambda j, i, s: (s, 0, 0), kind)


def _load(ref):
    return ref[...].astype(F32) if ref.dtype != F32 else ref[...]


def stage_fwd(name, f, rows, args, outs):
    n_in = len(args)

    def body(*refs):
        vals = [_load(r) for r in refs[:n_in]]
        res = f(*vals)
        for r, v in zip(refs[n_in:], res):
            r[...] = v.astype(r.dtype)

    T = rows.nt * rows.tm
    out_shape = [jax.ShapeDtypeStruct((T, w * (rows.ncol if fo else 1)), dt) for w, dt, fo in outs]
    out_specs = [pl.BlockSpec((rows.tm, w), (lambda j, i, fo=fo: (i, j if fo else 0))) for w, dt, fo in outs]
    res = _pcall(
        body, name=name, out_shape=out_shape, grid=(rows.ncol, rows.nt),
        in_specs=[rows.spec(a.block, a.imap) for a in args], out_specs=out_specs,
        compiler_params=_params("parallel", "parallel"),
    )(*[a.arr for a in args])
    return res


def stage_bwd(name, f, rows, args, cots, row_dtypes, ex=None, primal=()):
    n_in, n_ct = len(args), len(cots)
    diff = [k for k, a in enumerate(args) if a.kind != "const"]
    row_dt = {}
    for k in diff:
        if args[k].kind == "row":
            row_dt[k] = row_dtypes[len(row_dt)]

    def body(*refs):
        i = pl.program_id(1)
        vals = [_load(r) for r in refs[:n_in]]
        cts = tuple(_load(r) for r in refs[n_in:n_in + n_ct])
        outs = refs[n_in + n_ct:]

        def g(*dv):
            full = list(vals)
            for k, v in zip(diff, dv):
                full[k] = v
            return tuple(f(*full))

        prim, vjp = jax.vjp(g, *[vals[k] for k in diff])
        grads = vjp(cts)
        for o, v in zip(outs[len(diff):], prim):
            o[...] = v.astype(o.dtype)
        for k, o, gr in zip(diff, outs, grads):
            kind = args[k].kind
            if kind == "row":
                o[...] = gr.astype(o.dtype)
            else:
                first = (i == 0) | (i == rows.nct) if kind == "seg" else (i == 0)

                @pl.when(first)
                def _():
                    o[...] = gr.astype(o.dtype)

                @pl.when(jnp.logical_not(first))
                def _():
                    o[...] += gr.astype(o.dtype)

    T = rows.nt * rows.tm
    out_shape, out_specs = [], []
    for k in diff:
        a = args[k]
        if a.kind == "row":
            out_shape.append(jax.ShapeDtypeStruct((T, a.block[1] * (rows.ncol if _follows(a) else 1)), row_dt[k]))
            fo = _follows(a)
            out_specs.append(pl.BlockSpec(a.block, (lambda j, i, fo=fo: (i, j if fo else 0))))
        else:
            out_shape.append(jax.ShapeDtypeStruct(a.arr.shape, F32))
            out_specs.append(rows.spec(a.block, a.imap))
    for w, dt in primal:
        out_shape.append(jax.ShapeDtypeStruct((T, w), dt))
        out_specs.append(pl.BlockSpec((rows.tm, w), lambda j, i: (i, 0)))
    res, xres = hosted_call(
        body, ex, [a.arr for a in list(args) + list(cots)], name=name, out_shape=out_shape, grid=(rows.ncol, rows.nt),
        in_specs=[rows.spec(a.block, a.imap) for a in list(args) + list(cots)], out_specs=out_specs)
    return res if ex is None else (res, xres)


def _follows(a):
    return a.imap(1, 0, 0)[-1] != a.imap(0, 0, 0)[-1]


def _rms(x):
    return x * lax.rsqrt(jnp.mean(x * x, axis=-1, keepdims=True) + EPS)


def f_norm_mod(x, g, sh, sc):
    return ((_rms(x) * g) * (1.0 + sc) + sh,)


def f_resid_norm_mod(x, mo, ga, g, sh, sc):
    x1 = x + ga * mo
    return x1, (_rms(x1) * g) * (1.0 + sc) + sh


def f_resid(x, dn, ga):
    return (x + ga * dn,)


def f_silu(x):
    return (x * jax.nn.sigmoid(x),)


def f_bias(x, b):
    return (x + b,)


def f_ssd_gate(y0, y1, xs, z, dskip, nw):
    y = y0 + y1 + dskip * xs
    return (_rms(y * (z * jax.nn.sigmoid(z))) * nw,)


def f_pool(u, pmat, pmat_t, inv_cnt, pw, scale):
    pm = _lin01(pmat, pmat_t, u) * inv_cnt - u
    return (_mm(pm, pw) * scale,)


def f_merge(o_ssd, o_pool, gl_ssd, gl_pool):
    return (jax.nn.sigmoid(gl_ssd) * o_ssd + jax.nn.sigmoid(gl_pool) * o_pool,)


def _column_splitter(n):
    @jax.custom_vjp
    def split(x):
        w = x.shape[1] // n
        return tuple(x[:, k * w:(k + 1) * w] for k in range(n))

    def fwd(x):
        return split(x), None

    def bwd(_, g):
        return (jnp.concatenate(g, axis=1),)

    split.defvjp(fwd, bwd)
    return split


_halve_cols = _column_splitter(2)
_quarter_cols = _column_splitter(len(POOL_WINDOWS))


def f_swiglu(gu):
    a, b = _halve_cols(gu)
    return ((a * jax.nn.sigmoid(a)) * b,)


def f_pool_all(u, pmat, pmat_t, inv_cnt, scale, *pws):
    outs = [f_pool(part, pmat[k], pmat_t[k], inv_cnt[k], pws[k], 1.0)[0] for k, part in enumerate(_quarter_cols(u))]
    return (jnp.concatenate(outs, axis=1) * scale,)


def f_loss_resid(x1, dn, ga, tgt, g):
    err = _rms(x1 + ga * dn) * g - tgt
    return (0.5 * jnp.mean(err * err, axis=-1, keepdims=True),)


CONV_TILE = 128


CONV_GAP = 8


def _gapped(v, n_ctx):
    z = jnp.zeros((CONV_GAP, v.shape[1]), v.dtype)
    return jnp.concatenate([v[:n_ctx], z, v[n_ctx:], z], axis=0)


def _ungapped(v, n_ctx):
    return jnp.concatenate([v[:n_ctx], v[n_ctx + CONV_GAP:v.shape[0] - CONV_GAP]], axis=0)


def _shift_rows(v, j):
    return v if j == 0 else pltpu.roll(v, (-j) % v.shape[0], 0)


def conv_fwd(name, proj, conv_w, conv_b, n_ctx, width, ex=None):
    T = proj.shape[0]
    half = SSD_CONV // 2

    def body(u_ref, w_ref, b_ref, o_ref):
        u = _gapped(u_ref[...].astype(F32), n_ctx)
        pre = jnp.broadcast_to(b_ref[...], u.shape)
        for k in range(SSD_CONV):
            pre = pre + w_ref[k:k + 1, :] * _shift_rows(u, k - half)
        o_ref[...] = _ungapped(pre * jax.nn.sigmoid(pre), n_ctx)

    col = lambda t: (0, t)
    res, xres = hosted_call(
        body, ex, [proj, conv_w, conv_b], name=name, out_shape=[jax.ShapeDtypeStruct((T, width), F32)],
        grid=(width // CONV_TILE,),
        in_specs=[pl.BlockSpec((T, CONV_TILE), col), pl.BlockSpec((SSD_CONV, CONV_TILE), col),
                  pl.BlockSpec((1, CONV_TILE), col)],
        out_specs=[pl.BlockSpec((T, CONV_TILE), col)])
    return res[0], xres


def conv_bwd(name, proj, conv_w, conv_b, d_act2, d_skip, n_ctx, width, ex=None):
    T = proj.shape[0]
    half = SSD_CONV // 2

    def body(u_ref, w_ref, b_ref, c0_ref, c1_ref, cs_ref, du_ref, dw_ref, db_ref):
        t = pl.program_id(0)
        u = _gapped(u_ref[...].astype(F32), n_ctx)
        pre = jnp.broadcast_to(b_ref[...], u.shape)
        for k in range(SSD_CONV):
            pre = pre + w_ref[k:k + 1, :] * _shift_rows(u, k - half)
        sg = jax.nn.sigmoid(pre)
        ct = c0_ref[...].astype(F32) + c1_ref[...].astype(F32) + jnp.where(t % 4 < 2, cs_ref[...].astype(F32), 0.0)
        dpre = _gapped(ct, n_ctx) * (sg * (1.0 + pre * (1.0 - sg)))
        du = jnp.zeros_like(u)
        for k in range(SSD_CONV):
            du = du + w_ref[k:k + 1, :] * _shift_rows(dpre, half - k)
            dw_ref[k:k + 1, :] = jnp.sum(dpre * _shift_rows(u, k - half), axis=0, keepdims=True)
        du_ref[...] = _ungapped(du, n_ctx).astype(du_ref.dtype)
        db_ref[...] = jnp.sum(dpre, axis=0, keepdims=True)

    col = lambda t: (0, t)
    skip_col = lambda t: (0, (t // 4) * 2 + jnp.minimum(t % 4, 1))
    res, xres = hosted_call(
        body, ex, [proj, conv_w, conv_b, d_act2[0], d_act2[1], d_skip], name=name,
        out_shape=[jax.ShapeDtypeStruct((T, width), ACT_DTYPE), jax.ShapeDtypeStruct((SSD_CONV, width), F32),
                   jax.ShapeDtypeStruct((1, width), F32)],
        grid=(width // CONV_TILE,),
        in_specs=[pl.BlockSpec((T, CONV_TILE), col), pl.BlockSpec((SSD_CONV, CONV_TILE), col),
                  pl.BlockSpec((1, CONV_TILE), col), pl.BlockSpec((T, CONV_TILE), col),
                  pl.BlockSpec((T, CONV_TILE), col), pl.BlockSpec((T, CONV_TILE), skip_col)],
        out_specs=[pl.BlockSpec((T, CONV_TILE), col), pl.BlockSpec((SSD_CONV, CONV_TILE), col),
                   pl.BlockSpec((1, CONV_TILE), col)])
    return res[0], res[1], res[2], xres


@jax.custom_vjp
def _cumsum_mat(tri, tri_t, a):
    return jnp.dot(tri, a, precision=lax.Precision.HIGHEST, preferred_element_type=F32)


def _cumsum_fwd(tri, tri_t, a):
    return _cumsum_mat(tri, tri_t, a), (tri, tri_t)


def _cumsum_bwd(res, g):
    tri, tri_t = res
    return (jnp.zeros_like(tri), jnp.zeros_like(tri_t),
            jnp.dot(tri_t, g, precision=lax.Precision.HIGHEST, preferred_element_type=F32))


_cumsum_mat.defvjp(_cumsum_fwd, _cumsum_bwd)


def _ssd_dt(dtraw, dt_bias, a_log, tri, tri_t):
    dt_all = jax.nn.softplus(dtraw + dt_bias)
    a_all = dt_all * (-jnp.exp(a_log))
    return dt_all, a_all, _cumsum_mat(tri, tri_t, a_all)


def _ssd_chunk(xs, bm, cm, dt_all, a_all, s_all, s_in, mask, idx0):
    (xs,), (s_in,) = xs, s_in
    Q = xs.shape[0]
    hpg = xs.shape[1] // SSD_HEADDIM
    lane = lax.broadcasted_iota(jnp.int32, dt_all.shape, 1)
    head = lax.broadcasted_iota(jnp.int32, xs.shape, 1) // SSD_HEADDIM
    head1 = lax.broadcasted_iota(jnp.int32, (1, xs.shape[1]), 1) // SSD_HEADDIM

    def pick(v, r):
        return jnp.sum(jnp.where(lane == idx0 + r, v, 0.0), axis=1, keepdims=True)

    def expand(cols, hd):
        out = cols[hpg - 1]
        for r in range(hpg - 2, -1, -1):
            out = jnp.where(hd == r, cols[r], out)
        return out

    def spread(*cols):
        return expand([jnp.broadcast_to(c, xs.shape) for c in cols], head)

    dt_r = [pick(dt_all, r) for r in range(hpg)]
    s_r = [pick(s_all, r) for r in range(hpg)]
    stot_r = [jnp.sum(jnp.where(lane == idx0 + r, a_all, 0.0), keepdims=True).reshape(1, 1) for r in range(hpg)]

    xd = xs * spread(*dt_r)
    cb = _mm_nt(cm, bm)
    weights, stacked = [], []
    for r in range(hpg):
        sm = jnp.broadcast_to(s_r[r], (Q, Q))
        weights.append(cb * jnp.exp(jnp.where(mask, sm - sm.T, NEG)))
        stacked.append(jnp.where(head == r, xd, 0.0))
    y = spread(*[jnp.exp(c) for c in s_r]) * _mm(cm, s_in)
    y = y + _mm(jnp.concatenate(weights, axis=1), jnp.concatenate(stacked, axis=0))
    to_end = spread(*[jnp.exp(t - c) for t, c in zip(stot_r, s_r)])
    carry = expand([jnp.broadcast_to(jnp.exp(t), (1, xs.shape[1])) for t in stot_r], head1)
    s_out = carry * s_in + _mm_tn(bm, xd * to_end)
    return [y], [s_out]


def _scan_consts():
    q = SSD_CHUNK
    i = np.arange(q)[:, None]
    j = np.arange(q)[None, :]
    fwd = (j <= i).astype(np.float32)
    bwd = (j >= i).astype(np.float32)
    tri = np.stack([fwd, bwd])
    return jnp.asarray(tri), jnp.asarray(np.stack([fwd.T, bwd.T]))


def _chunk_of(d, k, ncc, nc):
    rev = jnp.where(k < ncc, ncc - 1 - k, nc - 1 + ncc - k)
    return jnp.where(d == 0, k, rev)


def ssd_fwd(name, xbc, dtraw, dt_bias, a_log, n_ctx, ex=None):
    T = xbc.shape[0]
    q, G = SSD_CHUNK, SSD_GROUPS
    nc, ncc = T // q, n_ctx // q
    gw = xbc.shape[1] // G
    xw = gw - 2 * SSD_STATE
    hpg = xw // SSD_HEADDIM
    nh = G * hpg
    tri, tri_t = _scan_consts()

    gs = SSD_GROUPS_PER_STEP

    def body(x0_ref, x1_ref, dt0_ref, dt1_ref, bias_ref, alog_ref, tri_ref, trit_ref, y0_ref, y1_ref, sin_ref, state):
        gb, k = pl.program_id(0), pl.program_id(1)

        @pl.when(k == 0)
        def _():
            state[...] = jnp.zeros_like(state)

        for d, (x_ref, dt_ref, y_ref) in enumerate(((x0_ref, dt0_ref, y0_ref), (x1_ref, dt1_ref, y1_ref))):
            tri_v = tri_ref[d]
            dt_all, a_all, s_all = _ssd_dt(dt_ref[...], bias_ref[...], alog_ref[...], tri_v, trit_ref[d])
            for j in range(gs):
                o = j * gw
                sin_ref[d, j] = state[d, j]
                (y,), (s_out,) = _ssd_chunk(
                    [x_ref[:, o:o + xw]], x_ref[:, o + xw:o + xw + SSD_STATE], x_ref[:, o + xw + SSD_STATE:o + gw],
                    dt_all, a_all, s_all, [state[d, j]], tri_v > 0.5, d * nh + (gb * gs + j) * hpg)
                y_ref[:, j * xw:(j + 1) * xw] = y.astype(y_ref.dtype)
                state[d, j] = s_out

    ch = lambda d, k: _chunk_of(d, k, ncc, nc)
    y_shape = jax.ShapeDtypeStruct((T, G * xw), ACT_DTYPE)
    res, xres = hosted_call(
        body, ex, [xbc, xbc, dtraw, dtraw, dt_bias, a_log, tri, tri_t], name=name,
        out_shape=[y_shape, y_shape, jax.ShapeDtypeStruct((2, nc, G, SSD_STATE, xw), F32)],
        grid=(G // gs, nc),
        in_specs=[pl.BlockSpec((q, gs * gw), lambda g, k: (ch(0, k), g)),
                  pl.BlockSpec((q, gs * gw), lambda g, k: (ch(1, k), g)),
                  pl.BlockSpec((q, 128), lambda g, k: (ch(0, k), 0)),
                  pl.BlockSpec((q, 128), lambda g, k: (ch(1, k), 0)),
                  pl.BlockSpec((1, 128), lambda g, k: (0, 0)),
                  pl.BlockSpec((1, 128), lambda g, k: (0, 0)),
                  pl.BlockSpec((2, q, q), lambda g, k: (0, 0, 0)),
                  pl.BlockSpec((2, q, q), lambda g, k: (0, 0, 0))],
        out_specs=[pl.BlockSpec((q, gs * xw), lambda g, k: (ch(0, k), g)),
                   pl.BlockSpec((q, gs * xw), lambda g, k: (ch(1, k), g)),
                   pl.BlockSpec((2, None, gs, SSD_STATE, xw), lambda g, k: (0, k, g, 0, 0))],
        scratch_shapes=[pltpu.VMEM((2, gs, SSD_STATE, xw), F32)])
    return res[0], res[1], res[2], xres


def ssd_bwd(name, xbc, dtraw, dt_bias, a_log, states, dy, n_ctx, ex=None):
    T = xbc.shape[0]
    q, G = SSD_CHUNK, SSD_GROUPS
    nc, ncc = T // q, n_ctx // q
    gw = xbc.shape[1] // G
    xw = gw - 2 * SSD_STATE
    hpg = xw // SSD_HEADDIM
    nh = G * hpg
    tri, tri_t = _scan_consts()

    gs = SSD_GROUPS_PER_STEP

    def body(x0_ref, x1_ref, dt0_ref, dt1_ref, bias_ref, alog_ref, tri_ref, trit_ref, sin_ref, dy0_ref, dy1_ref,
             dx0_ref, dx1_ref, ddt_ref, dbias_ref, dalog_ref, dstate):
        gb, k = pl.program_id(0), pl.program_id(1)

        @pl.when((gb == 0) & (k == 0))
        def _():
            ddt_ref[...] = jnp.zeros_like(ddt_ref)
            dbias_ref[...] = jnp.zeros_like(dbias_ref)
            dalog_ref[...] = jnp.zeros_like(dalog_ref)

        @pl.when(k == 0)
        def _():
            dstate[...] = jnp.zeros_like(dstate)

        tris = [(tri_ref[d], trit_ref[d]) for d in range(2)]
        per = 4

        def fn(bias, alog, dtraw0, dtraw1, *per_group):
            ys, s_outs = [], []
            for d, dtraw in enumerate((dtraw0, dtraw1)):
                tri_v, trit_v = tris[d]
                dt_all, a_all, s_all = _ssd_dt(dtraw, bias, alog, tri_v, trit_v)
                for j in range(gs):
                    xs, bm, cm, s_in = per_group[per * (d * gs + j):per * (d * gs + j + 1)]
                    y, s_out = _ssd_chunk([xs], bm, cm, dt_all, a_all, s_all, [s_in], tri_v > 0.5,
                                          d * nh + (gb * gs + j) * hpg)
                    ys += y
                    s_outs += s_out
            return ys, s_outs

        per_group, dys, dss = [], [], []
        for d, (x_ref, dy_ref) in enumerate(((x0_ref, dy0_ref), (x1_ref, dy1_ref))):
            for j in range(gs):
                o = j * gw
                per_group += [x_ref[:, o:o + xw], x_ref[:, o + xw:o + xw + SSD_STATE], x_ref[:, o + xw + SSD_STATE:o + gw],
                              sin_ref[d, j]]
                dys.append(dy_ref[:, j * xw:(j + 1) * xw].astype(F32))
                dss.append(dstate[d, j])
        _, vjp = jax.vjp(fn, bias_ref[...], alog_ref[...], dt0_ref[...], dt1_ref[...], *per_group)
        cts = vjp((dys, dss))
        dbias, dalog, ddt0, ddt1 = cts[:4]
        for d, dx_ref in enumerate((dx0_ref, dx1_ref)):
            for j in range(gs):
                o = j * gw
                dxs, dbm, dcm, ds_in = cts[4 + per * (d * gs + j):4 + per * (d * gs + j + 1)]
                dx_ref[:, o:o + xw] = dxs.astype(dx_ref.dtype)
                dx_ref[:, o + xw:o + xw + SSD_STATE] = dbm.astype(dx_ref.dtype)
                dx_ref[:, o + xw + SSD_STATE:o + gw] = dcm.astype(dx_ref.dtype)
                dstate[d, j] = ds_in
        for d, ddt in enumerate((ddt0, ddt1)):
            row0 = pl.multiple_of(_chunk_of(d, nc - 1 - k, ncc, nc) * q, q)
            ddt_ref[pl.ds(row0, q), :] += ddt
        dbias_ref[...] += dbias
        dalog_ref[...] += dalog

    ch = lambda d, k: _chunk_of(d, nc - 1 - k, ncc, nc)
    dx_shape = jax.ShapeDtypeStruct((T, G * gw), ACT_DTYPE)
    res, xres = hosted_call(
        body, ex, [xbc, xbc, dtraw, dtraw, dt_bias, a_log, tri, tri_t, states, dy, dy], name=name,
        out_shape=[dx_shape, dx_shape, jax.ShapeDtypeStruct((T, 128), F32),
                   jax.ShapeDtypeStruct((1, 128), F32), jax.ShapeDtypeStruct((1, 128), F32)],
        grid=(G // gs, nc),
        in_specs=[pl.BlockSpec((q, gs * gw), lambda g, k: (ch(0, k), g)),
                  pl.BlockSpec((q, gs * gw), lambda g, k: (ch(1, k), g)),
                  pl.BlockSpec((q, 128), lambda g, k: (ch(0, k), 0)),
                  pl.BlockSpec((q, 128), lambda g, k: (ch(1, k), 0)),
                  pl.BlockSpec((1, 128), lambda g, k: (0, 0)),
                  pl.BlockSpec((1, 128), lambda g, k: (0, 0)),
                  pl.BlockSpec((2, q, q), lambda g, k: (0, 0, 0)),
                  pl.BlockSpec((2, q, q), lambda g, k: (0, 0, 0)),
                  pl.BlockSpec((2, None, gs, SSD_STATE, xw), lambda g, k: (0, nc - 1 - k, g, 0, 0)),
                  pl.BlockSpec((q, gs * xw), lambda g, k: (ch(0, k), g)),
                  pl.BlockSpec((q, gs * xw), lambda g, k: (ch(1, k), g))],
        out_specs=[pl.BlockSpec((q, gs * gw), lambda g, k: (ch(0, k), g)),
                   pl.BlockSpec((q, gs * gw), lambda g, k: (ch(1, k), g)),
                   pl.BlockSpec((T, 128), lambda g, k: (0, 0)),
                   pl.BlockSpec((1, 128), lambda g, k: (0, 0)),
                   pl.BlockSpec((1, 128), lambda g, k: (0, 0))],
        scratch_shapes=[pltpu.VMEM((2, gs, SSD_STATE, xw), F32)])
    return res[0], res[1], res[2], res[3], res[4], xres


def _perm_xbc(a):
    G = SSD_GROUPS
    n = a.shape[-1]
    gn = G * SSD_STATE
    di = n - 2 * gn
    lead = a.shape[:-1]
    xs = a[..., :di].reshape(lead + (G, di // G))
    bm = a[..., di:di + gn].reshape(lead + (G, SSD_STATE))
    cm = a[..., di + gn:].reshape(lead + (G, SSD_STATE))
    return jnp.concatenate([xs, bm, cm], axis=-1).reshape(lead + (n,))


def _unperm_xbc(a):
    G = SSD_GROUPS
    n = a.shape[-1]
    gn = G * SSD_STATE
    di = n - 2 * gn
    lead = a.shape[:-1]
    r = a.reshape(lead + (G, n // G))
    xw = di // G
    return jnp.concatenate([r[..., :xw].reshape(lead + (di,)), r[..., xw:xw + SSD_STATE].reshape(lead + (gn,)),
                            r[..., xw + SSD_STATE:].reshape(lead + (gn,))], axis=-1)


def _pool_consts(tm, n_ctx):
    assert n_ctx == tm and tm % GRID_W == 0
    mats, cnts = [], []
    for seq in (n_ctx, GRID_W):
        t = np.arange(tm)
        tt = t % seq
        base = t - tt
        ms, cs = [], []
        for k in POOL_WINDOWS:
            lo = np.clip(tt - k // 2, 0, seq) + base
            hi = np.clip(tt + k // 2, 0, seq) + base
            m = ((t[None, :] >= lo[:, None]) & (t[None, :] < hi[:, None])).astype(np.float32)
            ms.append(m)
            cs.append((1.0 / (hi - lo).astype(np.float32))[:, None])
        mats.append(np.stack(ms))
        cnts.append(np.stack(cs))
    m = np.stack(mats)
    return jnp.asarray(m), jnp.asarray(np.swapaxes(m, -1, -2)), jnp.asarray(np.stack(cnts).astype(np.float32))


def _prep_layer_weights(w_ada, b_ada, g_mix, w_in, conv_w, conv_b, dt_bias, a_log, d_skip, ssd_norm_w, w_ssd_out,
                        pool_w, pool_scale, w_pool_out, w_out, g_ffn, w_gate_up, w_down):
    D = w_in.shape[0]
    di = ssd_norm_w.shape[0]
    xbc = conv_w.shape[1]
    nh2 = dt_bias.size
    pw = pool_scale.shape[0]
    o = 0
    wz = w_in[:, o:o + di]; o += di
    wx = w_in[:, o:o + xbc]; o += xbc
    wdt = w_in[:, o:o + nh2]; o += nh2
    wp = w_in[:, o:o + pw]; o += pw
    wg = w_in[:, o:]
    w1 = jnp.concatenate([_perm_xbc(wx), wz, wg, wp, wdt, jnp.zeros((D, DT_PAD - nh2), w_in.dtype)], axis=1)
    pad128 = lambda v: jnp.concatenate([v.reshape(1, -1), jnp.zeros((1, 128 - v.size), F32)], axis=1)
    return dict(
        w_ada=w_ada, b_ada=b_ada.reshape(1, -1), g_mix=g_mix.reshape(1, -1), w1=w1,
        conv_w=_perm_xbc(conv_w), conv_b=_perm_xbc(conv_b.reshape(1, -1)),
        dt_bias=pad128(dt_bias), a_log=pad128(a_log),
        dskip=jnp.repeat(d_skip[0] + d_skip[1], SSD_HEADDIM).reshape(1, -1),
        ssd_norm_w=ssd_norm_w.reshape(1, -1), w_ssd_out=w_ssd_out, pool_w=pool_w,
        pool_scale=pool_scale.reshape(1, -1), w_pool_out=w_pool_out, w_out=w_out, g_ffn=g_ffn.reshape(1, -1),
        w_gate_up=w_gate_up, w_down=w_down)


def _unprep_layer_grads(g, dims):
    di, xbc, nh2, pw = dims
    dxbc, dz, dgs, dgp, dp, ddt = g["w1"]
    r = dxbc.reshape(SSD_GROUPS, xbc // SSD_GROUPS, dxbc.shape[1])
    xw = di // SSD_GROUPS
    parts = [r[:, :xw], r[:, xw:xw + SSD_STATE], r[:, xw + SSD_STATE:]]
    w_in_t = jnp.concatenate([dz] + [p.reshape(-1, dxbc.shape[1]) for p in parts] + [ddt[:nh2], dp, dgs, dgp], axis=0)
    nh = nh2 // 2
    dsk = g["dskip"].reshape(nh, SSD_HEADDIM).sum(axis=1)
    return dict(
        w_ada=g["w_ada"], b_ada=g["b_ada"].reshape(-1), g_mix=g["g_mix"].reshape(-1),
        w_in=w_in_t,
        conv_w=_unperm_xbc(g["conv_w"]), conv_b=_unperm_xbc(g["conv_b"]).reshape(-1),
        dt_bias=g["dt_bias"][0, :nh2].reshape(2, nh), a_log=g["a_log"][0, :nh2].reshape(2, nh),
        d_skip=jnp.stack([dsk, dsk]), ssd_norm_w=g["ssd_norm_w"].reshape(-1), w_ssd_out=g["w_ssd_out"],
        pool_w=g["pool_w"], pool_scale=g["pool_scale"].reshape(-1), w_pool_out=g["w_pool_out"], w_out=g["w_out"],
        g_ffn=g["g_ffn"].reshape(-1), w_gate_up=g["w_gate_up"], w_down=g["w_down"])


COND_ROWS = 16


def _split_mods(m):
    d = m.shape[1] // 6
    return [m[:2, k * d:(k + 1) * d].reshape(2, 1, d) for k in range(6)]


def _pool_args(rows, proj, col_block, width, pc, w):
    seg_const = lambda a: Arg(a, (None,) + a.shape[1:], lambda j, i, s: (s, 0, 0, 0), "const")
    pws = [Arg(w["pool_w"][k], w["pool_w"].shape[1:], lambda j, i, s: (0, 0), "acc") for k in range(w["pool_w"].shape[0])]
    return [rows.row(proj, width, col_block)] + [seg_const(a) for a in pc] + [rows.vec(w["pool_scale"])] + pws


TALL_ROW_TILE = 1088


def _tall_rows(T, ncol):
    tm = max(t for t in range(16, min(T, TALL_ROW_TILE) + 1, 16) if T % t == 0)
    return Rows(T // tm, 0, tm, ncol)


def _hosted(hosts, box, key):
    fn = (hosts or {}).get(key)
    return fn(box) if fn else None


def _layer_fwd(l, pre, cond_s, w, rows, n_ctx, pc, hosts=None, box=None):
    T, D = pre[0].shape if isinstance(pre, tuple) else pre.shape
    nt, nct, tm = rows.nt, rows.nct, rows.tm
    n = lambda s: f"l{l}_{s}"
    crow = Rows(1, 0, COND_ROWS)
    mraw = matmul_nn(n("ada_mm"), cond_s, w["w_ada"])
    (m,) = stage_fwd(n("ada_bias"), f_bias, crow, [crow.row(mraw, mraw.shape[1]), crow.vec(w["b_ada"])],
                     [(mraw.shape[1], F32, False)])
    sh1, sc1, ga1, sh2, sc2, ga2 = _split_mods(m)

    if isinstance(pre, tuple):
        x, h1 = stage_fwd(n("norm1"), f_resid_norm_mod, rows, _resid_norm_args(rows, pre, w["g_mix"], sh1, sc1, D),
                          [(D, F32, False), (D, ACT_DTYPE, False)])
    else:
        x = pre
        (h1,) = stage_fwd(n("norm1"), f_norm_mod, rows,
                          [rows.row(x, D), rows.vec(w["g_mix"]), rows.segvec(sh1), rows.segvec(sc1)],
                          [(D, ACT_DTYPE, False)])
    xbc_w = w["conv_w"].shape[1]
    di = w["ssd_norm_w"].shape[1]
    pw = w["pool_scale"].shape[1]
    c_z, c_g, c_p, c_dt = xbc_w, xbc_w + di, xbc_w + di + 2 * pw, xbc_w + di + 3 * pw
    ex = _hosted(hosts, box, "in_mm")
    proj = matmul_nn(n("in_mm"), h1, w["w1"], out_dtype=ACT_DTYPE, ex=ex, ncols=c_dt)
    if ex is not None:
        proj, box["in_mm"] = proj
    dtraw = matmul_nn(n("in_dt_mm"), h1, w["w1"], col0=c_dt, ncols=128)
    ex = _hosted(hosts, box, "conv")
    xbc, xres = conv_fwd(n("conv"), proj, w["conv_w"], w["conv_b"], n_ctx, xbc_w, ex)
    if ex is not None:
        box["conv"] = xres
    ex = _hosted(hosts, box, "ssd")
    y0, y1, states, xres = ssd_fwd(n("ssd"), xbc, dtraw, w["dt_bias"], w["a_log"], n_ctx, ex)
    y2 = (y0, y1)
    if ex is not None:
        box["ssd"] = xres

    G = SSD_GROUPS
    gw = di // G
    r8 = _tall_rows(T, G)
    gate_args = [r8.row(y2[0], gw, 0, True), r8.row(y2[1], gw, 0, True), r8.row(xbc, gw, 0, True, stride=2),
                 r8.row(proj, gw, c_z // gw, True), r8.vec(w["dskip"], True), r8.vec(w["ssd_norm_w"], True)]
    (ynw,) = stage_fwd(n("ssd_gate"), f_ssd_gate, r8, gate_args, [(gw, ACT_DTYPE, True)])
    ex = _hosted(hosts, box, "ssd_out_mm")
    o_ssd = matmul_nn(n("ssd_out_mm"), ynw, w["w_ssd_out"], ex=ex)
    if ex is not None:
        o_ssd, box["ssd_out_mm"] = o_ssd

    nw = len(POOL_WINDOWS)
    pg = pw // nw
    (ps,) = stage_fwd(n("pool"), f_pool_all, rows, _pool_args(rows, proj, c_p // pw, pw, pc, w), [(pw, ACT_DTYPE, False)])
    o_pool = matmul_nn(n("pool_out_mm"), ps, w["w_pool_out"])

    merge_args = [rows.row(o_ssd, D), rows.row(o_pool, D), rows.row(proj, pw, c_g // pw), rows.row(proj, pw, c_g // pw + 1)]
    (mg,) = stage_fwd(n("merge"), f_merge, rows, merge_args, [(D, ACT_DTYPE, False)])
    mo = matmul_nn(n("out_mm"), mg, w["w_out"])

    rn_args = [rows.row(x, D), rows.row(mo, D), rows.segvec(ga1), rows.vec(w["g_ffn"]), rows.segvec(sh2), rows.segvec(sc2)]
    x1, h2 = stage_fwd(n("norm2"), f_resid_norm_mod, rows, rn_args, [(D, F32, False), (D, ACT_DTYPE, False)])
    ex = _hosted(hosts, box, "gate_up_mm")
    gu = matmul_nn(n("gate_up_mm"), h2, w["w_gate_up"], ex=ex)
    if ex is not None:
        gu, box["gate_up_mm"] = gu
    fh = gu.shape[1] // 2
    (act,) = stage_fwd(n("swiglu"), f_swiglu, rows, [rows.row(gu, 2 * fh)], [(fh, ACT_DTYPE, False)])
    ex = _hosted(hosts, box, "down_mm")
    dn = matmul_nn(n("down_mm"), act, w["w_down"], ex=ex)
    if ex is not None:
        dn, box["down_mm"] = dn
    saved = dict(x=x, pre=pre, mraw=mraw, mods=(sh1, sc1, ga1, sh2, sc2, ga2), h1=h1, proj=proj, dtraw=dtraw, xbc=xbc, y2=y2,
                 states=states,
                 ynw=ynw, o_ssd=o_ssd, ps=ps, o_pool=o_pool, mg=mg, mo=mo, x1=x1, h2=h2, gu=gu, act=act, dn=dn,
                 cols=(c_z, c_g, c_p, c_dt))
    return (x1, dn, ga2), saved


def _resid_norm_args(rows, pre, g, sh, sc, D):
    x1, dn, ga2 = pre
    return [rows.row(x1, D), rows.row(dn, D), rows.segvec(ga2), rows.vec(g), rows.segvec(sh), rows.segvec(sc)]


def f_norm_mod_keep(x, g, sh, sc):
    return f_norm_mod(x, g, sh, sc)[0], x


def _layer_bwd(l, cot, cond_s, w, s, rows, n_ctx, pc, hosts=None, box=None):
    dx1, ddn, dga2 = cot
    T, D = dx1.shape
    nt, nct, tm = rows.nt, rows.nct, rows.tm
    n = lambda t: f"l{l}_{t}_bwd"
    sh1, sc1, ga1, sh2, sc2, ga2 = s["mods"]
    c_z, c_g, c_p, c_dt = s["cols"]
    x, proj, xbc, y2, gu = s["x"], s["proj"], s["xbc"], s["y2"], s["gu"]
    g = {}
    if box is not None:
        box["g"] = g

    ex = _hosted(hosts, box, "down_dx")
    dact = matmul_nt(n("down_dx"), ddn, w["w_down"], ex=ex)
    if ex is not None:
        dact, box["down_dx"] = dact
    ex = _hosted(hosts, box, "down_dw")
    g["w_down"] = matmul_tn(n("down_dw"), s["act"], ddn, ex=ex)
    if ex is not None:
        g["w_down"], box["down_dw"] = g["w_down"]
    fh = gu.shape[1] // 2
    (dgu,) = stage_bwd(n("swiglu"), f_swiglu, rows, [rows.row(gu, 2 * fh)], [rows.row(dact, fh)], [ACT_DTYPE])
    dh2 = matmul_nt(n("gate_up_dx"), dgu, w["w_gate_up"])
    g["w_gate_up"] = matmul_tn(n("gate_up_dw"), s["h2"], dgu, blocks=w["w_gate_up"].shape[0])

    rn_args = [rows.row(x, D), rows.row(s["mo"], D), rows.segvec(ga1), rows.vec(w["g_ffn"]), rows.segvec(sh2), rows.segvec(sc2)]
    dxr, dmo, dga1, g["g_ffn"], dsh2, dsc2 = stage_bwd(
        n("norm2"), f_resid_norm_mod, rows, rn_args, [rows.row(dx1, D), rows.row(dh2, D)], [F32, ACT_DTYPE])
    dmg = matmul_nt(n("out_dx"), dmo, w["w_out"])
    g["w_out"] = matmul_tn(n("out_dw"), s["mg"], dmo)

    pw = w["pool_scale"].shape[1]
    merge_args = [rows.row(s["o_ssd"], D), rows.row(s["o_pool"], D), rows.row(proj, pw, c_g // pw), rows.row(proj, pw, c_g // pw + 1)]
    do_ssd, do_pool, dgl_s, dgl_p = stage_bwd(n("merge"), f_merge, rows, merge_args, [rows.row(dmg, D)], [ACT_DTYPE] * 4)
    dps = matmul_nt(n("pool_out_dx"), do_pool, w["w_pool_out"])
    g["w_pool_out"] = matmul_tn(n("pool_out_dw"), s["ps"], do_pool)

    nw = len(POOL_WINDOWS)
    pg = pw // nw
    du_pool, g["pool_scale"], *dpw = stage_bwd(n("pool"), f_pool_all, rows, _pool_args(rows, proj, c_p // pw, pw, pc, w),
                                               [rows.row(dps, pw)], [ACT_DTYPE])
    g["pool_w"] = jnp.stack(dpw)

    dynw = matmul_nt(n("ssd_out_dx"), do_ssd, w["w_ssd_out"])
    g["w_ssd_out"] = matmul_tn(n("ssd_out_dw"), s["ynw"], do_ssd)
    G = SSD_GROUPS
    di = w["ssd_norm_w"].shape[1]
    gw = di // G
    r8 = _tall_rows(T, G)
    gate_args = [r8.row(y2[0], gw, 0, True), r8.row(y2[1], gw, 0, True), r8.row(xbc, gw, 0, True, stride=2),
                 r8.row(proj, gw, c_z // gw, True), r8.vec(w["dskip"], True), r8.vec(w["ssd_norm_w"], True)]
    gate_args[1].kind = "const"
    ex = _hosted(hosts, box, "ssd_gate")
    res = stage_bwd(n("ssd_gate"), f_ssd_gate, r8, gate_args, [r8.row(dynw, gw, 0, True)], [ACT_DTYPE] * 3, ex)
    if ex is not None:
        res, box["ssd_gate"] = res
    dy, dxs_skip, dz, g["dskip"], g["ssd_norm_w"] = res

    ex = _hosted(hosts, box, "ssd")
    dxbc0, dxbc1, ddt, g["dt_bias"], g["a_log"], xres = ssd_bwd(n("ssd"), xbc, s["dtraw"], w["dt_bias"], w["a_log"],
                                                                s["states"], dy, n_ctx, ex)
    dxbc2 = (dxbc0, dxbc1)
    if ex is not None:
        box["ssd"] = xres
    xbc_w = xbc.shape[1]
    ex = _hosted(hosts, box, "conv")
    dxbc_raw, g["conv_w"], g["conv_b"], xres = conv_bwd(n("conv"), proj, w["conv_w"], w["conv_b"], dxbc2, dxs_skip,
                                                         n_ctx, xbc_w, ex)
    if ex is not None:
        box["conv"] = xres
    pieces = [dxbc_raw, dz, dgl_s, dgl_p, du_pool, ddt]
    offsets = [0, c_z, c_g, c_g + pw, c_p, c_dt]
    ex = _hosted(hosts, box, "in_dx")
    dh1 = matmul_nt(n("in_dx"), pieces, w["w1"], ex=ex, offsets=offsets)
    if ex is not None:
        dh1, box["in_dx"] = dh1
    ex = _hosted(hosts, box, "in_dw")
    first = matmul_tn(n("in_dw0"), pieces[0], s["h1"], ex=ex)
    if ex is not None:
        first, box["in_dw"] = first
    g["w1"] = [first] + [matmul_tn(n(f"in_dw{k}"), p, s["h1"]) for k, p in enumerate(pieces) if k]

    if isinstance(s["pre"], tuple):
        dx1p, ddnp, dga2p, g["g_mix"], dsh1, dsc1 = stage_bwd(
            n("norm1"), f_resid_norm_mod, rows, _resid_norm_args(rows, s["pre"], w["g_mix"], sh1, sc1, D),
            [rows.row(dxr, D), rows.row(dh1, D)], [F32, ACT_DTYPE])
        dx = (dx1p, ddnp, dga2p)
    else:
        n1_args = [rows.row(x, D), rows.vec(w["g_mix"]), rows.segvec(sh1), rows.segvec(sc1)]
        dx, g["g_mix"], dsh1, dsc1 = stage_bwd(n("norm1"), f_norm_mod_keep, rows, n1_args,
                                               [rows.row(dh1, D), rows.row(dxr, D)], [F32])

    dm = jnp.concatenate([v.reshape(2, D) for v in (dsh1, dsc1, dga1, dsh2, dsc2, dga2)], axis=1)
    dm = jnp.concatenate([dm, jnp.zeros((COND_ROWS - 2, dm.shape[1]), F32)], axis=0)
    crow = Rows(1, 0, COND_ROWS)
    dmraw, g["b_ada"] = stage_bwd(n("ada_bias"), f_bias, crow, [crow.row(s["mraw"], dm.shape[1]), crow.vec(w["b_ada"])],
                                  [crow.row(dm, dm.shape[1])], [ACT_DTYPE])
    dcs = matmul_nt(n("ada_dx"), dmraw, w["w_ada"])
    g["w_ada"] = matmul_tn(n("ada_dw"), cond_s, dmraw, blocks=w["w_ada"].shape[0])
    return dx, dcs, g


def local_step(x, ctx, c, c_ctx, target, layer_w_fn, n_layers, g_final, fwd_hosts=None, bwd_hosts=None):
    L, D = x.shape
    n_ctx = ctx.shape[0]
    tm = ROW_TILE
    T = L + n_ctx
    rows = Rows(T // tm, n_ctx // tm, tm)
    pc = _pool_consts(tm, n_ctx)
    xa = jnp.concatenate([ctx, x], axis=0)
    cond = jnp.concatenate([c_ctx.reshape(1, D), c.reshape(1, D), jnp.zeros((COND_ROWS - 2, D), F32)], axis=0)
    crow = Rows(1, 0, COND_ROWS)
    (cond_s,) = stage_fwd("cond_silu", f_silu, crow, [crow.row(cond, D)], [(D, ACT_DTYPE, False)])

    saved, layer_w = [], []
    for l in range(n_layers):
        layer_w.append(layer_w_fn(l))
        box = {}
        xa, s = _layer_fwd(l, xa, cond_s, layer_w[l], rows, n_ctx, pc, fwd_hosts(l, box) if fwd_hosts else None, box)
        saved.append(s)

    x1, dn, ga2 = xa
    rl = Rows(L // tm, 0, tm)
    gf = g_final.reshape(1, D)
    tgt = rl.row(target, D)
    tgt.kind = "const"
    off = n_ctx // tm
    loss_args = [rl.row(x1, D, roff=off), rl.row(dn, D, roff=off), rl.vec(ga2[1]), tgt, rl.vec(gf)]
    ones = jnp.ones((L, 1), F32)
    dx1_lat, ddn_lat, dga2_lat, dgf, loss_rows = stage_bwd("loss", f_loss_resid, rl, loss_args, [rl.row(ones, 1)],
                                                           [F32, ACT_DTYPE], primal=[(1, F32)])
    loss = jnp.sum(loss_rows)
    cot = (jnp.concatenate([jnp.zeros((n_ctx, D), F32), dx1_lat], axis=0),
           jnp.concatenate([jnp.zeros((n_ctx, D), ACT_DTYPE), ddn_lat], axis=0),
           jnp.stack([jnp.zeros((1, D), F32), dga2_lat]))

    grads = [None] * n_layers
    dcs = jnp.zeros((COND_ROWS, D), F32)
    for l in reversed(range(n_layers)):
        box = {}
        hosts = bwd_hosts(l, grads, box) if bwd_hosts else None
        cot, dcs_l, grads[l] = _layer_bwd(l, cot, cond_s, layer_w[l], saved[l], rows, n_ctx, pc, hosts, box)
        dcs = dcs + dcs_l
    dx = cot
    (dcond,) = stage_bwd("cond_silu_bwd", f_silu, crow, [crow.row(cond, D)], [crow.row(dcs, D)], [F32])
    return loss, dx[n_ctx:], grads, dcond[0], dgf


def gather_chips(halves, conv=None):
    n = len(halves)
    ops = list(halves) + ([conv] if conv is not None else [])

    def copies(ins, outs, pos):
        c, me = pos[2], _chip_index(pos)
        pairs = [(s.at[c], o.at[me, c]) for s, o in zip(ins[:n], outs[:n])]
        pairs += [(s, o.at[me]) for s, o in zip(ins[n:], outs[n:])]
        return pairs, [(s, d, _flip(pos, rel)) for rel in PLANE for s, d in pairs]

    shapes = [jax.ShapeDtypeStruct((4,) + s.shape, s.dtype) for s in ops]
    return Exchange(copies, 3 * len(ops), len(ops), ops, shapes)


def gather_pair(gathered):
    n = len(gathered)

    def copies(ins, outs, pos):
        c = pos[2]
        return [], [(s.at[b, c], o.at[b, c], _flip(pos, PAIR[0])) for s, o in zip(ins, outs) for b in range(4)]

    shapes = [jax.ShapeDtypeStruct(g.shape, g.dtype) for g in gathered]
    return Exchange(copies, 4 * n, 0, gathered, shapes, aliases={k: k for k in range(n)})


def swap_halves(grads):
    n = len(grads)

    def copies(ins, outs, pos):
        c = pos[2]
        return [], [(g.at[b, 1 - c], o.at[b], _flip(pos, PAIR[0])) for g, o in zip(ins, outs) for b in range(4)]

    shapes = [jax.ShapeDtypeStruct((g.shape[0],) + g.shape[2:], g.dtype) for g in grads]
    return Exchange(copies, 4 * n, 0, grads, shapes)


def scatter_chips(sums):
    n = len(sums)

    def copies(ins, outs, pos):
        me = _chip_index(pos)
        local = [(p.at[me], o.at[me]) for p, o in zip(ins, outs)]
        remote = []
        for rel in PLANE:
            peer = _flip(pos, rel)
            remote += [(p.at[_chip_index(peer)], o.at[me], peer) for p, o in zip(ins, outs)]
        return local, remote

    shapes = [jax.ShapeDtypeStruct(p.shape, p.dtype) for p in sums]
    return Exchange(copies, 3 * n, n, sums, shapes)


def share_halves(finals):
    n = len(finals)

    def copies(ins, outs, pos):
        c = pos[2]
        return [], [(f.at[c], o.at[c], _flip(pos, PAIR[0])) for f, o in zip(ins, outs)]

    shapes = [jax.ShapeDtypeStruct(f.shape, f.dtype) for f in finals]
    return Exchange(copies, n, 0, finals, shapes, aliases={k: k for k in range(n)})


def gather_everyone(vec):
    def copies(ins, outs, pos):
        me = _device_index(pos)
        (v,), (o,) = ins, outs
        return [(v, o.at[me])], [(v, o.at[me], _flip(pos, rel)) for rel in EVERYONE]

    return Exchange(copies, len(EVERYONE), 1, [vec], [jax.ShapeDtypeStruct((8,) + vec.shape, vec.dtype)])


def _row_tile(rows, cols, n_bufs, mult=8):
    cap = VMEM_LIMIT_BYTES // 2 // (2 * n_bufs * cols * 4)
    for t in range(min(rows, cap) // mult * mult, 0, -mult):
        if rows % t == 0:
            return t
    return rows


def _adamw_update(w, g, m, v):
    nm = ADAM_B1 * m + (1.0 - ADAM_B1) * g
    nv = ADAM_B2 * v + (1.0 - ADAM_B2) * jnp.square(g)
    m_hat = nm / (1.0 - ADAM_B1 ** ADAM_STEP)
    v_hat = nv / (1.0 - ADAM_B2 ** ADAM_STEP)
    return -ADAM_LR * (m_hat / (jnp.sqrt(v_hat) + ADAM_EPS) + ADAM_WD * w), nm, nv


def adamw_small(name, ws, gs, ms, vs):
    n = len(ws)

    def body(*refs):
        ins, outs = refs[:4 * n], refs[4 * n:]
        for k in range(n):
            d, nm, nv = _adamw_update(ins[k][...], ins[n + k][...], ins[2 * n + k][...], ins[3 * n + k][...])
            outs[k][...] = d
            outs[n + k][...] = nm
            outs[2 * n + k][...] = nv

    shapes = [jax.ShapeDtypeStruct(a.shape, F32) for a in ws]
    vmem = pl.BlockSpec(memory_space=pltpu.VMEM)
    res = _pcall(body, name=name, out_shape=shapes * 3, in_specs=[vmem] * (4 * n), out_specs=[vmem] * (3 * n),
                 compiler_params=pltpu.CompilerParams(vmem_limit_bytes=VMEM_LIMIT_BYTES))(*ws, *gs, *ms, *vs)
    return res[:n], res[n:2 * n], res[2 * n:]


WIRE_DTYPE = jnp.bfloat16


def add_own_half(name, grads, recv, c):
    nb, _, R, C = grads.shape
    tr = _row_tile(R, C, 3, mult=16)

    def body(c_ref, g_ref, r_ref, o_ref):
        o_ref[...] = (g_ref[...] + r_ref[...]).astype(o_ref.dtype)

    spec = pl.BlockSpec((None, tr, C), lambda b, i, c_ref: (b, i, 0))
    return _pcall(
        body, name=name, out_shape=jax.ShapeDtypeStruct(recv.shape, WIRE_DTYPE),
        grid_spec=pltpu.PrefetchScalarGridSpec(
            num_scalar_prefetch=1, grid=(nb, R // tr),
            in_specs=[pl.BlockSpec((None, None, tr, C), lambda b, i, c_ref: (b, c_ref[0], i, 0)), spec],
            out_specs=spec),
        compiler_params=_params("parallel", "parallel"),
    )(c, grads, recv)


def sum_slots(name, a, c=None):
    n, R, C = a.shape
    tr = _row_tile(R, C, n + 1, mult=16 if a.dtype.itemsize == 2 else 8)

    def body(*refs):
        a_ref, o_ref = refs[-2:]
        acc = a_ref[0].astype(F32)
        for k in range(1, n):
            acc = acc + a_ref[k].astype(F32)
        o_ref[...] = acc

    if c is None:
        return _pcall(
            body, name=name, out_shape=jax.ShapeDtypeStruct((R, C), F32), grid=(R // tr,),
            in_specs=[pl.BlockSpec((n, tr, C), lambda i: (0, i, 0))], out_specs=pl.BlockSpec((tr, C), lambda i: (i, 0)),
            compiler_params=_params("parallel"),
        )(a)
    return _pcall(
        body, name=name, out_shape=jax.ShapeDtypeStruct((2, R, C), F32),
        grid_spec=pltpu.PrefetchScalarGridSpec(
            num_scalar_prefetch=1, grid=(R // tr,),
            in_specs=[pl.BlockSpec((n, tr, C), lambda i, c_ref: (0, i, 0))],
            out_specs=pl.BlockSpec((None, tr, C), lambda i, c_ref: (c_ref[0], i, 0))),
        compiler_params=_params("parallel"),
    )(c, a)


def adamw(name, w, g_layers, m, v):
    nl, R, C = w.shape
    assert len(g_layers) == nl
    tr = _row_tile(R, C, 8 + nl)
    nr = R // tr

    def body(*refs):
        w_ref, m_ref, v_ref = refs[:3]
        g_refs = refs[3:3 + nl]
        go_ref, d_ref, nm_ref, nv_ref = refs[3 + nl:]
        l = pl.program_id(0)
        gr = g_refs[0][...]
        for k in range(1, nl):
            gr = jnp.where(l == k, g_refs[k][...], gr)
        d_ref[...], nm_ref[...], nv_ref[...] = _adamw_update(w_ref[...], gr, m_ref[...], v_ref[...])
        go_ref[...] = gr

    spec = pl.BlockSpec((None, tr, C), lambda l, i: (l, i, 0))
    g_specs = [pl.BlockSpec((tr, C), (lambda l, i, k=k: (jnp.where(l == k, i, jnp.where(l < k, 0, nr - 1)), 0)))
               for k in range(nl)]
    return _pcall(
        body, name=name, out_shape=[jax.ShapeDtypeStruct((nl, R, C), F32)] * 4, grid=(nl, nr),
        in_specs=[spec] * 3 + g_specs, out_specs=[spec] * 4, compiler_params=_params("arbitrary", "arbitrary"),
    )(w, m, v, *g_layers)


BIG = ("w_ada", "w_in", "w_ssd_out", "pool_w", "w_pool_out", "w_out", "w_gate_up", "w_down")
COL_SHARDED = ("w_ada", "w_in", "w_gate_up")
BLOCK_LAYOUT = ("w_ada", "w_gate_up")
GRAD_TRANSPOSED = ("w_in",)
FIRST_USED = ("w_ada", "w_in")
MID_USED = ("w_ssd_out", "pool_w", "w_pool_out", "w_out")
END_USED = ("w_gate_up", "w_down")
LATER_USED = MID_USED + END_USED
assert FIRST_USED + LATER_USED == BIG
READY_LAST = FIRST_USED
READY_EARLY = LATER_USED
SMALL = ("c_ctx", "b_ada", "g_mix", "conv_w", "conv_b", "dt_bias", "a_log", "d_skip", "ssd_norm_w", "pool_scale",
         "g_ffn", "g_final")
WEIGHTS = ("c_ctx", "w_ada", "b_ada", "g_mix", "w_in", "conv_w", "conv_b", "dt_bias", "a_log", "d_skip", "ssd_norm_w",
           "w_ssd_out", "pool_w", "pool_scale", "w_pool_out", "w_out", "g_ffn", "w_gate_up", "w_down", "g_final")
LAYER_KEYS = ("w_ada", "b_ada", "g_mix", "w_in", "conv_w", "conv_b", "dt_bias", "a_log", "d_skip", "ssd_norm_w",
              "w_ssd_out", "pool_w", "pool_scale", "w_pool_out", "w_out", "g_ffn", "w_gate_up", "w_down")


def _shard2d(name, a):
    if name == "pool_w":
        return a.reshape(a.shape[0], a.shape[1] * a.shape[2], a.shape[3])
    return a


def _full_from_blocks(name, a):
    nb, R, C = a.shape
    if name in BLOCK_LAYOUT:
        return a
    if name in COL_SHARDED:
        return jnp.transpose(a, (1, 0, 2)).reshape(R, nb * C)
    if name == "pool_w":
        nw = len(POOL_WINDOWS)
        return jnp.transpose(a.reshape(nb, nw, R // nw, C), (1, 0, 2, 3)).reshape(nw, nb * R // nw, C)
    return a.reshape(nb * R, C)


def _blocks_from_full(name, g):
    nb = 4
    if name in BLOCK_LAYOUT:
        return g
    if name in COL_SHARDED and name not in GRAD_TRANSPOSED:
        K, N = g.shape
        return jnp.transpose(g.reshape(K, nb, N // nb), (1, 0, 2))
    if name == "pool_w":
        nw, r, C = g.shape
        return jnp.transpose(g.reshape(nw, nb, r // nb, C), (1, 0, 2, 3)).reshape(nb, nw * r // nb, C)
    return g.reshape(nb, g.shape[0] // nb, g.shape[1])


def _pack(arrs, rows):
    flat = jnp.concatenate([a.reshape(-1).astype(F32) for a in arrs])
    return jnp.concatenate([flat, jnp.zeros((rows * 128 - flat.size,), F32)]).reshape(rows, 128)


def _unpack(vec, shapes):
    flat = vec.reshape(-1)
    out, o = [], 0
    for s in shapes:
        n = int(np.prod(s))
        out.append(flat[o:o + n].reshape(s))
        o += n
    return out


def _rows_for(shapes):
    n = sum(int(np.prod(s)) for s in shapes)
    return -(-n // (8 * 128)) * 8


def kernel(x, c, ctx, c_ctx, w_ada, b_ada, g_mix, w_in, conv_w, conv_b, dt_bias, a_log, d_skip, ssd_norm_w, w_ssd_out, pool_w, pool_scale, w_pool_out, w_out, g_ffn, w_gate_up, w_down, g_final, loss_target, m_c_ctx, m_w_ada, m_b_ada, m_g_mix, m_w_in, m_conv_w, m_conv_b, m_dt_bias, m_a_log, m_d_skip, m_ssd_norm_w, m_w_ssd_out, m_pool_w, m_pool_scale, m_w_pool_out, m_w_out, m_g_ffn, m_w_gate_up, m_w_down, m_g_final, v_c_ctx, v_w_ada, v_b_ada, v_g_mix, v_w_in, v_conv_w, v_conv_b, v_dt_bias, v_a_log, v_d_skip, v_ssd_norm_w, v_w_ssd_out, v_pool_w, v_pool_scale, v_w_pool_out, v_w_out, v_g_ffn, v_w_gate_up, v_w_down, v_g_final):
    w = dict(c_ctx=c_ctx, w_ada=w_ada, b_ada=b_ada, g_mix=g_mix, w_in=w_in, conv_w=conv_w, conv_b=conv_b, dt_bias=dt_bias,
             a_log=a_log, d_skip=d_skip, ssd_norm_w=ssd_norm_w, w_ssd_out=w_ssd_out, pool_w=pool_w, pool_scale=pool_scale,
             w_pool_out=w_pool_out, w_out=w_out, g_ffn=g_ffn, w_gate_up=w_gate_up, w_down=w_down, g_final=g_final)
    m = dict(c_ctx=m_c_ctx, w_ada=m_w_ada, b_ada=m_b_ada, g_mix=m_g_mix, w_in=m_w_in, conv_w=m_conv_w, conv_b=m_conv_b,
             dt_bias=m_dt_bias, a_log=m_a_log, d_skip=m_d_skip, ssd_norm_w=m_ssd_norm_w, w_ssd_out=m_w_ssd_out,
             pool_w=m_pool_w, pool_scale=m_pool_scale, w_pool_out=m_w_pool_out, w_out=m_w_out, g_ffn=m_g_ffn,
             w_gate_up=m_w_gate_up, w_down=m_w_down, g_final=m_g_final)
    v = dict(c_ctx=v_c_ctx, w_ada=v_w_ada, b_ada=v_b_ada, g_mix=v_g_mix, w_in=v_w_in, conv_w=v_conv_w, conv_b=v_conv_b,
             dt_bias=v_dt_bias, a_log=v_a_log, d_skip=v_d_skip, ssd_norm_w=v_ssd_norm_w, w_ssd_out=v_w_ssd_out,
             pool_w=v_pool_w, pool_scale=v_pool_scale, w_pool_out=v_w_pool_out, w_out=v_w_out, g_ffn=v_g_ffn,
             w_gate_up=v_w_gate_up, w_down=v_w_down, g_final=v_g_final)
    assert x.shape[0] == 1, "one example per device"
    pos = _position()
    core = pos[2].astype(jnp.int32).reshape(1)
    n_layers = w_in.shape[0]
    assert n_layers == 2
    dims = (ssd_norm_w.shape[1], conv_w.shape[2] * 4, dt_bias[0].size, pool_scale.shape[1])
    shard = {k: _shard2d(k, w[k]) for k in BIG}

    def halves(a):
        return a.reshape(a.shape[:-2] + (2, a.shape[-2] // 2, a.shape[-1]))

    def whole(a):
        return a.reshape(a.shape[:-3] + (2 * a.shape[-2], a.shape[-1]))

    def wire_shards(l, names):
        return [halves(shard[k][l].astype(MXU_DTYPE)) for k in names]

    def full_weights(names, gathered):
        return {k: _full_from_blocks(k, whole(a)) for k, a in zip(names, gathered)}

    first = comm_call("gather0_chips", gather_chips(wire_shards(0, FIRST_USED), conv=conv_w))
    got0 = full_weights(FIRST_USED, comm_call("gather0_pair", gather_pair(first[:-1])))
    conv_all = first[-1]
    conv_full = [jnp.transpose(conv_all[:, l], (1, 0, 2)).reshape(conv_all.shape[2], -1) for l in range(n_layers)]

    boxes = {}

    n_first, n_mid = len(FIRST_USED), len(MID_USED)

    def layer_w_fn(l):
        f0, f1 = boxes.get(("fwd", 0)), boxes.get(("fwd", 1))
        if l == 0:
            full = dict(got0)
            late = {k: (lambda i=i: boxes[("fwd", 0)]["conv"][i]) for i, k in enumerate(LATER_USED)}
        else:
            full = full_weights(FIRST_USED, f0["gate_up_mm"][:n_first])
            full.update(full_weights(MID_USED, f0["down_mm"]))
            late = {k: (lambda i=i: boxes[("fwd", 1)]["ssd_out_mm"][i]) for i, k in enumerate(END_USED)}
        full["conv_w"] = conv_full[l]
        lw = LazyDict(_prep_layer_weights(*[full[k] if k in full else (None if k in late else w[k][l]) for k in LAYER_KEYS]))
        for k, get in late.items():
            lw[k] = (lambda k=k, get=get: _full_from_blocks(k, whole(get())))
        return lw

    def fwd_hosts(l, box):
        boxes[("fwd", l)] = box
        if l == 0:
            return {"in_mm": lambda box: gather_chips(wire_shards(0, LATER_USED)),
                    "conv": lambda box: gather_pair(box["in_mm"]),
                    "ssd": lambda box: gather_chips(wire_shards(1, FIRST_USED)),
                    "gate_up_mm": lambda box: combine(gather_pair(box["ssd"]), gather_chips(wire_shards(1, MID_USED))),
                    "down_mm": lambda box: gather_pair(box["gate_up_mm"][n_first:])}
        return {"ssd": lambda box: gather_chips(wire_shards(1, END_USED)),
                "ssd_out_mm": lambda box: gather_pair(box["ssd"])}

    def blocks(gl, names):
        return [halves(_blocks_from_full(k, gl[k])) for k in names]

    def pair_sums(tag, names, G, recv):
        return [add_own_half(f"pair_sum{tag}_{k}", g, r, core) for k, g, r in zip(names, G, recv)]

    def chip_sums(tag, names, parts):
        return [sum_slots(f"chip_sum{tag}_{k}", p, core) for k, p in zip(names, parts)]

    def reduce_now(tag, gl, names):
        G = blocks(gl, names)
        pair = pair_sums(tag, names, G, comm_call(f"swap{tag}", swap_halves(G)))
        fin = chip_sums(tag, names, comm_call(f"scatter{tag}", scatter_chips(pair)))
        return [whole(a) for a in comm_call(f"share{tag}", share_halves(fin))]

    small_layers = {}
    n_big = len(BIG)

    def bwd_hosts(l, grads, box):
        boxes[("bwd", l)] = box
        if l != 0:
            return None
        gl1 = _unprep_layer_grads(grads[1], dims)
        small_layers[1] = gl1
        G1 = blocks(gl1, BIG)
        early = {}

        def gate_host(box):
            early["G"] = blocks(box["g"], READY_EARLY)
            return swap_halves(early["G"])

        def scan_host(box):
            return combine(scatter_chips(pair_sums("1", BIG, G1, box["down_dx"] + box["down_dw"])),
                           scatter_chips(pair_sums("0e", READY_EARLY, early["G"], box["ssd_gate"])))

        def conv_host(box):
            return combine(share_halves(chip_sums("1", BIG, box["ssd"][:n_big])),
                           share_halves(chip_sums("0e", READY_EARLY, box["ssd"][n_big:])))

        return {"down_dx": lambda box: swap_halves(G1[:n_first]), "down_dw": lambda box: swap_halves(G1[n_first:]),
                "ssd_gate": gate_host, "ssd": scan_host, "in_dx": conv_host}

    loss, grad_x, grads, d_c_ctx, d_g_final = local_step(
        x[0], ctx[0], c[0], c_ctx, loss_target[0], layer_w_fn, n_layers, g_final, fwd_hosts, bwd_hosts)
    shared =[whole(a) for a in boxes[("bwd", 0)]["in_dx"]]
    reduced1 = shared[:n_big]
    gl0 = _unprep_layer_grads(grads[0], dims)
    small_layers[0] = gl0
    red0 = dict(zip(READY_EARLY, shared[n_big:]))
    red0.update(zip(READY_LAST, reduce_now("0", gl0, READY_LAST)))
    reduced0 = [red0[k] for k in BIG]

    small_full = dict(c_ctx=d_c_ctx, g_final=d_g_final.reshape(-1))
    for k in SMALL:
        if k not in small_full:
            small_full[k] = jnp.stack([small_layers[l][k] for l in range(n_layers)])
    shapes = [small_full[k].shape for k in SMALL] + [(1,)]
    packed = _pack([small_full[k] for k in SMALL] + [loss.reshape(1)], _rows_for(shapes))
    total = sum_slots("small_sum", comm_call("gather_small", gather_everyone(packed))[0])
    *small_vals, loss = _unpack(total, shapes)
    loss = loss.reshape(())
    small_g = dict(zip(SMALL, small_vals))
    cw = conv_w.shape[2]
    small_g["conv_w"] = lax.dynamic_slice_in_dim(small_g["conv_w"], _chip_index(pos) * cw, cw, axis=2)

    grad, delta, new_m, new_v = {}, {}, {}, {}
    for k, g0, g1 in zip(BIG, reduced0, reduced1):
        shp = w[k].shape
        if k in GRAD_TRANSPOSED:
            flat = lambda a: jnp.swapaxes(a, 1, 2)
            back = lambda a: jnp.swapaxes(a, 1, 2)
        else:
            flat = lambda a: _shard2d(k, a)
            back = lambda a: a.reshape(shp)
        outs = adamw(f"adamw_{k}", flat(w[k]), [g0, g1], flat(m[k]), flat(v[k]))
        grad[k], delta[k], new_m[k], new_v[k] = [back(a) for a in outs]
    flat2 = lambda d: [d[k].reshape(-1, d[k].shape[-1]) for k in SMALL]
    d_, m_, v_ = adamw_small("adamw_small", flat2(w), flat2(small_g), flat2(m), flat2(v))
    for k, dd, mm, vv in zip(SMALL, d_, m_, v_):
        shp = w[k].shape
        grad[k], delta[k], new_m[k], new_v[k] = small_g[k], dd.reshape(shp), mm.reshape(shp), vv.reshape(shp)

    return (loss, grad_x[None], *[grad[k] for k in WEIGHTS], *[delta[k] for k in WEIGHTS],
            *[new_m[k] for k in WEIGHTS], *[new_v[k] for k in WEIGHTS])
```

```python
import functools

import jax
import jax.numpy as jnp
import numpy as np
from jax import lax
from jax.experimental import pallas as pl
from jax.experimental.pallas import tpu as pltpu

F32 = jnp.float32
MXU_DTYPE = jnp.bfloat16
ACT_DTYPE = jnp.bfloat16
VMEM_LIMIT_BYTES = 48 * 1024 * 1024
EPS = 1e-6
NEG = -1e30

SSD_HEADDIM = 64
SSD_GROUPS = 8
SSD_STATE = 128
SSD_CHUNK = 128
SSD_GROUPS_PER_STEP = 8
SSD_CONV = 5
GRID_W = 64
POOL_WINDOWS = (2, 4, 8, 16)
ROW_TILE = 256
DT_PAD = 512

ADAM_LR = 0.001
ADAM_B1 = 0.9
ADAM_B2 = 0.999
ADAM_EPS = 1e-08
ADAM_WD = 0.01
ADAM_STEP = 10

MESH = pl.DeviceIdType.MESH


def _pcall(body, **kw):
    return pl.pallas_call(body, **kw)


def _params(*sem):
    return pltpu.CompilerParams(dimension_semantics=tuple(sem), vmem_limit_bytes=VMEM_LIMIT_BYTES)


def _pick_tile(n, cands):
    for t in cands:
        if n % t == 0:
            return t
    return n


PLANE = ((1, 0, 0), (0, 1, 0), (1, 1, 0))
PAIR = ((0, 0, 1),)
EVERYONE = tuple((a, b, d) for a in (0, 1) for b in (0, 1) for d in (0, 1) if a + b + d)
HBM = pl.BlockSpec(memory_space=pl.ANY)


def _position():
    return lax.axis_index("x"), lax.axis_index("y"), lax.axis_index("c")


def _flip(pos, rel):
    return tuple(1 - p if r else p for p, r in zip(pos, rel))


def _chip_index(pos):
    return 2 * pos[0] + pos[1]


def _device_index(pos):
    return 4 * pos[0] + 2 * pos[1] + pos[2]


class Exchange:
    def __init__(self, copies, n_remote, n_local, operands, out_shapes, aliases=None):
        self.copies, self.n_remote, self.n_local = copies, n_remote, n_local
        self.operands, self.out_shapes, self.aliases = list(operands), list(out_shapes), dict(aliases or {})

    def scratch(self):
        return [pltpu.SemaphoreType.DMA((max(self.n_remote, 1),)), pltpu.SemaphoreType.DMA((max(self.n_remote, 1),)),
                pltpu.SemaphoreType.DMA((max(self.n_local, 1),))]

    def descriptors(self, ins, outs, sems):
        send_sems, recv_sems, local_sems = sems
        local, remote = self.copies(ins, outs, _position())
        assert len(local) == self.n_local and len(remote) == self.n_remote
        cps = [pltpu.make_async_copy(src, dst, local_sems.at[k]) for k, (src, dst) in enumerate(local)]
        cps += [pltpu.make_async_remote_copy(src_ref=src, dst_ref=dst, send_sem=send_sems.at[k], recv_sem=recv_sems.at[k],
                                             device_id=peer, device_id_type=MESH) for k, (src, dst, peer) in enumerate(remote)]
        return cps


def combine(a, b):
    na, nao = len(a.operands), len(a.out_shapes)

    def copies(ins, outs, pos):
        la, ra = a.copies(ins[:na], outs[:nao], pos)
        lb, rb = b.copies(ins[na:], outs[nao:], pos)
        return la + lb, ra + rb

    aliases = dict(a.aliases)
    aliases.update({na + k: nao + v for k, v in b.aliases.items()})
    return Exchange(copies, a.n_remote + b.n_remote, a.n_local + b.n_local, a.operands + b.operands,
                    a.out_shapes + b.out_shapes, aliases)


class LazyDict(dict):
    def __getitem__(self, key):
        v = dict.__getitem__(self, key)
        if callable(v):
            v = v()
            dict.__setitem__(self, key, v)
        return v


def comm_call(name, ex):
    n_in, n_out = len(ex.operands), len(ex.out_shapes)

    def body(*refs):
        cps = ex.descriptors(refs[:n_in], refs[n_in:n_in + n_out], refs[n_in + n_out:])
        for cp in cps:
            cp.start()
        for cp in cps:
            cp.wait()

    return _pcall(
        body, name=name, out_shape=ex.out_shapes, in_specs=[HBM] * n_in, out_specs=[HBM] * n_out,
        scratch_shapes=ex.scratch(), input_output_aliases=ex.aliases,
        compiler_params=pltpu.CompilerParams(has_side_effects=True),
    )(*ex.operands)


def hosted_call(body, ex, operands, *, name, out_shape, grid, in_specs, out_specs, scratch_shapes=()):
    n_in, n_out, n_scr = len(operands), len(out_shape), len(scratch_shapes)
    sem = ("arbitrary",) * len(grid)
    if ex is None:
        res = _pcall(body, name=name, out_shape=list(out_shape), grid=grid, in_specs=list(in_specs),
                     out_specs=list(out_specs), scratch_shapes=list(scratch_shapes), compiler_params=_params(*sem))(*operands)
        return res, []
    x_in, x_out = len(ex.operands), len(ex.out_shapes)

    def wrapped(*refs):
        o = 0
        ins = refs[o:o + n_in]; o += n_in
        xins = refs[o:o + x_in]; o += x_in
        outs = refs[o:o + n_out]; o += n_out
        xouts = refs[o:o + x_out]; o += x_out
        scr = refs[o:o + n_scr]; o += n_scr
        sems = refs[o:]
        first = last = None
        for a, n in enumerate(grid):
            i = pl.program_id(a)
            first = (i == 0) if first is None else first & (i == 0)
            last = (i == n - 1) if last is None else last & (i == n - 1)

        @pl.when(first)
        def _():
            for cp in ex.descriptors(xins, xouts, sems):
                cp.start()

        body(*ins, *outs, *scr)

        @pl.when(last)
        def _():
            for cp in ex.descriptors(xins, xouts, sems):
                cp.wait()

    aliases = {n_in + k: n_out + v for k, v in ex.aliases.items()}
    res = _pcall(
        wrapped, name=name, out_shape=list(out_shape) + ex.out_shapes, grid=grid,
        in_specs=list(in_specs) + [HBM] * x_in, out_specs=list(out_specs) + [HBM] * x_out,
        scratch_shapes=list(scratch_shapes) + ex.scratch(), input_output_aliases=aliases,
        compiler_params=pltpu.CompilerParams(dimension_semantics=sem, vmem_limit_bytes=VMEM_LIMIT_BYTES,
                                             has_side_effects=True),
    )(*operands, *ex.operands)
    return res[:n_out], res[n_out:]


def _dot(a, b, dims):
    return lax.dot_general(a.astype(MXU_DTYPE), b.astype(MXU_DTYPE), (dims, ((), ())), preferred_element_type=F32)


_NN = ((1,), (0,))
_NT = ((1,), (1,))
_TN = ((0,), (0,))


@jax.custom_vjp
def _mm(a, b):
    return _dot(a, b, _NN)


def _mm_fwd(a, b):
    return _mm(a, b), (a, b)


def _mm_bwd(res, g):
    a, b = res
    return _dot(g, b, _NT).astype(a.dtype), _dot(a, g, _TN).astype(b.dtype)


_mm.defvjp(_mm_fwd, _mm_bwd)


@jax.custom_vjp
def _mm_nt(a, b):
    return _dot(a, b, _NT)


def _mm_nt_fwd(a, b):
    return _mm_nt(a, b), (a, b)


def _mm_nt_bwd(res, g):
    a, b = res
    return _dot(g, b, _NN).astype(a.dtype), _dot(g, a, _TN).astype(b.dtype)


_mm_nt.defvjp(_mm_nt_fwd, _mm_nt_bwd)


@jax.custom_vjp
def _mm_tn(a, b):
    return _dot(a, b, _TN)


def _mm_tn_fwd(a, b):
    return _mm_tn(a, b), (a, b)


def _mm_tn_bwd(res, g):
    a, b = res
    return _dot(b, g, _NT).astype(a.dtype), _dot(a, g, _NN).astype(b.dtype)


_mm_tn.defvjp(_mm_tn_fwd, _mm_tn_bwd)


def _dot_exact(m01, v):
    m = m01.astype(jnp.bfloat16)
    hi = v.astype(jnp.bfloat16)
    r1 = v - hi.astype(F32)
    mid = r1.astype(jnp.bfloat16)
    lo = (r1 - mid.astype(F32)).astype(jnp.bfloat16)
    out = jnp.dot(m, hi, preferred_element_type=F32)
    out = out + jnp.dot(m, mid, preferred_element_type=F32)
    return out + jnp.dot(m, lo, preferred_element_type=F32)


@jax.custom_vjp
def _lin01(m, mt, v):
    return _dot_exact(m, v)


def _lin01_fwd(m, mt, v):
    return _dot_exact(m, v), (m, mt)


def _lin01_bwd(res, g):
    m, mt = res
    return jnp.zeros_like(m), jnp.zeros_like(mt), _dot_exact(mt, g)


_lin01.defvjp(_lin01_fwd, _lin01_bwd)


MATMUL_VMEM_BUDGET = VMEM_LIMIT_BYTES * 5 // 6


def _mm_tiles(m, n, k_bytes_a, k_bytes_b, out_bytes, cands_m, cands_n):
    best = None
    for tm in cands_m:
        if m % tm:
            continue
        for tn in cands_n:
            if n % tn:
                continue
            need = 2 * (tm * k_bytes_a + tn * k_bytes_b + tm * tn * out_bytes)
            if need <= MATMUL_VMEM_BUDGET and (best is None or tm * tn > best[0] * best[1]):
                best = (tm, tn)
    assert best is not None, (m, n)
    return best


_ROW_CANDS = (4352, 2176, 1088, 768, 544, 512, 272, 256, 128, 16)
_COL_CANDS = (2816, 2048, 1408, 1024, 512, 256, 128)


def _one(res, xres, ex):
    return res[0] if ex is None else (res[0], xres)


def _block_cands(c):
    return (c,) + tuple(t for t in (512, 256, 128) if c % t == 0)


def matmul_nn(name, a, b, out_dtype=F32, ex=None, col0=0, ncols=None):
    M, K = a.shape
    if b.ndim == 3:
        nb, _, C = b.shape
        N, cands = nb * C, _block_cands(C)
    else:
        N, cands = (b.shape[1] - col0 if ncols is None else ncols), (512, 256, 128)
    tm, tn = _mm_tiles(M, N, K * a.dtype.itemsize, K * b.dtype.itemsize, jnp.dtype(out_dtype).itemsize,
                       _ROW_CANDS, cands)
    if b.ndim == 3:
        per = C // tn
        b_spec = pl.BlockSpec((None, K, tn), lambda j, i: (j // per, 0, j % per))
    else:
        assert col0 % tn == 0
        first = col0 // tn
        b_spec = pl.BlockSpec((K, tn), lambda j, i: (0, first + j))

    def body(a_ref, b_ref, o_ref):
        o_ref[...] = _dot(a_ref[...], b_ref[...], _NN).astype(o_ref.dtype)

    res, xres = hosted_call(
        body, ex, [a, b], name=name, out_shape=[jax.ShapeDtypeStruct((M, N), out_dtype)], grid=(N // tn, M // tm),
        in_specs=[pl.BlockSpec((tm, K), lambda j, i: (i, 0)), b_spec],
        out_specs=[pl.BlockSpec((tm, tn), lambda j, i: (i, j))])
    return _one(res, xres, ex)


def matmul_nt(name, g, b, out_dtype=F32, ex=None, offsets=None):
    pieces = list(g) if isinstance(g, (list, tuple)) else [g]
    offsets = list(offsets) if offsets is not None else [0]
    M = pieces[0].shape[0]
    if b.ndim == 3:
        nb, K, C = b.shape
        N = nb * C
        assert len(pieces) == 1
    else:
        K, N = b.shape
    g_bytes = sum(p.shape[1] * p.dtype.itemsize for p in pieces)
    tm, tk = _mm_tiles(M, K, g_bytes, N * b.dtype.itemsize, jnp.dtype(out_dtype).itemsize, _ROW_CANDS, _COL_CANDS)

    def body(*refs):
        b_ref, o_ref = refs[-2:]
        acc = None
        if b.ndim == 3:
            parts = [_dot(refs[0][:, k * C:(k + 1) * C], b_ref[k], _NT) for k in range(nb)]
        else:
            parts = [_dot(g_ref[...], b_ref[:, off:off + g_ref.shape[1]], _NT) for g_ref, off in zip(refs[:-2], offsets)]
        for part in parts:
            acc = part if acc is None else acc + part
        o_ref[...] = acc.astype(o_ref.dtype)

    b_spec = (pl.BlockSpec((nb, tk, C), lambda j, i: (0, j, 0)) if b.ndim == 3
              else pl.BlockSpec((tk, N), lambda j, i: (j, 0)))
    res, xres = hosted_call(
        body, ex, pieces + [b], name=name, out_shape=[jax.ShapeDtypeStruct((M, K), out_dtype)], grid=(K // tk, M // tm),
        in_specs=[pl.BlockSpec((tm, p.shape[1]), lambda j, i: (i, 0)) for p in pieces] + [b_spec],
        out_specs=[pl.BlockSpec((tm, tk), lambda j, i: (i, j))])
    return _one(res, xres, ex)


def matmul_tn(name, a, g, ex=None, blocks=1):
    M, K = a.shape
    N = g.shape[1]
    C = N // blocks
    tk, tn = _mm_tiles(K, N, M * a.dtype.itemsize, M * g.dtype.itemsize, 4, (512, 256, 128),
                       (512, 256, 128) if blocks == 1 else _block_cands(C))

    def body(a_ref, g_ref, o_ref):
        o_ref[...] = _dot(a_ref[...], g_ref[...], _TN)

    if blocks == 1:
        out_shape, out_spec = jax.ShapeDtypeStruct((K, N), F32), pl.BlockSpec((tk, tn), lambda i, j: (i, j))
    else:
        per = C // tn
        out_shape = jax.ShapeDtypeStruct((blocks, K, C), F32)
        out_spec = pl.BlockSpec((None, tk, tn), lambda i, j: (j // per, i, j % per))
    res, xres = hosted_call(
        body, ex, [a, g], name=name, out_shape=[out_shape], grid=(K // tk, N // tn),
        in_specs=[pl.BlockSpec((M, tk), lambda i, j: (0, i)), pl.BlockSpec((M, tn), lambda i, j: (0, j))],
        out_specs=[out_spec])
    return _one(res, xres, ex)


class Arg:
    def __init__(self, arr, block, imap, kind):
        self.arr, self.block, self.imap, self.kind = arr, block, imap, kind


class Rows:
    def __init__(self, nt, nct, tm, ncol=1):
        self.nt, self.nct, self.tm, self.ncol = nt, nct, tm, ncol

    def seg(self, i):
        return jnp.where(i >= self.nct, 1, 0)

    def spec(self, block, imap):
        return pl.BlockSpec(block, lambda j, i: imap(j, i, self.seg(i)))

    def row(self, arr, width, cb0=0, follow=False, roff=0, stride=1):
        f = stride if follow else 0
        return Arg(arr, (self.tm, width), lambda j, i, s: (i + roff, cb0 + f * j), "row")

    def vec(self, arr, follow=False, kind="acc"):
        w = arr.shape[1] // (self.ncol if follow else 1)
        f = 1 if follow else 0
        return Arg(arr, (1, w), lambda j, i, s: (0, f * j), kind)

    def segvec(self, arr, kind="seg"):
        return Arg(arr, (None, 1, arr.shape[2]), lambda j, i, s: (s, 0, 0), kind)


def _load(ref):
    return ref[...].astype(F32) if ref.dtype != F32 else ref[...]


def stage_fwd(name, f, rows, args, outs):
    n_in = len(args)

    def body(*refs):
        vals = [_load(r) for r in refs[:n_in]]
        res = f(*vals)
        for r, v in zip(refs[n_in:], res):
            r[...] = v.astype(r.dtype)

    T = rows.nt * rows.tm
    out_shape = [jax.ShapeDtypeStruct((T, w * (rows.ncol if fo else 1)), dt) for w, dt, fo in outs]
    out_specs = [pl.BlockSpec((rows.tm, w), (lambda j, i, fo=fo: (i, j if fo else 0))) for w, dt, fo in outs]
    res = _pcall(
        body, name=name, out_shape=out_shape, grid=(rows.ncol, rows.nt),
        in_specs=[rows.spec(a.block, a.imap) for a in args], out_specs=out_specs,
        compiler_params=_params("parallel", "parallel"),
    )(*[a.arr for a in args])
    return res


def stage_bwd(name, f, rows, args, cots, row_dtypes, ex=None, primal=()):
    n_in, n_ct = len(args), len(cots)
    diff = [k for k, a in enumerate(args) if a.kind != "const"]
    row_dt = {}
    for k in diff:
        if args[k].kind == "row":
            row_dt[k] = row_dtypes[len(row_dt)]

    def body(*refs):
        i = pl.program_id(1)
        vals = [_load(r) for r in refs[:n_in]]
        cts = tuple(_load(r) for r in refs[n_in:n_in + n_ct])
        outs = refs[n_in + n_ct:]

        def g(*dv):
            full = list(vals)
            for k, v in zip(diff, dv):
                full[k] = v
            return tuple(f(*full))

        prim, vjp = jax.vjp(g, *[vals[k] for k in diff])
        grads = vjp(cts)
        for o, v in zip(outs[len(diff):], prim):
            o[...] = v.astype(o.dtype)
        for k, o, gr in zip(diff, outs, grads):
            kind = args[k].kind
            if kind == "row":
                o[...] = gr.astype(o.dtype)
            else:
                first = (i == 0) | (i == rows.nct) if kind == "seg" else (i == 0)

                @pl.when(first)
                def _():
                    o[...] = gr.astype(o.dtype)

                @pl.when(jnp.logical_not(first))
                def _():
                    o[...] += gr.astype(o.dtype)

    T = rows.nt * rows.tm
    out_shape, out_specs = [], []
    for k in diff:
        a = args[k]
        if a.kind == "row":
            out_shape.append(jax.ShapeDtypeStruct((T, a.block[1] * (rows.ncol if _follows(a) else 1)), row_dt[k]))
            fo = _follows(a)
            out_specs.append(pl.BlockSpec(a.block, (lambda j, i, fo=fo: (i, j if fo else 0))))
        else:
            out_shape.append(jax.ShapeDtypeStruct(a.arr.shape, F32))
            out_specs.append(rows.spec(a.block, a.imap))
    for w, dt in primal:
        out_shape.append(jax.ShapeDtypeStruct((T, w), dt))
        out_specs.append(pl.BlockSpec((rows.tm, w), lambda j, i: (i, 0)))
    res, xres = hosted_call(
        body, ex, [a.arr for a in list(args) + list(cots)], name=name, out_shape=out_shape, grid=(rows.ncol, rows.nt),
        in_specs=[rows.spec(a.block, a.imap) for a in list(args) + list(cots)], out_specs=out_specs)
    return res if ex is None else (res, xres)


def _follows(a):
    return a.imap(1, 0, 0)[-1] != a.imap(0, 0, 0)[-1]


def _rms(x):
    return x * lax.rsqrt(jnp.mean(x * x, axis=-1, keepdims=True) + EPS)


def f_norm_mod(x, g, sh, sc):
    return ((_rms(x) * g) * (1.0 + sc) + sh,)


def f_resid_norm_mod(x, mo, ga, g, sh, sc):
    x1 = x + ga * mo
    return x1, (_rms(x1) * g) * (1.0 + sc) + sh


def f_resid(x, dn, ga):
    return (x + ga * dn,)


def f_silu(x):
    return (x * jax.nn.sigmoid(x),)


def f_bias(x, b):
    return (x + b,)


def f_ssd_gate(y0, y1, xs, z, dskip, nw):
    y = y0 + y1 + dskip * xs
    return (_rms(y * (z * jax.nn.sigmoid(z))) * nw,)


def f_pool(u, pmat, pmat_t, inv_cnt, pw, scale):
    pm = _lin01(pmat, pmat_t, u) * inv_cnt - u
    return (_mm(pm, pw) * scale,)


def f_merge(o_ssd, o_pool, gl_ssd, gl_pool):
    return (jax.nn.sigmoid(gl_ssd) * o_ssd + jax.nn.sigmoid(gl_pool) * o_pool,)


def _column_splitter(n):
    @jax.custom_vjp
    def split(x):
        w = x.shape[1] // n
        return tuple(x[:, k * w:(k + 1) * w] for k in range(n))

    def fwd(x):
        return split(x), None

    def bwd(_, g):
        return (jnp.concatenate(g, axis=1),)

    split.defvjp(fwd, bwd)
    return split


_halve_cols = _column_splitter(2)
_quarter_cols = _column_splitter(len(POOL_WINDOWS))


def f_swiglu(gu):
    a, b = _halve_cols(gu)
    return ((a * jax.nn.sigmoid(a)) * b,)


def f_pool_all(u, pmat, pmat_t, inv_cnt, scale, *pws):
    outs = [f_pool(part, pmat[k], pmat_t[k], inv_cnt[k], pws[k], 1.0)[0] for k, part in enumerate(_quarter_cols(u))]
    return (jnp.concatenate(outs, axis=1) * scale,)


def f_loss_resid(x1, dn, ga, tgt, g):
    err = _rms(x1 + ga * dn) * g - tgt
    return (0.5 * jnp.mean(err * err, axis=-1, keepdims=True),)


CONV_TILE = 128


CONV_GAP = 8


def _gapped(v, n_ctx):
    z = jnp.zeros((CONV_GAP, v.shape[1]), v.dtype)
    return jnp.concatenate([v[:n_ctx], z, v[n_ctx:], z], axis=0)


def _ungapped(v, n_ctx):
    return jnp.concatenate([v[:n_ctx], v[n_ctx + CONV_GAP:v.shape[0] - CONV_GAP]], axis=0)


def _shift_rows(v, j):
    return v if j == 0 else pltpu.roll(v, (-j) % v.shape[0], 0)


def conv_fwd(name, proj, conv_w, conv_b, n_ctx, width, ex=None):
    T = proj.shape[0]
    half = SSD_CONV // 2

    def body(u_ref, w_ref, b_ref, o_ref):
        u = _gapped(u_ref[...].astype(F32), n_ctx)
        pre = jnp.broadcast_to(b_ref[...], u.shape)
        for k in range(SSD_CONV):
            pre = pre + w_ref[k:k + 1, :] * _shift_rows(u, k - half)
        o_ref[...] = _ungapped(pre * jax.nn.sigmoid(pre), n_ctx)

    col = lambda t: (0, t)
    res, xres = hosted_call(
        body, ex, [proj, conv_w, conv_b], name=name, out_shape=[jax.ShapeDtypeStruct((T, width), F32)],
        grid=(width // CONV_TILE,),
        in_specs=[pl.BlockSpec((T, CONV_TILE), col), pl.BlockSpec((SSD_CONV, CONV_TILE), col),
                  pl.BlockSpec((1, CONV_TILE), col)],
        out_specs=[pl.BlockSpec((T, CONV_TILE), col)])
    return res[0], xres


def conv_bwd(name, proj, conv_w, conv_b, d_act2, d_skip, n_ctx, width, ex=None):
    T = proj.shape[0]
    half = SSD_CONV // 2

    def body(u_ref, w_ref, b_ref, c0_ref, c1_ref, cs_ref, du_ref, dw_ref, db_ref):
        t = pl.program_id(0)
        u = _gapped(u_ref[...].astype(F32), n_ctx)
        pre = jnp.broadcast_to(b_ref[...], u.shape)
        for k in range(SSD_CONV):
            pre = pre + w_ref[k:k + 1, :] * _shift_rows(u, k - half)
        sg = jax.nn.sigmoid(pre)
        ct = c0_ref[...].astype(F32) + c1_ref[...].astype(F32) + jnp.where(t % 4 < 2, cs_ref[...].astype(F32), 0.0)
        dpre = _gapped(ct, n_ctx) * (sg * (1.0 + pre * (1.0 - sg)))
        du = jnp.zeros_like(u)
        for k in range(SSD_CONV):
            du = du + w_ref[k:k + 1, :] * _shift_rows(dpre, half - k)
            dw_ref[k:k + 1, :] = jnp.sum(dpre * _shift_rows(u, k - half), axis=0, keepdims=True)
        du_ref[...] = _ungapped(du, n_ctx).astype(du_ref.dtype)
        db_ref[...] = jnp.sum(dpre, axis=0, keepdims=True)

    col = lambda t: (0, t)
    skip_col = lambda t: (0, (t // 4) * 2 + jnp.minimum(t % 4, 1))
    res, xres = hosted_call(
        body, ex, [proj, conv_w, conv_b, d_act2[0], d_act2[1], d_skip], name=name,
        out_shape=[jax.ShapeDtypeStruct((T, width), ACT_DTYPE), jax.ShapeDtypeStruct((SSD_CONV, width), F32),
                   jax.ShapeDtypeStruct((1, width), F32)],
        grid=(width // CONV_TILE,),
        in_specs=[pl.BlockSpec((T, CONV_TILE), col), pl.BlockSpec((SSD_CONV, CONV_TILE), col),
                  pl.BlockSpec((1, CONV_TILE), col), pl.BlockSpec((T, CONV_TILE), col),
                  pl.BlockSpec((T, CONV_TILE), col), pl.BlockSpec((T, CONV_TILE), skip_col)],
        out_specs=[pl.BlockSpec((T, CONV_TILE), col), pl.BlockSpec((SSD_CONV, CONV_TILE), col),
                   pl.BlockSpec((1, CONV_TILE), col)])
    return res[0], res[1], res[2], xres


@jax.custom_vjp
def _cumsum_mat(tri, tri_t, a):
    return jnp.dot(tri, a, precision=lax.Precision.HIGHEST, preferred_element_type=F32)


def _cumsum_fwd(tri, tri_t, a):
    return _cumsum_mat(tri, tri_t, a), (tri, tri_t)


def _cumsum_bwd(res, g):
    tri, tri_t = res
    return (jnp.zeros_like(tri), jnp.zeros_like(tri_t),
            jnp.dot(tri_t, g, precision=lax.Precision.HIGHEST, preferred_element_type=F32))


_cumsum_mat.defvjp(_cumsum_fwd, _cumsum_bwd)


def _ssd_dt(dtraw, dt_bias, a_log, tri, tri_t):
    dt_all = jax.nn.softplus(dtraw + dt_bias)
    a_all = dt_all * (-jnp.exp(a_log))
    return dt_all, a_all, _cumsum_mat(tri, tri_t, a_all)


def _ssd_chunk(xs, bm, cm, dt_all, a_all, s_all, s_in, mask, idx0):
    (xs,), (s_in,) = xs, s_in
    Q = xs.shape[0]
    hpg = xs.shape[1] // SSD_HEADDIM
    lane = lax.broadcasted_iota(jnp.int32, dt_all.shape, 1)
    head = lax.broadcasted_iota(jnp.int32, xs.shape, 1) // SSD_HEADDIM
    head1 = lax.broadcasted_iota(jnp.int32, (1, xs.shape[1]), 1) // SSD_HEADDIM

    def pick(v, r):
        return jnp.sum(jnp.where(lane == idx0 + r, v, 0.0), axis=1, keepdims=True)

    def expand(cols, hd):
        out = cols[hpg - 1]
        for r in range(hpg - 2, -1, -1):
            out = jnp.where(hd == r, cols[r], out)
        return out

    def spread(*cols):
        return expand([jnp.broadcast_to(c, xs.shape) for c in cols], head)

    dt_r = [pick(dt_all, r) for r in range(hpg)]
    s_r = [pick(s_all, r) for r in range(hpg)]
    stot_r = [jnp.sum(jnp.where(lane == idx0 + r, a_all, 0.0), keepdims=True).reshape(1, 1) for r in range(hpg)]

    xd = xs * spread(*dt_r)
    cb = _mm_nt(cm, bm)
    weights, stacked = [], []
    for r in range(hpg):
        sm = jnp.broadcast_to(s_r[r], (Q, Q))
        weights.append(cb * jnp.exp(jnp.where(mask, sm - sm.T, NEG)))
        stacked.append(jnp.where(head == r, xd, 0.0))
    y = spread(*[jnp.exp(c) for c in s_r]) * _mm(cm, s_in)
    y = y + _mm(jnp.concatenate(weights, axis=1), jnp.concatenate(stacked, axis=0))
    to_end = spread(*[jnp.exp(t - c) for t, c in zip(stot_r, s_r)])
    carry = expand([jnp.broadcast_to(jnp.exp(t), (1, xs.shape[1])) for t in stot_r], head1)
    s_out = carry * s_in + _mm_tn(bm, xd * to_end)
    return [y], [s_out]


def _scan_consts():
    q = SSD_CHUNK
    i = np.arange(q)[:, None]
    j = np.arange(q)[None, :]
    fwd = (j <= i).astype(np.float32)
    bwd = (j >= i).astype(np.float32)
    tri = np.stack([fwd, bwd])
    return jnp.asarray(tri), jnp.asarray(np.stack([fwd.T, bwd.T]))


def _chunk_of(d, k, ncc, nc):
    rev = jnp.where(k < ncc, ncc - 1 - k, nc - 1 + ncc - k)
    return jnp.where(d == 0, k, rev)


def ssd_fwd(name, xbc, dtraw, dt_bias, a_log, n_ctx, ex=None):
    T = xbc.shape[0]
    q, G = SSD_CHUNK, SSD_GROUPS
    nc, ncc = T // q, n_ctx // q
    gw = xbc.shape[1] // G
    xw = gw - 2 * SSD_STATE
    hpg = xw // SSD_HEADDIM
    nh = G * hpg
    tri, tri_t = _scan_consts()

    gs = SSD_GROUPS_PER_STEP

    def body(x0_ref, x1_ref, dt0_ref, dt1_ref, bias_ref, alog_ref, tri_ref, trit_ref, y0_ref, y1_ref, sin_ref, state):
        gb, k = pl.program_id(0), pl.program_id(1)

        @pl.when(k == 0)
        def _():
            state[...] = jnp.zeros_like(state)

        for d, (x_ref, dt_ref, y_ref) in enumerate(((x0_ref, dt0_ref, y0_ref), (x1_ref, dt1_ref, y1_ref))):
            tri_v = tri_ref[d]
            dt_all, a_all, s_all = _ssd_dt(dt_ref[...], bias_ref[...], alog_ref[...], tri_v, trit_ref[d])
            for j in range(gs):
                o = j * gw
                sin_ref[d, j] = state[d, j]
                (y,), (s_out,) = _ssd_chunk(
                    [x_ref[:, o:o + xw]], x_ref[:, o + xw:o + xw + SSD_STATE], x_ref[:, o + xw + SSD_STATE:o + gw],
                    dt_all, a_all, s_all, [state[d, j]], tri_v > 0.5, d * nh + (gb * gs + j) * hpg)
                y_ref[:, j * xw:(j + 1) * xw] = y.astype(y_ref.dtype)
                state[d, j] = s_out

    ch = lambda d, k: _chunk_of(d, k, ncc, nc)
    y_shape = jax.ShapeDtypeStruct((T, G * xw), ACT_DTYPE)
    res, xres = hosted_call(
        body, ex, [xbc, xbc, dtraw, dtraw, dt_bias, a_log, tri, tri_t], name=name,
        out_shape=[y_shape, y_shape, jax.ShapeDtypeStruct((2, nc, G, SSD_STATE, xw), F32)],
        grid=(G // gs, nc),
        in_specs=[pl.BlockSpec((q, gs * gw), lambda g, k: (ch(0, k), g)),
                  pl.BlockSpec((q, gs * gw), lambda g, k: (ch(1, k), g)),
                  pl.BlockSpec((q, 128), lambda g, k: (ch(0, k), 0)),
                  pl.BlockSpec((q, 128), lambda g, k: (ch(1, k), 0)),
                  pl.BlockSpec((1, 128), lambda g, k: (0, 0)),
                  pl.BlockSpec((1, 128), lambda g, k: (0, 0)),
                  pl.BlockSpec((2, q, q), lambda g, k: (0, 0, 0)),
                  pl.BlockSpec((2, q, q), lambda g, k: (0, 0, 0))],
        out_specs=[pl.BlockSpec((q, gs * xw), lambda g, k: (ch(0, k), g)),
                   pl.BlockSpec((q, gs * xw), lambda g, k: (ch(1, k), g)),
                   pl.BlockSpec((2, None, gs, SSD_STATE, xw), lambda g, k: (0, k, g, 0, 0))],
        scratch_shapes=[pltpu.VMEM((2, gs, SSD_STATE, xw), F32)])
    return res[0], res[1], res[2], xres


def ssd_bwd(name, xbc, dtraw, dt_bias, a_log, states, dy, n_ctx, ex=None):
    T = xbc.shape[0]
    q, G = SSD_CHUNK, SSD_GROUPS
    nc, ncc = T // q, n_ctx // q
    gw = xbc.shape[1] // G
    xw = gw - 2 * SSD_STATE
    hpg = xw // SSD_HEADDIM
    nh = G * hpg
    tri, tri_t = _scan_consts()

    gs = SSD_GROUPS_PER_STEP

    def body(x0_ref, x1_ref, dt0_ref, dt1_ref, bias_ref, alog_ref, tri_ref, trit_ref, sin_ref, dy0_ref, dy1_ref,
             dx0_ref, dx1_ref, ddt_ref, dbias_ref, dalog_ref, dstate):
        gb, k = pl.program_id(0), pl.program_id(1)

        @pl.when((gb == 0) & (k == 0))
        def _():
            ddt_ref[...] = jnp.zeros_like(ddt_ref)
            dbias_ref[...] = jnp.zeros_like(dbias_ref)
            dalog_ref[...] = jnp.zeros_like(dalog_ref)

        @pl.when(k == 0)
        def _():
            dstate[...] = jnp.zeros_like(dstate)

        tris = [(tri_ref[d], trit_ref[d]) for d in range(2)]
        per = 4

        def fn(bias, alog, dtraw0, dtraw1, *per_group):
            ys, s_outs = [], []
            for d, dtraw in enumerate((dtraw0, dtraw1)):
                tri_v, trit_v = tris[d]
                dt_all, a_all, s_all = _ssd_dt(dtraw, bias, alog, tri_v, trit_v)
                for j in range(gs):
                    xs, bm, cm, s_in = per_group[per * (d * gs + j):per * (d * gs + j + 1)]
                    y, s_out = _ssd_chunk([xs], bm, cm, dt_all, a_all, s_all, [s_in], tri_v > 0.5,
                                          d * nh + (gb * gs + j) * hpg)
                    ys += y
                    s_outs += s_out
            return ys, s_outs

        per_group, dys, dss = [], [], []
        for d, (x_ref, dy_ref) in enumerate(((x0_ref, dy0_ref), (x1_ref, dy1_ref))):
            for j in range(gs):
                o = j * gw
                per_group += [x_ref[:, o:o + xw], x_ref[:, o + xw:o + xw + SSD_STATE], x_ref[:, o + xw + SSD_STATE:o + gw],
                              sin_ref[d, j]]
                dys.append(dy_ref[:, j * xw:(j + 1) * xw].astype(F32))
                dss.append(dstate[d, j])
        _, vjp = jax.vjp(fn, bias_ref[...], alog_ref[...], dt0_ref[...], dt1_ref[...], *per_group)
        cts = vjp((dys, dss))
        dbias, dalog, ddt0, ddt1 = cts[:4]
        for d, dx_ref in enumerate((dx0_ref, dx1_ref)):
            for j in range(gs):
                o = j * gw
                dxs, dbm, dcm, ds_in = cts[4 + per * (d * gs + j):4 + per * (d * gs + j + 1)]
                dx_ref[:, o:o + xw] = dxs.astype(dx_ref.dtype)
                dx_ref[:, o + xw:o + xw + SSD_STATE] = dbm.astype(dx_ref.dtype)
                dx_ref[:, o + xw + SSD_STATE:o + gw] = dcm.astype(dx_ref.dtype)
                dstate[d, j] = ds_in
        for d, ddt in enumerate((ddt0, ddt1)):
            row0 = pl.multiple_of(_chunk_of(d, nc - 1 - k, ncc, nc) * q, q)
            ddt_ref[pl.ds(row0, q), :] += ddt
        dbias_ref[...] += dbias
        dalog_ref[...] += dalog

    ch = lambda d, k: _chunk_of(d, nc - 1 - k, ncc, nc)
    dx_shape = jax.ShapeDtypeStruct((T, G * gw), ACT_DTYPE)
    res, xres = hosted_call(
        body, ex, [xbc, xbc, dtraw, dtraw, dt_bias, a_log, tri, tri_t, states, dy, dy], name=name,
        out_shape=[dx_shape, dx_shape, jax.ShapeDtypeStruct((T, 128), F32),
                   jax.ShapeDtypeStruct((1, 128), F32), jax.ShapeDtypeStruct((1, 128), F32)],
        grid=(G // gs, nc),
        in_specs=[pl.BlockSpec((q, gs * gw), lambda g, k: (ch(0, k), g)),
                  pl.BlockSpec((q, gs * gw), lambda g, k: (ch(1, k), g)),
                  pl.BlockSpec((q, 128), lambda g, k: (ch(0, k), 0)),
                  pl.BlockSpec((q, 128), lambda g, k: (ch(1, k), 0)),
                  pl.BlockSpec((1, 128), lambda g, k: (0, 0)),
                  pl.BlockSpec((1, 128), lambda g, k: (0, 0)),
                  pl.BlockSpec((2, q, q), lambda g, k: (0, 0, 0)),
                  pl.BlockSpec((2, q, q), lambda g, k: (0, 0, 0)),
                  pl.BlockSpec((2, None, gs, SSD_STATE, xw), lambda g, k: (0, nc - 1 - k, g, 0, 0)),
                  pl.BlockSpec((q, gs * xw), lambda g, k: (ch(0, k), g)),
                  pl.BlockSpec((q, gs * xw), lambda g, k: (ch(1, k), g))],
        out_specs=[pl.BlockSpec((q, gs * gw), lambda g, k: (ch(0, k), g)),
                   pl.BlockSpec((q, gs * gw), lambda g, k: (ch(1, k), g)),
                   pl.BlockSpec((T, 128), lambda g, k: (0, 0)),
                   pl.BlockSpec((1, 128), lambda g, k: (0, 0)),
                   pl.BlockSpec((1, 128), lambda g, k: (0, 0))],
        scratch_shapes=[pltpu.VMEM((2, gs, SSD_STATE, xw), F32)])
    return res[0], res[1], res[2], res[3], res[4], xres


def _perm_xbc(a):
    G = SSD_GROUPS
    n = a.shape[-1]
    gn = G * SSD_STATE
    di = n - 2 * gn
    lead = a.shape[:-1]
    xs = a[..., :di].reshape(lead + (G, di // G))
    bm = a[..., di:di + gn].reshape(lead + (G, SSD_STATE))
    cm = a[..., di + gn:].reshape(lead + (G, SSD_STATE))
    return jnp.concatenate([xs, bm, cm], axis=-1).reshape(lead + (n,))


def _unperm_xbc(a):
    G = SSD_GROUPS
    n = a.shape[-1]
    gn = G * SSD_STATE
    di = n - 2 * gn
    lead = a.shape[:-1]
    r = a.reshape(lead + (G, n // G))
    xw = di // G
    return jnp.concatenate([r[..., :xw].reshape(lead + (di,)), r[..., xw:xw + SSD_STATE].reshape(lead + (gn,)),
                            r[..., xw + SSD_STATE:].reshape(lead + (gn,))], axis=-1)


def _pool_consts(tm, n_ctx):
    assert n_ctx == tm and tm % GRID_W == 0
    mats, cnts = [], []
    for seq in (n_ctx, GRID_W):
        t = np.arange(tm)
        tt = t % seq
        base = t - tt
        ms, cs = [], []
        for k in POOL_WINDOWS:
            lo = np.clip(tt - k // 2, 0, seq) + base
            hi = np.clip(tt + k // 2, 0, seq) + base
            m = ((t[None, :] >= lo[:, None]) & (t[None, :] < hi[:, None])).astype(np.float32)
            ms.append(m)
            cs.append((1.0 / (hi - lo).astype(np.float32))[:, None])
        mats.append(np.stack(ms))
        cnts.append(np.stack(cs))
    m = np.stack(mats)
    return jnp.asarray(m), jnp.asarray(np.swapaxes(m, -1, -2)), jnp.asarray(np.stack(cnts).astype(np.float32))


def _prep_layer_weights(w_ada, b_ada, g_mix, w_in, conv_w, conv_b, dt_bias, a_log, d_skip, ssd_norm_w, w_ssd_out,
                        pool_w, pool_scale, w_pool_out, w_out, g_ffn, w_gate_up, w_down):
    D = w_in.shape[0]
    di = ssd_norm_w.shape[0]
    xbc = conv_w.shape[1]
    nh2 = dt_bias.size
    pw = pool_scale.shape[0]
    o = 0
    wz = w_in[:, o:o + di]; o += di
    wx = w_in[:, o:o + xbc]; o += xbc
    wdt = w_in[:, o:o + nh2]; o += nh2
    wp = w_in[:, o:o + pw]; o += pw
    wg = w_in[:, o:]
    w1 = jnp.concatenate([_perm_xbc(wx), wz, wg, wp, wdt, jnp.zeros((D, DT_PAD - nh2), w_in.dtype)], axis=1)
    pad128 = lambda v: jnp.concatenate([v.reshape(1, -1), jnp.zeros((1, 128 - v.size), F32)], axis=1)
    return dict(
        w_ada=w_ada, b_ada=b_ada.reshape(1, -1), g_mix=g_mix.reshape(1, -1), w1=w1,
        conv_w=_perm_xbc(conv_w), conv_b=_perm_xbc(conv_b.reshape(1, -1)),
        dt_bias=pad128(dt_bias), a_log=pad128(a_log),
        dskip=jnp.repeat(d_skip[0] + d_skip[1], SSD_HEADDIM).reshape(1, -1),
        ssd_norm_w=ssd_norm_w.reshape(1, -1), w_ssd_out=w_ssd_out, pool_w=pool_w,
        pool_scale=pool_scale.reshape(1, -1), w_pool_out=w_pool_out, w_out=w_out, g_ffn=g_ffn.reshape(1, -1),
        w_gate_up=w_gate_up, w_down=w_down)


def _unprep_layer_grads(g, dims):
    di, xbc, nh2, pw = dims
    dxbc, dz, dgs, dgp, dp, ddt = g["w1"]
    r = dxbc.reshape(SSD_GROUPS, xbc // SSD_GROUPS, dxbc.shape[1])
    xw = di // SSD_GROUPS
    parts = [r[:, :xw], r[:, xw:xw + SSD_STATE], r[:, xw + SSD_STATE:]]
    w_in_t = jnp.concatenate([dz] + [p.reshape(-1, dxbc.shape[1]) for p in parts] + [ddt[:nh2], dp, dgs, dgp], axis=0)
    nh = nh2 // 2
    dsk = g["dskip"].reshape(nh, SSD_HEADDIM).sum(axis=1)
    return dict(
        w_ada=g["w_ada"], b_ada=g["b_ada"].reshape(-1), g_mix=g["g_mix"].reshape(-1),
        w_in=w_in_t,
        conv_w=_unperm_xbc(g["conv_w"]), conv_b=_unperm_xbc(g["conv_b"]).reshape(-1),
        dt_bias=g["dt_bias"][0, :nh2].reshape(2, nh), a_log=g["a_log"][0, :nh2].reshape(2, nh),
        d_skip=jnp.stack([dsk, dsk]), ssd_norm_w=g["ssd_norm_w"].reshape(-1), w_ssd_out=g["w_ssd_out"],
        pool_w=g["pool_w"], pool_scale=g["pool_scale"].reshape(-1), w_pool_out=g["w_pool_out"], w_out=g["w_out"],
        g_ffn=g["g_ffn"].reshape(-1), w_gate_up=g["w_gate_up"], w_down=g["w_down"])


COND_ROWS = 16


def _split_mods(m):
    d = m.shape[1] // 6
    return [m[:2, k * d:(k + 1) * d].reshape(2, 1, d) for k in range(6)]


def _pool_args(rows, proj, col_block, width, pc, w):
    seg_const = lambda a: Arg(a, (None,) + a.shape[1:], lambda j, i, s: (s, 0, 0, 0), "const")
    pws = [Arg(w["pool_w"][k], w["pool_w"].shape[1:], lambda j, i, s: (0, 0), "acc") for k in range(w["pool_w"].shape[0])]
    return [rows.row(proj, width, col_block)] + [seg_const(a) for a in pc] + [rows.vec(w["pool_scale"])] + pws


TALL_ROW_TILE = 1088


def _tall_rows(T, ncol):
    tm = max(t for t in range(16, min(T, TALL_ROW_TILE) + 1, 16) if T % t == 0)
    return Rows(T // tm, 0, tm, ncol)


def _hosted(hosts, box, key):
    fn = (hosts or {}).get(key)
    return fn(box) if fn else None


def _layer_fwd(l, pre, cond_s, w, rows, n_ctx, pc, hosts=None, box=None):
    T, D = pre[0].shape if isinstance(pre, tuple) else pre.shape
    nt, nct, tm = rows.nt, rows.nct, rows.tm
    n = lambda s: f"l{l}_{s}"
    crow = Rows(1, 0, COND_ROWS)
    mraw = matmul_nn(n("ada_mm"), cond_s, w["w_ada"])
    (m,) = stage_fwd(n("ada_bias"), f_bias, crow, [crow.row(mraw, mraw.shape[1]), crow.vec(w["b_ada"])],
                     [(mraw.shape[1], F32, False)])
    sh1, sc1, ga1, sh2, sc2, ga2 = _split_mods(m)

    if isinstance(pre, tuple):
        x, h1 = stage_fwd(n("norm1"), f_resid_norm_mod, rows, _resid_norm_args(rows, pre, w["g_mix"], sh1, sc1, D),
                          [(D, F32, False), (D, ACT_DTYPE, False)])
    else:
        x = pre
        (h1,) = stage_fwd(n("norm1"), f_norm_mod, rows,
                          [rows.row(x, D), rows.vec(w["g_mix"]), rows.segvec(sh1), rows.segvec(sc1)],
                          [(D, ACT_DTYPE, False)])
    xbc_w = w["conv_w"].shape[1]
    di = w["ssd_norm_w"].shape[1]
    pw = w["pool_scale"].shape[1]
    c_z, c_g, c_p, c_dt = xbc_w, xbc_w + di, xbc_w + di + 2 * pw, xbc_w + di + 3 * pw
    ex = _hosted(hosts, box, "in_mm")
    proj = matmul_nn(n("in_mm"), h1, w["w1"], out_dtype=ACT_DTYPE, ex=ex, ncols=c_dt)
    if ex is not None:
        proj, box["in_mm"] = proj
    dtraw = matmul_nn(n("in_dt_mm"), h1, w["w1"], col0=c_dt, ncols=128)
    ex = _hosted(hosts, box, "conv")
    xbc, xres = conv_fwd(n("conv"), proj, w["conv_w"], w["conv_b"], n_ctx, xbc_w, ex)
    if ex is not None:
        box["conv"] = xres
    ex = _hosted(hosts, box, "ssd")
    y0, y1, states, xres = ssd_fwd(n("ssd"), xbc, dtraw, w["dt_bias"], w["a_log"], n_ctx, ex)
    y2 = (y0, y1)
    if ex is not None:
        box["ssd"] = xres

    G = SSD_GROUPS
    gw = di // G
    r8 = _tall_rows(T, G)
    gate_args = [r8.row(y2[0], gw, 0, True), r8.row(y2[1], gw, 0, True), r8.row(xbc, gw, 0, True, stride=2),
                 r8.row(proj, gw, c_z // gw, True), r8.vec(w["dskip"], True), r8.vec(w["ssd_norm_w"], True)]
    (ynw,) = stage_fwd(n("ssd_gate"), f_ssd_gate, r8, gate_args, [(gw, ACT_DTYPE, True)])
    ex = _hosted(hosts, box, "ssd_out_mm")
    o_ssd = matmul_nn(n("ssd_out_mm"), ynw, w["w_ssd_out"], ex=ex)
    if ex is not None:
        o_ssd, box["ssd_out_mm"] = o_ssd

    nw = len(POOL_WINDOWS)
    pg = pw // nw
    (ps,) = stage_fwd(n("pool"), f_pool_all, rows, _pool_args(rows, proj, c_p // pw, pw, pc, w), [(pw, ACT_DTYPE, False)])
    o_pool = matmul_nn(n("pool_out_mm"), ps, w["w_pool_out"])

    merge_args = [rows.row(o_ssd, D), rows.row(o_pool, D), rows.row(proj, pw, c_g // pw), rows.row(proj, pw, c_g // pw + 1)]
    (mg,) = stage_fwd(n("merge"), f_merge, rows, merge_args, [(D, ACT_DTYPE, False)])
    mo = matmul_nn(n("out_mm"), mg, w["w_out"])

    rn_args = [rows.row(x, D), rows.row(mo, D), rows.segvec(ga1), rows.vec(w["g_ffn"]), rows.segvec(sh2), rows.segvec(sc2)]
    x1, h2 = stage_fwd(n("norm2"), f_resid_norm_mod, rows, rn_args, [(D, F32, False), (D, ACT_DTYPE, False)])
    ex = _hosted(hosts, box, "gate_up_mm")
    gu = matmul_nn(n("gate_up_mm"), h2, w["w_gate_up"], ex=ex)
    if ex is not None:
        gu, box["gate_up_mm"] = gu
    fh = gu.shape[1] // 2
    (act,) = stage_fwd(n("swiglu"), f_swiglu, rows, [rows.row(gu, 2 * fh)], [(fh, ACT_DTYPE, False)])
    ex = _hosted(hosts, box, "down_mm")
    dn = matmul_nn(n("down_mm"), act, w["w_down"], ex=ex)
    if ex is not None:
        dn, box["down_mm"] = dn
    saved = dict(x=x, pre=pre, mraw=mraw, mods=(sh1, sc1, ga1, sh2, sc2, ga2), h1=h1, proj=proj, dtraw=dtraw, xbc=xbc, y2=y2,
                 states=states,
                 ynw=ynw, o_ssd=o_ssd, ps=ps, o_pool=o_pool, mg=mg, mo=mo, x1=x1, h2=h2, gu=gu, act=act, dn=dn,
                 cols=(c_z, c_g, c_p, c_dt))
    return (x1, dn, ga2), saved


def _resid_norm_args(rows, pre, g, sh, sc, D):
    x1, dn, ga2 = pre
    return [rows.row(x1, D), rows.row(dn, D), rows.segvec(ga2), rows.vec(g), rows.segvec(sh), rows.segvec(sc)]


def f_norm_mod_keep(x, g, sh, sc):
    return f_norm_mod(x, g, sh, sc)[0], x


def _layer_bwd(l, cot, cond_s, w, s, rows, n_ctx, pc, hosts=None, box=None):
    dx1, ddn, dga2 = cot
    T, D = dx1.shape
    nt, nct, tm = rows.nt, rows.nct, rows.tm
    n = lambda t: f"l{l}_{t}_bwd"
    sh1, sc1, ga1, sh2, sc2, ga2 = s["mods"]
    c_z, c_g, c_p, c_dt = s["cols"]
    x, proj, xbc, y2, gu = s["x"], s["proj"], s["xbc"], s["y2"], s["gu"]
    g = {}
    if box is not None:
        box["g"] = g

    ex = _hosted(hosts, box, "down_dx")
    dact = matmul_nt(n("down_dx"), ddn, w["w_down"], ex=ex)
    if ex is not None:
        dact, box["down_dx"] = dact
    ex = _hosted(hosts, box, "down_dw")
    g["w_down"] = matmul_tn(n("down_dw"), s["act"], ddn, ex=ex)
    if ex is not None:
        g["w_down"], box["down_dw"] = g["w_down"]
    fh = gu.shape[1] // 2
    (dgu,) = stage_bwd(n("swiglu"), f_swiglu, rows, [rows.row(gu, 2 * fh)], [rows.row(dact, fh)], [ACT_DTYPE])
    dh2 = matmul_nt(n("gate_up_dx"), dgu, w["w_gate_up"])
    g["w_gate_up"] = matmul_tn(n("gate_up_dw"), s["h2"], dgu, blocks=w["w_gate_up"].shape[0])

    rn_args = [rows.row(x, D), rows.row(s["mo"], D), rows.segvec(ga1), rows.vec(w["g_ffn"]), rows.segvec(sh2), rows.segvec(sc2)]
    dxr, dmo, dga1, g["g_ffn"], dsh2, dsc2 = stage_bwd(
        n("norm2"), f_resid_norm_mod, rows, rn_args, [rows.row(dx1, D), rows.row(dh2, D)], [F32, ACT_DTYPE])
    dmg = matmul_nt(n("out_dx"), dmo, w["w_out"])
    g["w_out"] = matmul_tn(n("out_dw"), s["mg"], dmo)

    pw = w["pool_scale"].shape[1]
    merge_args = [rows.row(s["o_ssd"], D), rows.row(s["o_pool"], D), rows.row(proj, pw, c_g // pw), rows.row(proj, pw, c_g // pw + 1)]
    do_ssd, do_pool, dgl_s, dgl_p = stage_bwd(n("merge"), f_merge, rows, merge_args, [rows.row(dmg, D)], [ACT_DTYPE] * 4)
    dps = matmul_nt(n("pool_out_dx"), do_pool, w["w_pool_out"])
    g["w_pool_out"] = matmul_tn(n("pool_out_dw"), s["ps"], do_pool)

    nw = len(POOL_WINDOWS)
    pg = pw // nw
    du_pool, g["pool_scale"], *dpw = stage_bwd(n("pool"), f_pool_all, rows, _pool_args(rows, proj, c_p // pw, pw, pc, w),
                                               [rows.row(dps, pw)], [ACT_DTYPE])
    g["pool_w"] = jnp.stack(dpw)

    dynw = matmul_nt(n("ssd_out_dx"), do_ssd, w["w_ssd_out"])
    g["w_ssd_out"] = matmul_tn(n("ssd_out_dw"), s["ynw"], do_ssd)
    G = SSD_GROUPS
    di = w["ssd_norm_w"].shape[1]
    gw = di // G
    r8 = _tall_rows(T, G)
    gate_args = [r8.row(y2[0], gw, 0, True), r8.row(y2[1], gw, 0, True), r8.row(xbc, gw, 0, True, stride=2),
                 r8.row(proj, gw, c_z // gw, True), r8.vec(w["dskip"], True), r8.vec(w["ssd_norm_w"], True)]
    gate_args[1].kind = "const"
    ex = _hosted(hosts, box, "ssd_gate")
    res = stage_bwd(n("ssd_gate"), f_ssd_gate, r8, gate_args, [r8.row(dynw, gw, 0, True)], [ACT_DTYPE] * 3, ex)
    if ex is not None:
        res, box["ssd_gate"] = res
    dy, dxs_skip, dz, g["dskip"], g["ssd_norm_w"] = res

    ex = _hosted(hosts, box, "ssd")
    dxbc0, dxbc1, ddt, g["dt_bias"], g["a_log"], xres = ssd_bwd(n("ssd"), xbc, s["dtraw"], w["dt_bias"], w["a_log"],
                                                                s["states"], dy, n_ctx, ex)
    dxbc2 = (dxbc0, dxbc1)
    if ex is not None:
        box["ssd"] = xres
    xbc_w = xbc.shape[1]
    ex = _hosted(hosts, box, "conv")
    dxbc_raw, g["conv_w"], g["conv_b"], xres = conv_bwd(n("conv"), proj, w["conv_w"], w["conv_b"], dxbc2, dxs_skip,
                                                         n_ctx, xbc_w, ex)
    if ex is not None:
        box["conv"] = xres
    pieces = [dxbc_raw, dz, dgl_s, dgl_p, du_pool, ddt]
    offsets = [0, c_z, c_g, c_g + pw, c_p, c_dt]
    ex = _hosted(hosts, box, "in_dx")
    dh1 = matmul_nt(n("in_dx"), pieces, w["w1"], ex=ex, offsets=offsets)
    if ex is not None:
        dh1, box["in_dx"] = dh1
    ex = _hosted(hosts, box, "in_dw")
    first = matmul_tn(n("in_dw0"), pieces[0], s["h1"], ex=ex)
    if ex is not None:
        first, box["in_dw"] = first
    g["w1"] = [first] + [matmul_tn(n(f"in_dw{k}"), p, s["h1"]) for k, p in enumerate(pieces) if k]

    if isinstance(s["pre"], tuple):
        dx1p, ddnp, dga2p, g["g_mix"], dsh1, dsc1 = stage_bwd(
            n("norm1"), f_resid_norm_mod, rows, _resid_norm_args(rows, s["pre"], w["g_mix"], sh1, sc1, D),
            [rows.row(dxr, D), rows.row(dh1, D)], [F32, ACT_DTYPE])
        dx = (dx1p, ddnp, dga2p)
    else:
        n1_args = [rows.row(x, D), rows.vec(w["g_mix"]), rows.segvec(sh1), rows.segvec(sc1)]
        dx, g["g_mix"], dsh1, dsc1 = stage_bwd(n("norm1"), f_norm_mod_keep, rows, n1_args,
                                               [rows.row(dh1, D), rows.row(dxr, D)], [F32])

    dm = jnp.concatenate([v.reshape(2, D) for v in (dsh1, dsc1, dga1, dsh2, dsc2, dga2)], axis=1)
    dm = jnp.concatenate([dm, jnp.zeros((COND_ROWS - 2, dm.shape[1]), F32)], axis=0)
    crow = Rows(1, 0, COND_ROWS)
    dmraw, g["b_ada"] = stage_bwd(n("ada_bias"), f_bias, crow, [crow.row(s["mraw"], dm.shape[1]), crow.vec(w["b_ada"])],
                                  [crow.row(dm, dm.shape[1])], [ACT_DTYPE])
    dcs = matmul_nt(n("ada_dx"), dmraw, w["w_ada"])
    g["w_ada"] = matmul_tn(n("ada_dw"), cond_s, dmraw, blocks=w["w_ada"].shape[0])
    return dx, dcs, g


def local_step(x, ctx, c, c_ctx, target, layer_w_fn, n_layers, g_final, fwd_hosts=None, bwd_hosts=None):
    L, D = x.shape
    n_ctx = ctx.shape[0]
    tm = ROW_TILE
    T = L + n_ctx
    rows = Rows(T // tm, n_ctx // tm, tm)
    pc = _pool_consts(tm, n_ctx)
    xa = jnp.concatenate([ctx, x], axis=0)
    cond = jnp.concatenate([c_ctx.reshape(1, D), c.reshape(1, D), jnp.zeros((COND_ROWS - 2, D), F32)], axis=0)
    crow = Rows(1, 0, COND_ROWS)
    (cond_s,) = stage_fwd("cond_silu", f_silu, crow, [crow.row(cond, D)], [(D, ACT_DTYPE, False)])

    saved, layer_w = [], []
    for l in range(n_layers):
        layer_w.append(layer_w_fn(l))
        box = {}
        xa, s = _layer_fwd(l, xa, cond_s, layer_w[l], rows, n_ctx, pc, fwd_hosts(l, box) if fwd_hosts else None, box)
        saved.append(s)

    x1, dn, ga2 = xa
    rl = Rows(L // tm, 0, tm)
    gf = g_final.reshape(1, D)
    tgt = rl.row(target, D)
    tgt.kind = "const"
    off = n_ctx // tm
    loss_args = [rl.row(x1, D, roff=off), rl.row(dn, D, roff=off), rl.vec(ga2[1]), tgt, rl.vec(gf)]
    ones = jnp.ones((L, 1), F32)
    dx1_lat, ddn_lat, dga2_lat, dgf, loss_rows = stage_bwd("loss", f_loss_resid, rl, loss_args, [rl.row(ones, 1)],
                                                           [F32, ACT_DTYPE], primal=[(1, F32)])
    loss = jnp.sum(loss_rows)
    cot = (jnp.concatenate([jnp.zeros((n_ctx, D), F32), dx1_lat], axis=0),
           jnp.concatenate([jnp.zeros((n_ctx, D), ACT_DTYPE), ddn_lat], axis=0),
           jnp.stack([jnp.zeros((1, D), F32), dga2_lat]))

    grads = [None] * n_layers
    dcs = jnp.zeros((COND_ROWS, D), F32)
    for l in reversed(range(n_layers)):
        box = {}
        hosts = bwd_hosts(l, grads, box) if bwd_hosts else None
        cot, dcs_l, grads[l] = _layer_bwd(l, cot, cond_s, layer_w[l], saved[l], rows, n_ctx, pc, hosts, box)
        dcs = dcs + dcs_l
    dx = cot
    (dcond,) = stage_bwd("cond_silu_bwd", f_silu, crow, [crow.row(cond, D)], [crow.row(dcs, D)], [F32])
    return loss, dx[n_ctx:], grads, dcond[0], dgf


def gather_chips(halves, conv=None):
    n = len(halves)
    ops = list(halves) + ([conv] if conv is not None else [])

    def copies(ins, outs, pos):
        c, me = pos[2], _chip_index(pos)
        pairs = [(s.at[c], o.at[me, c]) for s, o in zip(ins[:n], outs[:n])]
        pairs += [(s, o.at[me]) for s, o in zip(ins[n:], outs[n:])]
        return pairs, [(s, d, _flip(pos, rel)) for rel in PLANE for s, d in pairs]

    shapes = [jax.ShapeDtypeStruct((4,) + s.shape, s.dtype) for s in ops]
    return Exchange(copies, 3 * len(ops), len(ops), ops, shapes)


def gather_pair(gathered):
    n = len(gathered)

    def copies(ins, outs, pos):
        c = pos[2]
        return [], [(s.at[b, c], o.at[b, c], _flip(pos, PAIR[0])) for s, o in zip(ins, outs) for b in range(4)]

    shapes = [jax.ShapeDtypeStruct(g.shape, g.dtype) for g in gathered]
    return Exchange(copies, 4 * n, 0, gathered, shapes, aliases={k: k for k in range(n)})


def swap_halves(grads):
    n = len(grads)

    def copies(ins, outs, pos):
        c = pos[2]
        return [], [(g.at[b, 1 - c], o.at[b], _flip(pos, PAIR[0])) for g, o in zip(ins, outs) for b in range(4)]

    shapes = [jax.ShapeDtypeStruct((g.shape[0],) + g.shape[2:], g.dtype) for g in grads]
    return Exchange(copies, 4 * n, 0, grads, shapes)


def scatter_chips(sums):
    n = len(sums)

    def copies(ins, outs, pos):
        me = _chip_index(pos)
        local = [(p.at[me], o.at[me]) for p, o in zip(ins, outs)]
        remote = []
        for rel in PLANE:
            peer = _flip(pos, rel)
            remote += [(p.at[_chip_index(peer)], o.at[me], peer) for p, o in zip(ins, outs)]
        return local, remote

    shapes = [jax.ShapeDtypeStruct(p.shape, p.dtype) for p in sums]
    return Exchange(copies, 3 * n, n, sums, shapes)


def share_halves(finals):
    n = len(finals)

    def copies(ins, outs, pos):
        c = pos[2]
        return [], [(f.at[c], o.at[c], _flip(pos, PAIR[0])) for f, o in zip(ins, outs)]

    shapes = [jax.ShapeDtypeStruct(f.shape, f.dtype) for f in finals]
    return Exchange(copies, n, 0, finals, shapes, aliases={k: k for k in range(n)})


def gather_everyone(vec):
    def copies(ins, outs, pos):
        me = _device_index(pos)
        (v,), (o,) = ins, outs
        return [(v, o.at[me])], [(v, o.at[me], _flip(pos, rel)) for rel in EVERYONE]

    return Exchange(copies, len(EVERYONE), 1, [vec], [jax.ShapeDtypeStruct((8,) + vec.shape, vec.dtype)])


def _row_tile(rows, cols, n_bufs, mult=8):
    cap = VMEM_LIMIT_BYTES // 2 // (2 * n_bufs * cols * 4)
    for t in range(min(rows, cap) // mult * mult, 0, -mult):
        if rows % t == 0:
            return t
    return rows


def _adamw_update(w, g, m, v):
    nm = ADAM_B1 * m + (1.0 - ADAM_B1) * g
    nv = ADAM_B2 * v + (1.0 - ADAM_B2) * jnp.square(g)
    m_hat = nm / (1.0 - ADAM_B1 ** ADAM_STEP)
    v_hat = nv / (1.0 - ADAM_B2 ** ADAM_STEP)
    return -ADAM_LR * (m_hat / (jnp.sqrt(v_hat) + ADAM_EPS) + ADAM_WD * w), nm, nv


def adamw_small(name, ws, gs, ms, vs):
    n = len(ws)

    def body(*refs):
        ins, outs = refs[:4 * n], refs[4 * n:]
        for k in range(n):
            d, nm, nv = _adamw_update(ins[k][...], ins[n + k][...], ins[2 * n + k][...], ins[3 * n + k][...])
            outs[k][...] = d
            outs[n + k][...] = nm
            outs[2 * n + k][...] = nv

    shapes = [jax.ShapeDtypeStruct(a.shape, F32) for a in ws]
    vmem = pl.BlockSpec(memory_space=pltpu.VMEM)
    res = _pcall(body, name=name, out_shape=shapes * 3, in_specs=[vmem] * (4 * n), out_specs=[vmem] * (3 * n),
                 compiler_params=pltpu.CompilerParams(vmem_limit_bytes=VMEM_LIMIT_BYTES))(*ws, *gs, *ms, *vs)
    return res[:n], res[n:2 * n], res[2 * n:]


WIRE_DTYPE = jnp.bfloat16


def add_own_half(name, grads, recv, c):
    nb, _, R, C = grads.shape
    tr = _row_tile(R, C, 3, mult=16)

    def body(c_ref, g_ref, r_ref, o_ref):
        o_ref[...] = (g_ref[...] + r_ref[...]).astype(o_ref.dtype)

    spec = pl.BlockSpec((None, tr, C), lambda b, i, c_ref: (b, i, 0))
    return _pcall(
        body, name=name, out_shape=jax.ShapeDtypeStruct(recv.shape, WIRE_DTYPE),
        grid_spec=pltpu.PrefetchScalarGridSpec(
            num_scalar_prefetch=1, grid=(nb, R // tr),
            in_specs=[pl.BlockSpec((None, None, tr, C), lambda b, i, c_ref: (b, c_ref[0], i, 0)), spec],
            out_specs=spec),
        compiler_params=_params("parallel", "parallel"),
    )(c, grads, recv)


def sum_slots(name, a, c=None):
    n, R, C = a.shape
    tr = _row_tile(R, C, n + 1, mult=16 if a.dtype.itemsize == 2 else 8)

    def body(*refs):
        a_ref, o_ref = refs[-2:]
        acc = a_ref[0].astype(F32)
        for k in range(1, n):
            acc = acc + a_ref[k].astype(F32)
        o_ref[...] = acc

    if c is None:
        return _pcall(
            body, name=name, out_shape=jax.ShapeDtypeStruct((R, C), F32), grid=(R // tr,),
            in_specs=[pl.BlockSpec((n, tr, C), lambda i: (0, i, 0))], out_specs=pl.BlockSpec((tr, C), lambda i: (i, 0)),
            compiler_params=_params("parallel"),
        )(a)
    return _pcall(
        body, name=name, out_shape=jax.ShapeDtypeStruct((2, R, C), F32),
        grid_spec=pltpu.PrefetchScalarGridSpec(
            num_scalar_prefetch=1, grid=(R // tr,),
            in_specs=[pl.BlockSpec((n, tr, C), lambda i, c_ref: (0, i, 0))],
            out_specs=pl.BlockSpec((None, tr, C), lambda i, c_ref: (c_ref[0], i, 0))),
        compiler_params=_params("parallel"),
    )(c, a)


def adamw(name, w, g_layers, m, v):
    nl, R, C = w.shape
    assert len(g_layers) == nl
    tr = _row_tile(R, C, 8 + nl)
    nr = R // tr

    def body(*refs):
        w_ref, m_ref, v_ref = refs[:3]
        g_refs = refs[3:3 + nl]
        go_ref, d_ref, nm_ref, nv_ref = refs[3 + nl:]
        l = pl.program_id(0)
        gr = g_refs[0][...]
        for k in range(1, nl):
            gr = jnp.where(l == k, g_refs[k][...], gr)
        d_ref[...], nm_ref[...], nv_ref[...] = _adamw_update(w_ref[...], gr, m_ref[...], v_ref[...])
        go_ref[...] = gr

    spec = pl.BlockSpec((None, tr, C), lambda l, i: (l, i, 0))
    g_specs = [pl.BlockSpec((tr, C), (lambda l, i, k=k: (jnp.where(l == k, i, jnp.where(l < k, 0, nr - 1)), 0)))
               for k in range(nl)]
    return _pcall(
        body, name=name, out_shape=[jax.ShapeDtypeStruct((nl, R, C), F32)] * 4, grid=(nl, nr),
        in_specs=[spec] * 3 + g_specs, out_specs=[spec] * 4, compiler_params=_params("arbitrary", "arbitrary"),
    )(w, m, v, *g_layers)


BIG = ("w_ada", "w_in", "w_ssd_out", "pool_w", "w_pool_out", "w_out", "w_gate_up", "w_down")
COL_SHARDED = ("w_ada", "w_in", "w_gate_up")
BLOCK_LAYOUT = ("w_ada", "w_gate_up")
GRAD_TRANSPOSED = ("w_in",)
FIRST_USED = ("w_ada", "w_in")
MID_USED = ("w_ssd_out", "pool_w", "w_pool_out", "w_out")
END_USED = ("w_gate_up", "w_down")
LATER_USED = MID_USED + END_USED
assert FIRST_USED + LATER_USED == BIG
READY_LAST = FIRST_USED
READY_EARLY = LATER_USED
SMALL = ("c_ctx", "b_ada", "g_mix", "conv_w", "conv_b", "dt_bias", "a_log", "d_skip", "ssd_norm_w", "pool_scale",
         "g_ffn", "g_final")
WEIGHTS = ("c_ctx", "w_ada", "b_ada", "g_mix", "w_in", "conv_w", "conv_b", "dt_bias", "a_log", "d_skip", "ssd_norm_w",
           "w_ssd_out", "pool_w", "pool_scale", "w_pool_out", "w_out", "g_ffn", "w_gate_up", "w_down", "g_final")
LAYER_KEYS = ("w_ada", "b_ada", "g_mix", "w_in", "conv_w", "conv_b", "dt_bias", "a_log", "d_skip", "ssd_norm_w",
              "w_ssd_out", "pool_w", "pool_scale", "w_pool_out", "w_out", "g_ffn", "w_gate_up", "w_down")


def _shard2d(name, a):
    if name == "pool_w":
        return a.reshape(a.shape[0], a.shape[1] * a.shape[2], a.shape[3])
    return a


def _full_from_blocks(name, a):
    nb, R, C = a.shape
    if name in BLOCK_LAYOUT:
        return a
    if name in COL_SHARDED:
        return jnp.transpose(a, (1, 0, 2)).reshape(R, nb * C)
    if name == "pool_w":
        nw = len(POOL_WINDOWS)
        return jnp.transpose(a.reshape(nb, nw, R // nw, C), (1, 0, 2, 3)).reshape(nw, nb * R // nw, C)
    return a.reshape(nb * R, C)


def _blocks_from_full(name, g):
    nb = 4
    if name in BLOCK_LAYOUT:
        return g
    if name in COL_SHARDED and name not in GRAD_TRANSPOSED:
        K, N = g.shape
        return jnp.transpose(g.reshape(K, nb, N // nb), (1, 0, 2))
    if name == "pool_w":
        nw, r, C = g.shape
        return jnp.transpose(g.reshape(nw, nb, r // nb, C), (1, 0, 2, 3)).reshape(nb, nw * r // nb, C)
    return g.reshape(nb, g.shape[0] // nb, g.shape[1])


def _pack(arrs, rows):
    flat = jnp.concatenate([a.reshape(-1).astype(F32) for a in arrs])
    return jnp.concatenate([flat, jnp.zeros((rows * 128 - flat.size,), F32)]).reshape(rows, 128)


def _unpack(vec, shapes):
    flat = vec.reshape(-1)
    out, o = [], 0
    for s in shapes:
        n = int(np.prod(s))
        out.append(flat[o:o + n].reshape(s))
        o += n
    return out


def _rows_for(shapes):
    n = sum(int(np.prod(s)) for s in shapes)
    return -(-n // (8 * 128)) * 8


def kernel(x, c, ctx, c_ctx, w_ada, b_ada, g_mix, w_in, conv_w, conv_b, dt_bias, a_log, d_skip, ssd_norm_w, w_ssd_out, pool_w, pool_scale, w_pool_out, w_out, g_ffn, w_gate_up, w_down, g_final, loss_target, m_c_ctx, m_w_ada, m_b_ada, m_g_mix, m_w_in, m_conv_w, m_conv_b, m_dt_bias, m_a_log, m_d_skip, m_ssd_norm_w, m_w_ssd_out, m_pool_w, m_pool_scale, m_w_pool_out, m_w_out, m_g_ffn, m_w_gate_up, m_w_down, m_g_final, v_c_ctx, v_w_ada, v_b_ada, v_g_mix, v_w_in, v_conv_w, v_conv_b, v_dt_bias, v_a_log, v_d_skip, v_ssd_norm_w, v_w_ssd_out, v_pool_w, v_pool_scale, v_w_pool_out, v_w_out, v_g_ffn, v_w_gate_up, v_w_down, v_g_final):
    w = dict(c_ctx=c_ctx, w_ada=w_ada, b_ada=b_ada, g_mix=g_mix, w_in=w_in, conv_w=conv_w, conv_b=conv_b, dt_bias=dt_bias,
             a_log=a_log, d_skip=d_skip, ssd_norm_w=ssd_norm_w, w_ssd_out=w_ssd_out, pool_w=pool_w, pool_scale=pool_scale,
             w_pool_out=w_pool_out, w_out=w_out, g_ffn=g_ffn, w_gate_up=w_gate_up, w_down=w_down, g_final=g_final)
    m = dict(c_ctx=m_c_ctx, w_ada=m_w_ada, b_ada=m_b_ada, g_mix=m_g_mix, w_in=m_w_in, conv_w=m_conv_w, conv_b=m_conv_b,
             dt_bias=m_dt_bias, a_log=m_a_log, d_skip=m_d_skip, ssd_norm_w=m_ssd_norm_w, w_ssd_out=m_w_ssd_out,
             pool_w=m_pool_w, pool_scale=m_pool_scale, w_pool_out=m_w_pool_out, w_out=m_w_out, g_ffn=m_g_ffn,
             w_gate_up=m_w_gate_up, w_down=m_w_down, g_final=m_g_final)
    v = dict(c_ctx=v_c_ctx, w_ada=v_w_ada, b_ada=v_b_ada, g_mix=v_g_mix, w_in=v_w_in, conv_w=v_conv_w, conv_b=v_conv_b,
             dt_bias=v_dt_bias, a_log=v_a_log, d_skip=v_d_skip, ssd_norm_w=v_ssd_norm_w, w_ssd_out=v_w_ssd_out,
             pool_w=v_pool_w, pool_scale=v_pool_scale, w_pool_out=v_w_pool_out, w_out=v_w_out, g_ffn=v_g_ffn,
             w_gate_up=v_w_gate_up, w_down=v_w_down, g_final=v_g_final)
    assert x.shape[0] == 1, "one example per device"
    pos = _position()
    core = pos[2].astype(jnp.int32).reshape(1)
    n_layers = w_in.shape[0]
    assert n_layers == 2
    dims = (ssd_norm_w.shape[1], conv_w.shape[2] * 4, dt_bias[0].size, pool_scale.shape[1])
    shard = {k: _shard2d(k, w[k]) for k in BIG}

    def halves(a):
        return a.reshape(a.shape[:-2] + (2, a.shape[-2] // 2, a.shape[-1]))

    def whole(a):
        return a.reshape(a.shape[:-3] + (2 * a.shape[-2], a.shape[-1]))

    def wire_shards(l, names):
        return [halves(shard[k][l].astype(MXU_DTYPE)) for k in names]

    def full_weights(names, gathered):
        return {k: _full_from_blocks(k, whole(a)) for k, a in zip(names, gathered)}

    first = comm_call("gather0_chips", gather_chips(wire_shards(0, FIRST_USED), conv=conv_w))
    got0 = full_weights(FIRST_USED, comm_call("gather0_pair", gather_pair(first[:-1])))
    conv_all = first[-1]
    conv_full = [jnp.transpose(conv_all[:, l], (1, 0, 2)).reshape(conv_all.shape[2], -1) for l in range(n_layers)]

    boxes = {}

    n_first, n_mid = len(FIRST_USED), len(MID_USED)

    n_first, n_end = len(FIRST_USED), len(END_USED)

    def layer_w_fn(l):
        if l == 0:
            full = dict(got0)
        else:
            f0 = boxes[("fwd", 0)]
            full = full_weights(("w_in",), f0["gate_up_mm"][:1])
            full.update(full_weights(("w_ada",), f0["down_mm"]))
        late = {k: (lambda i=i: boxes[("fwd", l)]["conv"][i]) for i, k in enumerate(MID_USED)}
        late.update({k: (lambda i=i: boxes[("fwd", l)]["ssd_out_mm"][i]) for i, k in enumerate(END_USED)})
        full["conv_w"] = conv_full[l]
        lw = LazyDict(_prep_layer_weights(*[full[k] if k in full else (None if k in late else w[k][l]) for k in LAYER_KEYS]))
        for k, get in late.items():
            lw[k] = (lambda k=k, get=get: _full_from_blocks(k, whole(get())))
        return lw

    def fwd_hosts(l, box):
        boxes[("fwd", l)] = box
        hosts = {"in_mm": lambda box: gather_chips(wire_shards(l, MID_USED)),
                 "conv": lambda box: gather_pair(box["in_mm"]),
                 "ssd": lambda box: gather_chips(wire_shards(l, END_USED)),
                 "ssd_out_mm": lambda box: gather_pair(box["ssd"][:n_end])}
        if l == 0:
            hosts["ssd"] = lambda box: combine(gather_chips(wire_shards(0, END_USED)), gather_chips(wire_shards(1, ("w_in",))))
            hosts["gate_up_mm"] = lambda box: combine(gather_pair(box["ssd"][n_end:]),
                                                      gather_chips(wire_shards(1, ("w_ada",))))
            hosts["down_mm"] = lambda box: gather_pair(box["gate_up_mm"][1:])
        return hosts

    def blocks(gl, names):
        return [halves(_blocks_from_full(k, gl[k])) for k in names]

    def pair_sums(tag, names, G, recv):
        return [add_own_half(f"pair_sum{tag}_{k}", g, r, core) for k, g, r in zip(names, G, recv)]

    def chip_sums(tag, names, parts):
        return [sum_slots(f"chip_sum{tag}_{k}", p, core) for k, p in zip(names, parts)]

    def reduce_now(tag, gl, names):
        G = blocks(gl, names)
        pair = pair_sums(tag, names, G, comm_call(f"swap{tag}", swap_halves(G)))
        return chip_sums(tag, names, comm_call(f"scatter{tag}", scatter_chips(pair)))

    small_layers = {}
    n_big = len(BIG)

    def bwd_hosts(l, grads, box):
        boxes[("bwd", l)] = box
        if l != 0:
            return None
        gl1 = _unprep_layer_grads(grads[1], dims)
        small_layers[1] = gl1
        G1 = blocks(gl1, BIG)
        early = {}

        def gate_host(box):
            early["G"] = blocks(box["g"], READY_EARLY)
            return swap_halves(early["G"])

        def scan_host(box):
            return combine(scatter_chips(pair_sums("1", BIG, G1, box["down_dx"] + box["down_dw"])),
                           scatter_chips(pair_sums("0e", READY_EARLY, early["G"], box["ssd_gate"])))

        def conv_host(box):
            return combine(share_halves(chip_sums("1", BIG, box["ssd"][:n_big])),
                           share_halves(chip_sums("0e", READY_EARLY, box["ssd"][n_big:])))

        return {"down_dx": lambda box: swap_halves(G1[:n_first]), "down_dw": lambda box: swap_halves(G1[n_first:]),
                "ssd_gate": gate_host, "ssd": scan_host, "in_dx": conv_host}

    loss, grad_x, grads, d_c_ctx, d_g_final = local_step(
        x[0], ctx[0], c[0], c_ctx, loss_target[0], layer_w_fn, n_layers, g_final, fwd_hosts, bwd_hosts)
    shared =[whole(a) for a in boxes[("bwd", 0)]["in_dx"]]
    reduced1 = shared[:n_big]
    gl0 = _unprep_layer_grads(grads[0], dims)
    small_layers[0] = gl0
    last_halves = reduce_now("0", gl0, READY_LAST)

    small_full = dict(c_ctx=d_c_ctx, g_final=d_g_final.reshape(-1))
    for k in SMALL:
        if k not in small_full:
            small_full[k] = jnp.stack([small_layers[l][k] for l in range(n_layers)])
    shapes = [small_full[k].shape for k in SMALL] + [(1,)]
    packed = _pack([small_full[k] for k in SMALL] + [loss.reshape(1)], _rows_for(shapes))
    *last, everyone = comm_call("share0_small", combine(share_halves(last_halves), gather_everyone(packed)))
    red0 = dict(zip(READY_EARLY, shared[n_big:]))
    red0.update(zip(READY_LAST, [whole(a) for a in last]))
    reduced0 = [red0[k] for k in BIG]
    total = sum_slots("small_sum", everyone)
    *small_vals, loss = _unpack(total, shapes)
    loss = loss.reshape(())
    small_g = dict(zip(SMALL, small_vals))
    cw = conv_w.shape[2]
    small_g["conv_w"] = lax.dynamic_slice_in_dim(small_g["conv_w"], _chip_index(pos) * cw, cw, axis=2)

    grad, delta, new_m, new_v = {}, {}, {}, {}
    for k, g0, g1 in zip(BIG, reduced0, reduced1):
        shp = w[k].shape
        if k in GRAD_TRANSPOSED:
            flat = lambda a: jnp.swapaxes(a, 1, 2)
            back = lambda a: jnp.swapaxes(a, 1, 2)
        else:
            flat = lambda a: _shard2d(k, a)
            back = lambda a: a.reshape(shp)
        outs = adamw(f"adamw_{k}", flat(w[k]), [g0, g1], flat(m[k]), flat(v[k]))
        grad[k], delta[k], new_m[k], new_v[k] = [back(a) for a in outs]
    flat2 = lambda d: [d[k].reshape(-1, d[k].shape[-1]) for k in SMALL]
    d_, m_, v_ = adamw_small("adamw_small", flat2(w), flat2(small_g), flat2(m), flat2(v))
    for k, dd, mm, vv in zip(SMALL, d_, m_, v_):
        shp = w[k].shape
        grad[k], delta[k], new_m[k], new_v[k] = small_g[k], dd.reshape(shp), mm.reshape(shp), vv.reshape(shp)

    return (loss, grad_x[None], *[grad[k] for k in WEIGHTS], *[delta[k] for k in WEIGHTS],
            *[new_m[k] for k in WEIGHTS], *[new_v[k] for k in WEIGHTS])
```

```python
import functools

import jax
import jax.numpy as jnp
import numpy as np
from jax import lax
from jax.experimental import pallas as pl
from jax.experimental.pallas import tpu as pltpu

F32 = jnp.float32
MXU_DTYPE = jnp.bfloat16
ACT_DTYPE = jnp.bfloat16
VMEM_LIMIT_BYTES = 56 * 1024 * 1024
EPS = 1e-6
NEG = -1e30

SSD_HEADDIM = 64
SSD_GROUPS = 8
SSD_STATE = 128
SSD_CHUNK = 128
SSD_GROUPS_PER_STEP = 8
SSD_CONV = 5
GRID_W = 64
POOL_WINDOWS = (2, 4, 8, 16)
ROW_TILE = 256
DT_PAD = 512

ADAM_LR = 0.001
ADAM_B1 = 0.9
ADAM_B2 = 0.999
ADAM_EPS = 1e-08
ADAM_WD = 0.01
ADAM_STEP = 10

MESH = pl.DeviceIdType.MESH


def _pcall(body, **kw):
    return pl.pallas_call(body, **kw)


def _params(*sem):
    return pltpu.CompilerParams(dimension_semantics=tuple(sem), vmem_limit_bytes=VMEM_LIMIT_BYTES)


def _pick_tile(n, cands):
    for t in cands:
        if n % t == 0:
            return t
    return n


PLANE = ((1, 0, 0), (0, 1, 0), (1, 1, 0))
PAIR = ((0, 0, 1),)
EVERYONE = tuple((a, b, d) for a in (0, 1) for b in (0, 1) for d in (0, 1) if a + b + d)
HBM = pl.BlockSpec(memory_space=pl.ANY)


def _position():
    return lax.axis_index("x"), lax.axis_index("y"), lax.axis_index("c")


def _flip(pos, rel):
    return tuple(1 - p if r else p for p, r in zip(pos, rel))


def _chip_index(pos):
    return 2 * pos[0] + pos[1]


def _device_index(pos):
    return 4 * pos[0] + 2 * pos[1] + pos[2]


class Exchange:
    def __init__(self, copies, n_remote, n_local, operands, out_shapes, aliases=None):
        self.copies, self.n_remote, self.n_local = copies, n_remote, n_local
        self.operands, self.out_shapes, self.aliases = list(operands), list(out_shapes), dict(aliases or {})

    def scratch(self):
        return [pltpu.SemaphoreType.DMA((max(self.n_remote, 1),)), pltpu.SemaphoreType.DMA((max(self.n_remote, 1),)),
                pltpu.SemaphoreType.DMA((max(self.n_local, 1),))]

    def descriptors(self, ins, outs, sems):
        send_sems, recv_sems, local_sems = sems
        local, remote = self.copies(ins, outs, _position())
        assert len(local) == self.n_local and len(remote) == self.n_remote
        cps = [pltpu.make_async_copy(src, dst, local_sems.at[k]) for k, (src, dst) in enumerate(local)]
        cps += [pltpu.make_async_remote_copy(src_ref=src, dst_ref=dst, send_sem=send_sems.at[k], recv_sem=recv_sems.at[k],
                                             device_id=peer, device_id_type=MESH) for k, (src, dst, peer) in enumerate(remote)]
        return cps


def combine(a, b):
    na, nao = len(a.operands), len(a.out_shapes)

    def copies(ins, outs, pos):
        la, ra = a.copies(ins[:na], outs[:nao], pos)
        lb, rb = b.copies(ins[na:], outs[nao:], pos)
        return la + lb, ra + rb

    aliases = dict(a.aliases)
    aliases.update({na + k: nao + v for k, v in b.aliases.items()})
    return Exchange(copies, a.n_remote + b.n_remote, a.n_local + b.n_local, a.operands + b.operands,
                    a.out_shapes + b.out_shapes, aliases)


class LazyDict(dict):
    def __getitem__(self, key):
        v = dict.__getitem__(self, key)
        if callable(v):
            v = v()
            dict.__setitem__(self, key, v)
        return v


def comm_call(name, ex):
    n_in, n_out = len(ex.operands), len(ex.out_shapes)

    def body(*refs):
        cps = ex.descriptors(refs[:n_in], refs[n_in:n_in + n_out], refs[n_in + n_out:])
        for cp in cps:
            cp.start()
        for cp in cps:
            cp.wait()

    return _pcall(
        body, name=name, out_shape=ex.out_shapes, in_specs=[HBM] * n_in, out_specs=[HBM] * n_out,
        scratch_shapes=ex.scratch(), input_output_aliases=ex.aliases,
        compiler_params=pltpu.CompilerParams(has_side_effects=True),
    )(*ex.operands)


def hosted_call(body, ex, operands, *, name, out_shape, grid, in_specs, out_specs, scratch_shapes=()):
    n_in, n_out, n_scr = len(operands), len(out_shape), len(scratch_shapes)
    sem = ("arbitrary",) * len(grid)
    if ex is None:
        res = _pcall(body, name=name, out_shape=list(out_shape), grid=grid, in_specs=list(in_specs),
                     out_specs=list(out_specs), scratch_shapes=list(scratch_shapes), compiler_params=_params(*sem))(*operands)
        return res, []
    x_in, x_out = len(ex.operands), len(ex.out_shapes)

    def wrapped(*refs):
        o = 0
        ins = refs[o:o + n_in]; o += n_in
        xins = refs[o:o + x_in]; o += x_in
        outs = refs[o:o + n_out]; o += n_out
        xouts = refs[o:o + x_out]; o += x_out
        scr = refs[o:o + n_scr]; o += n_scr
        sems = refs[o:]
        first = last = None
        for a, n in enumerate(grid):
            i = pl.program_id(a)
            first = (i == 0) if first is None else first & (i == 0)
            last = (i == n - 1) if last is None else last & (i == n - 1)

        @pl.when(first)
        def _():
            for cp in ex.descriptors(xins, xouts, sems):
                cp.start()

        body(*ins, *outs, *scr)

        @pl.when(last)
        def _():
            for cp in ex.descriptors(xins, xouts, sems):
                cp.wait()

    aliases = {n_in + k: n_out + v for k, v in ex.aliases.items()}
    res = _pcall(
        wrapped, name=name, out_shape=list(out_shape) + ex.out_shapes, grid=grid,
        in_specs=list(in_specs) + [HBM] * x_in, out_specs=list(out_specs) + [HBM] * x_out,
        scratch_shapes=list(scratch_shapes) + ex.scratch(), input_output_aliases=aliases,
        compiler_params=pltpu.CompilerParams(dimension_semantics=sem, vmem_limit_bytes=VMEM_LIMIT_BYTES,
                                             has_side_effects=True),
    )(*operands, *ex.operands)
    return res[:n_out], res[n_out:]


def _dot(a, b, dims):
    return lax.dot_general(a.astype(MXU_DTYPE), b.astype(MXU_DTYPE), (dims, ((), ())), preferred_element_type=F32)


_NN = ((1,), (0,))
_NT = ((1,), (1,))
_TN = ((0,), (0,))


@jax.custom_vjp
def _mm(a, b):
    return _dot(a, b, _NN)


def _mm_fwd(a, b):
    return _mm(a, b), (a, b)


def _mm_bwd(res, g):
    a, b = res
    return _dot(g, b, _NT).astype(a.dtype), _dot(a, g, _TN).astype(b.dtype)


_mm.defvjp(_mm_fwd, _mm_bwd)


@jax.custom_vjp
def _mm_nt(a, b):
    return _dot(a, b, _NT)


def _mm_nt_fwd(a, b):
    return _mm_nt(a, b), (a, b)


def _mm_nt_bwd(res, g):
    a, b = res
    return _dot(g, b, _NN).astype(a.dtype), _dot(g, a, _TN).astype(b.dtype)


_mm_nt.defvjp(_mm_nt_fwd, _mm_nt_bwd)


@jax.custom_vjp
def _mm_tn(a, b):
    return _dot(a, b, _TN)


def _mm_tn_fwd(a, b):
    return _mm_tn(a, b), (a, b)


def _mm_tn_bwd(res, g):
    a, b = res
    return _dot(b, g, _NT).astype(a.dtype), _dot(a, g, _NN).astype(b.dtype)


_mm_tn.defvjp(_mm_tn_fwd, _mm_tn_bwd)


def _dot_exact(m01, v):
    m = m01.astype(jnp.bfloat16)
    hi = v.astype(jnp.bfloat16)
    r1 = v - hi.astype(F32)
    mid = r1.astype(jnp.bfloat16)
    lo = (r1 - mid.astype(F32)).astype(jnp.bfloat16)
    out = jnp.dot(m, hi, preferred_element_type=F32)
    out = out + jnp.dot(m, mid, preferred_element_type=F32)
    return out + jnp.dot(m, lo, preferred_element_type=F32)


@jax.custom_vjp
def _lin01(m, mt, v):
    return _dot_exact(m, v)


def _lin01_fwd(m, mt, v):
    return _dot_exact(m, v), (m, mt)


def _lin01_bwd(res, g):
    m, mt = res
    return jnp.zeros_like(m), jnp.zeros_like(mt), _dot_exact(mt, g)


_lin01.defvjp(_lin01_fwd, _lin01_bwd)


MATMUL_VMEM_BUDGET = VMEM_LIMIT_BYTES * 5 // 6


def _mm_tiles(m, n, k_bytes_a, k_bytes_b, out_bytes, cands_m, cands_n):
    best = None
    for tm in cands_m:
        if m % tm:
            continue
        for tn in cands_n:
            if n % tn:
                continue
            need = 2 * (tm * k_bytes_a + tn * k_bytes_b + tm * tn * out_bytes)
            if need <= MATMUL_VMEM_BUDGET and (best is None or tm * tn > best[0] * best[1]):
                best = (tm, tn)
    assert best is not None, (m, n)
    return best


_ROW_CANDS = (4352, 2176, 1088, 768, 544, 512, 272, 256, 128, 16)
_COL_CANDS = (2816, 2048, 1408, 1024, 512, 256, 128)


def _one(res, xres, ex):
    return res[0] if ex is None else (res[0], xres)


def _block_cands(c):
    return (c,) + tuple(t for t in (512, 256, 128) if c % t == 0)


def matmul_nn(name, a, b, out_dtype=F32, ex=None, col0=0, ncols=None):
    M, K = a.shape
    if b.ndim == 3:
        nb, _, C = b.shape
        N, cands = nb * C, _block_cands(C)
    else:
        N, cands = (b.shape[1] - col0 if ncols is None else ncols), (512, 256, 128)
    tm, tn = _mm_tiles(M, N, K * a.dtype.itemsize, K * b.dtype.itemsize, jnp.dtype(out_dtype).itemsize,
                       _ROW_CANDS, cands)
    if b.ndim == 3:
        per = C // tn
        b_spec = pl.BlockSpec((None, K, tn), lambda j, i: (j // per, 0, j % per))
    else:
        assert col0 % tn == 0
        first = col0 // tn
        b_spec = pl.BlockSpec((K, tn), lambda j, i: (0, first + j))

    def body(a_ref, b_ref, o_ref):
        o_ref[...] = _dot(a_ref[...], b_ref[...], _NN).astype(o_ref.dtype)

    res, xres = hosted_call(
        body, ex, [a, b], name=name, out_shape=[jax.ShapeDtypeStruct((M, N), out_dtype)], grid=(N // tn, M // tm),
        in_specs=[pl.BlockSpec((tm, K), lambda j, i: (i, 0)), b_spec],
        out_specs=[pl.BlockSpec((tm, tn), lambda j, i: (i, j))])
    return _one(res, xres, ex)


def matmul_nt(name, g, b, out_dtype=F32, ex=None, offsets=None):
    pieces = list(g) if isinstance(g, (list, tuple)) else [g]
    offsets = list(offsets) if offsets is not None else [0]
    M = pieces[0].shape[0]
    if b.ndim == 3:
        nb, K, C = b.shape
        N = nb * C
        assert len(pieces) == 1
    else:
        K, N = b.shape
    g_bytes = sum(p.shape[1] * p.dtype.itemsize for p in pieces)
    tm, tk = _mm_tiles(M, K, g_bytes, N * b.dtype.itemsize, jnp.dtype(out_dtype).itemsize, _ROW_CANDS, _COL_CANDS)

    def body(*refs):
        b_ref, o_ref = refs[-2:]
        acc = None
        if b.ndim == 3:
            parts = [_dot(refs[0][:, k * C:(k + 1) * C], b_ref[k], _NT) for k in range(nb)]
        else:
            parts = [_dot(g_ref[...], b_ref[:, off:off + g_ref.shape[1]], _NT) for g_ref, off in zip(refs[:-2], offsets)]
        for part in parts:
            acc = part if acc is None else acc + part
        o_ref[...] = acc.astype(o_ref.dtype)

    b_spec = (pl.BlockSpec((nb, tk, C), lambda j, i: (0, j, 0)) if b.ndim == 3
              else pl.BlockSpec((tk, N), lambda j, i: (j, 0)))
    res, xres = hosted_call(
        body, ex, pieces + [b], name=name, out_shape=[jax.ShapeDtypeStruct((M, K), out_dtype)], grid=(K // tk, M // tm),
        in_specs=[pl.BlockSpec((tm, p.shape[1]), lambda j, i: (i, 0)) for p in pieces] + [b_spec],
        out_specs=[pl.BlockSpec((tm, tk), lambda j, i: (i, j))])
    return _one(res, xres, ex)


def matmul_tn(name, a, g, ex=None, blocks=1):
    M, K = a.shape
    N = g.shape[1]
    C = N // blocks
    tk, tn = _mm_tiles(K, N, M * a.dtype.itemsize, M * g.dtype.itemsize, 4, (512, 256, 128),
                       (512, 256, 128) if blocks == 1 else _block_cands(C))

    def body(a_ref, g_ref, o_ref):
        o_ref[...] = _dot(a_ref[...], g_ref[...], _TN)

    if blocks == 1:
        out_shape, out_spec = jax.ShapeDtypeStruct((K, N), F32), pl.BlockSpec((tk, tn), lambda i, j: (i, j))
    else:
        per = C // tn
        out_shape = jax.ShapeDtypeStruct((blocks, K, C), F32)
        out_spec = pl.BlockSpec((None, tk, tn), lambda i, j: (j // per, i, j % per))
    res, xres = hosted_call(
        body, ex, [a, g], name=name, out_shape=[out_shape], grid=(K // tk, N // tn),
        in_specs=[pl.BlockSpec((M, tk), lambda i, j: (0, i)), pl.BlockSpec((M, tn), lambda i, j: (0, j))],
        out_specs=[out_spec])
    return _one(res, xres, ex)


class Arg:
    def __init__(self, arr, block, imap, kind):
        self.arr, self.block, self.imap, self.kind = arr, block, imap, kind


class Rows:
    def __init__(self, nt, nct, tm, ncol=1):
        self.nt, self.nct, self.tm, self.ncol = nt, nct, tm, ncol

    def seg(self, i):
        return jnp.where(i >= self.nct, 1, 0)

    def spec(self, block, imap):
        return pl.BlockSpec(block, lambda j, i: imap(j, i, self.seg(i)))

    def row(self, arr, width, cb0=0, follow=False, roff=0, stride=1):
        f = stride if follow else 0
        return Arg(arr, (self.tm, width), lambda j, i, s: (i + roff, cb0 + f * j), "row")

    def vec(self, arr, follow=False, kind="acc"):
        w = arr.shape[1] // (self.ncol if follow else 1)
        f = 1 if follow else 0
        return Arg(arr, (1, w), lambda j, i, s: (0, f * j), kind)

    def segvec(self, arr, kind="seg"):
        return Arg(arr, (None, 1, arr.shape[2]), lambda j, i, s: (s, 0, 0), kind)


def _load(ref):
    return ref[...].astype(F32) if ref.dtype != F32 else ref[...]


def stage_fwd(name, f, rows, args, outs):
    n_in = len(args)

    def body(*refs):
        vals = [_load(r) for r in refs[:n_in]]
        res = f(*vals)
        for r, v in zip(refs[n_in:], res):
            r[...] = v.astype(r.dtype)

    T = rows.nt * rows.tm
    out_shape = [jax.ShapeDtypeStruct((T, w * (rows.ncol if fo else 1)), dt) for w, dt, fo in outs]
    out_specs = [pl.BlockSpec((rows.tm, w), (lambda j, i, fo=fo: (i, j if fo else 0))) for w, dt, fo in outs]
    res = _pcall(
        body, name=name, out_shape=out_shape, grid=(rows.ncol, rows.nt),
        in_specs=[rows.spec(a.block, a.imap) for a in args], out_specs=out_specs,
        compiler_params=_params("parallel", "parallel"),
    )(*[a.arr for a in args])
    return res


def stage_bwd(name, f, rows, args, cots, row_dtypes, ex=None, primal=()):
    n_in, n_ct = len(args), len(cots)
    diff = [k for k, a in enumerate(args) if a.kind != "const"]
    row_dt = {}
    for k in diff:
        if args[k].kind == "row":
            row_dt[k] = row_dtypes[len(row_dt)]

    def body(*refs):
        i = pl.program_id(1)
        vals = [_load(r) for r in refs[:n_in]]
        cts = tuple(_load(r) for r in refs[n_in:n_in + n_ct])
        outs = refs[n_in + n_ct:]

        def g(*dv):
            full = list(vals)
            for k, v in zip(diff, dv):
                full[k] = v
            return tuple(f(*full))

        prim, vjp = jax.vjp(g, *[vals[k] for k in diff])
        grads = vjp(cts)
        for o, v in zip(outs[len(diff):], prim):
            o[...] = v.astype(o.dtype)
        for k, o, gr in zip(diff, outs, grads):
            kind = args[k].kind
            if kind == "row":
                o[...] = gr.astype(o.dtype)
            else:
                first = (i == 0) | (i == rows.nct) if kind == "seg" else (i == 0)

                @pl.when(first)
                def _():
                    o[...] = gr.astype(o.dtype)

                @pl.when(jnp.logical_not(first))
                def _():
                    o[...] += gr.astype(o.dtype)

    T = rows.nt * rows.tm
    out_shape, out_specs = [], []
    for k in diff:
        a = args[k]
        if a.kind == "row":
            out_shape.append(jax.ShapeDtypeStruct((T, a.block[1] * (rows.ncol if _follows(a) else 1)), row_dt[k]))
            fo = _follows(a)
            out_specs.append(pl.BlockSpec(a.block, (lambda j, i, fo=fo: (i, j if fo else 0))))
        else:
            out_shape.append(jax.ShapeDtypeStruct(a.arr.shape, F32))
            out_specs.append(rows.spec(a.block, a.imap))
    for w, dt in primal:
        out_shape.append(jax.ShapeDtypeStruct((T, w), dt))
        out_specs.append(pl.BlockSpec((rows.tm, w), lambda j, i: (i, 0)))
    res, xres = hosted_call(
        body, ex, [a.arr for a in list(args) + list(cots)], name=name, out_shape=out_shape, grid=(rows.ncol, rows.nt),
        in_specs=[rows.spec(a.block, a.imap) for a in list(args) + list(cots)], out_specs=out_specs)
    return res if ex is None else (res, xres)


def _follows(a):
    return a.imap(1, 0, 0)[-1] != a.imap(0, 0, 0)[-1]


def _rms(x):
    return x * lax.rsqrt(jnp.mean(x * x, axis=-1, keepdims=True) + EPS)


def f_norm_mod(x, g, sh, sc):
    return ((_rms(x) * g) * (1.0 + sc) + sh,)


def f_resid_norm_mod(x, mo, ga, g, sh, sc):
    x1 = x + ga * mo
    return x1, (_rms(x1) * g) * (1.0 + sc) + sh


def f_resid(x, dn, ga):
    return (x + ga * dn,)


def f_silu(x):
    return (x * jax.nn.sigmoid(x),)


def f_bias(x, b):
    return (x + b,)


def f_ssd_gate(y0, y1, xs, z, dskip, nw):
    y = y0 + y1 + dskip * xs
    return (_rms(y * (z * jax.nn.sigmoid(z))) * nw,)


def f_pool(u, pmat, pmat_t, inv_cnt, pw, scale):
    pm = _lin01(pmat, pmat_t, u) * inv_cnt - u
    return (_mm(pm, pw) * scale,)


def f_merge(o_ssd, o_pool, gl_ssd, gl_pool):
    return (jax.nn.sigmoid(gl_ssd) * o_ssd + jax.nn.sigmoid(gl_pool) * o_pool,)


def _column_splitter(n):
    @jax.custom_vjp
    def split(x):
        w = x.shape[1] // n
        return tuple(x[:, k * w:(k + 1) * w] for k in range(n))

    def fwd(x):
        return split(x), None

    def bwd(_, g):
        return (jnp.concatenate(g, axis=1),)

    split.defvjp(fwd, bwd)
    return split


_halve_cols = _column_splitter(2)
_quarter_cols = _column_splitter(len(POOL_WINDOWS))


def f_swiglu(gu):
    a, b = _halve_cols(gu)
    return ((a * jax.nn.sigmoid(a)) * b,)


def f_pool_all(u, pmat, pmat_t, inv_cnt, scale, *pws):
    outs = [f_pool(part, pmat[k], pmat_t[k], inv_cnt[k], pws[k], 1.0)[0] for k, part in enumerate(_quarter_cols(u))]
    return (jnp.concatenate(outs, axis=1) * scale,)


def f_loss_resid(x1, dn, ga, tgt, g):
    err = _rms(x1 + ga * dn) * g - tgt
    return (0.5 * jnp.mean(err * err, axis=-1, keepdims=True),)


CONV_TILE = 128


CONV_GAP = 8


def _gapped(v, n_ctx):
    z = jnp.zeros((CONV_GAP, v.shape[1]), v.dtype)
    return jnp.concatenate([v[:n_ctx], z, v[n_ctx:], z], axis=0)


def _ungapped(v, n_ctx):
    return jnp.concatenate([v[:n_ctx], v[n_ctx + CONV_GAP:v.shape[0] - CONV_GAP]], axis=0)


def _shift_rows(v, j):
    return v if j == 0 else pltpu.roll(v, (-j) % v.shape[0], 0)


def conv_fwd(name, proj, conv_w, conv_b, n_ctx, width, ex=None):
    T = proj.shape[0]
    half = SSD_CONV // 2

    def body(u_ref, w_ref, b_ref, o_ref):
        u = _gapped(u_ref[...].astype(F32), n_ctx)
        pre = jnp.broadcast_to(b_ref[...], u.shape)
        for k in range(SSD_CONV):
            pre = pre + w_ref[k:k + 1, :] * _shift_rows(u, k - half)
        o_ref[...] = _ungapped(pre * jax.nn.sigmoid(pre), n_ctx)

    col = lambda t: (0, t)
    res, xres = hosted_call(
        body, ex, [proj, conv_w, conv_b], name=name, out_shape=[jax.ShapeDtypeStruct((T, width), F32)],
        grid=(width // CONV_TILE,),
        in_specs=[pl.BlockSpec((T, CONV_TILE), col), pl.BlockSpec((SSD_CONV, CONV_TILE), col),
                  pl.BlockSpec((1, CONV_TILE), col)],
        out_specs=[pl.BlockSpec((T, CONV_TILE), col)])
    return res[0], xres


def conv_bwd(name, proj, conv_w, conv_b, d_act2, d_skip, n_ctx, width, ex=None):
    T = proj.shape[0]
    half = SSD_CONV // 2

    def body(u_ref, w_ref, b_ref, c0_ref, c1_ref, cs_ref, du_ref, dw_ref, db_ref):
        t = pl.program_id(0)
        u = _gapped(u_ref[...].astype(F32), n_ctx)
        pre = jnp.broadcast_to(b_ref[...], u.shape)
        for k in range(SSD_CONV):
            pre = pre + w_ref[k:k + 1, :] * _shift_rows(u, k - half)
        sg = jax.nn.sigmoid(pre)
        ct = c0_ref[...].astype(F32) + c1_ref[...].astype(F32) + jnp.where(t % 4 < 2, cs_ref[...].astype(F32), 0.0)
        dpre = _gapped(ct, n_ctx) * (sg * (1.0 + pre * (1.0 - sg)))
        du = jnp.zeros_like(u)
        for k in range(SSD_CONV):
            du = du + w_ref[k:k + 1, :] * _shift_rows(dpre, half - k)
            dw_ref[k:k + 1, :] = jnp.sum(dpre * _shift_rows(u, k - half), axis=0, keepdims=True)
        du_ref[...] = _ungapped(du, n_ctx).astype(du_ref.dtype)
        db_ref[...] = jnp.sum(dpre, axis=0, keepdims=True)

    col = lambda t: (0, t)
    skip_col = lambda t: (0, (t // 4) * 2 + jnp.minimum(t % 4, 1))
    res, xres = hosted_call(
        body, ex, [proj, conv_w, conv_b, d_act2[0], d_act2[1], d_skip], name=name,
        out_shape=[jax.ShapeDtypeStruct((T, width), ACT_DTYPE), jax.ShapeDtypeStruct((SSD_CONV, width), F32),
                   jax.ShapeDtypeStruct((1, width), F32)],
        grid=(width // CONV_TILE,),
        in_specs=[pl.BlockSpec((T, CONV_TILE), col), pl.BlockSpec((SSD_CONV, CONV_TILE), col),
                  pl.BlockSpec((1, CONV_TILE), col), pl.BlockSpec((T, CONV_TILE), col),
                  pl.BlockSpec((T, CONV_TILE), col), pl.BlockSpec((T, CONV_TILE), skip_col)],
        out_specs=[pl.BlockSpec((T, CONV_TILE), col), pl.BlockSpec((SSD_CONV, CONV_TILE), col),
                   pl.BlockSpec((1, CONV_TILE), col)])
    return res[0], res[1], res[2], xres


@jax.custom_vjp
def _cumsum_mat(tri, tri_t, a):
    return jnp.dot(tri, a, precision=lax.Precision.HIGHEST, preferred_element_type=F32)


def _cumsum_fwd(tri, tri_t, a):
    return _cumsum_mat(tri, tri_t, a), (tri, tri_t)


def _cumsum_bwd(res, g):
    tri, tri_t = res
    return (jnp.zeros_like(tri), jnp.zeros_like(tri_t),
            jnp.dot(tri_t, g, precision=lax.Precision.HIGHEST, preferred_element_type=F32))


_cumsum_mat.defvjp(_cumsum_fwd, _cumsum_bwd)


def _ssd_dt(dtraw, dt_bias, a_log, tri, tri_t):
    dt_all = jax.nn.softplus(dtraw + dt_bias)
    a_all = dt_all * (-jnp.exp(a_log))
    return dt_all, a_all, _cumsum_mat(tri, tri_t, a_all)


def _ssd_chunk(xs, bm, cm, dt_all, a_all, s_all, s_in, mask, idx0):
    (xs,), (s_in,) = xs, s_in
    Q = xs.shape[0]
    hpg = xs.shape[1] // SSD_HEADDIM
    lane = lax.broadcasted_iota(jnp.int32, dt_all.shape, 1)
    head = lax.broadcasted_iota(jnp.int32, xs.shape, 1) // SSD_HEADDIM
    head1 = lax.broadcasted_iota(jnp.int32, (1, xs.shape[1]), 1) // SSD_HEADDIM

    def pick(v, r):
        return jnp.sum(jnp.where(lane == idx0 + r, v, 0.0), axis=1, keepdims=True)

    def expand(cols, hd):
        out = cols[hpg - 1]
        for r in range(hpg - 2, -1, -1):
            out = jnp.where(hd == r, cols[r], out)
        return out

    def spread(*cols):
        return expand([jnp.broadcast_to(c, xs.shape) for c in cols], head)

    dt_r = [pick(dt_all, r) for r in range(hpg)]
    s_r = [pick(s_all, r) for r in range(hpg)]
    stot_r = [jnp.sum(jnp.where(lane == idx0 + r, a_all, 0.0), keepdims=True).reshape(1, 1) for r in range(hpg)]

    xd = xs * spread(*dt_r)
    cb = _mm_nt(cm, bm)
    weights, stacked = [], []
    for r in range(hpg):
        sm = jnp.broadcast_to(s_r[r], (Q, Q))
        weights.append(cb * jnp.exp(jnp.where(mask, sm - sm.T, NEG)))
        stacked.append(jnp.where(head == r, xd, 0.0))
    y = spread(*[jnp.exp(c) for c in s_r]) * _mm(cm, s_in)
    y = y + _mm(jnp.concatenate(weights, axis=1), jnp.concatenate(stacked, axis=0))
    to_end = spread(*[jnp.exp(t - c) for t, c in zip(stot_r, s_r)])
    carry = expand([jnp.broadcast_to(jnp.exp(t), (1, xs.shape[1])) for t in stot_r], head1)
    s_out = carry * s_in + _mm_tn(bm, xd * to_end)
    return [y], [s_out]


def _scan_consts():
    q = SSD_CHUNK
    i = np.arange(q)[:, None]
    j = np.arange(q)[None, :]
    fwd = (j <= i).astype(np.float32)
    bwd = (j >= i).astype(np.float32)
    tri = np.stack([fwd, bwd])
    return jnp.asarray(tri), jnp.asarray(np.stack([fwd.T, bwd.T]))


def _chunk_of(d, k, ncc, nc):
    rev = jnp.where(k < ncc, ncc - 1 - k, nc - 1 + ncc - k)
    return jnp.where(d == 0, k, rev)


def ssd_fwd(name, xbc, dtraw, dt_bias, a_log, n_ctx, ex=None):
    T = xbc.shape[0]
    q, G = SSD_CHUNK, SSD_GROUPS
    nc, ncc = T // q, n_ctx // q
    gw = xbc.shape[1] // G
    xw = gw - 2 * SSD_STATE
    hpg = xw // SSD_HEADDIM
    nh = G * hpg
    tri, tri_t = _scan_consts()

    gs = SSD_GROUPS_PER_STEP

    def body(x0_ref, x1_ref, dt0_ref, dt1_ref, bias_ref, alog_ref, tri_ref, trit_ref, y0_ref, y1_ref, sin_ref, state):
        gb, k = pl.program_id(0), pl.program_id(1)

        @pl.when(k == 0)
        def _():
            state[...] = jnp.zeros_like(state)

        for d, (x_ref, dt_ref, y_ref) in enumerate(((x0_ref, dt0_ref, y0_ref), (x1_ref, dt1_ref, y1_ref))):
            tri_v = tri_ref[d]
            dt_all, a_all, s_all = _ssd_dt(dt_ref[...], bias_ref[...], alog_ref[...], tri_v, trit_ref[d])
            for j in range(gs):
                o = j * gw
                sin_ref[d, j] = state[d, j]
                (y,), (s_out,) = _ssd_chunk(
                    [x_ref[:, o:o + xw]], x_ref[:, o + xw:o + xw + SSD_STATE], x_ref[:, o + xw + SSD_STATE:o + gw],
                    dt_all, a_all, s_all, [state[d, j]], tri_v > 0.5, d * nh + (gb * gs + j) * hpg)
                y_ref[:, j * xw:(j + 1) * xw] = y.astype(y_ref.dtype)
                state[d, j] = s_out

    ch = lambda d, k: _chunk_of(d, k, ncc, nc)
    y_shape = jax.ShapeDtypeStruct((T, G * xw), ACT_DTYPE)
    res, xres = hosted_call(
        body, ex, [xbc, xbc, dtraw, dtraw, dt_bias, a_log, tri, tri_t], name=name,
        out_shape=[y_shape, y_shape, jax.ShapeDtypeStruct((2, nc, G, SSD_STATE, xw), F32)],
        grid=(G // gs, nc),
        in_specs=[pl.BlockSpec((q, gs * gw), lambda g, k: (ch(0, k), g)),
                  pl.BlockSpec((q, gs * gw), lambda g, k: (ch(1, k), g)),
                  pl.BlockSpec((q, 128), lambda g, k: (ch(0, k), 0)),
                  pl.BlockSpec((q, 128), lambda g, k: (ch(1, k), 0)),
                  pl.BlockSpec((1, 128), lambda g, k: (0, 0)),
                  pl.BlockSpec((1, 128), lambda g, k: (0, 0)),
                  pl.BlockSpec((2, q, q), lambda g, k: (0, 0, 0)),
                  pl.BlockSpec((2, q, q), lambda g, k: (0, 0, 0))],
        out_specs=[pl.BlockSpec((q, gs * xw), lambda g, k: (ch(0, k), g)),
                   pl.BlockSpec((q, gs * xw), lambda g, k: (ch(1, k), g)),
                   pl.BlockSpec((2, None, gs, SSD_STATE, xw), lambda g, k: (0, k, g, 0, 0))],
        scratch_shapes=[pltpu.VMEM((2, gs, SSD_STATE, xw), F32)])
    return res[0], res[1], res[2], xres


def ssd_bwd(name, xbc, dtraw, dt_bias, a_log, states, dy, n_ctx, ex=None):
    T = xbc.shape[0]
    q, G = SSD_CHUNK, SSD_GROUPS
    nc, ncc = T // q, n_ctx // q
    gw = xbc.shape[1] // G
    xw = gw - 2 * SSD_STATE
    hpg = xw // SSD_HEADDIM
    nh = G * hpg
    tri, tri_t = _scan_consts()

    gs = SSD_GROUPS_PER_STEP

    def body(x0_ref, x1_ref, dt0_ref, dt1_ref, bias_ref, alog_ref, tri_ref, trit_ref, sin_ref, dy0_ref, dy1_ref,
             dx0_ref, dx1_ref, ddt_ref, dbias_ref, dalog_ref, dstate):
        gb, k = pl.program_id(0), pl.program_id(1)

        @pl.when((gb == 0) & (k == 0))
        def _():
            ddt_ref[...] = jnp.zeros_like(ddt_ref)
            dbias_ref[...] = jnp.zeros_like(dbias_ref)
            dalog_ref[...] = jnp.zeros_like(dalog_ref)

        @pl.when(k == 0)
        def _():
            dstate[...] = jnp.zeros_like(dstate)

        tris = [(tri_ref[d], trit_ref[d]) for d in range(2)]
        per = 4

        def fn(bias, alog, dtraw0, dtraw1, *per_group):
            ys, s_outs = [], []
            for d, dtraw in enumerate((dtraw0, dtraw1)):
                tri_v, trit_v = tris[d]
                dt_all, a_all, s_all = _ssd_dt(dtraw, bias, alog, tri_v, trit_v)
                for j in range(gs):
                    xs, bm, cm, s_in = per_group[per * (d * gs + j):per * (d * gs + j + 1)]
                    y, s_out = _ssd_chunk([xs], bm, cm, dt_all, a_all, s_all, [s_in], tri_v > 0.5,
                                          d * nh + (gb * gs + j) * hpg)
                    ys += y
                    s_outs += s_out
            return ys, s_outs

        per_group, dys, dss = [], [], []
        for d, (x_ref, dy_ref) in enumerate(((x0_ref, dy0_ref), (x1_ref, dy1_ref))):
            for j in range(gs):
                o = j * gw
                per_group += [x_ref[:, o:o + xw], x_ref[:, o + xw:o + xw + SSD_STATE], x_ref[:, o + xw + SSD_STATE:o + gw],
                              sin_ref[d, j]]
                dys.append(dy_ref[:, j * xw:(j + 1) * xw].astype(F32))
                dss.append(dstate[d, j])
        _, vjp = jax.vjp(fn, bias_ref[...], alog_ref[...], dt0_ref[...], dt1_ref[...], *per_group)
        cts = vjp((dys, dss))
        dbias, dalog, ddt0, ddt1 = cts[:4]
        for d, dx_ref in enumerate((dx0_ref, dx1_ref)):
            for j in range(gs):
                o = j * gw
                dxs, dbm, dcm, ds_in = cts[4 + per * (d * gs + j):4 + per * (d * gs + j + 1)]
                dx_ref[:, o:o + xw] = dxs.astype(dx_ref.dtype)
                dx_ref[:, o + xw:o + xw + SSD_STATE] = dbm.astype(dx_ref.dtype)
                dx_ref[:, o + xw + SSD_STATE:o + gw] = dcm.astype(dx_ref.dtype)
                dstate[d, j] = ds_in
        for d, ddt in enumerate((ddt0, ddt1)):
            row0 = pl.multiple_of(_chunk_of(d, nc - 1 - k, ncc, nc) * q, q)
            ddt_ref[pl.ds(row0, q), :] += ddt
        dbias_ref[...] += dbias
        dalog_ref[...] += dalog

    ch = lambda d, k: _chunk_of(d, nc - 1 - k, ncc, nc)
    dx_shape = jax.ShapeDtypeStruct((T, G * gw), ACT_DTYPE)
    res, xres = hosted_call(
        body, ex, [xbc, xbc, dtraw, dtraw, dt_bias, a_log, tri, tri_t, states, dy, dy], name=name,
        out_shape=[dx_shape, dx_shape, jax.ShapeDtypeStruct((T, 128), F32),
                   jax.ShapeDtypeStruct((1, 128), F32), jax.ShapeDtypeStruct((1, 128), F32)],
        grid=(G // gs, nc),
        in_specs=[pl.BlockSpec((q, gs * gw), lambda g, k: (ch(0, k), g)),
                  pl.BlockSpec((q, gs * gw), lambda g, k: (ch(1, k), g)),
                  pl.BlockSpec((q, 128), lambda g, k: (ch(0, k), 0)),
                  pl.BlockSpec((q, 128), lambda g, k: (ch(1, k), 0)),
                  pl.BlockSpec((1, 128), lambda g, k: (0, 0)),
                  pl.BlockSpec((1, 128), lambda g, k: (0, 0)),
                  pl.BlockSpec((2, q, q), lambda g, k: (0, 0, 0)),
                  pl.BlockSpec((2, q, q), lambda g, k: (0, 0, 0)),
                  pl.BlockSpec((2, None, gs, SSD_STATE, xw), lambda g, k: (0, nc - 1 - k, g, 0, 0)),
                  pl.BlockSpec((q, gs * xw), lambda g, k: (ch(0, k), g)),
                  pl.BlockSpec((q, gs * xw), lambda g, k: (ch(1, k), g))],
        out_specs=[pl.BlockSpec((q, gs * gw), lambda g, k: (ch(0, k), g)),
                   pl.BlockSpec((q, gs * gw), lambda g, k: (ch(1, k), g)),
                   pl.BlockSpec((T, 128), lambda g, k: (0, 0)),
                   pl.BlockSpec((1, 128), lambda g, k: (0, 0)),
                   pl.BlockSpec((1, 128), lambda g, k: (0, 0))],
        scratch_shapes=[pltpu.VMEM((2, gs, SSD_STATE, xw), F32)])
    return res[0], res[1], res[2], res[3], res[4], xres


def _perm_xbc(a):
    G = SSD_GROUPS
    n = a.shape[-1]
    gn = G * SSD_STATE
    di = n - 2 * gn
    lead = a.shape[:-1]
    xs = a[..., :di].reshape(lead + (G, di // G))
    bm = a[..., di:di + gn].reshape(lead + (G, SSD_STATE))
    cm = a[..., di + gn:].reshape(lead + (G, SSD_STATE))
    return jnp.concatenate([xs, bm, cm], axis=-1).reshape(lead + (n,))


def _unperm_xbc(a):
    G = SSD_GROUPS
    n = a.shape[-1]
    gn = G * SSD_STATE
    di = n - 2 * gn
    lead = a.shape[:-1]
    r = a.reshape(lead + (G, n // G))
    xw = di // G
    return jnp.concatenate([r[..., :xw].reshape(lead + (di,)), r[..., xw:xw + SSD_STATE].reshape(lead + (gn,)),
                            r[..., xw + SSD_STATE:].reshape(lead + (gn,))], axis=-1)


def _pool_consts(tm, n_ctx):
    assert n_ctx == tm and tm % GRID_W == 0
    mats, cnts = [], []
    for seq in (n_ctx, GRID_W):
        t = np.arange(tm)
        tt = t % seq
        base = t - tt
        ms, cs = [], []
        for k in POOL_WINDOWS:
            lo = np.clip(tt - k // 2, 0, seq) + base
            hi = np.clip(tt + k // 2, 0, seq) + base
            m = ((t[None, :] >= lo[:, None]) & (t[None, :] < hi[:, None])).astype(np.float32)
            ms.append(m)
            cs.append((1.0 / (hi - lo).astype(np.float32))[:, None])
        mats.append(np.stack(ms))
        cnts.append(np.stack(cs))
    m = np.stack(mats)
    return jnp.asarray(m), jnp.asarray(np.swapaxes(m, -1, -2)), jnp.asarray(np.stack(cnts).astype(np.float32))


def _prep_layer_weights(w_ada, b_ada, g_mix, w_in, conv_w, conv_b, dt_bias, a_log, d_skip, ssd_norm_w, w_ssd_out,
                        pool_w, pool_scale, w_pool_out, w_out, g_ffn, w_gate_up, w_down):
    D = w_in.shape[0]
    di = ssd_norm_w.shape[0]
    xbc = conv_w.shape[1]
    nh2 = dt_bias.size
    pw = pool_scale.shape[0]
    o = 0
    wz = w_in[:, o:o + di]; o += di
    wx = w_in[:, o:o + xbc]; o += xbc
    wdt = w_in[:, o:o + nh2]; o += nh2
    wp = w_in[:, o:o + pw]; o += pw
    wg = w_in[:, o:]
    w1 = jnp.concatenate([_perm_xbc(wx), wz, wg, wp, wdt, jnp.zeros((D, DT_PAD - nh2), w_in.dtype)], axis=1)
    pad128 = lambda v: jnp.concatenate([v.reshape(1, -1), jnp.zeros((1, 128 - v.size), F32)], axis=1)
    return dict(
        w_ada=w_ada, b_ada=b_ada.reshape(1, -1), g_mix=g_mix.reshape(1, -1), w1=w1,
        conv_w=_perm_xbc(conv_w), conv_b=_perm_xbc(conv_b.reshape(1, -1)),
        dt_bias=pad128(dt_bias), a_log=pad128(a_log),
        dskip=jnp.repeat(d_skip[0] + d_skip[1], SSD_HEADDIM).reshape(1, -1),
        ssd_norm_w=ssd_norm_w.reshape(1, -1), w_ssd_out=w_ssd_out, pool_w=pool_w,
        pool_scale=pool_scale.reshape(1, -1), w_pool_out=w_pool_out, w_out=w_out, g_ffn=g_ffn.reshape(1, -1),
        w_gate_up=w_gate_up, w_down=w_down)


def _unprep_layer_grads(g, dims):
    di, xbc, nh2, pw = dims
    dxbc, dz, dgs, dgp, dp, ddt = g["w1"]
    r = dxbc.reshape(SSD_GROUPS, xbc // SSD_GROUPS, dxbc.shape[1])
    xw = di // SSD_GROUPS
    parts = [r[:, :xw], r[:, xw:xw + SSD_STATE], r[:, xw + SSD_STATE:]]
    w_in_t = jnp.concatenate([dz] + [p.reshape(-1, dxbc.shape[1]) for p in parts] + [ddt[:nh2], dp, dgs, dgp], axis=0)
    nh = nh2 // 2
    dsk = g["dskip"].reshape(nh, SSD_HEADDIM).sum(axis=1)
    return dict(
        w_ada=g["w_ada"], b_ada=g["b_ada"].reshape(-1), g_mix=g["g_mix"].reshape(-1),
        w_in=w_in_t,
        conv_w=_unperm_xbc(g["conv_w"]), conv_b=_unperm_xbc(g["conv_b"]).reshape(-1),
        dt_bias=g["dt_bias"][0, :nh2].reshape(2, nh), a_log=g["a_log"][0, :nh2].reshape(2, nh),
        d_skip=jnp.stack([dsk, dsk]), ssd_norm_w=g["ssd_norm_w"].reshape(-1), w_ssd_out=g["w_ssd_out"],
        pool_w=g["pool_w"], pool_scale=g["pool_scale"].reshape(-1), w_pool_out=g["w_pool_out"], w_out=g["w_out"],
        g_ffn=g["g_ffn"].reshape(-1), w_gate_up=g["w_gate_up"], w_down=g["w_down"])


COND_ROWS = 16


def _split_mods(m):
    d = m.shape[1] // 6
    return [m[:2, k * d:(k + 1) * d].reshape(2, 1, d) for k in range(6)]


def _pool_args(rows, proj, col_block, width, pc, w):
    seg_const = lambda a: Arg(a, (None,) + a.shape[1:], lambda j, i, s: (s, 0, 0, 0), "const")
    pws = [Arg(w["pool_w"][k], w["pool_w"].shape[1:], lambda j, i, s: (0, 0), "acc") for k in range(w["pool_w"].shape[0])]
    return [rows.row(proj, width, col_block)] + [seg_const(a) for a in pc] + [rows.vec(w["pool_scale"])] + pws


TALL_ROW_TILE = 1088


def _tall_rows(T, ncol):
    tm = max(t for t in range(16, min(T, TALL_ROW_TILE) + 1, 16) if T % t == 0)
    return Rows(T // tm, 0, tm, ncol)


def _hosted(hosts, box, key):
    fn = (hosts or {}).get(key)
    return fn(box) if fn else None


def _layer_fwd(l, pre, cond_s, w, rows, n_ctx, pc, hosts=None, box=None):
    T, D = pre[0].shape if isinstance(pre, tuple) else pre.shape
    nt, nct, tm = rows.nt, rows.nct, rows.tm
    n = lambda s: f"l{l}_{s}"
    crow = Rows(1, 0, COND_ROWS)
    mraw = matmul_nn(n("ada_mm"), cond_s, w["w_ada"])
    (m,) = stage_fwd(n("ada_bias"), f_bias, crow, [crow.row(mraw, mraw.shape[1]), crow.vec(w["b_ada"])],
                     [(mraw.shape[1], F32, False)])
    sh1, sc1, ga1, sh2, sc2, ga2 = _split_mods(m)

    if isinstance(pre, tuple):
        x, h1 = stage_fwd(n("norm1"), f_resid_norm_mod, rows, _resid_norm_args(rows, pre, w["g_mix"], sh1, sc1, D),
                          [(D, F32, False), (D, ACT_DTYPE, False)])
    else:
        x = pre
        (h1,) = stage_fwd(n("norm1"), f_norm_mod, rows,
                          [rows.row(x, D), rows.vec(w["g_mix"]), rows.segvec(sh1), rows.segvec(sc1)],
                          [(D, ACT_DTYPE, False)])
    xbc_w = w["conv_w"].shape[1]
    di = w["ssd_norm_w"].shape[1]
    pw = w["pool_scale"].shape[1]
    c_z, c_g, c_p, c_dt = xbc_w, xbc_w + di, xbc_w + di + 2 * pw, xbc_w + di + 3 * pw
    ex = _hosted(hosts, box, "in_mm")
    proj = matmul_nn(n("in_mm"), h1, w["w1"], out_dtype=ACT_DTYPE, ex=ex, ncols=c_dt)
    if ex is not None:
        proj, box["in_mm"] = proj
    dtraw = matmul_nn(n("in_dt_mm"), h1, w["w1"], col0=c_dt, ncols=128)
    ex = _hosted(hosts, box, "conv")
    xbc, xres = conv_fwd(n("conv"), proj, w["conv_w"], w["conv_b"], n_ctx, xbc_w, ex)
    if ex is not None:
        box["conv"] = xres
    ex = _hosted(hosts, box, "ssd")
    y0, y1, states, xres = ssd_fwd(n("ssd"), xbc, dtraw, w["dt_bias"], w["a_log"], n_ctx, ex)
    y2 = (y0, y1)
    if ex is not None:
        box["ssd"] = xres

    G = SSD_GROUPS
    gw = di // G
    r8 = _tall_rows(T, G)
    gate_args = [r8.row(y2[0], gw, 0, True), r8.row(y2[1], gw, 0, True), r8.row(xbc, gw, 0, True, stride=2),
                 r8.row(proj, gw, c_z // gw, True), r8.vec(w["dskip"], True), r8.vec(w["ssd_norm_w"], True)]
    (ynw,) = stage_fwd(n("ssd_gate"), f_ssd_gate, r8, gate_args, [(gw, ACT_DTYPE, True)])
    ex = _hosted(hosts, box, "ssd_out_mm")
    o_ssd = matmul_nn(n("ssd_out_mm"), ynw, w["w_ssd_out"], ex=ex)
    if ex is not None:
        o_ssd, box["ssd_out_mm"] = o_ssd

    nw = len(POOL_WINDOWS)
    pg = pw // nw
    (ps,) = stage_fwd(n("pool"), f_pool_all, rows, _pool_args(rows, proj, c_p // pw, pw, pc, w), [(pw, ACT_DTYPE, False)])
    o_pool = matmul_nn(n("pool_out_mm"), ps, w["w_pool_out"])

    merge_args = [rows.row(o_ssd, D), rows.row(o_pool, D), rows.row(proj, pw, c_g // pw), rows.row(proj, pw, c_g // pw + 1)]
    (mg,) = stage_fwd(n("merge"), f_merge, rows, merge_args, [(D, ACT_DTYPE, False)])
    mo = matmul_nn(n("out_mm"), mg, w["w_out"])

    rn_args = [rows.row(x, D), rows.row(mo, D), rows.segvec(ga1), rows.vec(w["g_ffn"]), rows.segvec(sh2), rows.segvec(sc2)]
    x1, h2 = stage_fwd(n("norm2"), f_resid_norm_mod, rows, rn_args, [(D, F32, False), (D, ACT_DTYPE, False)])
    ex = _hosted(hosts, box, "gate_up_mm")
    gu = matmul_nn(n("gate_up_mm"), h2, w["w_gate_up"], ex=ex)
    if ex is not None:
        gu, box["gate_up_mm"] = gu
    fh = gu.shape[1] // 2
    (act,) = stage_fwd(n("swiglu"), f_swiglu, rows, [rows.row(gu, 2 * fh)], [(fh, ACT_DTYPE, False)])
    ex = _hosted(hosts, box, "down_mm")
    dn = matmul_nn(n("down_mm"), act, w["w_down"], ex=ex)
    if ex is not None:
        dn, box["down_mm"] = dn
    saved = dict(x=x, pre=pre, mraw=mraw, mods=(sh1, sc1, ga1, sh2, sc2, ga2), h1=h1, proj=proj, dtraw=dtraw, xbc=xbc, y2=y2,
                 states=states,
                 ynw=ynw, o_ssd=o_ssd, ps=ps, o_pool=o_pool, mg=mg, mo=mo, x1=x1, h2=h2, gu=gu, act=act, dn=dn,
                 cols=(c_z, c_g, c_p, c_dt))
    return (x1, dn, ga2), saved


def _resid_norm_args(rows, pre, g, sh, sc, D):
    x1, dn, ga2 = pre
    return [rows.row(x1, D), rows.row(dn, D), rows.segvec(ga2), rows.vec(g), rows.segvec(sh), rows.segvec(sc)]


def f_norm_mod_keep(x, g, sh, sc):
    return f_norm_mod(x, g, sh, sc)[0], x


def _layer_bwd(l, cot, cond_s, w, s, rows, n_ctx, pc, hosts=None, box=None):
    dx1, ddn, dga2 = cot
    T, D = dx1.shape
    nt, nct, tm = rows.nt, rows.nct, rows.tm
    n = lambda t: f"l{l}_{t}_bwd"
    sh1, sc1, ga1, sh2, sc2, ga2 = s["mods"]
    c_z, c_g, c_p, c_dt = s["cols"]
    x, proj, xbc, y2, gu = s["x"], s["proj"], s["xbc"], s["y2"], s["gu"]
    g = {}
    if box is not None:
        box["g"] = g

    ex = _hosted(hosts, box, "down_dx")
    dact = matmul_nt(n("down_dx"), ddn, w["w_down"], ex=ex)
    if ex is not None:
        dact, box["down_dx"] = dact
    ex = _hosted(hosts, box, "down_dw")
    g["w_down"] = matmul_tn(n("down_dw"), s["act"], ddn, ex=ex)
    if ex is not None:
        g["w_down"], box["down_dw"] = g["w_down"]
    fh = gu.shape[1] // 2
    (dgu,) = stage_bwd(n("swiglu"), f_swiglu, rows, [rows.row(gu, 2 * fh)], [rows.row(dact, fh)], [ACT_DTYPE])
    dh2 = matmul_nt(n("gate_up_dx"), dgu, w["w_gate_up"])
    g["w_gate_up"] = matmul_tn(n("gate_up_dw"), s["h2"], dgu, blocks=w["w_gate_up"].shape[0])

    rn_args = [rows.row(x, D), rows.row(s["mo"], D), rows.segvec(ga1), rows.vec(w["g_ffn"]), rows.segvec(sh2), rows.segvec(sc2)]
    dxr, dmo, dga1, g["g_ffn"], dsh2, dsc2 = stage_bwd(
        n("norm2"), f_resid_norm_mod, rows, rn_args, [rows.row(dx1, D), rows.row(dh2, D)], [F32, ACT_DTYPE])
    dmg = matmul_nt(n("out_dx"), dmo, w["w_out"])
    g["w_out"] = matmul_tn(n("out_dw"), s["mg"], dmo)

    pw = w["pool_scale"].shape[1]
    merge_args = [rows.row(s["o_ssd"], D), rows.row(s["o_pool"], D), rows.row(proj, pw, c_g // pw), rows.row(proj, pw, c_g // pw + 1)]
    do_ssd, do_pool, dgl_s, dgl_p = stage_bwd(n("merge"), f_merge, rows, merge_args, [rows.row(dmg, D)], [ACT_DTYPE] * 4)
    dps = matmul_nt(n("pool_out_dx"), do_pool, w["w_pool_out"])
    g["w_pool_out"] = matmul_tn(n("pool_out_dw"), s["ps"], do_pool)

    nw = len(POOL_WINDOWS)
    pg = pw // nw
    du_pool, g["pool_scale"], *dpw = stage_bwd(n("pool"), f_pool_all, rows, _pool_args(rows, proj, c_p // pw, pw, pc, w),
                                               [rows.row(dps, pw)], [ACT_DTYPE])
    g["pool_w"] = jnp.stack(dpw)

    dynw = matmul_nt(n("ssd_out_dx"), do_ssd, w["w_ssd_out"])
    g["w_ssd_out"] = matmul_tn(n("ssd_out_dw"), s["ynw"], do_ssd)
    G = SSD_GROUPS
    di = w["ssd_norm_w"].shape[1]
    gw = di // G
    r8 = _tall_rows(T, G)
    gate_args = [r8.row(y2[0], gw, 0, True), r8.row(y2[1], gw, 0, True), r8.row(xbc, gw, 0, True, stride=2),
                 r8.row(proj, gw, c_z // gw, True), r8.vec(w["dskip"], True), r8.vec(w["ssd_norm_w"], True)]
    gate_args[1].kind = "const"
    ex = _hosted(hosts, box, "ssd_gate")
    res = stage_bwd(n("ssd_gate"), f_ssd_gate, r8, gate_args, [r8.row(dynw, gw, 0, True)], [ACT_DTYPE] * 3, ex)
    if ex is not None:
        res, box["ssd_gate"] = res
    dy, dxs_skip, dz, g["dskip"], g["ssd_norm_w"] = res

    ex = _hosted(hosts, box, "ssd")
    dxbc0, dxbc1, ddt, g["dt_bias"], g["a_log"], xres = ssd_bwd(n("ssd"), xbc, s["dtraw"], w["dt_bias"], w["a_log"],
                                                                s["states"], dy, n_ctx, ex)
    dxbc2 = (dxbc0, dxbc1)
    if ex is not None:
        box["ssd"] = xres
    xbc_w = xbc.shape[1]
    ex = _hosted(hosts, box, "conv")
    dxbc_raw, g["conv_w"], g["conv_b"], xres = conv_bwd(n("conv"), proj, w["conv_w"], w["conv_b"], dxbc2, dxs_skip,
                                                         n_ctx, xbc_w, ex)
    if ex is not None:
        box["conv"] = xres
    pieces = [dxbc_raw, dz, dgl_s, dgl_p, du_pool, ddt]
    offsets = [0, c_z, c_g, c_g + pw, c_p, c_dt]
    ex = _hosted(hosts, box, "in_dx")
    dh1 = matmul_nt(n("in_dx"), pieces, w["w1"], ex=ex, offsets=offsets)
    if ex is not None:
        dh1, box["in_dx"] = dh1
    ex = _hosted(hosts, box, "in_dw")
    first = matmul_tn(n("in_dw0"), pieces[0], s["h1"], ex=ex)
    if ex is not None:
        first, box["in_dw"] = first
    g["w1"] = [first] + [matmul_tn(n(f"in_dw{k}"), p, s["h1"]) for k, p in enumerate(pieces) if k]

    if isinstance(s["pre"], tuple):
        dx1p, ddnp, dga2p, g["g_mix"], dsh1, dsc1 = stage_bwd(
            n("norm1"), f_resid_norm_mod, rows, _resid_norm_args(rows, s["pre"], w["g_mix"], sh1, sc1, D),
            [rows.row(dxr, D), rows.row(dh1, D)], [F32, ACT_DTYPE])
        dx = (dx1p, ddnp, dga2p)
    else:
        n1_args = [rows.row(x, D), rows.vec(w["g_mix"]), rows.segvec(sh1), rows.segvec(sc1)]
        dx, g["g_mix"], dsh1, dsc1 = stage_bwd(n("norm1"), f_norm_mod_keep, rows, n1_args,
                                               [rows.row(dh1, D), rows.row(dxr, D)], [F32])

    dm = jnp.concatenate([v.reshape(2, D) for v in (dsh1, dsc1, dga1, dsh2, dsc2, dga2)], axis=1)
    dm = jnp.concatenate([dm, jnp.zeros((COND_ROWS - 2, dm.shape[1]), F32)], axis=0)
    crow = Rows(1, 0, COND_ROWS)
    dmraw, g["b_ada"] = stage_bwd(n("ada_bias"), f_bias, crow, [crow.row(s["mraw"], dm.shape[1]), crow.vec(w["b_ada"])],
                                  [crow.row(dm, dm.shape[1])], [ACT_DTYPE])
    dcs = matmul_nt(n("ada_dx"), dmraw, w["w_ada"])
    g["w_ada"] = matmul_tn(n("ada_dw"), cond_s, dmraw, blocks=w["w_ada"].shape[0])
    return dx, dcs, g


def local_step(x, ctx, c, c_ctx, target, layer_w_fn, n_layers, g_final, fwd_hosts=None, bwd_hosts=None):
    L, D = x.shape
    n_ctx = ctx.shape[0]
    tm = ROW_TILE
    T = L + n_ctx
    rows = Rows(T // tm, n_ctx // tm, tm)
    pc = _pool_consts(tm, n_ctx)
    xa = jnp.concatenate([ctx, x], axis=0)
    cond = jnp.concatenate([c_ctx.reshape(1, D), c.reshape(1, D), jnp.zeros((COND_ROWS - 2, D), F32)], axis=0)
    crow = Rows(1, 0, COND_ROWS)
    (cond_s,) = stage_fwd("cond_silu", f_silu, crow, [crow.row(cond, D)], [(D, ACT_DTYPE, False)])

    saved, layer_w = [], []
    for l in range(n_layers):
        layer_w.append(layer_w_fn(l))
        box = {}
        xa, s = _layer_fwd(l, xa, cond_s, layer_w[l], rows, n_ctx, pc, fwd_hosts(l, box) if fwd_hosts else None, box)
        saved.append(s)

    x1, dn, ga2 = xa
    rl = Rows(L // tm, 0, tm)
    gf = g_final.reshape(1, D)
    tgt = rl.row(target, D)
    tgt.kind = "const"
    off = n_ctx // tm
    loss_args = [rl.row(x1, D, roff=off), rl.row(dn, D, roff=off), rl.vec(ga2[1]), tgt, rl.vec(gf)]
    ones = jnp.ones((L, 1), F32)
    dx1_lat, ddn_lat, dga2_lat, dgf, loss_rows = stage_bwd("loss", f_loss_resid, rl, loss_args, [rl.row(ones, 1)],
                                                           [F32, ACT_DTYPE], primal=[(1, F32)])
    loss = jnp.sum(loss_rows)
    cot = (jnp.concatenate([jnp.zeros((n_ctx, D), F32), dx1_lat], axis=0),
           jnp.concatenate([jnp.zeros((n_ctx, D), ACT_DTYPE), ddn_lat], axis=0),
           jnp.stack([jnp.zeros((1, D), F32), dga2_lat]))

    grads = [None] * n_layers
    dcs = jnp.zeros((COND_ROWS, D), F32)
    for l in reversed(range(n_layers)):
        box = {}
        hosts = bwd_hosts(l, grads, box) if bwd_hosts else None
        cot, dcs_l, grads[l] = _layer_bwd(l, cot, cond_s, layer_w[l], saved[l], rows, n_ctx, pc, hosts, box)
        dcs = dcs + dcs_l
    dx = cot
    (dcond,) = stage_bwd("cond_silu_bwd", f_silu, crow, [crow.row(cond, D)], [crow.row(dcs, D)], [F32])
    return loss, dx[n_ctx:], grads, dcond[0], dgf


def gather_chips(halves, conv=None):
    n = len(halves)
    ops = list(halves) + ([conv] if conv is not None else [])

    def copies(ins, outs, pos):
        c, me = pos[2], _chip_index(pos)
        pairs = [(s.at[c], o.at[me, c]) for s, o in zip(ins[:n], outs[:n])]
        pairs += [(s, o.at[me]) for s, o in zip(ins[n:], outs[n:])]
        return pairs, [(s, d, _flip(pos, rel)) for rel in PLANE for s, d in pairs]

    shapes = [jax.ShapeDtypeStruct((4,) + s.shape, s.dtype) for s in ops]
    return Exchange(copies, 3 * len(ops), len(ops), ops, shapes)


def gather_pair(gathered):
    n = len(gathered)

    def copies(ins, outs, pos):
        c = pos[2]
        return [], [(s.at[b, c], o.at[b, c], _flip(pos, PAIR[0])) for s, o in zip(ins, outs) for b in range(4)]

    shapes = [jax.ShapeDtypeStruct(g.shape, g.dtype) for g in gathered]
    return Exchange(copies, 4 * n, 0, gathered, shapes, aliases={k: k for k in range(n)})


def swap_halves(grads):
    n = len(grads)

    def copies(ins, outs, pos):
        c = pos[2]
        return [], [(g.at[b, 1 - c], o.at[b], _flip(pos, PAIR[0])) for g, o in zip(ins, outs) for b in range(4)]

    shapes = [jax.ShapeDtypeStruct((g.shape[0],) + g.shape[2:], g.dtype) for g in grads]
    return Exchange(copies, 4 * n, 0, grads, shapes)


def scatter_chips(sums):
    n = len(sums)

    def copies(ins, outs, pos):
        me = _chip_index(pos)
        local = [(p.at[me], o.at[me]) for p, o in zip(ins, outs)]
        remote = []
        for rel in PLANE:
            peer = _flip(pos, rel)
            remote += [(p.at[_chip_index(peer)], o.at[me], peer) for p, o in zip(ins, outs)]
        return local, remote

    shapes = [jax.ShapeDtypeStruct(p.shape, p.dtype) for p in sums]
    return Exchange(copies, 3 * n, n, sums, shapes)


def share_halves(finals):
    n = len(finals)

    def copies(ins, outs, pos):
        c = pos[2]
        return [], [(f.at[c], o.at[c], _flip(pos, PAIR[0])) for f, o in zip(ins, outs)]

    shapes = [jax.ShapeDtypeStruct(f.shape, f.dtype) for f in finals]
    return Exchange(copies, n, 0, finals, shapes, aliases={k: k for k in range(n)})


def gather_everyone(vec):
    def copies(ins, outs, pos):
        me = _device_index(pos)
        (v,), (o,) = ins, outs
        return [(v, o.at[me])], [(v, o.at[me], _flip(pos, rel)) for rel in EVERYONE]

    return Exchange(copies, len(EVERYONE), 1, [vec], [jax.ShapeDtypeStruct((8,) + vec.shape, vec.dtype)])


def _row_tile(rows, cols, n_bufs, mult=8):
    cap = VMEM_LIMIT_BYTES // 2 // (2 * n_bufs * cols * 4)
    for t in range(min(rows, cap) // mult * mult, 0, -mult):
        if rows % t == 0:
            return t
    return rows


def _adamw_update(w, g, m, v):
    nm = ADAM_B1 * m + (1.0 - ADAM_B1) * g
    nv = ADAM_B2 * v + (1.0 - ADAM_B2) * jnp.square(g)
    m_hat = nm / (1.0 - ADAM_B1 ** ADAM_STEP)
    v_hat = nv / (1.0 - ADAM_B2 ** ADAM_STEP)
    return -ADAM_LR * (m_hat / (jnp.sqrt(v_hat) + ADAM_EPS) + ADAM_WD * w), nm, nv


def adamw_small(name, ws, gs, ms, vs):
    n = len(ws)

    def body(*refs):
        ins, outs = refs[:4 * n], refs[4 * n:]
        for k in range(n):
            d, nm, nv = _adamw_update(ins[k][...], ins[n + k][...], ins[2 * n + k][...], ins[3 * n + k][...])
            outs[k][...] = d
            outs[n + k][...] = nm
            outs[2 * n + k][...] = nv

    shapes = [jax.ShapeDtypeStruct(a.shape, F32) for a in ws]
    vmem = pl.BlockSpec(memory_space=pltpu.VMEM)
    res = _pcall(body, name=name, out_shape=shapes * 3, in_specs=[vmem] * (4 * n), out_specs=[vmem] * (3 * n),
                 compiler_params=pltpu.CompilerParams(vmem_limit_bytes=VMEM_LIMIT_BYTES))(*ws, *gs, *ms, *vs)
    return res[:n], res[n:2 * n], res[2 * n:]


WIRE_DTYPE = jnp.bfloat16


def add_own_half(name, grads, recv, c):
    nb, _, R, C = grads.shape
    tr = _row_tile(R, C, 3, mult=16)

    def body(c_ref, g_ref, r_ref, o_ref):
        o_ref[...] = (g_ref[...] + r_ref[...]).astype(o_ref.dtype)

    spec = pl.BlockSpec((None, tr, C), lambda b, i, c_ref: (b, i, 0))
    return _pcall(
        body, name=name, out_shape=jax.ShapeDtypeStruct(recv.shape, WIRE_DTYPE),
        grid_spec=pltpu.PrefetchScalarGridSpec(
            num_scalar_prefetch=1, grid=(nb, R // tr),
            in_specs=[pl.BlockSpec((None, None, tr, C), lambda b, i, c_ref: (b, c_ref[0], i, 0)), spec],
            out_specs=spec),
        compiler_params=_params("parallel", "parallel"),
    )(c, grads, recv)


def sum_slots(name, a, c=None):
    n, R, C = a.shape
    tr = _row_tile(R, C, n + 1, mult=16 if a.dtype.itemsize == 2 else 8)

    def body(*refs):
        a_ref, o_ref = refs[-2:]
        acc = a_ref[0].astype(F32)
        for k in range(1, n):
            acc = acc + a_ref[k].astype(F32)
        o_ref[...] = acc

    if c is None:
        return _pcall(
            body, name=name, out_shape=jax.ShapeDtypeStruct((R, C), F32), grid=(R // tr,),
            in_specs=[pl.BlockSpec((n, tr, C), lambda i: (0, i, 0))], out_specs=pl.BlockSpec((tr, C), lambda i: (i, 0)),
            compiler_params=_params("parallel"),
        )(a)
    return _pcall(
        body, name=name, out_shape=jax.ShapeDtypeStruct((2, R, C), F32),
        grid_spec=pltpu.PrefetchScalarGridSpec(
            num_scalar_prefetch=1, grid=(R // tr,),
            in_specs=[pl.BlockSpec((n, tr, C), lambda i, c_ref: (0, i, 0))],
            out_specs=pl.BlockSpec((None, tr, C), lambda i, c_ref: (c_ref[0], i, 0))),
        compiler_params=_params("parallel"),
    )(c, a)


def adamw(name, w, g_layers, m, v):
    nl, R, C = w.shape
    assert len(g_layers) == nl
    tr = _row_tile(R, C, 8 + nl)
    nr = R // tr

    def body(*refs):
        w_ref, m_ref, v_ref = refs[:3]
        g_refs = refs[3:3 + nl]
        go_ref, d_ref, nm_ref, nv_ref = refs[3 + nl:]
        l = pl.program_id(0)
        gr = g_refs[0][...]
        for k in range(1, nl):
            gr = jnp.where(l == k, g_refs[k][...], gr)
        d_ref[...], nm_ref[...], nv_ref[...] = _adamw_update(w_ref[...], gr, m_ref[...], v_ref[...])
        go_ref[...] = gr

    spec = pl.BlockSpec((None, tr, C), lambda l, i: (l, i, 0))
    g_specs = [pl.BlockSpec((tr, C), (lambda l, i, k=k: (jnp.where(l == k, i, jnp.where(l < k, 0, nr - 1)), 0)))
               for k in range(nl)]
    return _pcall(
        body, name=name, out_shape=[jax.ShapeDtypeStruct((nl, R, C), F32)] * 4, grid=(nl, nr),
        in_specs=[spec] * 3 + g_specs, out_specs=[spec] * 4, compiler_params=_params("arbitrary", "arbitrary"),
    )(w, m, v, *g_layers)


BIG = ("w_ada", "w_in", "w_ssd_out", "pool_w", "w_pool_out", "w_out", "w_gate_up", "w_down")
COL_SHARDED = ("w_ada", "w_in", "w_gate_up")
BLOCK_LAYOUT = ("w_ada", "w_gate_up")
GRAD_TRANSPOSED = ("w_in",)
FIRST_USED = ("w_ada", "w_in")
MID_USED = ("w_ssd_out", "pool_w", "w_pool_out", "w_out")
END_USED = ("w_gate_up", "w_down")
LATER_USED = MID_USED + END_USED
assert FIRST_USED + LATER_USED == BIG
READY_LAST = FIRST_USED
READY_EARLY = LATER_USED
SMALL = ("c_ctx", "b_ada", "g_mix", "conv_w", "conv_b", "dt_bias", "a_log", "d_skip", "ssd_norm_w", "pool_scale",
         "g_ffn", "g_final")
WEIGHTS = ("c_ctx", "w_ada", "b_ada", "g_mix", "w_in", "conv_w", "conv_b", "dt_bias", "a_log", "d_skip", "ssd_norm_w",
           "w_ssd_out", "pool_w", "pool_scale", "w_pool_out", "w_out", "g_ffn", "w_gate_up", "w_down", "g_final")
LAYER_KEYS = ("w_ada", "b_ada", "g_mix", "w_in", "conv_w", "conv_b", "dt_bias", "a_log", "d_skip", "ssd_norm_w",
              "w_ssd_out", "pool_w", "pool_scale", "w_pool_out", "w_out", "g_ffn", "w_gate_up", "w_down")


def _shard2d(name, a):
    if name == "pool_w":
        return a.reshape(a.shape[0], a.shape[1] * a.shape[2], a.shape[3])
    return a


def _full_from_blocks(name, a):
    nb, R, C = a.shape
    if name in BLOCK_LAYOUT:
        return a
    if name in COL_SHARDED:
        return jnp.transpose(a, (1, 0, 2)).reshape(R, nb * C)
    if name == "pool_w":
        nw = len(POOL_WINDOWS)
        return jnp.transpose(a.reshape(nb, nw, R // nw, C), (1, 0, 2, 3)).reshape(nw, nb * R // nw, C)
    return a.reshape(nb * R, C)


def _blocks_from_full(name, g):
    nb = 4
    if name in BLOCK_LAYOUT:
        return g
    if name in COL_SHARDED and name not in GRAD_TRANSPOSED:
        K, N = g.shape
        return jnp.transpose(g.reshape(K, nb, N // nb), (1, 0, 2))
    if name == "pool_w":
        nw, r, C = g.shape
        return jnp.transpose(g.reshape(nw, nb, r // nb, C), (1, 0, 2, 3)).reshape(nb, nw * r // nb, C)
    return g.reshape(nb, g.shape[0] // nb, g.shape[1])


def _pack(arrs, rows):
    flat = jnp.concatenate([a.reshape(-1).astype(F32) for a in arrs])
    return jnp.concatenate([flat, jnp.zeros((rows * 128 - flat.size,), F32)]).reshape(rows, 128)


def _unpack(vec, shapes):
    flat = vec.reshape(-1)
    out, o = [], 0
    for s in shapes:
        n = int(np.prod(s))
        out.append(flat[o:o + n].reshape(s))
        o += n
    return out


def _rows_for(shapes):
    n = sum(int(np.prod(s)) for s in shapes)
    return -(-n // (8 * 128)) * 8


def kernel(x, c, ctx, c_ctx, w_ada, b_ada, g_mix, w_in, conv_w, conv_b, dt_bias, a_log, d_skip, ssd_norm_w, w_ssd_out, pool_w, pool_scale, w_pool_out, w_out, g_ffn, w_gate_up, w_down, g_final, loss_target, m_c_ctx, m_w_ada, m_b_ada, m_g_mix, m_w_in, m_conv_w, m_conv_b, m_dt_bias, m_a_log, m_d_skip, m_ssd_norm_w, m_w_ssd_out, m_pool_w, m_pool_scale, m_w_pool_out, m_w_out, m_g_ffn, m_w_gate_up, m_w_down, m_g_final, v_c_ctx, v_w_ada, v_b_ada, v_g_mix, v_w_in, v_conv_w, v_conv_b, v_dt_bias, v_a_log, v_d_skip, v_ssd_norm_w, v_w_ssd_out, v_pool_w, v_pool_scale, v_w_pool_out, v_w_out, v_g_ffn, v_w_gate_up, v_w_down, v_g_final):
    w = dict(c_ctx=c_ctx, w_ada=w_ada, b_ada=b_ada, g_mix=g_mix, w_in=w_in, conv_w=conv_w, conv_b=conv_b, dt_bias=dt_bias,
             a_log=a_log, d_skip=d_skip, ssd_norm_w=ssd_norm_w, w_ssd_out=w_ssd_out, pool_w=pool_w, pool_scale=pool_scale,
             w_pool_out=w_pool_out, w_out=w_out, g_ffn=g_ffn, w_gate_up=w_gate_up, w_down=w_down, g_final=g_final)
    m = dict(c_ctx=m_c_ctx, w_ada=m_w_ada, b_ada=m_b_ada, g_mix=m_g_mix, w_in=m_w_in, conv_w=m_conv_w, conv_b=m_conv_b,
             dt_bias=m_dt_bias, a_log=m_a_log, d_skip=m_d_skip, ssd_norm_w=m_ssd_norm_w, w_ssd_out=m_w_ssd_out,
             pool_w=m_pool_w, pool_scale=m_pool_scale, w_pool_out=m_w_pool_out, w_out=m_w_out, g_ffn=m_g_ffn,
             w_gate_up=m_w_gate_up, w_down=m_w_down, g_final=m_g_final)
    v = dict(c_ctx=v_c_ctx, w_ada=v_w_ada, b_ada=v_b_ada, g_mix=v_g_mix, w_in=v_w_in, conv_w=v_conv_w, conv_b=v_conv_b,
             dt_bias=v_dt_bias, a_log=v_a_log, d_skip=v_d_skip, ssd_norm_w=v_ssd_norm_w, w_ssd_out=v_w_ssd_out,
             pool_w=v_pool_w, pool_scale=v_pool_scale, w_pool_out=v_w_pool_out, w_out=v_w_out, g_ffn=v_g_ffn,
             w_gate_up=v_w_gate_up, w_down=v_w_down, g_final=v_g_final)
    assert x.shape[0] == 1, "one example per device"
    pos = _position()
    core = pos[2].astype(jnp.int32).reshape(1)
    n_layers = w_in.shape[0]
    assert n_layers == 2
    dims = (ssd_norm_w.shape[1], conv_w.shape[2] * 4, dt_bias[0].size, pool_scale.shape[1])
    shard = {k: _shard2d(k, w[k]) for k in BIG}

    def halves(a):
        return a.reshape(a.shape[:-2] + (2, a.shape[-2] // 2, a.shape[-1]))

    def whole(a):
        return a.reshape(a.shape[:-3] + (2 * a.shape[-2], a.shape[-1]))

    def wire_shards(l, names):
        return [halves(shard[k][l].astype(MXU_DTYPE)) for k in names]

    def full_weights(names, gathered):
        return {k: _full_from_blocks(k, whole(a)) for k, a in zip(names, gathered)}

    first = comm_call("gather0_chips", gather_chips(wire_shards(0, FIRST_USED), conv=conv_w))
    got0 = full_weights(FIRST_USED, comm_call("gather0_pair", gather_pair(first[:-1])))
    conv_all = first[-1]
    conv_full = [jnp.transpose(conv_all[:, l], (1, 0, 2)).reshape(conv_all.shape[2], -1) for l in range(n_layers)]

    boxes = {}

    n_first, n_mid = len(FIRST_USED), len(MID_USED)

    n_first, n_end = len(FIRST_USED), len(END_USED)

    def layer_w_fn(l):
        if l == 0:
            full = dict(got0)
        else:
            f0 = boxes[("fwd", 0)]
            full = full_weights(("w_in",), f0["gate_up_mm"][:1])
            full.update(full_weights(("w_ada",), f0["down_mm"]))
        late = {k: (lambda i=i: boxes[("fwd", l)]["conv"][i]) for i, k in enumerate(MID_USED)}
        late.update({k: (lambda i=i: boxes[("fwd", l)]["ssd_out_mm"][i]) for i, k in enumerate(END_USED)})
        full["conv_w"] = conv_full[l]
        lw = LazyDict(_prep_layer_weights(*[full[k] if k in full else (None if k in late else w[k][l]) for k in LAYER_KEYS]))
        for k, get in late.items():
            lw[k] = (lambda k=k, get=get: _full_from_blocks(k, whole(get())))
        return lw

    def fwd_hosts(l, box):
        boxes[("fwd", l)] = box
        hosts = {"in_mm": lambda box: gather_chips(wire_shards(l, MID_USED)),
                 "conv": lambda box: gather_pair(box["in_mm"]),
                 "ssd": lambda box: gather_chips(wire_shards(l, END_USED)),
                 "ssd_out_mm": lambda box: gather_pair(box["ssd"][:n_end])}
        if l == 0:
            hosts["ssd"] = lambda box: combine(gather_chips(wire_shards(0, END_USED)), gather_chips(wire_shards(1, ("w_in",))))
            hosts["gate_up_mm"] = lambda box: combine(gather_pair(box["ssd"][n_end:]),
                                                      gather_chips(wire_shards(1, ("w_ada",))))
            hosts["down_mm"] = lambda box: gather_pair(box["gate_up_mm"][1:])
        return hosts

    def blocks(gl, names):
        return [halves(_blocks_from_full(k, gl[k])) for k in names]

    def pair_sums(tag, names, G, recv):
        return [add_own_half(f"pair_sum{tag}_{k}", g, r, core) for k, g, r in zip(names, G, recv)]

    def chip_sums(tag, names, parts):
        return [sum_slots(f"chip_sum{tag}_{k}", p, core) for k, p in zip(names, parts)]

    def reduce_now(tag, gl, names):
        G = blocks(gl, names)
        pair = pair_sums(tag, names, G, comm_call(f"swap{tag}", swap_halves(G)))
        return chip_sums(tag, names, comm_call(f"scatter{tag}", scatter_chips(pair)))

    small_layers = {}
    n_big = len(BIG)

    def bwd_hosts(l, grads, box):
        boxes[("bwd", l)] = box
        if l != 0:
            return None
        gl1 = _unprep_layer_grads(grads[1], dims)
        small_layers[1] = gl1
        G1 = blocks(gl1, BIG)
        early = {}

        def gate_host(box):
            early["G"] = blocks(box["g"], READY_EARLY)
            return swap_halves(early["G"])

        def scan_host(box):
            return combine(scatter_chips(pair_sums("1", BIG, G1, box["down_dx"] + box["down_dw"])),
                           scatter_chips(pair_sums("0e", READY_EARLY, early["G"], box["ssd_gate"])))

        def conv_host(box):
            return combine(share_halves(chip_sums("1", BIG, box["ssd"][:n_big])),
                           share_halves(chip_sums("0e", READY_EARLY, box["ssd"][n_big:])))

        return {"down_dx": lambda box: swap_halves(G1[:n_first]), "down_dw": lambda box: swap_halves(G1[n_first:]),
                "ssd_gate": gate_host, "ssd": scan_host, "in_dx": conv_host}

    loss, grad_x, grads, d_c_ctx, d_g_final = local_step(
        x[0], ctx[0], c[0], c_ctx, loss_target[0], layer_w_fn, n_layers, g_final, fwd_hosts, bwd_hosts)
    shared =[whole(a) for a in boxes[("bwd", 0)]["in_dx"]]
    reduced1 = shared[:n_big]
    gl0 = _unprep_layer_grads(grads[0], dims)
    small_layers[0] = gl0
    last_halves = reduce_now("0", gl0, READY_LAST)

    small_full = dict(c_ctx=d_c_ctx, g_final=d_g_final.reshape(-1))
    for k in SMALL:
        if k not in small_full:
            small_full[k] = jnp.stack([small_layers[l][k] for l in range(n_layers)])
    shapes = [small_full[k].shape for k in SMALL] + [(1,)]
    packed = _pack([small_full[k] for k in SMALL] + [loss.reshape(1)], _rows_for(shapes))
    *last, everyone = comm_call("share0_small", combine(share_halves(last_halves), gather_everyone(packed)))
    red0 = dict(zip(READY_EARLY, shared[n_big:]))
    red0.update(zip(READY_LAST, [whole(a) for a in last]))
    reduced0 = [red0[k] for k in BIG]
    total = sum_slots("small_sum", everyone)
    *small_vals, loss = _unpack(total, shapes)
    loss = loss.reshape(())
    small_g = dict(zip(SMALL, small_vals))
    cw = conv_w.shape[2]
    small_g["conv_w"] = lax.dynamic_slice_in_dim(small_g["conv_w"], _chip_index(pos) * cw, cw, axis=2)

    grad, delta, new_m, new_v = {}, {}, {}, {}
    for k, g0, g1 in zip(BIG, reduced0, reduced1):
        shp = w[k].shape
        if k in GRAD_TRANSPOSED:
            flat = lambda a: jnp.swapaxes(a, 1, 2)
            back = lambda a: jnp.swapaxes(a, 1, 2)
        else:
            flat = lambda a: _shard2d(k, a)
            back = lambda a: a.reshape(shp)
        outs = adamw(f"adamw_{k}", flat(w[k]), [g0, g1], flat(m[k]), flat(v[k]))
        grad[k], delta[k], new_m[k], new_v[k] = [back(a) for a in outs]
    flat2 = lambda d: [d[k].reshape(-1, d[k].shape[-1]) for k in SMALL]
    d_, m_, v_ = adamw_small("adamw_small", flat2(w), flat2(small_g), flat2(m), flat2(v))
    for k, dd, mm, vv in zip(SMALL, d_, m_, v_):
        shp = w[k].shape
        grad[k], delta[k], new_m[k], new_v[k] = small_g[k], dd.reshape(shp), mm.reshape(shp), vv.reshape(shp)

    return (loss, grad_x[None], *[grad[k] for k in WEIGHTS], *[delta[k] for k in WEIGHTS],
            *[new_m[k] for k in WEIGHTS], *[new_v[k] for k in WEIGHTS])
```

```python
import functools

import jax
import jax.numpy as jnp
import numpy as np
from jax import lax
from jax.experimental import pallas as pl
from jax.experimental.pallas import tpu as pltpu

F32 = jnp.float32
MXU_DTYPE = jnp.bfloat16
ACT_DTYPE = jnp.bfloat16
VMEM_LIMIT_BYTES = 48 * 1024 * 1024
MATMUL_VMEM_LIMIT_BYTES = 56 * 1024 * 1024
EPS = 1e-6
NEG = -1e30

SSD_HEADDIM = 64
SSD_GROUPS = 8
SSD_STATE = 128
SSD_CHUNK = 128
SSD_GROUPS_PER_STEP = 8
SSD_CONV = 5
GRID_W = 64
POOL_WINDOWS = (2, 4, 8, 16)
ROW_TILE = 256
DT_PAD = 512

ADAM_LR = 0.001
ADAM_B1 = 0.9
ADAM_B2 = 0.999
ADAM_EPS = 1e-08
ADAM_WD = 0.01
ADAM_STEP = 10

MESH = pl.DeviceIdType.MESH


def _pcall(body, **kw):
    return pl.pallas_call(body, **kw)


def _params(*sem):
    return pltpu.CompilerParams(dimension_semantics=tuple(sem), vmem_limit_bytes=VMEM_LIMIT_BYTES)


def _pick_tile(n, cands):
    for t in cands:
        if n % t == 0:
            return t
    return n


PLANE = ((1, 0, 0), (0, 1, 0), (1, 1, 0))
PAIR = ((0, 0, 1),)
EVERYONE = tuple((a, b, d) for a in (0, 1) for b in (0, 1) for d in (0, 1) if a + b + d)
HBM = pl.BlockSpec(memory_space=pl.ANY)


def _position():
    return lax.axis_index("x"), lax.axis_index("y"), lax.axis_index("c")


def _flip(pos, rel):
    return tuple(1 - p if r else p for p, r in zip(pos, rel))


def _chip_index(pos):
    return 2 * pos[0] + pos[1]


def _device_index(pos):
    return 4 * pos[0] + 2 * pos[1] + pos[2]


class Exchange:
    def __init__(self, copies, n_remote, n_local, operands, out_shapes, aliases=None):
        self.copies, self.n_remote, self.n_local = copies, n_remote, n_local
        self.operands, self.out_shapes, self.aliases = list(operands), list(out_shapes), dict(aliases or {})

    def scratch(self):
        return [pltpu.SemaphoreType.DMA((max(self.n_remote, 1),)), pltpu.SemaphoreType.DMA((max(self.n_remote, 1),)),
                pltpu.SemaphoreType.DMA((max(self.n_local, 1),))]

    def descriptors(self, ins, outs, sems):
        send_sems, recv_sems, local_sems = sems
        local, remote = self.copies(ins, outs, _position())
        assert len(local) == self.n_local and len(remote) == self.n_remote
        cps = [pltpu.make_async_copy(src, dst, local_sems.at[k]) for k, (src, dst) in enumerate(local)]
        cps += [pltpu.make_async_remote_copy(src_ref=src, dst_ref=dst, send_sem=send_sems.at[k], recv_sem=recv_sems.at[k],
                                             device_id=peer, device_id_type=MESH) for k, (src, dst, peer) in enumerate(remote)]
        return cps


def combine(a, b):
    na, nao = len(a.operands), len(a.out_shapes)

    def copies(ins, outs, pos):
        la, ra = a.copies(ins[:na], outs[:nao], pos)
        lb, rb = b.copies(ins[na:], outs[nao:], pos)
        return la + lb, ra + rb

    aliases = dict(a.aliases)
    aliases.update({na + k: nao + v for k, v in b.aliases.items()})
    return Exchange(copies, a.n_remote + b.n_remote, a.n_local + b.n_local, a.operands + b.operands,
                    a.out_shapes + b.out_shapes, aliases)


class LazyDict(dict):
    def __getitem__(self, key):
        v = dict.__getitem__(self, key)
        if callable(v):
            v = v()
            dict.__setitem__(self, key, v)
        return v


def comm_call(name, ex):
    n_in, n_out = len(ex.operands), len(ex.out_shapes)

    def body(*refs):
        cps = ex.descriptors(refs[:n_in], refs[n_in:n_in + n_out], refs[n_in + n_out:])
        for cp in cps:
            cp.start()
        for cp in cps:
            cp.wait()

    return _pcall(
        body, name=name, out_shape=ex.out_shapes, in_specs=[HBM] * n_in, out_specs=[HBM] * n_out,
        scratch_shapes=ex.scratch(), input_output_aliases=ex.aliases,
        compiler_params=pltpu.CompilerParams(has_side_effects=True),
    )(*ex.operands)


def hosted_call(body, ex, operands, *, name, out_shape, grid, in_specs, out_specs, scratch_shapes=(),
                vmem_limit=None):
    n_in, n_out, n_scr = len(operands), len(out_shape), len(scratch_shapes)
    sem = ("arbitrary",) * len(grid)
    vmem_limit = vmem_limit or VMEM_LIMIT_BYTES
    if ex is None:
        res = _pcall(body, name=name, out_shape=list(out_shape), grid=grid, in_specs=list(in_specs),
                     out_specs=list(out_specs), scratch_shapes=list(scratch_shapes),
                     compiler_params=pltpu.CompilerParams(dimension_semantics=sem, vmem_limit_bytes=vmem_limit))(*operands)
        return res, []
    x_in, x_out = len(ex.operands), len(ex.out_shapes)

    def wrapped(*refs):
        o = 0
        ins = refs[o:o + n_in]; o += n_in
        xins = refs[o:o + x_in]; o += x_in
        outs = refs[o:o + n_out]; o += n_out
        xouts = refs[o:o + x_out]; o += x_out
        scr = refs[o:o + n_scr]; o += n_scr
        sems = refs[o:]
        first = last = None
        for a, n in enumerate(grid):
            i = pl.program_id(a)
            first = (i == 0) if first is None else first & (i == 0)
            last = (i == n - 1) if last is None else last & (i == n - 1)

        @pl.when(first)
        def _():
            for cp in ex.descriptors(xins, xouts, sems):
                cp.start()

        body(*ins, *outs, *scr)

        @pl.when(last)
        def _():
            for cp in ex.descriptors(xins, xouts, sems):
                cp.wait()

    aliases = {n_in + k: n_out + v for k, v in ex.aliases.items()}
    res = _pcall(
        wrapped, name=name, out_shape=list(out_shape) + ex.out_shapes, grid=grid,
        in_specs=list(in_specs) + [HBM] * x_in, out_specs=list(out_specs) + [HBM] * x_out,
        scratch_shapes=list(scratch_shapes) + ex.scratch(), input_output_aliases=aliases,
        compiler_params=pltpu.CompilerParams(dimension_semantics=sem, vmem_limit_bytes=vmem_limit,
                                             has_side_effects=True),
    )(*operands, *ex.operands)
    return res[:n_out], res[n_out:]


def _dot(a, b, dims):
    return lax.dot_general(a.astype(MXU_DTYPE), b.astype(MXU_DTYPE), (dims, ((), ())), preferred_element_type=F32)


_NN = ((1,), (0,))
_NT = ((1,), (1,))
_TN = ((0,), (0,))


@jax.custom_vjp
def _mm(a, b):
    return _dot(a, b, _NN)


def _mm_fwd(a, b):
    return _mm(a, b), (a, b)


def _mm_bwd(res, g):
    a, b = res
    return _dot(g, b, _NT).astype(a.dtype), _dot(a, g, _TN).astype(b.dtype)


_mm.defvjp(_mm_fwd, _mm_bwd)


@jax.custom_vjp
def _mm_nt(a, b):
    return _dot(a, b, _NT)


def _mm_nt_fwd(a, b):
    return _mm_nt(a, b), (a, b)


def _mm_nt_bwd(res, g):
    a, b = res
    return _dot(g, b, _NN).astype(a.dtype), _dot(g, a, _TN).astype(b.dtype)


_mm_nt.defvjp(_mm_nt_fwd, _mm_nt_bwd)


@jax.custom_vjp
def _mm_tn(a, b):
    return _dot(a, b, _TN)


def _mm_tn_fwd(a, b):
    return _mm_tn(a, b), (a, b)


def _mm_tn_bwd(res, g):
    a, b = res
    return _dot(b, g, _NT).astype(a.dtype), _dot(a, g, _NN).astype(b.dtype)


_mm_tn.defvjp(_mm_tn_fwd, _mm_tn_bwd)


def _dot_exact(m01, v):
    m = m01.astype(jnp.bfloat16)
    hi = v.astype(jnp.bfloat16)
    r1 = v - hi.astype(F32)
    mid = r1.astype(jnp.bfloat16)
    lo = (r1 - mid.astype(F32)).astype(jnp.bfloat16)
    out = jnp.dot(m, hi, preferred_element_type=F32)
    out = out + jnp.dot(m, mid, preferred_element_type=F32)
    return out + jnp.dot(m, lo, preferred_element_type=F32)


@jax.custom_vjp
def _lin01(m, mt, v):
    return _dot_exact(m, v)


def _lin01_fwd(m, mt, v):
    return _dot_exact(m, v), (m, mt)


def _lin01_bwd(res, g):
    m, mt = res
    return jnp.zeros_like(m), jnp.zeros_like(mt), _dot_exact(mt, g)


_lin01.defvjp(_lin01_fwd, _lin01_bwd)


MATMUL_VMEM_BUDGET = MATMUL_VMEM_LIMIT_BYTES * 5 // 6


def _mm_tiles(m, n, k_bytes_a, k_bytes_b, out_bytes, cands_m, cands_n):
    best = None
    for tm in cands_m:
        if m % tm:
            continue
        for tn in cands_n:
            if n % tn:
                continue
            need = 2 * (tm * k_bytes_a + tn * k_bytes_b + tm * tn * out_bytes)
            if need <= MATMUL_VMEM_BUDGET and (best is None or tm * tn > best[0] * best[1]):
                best = (tm, tn)
    assert best is not None, (m, n)
    return best


_ROW_CANDS = (4352, 2176, 1088, 768, 544, 512, 272, 256, 128, 16)
_COL_CANDS = (2816, 2048, 1408, 1024, 512, 256, 128)


def _one(res, xres, ex):
    return res[0] if ex is None else (res[0], xres)


def _block_cands(c):
    return (c,) + tuple(t for t in (512, 256, 128) if c % t == 0)


def matmul_nn(name, a, b, out_dtype=F32, ex=None, col0=0, ncols=None):
    M, K = a.shape
    if b.ndim == 3:
        nb, _, C = b.shape
        N, cands = nb * C, _block_cands(C)
    else:
        N, cands = (b.shape[1] - col0 if ncols is None else ncols), (512, 256, 128)
    tm, tn = _mm_tiles(M, N, K * a.dtype.itemsize, K * b.dtype.itemsize, jnp.dtype(out_dtype).itemsize,
                       _ROW_CANDS, cands)
    if b.ndim == 3:
        per = C // tn
        b_spec = pl.BlockSpec((None, K, tn), lambda j, i: (j // per, 0, j % per))
    else:
        assert col0 % tn == 0
        first = col0 // tn
        b_spec = pl.BlockSpec((K, tn), lambda j, i: (0, first + j))

    def body(a_ref, b_ref, o_ref):
        o_ref[...] = _dot(a_ref[...], b_ref[...], _NN).astype(o_ref.dtype)

    res, xres = hosted_call(
        body, ex, [a, b], name=name, out_shape=[jax.ShapeDtypeStruct((M, N), out_dtype)], grid=(N // tn, M // tm),
        in_specs=[pl.BlockSpec((tm, K), lambda j, i: (i, 0)), b_spec],
        out_specs=[pl.BlockSpec((tm, tn), lambda j, i: (i, j))], vmem_limit=MATMUL_VMEM_LIMIT_BYTES)
    return _one(res, xres, ex)


def matmul_nt(name, g, b, out_dtype=F32, ex=None, offsets=None):
    pieces = list(g) if isinstance(g, (list, tuple)) else [g]
    offsets = list(offsets) if offsets is not None else [0]
    M = pieces[0].shape[0]
    if b.ndim == 3:
        nb, K, C = b.shape
        N = nb * C
        assert len(pieces) == 1
    else:
        K, N = b.shape
    g_bytes = sum(p.shape[1] * p.dtype.itemsize for p in pieces)
    tm, tk = _mm_tiles(M, K, g_bytes, N * b.dtype.itemsize, jnp.dtype(out_dtype).itemsize, _ROW_CANDS, _COL_CANDS)

    def body(*refs):
        b_ref, o_ref = refs[-2:]
        acc = None
        if b.ndim == 3:
            parts = [_dot(refs[0][:, k * C:(k + 1) * C], b_ref[k], _NT) for k in range(nb)]
        else:
            parts = [_dot(g_ref[...], b_ref[:, off:off + g_ref.shape[1]], _NT) for g_ref, off in zip(refs[:-2], offsets)]
        for part in parts:
            acc = part if acc is None else acc + part
        o_ref[...] = acc.astype(o_ref.dtype)

    b_spec = (pl.BlockSpec((nb, tk, C), lambda j, i: (0, j, 0)) if b.ndim == 3
              else pl.BlockSpec((tk, N), lambda j, i: (j, 0)))
    res, xres = hosted_call(
        body, ex, pieces + [b], name=name, out_shape=[jax.ShapeDtypeStruct((M, K), out_dtype)], grid=(K // tk, M // tm),
        in_specs=[pl.BlockSpec((tm, p.shape[1]), lambda j, i: (i, 0)) for p in pieces] + [b_spec],
        out_specs=[pl.BlockSpec((tm, tk), lambda j, i: (i, j))], vmem_limit=MATMUL_VMEM_LIMIT_BYTES)
    return _one(res, xres, ex)


def matmul_tn(name, a, g, ex=None, blocks=1):
    M, K = a.shape
    N = g.shape[1]
    C = N // blocks
    tk, tn = _mm_tiles(K, N, M * a.dtype.itemsize, M * g.dtype.itemsize, 4, (512, 256, 128),
                       (512, 256, 128) if blocks == 1 else _block_cands(C))

    def body(a_ref, g_ref, o_ref):
        o_ref[...] = _dot(a_ref[...], g_ref[...], _TN)

    if blocks == 1:
        out_shape, out_spec = jax.ShapeDtypeStruct((K, N), F32), pl.BlockSpec((tk, tn), lambda i, j: (i, j))
    else:
        per = C // tn
        out_shape = jax.ShapeDtypeStruct((blocks, K, C), F32)
        out_spec = pl.BlockSpec((None, tk, tn), lambda i, j: (j // per, i, j % per))
    res, xres = hosted_call(
        body, ex, [a, g], name=name, out_shape=[out_shape], grid=(K // tk, N // tn),
        in_specs=[pl.BlockSpec((M, tk), lambda i, j: (0, i)), pl.BlockSpec((M, tn), lambda i, j: (0, j))],
        out_specs=[out_spec], vmem_limit=MATMUL_VMEM_LIMIT_BYTES)
    return _one(res, xres, ex)


class Arg:
    def __init__(self, arr, block, imap, kind):
        self.arr, self.block, self.imap, self.kind = arr, block, imap, kind


class Rows:
    def __init__(self, nt, nct, tm, ncol=1):
        self.nt, self.nct, self.tm, self.ncol = nt, nct, tm, ncol

    def seg(self, i):
        return jnp.where(i >= self.nct, 1, 0)

    def spec(self, block, imap):
        return pl.BlockSpec(block, lambda j, i: imap(j, i, self.seg(i)))

    def row(self, arr, width, cb0=0, follow=False, roff=0, stride=1):
        f = stride if follow else 0
        return Arg(arr, (self.tm, width), lambda j, i, s: (i + roff, cb0 + f * j), "row")

    def vec(self, arr, follow=False, kind="acc"):
        w = arr.shape[1] // (self.ncol if follow else 1)
        f = 1 if follow else 0
        return Arg(arr, (1, w), lambda j, i, s: (0, f * j), kind)

    def segvec(self, arr, kind="seg"):
        return Arg(arr, (None, 1, arr.shape[2]), lambda j, i, s: (s, 0, 0), kind)


def _load(ref):
    return ref[...].astype(F32) if ref.dtype != F32 else ref[...]


def stage_fwd(name, f, rows, args, outs):
    n_in = len(args)

    def body(*refs):
        vals = [_load(r) for r in refs[:n_in]]
        res = f(*vals)
        for r, v in zip(refs[n_in:], res):
            r[...] = v.astype(r.dtype)

    T = rows.nt * rows.tm
    out_shape = [jax.ShapeDtypeStruct((T, w * (rows.ncol if fo else 1)), dt) for w, dt, fo in outs]
    out_specs = [pl.BlockSpec((rows.tm, w), (lambda j, i, fo=fo: (i, j if fo else 0))) for w, dt, fo in outs]
    res = _pcall(
        body, name=name, out_shape=out_shape, grid=(rows.ncol, rows.nt),
        in_specs=[rows.spec(a.block, a.imap) for a in args], out_specs=out_specs,
        compiler_params=_params("parallel", "parallel"),
    )(*[a.arr for a in args])
    return res


def stage_bwd(name, f, rows, args, cots, row_dtypes, ex=None, primal=()):
    n_in, n_ct = len(args), len(cots)
    diff = [k for k, a in enumerate(args) if a.kind != "const"]
    row_dt = {}
    for k in diff:
        if args[k].kind == "row":
            row_dt[k] = row_dtypes[len(row_dt)]

    def body(*refs):
        i = pl.program_id(1)
        vals = [_load(r) for r in refs[:n_in]]
        cts = tuple(_load(r) for r in refs[n_in:n_in + n_ct])
        outs = refs[n_in + n_ct:]

        def g(*dv):
            full = list(vals)
            for k, v in zip(diff, dv):
                full[k] = v
            return tuple(f(*full))

        prim, vjp = jax.vjp(g, *[vals[k] for k in diff])
        grads = vjp(cts)
        for o, v in zip(outs[len(diff):], prim):
            o[...] = v.astype(o.dtype)
        for k, o, gr in zip(diff, outs, grads):
            kind = args[k].kind
            if kind == "row":
                o[...] = gr.astype(o.dtype)
            else:
                first = (i == 0) | (i == rows.nct) if kind == "seg" else (i == 0)

                @pl.when(first)
                def _():
                    o[...] = gr.astype(o.dtype)

                @pl.when(jnp.logical_not(first))
                def _():
                    o[...] += gr.astype(o.dtype)

    T = rows.nt * rows.tm
    out_shape, out_specs = [], []
    for k in diff:
        a = args[k]
        if a.kind == "row":
            out_shape.append(jax.ShapeDtypeStruct((T, a.block[1] * (rows.ncol if _follows(a) else 1)), row_dt[k]))
            fo = _follows(a)
            out_specs.append(pl.BlockSpec(a.block, (lambda j, i, fo=fo: (i, j if fo else 0))))
        else:
            out_shape.append(jax.ShapeDtypeStruct(a.arr.shape, F32))
            out_specs.append(rows.spec(a.block, a.imap))
    for w, dt in primal:
        out_shape.append(jax.ShapeDtypeStruct((T, w), dt))
        out_specs.append(pl.BlockSpec((rows.tm, w), lambda j, i: (i, 0)))
    res, xres = hosted_call(
        body, ex, [a.arr for a in list(args) + list(cots)], name=name, out_shape=out_shape, grid=(rows.ncol, rows.nt),
        in_specs=[rows.spec(a.block, a.imap) for a in list(args) + list(cots)], out_specs=out_specs)
    return res if ex is None else (res, xres)


def _follows(a):
    return a.imap(1, 0, 0)[-1] != a.imap(0, 0, 0)[-1]


def _rms(x):
    return x * lax.rsqrt(jnp.mean(x * x, axis=-1, keepdims=True) + EPS)


def f_norm_mod(x, g, sh, sc):
    return ((_rms(x) * g) * (1.0 + sc) + sh,)


def f_resid_norm_mod(x, mo, ga, g, sh, sc):
    x1 = x + ga * mo
    return x1, (_rms(x1) * g) * (1.0 + sc) + sh


def f_resid(x, dn, ga):
    return (x + ga * dn,)


def f_silu(x):
    return (x * jax.nn.sigmoid(x),)


def f_bias(x, b):
    return (x + b,)


def f_ssd_gate(y0, y1, xs, z, dskip, nw):
    y = y0 + y1 + dskip * xs
    return (_rms(y * (z * jax.nn.sigmoid(z))) * nw,)


def f_pool(u, pmat, pmat_t, inv_cnt, pw, scale):
    pm = _lin01(pmat, pmat_t, u) * inv_cnt - u
    return (_mm(pm, pw) * scale,)


def f_merge(o_ssd, o_pool, gl_ssd, gl_pool):
    return (jax.nn.sigmoid(gl_ssd) * o_ssd + jax.nn.sigmoid(gl_pool) * o_pool,)


def _column_splitter(n):
    @jax.custom_vjp
    def split(x):
        w = x.shape[1] // n
        return tuple(x[:, k * w:(k + 1) * w] for k in range(n))

    def fwd(x):
        return split(x), None

    def bwd(_, g):
        return (jnp.concatenate(g, axis=1),)

    split.defvjp(fwd, bwd)
    return split


_halve_cols = _column_splitter(2)
_quarter_cols = _column_splitter(len(POOL_WINDOWS))


def f_swiglu(gu):
    a, b = _halve_cols(gu)
    return ((a * jax.nn.sigmoid(a)) * b,)


def f_pool_all(u, pmat, pmat_t, inv_cnt, scale, *pws):
    outs = [f_pool(part, pmat[k], pmat_t[k], inv_cnt[k], pws[k], 1.0)[0] for k, part in enumerate(_quarter_cols(u))]
    return (jnp.concatenate(outs, axis=1) * scale,)


def f_loss_resid(x1, dn, ga, tgt, g):
    err = _rms(x1 + ga * dn) * g - tgt
    return (0.5 * jnp.mean(err * err, axis=-1, keepdims=True),)


CONV_TILE = 128


CONV_GAP = 8


def _gapped(v, n_ctx):
    z = jnp.zeros((CONV_GAP, v.shape[1]), v.dtype)
    return jnp.concatenate([v[:n_ctx], z, v[n_ctx:], z], axis=0)


def _ungapped(v, n_ctx):
    return jnp.concatenate([v[:n_ctx], v[n_ctx + CONV_GAP:v.shape[0] - CONV_GAP]], axis=0)


def _shift_rows(v, j):
    return v if j == 0 else pltpu.roll(v, (-j) % v.shape[0], 0)


def conv_fwd(name, proj, conv_w, conv_b, n_ctx, width, ex=None):
    T = proj.shape[0]
    half = SSD_CONV // 2

    def body(u_ref, w_ref, b_ref, o_ref):
        u = _gapped(u_ref[...].astype(F32), n_ctx)
        pre = jnp.broadcast_to(b_ref[...], u.shape)
        for k in range(SSD_CONV):
            pre = pre + w_ref[k:k + 1, :] * _shift_rows(u, k - half)
        o_ref[...] = _ungapped(pre * jax.nn.sigmoid(pre), n_ctx)

    col = lambda t: (0, t)
    res, xres = hosted_call(
        body, ex, [proj, conv_w, conv_b], name=name, out_shape=[jax.ShapeDtypeStruct((T, width), F32)],
        grid=(width // CONV_TILE,),
        in_specs=[pl.BlockSpec((T, CONV_TILE), col), pl.BlockSpec((SSD_CONV, CONV_TILE), col),
                  pl.BlockSpec((1, CONV_TILE), col)],
        out_specs=[pl.BlockSpec((T, CONV_TILE), col)])
    return res[0], xres


def conv_bwd(name, proj, conv_w, conv_b, d_act2, d_skip, n_ctx, width, ex=None):
    T = proj.shape[0]
    half = SSD_CONV // 2

    def body(u_ref, w_ref, b_ref, c0_ref, c1_ref, cs_ref, du_ref, dw_ref, db_ref):
        t = pl.program_id(0)
        u = _gapped(u_ref[...].astype(F32), n_ctx)
        pre = jnp.broadcast_to(b_ref[...], u.shape)
        for k in range(SSD_CONV):
            pre = pre + w_ref[k:k + 1, :] * _shift_rows(u, k - half)
        sg = jax.nn.sigmoid(pre)
        ct = c0_ref[...].astype(F32) + c1_ref[...].astype(F32) + jnp.where(t % 4 < 2, cs_ref[...].astype(F32), 0.0)
        dpre = _gapped(ct, n_ctx) * (sg * (1.0 + pre * (1.0 - sg)))
        du = jnp.zeros_like(u)
        for k in range(SSD_CONV):
            du = du + w_ref[k:k + 1, :] * _shift_rows(dpre, half - k)
            dw_ref[k:k + 1, :] = jnp.sum(dpre * _shift_rows(u, k - half), axis=0, keepdims=True)
        du_ref[...] = _ungapped(du, n_ctx).astype(du_ref.dtype)
        db_ref[...] = jnp.sum(dpre, axis=0, keepdims=True)

    col = lambda t: (0, t)
    skip_col = lambda t: (0, (t // 4) * 2 + jnp.minimum(t % 4, 1))
    res, xres = hosted_call(
        body, ex, [proj, conv_w, conv_b, d_act2[0], d_act2[1], d_skip], name=name,
        out_shape=[jax.ShapeDtypeStruct((T, width), ACT_DTYPE), jax.ShapeDtypeStruct((SSD_CONV, width), F32),
                   jax.ShapeDtypeStruct((1, width), F32)],
        grid=(width // CONV_TILE,),
        in_specs=[pl.BlockSpec((T, CONV_TILE), col), pl.BlockSpec((SSD_CONV, CONV_TILE), col),
                  pl.BlockSpec((1, CONV_TILE), col), pl.BlockSpec((T, CONV_TILE), col),
                  pl.BlockSpec((T, CONV_TILE), col), pl.BlockSpec((T, CONV_TILE), skip_col)],
        out_specs=[pl.BlockSpec((T, CONV_TILE), col), pl.BlockSpec((SSD_CONV, CONV_TILE), col),
                   pl.BlockSpec((1, CONV_TILE), col)])
    return res[0], res[1], res[2], xres


@jax.custom_vjp
def _cumsum_mat(tri, tri_t, a):
    return jnp.dot(tri, a, precision=lax.Precision.HIGHEST, preferred_element_type=F32)


def _cumsum_fwd(tri, tri_t, a):
    return _cumsum_mat(tri, tri_t, a), (tri, tri_t)


def _cumsum_bwd(res, g):
    tri, tri_t = res
    return (jnp.zeros_like(tri), jnp.zeros_like(tri_t),
            jnp.dot(tri_t, g, precision=lax.Precision.HIGHEST, preferred_element_type=F32))


_cumsum_mat.defvjp(_cumsum_fwd, _cumsum_bwd)


def _ssd_dt(dtraw, dt_bias, a_log, tri, tri_t):
    dt_all = jax.nn.softplus(dtraw + dt_bias)
    a_all = dt_all * (-jnp.exp(a_log))
    return dt_all, a_all, _cumsum_mat(tri, tri_t, a_all)


def _ssd_chunk(xs, bm, cm, dt_all, a_all, s_all, s_in, mask, idx0):
    (xs,), (s_in,) = xs, s_in
    Q = xs.shape[0]
    hpg = xs.shape[1] // SSD_HEADDIM
    lane = lax.broadcasted_iota(jnp.int32, dt_all.shape, 1)
    head = lax.broadcasted_iota(jnp.int32, xs.shape, 1) // SSD_HEADDIM
    head1 = lax.broadcasted_iota(jnp.int32, (1, xs.shape[1]), 1) // SSD_HEADDIM

    def pick(v, r):
        return jnp.sum(jnp.where(lane == idx0 + r, v, 0.0), axis=1, keepdims=True)

    def expand(cols, hd):
        out = cols[hpg - 1]
        for r in range(hpg - 2, -1, -1):
            out = jnp.where(hd == r, cols[r], out)
        return out

    def spread(*cols):
        return expand([jnp.broadcast_to(c, xs.shape) for c in cols], head)

    dt_r = [pick(dt_all, r) for r in range(hpg)]
    s_r = [pick(s_all, r) for r in range(hpg)]
    stot_r = [jnp.sum(jnp.where(lane == idx0 + r, a_all, 0.0), keepdims=True).reshape(1, 1) for r in range(hpg)]

    xd = xs * spread(*dt_r)
    cb = _mm_nt(cm, bm)
    weights, stacked = [], []
    for r in range(hpg):
        sm = jnp.broadcast_to(s_r[r], (Q, Q))
        weights.append(cb * jnp.exp(jnp.where(mask, sm - sm.T, NEG)))
        stacked.append(jnp.where(head == r, xd, 0.0))
    y = spread(*[jnp.exp(c) for c in s_r]) * _mm(cm, s_in)
    y = y + _mm(jnp.concatenate(weights, axis=1), jnp.concatenate(stacked, axis=0))
    to_end = spread(*[jnp.exp(t - c) for t, c in zip(stot_r, s_r)])
    carry = expand([jnp.broadcast_to(jnp.exp(t), (1, xs.shape[1])) for t in stot_r], head1)
    s_out = carry * s_in + _mm_tn(bm, xd * to_end)
    return [y], [s_out]


def _scan_consts():
    q = SSD_CHUNK
    i = np.arange(q)[:, None]
    j = np.arange(q)[None, :]
    fwd = (j <= i).astype(np.float32)
    bwd = (j >= i).astype(np.float32)
    tri = np.stack([fwd, bwd])
    return jnp.asarray(tri), jnp.asarray(np.stack([fwd.T, bwd.T]))


def _chunk_of(d, k, ncc, nc):
    rev = jnp.where(k < ncc, ncc - 1 - k, nc - 1 + ncc - k)
    return jnp.where(d == 0, k, rev)


def ssd_fwd(name, xbc, dtraw, dt_bias, a_log, n_ctx, ex=None):
    T = xbc.shape[0]
    q, G = SSD_CHUNK, SSD_GROUPS
    nc, ncc = T // q, n_ctx // q
    gw = xbc.shape[1] // G
    xw = gw - 2 * SSD_STATE
    hpg = xw // SSD_HEADDIM
    nh = G * hpg
    tri, tri_t = _scan_consts()

    gs = SSD_GROUPS_PER_STEP

    def body(x0_ref, x1_ref, dt0_ref, dt1_ref, bias_ref, alog_ref, tri_ref, trit_ref, y0_ref, y1_ref, sin_ref, state):
        gb, k = pl.program_id(0), pl.program_id(1)

        @pl.when(k == 0)
        def _():
            state[...] = jnp.zeros_like(state)

        for d, (x_ref, dt_ref, y_ref) in enumerate(((x0_ref, dt0_ref, y0_ref), (x1_ref, dt1_ref, y1_ref))):
            tri_v = tri_ref[d]
            dt_all, a_all, s_all = _ssd_dt(dt_ref[...], bias_ref[...], alog_ref[...], tri_v, trit_ref[d])
            for j in range(gs):
                o = j * gw
                sin_ref[d, j] = state[d, j]
                (y,), (s_out,) = _ssd_chunk(
                    [x_ref[:, o:o + xw]], x_ref[:, o + xw:o + xw + SSD_STATE], x_ref[:, o + xw + SSD_STATE:o + gw],
                    dt_all, a_all, s_all, [state[d, j]], tri_v > 0.5, d * nh + (gb * gs + j) * hpg)
                y_ref[:, j * xw:(j + 1) * xw] = y.astype(y_ref.dtype)
                state[d, j] = s_out

    ch = lambda d, k: _chunk_of(d, k, ncc, nc)
    y_shape = jax.ShapeDtypeStruct((T, G * xw), ACT_DTYPE)
    res, xres = hosted_call(
        body, ex, [xbc, xbc, dtraw, dtraw, dt_bias, a_log, tri, tri_t], name=name,
        out_shape=[y_shape, y_shape, jax.ShapeDtypeStruct((2, nc, G, SSD_STATE, xw), F32)],
        grid=(G // gs, nc),
        in_specs=[pl.BlockSpec((q, gs * gw), lambda g, k: (ch(0, k), g)),
                  pl.BlockSpec((q, gs * gw), lambda g, k: (ch(1, k), g)),
                  pl.BlockSpec((q, 128), lambda g, k: (ch(0, k), 0)),
                  pl.BlockSpec((q, 128), lambda g, k: (ch(1, k), 0)),
                  pl.BlockSpec((1, 128), lambda g, k: (0, 0)),
                  pl.BlockSpec((1, 128), lambda g, k: (0, 0)),
                  pl.BlockSpec((2, q, q), lambda g, k: (0, 0, 0)),
                  pl.BlockSpec((2, q, q), lambda g, k: (0, 0, 0))],
        out_specs=[pl.BlockSpec((q, gs * xw), lambda g, k: (ch(0, k), g)),
                   pl.BlockSpec((q, gs * xw), lambda g, k: (ch(1, k), g)),
                   pl.BlockSpec((2, None, gs, SSD_STATE, xw), lambda g, k: (0, k, g, 0, 0))],
        scratch_shapes=[pltpu.VMEM((2, gs, SSD_STATE, xw), F32)])
    return res[0], res[1], res[2], xres


def ssd_bwd(name, xbc, dtraw, dt_bias, a_log, states, dy, n_ctx, ex=None):
    T = xbc.shape[0]
    q, G = SSD_CHUNK, SSD_GROUPS
    nc, ncc = T // q, n_ctx // q
    gw = xbc.shape[1] // G
    xw = gw - 2 * SSD_STATE
    hpg = xw // SSD_HEADDIM
    nh = G * hpg
    tri, tri_t = _scan_consts()

    gs = SSD_GROUPS_PER_STEP

    def body(x0_ref, x1_ref, dt0_ref, dt1_ref, bias_ref, alog_ref, tri_ref, trit_ref, sin_ref, dy0_ref, dy1_ref,
             dx0_ref, dx1_ref, ddt_ref, dbias_ref, dalog_ref, dstate):
        gb, k = pl.program_id(0), pl.program_id(1)

        @pl.when((gb == 0) & (k == 0))
        def _():
            ddt_ref[...] = jnp.zeros_like(ddt_ref)
            dbias_ref[...] = jnp.zeros_like(dbias_ref)
            dalog_ref[...] = jnp.zeros_like(dalog_ref)

        @pl.when(k == 0)
        def _():
            dstate[...] = jnp.zeros_like(dstate)

        tris = [(tri_ref[d], trit_ref[d]) for d in range(2)]
        per = 4

        def fn(bias, alog, dtraw0, dtraw1, *per_group):
            ys, s_outs = [], []
            for d, dtraw in enumerate((dtraw0, dtraw1)):
                tri_v, trit_v = tris[d]
                dt_all, a_all, s_all = _ssd_dt(dtraw, bias, alog, tri_v, trit_v)
                for j in range(gs):
                    xs, bm, cm, s_in = per_group[per * (d * gs + j):per * (d * gs + j + 1)]
                    y, s_out = _ssd_chunk([xs], bm, cm, dt_all, a_all, s_all, [s_in], tri_v > 0.5,
                                          d * nh + (gb * gs + j) * hpg)
                    ys += y
                    s_outs += s_out
            return ys, s_outs

        per_group, dys, dss = [], [], []
        for d, (x_ref, dy_ref) in enumerate(((x0_ref, dy0_ref), (x1_ref, dy1_ref))):
            for j in range(gs):
                o = j * gw
                per_group += [x_ref[:, o:o + xw], x_ref[:, o + xw:o + xw + SSD_STATE], x_ref[:, o + xw + SSD_STATE:o + gw],
                              sin_ref[d, j]]
                dys.append(dy_ref[:, j * xw:(j + 1) * xw].astype(F32))
                dss.append(dstate[d, j])
        _, vjp = jax.vjp(fn, bias_ref[...], alog_ref[...], dt0_ref[...], dt1_ref[...], *per_group)
        cts = vjp((dys, dss))
        dbias, dalog, ddt0, ddt1 = cts[:4]
        for d, dx_ref in enumerate((dx0_ref, dx1_ref)):
            for j in range(gs):
                o = j * gw
                dxs, dbm, dcm, ds_in = cts[4 + per * (d * gs + j):4 + per * (d * gs + j + 1)]
                dx_ref[:, o:o + xw] = dxs.astype(dx_ref.dtype)
                dx_ref[:, o + xw:o + xw + SSD_STATE] = dbm.astype(dx_ref.dtype)
                dx_ref[:, o + xw + SSD_STATE:o + gw] = dcm.astype(dx_ref.dtype)
                dstate[d, j] = ds_in
        for d, ddt in enumerate((ddt0, ddt1)):
            row0 = pl.multiple_of(_chunk_of(d, nc - 1 - k, ncc, nc) * q, q)
            ddt_ref[pl.ds(row0, q), :] += ddt
        dbias_ref[...] += dbias
        dalog_ref[...] += dalog

    ch = lambda d, k: _chunk_of(d, nc - 1 - k, ncc, nc)
    dx_shape = jax.ShapeDtypeStruct((T, G * gw), ACT_DTYPE)
    res, xres = hosted_call(
        body, ex, [xbc, xbc, dtraw, dtraw, dt_bias, a_log, tri, tri_t, states, dy, dy], name=name,
        out_shape=[dx_shape, dx_shape, jax.ShapeDtypeStruct((T, 128), F32),
                   jax.ShapeDtypeStruct((1, 128), F32), jax.ShapeDtypeStruct((1, 128), F32)],
        grid=(G // gs, nc),
        in_specs=[pl.BlockSpec((q, gs * gw), lambda g, k: (ch(0, k), g)),
                  pl.BlockSpec((q, gs * gw), lambda g, k: (ch(1, k), g)),
                  pl.BlockSpec((q, 128), lambda g, k: (ch(0, k), 0)),
                  pl.BlockSpec((q, 128), lambda g, k: (ch(1, k), 0)),
                  pl.BlockSpec((1, 128), lambda g, k: (0, 0)),
                  pl.BlockSpec((1, 128), lambda g, k: (0, 0)),
                  pl.BlockSpec((2, q, q), lambda g, k: (0, 0, 0)),
                  pl.BlockSpec((2, q, q), lambda g, k: (0, 0, 0)),
                  pl.BlockSpec((2, None, gs, SSD_STATE, xw), lambda g, k: (0, nc - 1 - k, g, 0, 0)),
                  pl.BlockSpec((q, gs * xw), lambda g, k: (ch(0, k), g)),
                  pl.BlockSpec((q, gs * xw), lambda g, k: (ch(1, k), g))],
        out_specs=[pl.BlockSpec((q, gs * gw), lambda g, k: (ch(0, k), g)),
                   pl.BlockSpec((q, gs * gw), lambda g, k: (ch(1, k), g)),
                   pl.BlockSpec((T, 128), lambda g, k: (0, 0)),
                   pl.BlockSpec((1, 128), lambda g, k: (0, 0)),
                   pl.BlockSpec((1, 128), lambda g, k: (0, 0))],
        scratch_shapes=[pltpu.VMEM((2, gs, SSD_STATE, xw), F32)])
    return res[0], res[1], res[2], res[3], res[4], xres


def _perm_xbc(a):
    G = SSD_GROUPS
    n = a.shape[-1]
    gn = G * SSD_STATE
    di = n - 2 * gn
    lead = a.shape[:-1]
    xs = a[..., :di].reshape(lead + (G, di // G))
    bm = a[..., di:di + gn].reshape(lead + (G, SSD_STATE))
    cm = a[..., di + gn:].reshape(lead + (G, SSD_STATE))
    return jnp.concatenate([xs, bm, cm], axis=-1).reshape(lead + (n,))


def _unperm_xbc(a):
    G = SSD_GROUPS
    n = a.shape[-1]
    gn = G * SSD_STATE
    di = n - 2 * gn
    lead = a.shape[:-1]
    r = a.reshape(lead + (G, n // G))
    xw = di // G
    return jnp.concatenate([r[..., :xw].reshape(lead + (di,)), r[..., xw:xw + SSD_STATE].reshape(lead + (gn,)),
                            r[..., xw + SSD_STATE:].reshape(lead + (gn,))], axis=-1)


def _pool_consts(tm, n_ctx):
    assert n_ctx == tm and tm % GRID_W == 0
    mats, cnts = [], []
    for seq in (n_ctx, GRID_W):
        t = np.arange(tm)
        tt = t % seq
        base = t - tt
        ms, cs = [], []
        for k in POOL_WINDOWS:
            lo = np.clip(tt - k // 2, 0, seq) + base
            hi = np.clip(tt + k // 2, 0, seq) + base
            m = ((t[None, :] >= lo[:, None]) & (t[None, :] < hi[:, None])).astype(np.float32)
            ms.append(m)
            cs.append((1.0 / (hi - lo).astype(np.float32))[:, None])
        mats.append(np.stack(ms))
        cnts.append(np.stack(cs))
    m = np.stack(mats)
    return jnp.asarray(m), jnp.asarray(np.swapaxes(m, -1, -2)), jnp.asarray(np.stack(cnts).astype(np.float32))


def _prep_layer_weights(w_ada, b_ada, g_mix, w_in, conv_w, conv_b, dt_bias, a_log, d_skip, ssd_norm_w, w_ssd_out,
                        pool_w, pool_scale, w_pool_out, w_out, g_ffn, w_gate_up, w_down):
    D = w_in.shape[0]
    di = ssd_norm_w.shape[0]
    xbc = conv_w.shape[1]
    nh2 = dt_bias.size
    pw = pool_scale.shape[0]
    o = 0
    wz = w_in[:, o:o + di]; o += di
    wx = w_in[:, o:o + xbc]; o += xbc
    wdt = w_in[:, o:o + nh2]; o += nh2
    wp = w_in[:, o:o + pw]; o += pw
    wg = w_in[:, o:]
    w1 = jnp.concatenate([_perm_xbc(wx), wz, wg, wp, wdt, jnp.zeros((D, DT_PAD - nh2), w_in.dtype)], axis=1)
    pad128 = lambda v: jnp.concatenate([v.reshape(1, -1), jnp.zeros((1, 128 - v.size), F32)], axis=1)
    return dict(
        w_ada=w_ada, b_ada=b_ada.reshape(1, -1), g_mix=g_mix.reshape(1, -1), w1=w1,
        conv_w=_perm_xbc(conv_w), conv_b=_perm_xbc(conv_b.reshape(1, -1)),
        dt_bias=pad128(dt_bias), a_log=pad128(a_log),
        dskip=jnp.repeat(d_skip[0] + d_skip[1], SSD_HEADDIM).reshape(1, -1),
        ssd_norm_w=ssd_norm_w.reshape(1, -1), w_ssd_out=w_ssd_out, pool_w=pool_w,
        pool_scale=pool_scale.reshape(1, -1), w_pool_out=w_pool_out, w_out=w_out, g_ffn=g_ffn.reshape(1, -1),
        w_gate_up=w_gate_up, w_down=w_down)


def _unprep_layer_grads(g, dims):
    di, xbc, nh2, pw = dims
    dxbc, dz, dgs, dgp, dp, ddt = g["w1"]
    r = dxbc.reshape(SSD_GROUPS, xbc // SSD_GROUPS, dxbc.shape[1])
    xw = di // SSD_GROUPS
    parts = [r[:, :xw], r[:, xw:xw + SSD_STATE], r[:, xw + SSD_STATE:]]
    w_in_t = jnp.concatenate([dz] + [p.reshape(-1, dxbc.shape[1]) for p in parts] + [ddt[:nh2], dp, dgs, dgp], axis=0)
    nh = nh2 // 2
    dsk = g["dskip"].reshape(nh, SSD_HEADDIM).sum(axis=1)
    return dict(
        w_ada=g["w_ada"], b_ada=g["b_ada"].reshape(-1), g_mix=g["g_mix"].reshape(-1),
        w_in=w_in_t,
        conv_w=_unperm_xbc(g["conv_w"]), conv_b=_unperm_xbc(g["conv_b"]).reshape(-1),
        dt_bias=g["dt_bias"][0, :nh2].reshape(2, nh), a_log=g["a_log"][0, :nh2].reshape(2, nh),
        d_skip=jnp.stack([dsk, dsk]), ssd_norm_w=g["ssd_norm_w"].reshape(-1), w_ssd_out=g["w_ssd_out"],
        pool_w=g["pool_w"], pool_scale=g["pool_scale"].reshape(-1), w_pool_out=g["w_pool_out"], w_out=g["w_out"],
        g_ffn=g["g_ffn"].reshape(-1), w_gate_up=g["w_gate_up"], w_down=g["w_down"])


COND_ROWS = 16


def _split_mods(m):
    d = m.shape[1] // 6
    return [m[:2, k * d:(k + 1) * d].reshape(2, 1, d) for k in range(6)]


def _pool_args(rows, proj, col_block, width, pc, w):
    seg_const = lambda a: Arg(a, (None,) + a.shape[1:], lambda j, i, s: (s, 0, 0, 0), "const")
    pws = [Arg(w["pool_w"][k], w["pool_w"].shape[1:], lambda j, i, s: (0, 0), "acc") for k in range(w["pool_w"].shape[0])]
    return [rows.row(proj, width, col_block)] + [seg_const(a) for a in pc] + [rows.vec(w["pool_scale"])] + pws


TALL_ROW_TILE = 1088


def _tall_rows(T, ncol):
    tm = max(t for t in range(16, min(T, TALL_ROW_TILE) + 1, 16) if T % t == 0)
    return Rows(T // tm, 0, tm, ncol)


def _hosted(hosts, box, key):
    fn = (hosts or {}).get(key)
    return fn(box) if fn else None


def _layer_fwd(l, pre, cond_s, w, rows, n_ctx, pc, hosts=None, box=None):
    T, D = pre[0].shape if isinstance(pre, tuple) else pre.shape
    nt, nct, tm = rows.nt, rows.nct, rows.tm
    n = lambda s: f"l{l}_{s}"
    crow = Rows(1, 0, COND_ROWS)
    mraw = matmul_nn(n("ada_mm"), cond_s, w["w_ada"])
    (m,) = stage_fwd(n("ada_bias"), f_bias, crow, [crow.row(mraw, mraw.shape[1]), crow.vec(w["b_ada"])],
                     [(mraw.shape[1], F32, False)])
    sh1, sc1, ga1, sh2, sc2, ga2 = _split_mods(m)

    if isinstance(pre, tuple):
        x, h1 = stage_fwd(n("norm1"), f_resid_norm_mod, rows, _resid_norm_args(rows, pre, w["g_mix"], sh1, sc1, D),
                          [(D, F32, False), (D, ACT_DTYPE, False)])
    else:
        x = pre
        (h1,) = stage_fwd(n("norm1"), f_norm_mod, rows,
                          [rows.row(x, D), rows.vec(w["g_mix"]), rows.segvec(sh1), rows.segvec(sc1)],
                          [(D, ACT_DTYPE, False)])
    xbc_w = w["conv_w"].shape[1]
    di = w["ssd_norm_w"].shape[1]
    pw = w["pool_scale"].shape[1]
    c_z, c_g, c_p, c_dt = xbc_w, xbc_w + di, xbc_w + di + 2 * pw, xbc_w + di + 3 * pw
    ex = _hosted(hosts, box, "in_mm")
    proj = matmul_nn(n("in_mm"), h1, w["w1"], out_dtype=ACT_DTYPE, ex=ex, ncols=c_dt)
    if ex is not None:
        proj, box["in_mm"] = proj
    dtraw = matmul_nn(n("in_dt_mm"), h1, w["w1"], col0=c_dt, ncols=128)
    ex = _hosted(hosts, box, "conv")
    xbc, xres = conv_fwd(n("conv"), proj, w["conv_w"], w["conv_b"], n_ctx, xbc_w, ex)
    if ex is not None:
        box["conv"] = xres
    ex = _hosted(hosts, box, "ssd")
    y0, y1, states, xres = ssd_fwd(n("ssd"), xbc, dtraw, w["dt_bias"], w["a_log"], n_ctx, ex)
    y2 = (y0, y1)
    if ex is not None:
        box["ssd"] = xres

    G = SSD_GROUPS
    gw = di // G
    r8 = _tall_rows(T, G)
    gate_args = [r8.row(y2[0], gw, 0, True), r8.row(y2[1], gw, 0, True), r8.row(xbc, gw, 0, True, stride=2),
                 r8.row(proj, gw, c_z // gw, True), r8.vec(w["dskip"], True), r8.vec(w["ssd_norm_w"], True)]
    (ynw,) = stage_fwd(n("ssd_gate"), f_ssd_gate, r8, gate_args, [(gw, ACT_DTYPE, True)])
    ex = _hosted(hosts, box, "ssd_out_mm")
    o_ssd = matmul_nn(n("ssd_out_mm"), ynw, w["w_ssd_out"], ex=ex)
    if ex is not None:
        o_ssd, box["ssd_out_mm"] = o_ssd

    nw = len(POOL_WINDOWS)
    pg = pw // nw
    (ps,) = stage_fwd(n("pool"), f_pool_all, rows, _pool_args(rows, proj, c_p // pw, pw, pc, w), [(pw, ACT_DTYPE, False)])
    o_pool = matmul_nn(n("pool_out_mm"), ps, w["w_pool_out"])

    merge_args = [rows.row(o_ssd, D), rows.row(o_pool, D), rows.row(proj, pw, c_g // pw), rows.row(proj, pw, c_g // pw + 1)]
    (mg,) = stage_fwd(n("merge"), f_merge, rows, merge_args, [(D, ACT_DTYPE, False)])
    mo = matmul_nn(n("out_mm"), mg, w["w_out"])

    rn_args = [rows.row(x, D), rows.row(mo, D), rows.segvec(ga1), rows.vec(w["g_ffn"]), rows.segvec(sh2), rows.segvec(sc2)]
    x1, h2 = stage_fwd(n("norm2"), f_resid_norm_mod, rows, rn_args, [(D, F32, False), (D, ACT_DTYPE, False)])
    ex = _hosted(hosts, box, "gate_up_mm")
    gu = matmul_nn(n("gate_up_mm"), h2, w["w_gate_up"], ex=ex)
    if ex is not None:
        gu, box["gate_up_mm"] = gu
    fh = gu.shape[1] // 2
    (act,) = stage_fwd(n("swiglu"), f_swiglu, rows, [rows.row(gu, 2 * fh)], [(fh, ACT_DTYPE, False)])
    ex = _hosted(hosts, box, "down_mm")
    dn = matmul_nn(n("down_mm"), act, w["w_down"], ex=ex)
    if ex is not None:
        dn, box["down_mm"] = dn
    saved = dict(x=x, pre=pre, mraw=mraw, mods=(sh1, sc1, ga1, sh2, sc2, ga2), h1=h1, proj=proj, dtraw=dtraw, xbc=xbc, y2=y2,
                 states=states,
                 ynw=ynw, o_ssd=o_ssd, ps=ps, o_pool=o_pool, mg=mg, mo=mo, x1=x1, h2=h2, gu=gu, act=act, dn=dn,
                 cols=(c_z, c_g, c_p, c_dt))
    return (x1, dn, ga2), saved


def _resid_norm_args(rows, pre, g, sh, sc, D):
    x1, dn, ga2 = pre
    return [rows.row(x1, D), rows.row(dn, D), rows.segvec(ga2), rows.vec(g), rows.segvec(sh), rows.segvec(sc)]


def f_norm_mod_keep(x, g, sh, sc):
    return f_norm_mod(x, g, sh, sc)[0], x


def _layer_bwd(l, cot, cond_s, w, s, rows, n_ctx, pc, hosts=None, box=None):
    dx1, ddn, dga2 = cot
    T, D = dx1.shape
    nt, nct, tm = rows.nt, rows.nct, rows.tm
    n = lambda t: f"l{l}_{t}_bwd"
    sh1, sc1, ga1, sh2, sc2, ga2 = s["mods"]
    c_z, c_g, c_p, c_dt = s["cols"]
    x, proj, xbc, y2, gu = s["x"], s["proj"], s["xbc"], s["y2"], s["gu"]
    g = {}
    if box is not None:
        box["g"] = g

    ex = _hosted(hosts, box, "down_dx")
    dact = matmul_nt(n("down_dx"), ddn, w["w_down"], ex=ex)
    if ex is not None:
        dact, box["down_dx"] = dact
    ex = _hosted(hosts, box, "down_dw")
    g["w_down"] = matmul_tn(n("down_dw"), s["act"], ddn, ex=ex)
    if ex is not None:
        g["w_down"], box["down_dw"] = g["w_down"]
    fh = gu.shape[1] // 2
    (dgu,) = stage_bwd(n("swiglu"), f_swiglu, rows, [rows.row(gu, 2 * fh)], [rows.row(dact, fh)], [ACT_DTYPE])
    dh2 = matmul_nt(n("gate_up_dx"), dgu, w["w_gate_up"])
    g["w_gate_up"] = matmul_tn(n("gate_up_dw"), s["h2"], dgu, blocks=w["w_gate_up"].shape[0])

    rn_args = [rows.row(x, D), rows.row(s["mo"], D), rows.segvec(ga1), rows.vec(w["g_ffn"]), rows.segvec(sh2), rows.segvec(sc2)]
    dxr, dmo, dga1, g["g_ffn"], dsh2, dsc2 = stage_bwd(
        n("norm2"), f_resid_norm_mod, rows, rn_args, [rows.row(dx1, D), rows.row(dh2, D)], [F32, ACT_DTYPE])
    dmg = matmul_nt(n("out_dx"), dmo, w["w_out"])
    g["w_out"] = matmul_tn(n("out_dw"), s["mg"], dmo)

    pw = w["pool_scale"].shape[1]
    merge_args = [rows.row(s["o_ssd"], D), rows.row(s["o_pool"], D), rows.row(proj, pw, c_g // pw), rows.row(proj, pw, c_g // pw + 1)]
    do_ssd, do_pool, dgl_s, dgl_p = stage_bwd(n("merge"), f_merge, rows, merge_args, [rows.row(dmg, D)], [ACT_DTYPE] * 4)
    dps = matmul_nt(n("pool_out_dx"), do_pool, w["w_pool_out"])
    g["w_pool_out"] = matmul_tn(n("pool_out_dw"), s["ps"], do_pool)

    nw = len(POOL_WINDOWS)
    pg = pw // nw
    du_pool, g["pool_scale"], *dpw = stage_bwd(n("pool"), f_pool_all, rows, _pool_args(rows, proj, c_p // pw, pw, pc, w),
                                               [rows.row(dps, pw)], [ACT_DTYPE])
    g["pool_w"] = jnp.stack(dpw)

    dynw = matmul_nt(n("ssd_out_dx"), do_ssd, w["w_ssd_out"])
    g["w_ssd_out"] = matmul_tn(n("ssd_out_dw"), s["ynw"], do_ssd)
    G = SSD_GROUPS
    di = w["ssd_norm_w"].shape[1]
    gw = di // G
    r8 = _tall_rows(T, G)
    gate_args = [r8.row(y2[0], gw, 0, True), r8.row(y2[1], gw, 0, True), r8.row(xbc, gw, 0, True, stride=2),
                 r8.row(proj, gw, c_z // gw, True), r8.vec(w["dskip"], True), r8.vec(w["ssd_norm_w"], True)]
    gate_args[1].kind = "const"
    ex = _hosted(hosts, box, "ssd_gate")
    res = stage_bwd(n("ssd_gate"), f_ssd_gate, r8, gate_args, [r8.row(dynw, gw, 0, True)], [ACT_DTYPE] * 3, ex)
    if ex is not None:
        res, box["ssd_gate"] = res
    dy, dxs_skip, dz, g["dskip"], g["ssd_norm_w"] = res

    ex = _hosted(hosts, box, "ssd")
    dxbc0, dxbc1, ddt, g["dt_bias"], g["a_log"], xres = ssd_bwd(n("ssd"), xbc, s["dtraw"], w["dt_bias"], w["a_log"],
                                                                s["states"], dy, n_ctx, ex)
    dxbc2 = (dxbc0, dxbc1)
    if ex is not None:
        box["ssd"] = xres
    xbc_w = xbc.shape[1]
    ex = _hosted(hosts, box, "conv")
    dxbc_raw, g["conv_w"], g["conv_b"], xres = conv_bwd(n("conv"), proj, w["conv_w"], w["conv_b"], dxbc2, dxs_skip,
                                                         n_ctx, xbc_w, ex)
    if ex is not None:
        box["conv"] = xres
    pieces = [dxbc_raw, dz, dgl_s, dgl_p, du_pool, ddt]
    offsets = [0, c_z, c_g, c_g + pw, c_p, c_dt]
    ex = _hosted(hosts, box, "in_dx")
    dh1 = matmul_nt(n("in_dx"), pieces, w["w1"], ex=ex, offsets=offsets)
    if ex is not None:
        dh1, box["in_dx"] = dh1
    ex = _hosted(hosts, box, "in_dw")
    first = matmul_tn(n("in_dw0"), pieces[0], s["h1"], ex=ex)
    if ex is not None:
        first, box["in_dw"] = first
    g["w1"] = [first] + [matmul_tn(n(f"in_dw{k}"), p, s["h1"]) for k, p in enumerate(pieces) if k]

    if isinstance(s["pre"], tuple):
        dx1p, ddnp, dga2p, g["g_mix"], dsh1, dsc1 = stage_bwd(
            n("norm1"), f_resid_norm_mod, rows, _resid_norm_args(rows, s["pre"], w["g_mix"], sh1, sc1, D),
            [rows.row(dxr, D), rows.row(dh1, D)], [F32, ACT_DTYPE])
        dx = (dx1p, ddnp, dga2p)
    else:
        n1_args = [rows.row(x, D), rows.vec(w["g_mix"]), rows.segvec(sh1), rows.segvec(sc1)]
        dx, g["g_mix"], dsh1, dsc1 = stage_bwd(n("norm1"), f_norm_mod_keep, rows, n1_args,
                                               [rows.row(dh1, D), rows.row(dxr, D)], [F32])

    dm = jnp.concatenate([v.reshape(2, D) for v in (dsh1, dsc1, dga1, dsh2, dsc2, dga2)], axis=1)
    dm = jnp.concatenate([dm, jnp.zeros((COND_ROWS - 2, dm.shape[1]), F32)], axis=0)
    crow = Rows(1, 0, COND_ROWS)
    dmraw, g["b_ada"] = stage_bwd(n("ada_bias"), f_bias, crow, [crow.row(s["mraw"], dm.shape[1]), crow.vec(w["b_ada"])],
                                  [crow.row(dm, dm.shape[1])], [ACT_DTYPE])
    dcs = matmul_nt(n("ada_dx"), dmraw, w["w_ada"])
    g["w_ada"] = matmul_tn(n("ada_dw"), cond_s, dmraw, blocks=w["w_ada"].shape[0])
    return dx, dcs, g


def local_step(x, ctx, c, c_ctx, target, layer_w_fn, n_layers, g_final, fwd_hosts=None, bwd_hosts=None):
    L, D = x.shape
    n_ctx = ctx.shape[0]
    tm = ROW_TILE
    T = L + n_ctx
    rows = Rows(T // tm, n_ctx // tm, tm)
    pc = _pool_consts(tm, n_ctx)
    xa = jnp.concatenate([ctx, x], axis=0)
    cond = jnp.concatenate([c_ctx.reshape(1, D), c.reshape(1, D), jnp.zeros((COND_ROWS - 2, D), F32)], axis=0)
    crow = Rows(1, 0, COND_ROWS)
    (cond_s,) = stage_fwd("cond_silu", f_silu, crow, [crow.row(cond, D)], [(D, ACT_DTYPE, False)])

    saved, layer_w = [], []
    for l in range(n_layers):
        layer_w.append(layer_w_fn(l))
        box = {}
        xa, s = _layer_fwd(l, xa, cond_s, layer_w[l], rows, n_ctx, pc, fwd_hosts(l, box) if fwd_hosts else None, box)
        saved.append(s)

    x1, dn, ga2 = xa
    rl = Rows(L // tm, 0, tm)
    gf = g_final.reshape(1, D)
    tgt = rl.row(target, D)
    tgt.kind = "const"
    off = n_ctx // tm
    loss_args = [rl.row(x1, D, roff=off), rl.row(dn, D, roff=off), rl.vec(ga2[1]), tgt, rl.vec(gf)]
    ones = jnp.ones((L, 1), F32)
    dx1_lat, ddn_lat, dga2_lat, dgf, loss_rows = stage_bwd("loss", f_loss_resid, rl, loss_args, [rl.row(ones, 1)],
                                                           [F32, ACT_DTYPE], primal=[(1, F32)])
    loss = jnp.sum(loss_rows)
    cot = (jnp.concatenate([jnp.zeros((n_ctx, D), F32), dx1_lat], axis=0),
           jnp.concatenate([jnp.zeros((n_ctx, D), ACT_DTYPE), ddn_lat], axis=0),
           jnp.stack([jnp.zeros((1, D), F32), dga2_lat]))

    grads = [None] * n_layers
    dcs = jnp.zeros((COND_ROWS, D), F32)
    for l in reversed(range(n_layers)):
        box = {}
        hosts = bwd_hosts(l, grads, box) if bwd_hosts else None
        cot, dcs_l, grads[l] = _layer_bwd(l, cot, cond_s, layer_w[l], saved[l], rows, n_ctx, pc, hosts, box)
        dcs = dcs + dcs_l
    dx = cot
    (dcond,) = stage_bwd("cond_silu_bwd", f_silu, crow, [crow.row(cond, D)], [crow.row(dcs, D)], [F32])
    return loss, dx[n_ctx:], grads, dcond[0], dgf


def gather_chips(halves, conv=None):
    n = len(halves)
    ops = list(halves) + ([conv] if conv is not None else [])

    def copies(ins, outs, pos):
        c, me = pos[2], _chip_index(pos)
        pairs = [(s.at[c], o.at[me, c]) for s, o in zip(ins[:n], outs[:n])]
        pairs += [(s, o.at[me]) for s, o in zip(ins[n:], outs[n:])]
        return pairs, [(s, d, _flip(pos, rel)) for rel in PLANE for s, d in pairs]

    shapes = [jax.ShapeDtypeStruct((4,) + s.shape, s.dtype) for s in ops]
    return Exchange(copies, 3 * len(ops), len(ops), ops, shapes)


def gather_pair(gathered):
    n = len(gathered)

    def copies(ins, outs, pos):
        c = pos[2]
        return [], [(s.at[b, c], o.at[b, c], _flip(pos, PAIR[0])) for s, o in zip(ins, outs) for b in range(4)]

    shapes = [jax.ShapeDtypeStruct(g.shape, g.dtype) for g in gathered]
    return Exchange(copies, 4 * n, 0, gathered, shapes, aliases={k: k for k in range(n)})


def swap_halves(grads):
    n = len(grads)

    def copies(ins, outs, pos):
        c = pos[2]
        return [], [(g.at[b, 1 - c], o.at[b], _flip(pos, PAIR[0])) for g, o in zip(ins, outs) for b in range(4)]

    shapes = [jax.ShapeDtypeStruct((g.shape[0],) + g.shape[2:], g.dtype) for g in grads]
    return Exchange(copies, 4 * n, 0, grads, shapes)


def scatter_chips(sums):
    n = len(sums)

    def copies(ins, outs, pos):
        me = _chip_index(pos)
        local = [(p.at[me], o.at[me]) for p, o in zip(ins, outs)]
        remote = []
        for rel in PLANE:
            peer = _flip(pos, rel)
            remote += [(p.at[_chip_index(peer)], o.at[me], peer) for p, o in zip(ins, outs)]
        return local, remote

    shapes = [jax.ShapeDtypeStruct(p.shape, p.dtype) for p in sums]
    return Exchange(copies, 3 * n, n, sums, shapes)


def share_halves(finals):
    n = len(finals)

    def copies(ins, outs, pos):
        c = pos[2]
        return [], [(f.at[c], o.at[c], _flip(pos, PAIR[0])) for f, o in zip(ins, outs)]

    shapes = [jax.ShapeDtypeStruct(f.shape, f.dtype) for f in finals]
    return Exchange(copies, n, 0, finals, shapes, aliases={k: k for k in range(n)})


def gather_everyone(vec):
    def copies(ins, outs, pos):
        me = _device_index(pos)
        (v,), (o,) = ins, outs
        return [(v, o.at[me])], [(v, o.at[me], _flip(pos, rel)) for rel in EVERYONE]

    return Exchange(copies, len(EVERYONE), 1, [vec], [jax.ShapeDtypeStruct((8,) + vec.shape, vec.dtype)])


def _row_tile(rows, cols, n_bufs, mult=8):
    cap = VMEM_LIMIT_BYTES // 2 // (2 * n_bufs * cols * 4)
    for t in range(min(rows, cap) // mult * mult, 0, -mult):
        if rows % t == 0:
            return t
    return rows


def _adamw_update(w, g, m, v):
    nm = ADAM_B1 * m + (1.0 - ADAM_B1) * g
    nv = ADAM_B2 * v + (1.0 - ADAM_B2) * jnp.square(g)
    m_hat = nm / (1.0 - ADAM_B1 ** ADAM_STEP)
    v_hat = nv / (1.0 - ADAM_B2 ** ADAM_STEP)
    return -ADAM_LR * (m_hat / (jnp.sqrt(v_hat) + ADAM_EPS) + ADAM_WD * w), nm, nv


def adamw_small(name, ws, gs, ms, vs):
    n = len(ws)

    def body(*refs):
        ins, outs = refs[:4 * n], refs[4 * n:]
        for k in range(n):
            d, nm, nv = _adamw_update(ins[k][...], ins[n + k][...], ins[2 * n + k][...], ins[3 * n + k][...])
            outs[k][...] = d
            outs[n + k][...] = nm
            outs[2 * n + k][...] = nv

    shapes = [jax.ShapeDtypeStruct(a.shape, F32) for a in ws]
    vmem = pl.BlockSpec(memory_space=pltpu.VMEM)
    res = _pcall(body, name=name, out_shape=shapes * 3, in_specs=[vmem] * (4 * n), out_specs=[vmem] * (3 * n),
                 compiler_params=pltpu.CompilerParams(vmem_limit_bytes=VMEM_LIMIT_BYTES))(*ws, *gs, *ms, *vs)
    return res[:n], res[n:2 * n], res[2 * n:]


WIRE_DTYPE = jnp.bfloat16


def add_own_half(name, grads, recv, c):
    nb, _, R, C = grads.shape
    tr = _row_tile(R, C, 3, mult=16)

    def body(c_ref, g_ref, r_ref, o_ref):
        o_ref[...] = (g_ref[...] + r_ref[...]).astype(o_ref.dtype)

    spec = pl.BlockSpec((None, tr, C), lambda b, i, c_ref: (b, i, 0))
    return _pcall(
        body, name=name, out_shape=jax.ShapeDtypeStruct(recv.shape, WIRE_DTYPE),
        grid_spec=pltpu.PrefetchScalarGridSpec(
            num_scalar_prefetch=1, grid=(nb, R // tr),
            in_specs=[pl.BlockSpec((None, None, tr, C), lambda b, i, c_ref: (b, c_ref[0], i, 0)), spec],
            out_specs=spec),
        compiler_params=_params("parallel", "parallel"),
    )(c, grads, recv)


def sum_slots(name, a, c=None):
    n, R, C = a.shape
    tr = _row_tile(R, C, n + 1, mult=16 if a.dtype.itemsize == 2 else 8)

    def body(*refs):
        a_ref, o_ref = refs[-2:]
        acc = a_ref[0].astype(F32)
        for k in range(1, n):
            acc = acc + a_ref[k].astype(F32)
        o_ref[...] = acc

    if c is None:
        return _pcall(
            body, name=name, out_shape=jax.ShapeDtypeStruct((R, C), F32), grid=(R // tr,),
            in_specs=[pl.BlockSpec((n, tr, C), lambda i: (0, i, 0))], out_specs=pl.BlockSpec((tr, C), lambda i: (i, 0)),
            compiler_params=_params("parallel"),
        )(a)
    return _pcall(
        body, name=name, out_shape=jax.ShapeDtypeStruct((2, R, C), F32),
        grid_spec=pltpu.PrefetchScalarGridSpec(
            num_scalar_prefetch=1, grid=(R // tr,),
            in_specs=[pl.BlockSpec((n, tr, C), lambda i, c_ref: (0, i, 0))],
            out_specs=pl.BlockSpec((None, tr, C), lambda i, c_ref: (c_ref[0], i, 0))),
        compiler_params=_params("parallel"),
    )(c, a)


def adamw(name, w, g_layers, m, v):
    nl, R, C = w.shape
    assert len(g_layers) == nl
    tr = _row_tile(R, C, 8 + nl)
    nr = R // tr

    def body(*refs):
        w_ref, m_ref, v_ref = refs[:3]
        g_refs = refs[3:3 + nl]
        go_ref, d_ref, nm_ref, nv_ref = refs[3 + nl:]
        l = pl.program_id(0)
        gr = g_refs[0][...]
        for k in range(1, nl):
            gr = jnp.where(l == k, g_refs[k][...], gr)
        d_ref[...], nm_ref[...], nv_ref[...] = _adamw_update(w_ref[...], gr, m_ref[...], v_ref[...])
        go_ref[...] = gr

    spec = pl.BlockSpec((None, tr, C), lambda l, i: (l, i, 0))
    g_specs = [pl.BlockSpec((tr, C), (lambda l, i, k=k: (jnp.where(l == k, i, jnp.where(l < k, 0, nr - 1)), 0)))
               for k in range(nl)]
    return _pcall(
        body, name=name, out_shape=[jax.ShapeDtypeStruct((nl, R, C), F32)] * 4, grid=(nl, nr),
        in_specs=[spec] * 3 + g_specs, out_specs=[spec] * 4, compiler_params=_params("arbitrary", "arbitrary"),
    )(w, m, v, *g_layers)


BIG = ("w_ada", "w_in", "w_ssd_out", "pool_w", "w_pool_out", "w_out", "w_gate_up", "w_down")
COL_SHARDED = ("w_ada", "w_in", "w_gate_up")
BLOCK_LAYOUT = ("w_ada", "w_gate_up")
GRAD_TRANSPOSED = ("w_in",)
FIRST_USED = ("w_ada", "w_in")
MID_USED = ("w_ssd_out", "pool_w", "w_pool_out", "w_out")
END_USED = ("w_gate_up", "w_down")
LATER_USED = MID_USED + END_USED
assert FIRST_USED + LATER_USED == BIG
READY_LAST = FIRST_USED
READY_EARLY = LATER_USED
SMALL = ("c_ctx", "b_ada", "g_mix", "conv_w", "conv_b", "dt_bias", "a_log", "d_skip", "ssd_norm_w", "pool_scale",
         "g_ffn", "g_final")
WEIGHTS = ("c_ctx", "w_ada", "b_ada", "g_mix", "w_in", "conv_w", "conv_b", "dt_bias", "a_log", "d_skip", "ssd_norm_w",
           "w_ssd_out", "pool_w", "pool_scale", "w_pool_out", "w_out", "g_ffn", "w_gate_up", "w_down", "g_final")
LAYER_KEYS = ("w_ada", "b_ada", "g_mix", "w_in", "conv_w", "conv_b", "dt_bias", "a_log", "d_skip", "ssd_norm_w",
              "w_ssd_out", "pool_w", "pool_scale", "w_pool_out", "w_out", "g_ffn", "w_gate_up", "w_down")


def _shard2d(name, a):
    if name == "pool_w":
        return a.reshape(a.shape[0], a.shape[1] * a.shape[2], a.shape[3])
    return a


def _full_from_blocks(name, a):
    nb, R, C = a.shape
    if name in BLOCK_LAYOUT:
        return a
    if name in COL_SHARDED:
        return jnp.transpose(a, (1, 0, 2)).reshape(R, nb * C)
    if name == "pool_w":
        nw = len(POOL_WINDOWS)
        return jnp.transpose(a.reshape(nb, nw, R // nw, C), (1, 0, 2, 3)).reshape(nw, nb * R // nw, C)
    return a.reshape(nb * R, C)


def _blocks_from_full(name, g):
    nb = 4
    if name in BLOCK_LAYOUT:
        return g
    if name in COL_SHARDED and name not in GRAD_TRANSPOSED:
        K, N = g.shape
        return jnp.transpose(g.reshape(K, nb, N // nb), (1, 0, 2))
    if name == "pool_w":
        nw, r, C = g.shape
        return jnp.transpose(g.reshape(nw, nb, r // nb, C), (1, 0, 2, 3)).reshape(nb, nw * r // nb, C)
    return g.reshape(nb, g.shape[0] // nb, g.shape[1])


def _pack(arrs, rows):
    flat = jnp.concatenate([a.reshape(-1).astype(F32) for a in arrs])
    return jnp.concatenate([flat, jnp.zeros((rows * 128 - flat.size,), F32)]).reshape(rows, 128)


def _unpack(vec, shapes):
    flat = vec.reshape(-1)
    out, o = [], 0
    for s in shapes:
        n = int(np.prod(s))
        out.append(flat[o:o + n].reshape(s))
        o += n
    return out


def _rows_for(shapes):
    n = sum(int(np.prod(s)) for s in shapes)
    return -(-n // (8 * 128)) * 8


def kernel(x, c, ctx, c_ctx, w_ada, b_ada, g_mix, w_in, conv_w, conv_b, dt_bias, a_log, d_skip, ssd_norm_w, w_ssd_out, pool_w, pool_scale, w_pool_out, w_out, g_ffn, w_gate_up, w_down, g_final, loss_target, m_c_ctx, m_w_ada, m_b_ada, m_g_mix, m_w_in, m_conv_w, m_conv_b, m_dt_bias, m_a_log, m_d_skip, m_ssd_norm_w, m_w_ssd_out, m_pool_w, m_pool_scale, m_w_pool_out, m_w_out, m_g_ffn, m_w_gate_up, m_w_down, m_g_final, v_c_ctx, v_w_ada, v_b_ada, v_g_mix, v_w_in, v_conv_w, v_conv_b, v_dt_bias, v_a_log, v_d_skip, v_ssd_norm_w, v_w_ssd_out, v_pool_w, v_pool_scale, v_w_pool_out, v_w_out, v_g_ffn, v_w_gate_up, v_w_down, v_g_final):
    w = dict(c_ctx=c_ctx, w_ada=w_ada, b_ada=b_ada, g_mix=g_mix, w_in=w_in, conv_w=conv_w, conv_b=conv_b, dt_bias=dt_bias,
             a_log=a_log, d_skip=d_skip, ssd_norm_w=ssd_norm_w, w_ssd_out=w_ssd_out, pool_w=pool_w, pool_scale=pool_scale,
             w_pool_out=w_pool_out, w_out=w_out, g_ffn=g_ffn, w_gate_up=w_gate_up, w_down=w_down, g_final=g_final)
    m = dict(c_ctx=m_c_ctx, w_ada=m_w_ada, b_ada=m_b_ada, g_mix=m_g_mix, w_in=m_w_in, conv_w=m_conv_w, conv_b=m_conv_b,
             dt_bias=m_dt_bias, a_log=m_a_log, d_skip=m_d_skip, ssd_norm_w=m_ssd_norm_w, w_ssd_out=m_w_ssd_out,
             pool_w=m_pool_w, pool_scale=m_pool_scale, w_pool_out=m_w_pool_out, w_out=m_w_out, g_ffn=m_g_ffn,
             w_gate_up=m_w_gate_up, w_down=m_w_down, g_final=m_g_final)
    v = dict(c_ctx=v_c_ctx, w_ada=v_w_ada, b_ada=v_b_ada, g_mix=v_g_mix, w_in=v_w_in, conv_w=v_conv_w, conv_b=v_conv_b,
             dt_bias=v_dt_bias, a_log=v_a_log, d_skip=v_d_skip, ssd_norm_w=v_ssd_norm_w, w_ssd_out=v_w_ssd_out,
             pool_w=v_pool_w, pool_scale=v_pool_scale, w_pool_out=v_w_pool_out, w_out=v_w_out, g_ffn=v_g_ffn,
             w_gate_up=v_w_gate_up, w_down=v_w_down, g_final=v_g_final)
    assert x.shape[0] == 1, "one example per device"
    pos = _position()
    core = pos[2].astype(jnp.int32).reshape(1)
    n_layers = w_in.shape[0]
    assert n_layers == 2
    dims = (ssd_norm_w.shape[1], conv_w.shape[2] * 4, dt_bias[0].size, pool_scale.shape[1])
    shard = {k: _shard2d(k, w[k]) for k in BIG}

    def halves(a):
        return a.reshape(a.shape[:-2] + (2, a.shape[-2] // 2, a.shape[-1]))

    def whole(a):
        return a.reshape(a.shape[:-3] + (2 * a.shape[-2], a.shape[-1]))

    def wire_shards(l, names):
        return [halves(shard[k][l].astype(MXU_DTYPE)) for k in names]

    def full_weights(names, gathered):
        return {k: _full_from_blocks(k, whole(a)) for k, a in zip(names, gathered)}

    first = comm_call("gather0_chips", gather_chips(wire_shards(0, FIRST_USED), conv=conv_w))
    got0 = full_weights(FIRST_USED, comm_call("gather0_pair", gather_pair(first[:-1])))
    conv_all = first[-1]
    conv_full = [jnp.transpose(conv_all[:, l], (1, 0, 2)).reshape(conv_all.shape[2], -1) for l in range(n_layers)]

    boxes = {}

    n_first, n_mid = len(FIRST_USED), len(MID_USED)

    n_first, n_end = len(FIRST_USED), len(END_USED)

    def layer_w_fn(l):
        if l == 0:
            full = dict(got0)
        else:
            f0 = boxes[("fwd", 0)]
            full = full_weights(("w_in",), f0["gate_up_mm"][:1])
            full.update(full_weights(("w_ada",), f0["down_mm"]))
        late = {k: (lambda i=i: boxes[("fwd", l)]["conv"][i]) for i, k in enumerate(MID_USED)}
        late.update({k: (lambda i=i: boxes[("fwd", l)]["ssd_out_mm"][i]) for i, k in enumerate(END_USED)})
        full["conv_w"] = conv_full[l]
        lw = LazyDict(_prep_layer_weights(*[full[k] if k in full else (None if k in late else w[k][l]) for k in LAYER_KEYS]))
        for k, get in late.items():
            lw[k] = (lambda k=k, get=get: _full_from_blocks(k, whole(get())))
        return lw

    def fwd_hosts(l, box):
        boxes[("fwd", l)] = box
        hosts = {"in_mm": lambda box: gather_chips(wire_shards(l, MID_USED)),
                 "conv": lambda box: gather_pair(box["in_mm"]),
                 "ssd": lambda box: gather_chips(wire_shards(l, END_USED)),
                 "ssd_out_mm": lambda box: gather_pair(box["ssd"][:n_end])}
        if l == 0:
            hosts["ssd"] = lambda box: combine(gather_chips(wire_shards(0, END_USED)), gather_chips(wire_shards(1, ("w_in",))))
            hosts["gate_up_mm"] = lambda box: combine(gather_pair(box["ssd"][n_end:]),
                                                      gather_chips(wire_shards(1, ("w_ada",))))
            hosts["down_mm"] = lambda box: gather_pair(box["gate_up_mm"][1:])
        return hosts

    def blocks(gl, names):
        return [halves(_blocks_from_full(k, gl[k])) for k in names]

    def pair_sums(tag, names, G, recv):
        return [add_own_half(f"pair_sum{tag}_{k}", g, r, core) for k, g, r in zip(names, G, recv)]

    def chip_sums(tag, names, parts):
        return [sum_slots(f"chip_sum{tag}_{k}", p, core) for k, p in zip(names, parts)]

    def reduce_now(tag, gl, names):
        G = blocks(gl, names)
        pair = pair_sums(tag, names, G, comm_call(f"swap{tag}", swap_halves(G)))
        return chip_sums(tag, names, comm_call(f"scatter{tag}", scatter_chips(pair)))

    small_layers = {}
    n_big = len(BIG)

    def bwd_hosts(l, grads, box):
        boxes[("bwd", l)] = box
        if l != 0:
            return None
        gl1 = _unprep_layer_grads(grads[1], dims)
        small_layers[1] = gl1
        G1 = blocks(gl1, BIG)
        early = {}

        def gate_host(box):
            early["G"] = blocks(box["g"], READY_EARLY)
            return swap_halves(early["G"])

        def scan_host(box):
            return combine(scatter_chips(pair_sums("1", BIG, G1, box["down_dx"] + box["down_dw"])),
                           scatter_chips(pair_sums("0e", READY_EARLY, early["G"], box["ssd_gate"])))

        def conv_host(box):
            return combine(share_halves(chip_sums("1", BIG, box["ssd"][:n_big])),
                           share_halves(chip_sums("0e", READY_EARLY, box["ssd"][n_big:])))

        return {"down_dx": lambda box: swap_halves(G1[:n_first]), "down_dw": lambda box: swap_halves(G1[n_first:]),
                "ssd_gate": gate_host, "ssd": scan_host, "in_dx": conv_host}

    loss, grad_x, grads, d_c_ctx, d_g_final = local_step(
        x[0], ctx[0], c[0], c_ctx, loss_target[0], layer_w_fn, n_layers, g_final, fwd_hosts, bwd_hosts)
    shared =[whole(a) for a in boxes[("bwd", 0)]["in_dx"]]
    reduced1 = shared[:n_big]
    gl0 = _unprep_layer_grads(grads[0], dims)
    small_layers[0] = gl0
    last_halves = reduce_now("0", gl0, READY_LAST)

    small_full = dict(c_ctx=d_c_ctx, g_final=d_g_final.reshape(-1))
    for k in SMALL:
        if k not in small_full:
            small_full[k] = jnp.stack([small_layers[l][k] for l in range(n_layers)])
    shapes = [small_full[k].shape for k in SMALL] + [(1,)]
    packed = _pack([small_full[k] for k in SMALL] + [loss.reshape(1)], _rows_for(shapes))
    *last, everyone = comm_call("share0_small", combine(share_halves(last_halves), gather_everyone(packed)))
    red0 = dict(zip(READY_EARLY, shared[n_big:]))
    red0.update(zip(READY_LAST, [whole(a) for a in last]))
    reduced0 = [red0[k] for k in BIG]
    total = sum_slots("small_sum", everyone)
    *small_vals, loss = _unpack(total, shapes)
    loss = loss.reshape(())
    small_g = dict(zip(SMALL, small_vals))
    cw = conv_w.shape[2]
    small_g["conv_w"] = lax.dynamic_slice_in_dim(small_g["conv_w"], _chip_index(pos) * cw, cw, axis=2)

    grad, delta, new_m, new_v = {}, {}, {}, {}
    for k, g0, g1 in zip(BIG, reduced0, reduced1):
        shp = w[k].shape
        if k in GRAD_TRANSPOSED:
            flat = lambda a: jnp.swapaxes(a, 1, 2)
            back = lambda a: jnp.swapaxes(a, 1, 2)
        else:
            flat = lambda a: _shard2d(k, a)
            back = lambda a: a.reshape(shp)
        outs = adamw(f"adamw_{k}", flat(w[k]), [g0, g1], flat(m[k]), flat(v[k]))
        grad[k], delta[k], new_m[k], new_v[k] = [back(a) for a in outs]
    flat2 = lambda d: [d[k].reshape(-1, d[k].shape[-1]) for k in SMALL]
    d_, m_, v_ = adamw_small("adamw_small", flat2(w), flat2(small_g), flat2(m), flat2(v))
    for k, dd, mm, vv in zip(SMALL, d_, m_, v_):
        shp = w[k].shape
        grad[k], delta[k], new_m[k], new_v[k] = small_g[k], dd.reshape(shp), mm.reshape(shp), vv.reshape(shp)

    return (loss, grad_x[None], *[grad[k] for k in WEIGHTS], *[delta[k] for k in WEIGHTS],
            *[new_m[k] for k in WEIGHTS], *[new_v[k] for k in WEIGHTS])
```

```python
import functools

import jax
import jax.numpy as jnp
import numpy as np
from jax import lax
from jax.experimental import pallas as pl
from jax.experimental.pallas import tpu as pltpu

F32 = jnp.float32
MXU_DTYPE = jnp.bfloat16
ACT_DTYPE = jnp.bfloat16
VMEM_LIMIT_BYTES = 48 * 1024 * 1024
MATMUL_VMEM_LIMIT_BYTES = 56 * 1024 * 1024
EPS = 1e-6
NEG = -1e30

SSD_HEADDIM = 64
SSD_GROUPS = 8
SSD_STATE = 128
SSD_CHUNK = 128
SSD_GROUPS_PER_STEP = 8
SSD_CONV = 5
GRID_W = 64
POOL_WINDOWS = (2, 4, 8, 16)
ROW_TILE = 256
DT_PAD = 512

ADAM_LR = 0.001
ADAM_B1 = 0.9
ADAM_B2 = 0.999
ADAM_EPS = 1e-08
ADAM_WD = 0.01
ADAM_STEP = 10

MESH = pl.DeviceIdType.MESH


def _pcall(body, **kw):
    return pl.pallas_call(body, **kw)


def _params(*sem):
    return pltpu.CompilerParams(dimension_semantics=tuple(sem), vmem_limit_bytes=VMEM_LIMIT_BYTES)


def _pick_tile(n, cands):
    for t in cands:
        if n % t == 0:
            return t
    return n


PLANE = ((1, 0, 0), (0, 1, 0), (1, 1, 0))
PAIR = ((0, 0, 1),)
EVERYONE = tuple((a, b, d) for a in (0, 1) for b in (0, 1) for d in (0, 1) if a + b + d)
HBM = pl.BlockSpec(memory_space=pl.ANY)


def _position():
    return lax.axis_index("x"), lax.axis_index("y"), lax.axis_index("c")


def _flip(pos, rel):
    return tuple(1 - p if r else p for p, r in zip(pos, rel))


def _chip_index(pos):
    return 2 * pos[0] + pos[1]


def _device_index(pos):
    return 4 * pos[0] + 2 * pos[1] + pos[2]


class Exchange:
    def __init__(self, copies, n_remote, n_local, operands, out_shapes, aliases=None):
        self.copies, self.n_remote, self.n_local = copies, n_remote, n_local
        self.operands, self.out_shapes, self.aliases = list(operands), list(out_shapes), dict(aliases or {})

    def scratch(self):
        return [pltpu.SemaphoreType.DMA((max(self.n_remote, 1),)), pltpu.SemaphoreType.DMA((max(self.n_remote, 1),)),
                pltpu.SemaphoreType.DMA((max(self.n_local, 1),))]

    def descriptors(self, ins, outs, sems):
        send_sems, recv_sems, local_sems = sems
        local, remote = self.copies(ins, outs, _position())
        assert len(local) == self.n_local and len(remote) == self.n_remote
        cps = [pltpu.make_async_copy(src, dst, local_sems.at[k]) for k, (src, dst) in enumerate(local)]
        cps += [pltpu.make_async_remote_copy(src_ref=src, dst_ref=dst, send_sem=send_sems.at[k], recv_sem=recv_sems.at[k],
                                             device_id=peer, device_id_type=MESH) for k, (src, dst, peer) in enumerate(remote)]
        return cps


def combine(a, b):
    na, nao = len(a.operands), len(a.out_shapes)

    def copies(ins, outs, pos):
        la, ra = a.copies(ins[:na], outs[:nao], pos)
        lb, rb = b.copies(ins[na:], outs[nao:], pos)
        return la + lb, ra + rb

    aliases = dict(a.aliases)
    aliases.update({na + k: nao + v for k, v in b.aliases.items()})
    return Exchange(copies, a.n_remote + b.n_remote, a.n_local + b.n_local, a.operands + b.operands,
                    a.out_shapes + b.out_shapes, aliases)


class LazyDict(dict):
    def __getitem__(self, key):
        v = dict.__getitem__(self, key)
        if callable(v):
            v = v()
            dict.__setitem__(self, key, v)
        return v


def comm_call(name, ex):
    n_in, n_out = len(ex.operands), len(ex.out_shapes)

    def body(*refs):
        cps = ex.descriptors(refs[:n_in], refs[n_in:n_in + n_out], refs[n_in + n_out:])
        for cp in cps:
            cp.start()
        for cp in cps:
            cp.wait()

    return _pcall(
        body, name=name, out_shape=ex.out_shapes, in_specs=[HBM] * n_in, out_specs=[HBM] * n_out,
        scratch_shapes=ex.scratch(), input_output_aliases=ex.aliases,
        compiler_params=pltpu.CompilerParams(has_side_effects=True),
    )(*ex.operands)


def hosted_call(body, ex, operands, *, name, out_shape, grid, in_specs, out_specs, scratch_shapes=(),
                vmem_limit=None):
    n_in, n_out, n_scr = len(operands), len(out_shape), len(scratch_shapes)
    sem = ("arbitrary",) * len(grid)
    vmem_limit = vmem_limit or VMEM_LIMIT_BYTES
    if ex is None:
        res = _pcall(body, name=name, out_shape=list(out_shape), grid=grid, in_specs=list(in_specs),
                     out_specs=list(out_specs), scratch_shapes=list(scratch_shapes),
                     compiler_params=pltpu.CompilerParams(dimension_semantics=sem, vmem_limit_bytes=vmem_limit))(*operands)
        return res, []
    x_in, x_out = len(ex.operands), len(ex.out_shapes)

    def wrapped(*refs):
        o = 0
        ins = refs[o:o + n_in]; o += n_in
        xins = refs[o:o + x_in]; o += x_in
        outs = refs[o:o + n_out]; o += n_out
        xouts = refs[o:o + x_out]; o += x_out
        scr = refs[o:o + n_scr]; o += n_scr
        sems = refs[o:]
        first = last = None
        for a, n in enumerate(grid):
            i = pl.program_id(a)
            first = (i == 0) if first is None else first & (i == 0)
            last = (i == n - 1) if last is None else last & (i == n - 1)

        @pl.when(first)
        def _():
            for cp in ex.descriptors(xins, xouts, sems):
                cp.start()

        body(*ins, *outs, *scr)

        @pl.when(last)
        def _():
            for cp in ex.descriptors(xins, xouts, sems):
                cp.wait()

    aliases = {n_in + k: n_out + v for k, v in ex.aliases.items()}
    res = _pcall(
        wrapped, name=name, out_shape=list(out_shape) + ex.out_shapes, grid=grid,
        in_specs=list(in_specs) + [HBM] * x_in, out_specs=list(out_specs) + [HBM] * x_out,
        scratch_shapes=list(scratch_shapes) + ex.scratch(), input_output_aliases=aliases,
        compiler_params=pltpu.CompilerParams(dimension_semantics=sem, vmem_limit_bytes=vmem_limit,
                                             has_side_effects=True),
    )(*operands, *ex.operands)
    return res[:n_out], res[n_out:]


def _dot(a, b, dims):
    return lax.dot_general(a.astype(MXU_DTYPE), b.astype(MXU_DTYPE), (dims, ((), ())), preferred_element_type=F32)


_NN = ((1,), (0,))
_NT = ((1,), (1,))
_TN = ((0,), (0,))


@jax.custom_vjp
def _mm(a, b):
    return _dot(a, b, _NN)


def _mm_fwd(a, b):
    return _mm(a, b), (a, b)


def _mm_bwd(res, g):
    a, b = res
    return _dot(g, b, _NT).astype(a.dtype), _dot(a, g, _TN).astype(b.dtype)


_mm.defvjp(_mm_fwd, _mm_bwd)


@jax.custom_vjp
def _mm_nt(a, b):
    return _dot(a, b, _NT)


def _mm_nt_fwd(a, b):
    return _mm_nt(a, b), (a, b)


def _mm_nt_bwd(res, g):
    a, b = res
    return _dot(g, b, _NN).astype(a.dtype), _dot(g, a, _TN).astype(b.dtype)


_mm_nt.defvjp(_mm_nt_fwd, _mm_nt_bwd)


@jax.custom_vjp
def _mm_tn(a, b):
    return _dot(a, b, _TN)


def _mm_tn_fwd(a, b):
    return _mm_tn(a, b), (a, b)


def _mm_tn_bwd(res, g):
    a, b = res
    return _dot(b, g, _NT).astype(a.dtype), _dot(a, g, _NN).astype(b.dtype)


_mm_tn.defvjp(_mm_tn_fwd, _mm_tn_bwd)


def _dot_exact(m01, v):
    m = m01.astype(jnp.bfloat16)
    hi = v.astype(jnp.bfloat16)
    r1 = v - hi.astype(F32)
    mid = r1.astype(jnp.bfloat16)
    lo = (r1 - mid.astype(F32)).astype(jnp.bfloat16)
    out = jnp.dot(m, hi, preferred_element_type=F32)
    out = out + jnp.dot(m, mid, preferred_element_type=F32)
    return out + jnp.dot(m, lo, preferred_element_type=F32)


@jax.custom_vjp
def _lin01(m, mt, v):
    return _dot_exact(m, v)


def _lin01_fwd(m, mt, v):
    return _dot_exact(m, v), (m, mt)


def _lin01_bwd(res, g):
    m, mt = res
    return jnp.zeros_like(m), jnp.zeros_like(mt), _dot_exact(mt, g)


_lin01.defvjp(_lin01_fwd, _lin01_bwd)


MATMUL_VMEM_BUDGET = MATMUL_VMEM_LIMIT_BYTES * 5 // 6


def _mm_tiles(m, n, k_bytes_a, k_bytes_b, out_bytes, cands_m, cands_n):
    best = None
    for tm in cands_m:
        if m % tm:
            continue
        for tn in cands_n:
            if n % tn:
                continue
            need = 2 * (tm * k_bytes_a + tn * k_bytes_b + tm * tn * out_bytes)
            if need <= MATMUL_VMEM_BUDGET and (best is None or tm * tn > best[0] * best[1]):
                best = (tm, tn)
    assert best is not None, (m, n)
    return best


_ROW_CANDS = (4352, 2176, 1088, 768, 544, 512, 272, 256, 128, 16)
_COL_CANDS = (2816, 2048, 1408, 1024, 512, 256, 128)


def _one(res, xres, ex):
    return res[0] if ex is None else (res[0], xres)


def _block_cands(c):
    return (c,) + tuple(t for t in (512, 256, 128) if c % t == 0)


def matmul_nn(name, a, b, out_dtype=F32, ex=None, col0=0, ncols=None):
    M, K = a.shape
    if b.ndim == 3:
        nb, _, C = b.shape
        N, cands = nb * C, _block_cands(C)
    else:
        N, cands = (b.shape[1] - col0 if ncols is None else ncols), (512, 256, 128)
    tm, tn = _mm_tiles(M, N, K * a.dtype.itemsize, K * b.dtype.itemsize, jnp.dtype(out_dtype).itemsize,
                       _ROW_CANDS, cands)
    if b.ndim == 3:
        per = C // tn
        b_spec = pl.BlockSpec((None, K, tn), lambda j, i: (j // per, 0, j % per))
    else:
        assert col0 % tn == 0
        first = col0 // tn
        b_spec = pl.BlockSpec((K, tn), lambda j, i: (0, first + j))

    def body(a_ref, b_ref, o_ref):
        o_ref[...] = _dot(a_ref[...], b_ref[...], _NN).astype(o_ref.dtype)

    res, xres = hosted_call(
        body, ex, [a, b], name=name, out_shape=[jax.ShapeDtypeStruct((M, N), out_dtype)], grid=(N // tn, M // tm),
        in_specs=[pl.BlockSpec((tm, K), lambda j, i: (i, 0)), b_spec],
        out_specs=[pl.BlockSpec((tm, tn), lambda j, i: (i, j))], vmem_limit=MATMUL_VMEM_LIMIT_BYTES)
    return _one(res, xres, ex)


def matmul_nt(name, g, b, out_dtype=F32, ex=None, offsets=None):
    pieces = list(g) if isinstance(g, (list, tuple)) else [g]
    offsets = list(offsets) if offsets is not None else [0]
    M = pieces[0].shape[0]
    if b.ndim == 3:
        nb, K, C = b.shape
        N = nb * C
        assert len(pieces) == 1
    else:
        K, N = b.shape
    g_bytes = sum(p.shape[1] * p.dtype.itemsize for p in pieces)
    tm, tk = _mm_tiles(M, K, g_bytes, N * b.dtype.itemsize, jnp.dtype(out_dtype).itemsize, _ROW_CANDS, _COL_CANDS)

    def body(*refs):
        b_ref, o_ref = refs[-2:]
        acc = None
        if b.ndim == 3:
            parts = [_dot(refs[0][:, k * C:(k + 1) * C], b_ref[k], _NT) for k in range(nb)]
        else:
            parts = [_dot(g_ref[...], b_ref[:, off:off + g_ref.shape[1]], _NT) for g_ref, off in zip(refs[:-2], offsets)]
        for part in parts:
            acc = part if acc is None else acc + part
        o_ref[...] = acc.astype(o_ref.dtype)

    b_spec = (pl.BlockSpec((nb, tk, C), lambda j, i: (0, j, 0)) if b.ndim == 3
              else pl.BlockSpec((tk, N), lambda j, i: (j, 0)))
    res, xres = hosted_call(
        body, ex, pieces + [b], name=name, out_shape=[jax.ShapeDtypeStruct((M, K), out_dtype)], grid=(K // tk, M // tm),
        in_specs=[pl.BlockSpec((tm, p.shape[1]), lambda j, i: (i, 0)) for p in pieces] + [b_spec],
        out_specs=[pl.BlockSpec((tm, tk), lambda j, i: (i, j))], vmem_limit=MATMUL_VMEM_LIMIT_BYTES)
    return _one(res, xres, ex)


def matmul_tn(name, a, g, ex=None, blocks=1):
    M, K = a.shape
    N = g.shape[1]
    C = N // blocks
    tk, tn = _mm_tiles(K, N, M * a.dtype.itemsize, M * g.dtype.itemsize, 4, (512, 256, 128),
                       (512, 256, 128) if blocks == 1 else _block_cands(C))

    def body(a_ref, g_ref, o_ref):
        o_ref[...] = _dot(a_ref[...], g_ref[...], _TN)

    if blocks == 1:
        out_shape, out_spec = jax.ShapeDtypeStruct((K, N), F32), pl.BlockSpec((tk, tn), lambda i, j: (i, j))
    else:
        per = C // tn
        out_shape = jax.ShapeDtypeStruct((blocks, K, C), F32)
        out_spec = pl.BlockSpec((None, tk, tn), lambda i, j: (j // per, i, j % per))
    res, xres = hosted_call(
        body, ex, [a, g], name=name, out_shape=[out_shape], grid=(K // tk, N // tn),
        in_specs=[pl.BlockSpec((M, tk), lambda i, j: (0, i)), pl.BlockSpec((M, tn), lambda i, j: (0, j))],
        out_specs=[out_spec], vmem_limit=MATMUL_VMEM_LIMIT_BYTES)
    return _one(res, xres, ex)


class Arg:
    def __init__(self, arr, block, imap, kind):
        self.arr, self.block, self.imap, self.kind = arr, block, imap, kind


class Rows:
    def __init__(self, nt, nct, tm, ncol=1):
        self.nt, self.nct, self.tm, self.ncol = nt, nct, tm, ncol

    def seg(self, i):
        return jnp.where(i >= self.nct, 1, 0)

    def spec(self, block, imap):
        return pl.BlockSpec(block, lambda j, i: imap(j, i, self.seg(i)))

    def row(self, arr, width, cb0=0, follow=False, roff=0, stride=1):
        f = stride if follow else 0
        return Arg(arr, (self.tm, width), lambda j, i, s: (i + roff, cb0 + f * j), "row")

    def vec(self, arr, follow=False, kind="acc"):
        w = arr.shape[1] // (self.ncol if follow else 1)
        f = 1 if follow else 0
        return Arg(arr, (1, w), lambda j, i, s: (0, f * j), kind)

    def segvec(self, arr, kind="seg"):
        return Arg(arr, (None, 1, arr.shape[2]), lambda j, i, s: (s, 0, 0), kind)


def _load(ref):
    return ref[...].astype(F32) if ref.dtype != F32 else ref[...]


def stage_fwd(name, f, rows, args, outs):
    n_in = len(args)

    def body(*refs):
        vals = [_load(r) for r in refs[:n_in]]
        res = f(*vals)
        for r, v in zip(refs[n_in:], res):
            r[...] = v.astype(r.dtype)

    T = rows.nt * rows.tm
    out_shape = [jax.ShapeDtypeStruct((T, w * (rows.ncol if fo else 1)), dt) for w, dt, fo in outs]
    out_specs = [pl.BlockSpec((rows.tm, w), (lambda j, i, fo=fo: (i, j if fo else 0))) for w, dt, fo in outs]
    res = _pcall(
        body, name=name, out_shape=out_shape, grid=(rows.ncol, rows.nt),
        in_specs=[rows.spec(a.block, a.imap) for a in args], out_specs=out_specs,
        compiler_params=_params("parallel", "parallel"),
    )(*[a.arr for a in args])
    return res


def stage_bwd(name, f, rows, args, cots, row_dtypes, ex=None, primal=()):
    n_in, n_ct = len(args), len(cots)
    diff = [k for k, a in enumerate(args) if a.kind != "const"]
    row_dt = {}
    for k in diff:
        if args[k].kind == "row":
            row_dt[k] = row_dtypes[len(row_dt)]

    def body(*refs):
        i = pl.program_id(1)
        vals = [_load(r) for r in refs[:n_in]]
        cts = tuple(_load(r) for r in refs[n_in:n_in + n_ct])
        outs = refs[n_in + n_ct:]

        def g(*dv):
            full = list(vals)
            for k, v in zip(diff, dv):
                full[k] = v
            return tuple(f(*full))

        prim, vjp = jax.vjp(g, *[vals[k] for k in diff])
        grads = vjp(cts)
        for o, v in zip(outs[len(diff):], prim):
            o[...] = v.astype(o.dtype)
        for k, o, gr in zip(diff, outs, grads):
            kind = args[k].kind
            if kind == "row":
                o[...] = gr.astype(o.dtype)
            else:
                first = (i == 0) | (i == rows.nct) if kind == "seg" else (i == 0)

                @pl.when(first)
                def _():
                    o[...] = gr.astype(o.dtype)

                @pl.when(jnp.logical_not(first))
                def _():
                    o[...] += gr.astype(o.dtype)

    T = rows.nt * rows.tm
    out_shape, out_specs = [], []
    for k in diff:
        a = args[k]
        if a.kind == "row":
            out_shape.append(jax.ShapeDtypeStruct((T, a.block[1] * (rows.ncol if _follows(a) else 1)), row_dt[k]))
            fo = _follows(a)
            out_specs.append(pl.BlockSpec(a.block, (lambda j, i, fo=fo: (i, j if fo else 0))))
        else:
            out_shape.append(jax.ShapeDtypeStruct(a.arr.shape, F32))
            out_specs.append(rows.spec(a.block, a.imap))
    for w, dt in primal:
        out_shape.append(jax.ShapeDtypeStruct((T, w), dt))
        out_specs.append(pl.BlockSpec((rows.tm, w), lambda j, i: (i, 0)))
    res, xres = hosted_call(
        body, ex, [a.arr for a in list(args) + list(cots)], name=name, out_shape=out_shape, grid=(rows.ncol, rows.nt),
        in_specs=[rows.spec(a.block, a.imap) for a in list(args) + list(cots)], out_specs=out_specs)
    return res if ex is None else (res, xres)


def _follows(a):
    return a.imap(1, 0, 0)[-1] != a.imap(0, 0, 0)[-1]


def _rms(x):
    return x * lax.rsqrt(jnp.mean(x * x, axis=-1, keepdims=True) + EPS)


def f_norm_mod(x, g, sh, sc):
    return ((_rms(x) * g) * (1.0 + sc) + sh,)


def f_resid_norm_mod(x, mo, ga, g, sh, sc):
    x1 = x + ga * mo
    return x1, (_rms(x1) * g) * (1.0 + sc) + sh


def f_resid(x, dn, ga):
    return (x + ga * dn,)


def f_silu(x):
    return (x * jax.nn.sigmoid(x),)


def f_bias(x, b):
    return (x + b,)


def f_ssd_gate(y0, y1, xs, z, dskip, nw):
    y = y0 + y1 + dskip * xs
    return (_rms(y * (z * jax.nn.sigmoid(z))) * nw,)


def f_pool(u, pmat, pmat_t, inv_cnt, pw, scale):
    pm = _lin01(pmat, pmat_t, u) * inv_cnt - u
    return (_mm(pm, pw) * scale,)


def f_merge(o_ssd, o_pool, gl_ssd, gl_pool):
    return (jax.nn.sigmoid(gl_ssd) * o_ssd + jax.nn.sigmoid(gl_pool) * o_pool,)


def _column_splitter(n):
    @jax.custom_vjp
    def split(x):
        w = x.shape[1] // n
        return tuple(x[:, k * w:(k + 1) * w] for k in range(n))

    def fwd(x):
        return split(x), None

    def bwd(_, g):
        return (jnp.concatenate(g, axis=1),)

    split.defvjp(fwd, bwd)
    return split


_halve_cols = _column_splitter(2)
_quarter_cols = _column_splitter(len(POOL_WINDOWS))


def f_swiglu(gu):
    a, b = _halve_cols(gu)
    return ((a * jax.nn.sigmoid(a)) * b,)


def f_pool_all(u, pmat, pmat_t, inv_cnt, scale, *pws):
    outs = [f_pool(part, pmat[k], pmat_t[k], inv_cnt[k], pws[k], 1.0)[0] for k, part in enumerate(_quarter_cols(u))]
    return (jnp.concatenate(outs, axis=1) * scale,)


def f_loss_resid(x1, dn, ga, tgt, g):
    err = _rms(x1 + ga * dn) * g - tgt
    return (0.5 * jnp.mean(err * err, axis=-1, keepdims=True),)


CONV_TILE = 128


CONV_GAP = 8


def _gapped(v, n_ctx):
    z = jnp.zeros((CONV_GAP, v.shape[1]), v.dtype)
    return jnp.concatenate([v[:n_ctx], z, v[n_ctx:], z], axis=0)


def _ungapped(v, n_ctx):
    return jnp.concatenate([v[:n_ctx], v[n_ctx + CONV_GAP:v.shape[0] - CONV_GAP]], axis=0)


def _shift_rows(v, j):
    return v if j == 0 else pltpu.roll(v, (-j) % v.shape[0], 0)


def conv_fwd(name, proj, conv_w, conv_b, n_ctx, width, ex=None):
    T = proj.shape[0]
    half = SSD_CONV // 2

    def body(u_ref, w_ref, b_ref, o_ref):
        u = _gapped(u_ref[...].astype(F32), n_ctx)
        pre = jnp.broadcast_to(b_ref[...], u.shape)
        for k in range(SSD_CONV):
            pre = pre + w_ref[k:k + 1, :] * _shift_rows(u, k - half)
        o_ref[...] = _ungapped(pre * jax.nn.sigmoid(pre), n_ctx)

    col = lambda t: (0, t)
    res, xres = hosted_call(
        body, ex, [proj, conv_w, conv_b], name=name, out_shape=[jax.ShapeDtypeStruct((T, width), F32)],
        grid=(width // CONV_TILE,),
        in_specs=[pl.BlockSpec((T, CONV_TILE), col), pl.BlockSpec((SSD_CONV, CONV_TILE), col),
                  pl.BlockSpec((1, CONV_TILE), col)],
        out_specs=[pl.BlockSpec((T, CONV_TILE), col)])
    return res[0], xres


def conv_bwd(name, proj, conv_w, conv_b, d_act2, d_skip, n_ctx, width, ex=None):
    T = proj.shape[0]
    half = SSD_CONV // 2

    def body(u_ref, w_ref, b_ref, c0_ref, c1_ref, cs_ref, du_ref, dw_ref, db_ref):
        t = pl.program_id(0)
        u = _gapped(u_ref[...].astype(F32), n_ctx)
        pre = jnp.broadcast_to(b_ref[...], u.shape)
        for k in range(SSD_CONV):
            pre = pre + w_ref[k:k + 1, :] * _shift_rows(u, k - half)
        sg = jax.nn.sigmoid(pre)
        ct = c0_ref[...].astype(F32) + c1_ref[...].astype(F32) + jnp.where(t % 4 < 2, cs_ref[...].astype(F32), 0.0)
        dpre = _gapped(ct, n_ctx) * (sg * (1.0 + pre * (1.0 - sg)))
        du = jnp.zeros_like(u)
        for k in range(SSD_CONV):
            du = du + w_ref[k:k + 1, :] * _shift_rows(dpre, half - k)
            dw_ref[k:k + 1, :] = jnp.sum(dpre * _shift_rows(u, k - half), axis=0, keepdims=True)
        du_ref[...] = _ungapped(du, n_ctx).astype(du_ref.dtype)
        db_ref[...] = jnp.sum(dpre, axis=0, keepdims=True)

    col = lambda t: (0, t)
    skip_col = lambda t: (0, (t // 4) * 2 + jnp.minimum(t % 4, 1))
    res, xres = hosted_call(
        body, ex, [proj, conv_w, conv_b, d_act2[0], d_act2[1], d_skip], name=name,
        out_shape=[jax.ShapeDtypeStruct((T, width), ACT_DTYPE), jax.ShapeDtypeStruct((SSD_CONV, width), F32),
                   jax.ShapeDtypeStruct((1, width), F32)],
        grid=(width // CONV_TILE,),
        in_specs=[pl.BlockSpec((T, CONV_TILE), col), pl.BlockSpec((SSD_CONV, CONV_TILE), col),
                  pl.BlockSpec((1, CONV_TILE), col), pl.BlockSpec((T, CONV_TILE), col),
                  pl.BlockSpec((T, CONV_TILE), col), pl.BlockSpec((T, CONV_TILE), skip_col)],
        out_specs=[pl.BlockSpec((T, CONV_TILE), col), pl.BlockSpec((SSD_CONV, CONV_TILE), col),
                   pl.BlockSpec((1, CONV_TILE), col)])
    return res[0], res[1], res[2], xres


@jax.custom_vjp
def _cumsum_mat(tri, tri_t, a):
    return jnp.dot(tri, a, precision=lax.Precision.HIGHEST, preferred_element_type=F32)


def _cumsum_fwd(tri, tri_t, a):
    return _cumsum_mat(tri, tri_t, a), (tri, tri_t)


def _cumsum_bwd(res, g):
    tri, tri_t = res
    return (jnp.zeros_like(tri), jnp.zeros_like(tri_t),
            jnp.dot(tri_t, g, precision=lax.Precision.HIGHEST, preferred_element_type=F32))


_cumsum_mat.defvjp(_cumsum_fwd, _cumsum_bwd)


def _ssd_dt(dtraw, dt_bias, a_log, tri, tri_t):
    dt_all = jax.nn.softplus(dtraw + dt_bias)
    a_all = dt_all * (-jnp.exp(a_log))
    return dt_all, a_all, _cumsum_mat(tri, tri_t, a_all)


def _ssd_chunk(xs, bm, cm, dt_all, a_all, s_all, s_in, mask, idx0):
    (xs,), (s_in,) = xs, s_in
    Q = xs.shape[0]
    hpg = xs.shape[1] // SSD_HEADDIM
    lane = lax.broadcasted_iota(jnp.int32, dt_all.shape, 1)
    head = lax.broadcasted_iota(jnp.int32, xs.shape, 1) // SSD_HEADDIM
    head1 = lax.broadcasted_iota(jnp.int32, (1, xs.shape[1]), 1) // SSD_HEADDIM

    def pick(v, r):
        return jnp.sum(jnp.where(lane == idx0 + r, v, 0.0), axis=1, keepdims=True)

    def expand(cols, hd):
        out = cols[hpg - 1]
        for r in range(hpg - 2, -1, -1):
            out = jnp.where(hd == r, cols[r], out)
        return out

    def spread(*cols):
        return expand([jnp.broadcast_to(c, xs.shape) for c in cols], head)

    dt_r = [pick(dt_all, r) for r in range(hpg)]
    s_r = [pick(s_all, r) for r in range(hpg)]
    stot_r = [jnp.sum(jnp.where(lane == idx0 + r, a_all, 0.0), keepdims=True).reshape(1, 1) for r in range(hpg)]

    xd = xs * spread(*dt_r)
    cb = _mm_nt(cm, bm)
    weights, stacked = [], []
    for r in range(hpg):
        sm = jnp.broadcast_to(s_r[r], (Q, Q))
        weights.append(cb * jnp.exp(jnp.where(mask, sm - sm.T, NEG)))
        stacked.append(jnp.where(head == r, xd, 0.0))
    y = spread(*[jnp.exp(c) for c in s_r]) * _mm(cm, s_in)
    y = y + _mm(jnp.concatenate(weights, axis=1), jnp.concatenate(stacked, axis=0))
    to_end = spread(*[jnp.exp(t - c) for t, c in zip(stot_r, s_r)])
    carry = expand([jnp.broadcast_to(jnp.exp(t), (1, xs.shape[1])) for t in stot_r], head1)
    s_out = carry * s_in + _mm_tn(bm, xd * to_end)
    return [y], [s_out]


def _scan_consts():
    q = SSD_CHUNK
    i = np.arange(q)[:, None]
    j = np.arange(q)[None, :]
    fwd = (j <= i).astype(np.float32)
    bwd = (j >= i).astype(np.float32)
    tri = np.stack([fwd, bwd])
    return jnp.asarray(tri), jnp.asarray(np.stack([fwd.T, bwd.T]))


def _chunk_of(d, k, ncc, nc):
    rev = jnp.where(k < ncc, ncc - 1 - k, nc - 1 + ncc - k)
    return jnp.where(d == 0, k, rev)


def ssd_fwd(name, xbc, dtraw, dt_bias, a_log, n_ctx, ex=None):
    T = xbc.shape[0]
    q, G = SSD_CHUNK, SSD_GROUPS
    nc, ncc = T // q, n_ctx // q
    gw = xbc.shape[1] // G
    xw = gw - 2 * SSD_STATE
    hpg = xw // SSD_HEADDIM
    nh = G * hpg
    tri, tri_t = _scan_consts()

    gs = SSD_GROUPS_PER_STEP

    def body(x0_ref, x1_ref, dt0_ref, dt1_ref, bias_ref, alog_ref, tri_ref, trit_ref, y0_ref, y1_ref, sin_ref, state):
        gb, k = pl.program_id(0), pl.program_id(1)

        @pl.when(k == 0)
        def _():
            state[...] = jnp.zeros_like(state)

        for d, (x_ref, dt_ref, y_ref) in enumerate(((x0_ref, dt0_ref, y0_ref), (x1_ref, dt1_ref, y1_ref))):
            tri_v = tri_ref[d]
            dt_all, a_all, s_all = _ssd_dt(dt_ref[...], bias_ref[...], alog_ref[...], tri_v, trit_ref[d])
            for j in range(gs):
                o = j * gw
                sin_ref[d, j] = state[d, j]
                (y,), (s_out,) = _ssd_chunk(
                    [x_ref[:, o:o + xw]], x_ref[:, o + xw:o + xw + SSD_STATE], x_ref[:, o + xw + SSD_STATE:o + gw],
                    dt_all, a_all, s_all, [state[d, j]], tri_v > 0.5, d * nh + (gb * gs + j) * hpg)
                y_ref[:, j * xw:(j + 1) * xw] = y.astype(y_ref.dtype)
                state[d, j] = s_out

    ch = lambda d, k: _chunk_of(d, k, ncc, nc)
    y_shape = jax.ShapeDtypeStruct((T, G * xw), ACT_DTYPE)
    res, xres = hosted_call(
        body, ex, [xbc, xbc, dtraw, dtraw, dt_bias, a_log, tri, tri_t], name=name,
        out_shape=[y_shape, y_shape, jax.ShapeDtypeStruct((2, nc, G, SSD_STATE, xw), F32)],
        grid=(G // gs, nc),
        in_specs=[pl.BlockSpec((q, gs * gw), lambda g, k: (ch(0, k), g)),
                  pl.BlockSpec((q, gs * gw), lambda g, k: (ch(1, k), g)),
                  pl.BlockSpec((q, 128), lambda g, k: (ch(0, k), 0)),
                  pl.BlockSpec((q, 128), lambda g, k: (ch(1, k), 0)),
                  pl.BlockSpec((1, 128), lambda g, k: (0, 0)),
                  pl.BlockSpec((1, 128), lambda g, k: (0, 0)),
                  pl.BlockSpec((2, q, q), lambda g, k: (0, 0, 0)),
                  pl.BlockSpec((2, q, q), lambda g, k: (0, 0, 0))],
        out_specs=[pl.BlockSpec((q, gs * xw), lambda g, k: (ch(0, k), g)),
                   pl.BlockSpec((q, gs * xw), lambda g, k: (ch(1, k), g)),
                   pl.BlockSpec((2, None, gs, SSD_STATE, xw), lambda g, k: (0, k, g, 0, 0))],
        scratch_shapes=[pltpu.VMEM((2, gs, SSD_STATE, xw), F32)])
    return res[0], res[1], res[2], xres


def ssd_bwd(name, xbc, dtraw, dt_bias, a_log, states, dy, n_ctx, ex=None):
    T = xbc.shape[0]
    q, G = SSD_CHUNK, SSD_GROUPS
    nc, ncc = T // q, n_ctx // q
    gw = xbc.shape[1] // G
    xw = gw - 2 * SSD_STATE
    hpg = xw // SSD_HEADDIM
    nh = G * hpg
    tri, tri_t = _scan_consts()

    gs = SSD_GROUPS_PER_STEP

    def body(x0_ref, x1_ref, dt0_ref, dt1_ref, bias_ref, alog_ref, tri_ref, trit_ref, sin_ref, dy0_ref, dy1_ref,
             dx0_ref, dx1_ref, ddt_ref, dbias_ref, dalog_ref, dstate):
        gb, k = pl.program_id(0), pl.program_id(1)

        @pl.when((gb == 0) & (k == 0))
        def _():
            ddt_ref[...] = jnp.zeros_like(ddt_ref)
            dbias_ref[...] = jnp.zeros_like(dbias_ref)
            dalog_ref[...] = jnp.zeros_like(dalog_ref)

        @pl.when(k == 0)
        def _():
            dstate[...] = jnp.zeros_like(dstate)

        tris = [(tri_ref[d], trit_ref[d]) for d in range(2)]
        per = 4

        def fn(bias, alog, dtraw0, dtraw1, *per_group):
            ys, s_outs = [], []
            for d, dtraw in enumerate((dtraw0, dtraw1)):
                tri_v, trit_v = tris[d]
                dt_all, a_all, s_all = _ssd_dt(dtraw, bias, alog, tri_v, trit_v)
                for j in range(gs):
                    xs, bm, cm, s_in = per_group[per * (d * gs + j):per * (d * gs + j + 1)]
                    y, s_out = _ssd_chunk([xs], bm, cm, dt_all, a_all, s_all, [s_in], tri_v > 0.5,
                                          d * nh + (gb * gs + j) * hpg)
                    ys += y
                    s_outs += s_out
            return ys, s_outs

        per_group, dys, dss = [], [], []
        for d, (x_ref, dy_ref) in enumerate(((x0_ref, dy0_ref), (x1_ref, dy1_ref))):
            for j in range(gs):
                o = j * gw
                per_group += [x_ref[:, o:o + xw], x_ref[:, o + xw:o + xw + SSD_STATE], x_ref[:, o + xw + SSD_STATE:o + gw],
                              sin_ref[d, j]]
                dys.append(dy_ref[:, j * xw:(j + 1) * xw].astype(F32))
                dss.append(dstate[d, j])
        _, vjp = jax.vjp(fn, bias_ref[...], alog_ref[...], dt0_ref[...], dt1_ref[...], *per_group)
        cts = vjp((dys, dss))
        dbias, dalog, ddt0, ddt1 = cts[:4]
        for d, dx_ref in enumerate((dx0_ref, dx1_ref)):
            for j in range(gs):
                o = j * gw
                dxs, dbm, dcm, ds_in = cts[4 + per * (d * gs + j):4 + per * (d * gs + j + 1)]
                dx_ref[:, o:o + xw] = dxs.astype(dx_ref.dtype)
                dx_ref[:, o + xw:o + xw + SSD_STATE] = dbm.astype(dx_ref.dtype)
                dx_ref[:, o + xw + SSD_STATE:o + gw] = dcm.astype(dx_ref.dtype)
                dstate[d, j] = ds_in
        for d, ddt in enumerate((ddt0, ddt1)):
            row0 = pl.multiple_of(_chunk_of(d, nc - 1 - k, ncc, nc) * q, q)
            ddt_ref[pl.ds(row0, q), :] += ddt
        dbias_ref[...] += dbias
        dalog_ref[...] += dalog

    ch = lambda d, k: _chunk_of(d, nc - 1 - k, ncc, nc)
    dx_shape = jax.ShapeDtypeStruct((T, G * gw), ACT_DTYPE)
    res, xres = hosted_call(
        body, ex, [xbc, xbc, dtraw, dtraw, dt_bias, a_log, tri, tri_t, states, dy, dy], name=name,
        out_shape=[dx_shape, dx_shape, jax.ShapeDtypeStruct((T, 128), F32),
                   jax.ShapeDtypeStruct((1, 128), F32), jax.ShapeDtypeStruct((1, 128), F32)],
        grid=(G // gs, nc),
        in_specs=[pl.BlockSpec((q, gs * gw), lambda g, k: (ch(0, k), g)),
                  pl.BlockSpec((q, gs * gw), lambda g, k: (ch(1, k), g)),
                  pl.BlockSpec((q, 128), lambda g, k: (ch(0, k), 0)),
                  pl.BlockSpec((q, 128), lambda g, k: (ch(1, k), 0)),
                  pl.BlockSpec((1, 128), lambda g, k: (0, 0)),
                  pl.BlockSpec((1, 128), lambda g, k: (0, 0)),
                  pl.BlockSpec((2, q, q), lambda g, k: (0, 0, 0)),
                  pl.BlockSpec((2, q, q), lambda g, k: (0, 0, 0)),
                  pl.BlockSpec((2, None, gs, SSD_STATE, xw), lambda g, k: (0, nc - 1 - k, g, 0, 0)),
                  pl.BlockSpec((q, gs * xw), lambda g, k: (ch(0, k), g)),
                  pl.BlockSpec((q, gs * xw), lambda g, k: (ch(1, k), g))],
        out_specs=[pl.BlockSpec((q, gs * gw), lambda g, k: (ch(0, k), g)),
                   pl.BlockSpec((q, gs * gw), lambda g, k: (ch(1, k), g)),
                   pl.BlockSpec((T, 128), lambda g, k: (0, 0)),
                   pl.BlockSpec((1, 128), lambda g, k: (0, 0)),
                   pl.BlockSpec((1, 128), lambda g, k: (0, 0))],
        scratch_shapes=[pltpu.VMEM((2, gs, SSD_STATE, xw), F32)])
    return res[0], res[1], res[2], res[3], res[4], xres


def _perm_xbc(a):
    G = SSD_GROUPS
    n = a.shape[-1]
    gn = G * SSD_STATE
    di = n - 2 * gn
    lead = a.shape[:-1]
    xs = a[..., :di].reshape(lead + (G, di // G))
    bm = a[..., di:di + gn].reshape(lead + (G, SSD_STATE))
    cm = a[..., di + gn:].reshape(lead + (G, SSD_STATE))
    return jnp.concatenate([xs, bm, cm], axis=-1).reshape(lead + (n,))


def _unperm_xbc(a):
    G = SSD_GROUPS
    n = a.shape[-1]
    gn = G * SSD_STATE
    di = n - 2 * gn
    lead = a.shape[:-1]
    r = a.reshape(lead + (G, n // G))
    xw = di // G
    return jnp.concatenate([r[..., :xw].reshape(lead + (di,)), r[..., xw:xw + SSD_STATE].reshape(lead + (gn,)),
                            r[..., xw + SSD_STATE:].reshape(lead + (gn,))], axis=-1)


def _pool_consts(tm, n_ctx):
    assert n_ctx == tm and tm % GRID_W == 0
    mats, cnts = [], []
    for seq in (n_ctx, GRID_W):
        t = np.arange(tm)
        tt = t % seq
        base = t - tt
        ms, cs = [], []
        for k in POOL_WINDOWS:
            lo = np.clip(tt - k // 2, 0, seq) + base
            hi = np.clip(tt + k // 2, 0, seq) + base
            m = ((t[None, :] >= lo[:, None]) & (t[None, :] < hi[:, None])).astype(np.float32)
            ms.append(m)
            cs.append((1.0 / (hi - lo).astype(np.float32))[:, None])
        mats.append(np.stack(ms))
        cnts.append(np.stack(cs))
    m = np.stack(mats)
    return jnp.asarray(m), jnp.asarray(np.swapaxes(m, -1, -2)), jnp.asarray(np.stack(cnts).astype(np.float32))


def _prep_layer_weights(w_ada, b_ada, g_mix, w_in, conv_w, conv_b, dt_bias, a_log, d_skip, ssd_norm_w, w_ssd_out,
                        pool_w, pool_scale, w_pool_out, w_out, g_ffn, w_gate_up, w_down):
    D = w_in.shape[0]
    di = ssd_norm_w.shape[0]
    xbc = conv_w.shape[1]
    nh2 = dt_bias.size
    pw = pool_scale.shape[0]
    o = 0
    wz = w_in[:, o:o + di]; o += di
    wx = w_in[:, o:o + xbc]; o += xbc
    wdt = w_in[:, o:o + nh2]; o += nh2
    wp = w_in[:, o:o + pw]; o += pw
    wg = w_in[:, o:]
    w1 = jnp.concatenate([_perm_xbc(wx), wz, wg, wp, wdt, jnp.zeros((D, DT_PAD - nh2), w_in.dtype)], axis=1)
    pad128 = lambda v: jnp.concatenate([v.reshape(1, -1), jnp.zeros((1, 128 - v.size), F32)], axis=1)
    return dict(
        w_ada=w_ada, b_ada=b_ada.reshape(1, -1), g_mix=g_mix.reshape(1, -1), w1=w1,
        conv_w=_perm_xbc(conv_w), conv_b=_perm_xbc(conv_b.reshape(1, -1)),
        dt_bias=pad128(dt_bias), a_log=pad128(a_log),
        dskip=jnp.repeat(d_skip[0] + d_skip[1], SSD_HEADDIM).reshape(1, -1),
        ssd_norm_w=ssd_norm_w.reshape(1, -1), w_ssd_out=w_ssd_out, pool_w=pool_w,
        pool_scale=pool_scale.reshape(1, -1), w_pool_out=w_pool_out, w_out=w_out, g_ffn=g_ffn.reshape(1, -1),
        w_gate_up=w_gate_up, w_down=w_down)


def _unprep_layer_grads(g, dims):
    di, xbc, nh2, pw = dims
    dxbc, dz, dgs, dgp, dp, ddt = g["w1"]
    r = dxbc.reshape(SSD_GROUPS, xbc // SSD_GROUPS, dxbc.shape[1])
    xw = di // SSD_GROUPS
    parts = [r[:, :xw], r[:, xw:xw + SSD_STATE], r[:, xw + SSD_STATE:]]
    w_in_t = jnp.concatenate([dz] + [p.reshape(-1, dxbc.shape[1]) for p in parts] + [ddt[:nh2], dp, dgs, dgp], axis=0)
    nh = nh2 // 2
    dsk = g["dskip"].reshape(nh, SSD_HEADDIM).sum(axis=1)
    return dict(
        w_ada=g["w_ada"], b_ada=g["b_ada"].reshape(-1), g_mix=g["g_mix"].reshape(-1),
        w_in=w_in_t,
        conv_w=_unperm_xbc(g["conv_w"]), conv_b=_unperm_xbc(g["conv_b"]).reshape(-1),
        dt_bias=g["dt_bias"][0, :nh2].reshape(2, nh), a_log=g["a_log"][0, :nh2].reshape(2, nh),
        d_skip=jnp.stack([dsk, dsk]), ssd_norm_w=g["ssd_norm_w"].reshape(-1), w_ssd_out=g["w_ssd_out"],
        pool_w=g["pool_w"], pool_scale=g["pool_scale"].reshape(-1), w_pool_out=g["w_pool_out"], w_out=g["w_out"],
        g_ffn=g["g_ffn"].reshape(-1), w_gate_up=g["w_gate_up"], w_down=g["w_down"])


COND_ROWS = 16


def _split_mods(m):
    d = m.shape[1] // 6
    return [m[:2, k * d:(k + 1) * d].reshape(2, 1, d) for k in range(6)]


def _pool_args(rows, proj, col_block, width, pc, w):
    seg_const = lambda a: Arg(a, (None,) + a.shape[1:], lambda j, i, s: (s, 0, 0, 0), "const")
    pws = [Arg(w["pool_w"][k], w["pool_w"].shape[1:], lambda j, i, s: (0, 0), "acc") for k in range(w["pool_w"].shape[0])]
    return [rows.row(proj, width, col_block)] + [seg_const(a) for a in pc] + [rows.vec(w["pool_scale"])] + pws


TALL_ROW_TILE = 2176


def _tall_rows(T, ncol):
    tm = max(t for t in range(16, min(T, TALL_ROW_TILE) + 1, 16) if T % t == 0)
    return Rows(T // tm, 0, tm, ncol)


def _hosted(hosts, box, key):
    fn = (hosts or {}).get(key)
    return fn(box) if fn else None


def _layer_fwd(l, pre, cond_s, w, rows, n_ctx, pc, hosts=None, box=None):
    T, D = pre[0].shape if isinstance(pre, tuple) else pre.shape
    nt, nct, tm = rows.nt, rows.nct, rows.tm
    n = lambda s: f"l{l}_{s}"
    crow = Rows(1, 0, COND_ROWS)
    mraw = matmul_nn(n("ada_mm"), cond_s, w["w_ada"])
    (m,) = stage_fwd(n("ada_bias"), f_bias, crow, [crow.row(mraw, mraw.shape[1]), crow.vec(w["b_ada"])],
                     [(mraw.shape[1], F32, False)])
    sh1, sc1, ga1, sh2, sc2, ga2 = _split_mods(m)

    if isinstance(pre, tuple):
        x, h1 = stage_fwd(n("norm1"), f_resid_norm_mod, rows, _resid_norm_args(rows, pre, w["g_mix"], sh1, sc1, D),
                          [(D, F32, False), (D, ACT_DTYPE, False)])
    else:
        x = pre
        (h1,) = stage_fwd(n("norm1"), f_norm_mod, rows,
                          [rows.row(x, D), rows.vec(w["g_mix"]), rows.segvec(sh1), rows.segvec(sc1)],
                          [(D, ACT_DTYPE, False)])
    xbc_w = w["conv_w"].shape[1]
    di = w["ssd_norm_w"].shape[1]
    pw = w["pool_scale"].shape[1]
    c_z, c_g, c_p, c_dt = xbc_w, xbc_w + di, xbc_w + di + 2 * pw, xbc_w + di + 3 * pw
    ex = _hosted(hosts, box, "in_mm")
    proj = matmul_nn(n("in_mm"), h1, w["w1"], out_dtype=ACT_DTYPE, ex=ex, ncols=c_dt)
    if ex is not None:
        proj, box["in_mm"] = proj
    dtraw = matmul_nn(n("in_dt_mm"), h1, w["w1"], col0=c_dt, ncols=128)
    ex = _hosted(hosts, box, "conv")
    xbc, xres = conv_fwd(n("conv"), proj, w["conv_w"], w["conv_b"], n_ctx, xbc_w, ex)
    if ex is not None:
        box["conv"] = xres
    ex = _hosted(hosts, box, "ssd")
    y0, y1, states, xres = ssd_fwd(n("ssd"), xbc, dtraw, w["dt_bias"], w["a_log"], n_ctx, ex)
    y2 = (y0, y1)
    if ex is not None:
        box["ssd"] = xres

    G = SSD_GROUPS
    gw = di // G
    r8 = _tall_rows(T, G)
    gate_args = [r8.row(y2[0], gw, 0, True), r8.row(y2[1], gw, 0, True), r8.row(xbc, gw, 0, True, stride=2),
                 r8.row(proj, gw, c_z // gw, True), r8.vec(w["dskip"], True), r8.vec(w["ssd_norm_w"], True)]
    (ynw,) = stage_fwd(n("ssd_gate"), f_ssd_gate, r8, gate_args, [(gw, ACT_DTYPE, True)])
    ex = _hosted(hosts, box, "ssd_out_mm")
    o_ssd = matmul_nn(n("ssd_out_mm"), ynw, w["w_ssd_out"], ex=ex)
    if ex is not None:
        o_ssd, box["ssd_out_mm"] = o_ssd

    nw = len(POOL_WINDOWS)
    pg = pw // nw
    (ps,) = stage_fwd(n("pool"), f_pool_all, rows, _pool_args(rows, proj, c_p // pw, pw, pc, w), [(pw, ACT_DTYPE, False)])
    o_pool = matmul_nn(n("pool_out_mm"), ps, w["w_pool_out"])

    merge_args = [rows.row(o_ssd, D), rows.row(o_pool, D), rows.row(proj, pw, c_g // pw), rows.row(proj, pw, c_g // pw + 1)]
    (mg,) = stage_fwd(n("merge"), f_merge, rows, merge_args, [(D, ACT_DTYPE, False)])
    mo = matmul_nn(n("out_mm"), mg, w["w_out"])

    rn_args = [rows.row(x, D), rows.row(mo, D), rows.segvec(ga1), rows.vec(w["g_ffn"]), rows.segvec(sh2), rows.segvec(sc2)]
    x1, h2 = stage_fwd(n("norm2"), f_resid_norm_mod, rows, rn_args, [(D, F32, False), (D, ACT_DTYPE, False)])
    ex = _hosted(hosts, box, "gate_up_mm")
    gu = matmul_nn(n("gate_up_mm"), h2, w["w_gate_up"], ex=ex)
    if ex is not None:
        gu, box["gate_up_mm"] = gu
    fh = gu.shape[1] // 2
    (act,) = stage_fwd(n("swiglu"), f_swiglu, rows, [rows.row(gu, 2 * fh)], [(fh, ACT_DTYPE, False)])
    ex = _hosted(hosts, box, "down_mm")
    dn = matmul_nn(n("down_mm"), act, w["w_down"], ex=ex)
    if ex is not None:
        dn, box["down_mm"] = dn
    saved = dict(x=x, pre=pre, mraw=mraw, mods=(sh1, sc1, ga1, sh2, sc2, ga2), h1=h1, proj=proj, dtraw=dtraw, xbc=xbc, y2=y2,
                 states=states,
                 ynw=ynw, o_ssd=o_ssd, ps=ps, o_pool=o_pool, mg=mg, mo=mo, x1=x1, h2=h2, gu=gu, act=act, dn=dn,
                 cols=(c_z, c_g, c_p, c_dt))
    return (x1, dn, ga2), saved


def _resid_norm_args(rows, pre, g, sh, sc, D):
    x1, dn, ga2 = pre
    return [rows.row(x1, D), rows.row(dn, D), rows.segvec(ga2), rows.vec(g), rows.segvec(sh), rows.segvec(sc)]


def f_norm_mod_keep(x, g, sh, sc):
    return f_norm_mod(x, g, sh, sc)[0], x


def _layer_bwd(l, cot, cond_s, w, s, rows, n_ctx, pc, hosts=None, box=None):
    dx1, ddn, dga2 = cot
    T, D = dx1.shape
    nt, nct, tm = rows.nt, rows.nct, rows.tm
    n = lambda t: f"l{l}_{t}_bwd"
    sh1, sc1, ga1, sh2, sc2, ga2 = s["mods"]
    c_z, c_g, c_p, c_dt = s["cols"]
    x, proj, xbc, y2, gu = s["x"], s["proj"], s["xbc"], s["y2"], s["gu"]
    g = {}
    if box is not None:
        box["g"] = g

    ex = _hosted(hosts, box, "down_dx")
    dact = matmul_nt(n("down_dx"), ddn, w["w_down"], ex=ex)
    if ex is not None:
        dact, box["down_dx"] = dact
    ex = _hosted(hosts, box, "down_dw")
    g["w_down"] = matmul_tn(n("down_dw"), s["act"], ddn, ex=ex)
    if ex is not None:
        g["w_down"], box["down_dw"] = g["w_down"]
    fh = gu.shape[1] // 2
    (dgu,) = stage_bwd(n("swiglu"), f_swiglu, rows, [rows.row(gu, 2 * fh)], [rows.row(dact, fh)], [ACT_DTYPE])
    dh2 = matmul_nt(n("gate_up_dx"), dgu, w["w_gate_up"])
    g["w_gate_up"] = matmul_tn(n("gate_up_dw"), s["h2"], dgu, blocks=w["w_gate_up"].shape[0])

    rn_args = [rows.row(x, D), rows.row(s["mo"], D), rows.segvec(ga1), rows.vec(w["g_ffn"]), rows.segvec(sh2), rows.segvec(sc2)]
    dxr, dmo, dga1, g["g_ffn"], dsh2, dsc2 = stage_bwd(
        n("norm2"), f_resid_norm_mod, rows, rn_args, [rows.row(dx1, D), rows.row(dh2, D)], [F32, ACT_DTYPE])
    dmg = matmul_nt(n("out_dx"), dmo, w["w_out"])
    g["w_out"] = matmul_tn(n("out_dw"), s["mg"], dmo)

    pw = w["pool_scale"].shape[1]
    merge_args = [rows.row(s["o_ssd"], D), rows.row(s["o_pool"], D), rows.row(proj, pw, c_g // pw), rows.row(proj, pw, c_g // pw + 1)]
    do_ssd, do_pool, dgl_s, dgl_p = stage_bwd(n("merge"), f_merge, rows, merge_args, [rows.row(dmg, D)], [ACT_DTYPE] * 4)
    dps = matmul_nt(n("pool_out_dx"), do_pool, w["w_pool_out"])
    g["w_pool_out"] = matmul_tn(n("pool_out_dw"), s["ps"], do_pool)

    nw = len(POOL_WINDOWS)
    pg = pw // nw
    du_pool, g["pool_scale"], *dpw = stage_bwd(n("pool"), f_pool_all, rows, _pool_args(rows, proj, c_p // pw, pw, pc, w),
                                               [rows.row(dps, pw)], [ACT_DTYPE])
    g["pool_w"] = jnp.stack(dpw)

    dynw = matmul_nt(n("ssd_out_dx"), do_ssd, w["w_ssd_out"])
    g["w_ssd_out"] = matmul_tn(n("ssd_out_dw"), s["ynw"], do_ssd)
    G = SSD_GROUPS
    di = w["ssd_norm_w"].shape[1]
    gw = di // G
    r8 = _tall_rows(T, G)
    gate_args = [r8.row(y2[0], gw, 0, True), r8.row(y2[1], gw, 0, True), r8.row(xbc, gw, 0, True, stride=2),
                 r8.row(proj, gw, c_z // gw, True), r8.vec(w["dskip"], True), r8.vec(w["ssd_norm_w"], True)]
    gate_args[1].kind = "const"
    ex = _hosted(hosts, box, "ssd_gate")
    res = stage_bwd(n("ssd_gate"), f_ssd_gate, r8, gate_args, [r8.row(dynw, gw, 0, True)], [ACT_DTYPE] * 3, ex)
    if ex is not None:
        res, box["ssd_gate"] = res
    dy, dxs_skip, dz, g["dskip"], g["ssd_norm_w"] = res

    ex = _hosted(hosts, box, "ssd")
    dxbc0, dxbc1, ddt, g["dt_bias"], g["a_log"], xres = ssd_bwd(n("ssd"), xbc, s["dtraw"], w["dt_bias"], w["a_log"],
                                                                s["states"], dy, n_ctx, ex)
    dxbc2 = (dxbc0, dxbc1)
    if ex is not None:
        box["ssd"] = xres
    xbc_w = xbc.shape[1]
    ex = _hosted(hosts, box, "conv")
    dxbc_raw, g["conv_w"], g["conv_b"], xres = conv_bwd(n("conv"), proj, w["conv_w"], w["conv_b"], dxbc2, dxs_skip,
                                                         n_ctx, xbc_w, ex)
    if ex is not None:
        box["conv"] = xres
    pieces = [dxbc_raw, dz, dgl_s, dgl_p, du_pool, ddt]
    offsets = [0, c_z, c_g, c_g + pw, c_p, c_dt]
    ex = _hosted(hosts, box, "in_dx")
    dh1 = matmul_nt(n("in_dx"), pieces, w["w1"], ex=ex, offsets=offsets)
    if ex is not None:
        dh1, box["in_dx"] = dh1
    ex = _hosted(hosts, box, "in_dw")
    first = matmul_tn(n("in_dw0"), pieces[0], s["h1"], ex=ex)
    if ex is not None:
        first, box["in_dw"] = first
    g["w1"] = [first] + [matmul_tn(n(f"in_dw{k}"), p, s["h1"]) for k, p in enumerate(pieces) if k]

    if isinstance(s["pre"], tuple):
        dx1p, ddnp, dga2p, g["g_mix"], dsh1, dsc1 = stage_bwd(
            n("norm1"), f_resid_norm_mod, rows, _resid_norm_args(rows, s["pre"], w["g_mix"], sh1, sc1, D),
            [rows.row(dxr, D), rows.row(dh1, D)], [F32, ACT_DTYPE])
        dx = (dx1p, ddnp, dga2p)
    else:
        n1_args = [rows.row(x, D), rows.vec(w["g_mix"]), rows.segvec(sh1), rows.segvec(sc1)]
        dx, g["g_mix"], dsh1, dsc1 = stage_bwd(n("norm1"), f_norm_mod_keep, rows, n1_args,
                                               [rows.row(dh1, D), rows.row(dxr, D)], [F32])

    dm = jnp.concatenate([v.reshape(2, D) for v in (dsh1, dsc1, dga1, dsh2, dsc2, dga2)], axis=1)
    dm = jnp.concatenate([dm, jnp.zeros((COND_ROWS - 2, dm.shape[1]), F32)], axis=0)
    crow = Rows(1, 0, COND_ROWS)
    dmraw, g["b_ada"] = stage_bwd(n("ada_bias"), f_bias, crow, [crow.row(s["mraw"], dm.shape[1]), crow.vec(w["b_ada"])],
                                  [crow.row(dm, dm.shape[1])], [ACT_DTYPE])
    dcs = matmul_nt(n("ada_dx"), dmraw, w["w_ada"])
    g["w_ada"] = matmul_tn(n("ada_dw"), cond_s, dmraw, blocks=w["w_ada"].shape[0])
    return dx, dcs, g


def local_step(x, ctx, c, c_ctx, target, layer_w_fn, n_layers, g_final, fwd_hosts=None, bwd_hosts=None):
    L, D = x.shape
    n_ctx = ctx.shape[0]
    tm = ROW_TILE
    T = L + n_ctx
    rows = Rows(T // tm, n_ctx // tm, tm)
    pc = _pool_consts(tm, n_ctx)
    xa = jnp.concatenate([ctx, x], axis=0)
    cond = jnp.concatenate([c_ctx.reshape(1, D), c.reshape(1, D), jnp.zeros((COND_ROWS - 2, D), F32)], axis=0)
    crow = Rows(1, 0, COND_ROWS)
    (cond_s,) = stage_fwd("cond_silu", f_silu, crow, [crow.row(cond, D)], [(D, ACT_DTYPE, False)])

    saved, layer_w = [], []
    for l in range(n_layers):
        layer_w.append(layer_w_fn(l))
        box = {}
        xa, s = _layer_fwd(l, xa, cond_s, layer_w[l], rows, n_ctx, pc, fwd_hosts(l, box) if fwd_hosts else None, box)
        saved.append(s)

    x1, dn, ga2 = xa
    rl = Rows(L // tm, 0, tm)
    gf = g_final.reshape(1, D)
    tgt = rl.row(target, D)
    tgt.kind = "const"
    off = n_ctx // tm
    loss_args = [rl.row(x1, D, roff=off), rl.row(dn, D, roff=off), rl.vec(ga2[1]), tgt, rl.vec(gf)]
    ones = jnp.ones((L, 1), F32)
    dx1_lat, ddn_lat, dga2_lat, dgf, loss_rows = stage_bwd("loss", f_loss_resid, rl, loss_args, [rl.row(ones, 1)],
                                                           [F32, ACT_DTYPE], primal=[(1, F32)])
    loss = jnp.sum(loss_rows)
    cot = (jnp.concatenate([jnp.zeros((n_ctx, D), F32), dx1_lat], axis=0),
           jnp.concatenate([jnp.zeros((n_ctx, D), ACT_DTYPE), ddn_lat], axis=0),
           jnp.stack([jnp.zeros((1, D), F32), dga2_lat]))

    grads = [None] * n_layers
    dcs = jnp.zeros((COND_ROWS, D), F32)
    for l in reversed(range(n_layers)):
        box = {}
        hosts = bwd_hosts(l, grads, box) if bwd_hosts else None
        cot, dcs_l, grads[l] = _layer_bwd(l, cot, cond_s, layer_w[l], saved[l], rows, n_ctx, pc, hosts, box)
        dcs = dcs + dcs_l
    dx = cot
    (dcond,) = stage_bwd("cond_silu_bwd", f_silu, crow, [crow.row(cond, D)], [crow.row(dcs, D)], [F32])
    return loss, dx[n_ctx:], grads, dcond[0], dgf


def gather_chips(halves, conv=None):
    n = len(halves)
    ops = list(halves) + ([conv] if conv is not None else [])

    def copies(ins, outs, pos):
        c, me = pos[2], _chip_index(pos)
        pairs = [(s.at[c], o.at[me, c]) for s, o in zip(ins[:n], outs[:n])]
        pairs += [(s, o.at[me]) for s, o in zip(ins[n:], outs[n:])]
        return pairs, [(s, d, _flip(pos, rel)) for rel in PLANE for s, d in pairs]

    shapes = [jax.ShapeDtypeStruct((4,) + s.shape, s.dtype) for s in ops]
    return Exchange(copies, 3 * len(ops), len(ops), ops, shapes)


def gather_pair(gathered):
    n = len(gathered)

    def copies(ins, outs, pos):
        c = pos[2]
        return [], [(s.at[b, c], o.at[b, c], _flip(pos, PAIR[0])) for s, o in zip(ins, outs) for b in range(4)]

    shapes = [jax.ShapeDtypeStruct(g.shape, g.dtype) for g in gathered]
    return Exchange(copies, 4 * n, 0, gathered, shapes, aliases={k: k for k in range(n)})


def swap_halves(grads):
    n = len(grads)

    def copies(ins, outs, pos):
        c = pos[2]
        return [], [(g.at[b, 1 - c], o.at[b], _flip(pos, PAIR[0])) for g, o in zip(ins, outs) for b in range(4)]

    shapes = [jax.ShapeDtypeStruct((g.shape[0],) + g.shape[2:], g.dtype) for g in grads]
    return Exchange(copies, 4 * n, 0, grads, shapes)


def scatter_chips(sums):
    n = len(sums)

    def copies(ins, outs, pos):
        me = _chip_index(pos)
        local = [(p.at[me], o.at[me]) for p, o in zip(ins, outs)]
        remote = []
        for rel in PLANE:
            peer = _flip(pos, rel)
            remote += [(p.at[_chip_index(peer)], o.at[me], peer) for p, o in zip(ins, outs)]
        return local, remote

    shapes = [jax.ShapeDtypeStruct(p.shape, p.dtype) for p in sums]
    return Exchange(copies, 3 * n, n, sums, shapes)


def share_halves(finals):
    n = len(finals)

    def copies(ins, outs, pos):
        c = pos[2]
        return [], [(f.at[c], o.at[c], _flip(pos, PAIR[0])) for f, o in zip(ins, outs)]

    shapes = [jax.ShapeDtypeStruct(f.shape, f.dtype) for f in finals]
    return Exchange(copies, n, 0, finals, shapes, aliases={k: k for k in range(n)})


def gather_everyone(vec):
    def copies(ins, outs, pos):
        me = _device_index(pos)
        (v,), (o,) = ins, outs
        return [(v, o.at[me])], [(v, o.at[me], _flip(pos, rel)) for rel in EVERYONE]

    return Exchange(copies, len(EVERYONE), 1, [vec], [jax.ShapeDtypeStruct((8,) + vec.shape, vec.dtype)])


def _row_tile(rows, cols, n_bufs, mult=8):
    cap = VMEM_LIMIT_BYTES // 2 // (2 * n_bufs * cols * 4)
    for t in range(min(rows, cap) // mult * mult, 0, -mult):
        if rows % t == 0:
            return t
    return rows


def _adamw_update(w, g, m, v):
    nm = ADAM_B1 * m + (1.0 - ADAM_B1) * g
    nv = ADAM_B2 * v + (1.0 - ADAM_B2) * jnp.square(g)
    m_hat = nm / (1.0 - ADAM_B1 ** ADAM_STEP)
    v_hat = nv / (1.0 - ADAM_B2 ** ADAM_STEP)
    return -ADAM_LR * (m_hat / (jnp.sqrt(v_hat) + ADAM_EPS) + ADAM_WD * w), nm, nv


def adamw_small(name, ws, gs, ms, vs):
    n = len(ws)

    def body(*refs):
        ins, outs = refs[:4 * n], refs[4 * n:]
        for k in range(n):
            d, nm, nv = _adamw_update(ins[k][...], ins[n + k][...], ins[2 * n + k][...], ins[3 * n + k][...])
            outs[k][...] = d
            outs[n + k][...] = nm
            outs[2 * n + k][...] = nv

    shapes = [jax.ShapeDtypeStruct(a.shape, F32) for a in ws]
    vmem = pl.BlockSpec(memory_space=pltpu.VMEM)
    res = _pcall(body, name=name, out_shape=shapes * 3, in_specs=[vmem] * (4 * n), out_specs=[vmem] * (3 * n),
                 compiler_params=pltpu.CompilerParams(vmem_limit_bytes=VMEM_LIMIT_BYTES))(*ws, *gs, *ms, *vs)
    return res[:n], res[n:2 * n], res[2 * n:]


WIRE_DTYPE = jnp.bfloat16


def add_own_half(name, grads, recv, c):
    nb, _, R, C = grads.shape
    tr = _row_tile(R, C, 3, mult=16)

    def body(c_ref, g_ref, r_ref, o_ref):
        o_ref[...] = (g_ref[...] + r_ref[...]).astype(o_ref.dtype)

    spec = pl.BlockSpec((None, tr, C), lambda b, i, c_ref: (b, i, 0))
    return _pcall(
        body, name=name, out_shape=jax.ShapeDtypeStruct(recv.shape, WIRE_DTYPE),
        grid_spec=pltpu.PrefetchScalarGridSpec(
            num_scalar_prefetch=1, grid=(nb, R // tr),
            in_specs=[pl.BlockSpec((None, None, tr, C), lambda b, i, c_ref: (b, c_ref[0], i, 0)), spec],
            out_specs=spec),
        compiler_params=_params("parallel", "parallel"),
    )(c, grads, recv)


def sum_slots(name, a, c=None):
    n, R, C = a.shape
    tr = _row_tile(R, C, n + 1, mult=16 if a.dtype.itemsize == 2 else 8)

    def body(*refs):
        a_ref, o_ref = refs[-2:]
        acc = a_ref[0].astype(F32)
        for k in range(1, n):
            acc = acc + a_ref[k].astype(F32)
        o_ref[...] = acc

    if c is None:
        return _pcall(
            body, name=name, out_shape=jax.ShapeDtypeStruct((R, C), F32), grid=(R // tr,),
            in_specs=[pl.BlockSpec((n, tr, C), lambda i: (0, i, 0))], out_specs=pl.BlockSpec((tr, C), lambda i: (i, 0)),
            compiler_params=_params("parallel"),
        )(a)
    return _pcall(
        body, name=name, out_shape=jax.ShapeDtypeStruct((2, R, C), F32),
        grid_spec=pltpu.PrefetchScalarGridSpec(
            num_scalar_prefetch=1, grid=(R // tr,),
            in_specs=[pl.BlockSpec((n, tr, C), lambda i, c_ref: (0, i, 0))],
            out_specs=pl.BlockSpec((None, tr, C), lambda i, c_ref: (c_ref[0], i, 0))),
        compiler_params=_params("parallel"),
    )(c, a)


def adamw(name, w, g_layers, m, v):
    nl, R, C = w.shape
    assert len(g_layers) == nl
    tr = _row_tile(R, C, 8 + nl)
    nr = R // tr

    def body(*refs):
        w_ref, m_ref, v_ref = refs[:3]
        g_refs = refs[3:3 + nl]
        go_ref, d_ref, nm_ref, nv_ref = refs[3 + nl:]
        l = pl.program_id(0)
        gr = g_refs[0][...]
        for k in range(1, nl):
            gr = jnp.where(l == k, g_refs[k][...], gr)
        d_ref[...], nm_ref[...], nv_ref[...] = _adamw_update(w_ref[...], gr, m_ref[...], v_ref[...])
        go_ref[...] = gr

    spec = pl.BlockSpec((None, tr, C), lambda l, i: (l, i, 0))
    g_specs = [pl.BlockSpec((tr, C), (lambda l, i, k=k: (jnp.where(l == k, i, jnp.where(l < k, 0, nr - 1)), 0)))
               for k in range(nl)]
    return _pcall(
        body, name=name, out_shape=[jax.ShapeDtypeStruct((nl, R, C), F32)] * 4, grid=(nl, nr),
        in_specs=[spec] * 3 + g_specs, out_specs=[spec] * 4, compiler_params=_params("arbitrary", "arbitrary"),
    )(w, m, v, *g_layers)


BIG = ("w_ada", "w_in", "w_ssd_out", "pool_w", "w_pool_out", "w_out", "w_gate_up", "w_down")
COL_SHARDED = ("w_ada", "w_in", "w_gate_up")
BLOCK_LAYOUT = ("w_ada", "w_gate_up")
GRAD_TRANSPOSED = ("w_in",)
FIRST_USED = ("w_ada", "w_in")
MID_USED = ("w_ssd_out", "pool_w", "w_pool_out", "w_out")
END_USED = ("w_gate_up", "w_down")
LATER_USED = MID_USED + END_USED
assert FIRST_USED + LATER_USED == BIG
READY_LAST = FIRST_USED
READY_EARLY = LATER_USED
SMALL = ("c_ctx", "b_ada", "g_mix", "conv_w", "conv_b", "dt_bias", "a_log", "d_skip", "ssd_norm_w", "pool_scale",
         "g_ffn", "g_final")
WEIGHTS = ("c_ctx", "w_ada", "b_ada", "g_mix", "w_in", "conv_w", "conv_b", "dt_bias", "a_log", "d_skip", "ssd_norm_w",
           "w_ssd_out", "pool_w", "pool_scale", "w_pool_out", "w_out", "g_ffn", "w_gate_up", "w_down", "g_final")
LAYER_KEYS = ("w_ada", "b_ada", "g_mix", "w_in", "conv_w", "conv_b", "dt_bias", "a_log", "d_skip", "ssd_norm_w",
              "w_ssd_out", "pool_w", "pool_scale", "w_pool_out", "w_out", "g_ffn", "w_gate_up", "w_down")


def _shard2d(name, a):
    if name == "pool_w":
        return a.reshape(a.shape[0], a.shape[1] * a.shape[2], a.shape[3])
    return a


def _full_from_blocks(name, a):
    nb, R, C = a.shape
    if name in BLOCK_LAYOUT:
        return a
    if name in COL_SHARDED:
        return jnp.transpose(a, (1, 0, 2)).reshape(R, nb * C)
    if name == "pool_w":
        nw = len(POOL_WINDOWS)
        return jnp.transpose(a.reshape(nb, nw, R // nw, C), (1, 0, 2, 3)).reshape(nw, nb * R // nw, C)
    return a.reshape(nb * R, C)


def _blocks_from_full(name, g):
    nb = 4
    if name in BLOCK_LAYOUT:
        return g
    if name in COL_SHARDED and name not in GRAD_TRANSPOSED:
        K, N = g.shape
        return jnp.transpose(g.reshape(K, nb, N // nb), (1, 0, 2))
    if name == "pool_w":
        nw, r, C = g.shape
        return jnp.transpose(g.reshape(nw, nb, r // nb, C), (1, 0, 2, 3)).reshape(nb, nw * r // nb, C)
    return g.reshape(nb, g.shape[0] // nb, g.shape[1])


def _pack(arrs, rows):
    flat = jnp.concatenate([a.reshape(-1).astype(F32) for a in arrs])
    return jnp.concatenate([flat, jnp.zeros((rows * 128 - flat.size,), F32)]).reshape(rows, 128)


def _unpack(vec, shapes):
    flat = vec.reshape(-1)
    out, o = [], 0
    for s in shapes:
        n = int(np.prod(s))
        out.append(flat[o:o + n].reshape(s))
        o += n
    return out


def _rows_for(shapes):
    n = sum(int(np.prod(s)) for s in shapes)
    return -(-n // (8 * 128)) * 8


def kernel(x, c, ctx, c_ctx, w_ada, b_ada, g_mix, w_in, conv_w, conv_b, dt_bias, a_log, d_skip, ssd_norm_w, w_ssd_out, pool_w, pool_scale, w_pool_out, w_out, g_ffn, w_gate_up, w_down, g_final, loss_target, m_c_ctx, m_w_ada, m_b_ada, m_g_mix, m_w_in, m_conv_w, m_conv_b, m_dt_bias, m_a_log, m_d_skip, m_ssd_norm_w, m_w_ssd_out, m_pool_w, m_pool_scale, m_w_pool_out, m_w_out, m_g_ffn, m_w_gate_up, m_w_down, m_g_final, v_c_ctx, v_w_ada, v_b_ada, v_g_mix, v_w_in, v_conv_w, v_conv_b, v_dt_bias, v_a_log, v_d_skip, v_ssd_norm_w, v_w_ssd_out, v_pool_w, v_pool_scale, v_w_pool_out, v_w_out, v_g_ffn, v_w_gate_up, v_w_down, v_g_final):
    w = dict(c_ctx=c_ctx, w_ada=w_ada, b_ada=b_ada, g_mix=g_mix, w_in=w_in, conv_w=conv_w, conv_b=conv_b, dt_bias=dt_bias,
             a_log=a_log, d_skip=d_skip, ssd_norm_w=ssd_norm_w, w_ssd_out=w_ssd_out, pool_w=pool_w, pool_scale=pool_scale,
             w_pool_out=w_pool_out, w_out=w_out, g_ffn=g_ffn, w_gate_up=w_gate_up, w_down=w_down, g_final=g_final)
    m = dict(c_ctx=m_c_ctx, w_ada=m_w_ada, b_ada=m_b_ada, g_mix=m_g_mix, w_in=m_w_in, conv_w=m_conv_w, conv_b=m_conv_b,
             dt_bias=m_dt_bias, a_log=m_a_log, d_skip=m_d_skip, ssd_norm_w=m_ssd_norm_w, w_ssd_out=m_w_ssd_out,
             pool_w=m_pool_w, pool_scale=m_pool_scale, w_pool_out=m_w_pool_out, w_out=m_w_out, g_ffn=m_g_ffn,
             w_gate_up=m_w_gate_up, w_down=m_w_down, g_final=m_g_final)
    v = dict(c_ctx=v_c_ctx, w_ada=v_w_ada, b_ada=v_b_ada, g_mix=v_g_mix, w_in=v_w_in, conv_w=v_conv_w, conv_b=v_conv_b,
             dt_bias=v_dt_bias, a_log=v_a_log, d_skip=v_d_skip, ssd_norm_w=v_ssd_norm_w, w_ssd_out=v_w_ssd_out,
             pool_w=v_pool_w, pool_scale=v_pool_scale, w_pool_out=v_w_pool_out, w_out=v_w_out, g_ffn=v_g_ffn,
             w_gate_up=v_w_gate_up, w_down=v_w_down, g_final=v_g_final)
    assert x.shape[0] == 1, "one example per device"
    pos = _position()
    core = pos[2].astype(jnp.int32).reshape(1)
    n_layers = w_in.shape[0]
    assert n_layers == 2
    dims = (ssd_norm_w.shape[1], conv_w.shape[2] * 4, dt_bias[0].size, pool_scale.shape[1])
    shard = {k: _shard2d(k, w[k]) for k in BIG}

    def halves(a):
        return a.reshape(a.shape[:-2] + (2, a.shape[-2] // 2, a.shape[-1]))

    def whole(a):
        return a.reshape(a.shape[:-3] + (2 * a.shape[-2], a.shape[-1]))

    def wire_shards(l, names):
        return [halves(shard[k][l].astype(MXU_DTYPE)) for k in names]

    def full_weights(names, gathered):
        return {k: _full_from_blocks(k, whole(a)) for k, a in zip(names, gathered)}

    first = comm_call("gather0_chips", gather_chips(wire_shards(0, FIRST_USED), conv=conv_w))
    got0 = full_weights(FIRST_USED, comm_call("gather0_pair", gather_pair(first[:-1])))
    conv_all = first[-1]
    conv_full = [jnp.transpose(conv_all[:, l], (1, 0, 2)).reshape(conv_all.shape[2], -1) for l in range(n_layers)]

    boxes = {}

    n_first, n_mid = len(FIRST_USED), len(MID_USED)

    n_first, n_end = len(FIRST_USED), len(END_USED)

    def layer_w_fn(l):
        if l == 0:
            full = dict(got0)
        else:
            f0 = boxes[("fwd", 0)]
            full = full_weights(("w_in",), f0["gate_up_mm"][:1])
            full.update(full_weights(("w_ada",), f0["down_mm"]))
        late = {k: (lambda i=i: boxes[("fwd", l)]["conv"][i]) for i, k in enumerate(MID_USED)}
        late.update({k: (lambda i=i: boxes[("fwd", l)]["ssd_out_mm"][i]) for i, k in enumerate(END_USED)})
        full["conv_w"] = conv_full[l]
        lw = LazyDict(_prep_layer_weights(*[full[k] if k in full else (None if k in late else w[k][l]) for k in LAYER_KEYS]))
        for k, get in late.items():
            lw[k] = (lambda k=k, get=get: _full_from_blocks(k, whole(get())))
        return lw

    def fwd_hosts(l, box):
        boxes[("fwd", l)] = box
        hosts = {"in_mm": lambda box: gather_chips(wire_shards(l, MID_USED)),
                 "conv": lambda box: gather_pair(box["in_mm"]),
                 "ssd": lambda box: gather_chips(wire_shards(l, END_USED)),
                 "ssd_out_mm": lambda box: gather_pair(box["ssd"][:n_end])}
        if l == 0:
            hosts["ssd"] = lambda box: combine(gather_chips(wire_shards(0, END_USED)), gather_chips(wire_shards(1, ("w_in",))))
            hosts["gate_up_mm"] = lambda box: combine(gather_pair(box["ssd"][n_end:]),
                                                      gather_chips(wire_shards(1, ("w_ada",))))
            hosts["down_mm"] = lambda box: gather_pair(box["gate_up_mm"][1:])
        return hosts

    def blocks(gl, names):
        return [halves(_blocks_from_full(k, gl[k])) for k in names]

    def pair_sums(tag, names, G, recv):
        return [add_own_half(f"pair_sum{tag}_{k}", g, r, core) for k, g, r in zip(names, G, recv)]

    def chip_sums(tag, names, parts):
        return [sum_slots(f"chip_sum{tag}_{k}", p, core) for k, p in zip(names, parts)]

    def reduce_now(tag, gl, names):
        G = blocks(gl, names)
        pair = pair_sums(tag, names, G, comm_call(f"swap{tag}", swap_halves(G)))
        return chip_sums(tag, names, comm_call(f"scatter{tag}", scatter_chips(pair)))

    small_layers = {}
    n_big = len(BIG)

    def bwd_hosts(l, grads, box):
        boxes[("bwd", l)] = box
        if l != 0:
            return None
        gl1 = _unprep_layer_grads(grads[1], dims)
        small_layers[1] = gl1
        G1 = blocks(gl1, BIG)
        early = {}

        def gate_host(box):
            early["G"] = blocks(box["g"], READY_EARLY)
            return swap_halves(early["G"])

        def scan_host(box):
            return combine(scatter_chips(pair_sums("1", BIG, G1, box["down_dx"] + box["down_dw"])),
                           scatter_chips(pair_sums("0e", READY_EARLY, early["G"], box["ssd_gate"])))

        def conv_host(box):
            return combine(share_halves(chip_sums("1", BIG, box["ssd"][:n_big])),
                           share_halves(chip_sums("0e", READY_EARLY, box["ssd"][n_big:])))

        return {"down_dx": lambda box: swap_halves(G1[:n_first]), "down_dw": lambda box: swap_halves(G1[n_first:]),
                "ssd_gate": gate_host, "ssd": scan_host, "in_dx": conv_host}

    loss, grad_x, grads, d_c_ctx, d_g_final = local_step(
        x[0], ctx[0], c[0], c_ctx, loss_target[0], layer_w_fn, n_layers, g_final, fwd_hosts, bwd_hosts)
    shared =[whole(a) for a in boxes[("bwd", 0)]["in_dx"]]
    reduced1 = shared[:n_big]
    gl0 = _unprep_layer_grads(grads[0], dims)
    small_layers[0] = gl0
    last_halves = reduce_now("0", gl0, READY_LAST)

    small_full = dict(c_ctx=d_c_ctx, g_final=d_g_final.reshape(-1))
    for k in SMALL:
        if k not in small_full:
            small_full[k] = jnp.stack([small_layers[l][k] for l in range(n_layers)])
    shapes = [small_full[k].shape for k in SMALL] + [(1,)]
    packed = _pack([small_full[k] for k in SMALL] + [loss.reshape(1)], _rows_for(shapes))
    *last, everyone = comm_call("share0_small", combine(share_halves(last_halves), gather_everyone(packed)))
    red0 = dict(zip(READY_EARLY, shared[n_big:]))
    red0.update(zip(READY_LAST, [whole(a) for a in last]))
    reduced0 = [red0[k] for k in BIG]
    total = sum_slots("small_sum", everyone)
    *small_vals, loss = _unpack(total, shapes)
    loss = loss.reshape(())
    small_g = dict(zip(SMALL, small_vals))
    cw = conv_w.shape[2]
    small_g["conv_w"] = lax.dynamic_slice_in_dim(small_g["conv_w"], _chip_index(pos) * cw, cw, axis=2)

    grad, delta, new_m, new_v = {}, {}, {}, {}
    for k, g0, g1 in zip(BIG, reduced0, reduced1):
        shp = w[k].shape
        if k in GRAD_TRANSPOSED:
            flat = lambda a: jnp.swapaxes(a, 1, 2)
            back = lambda a: jnp.swapaxes(a, 1, 2)
        else:
            flat = lambda a: _shard2d(k, a)
            back = lambda a: a.reshape(shp)
        outs = adamw(f"adamw_{k}", flat(w[k]), [g0, g1], flat(m[k]), flat(v[k]))
        grad[k], delta[k], new_m[k], new_v[k] = [back(a) for a in outs]
    flat2 = lambda d: [d[k].reshape(-1, d[k].shape[-1]) for k in SMALL]
    d_, m_, v_ = adamw_small("adamw_small", flat2(w), flat2(small_g), flat2(m), flat2(v))
    for k, dd, mm, vv in zip(SMALL, d_, m_, v_):
        shp = w[k].shape
        grad[k], delta[k], new_m[k], new_v[k] = small_g[k], dd.reshape(shp), mm.reshape(shp), vv.reshape(shp)

    return (loss, grad_x[None], *[grad[k] for k in WEIGHTS], *[delta[k] for k in WEIGHTS],
            *[new_m[k] for k in WEIGHTS], *[new_v[k] for k in WEIGHTS])
```

```python
import functools

import jax
import jax.numpy as jnp
import numpy as np
from jax import lax
from jax.experimental import pallas as pl
from jax.experimental.pallas import tpu as pltpu

F32 = jnp.float32
MXU_DTYPE = jnp.bfloat16
ACT_DTYPE = jnp.bfloat16
VMEM_LIMIT_BYTES = 48 * 1024 * 1024
MATMUL_VMEM_LIMIT_BYTES = 56 * 1024 * 1024
EPS = 1e-6
NEG = -1e30

SSD_HEADDIM = 64
SSD_GROUPS = 8
SSD_STATE = 128
SSD_CHUNK = 128
SSD_GROUPS_PER_STEP = 8
SSD_CONV = 5
GRID_W = 64
POOL_WINDOWS = (2, 4, 8, 16)
ROW_TILE = 256
DT_PAD = 512

ADAM_LR = 0.001
ADAM_B1 = 0.9
ADAM_B2 = 0.999
ADAM_EPS = 1e-08
ADAM_WD = 0.01
ADAM_STEP = 10

MESH = pl.DeviceIdType.MESH


def _pcall(body, **kw):
    return pl.pallas_call(body, **kw)


def _params(*sem):
    return pltpu.CompilerParams(dimension_semantics=tuple(sem), vmem_limit_bytes=VMEM_LIMIT_BYTES)


def _pick_tile(n, cands):
    for t in cands:
        if n % t == 0:
            return t
    return n


PLANE = ((1, 0, 0), (0, 1, 0), (1, 1, 0))
PAIR = ((0, 0, 1),)
EVERYONE = tuple((a, b, d) for a in (0, 1) for b in (0, 1) for d in (0, 1) if a + b + d)
HBM = pl.BlockSpec(memory_space=pl.ANY)


def _position():
    return lax.axis_index("x"), lax.axis_index("y"), lax.axis_index("c")


def _flip(pos, rel):
    return tuple(1 - p if r else p for p, r in zip(pos, rel))


def _chip_index(pos):
    return 2 * pos[0] + pos[1]


def _device_index(pos):
    return 4 * pos[0] + 2 * pos[1] + pos[2]


class Exchange:
    def __init__(self, copies, n_remote, n_local, operands, out_shapes, aliases=None):
        self.copies, self.n_remote, self.n_local = copies, n_remote, n_local
        self.operands, self.out_shapes, self.aliases = list(operands), list(out_shapes), dict(aliases or {})

    def scratch(self):
        return [pltpu.SemaphoreType.DMA((max(self.n_remote, 1),)), pltpu.SemaphoreType.DMA((max(self.n_remote, 1),)),
                pltpu.SemaphoreType.DMA((max(self.n_local, 1),))]

    def descriptors(self, ins, outs, sems):
        send_sems, recv_sems, local_sems = sems
        local, remote = self.copies(ins, outs, _position())
        assert len(local) == self.n_local and len(remote) == self.n_remote
        cps = [pltpu.make_async_copy(src, dst, local_sems.at[k]) for k, (src, dst) in enumerate(local)]
        cps += [pltpu.make_async_remote_copy(src_ref=src, dst_ref=dst, send_sem=send_sems.at[k], recv_sem=recv_sems.at[k],
                                             device_id=peer, device_id_type=MESH) for k, (src, dst, peer) in enumerate(remote)]
        return cps


def combine(a, b):
    na, nao = len(a.operands), len(a.out_shapes)

    def copies(ins, outs, pos):
        la, ra = a.copies(ins[:na], outs[:nao], pos)
        lb, rb = b.copies(ins[na:], outs[nao:], pos)
        return la + lb, ra + rb

    aliases = dict(a.aliases)
    aliases.update({na + k: nao + v for k, v in b.aliases.items()})
    return Exchange(copies, a.n_remote + b.n_remote, a.n_local + b.n_local, a.operands + b.operands,
                    a.out_shapes + b.out_shapes, aliases)


class LazyDict(dict):
    def __getitem__(self, key):
        v = dict.__getitem__(self, key)
        if callable(v):
            v = v()
            dict.__setitem__(self, key, v)
        return v


def comm_call(name, ex):
    n_in, n_out = len(ex.operands), len(ex.out_shapes)

    def body(*refs):
        cps = ex.descriptors(refs[:n_in], refs[n_in:n_in + n_out], refs[n_in + n_out:])
        for cp in cps:
            cp.start()
        for cp in cps:
            cp.wait()

    return _pcall(
        body, name=name, out_shape=ex.out_shapes, in_specs=[HBM] * n_in, out_specs=[HBM] * n_out,
        scratch_shapes=ex.scratch(), input_output_aliases=ex.aliases,
        compiler_params=pltpu.CompilerParams(has_side_effects=True),
    )(*ex.operands)


def hosted_call(body, ex, operands, *, name, out_shape, grid, in_specs, out_specs, scratch_shapes=(),
                vmem_limit=None):
    n_in, n_out, n_scr = len(operands), len(out_shape), len(scratch_shapes)
    sem = ("arbitrary",) * len(grid)
    vmem_limit = vmem_limit or VMEM_LIMIT_BYTES
    if ex is None:
        res = _pcall(body, name=name, out_shape=list(out_shape), grid=grid, in_specs=list(in_specs),
                     out_specs=list(out_specs), scratch_shapes=list(scratch_shapes),
                     compiler_params=pltpu.CompilerParams(dimension_semantics=sem, vmem_limit_bytes=vmem_limit))(*operands)
        return res, []
    x_in, x_out = len(ex.operands), len(ex.out_shapes)

    def wrapped(*refs):
        o = 0
        ins = refs[o:o + n_in]; o += n_in
        xins = refs[o:o + x_in]; o += x_in
        outs = refs[o:o + n_out]; o += n_out
        xouts = refs[o:o + x_out]; o += x_out
        scr = refs[o:o + n_scr]; o += n_scr
        sems = refs[o:]
        first = last = None
        for a, n in enumerate(grid):
            i = pl.program_id(a)
            first = (i == 0) if first is None else first & (i == 0)
            last = (i == n - 1) if last is None else last & (i == n - 1)

        @pl.when(first)
        def _():
            for cp in ex.descriptors(xins, xouts, sems):
                cp.start()

        body(*ins, *outs, *scr)

        @pl.when(last)
        def _():
            for cp in ex.descriptors(xins, xouts, sems):
                cp.wait()

    aliases = {n_in + k: n_out + v for k, v in ex.aliases.items()}
    res = _pcall(
        wrapped, name=name, out_shape=list(out_shape) + ex.out_shapes, grid=grid,
        in_specs=list(in_specs) + [HBM] * x_in, out_specs=list(out_specs) + [HBM] * x_out,
        scratch_shapes=list(scratch_shapes) + ex.scratch(), input_output_aliases=aliases,
        compiler_params=pltpu.CompilerParams(dimension_semantics=sem, vmem_limit_bytes=vmem_limit,
                                             has_side_effects=True),
    )(*operands, *ex.operands)
    return res[:n_out], res[n_out:]


def _dot(a, b, dims):
    return lax.dot_general(a.astype(MXU_DTYPE), b.astype(MXU_DTYPE), (dims, ((), ())), preferred_element_type=F32)


_NN = ((1,), (0,))
_NT = ((1,), (1,))
_TN = ((0,), (0,))


@jax.custom_vjp
def _mm(a, b):
    return _dot(a, b, _NN)


def _mm_fwd(a, b):
    return _mm(a, b), (a, b)


def _mm_bwd(res, g):
    a, b = res
    return _dot(g, b, _NT).astype(a.dtype), _dot(a, g, _TN).astype(b.dtype)


_mm.defvjp(_mm_fwd, _mm_bwd)


@jax.custom_vjp
def _mm_nt(a, b):
    return _dot(a, b, _NT)


def _mm_nt_fwd(a, b):
    return _mm_nt(a, b), (a, b)


def _mm_nt_bwd(res, g):
    a, b = res
    return _dot(g, b, _NN).astype(a.dtype), _dot(g, a, _TN).astype(b.dtype)


_mm_nt.defvjp(_mm_nt_fwd, _mm_nt_bwd)


@jax.custom_vjp
def _mm_tn(a, b):
    return _dot(a, b, _TN)


def _mm_tn_fwd(a, b):
    return _mm_tn(a, b), (a, b)


def _mm_tn_bwd(res, g):
    a, b = res
    return _dot(b, g, _NT).astype(a.dtype), _dot(a, g, _NN).astype(b.dtype)


_mm_tn.defvjp(_mm_tn_fwd, _mm_tn_bwd)


def _dot_exact(m01, v):
    m = m01.astype(jnp.bfloat16)
    hi = v.astype(jnp.bfloat16)
    r1 = v - hi.astype(F32)
    mid = r1.astype(jnp.bfloat16)
    lo = (r1 - mid.astype(F32)).astype(jnp.bfloat16)
    out = jnp.dot(m, hi, preferred_element_type=F32)
    out = out + jnp.dot(m, mid, preferred_element_type=F32)
    return out + jnp.dot(m, lo, preferred_element_type=F32)


@jax.custom_vjp
def _lin01(m, mt, v):
    return _dot_exact(m, v)


def _lin01_fwd(m, mt, v):
    return _dot_exact(m, v), (m, mt)


def _lin01_bwd(res, g):
    m, mt = res
    return jnp.zeros_like(m), jnp.zeros_like(mt), _dot_exact(mt, g)


_lin01.defvjp(_lin01_fwd, _lin01_bwd)


MATMUL_VMEM_BUDGET = MATMUL_VMEM_LIMIT_BYTES * 5 // 6


def _mm_tiles(m, n, k_bytes_a, k_bytes_b, out_bytes, cands_m, cands_n):
    best = None
    for tm in cands_m:
        if m % tm:
            continue
        for tn in cands_n:
            if n % tn:
                continue
            need = 2 * (tm * k_bytes_a + tn * k_bytes_b + tm * tn * out_bytes)
            if need <= MATMUL_VMEM_BUDGET and (best is None or tm * tn > best[0] * best[1]):
                best = (tm, tn)
    assert best is not None, (m, n)
    return best


_ROW_CANDS = (4352, 2176, 1088, 768, 544, 512, 272, 256, 128, 16)
_COL_CANDS = (2816, 2048, 1408, 1024, 512, 256, 128)


def _one(res, xres, ex):
    return res[0] if ex is None else (res[0], xres)


def _block_cands(c):
    return (c,) + tuple(t for t in (512, 256, 128) if c % t == 0)


def matmul_nn(name, a, b, out_dtype=F32, ex=None, col0=0, ncols=None):
    M, K = a.shape
    if b.ndim == 3:
        nb, _, C = b.shape
        N, cands = nb * C, _block_cands(C)
    else:
        N, cands = (b.shape[1] - col0 if ncols is None else ncols), (1024, 512, 256, 128)
    tm, tn = _mm_tiles(M, N, K * a.dtype.itemsize, K * b.dtype.itemsize, jnp.dtype(out_dtype).itemsize,
                       _ROW_CANDS, cands)
    if b.ndim == 3:
        per = C // tn
        b_spec = pl.BlockSpec((None, K, tn), lambda j, i: (j // per, 0, j % per))
    else:
        assert col0 % tn == 0
        first = col0 // tn
        b_spec = pl.BlockSpec((K, tn), lambda j, i: (0, first + j))

    def body(a_ref, b_ref, o_ref):
        o_ref[...] = _dot(a_ref[...], b_ref[...], _NN).astype(o_ref.dtype)

    res, xres = hosted_call(
        body, ex, [a, b], name=name, out_shape=[jax.ShapeDtypeStruct((M, N), out_dtype)], grid=(N // tn, M // tm),
        in_specs=[pl.BlockSpec((tm, K), lambda j, i: (i, 0)), b_spec],
        out_specs=[pl.BlockSpec((tm, tn), lambda j, i: (i, j))], vmem_limit=MATMUL_VMEM_LIMIT_BYTES)
    return _one(res, xres, ex)


def matmul_nt(name, g, b, out_dtype=F32, ex=None, offsets=None):
    pieces = list(g) if isinstance(g, (list, tuple)) else [g]
    offsets = list(offsets) if offsets is not None else [0]
    M = pieces[0].shape[0]
    if b.ndim == 3:
        nb, K, C = b.shape
        N = nb * C
        assert len(pieces) == 1
    else:
        K, N = b.shape
    g_bytes = sum(p.shape[1] * p.dtype.itemsize for p in pieces)
    tm, tk = _mm_tiles(M, K, g_bytes, N * b.dtype.itemsize, jnp.dtype(out_dtype).itemsize, _ROW_CANDS, _COL_CANDS)

    def body(*refs):
        b_ref, o_ref = refs[-2:]
        acc = None
        if b.ndim == 3:
            parts = [_dot(refs[0][:, k * C:(k + 1) * C], b_ref[k], _NT) for k in range(nb)]
        else:
            parts = [_dot(g_ref[...], b_ref[:, off:off + g_ref.shape[1]], _NT) for g_ref, off in zip(refs[:-2], offsets)]
        for part in parts:
            acc = part if acc is None else acc + part
        o_ref[...] = acc.astype(o_ref.dtype)

    b_spec = (pl.BlockSpec((nb, tk, C), lambda j, i: (0, j, 0)) if b.ndim == 3
              else pl.BlockSpec((tk, N), lambda j, i: (j, 0)))
    res, xres = hosted_call(
        body, ex, pieces + [b], name=name, out_shape=[jax.ShapeDtypeStruct((M, K), out_dtype)], grid=(K // tk, M // tm),
        in_specs=[pl.BlockSpec((tm, p.shape[1]), lambda j, i: (i, 0)) for p in pieces] + [b_spec],
        out_specs=[pl.BlockSpec((tm, tk), lambda j, i: (i, j))], vmem_limit=MATMUL_VMEM_LIMIT_BYTES)
    return _one(res, xres, ex)


def matmul_tn(name, a, g, ex=None, blocks=1):
    M, K = a.shape
    N = g.shape[1]
    C = N // blocks
    tk, tn = _mm_tiles(K, N, M * a.dtype.itemsize, M * g.dtype.itemsize, 4, (512, 256, 128),
                       (512, 256, 128) if blocks == 1 else _block_cands(C))

    def body(a_ref, g_ref, o_ref):
        o_ref[...] = _dot(a_ref[...], g_ref[...], _TN)

    if blocks == 1:
        out_shape, out_spec = jax.ShapeDtypeStruct((K, N), F32), pl.BlockSpec((tk, tn), lambda i, j: (i, j))
    else:
        per = C // tn
        out_shape = jax.ShapeDtypeStruct((blocks, K, C), F32)
        out_spec = pl.BlockSpec((None, tk, tn), lambda i, j: (j // per, i, j % per))
    res, xres = hosted_call(
        body, ex, [a, g], name=name, out_shape=[out_shape], grid=(K // tk, N // tn),
        in_specs=[pl.BlockSpec((M, tk), lambda i, j: (0, i)), pl.BlockSpec((M, tn), lambda i, j: (0, j))],
        out_specs=[out_spec], vmem_limit=MATMUL_VMEM_LIMIT_BYTES)
    return _one(res, xres, ex)


class Arg:
    def __init__(self, arr, block, imap, kind):
        self.arr, self.block, self.imap, self.kind = arr, block, imap, kind


class Rows:
    def __init__(self, nt, nct, tm, ncol=1):
        self.nt, self.nct, self.tm, self.ncol = nt, nct, tm, ncol

    def seg(self, i):
        return jnp.where(i >= self.nct, 1, 0)

    def spec(self, block, imap):
        return pl.BlockSpec(block, lambda j, i: imap(j, i, self.seg(i)))

    def row(self, arr, width, cb0=0, follow=False, roff=0, stride=1):
        f = stride if follow else 0
        return Arg(arr, (self.tm, width), lambda j, i, s: (i + roff, cb0 + f * j), "row")

    def vec(self, arr, follow=False, kind="acc"):
        w = arr.shape[1] // (self.ncol if follow else 1)
        f = 1 if follow else 0
        return Arg(arr, (1, w), lambda j, i, s: (0, f * j), kind)

    def segvec(self, arr, kind="seg"):
        return Arg(arr, (None, 1, arr.shape[2]), lambda j, i, s: (s, 0, 0), kind)


def _load(ref):
    return ref[...].astype(F32) if ref.dtype != F32 else ref[...]


def stage_fwd(name, f, rows, args, outs):
    n_in = len(args)

    def body(*refs):
        vals = [_load(r) for r in refs[:n_in]]
        res = f(*vals)
        for r, v in zip(refs[n_in:], res):
            r[...] = v.astype(r.dtype)

    T = rows.nt * rows.tm
    out_shape = [jax.ShapeDtypeStruct((T, w * (rows.ncol if fo else 1)), dt) for w, dt, fo in outs]
    out_specs = [pl.BlockSpec((rows.tm, w), (lambda j, i, fo=fo: (i, j if fo else 0))) for w, dt, fo in outs]
    res = _pcall(
        body, name=name, out_shape=out_shape, grid=(rows.ncol, rows.nt),
        in_specs=[rows.spec(a.block, a.imap) for a in args], out_specs=out_specs,
        compiler_params=_params("parallel", "parallel"),
    )(*[a.arr for a in args])
    return res


def stage_bwd(name, f, rows, args, cots, row_dtypes, ex=None, primal=()):
    n_in, n_ct = len(args), len(cots)
    diff = [k for k, a in enumerate(args) if a.kind != "const"]
    row_dt = {}
    for k in diff:
        if args[k].kind == "row":
            row_dt[k] = row_dtypes[len(row_dt)]

    def body(*refs):
        i = pl.program_id(1)
        vals = [_load(r) for r in refs[:n_in]]
        cts = tuple(_load(r) for r in refs[n_in:n_in + n_ct])
        outs = refs[n_in + n_ct:]

        def g(*dv):
            full = list(vals)
            for k, v in zip(diff, dv):
                full[k] = v
            return tuple(f(*full))

        prim, vjp = jax.vjp(g, *[vals[k] for k in diff])
        grads = vjp(cts)
        for o, v in zip(outs[len(diff):], prim):
            o[...] = v.astype(o.dtype)
        for k, o, gr in zip(diff, outs, grads):
            kind = args[k].kind
            if kind == "row":
                o[...] = gr.astype(o.dtype)
            else:
                first = (i == 0) | (i == rows.nct) if kind == "seg" else (i == 0)

                @pl.when(first)
                def _():
                    o[...] = gr.astype(o.dtype)

                @pl.when(jnp.logical_not(first))
                def _():
                    o[...] += gr.astype(o.dtype)

    T = rows.nt * rows.tm
    out_shape, out_specs = [], []
    for k in diff:
        a = args[k]
        if a.kind == "row":
            out_shape.append(jax.ShapeDtypeStruct((T, a.block[1] * (rows.ncol if _follows(a) else 1)), row_dt[k]))
            fo = _follows(a)
            out_specs.append(pl.BlockSpec(a.block, (lambda j, i, fo=fo: (i, j if fo else 0))))
        else:
            out_shape.append(jax.ShapeDtypeStruct(a.arr.shape, F32))
            out_specs.append(rows.spec(a.block, a.imap))
    for w, dt in primal:
        out_shape.append(jax.ShapeDtypeStruct((T, w), dt))
        out_specs.append(pl.BlockSpec((rows.tm, w), lambda j, i: (i, 0)))
    res, xres = hosted_call(
        body, ex, [a.arr for a in list(args) + list(cots)], name=name, out_shape=out_shape, grid=(rows.ncol, rows.nt),
        in_specs=[rows.spec(a.block, a.imap) for a in list(args) + list(cots)], out_specs=out_specs)
    return res if ex is None else (res, xres)


def _follows(a):
    return a.imap(1, 0, 0)[-1] != a.imap(0, 0, 0)[-1]


def _rms(x):
    return x * lax.rsqrt(jnp.mean(x * x, axis=-1, keepdims=True) + EPS)


def f_norm_mod(x, g, sh, sc):
    return ((_rms(x) * g) * (1.0 + sc) + sh,)


def f_resid_norm_mod(x, mo, ga, g, sh, sc):
    x1 = x + ga * mo
    return x1, (_rms(x1) * g) * (1.0 + sc) + sh


def f_resid(x, dn, ga):
    return (x + ga * dn,)


def f_silu(x):
    return (x * jax.nn.sigmoid(x),)


def f_bias(x, b):
    return (x + b,)


def f_ssd_gate(y0, y1, xs, z, dskip, nw):
    y = y0 + y1 + dskip * xs
    return (_rms(y * (z * jax.nn.sigmoid(z))) * nw,)


def f_pool(u, pmat, pmat_t, inv_cnt, pw, scale):
    pm = _lin01(pmat, pmat_t, u) * inv_cnt - u
    return (_mm(pm, pw) * scale,)


def f_merge(o_ssd, o_pool, gl_ssd, gl_pool):
    return (jax.nn.sigmoid(gl_ssd) * o_ssd + jax.nn.sigmoid(gl_pool) * o_pool,)


def _column_splitter(n):
    @jax.custom_vjp
    def split(x):
        w = x.shape[1] // n
        return tuple(x[:, k * w:(k + 1) * w] for k in range(n))

    def fwd(x):
        return split(x), None

    def bwd(_, g):
        return (jnp.concatenate(g, axis=1),)

    split.defvjp(fwd, bwd)
    return split


_halve_cols = _column_splitter(2)
_quarter_cols = _column_splitter(len(POOL_WINDOWS))


def f_swiglu(gu):
    a, b = _halve_cols(gu)
    return ((a * jax.nn.sigmoid(a)) * b,)


def f_pool_all(u, pmat, pmat_t, inv_cnt, scale, *pws):
    outs = [f_pool(part, pmat[k], pmat_t[k], inv_cnt[k], pws[k], 1.0)[0] for k, part in enumerate(_quarter_cols(u))]
    return (jnp.concatenate(outs, axis=1) * scale,)


def f_loss_resid(x1, dn, ga, tgt, g):
    err = _rms(x1 + ga * dn) * g - tgt
    return (0.5 * jnp.mean(err * err, axis=-1, keepdims=True),)


CONV_TILE = 128


CONV_GAP = 8


def _gapped(v, n_ctx):
    z = jnp.zeros((CONV_GAP, v.shape[1]), v.dtype)
    return jnp.concatenate([v[:n_ctx], z, v[n_ctx:], z], axis=0)


def _ungapped(v, n_ctx):
    return jnp.concatenate([v[:n_ctx], v[n_ctx + CONV_GAP:v.shape[0] - CONV_GAP]], axis=0)


def _shift_rows(v, j):
    return v if j == 0 else pltpu.roll(v, (-j) % v.shape[0], 0)


def conv_fwd(name, proj, conv_w, conv_b, n_ctx, width, ex=None):
    T = proj.shape[0]
    half = SSD_CONV // 2

    def body(u_ref, w_ref, b_ref, o_ref):
        u = _gapped(u_ref[...].astype(F32), n_ctx)
        pre = jnp.broadcast_to(b_ref[...], u.shape)
        for k in range(SSD_CONV):
            pre = pre + w_ref[k:k + 1, :] * _shift_rows(u, k - half)
        o_ref[...] = _ungapped(pre * jax.nn.sigmoid(pre), n_ctx)

    col = lambda t: (0, t)
    res, xres = hosted_call(
        body, ex, [proj, conv_w, conv_b], name=name, out_shape=[jax.ShapeDtypeStruct((T, width), F32)],
        grid=(width // CONV_TILE,),
        in_specs=[pl.BlockSpec((T, CONV_TILE), col), pl.BlockSpec((SSD_CONV, CONV_TILE), col),
                  pl.BlockSpec((1, CONV_TILE), col)],
        out_specs=[pl.BlockSpec((T, CONV_TILE), col)])
    return res[0], xres


def conv_bwd(name, proj, conv_w, conv_b, d_act2, d_skip, n_ctx, width, ex=None):
    T = proj.shape[0]
    half = SSD_CONV // 2

    def body(u_ref, w_ref, b_ref, c0_ref, c1_ref, cs_ref, du_ref, dw_ref, db_ref):
        t = pl.program_id(0)
        u = _gapped(u_ref[...].astype(F32), n_ctx)
        pre = jnp.broadcast_to(b_ref[...], u.shape)
        for k in range(SSD_CONV):
            pre = pre + w_ref[k:k + 1, :] * _shift_rows(u, k - half)
        sg = jax.nn.sigmoid(pre)
        ct = c0_ref[...].astype(F32) + c1_ref[...].astype(F32) + jnp.where(t % 4 < 2, cs_ref[...].astype(F32), 0.0)
        dpre = _gapped(ct, n_ctx) * (sg * (1.0 + pre * (1.0 - sg)))
        du = jnp.zeros_like(u)
        for k in range(SSD_CONV):
            du = du + w_ref[k:k + 1, :] * _shift_rows(dpre, half - k)
            dw_ref[k:k + 1, :] = jnp.sum(dpre * _shift_rows(u, k - half), axis=0, keepdims=True)
        du_ref[...] = _ungapped(du, n_ctx).astype(du_ref.dtype)
        db_ref[...] = jnp.sum(dpre, axis=0, keepdims=True)

    col = lambda t: (0, t)
    skip_col = lambda t: (0, (t // 4) * 2 + jnp.minimum(t % 4, 1))
    res, xres = hosted_call(
        body, ex, [proj, conv_w, conv_b, d_act2[0], d_act2[1], d_skip], name=name,
        out_shape=[jax.ShapeDtypeStruct((T, width), ACT_DTYPE), jax.ShapeDtypeStruct((SSD_CONV, width), F32),
                   jax.ShapeDtypeStruct((1, width), F32)],
        grid=(width // CONV_TILE,),
        in_specs=[pl.BlockSpec((T, CONV_TILE), col), pl.BlockSpec((SSD_CONV, CONV_TILE), col),
                  pl.BlockSpec((1, CONV_TILE), col), pl.BlockSpec((T, CONV_TILE), col),
                  pl.BlockSpec((T, CONV_TILE), col), pl.BlockSpec((T, CONV_TILE), skip_col)],
        out_specs=[pl.BlockSpec((T, CONV_TILE), col), pl.BlockSpec((SSD_CONV, CONV_TILE), col),
                   pl.BlockSpec((1, CONV_TILE), col)])
    return res[0], res[1], res[2], xres


@jax.custom_vjp
def _cumsum_mat(tri, tri_t, a):
    return jnp.dot(tri, a, precision=lax.Precision.HIGHEST, preferred_element_type=F32)


def _cumsum_fwd(tri, tri_t, a):
    return _cumsum_mat(tri, tri_t, a), (tri, tri_t)


def _cumsum_bwd(res, g):
    tri, tri_t = res
    return (jnp.zeros_like(tri), jnp.zeros_like(tri_t),
            jnp.dot(tri_t, g, precision=lax.Precision.HIGHEST, preferred_element_type=F32))


_cumsum_mat.defvjp(_cumsum_fwd, _cumsum_bwd)


def _ssd_dt(dtraw, dt_bias, a_log, tri, tri_t):
    dt_all = jax.nn.softplus(dtraw + dt_bias)
    a_all = dt_all * (-jnp.exp(a_log))
    return dt_all, a_all, _cumsum_mat(tri, tri_t, a_all)


def _ssd_chunk(xs, bm, cm, dt_all, a_all, s_all, s_in, mask, idx0):
    (xs,), (s_in,) = xs, s_in
    Q = xs.shape[0]
    hpg = xs.shape[1] // SSD_HEADDIM
    lane = lax.broadcasted_iota(jnp.int32, dt_all.shape, 1)
    head = lax.broadcasted_iota(jnp.int32, xs.shape, 1) // SSD_HEADDIM
    head1 = lax.broadcasted_iota(jnp.int32, (1, xs.shape[1]), 1) // SSD_HEADDIM

    def pick(v, r):
        return jnp.sum(jnp.where(lane == idx0 + r, v, 0.0), axis=1, keepdims=True)

    def expand(cols, hd):
        out = cols[hpg - 1]
        for r in range(hpg - 2, -1, -1):
            out = jnp.where(hd == r, cols[r], out)
        return out

    def spread(*cols):
        return expand([jnp.broadcast_to(c, xs.shape) for c in cols], head)

    dt_r = [pick(dt_all, r) for r in range(hpg)]
    s_r = [pick(s_all, r) for r in range(hpg)]
    stot_r = [jnp.sum(jnp.where(lane == idx0 + r, a_all, 0.0), keepdims=True).reshape(1, 1) for r in range(hpg)]

    xd = xs * spread(*dt_r)
    cb = _mm_nt(cm, bm)
    weights, stacked = [], []
    for r in range(hpg):
        sm = jnp.broadcast_to(s_r[r], (Q, Q))
        weights.append(cb * jnp.exp(jnp.where(mask, sm - sm.T, NEG)))
        stacked.append(jnp.where(head == r, xd, 0.0))
    y = spread(*[jnp.exp(c) for c in s_r]) * _mm(cm, s_in)
    y = y + _mm(jnp.concatenate(weights, axis=1), jnp.concatenate(stacked, axis=0))
    to_end = spread(*[jnp.exp(t - c) for t, c in zip(stot_r, s_r)])
    carry = expand([jnp.broadcast_to(jnp.exp(t), (1, xs.shape[1])) for t in stot_r], head1)
    s_out = carry * s_in + _mm_tn(bm, xd * to_end)
    return [y], [s_out]


def _scan_consts():
    q = SSD_CHUNK
    i = np.arange(q)[:, None]
    j = np.arange(q)[None, :]
    fwd = (j <= i).astype(np.float32)
    bwd = (j >= i).astype(np.float32)
    tri = np.stack([fwd, bwd])
    return jnp.asarray(tri), jnp.asarray(np.stack([fwd.T, bwd.T]))


def _chunk_of(d, k, ncc, nc):
    rev = jnp.where(k < ncc, ncc - 1 - k, nc - 1 + ncc - k)
    return jnp.where(d == 0, k, rev)


def ssd_fwd(name, xbc, dtraw, dt_bias, a_log, n_ctx, ex=None):
    T = xbc.shape[0]
    q, G = SSD_CHUNK, SSD_GROUPS
    nc, ncc = T // q, n_ctx // q
    gw = xbc.shape[1] // G
    xw = gw - 2 * SSD_STATE
    hpg = xw // SSD_HEADDIM
    nh = G * hpg
    tri, tri_t = _scan_consts()

    gs = SSD_GROUPS_PER_STEP

    def body(x0_ref, x1_ref, dt0_ref, dt1_ref, bias_ref, alog_ref, tri_ref, trit_ref, y0_ref, y1_ref, sin_ref, state):
        gb, k = pl.program_id(0), pl.program_id(1)

        @pl.when(k == 0)
        def _():
            state[...] = jnp.zeros_like(state)

        for d, (x_ref, dt_ref, y_ref) in enumerate(((x0_ref, dt0_ref, y0_ref), (x1_ref, dt1_ref, y1_ref))):
            tri_v = tri_ref[d]
            dt_all, a_all, s_all = _ssd_dt(dt_ref[...], bias_ref[...], alog_ref[...], tri_v, trit_ref[d])
            for j in range(gs):
                o = j * gw
                sin_ref[d, j] = state[d, j]
                (y,), (s_out,) = _ssd_chunk(
                    [x_ref[:, o:o + xw]], x_ref[:, o + xw:o + xw + SSD_STATE], x_ref[:, o + xw + SSD_STATE:o + gw],
                    dt_all, a_all, s_all, [state[d, j]], tri_v > 0.5, d * nh + (gb * gs + j) * hpg)
                y_ref[:, j * xw:(j + 1) * xw] = y.astype(y_ref.dtype)
                state[d, j] = s_out

    ch = lambda d, k: _chunk_of(d, k, ncc, nc)
    y_shape = jax.ShapeDtypeStruct((T, G * xw), ACT_DTYPE)
    res, xres = hosted_call(
        body, ex, [xbc, xbc, dtraw, dtraw, dt_bias, a_log, tri, tri_t], name=name,
        out_shape=[y_shape, y_shape, jax.ShapeDtypeStruct((2, nc, G, SSD_STATE, xw), F32)],
        grid=(G // gs, nc),
        in_specs=[pl.BlockSpec((q, gs * gw), lambda g, k: (ch(0, k), g)),
                  pl.BlockSpec((q, gs * gw), lambda g, k: (ch(1, k), g)),
                  pl.BlockSpec((q, 128), lambda g, k: (ch(0, k), 0)),
                  pl.BlockSpec((q, 128), lambda g, k: (ch(1, k), 0)),
                  pl.BlockSpec((1, 128), lambda g, k: (0, 0)),
                  pl.BlockSpec((1, 128), lambda g, k: (0, 0)),
                  pl.BlockSpec((2, q, q), lambda g, k: (0, 0, 0)),
                  pl.BlockSpec((2, q, q), lambda g, k: (0, 0, 0))],
        out_specs=[pl.BlockSpec((q, gs * xw), lambda g, k: (ch(0, k), g)),
                   pl.BlockSpec((q, gs * xw), lambda g, k: (ch(1, k), g)),
                   pl.BlockSpec((2, None, gs, SSD_STATE, xw), lambda g, k: (0, k, g, 0, 0))],
        scratch_shapes=[pltpu.VMEM((2, gs, SSD_STATE, xw), F32)])
    return res[0], res[1], res[2], xres


def ssd_bwd(name, xbc, dtraw, dt_bias, a_log, states, dy, n_ctx, ex=None):
    T = xbc.shape[0]
    q, G = SSD_CHUNK, SSD_GROUPS
    nc, ncc = T // q, n_ctx // q
    gw = xbc.shape[1] // G
    xw = gw - 2 * SSD_STATE
    hpg = xw // SSD_HEADDIM
    nh = G * hpg
    tri, tri_t = _scan_consts()

    gs = SSD_GROUPS_PER_STEP

    def body(x0_ref, x1_ref, dt0_ref, dt1_ref, bias_ref, alog_ref, tri_ref, trit_ref, sin_ref, dy0_ref, dy1_ref,
             dx0_ref, dx1_ref, ddt_ref, dbias_ref, dalog_ref, dstate):
        gb, k = pl.program_id(0), pl.program_id(1)

        @pl.when((gb == 0) & (k == 0))
        def _():
            ddt_ref[...] = jnp.zeros_like(ddt_ref)
            dbias_ref[...] = jnp.zeros_like(dbias_ref)
            dalog_ref[...] = jnp.zeros_like(dalog_ref)

        @pl.when(k == 0)
        def _():
            dstate[...] = jnp.zeros_like(dstate)

        tris = [(tri_ref[d], trit_ref[d]) for d in range(2)]
        per = 4

        def fn(bias, alog, dtraw0, dtraw1, *per_group):
            ys, s_outs = [], []
            for d, dtraw in enumerate((dtraw0, dtraw1)):
                tri_v, trit_v = tris[d]
                dt_all, a_all, s_all = _ssd_dt(dtraw, bias, alog, tri_v, trit_v)
                for j in range(gs):
                    xs, bm, cm, s_in = per_group[per * (d * gs + j):per * (d * gs + j + 1)]
                    y, s_out = _ssd_chunk([xs], bm, cm, dt_all, a_all, s_all, [s_in], tri_v > 0.5,
                                          d * nh + (gb * gs + j) * hpg)
                    ys += y
                    s_outs += s_out
            return ys, s_outs

        per_group, dys, dss = [], [], []
        for d, (x_ref, dy_ref) in enumerate(((x0_ref, dy0_ref), (x1_ref, dy1_ref))):
            for j in range(gs):
                o = j * gw
                per_group += [x_ref[:, o:o + xw], x_ref[:, o + xw:o + xw + SSD_STATE], x_ref[:, o + xw + SSD_STATE:o + gw],
                              sin_ref[d, j]]
                dys.append(dy_ref[:, j * xw:(j + 1) * xw].astype(F32))
                dss.append(dstate[d, j])
        _, vjp = jax.vjp(fn, bias_ref[...], alog_ref[...], dt0_ref[...], dt1_ref[...], *per_group)
        cts = vjp((dys, dss))
        dbias, dalog, ddt0, ddt1 = cts[:4]
        for d, dx_ref in enumerate((dx0_ref, dx1_ref)):
            for j in range(gs):
                o = j * gw
                dxs, dbm, dcm, ds_in = cts[4 + per * (d * gs + j):4 + per * (d * gs + j + 1)]
                dx_ref[:, o:o + xw] = dxs.astype(dx_ref.dtype)
                dx_ref[:, o + xw:o + xw + SSD_STATE] = dbm.astype(dx_ref.dtype)
                dx_ref[:, o + xw + SSD_STATE:o + gw] = dcm.astype(dx_ref.dtype)
                dstate[d, j] = ds_in
        for d, ddt in enumerate((ddt0, ddt1)):
            row0 = pl.multiple_of(_chunk_of(d, nc - 1 - k, ncc, nc) * q, q)
            ddt_ref[pl.ds(row0, q), :] += ddt
        dbias_ref[...] += dbias
        dalog_ref[...] += dalog

    ch = lambda d, k: _chunk_of(d, nc - 1 - k, ncc, nc)
    dx_shape = jax.ShapeDtypeStruct((T, G * gw), ACT_DTYPE)
    res, xres = hosted_call(
        body, ex, [xbc, xbc, dtraw, dtraw, dt_bias, a_log, tri, tri_t, states, dy, dy], name=name,
        out_shape=[dx_shape, dx_shape, jax.ShapeDtypeStruct((T, 128), F32),
                   jax.ShapeDtypeStruct((1, 128), F32), jax.ShapeDtypeStruct((1, 128), F32)],
        grid=(G // gs, nc),
        in_specs=[pl.BlockSpec((q, gs * gw), lambda g, k: (ch(0, k), g)),
                  pl.BlockSpec((q, gs * gw), lambda g, k: (ch(1, k), g)),
                  pl.BlockSpec((q, 128), lambda g, k: (ch(0, k), 0)),
                  pl.BlockSpec((q, 128), lambda g, k: (ch(1, k), 0)),
                  pl.BlockSpec((1, 128), lambda g, k: (0, 0)),
                  pl.BlockSpec((1, 128), lambda g, k: (0, 0)),
                  pl.BlockSpec((2, q, q), lambda g, k: (0, 0, 0)),
                  pl.BlockSpec((2, q, q), lambda g, k: (0, 0, 0)),
                  pl.BlockSpec((2, None, gs, SSD_STATE, xw), lambda g, k: (0, nc - 1 - k, g, 0, 0)),
                  pl.BlockSpec((q, gs * xw), lambda g, k: (ch(0, k), g)),
                  pl.BlockSpec((q, gs * xw), lambda g, k: (ch(1, k), g))],
        out_specs=[pl.BlockSpec((q, gs * gw), lambda g, k: (ch(0, k), g)),
                   pl.BlockSpec((q, gs * gw), lambda g, k: (ch(1, k), g)),
                   pl.BlockSpec((T, 128), lambda g, k: (0, 0)),
                   pl.BlockSpec((1, 128), lambda g, k: (0, 0)),
                   pl.BlockSpec((1, 128), lambda g, k: (0, 0))],
        scratch_shapes=[pltpu.VMEM((2, gs, SSD_STATE, xw), F32)])
    return res[0], res[1], res[2], res[3], res[4], xres


def _perm_xbc(a):
    G = SSD_GROUPS
    n = a.shape[-1]
    gn = G * SSD_STATE
    di = n - 2 * gn
    lead = a.shape[:-1]
    xs = a[..., :di].reshape(lead + (G, di // G))
    bm = a[..., di:di + gn].reshape(lead + (G, SSD_STATE))
    cm = a[..., di + gn:].reshape(lead + (G, SSD_STATE))
    return jnp.concatenate([xs, bm, cm], axis=-1).reshape(lead + (n,))


def _unperm_xbc(a):
    G = SSD_GROUPS
    n = a.shape[-1]
    gn = G * SSD_STATE
    di = n - 2 * gn
    lead = a.shape[:-1]
    r = a.reshape(lead + (G, n // G))
    xw = di // G
    return jnp.concatenate([r[..., :xw].reshape(lead + (di,)), r[..., xw:xw + SSD_STATE].reshape(lead + (gn,)),
                            r[..., xw + SSD_STATE:].reshape(lead + (gn,))], axis=-1)


def _pool_consts(tm, n_ctx):
    assert n_ctx == tm and tm % GRID_W == 0
    mats, cnts = [], []
    for seq in (n_ctx, GRID_W):
        t = np.arange(tm)
        tt = t % seq
        base = t - tt
        ms, cs = [], []
        for k in POOL_WINDOWS:
            lo = np.clip(tt - k // 2, 0, seq) + base
            hi = np.clip(tt + k // 2, 0, seq) + base
            m = ((t[None, :] >= lo[:, None]) & (t[None, :] < hi[:, None])).astype(np.float32)
            ms.append(m)
            cs.append((1.0 / (hi - lo).astype(np.float32))[:, None])
        mats.append(np.stack(ms))
        cnts.append(np.stack(cs))
    m = np.stack(mats)
    return jnp.asarray(m), jnp.asarray(np.swapaxes(m, -1, -2)), jnp.asarray(np.stack(cnts).astype(np.float32))


def _prep_layer_weights(w_ada, b_ada, g_mix, w_in, conv_w, conv_b, dt_bias, a_log, d_skip, ssd_norm_w, w_ssd_out,
                        pool_w, pool_scale, w_pool_out, w_out, g_ffn, w_gate_up, w_down):
    D = w_in.shape[0]
    di = ssd_norm_w.shape[0]
    xbc = conv_w.shape[1]
    nh2 = dt_bias.size
    pw = pool_scale.shape[0]
    o = 0
    wz = w_in[:, o:o + di]; o += di
    wx = w_in[:, o:o + xbc]; o += xbc
    wdt = w_in[:, o:o + nh2]; o += nh2
    wp = w_in[:, o:o + pw]; o += pw
    wg = w_in[:, o:]
    w1 = jnp.concatenate([_perm_xbc(wx), wz, wg, wp, wdt, jnp.zeros((D, DT_PAD - nh2), w_in.dtype)], axis=1)
    pad128 = lambda v: jnp.concatenate([v.reshape(1, -1), jnp.zeros((1, 128 - v.size), F32)], axis=1)
    return dict(
        w_ada=w_ada, b_ada=b_ada.reshape(1, -1), g_mix=g_mix.reshape(1, -1), w1=w1,
        conv_w=_perm_xbc(conv_w), conv_b=_perm_xbc(conv_b.reshape(1, -1)),
        dt_bias=pad128(dt_bias), a_log=pad128(a_log),
        dskip=jnp.repeat(d_skip[0] + d_skip[1], SSD_HEADDIM).reshape(1, -1),
        ssd_norm_w=ssd_norm_w.reshape(1, -1), w_ssd_out=w_ssd_out, pool_w=pool_w,
        pool_scale=pool_scale.reshape(1, -1), w_pool_out=w_pool_out, w_out=w_out, g_ffn=g_ffn.reshape(1, -1),
        w_gate_up=w_gate_up, w_down=w_down)


def _unprep_layer_grads(g, dims):
    di, xbc, nh2, pw = dims
    dxbc, dz, dgs, dgp, dp, ddt = g["w1"]
    r = dxbc.reshape(SSD_GROUPS, xbc // SSD_GROUPS, dxbc.shape[1])
    xw = di // SSD_GROUPS
    parts = [r[:, :xw], r[:, xw:xw + SSD_STATE], r[:, xw + SSD_STATE:]]
    w_in_t = jnp.concatenate([dz] + [p.reshape(-1, dxbc.shape[1]) for p in parts] + [ddt[:nh2], dp, dgs, dgp], axis=0)
    nh = nh2 // 2
    dsk = g["dskip"].reshape(nh, SSD_HEADDIM).sum(axis=1)
    return dict(
        w_ada=g["w_ada"], b_ada=g["b_ada"].reshape(-1), g_mix=g["g_mix"].reshape(-1),
        w_in=w_in_t,
        conv_w=_unperm_xbc(g["conv_w"]), conv_b=_unperm_xbc(g["conv_b"]).reshape(-1),
        dt_bias=g["dt_bias"][0, :nh2].reshape(2, nh), a_log=g["a_log"][0, :nh2].reshape(2, nh),
        d_skip=jnp.stack([dsk, dsk]), ssd_norm_w=g["ssd_norm_w"].reshape(-1), w_ssd_out=g["w_ssd_out"],
        pool_w=g["pool_w"], pool_scale=g["pool_scale"].reshape(-1), w_pool_out=g["w_pool_out"], w_out=g["w_out"],
        g_ffn=g["g_ffn"].reshape(-1), w_gate_up=g["w_gate_up"], w_down=g["w_down"])


COND_ROWS = 16


def _split_mods(m):
    d = m.shape[1] // 6
    return [m[:2, k * d:(k + 1) * d].reshape(2, 1, d) for k in range(6)]


def _pool_args(rows, proj, col_block, width, pc, w):
    seg_const = lambda a: Arg(a, (None,) + a.shape[1:], lambda j, i, s: (s, 0, 0, 0), "const")
    pws = [Arg(w["pool_w"][k], w["pool_w"].shape[1:], lambda j, i, s: (0, 0), "acc") for k in range(w["pool_w"].shape[0])]
    return [rows.row(proj, width, col_block)] + [seg_const(a) for a in pc] + [rows.vec(w["pool_scale"])] + pws


TALL_ROW_TILE = 2176


def _tall_rows(T, ncol):
    tm = max(t for t in range(16, min(T, TALL_ROW_TILE) + 1, 16) if T % t == 0)
    return Rows(T // tm, 0, tm, ncol)


def _hosted(hosts, box, key):
    fn = (hosts or {}).get(key)
    return fn(box) if fn else None


def _layer_fwd(l, pre, cond_s, w, rows, n_ctx, pc, hosts=None, box=None):
    T, D = pre[0].shape if isinstance(pre, tuple) else pre.shape
    nt, nct, tm = rows.nt, rows.nct, rows.tm
    n = lambda s: f"l{l}_{s}"
    crow = Rows(1, 0, COND_ROWS)
    mraw = matmul_nn(n("ada_mm"), cond_s, w["w_ada"])
    (m,) = stage_fwd(n("ada_bias"), f_bias, crow, [crow.row(mraw, mraw.shape[1]), crow.vec(w["b_ada"])],
                     [(mraw.shape[1], F32, False)])
    sh1, sc1, ga1, sh2, sc2, ga2 = _split_mods(m)

    if isinstance(pre, tuple):
        x, h1 = stage_fwd(n("norm1"), f_resid_norm_mod, rows, _resid_norm_args(rows, pre, w["g_mix"], sh1, sc1, D),
                          [(D, F32, False), (D, ACT_DTYPE, False)])
    else:
        x = pre
        (h1,) = stage_fwd(n("norm1"), f_norm_mod, rows,
                          [rows.row(x, D), rows.vec(w["g_mix"]), rows.segvec(sh1), rows.segvec(sc1)],
                          [(D, ACT_DTYPE, False)])
    xbc_w = w["conv_w"].shape[1]
    di = w["ssd_norm_w"].shape[1]
    pw = w["pool_scale"].shape[1]
    c_z, c_g, c_p, c_dt = xbc_w, xbc_w + di, xbc_w + di + 2 * pw, xbc_w + di + 3 * pw
    ex = _hosted(hosts, box, "in_mm")
    proj = matmul_nn(n("in_mm"), h1, w["w1"], out_dtype=ACT_DTYPE, ex=ex, ncols=c_dt)
    if ex is not None:
        proj, box["in_mm"] = proj
    dtraw = matmul_nn(n("in_dt_mm"), h1, w["w1"], col0=c_dt, ncols=128)
    ex = _hosted(hosts, box, "conv")
    xbc, xres = conv_fwd(n("conv"), proj, w["conv_w"], w["conv_b"], n_ctx, xbc_w, ex)
    if ex is not None:
        box["conv"] = xres
    ex = _hosted(hosts, box, "ssd")
    y0, y1, states, xres = ssd_fwd(n("ssd"), xbc, dtraw, w["dt_bias"], w["a_log"], n_ctx, ex)
    y2 = (y0, y1)
    if ex is not None:
        box["ssd"] = xres

    G = SSD_GROUPS
    gw = di // G
    r8 = _tall_rows(T, G)
    gate_args = [r8.row(y2[0], gw, 0, True), r8.row(y2[1], gw, 0, True), r8.row(xbc, gw, 0, True, stride=2),
                 r8.row(proj, gw, c_z // gw, True), r8.vec(w["dskip"], True), r8.vec(w["ssd_norm_w"], True)]
    (ynw,) = stage_fwd(n("ssd_gate"), f_ssd_gate, r8, gate_args, [(gw, ACT_DTYPE, True)])
    ex = _hosted(hosts, box, "ssd_out_mm")
    o_ssd = matmul_nn(n("ssd_out_mm"), ynw, w["w_ssd_out"], ex=ex)
    if ex is not None:
        o_ssd, box["ssd_out_mm"] = o_ssd

    nw = len(POOL_WINDOWS)
    pg = pw // nw
    (ps,) = stage_fwd(n("pool"), f_pool_all, rows, _pool_args(rows, proj, c_p // pw, pw, pc, w), [(pw, ACT_DTYPE, False)])
    o_pool = matmul_nn(n("pool_out_mm"), ps, w["w_pool_out"])

    merge_args = [rows.row(o_ssd, D), rows.row(o_pool, D), rows.row(proj, pw, c_g // pw), rows.row(proj, pw, c_g // pw + 1)]
    (mg,) = stage_fwd(n("merge"), f_merge, rows, merge_args, [(D, ACT_DTYPE, False)])
    mo = matmul_nn(n("out_mm"), mg, w["w_out"])

    rn_args = [rows.row(x, D), rows.row(mo, D), rows.segvec(ga1), rows.vec(w["g_ffn"]), rows.segvec(sh2), rows.segvec(sc2)]
    x1, h2 = stage_fwd(n("norm2"), f_resid_norm_mod, rows, rn_args, [(D, F32, False), (D, ACT_DTYPE, False)])
    ex = _hosted(hosts, box, "gate_up_mm")
    gu = matmul_nn(n("gate_up_mm"), h2, w["w_gate_up"], ex=ex)
    if ex is not None:
        gu, box["gate_up_mm"] = gu
    fh = gu.shape[1] // 2
    (act,) = stage_fwd(n("swiglu"), f_swiglu, rows, [rows.row(gu, 2 * fh)], [(fh, ACT_DTYPE, False)])
    ex = _hosted(hosts, box, "down_mm")
    dn = matmul_nn(n("down_mm"), act, w["w_down"], ex=ex)
    if ex is not None:
        dn, box["down_mm"] = dn
    saved = dict(x=x, pre=pre, mraw=mraw, mods=(sh1, sc1, ga1, sh2, sc2, ga2), h1=h1, proj=proj, dtraw=dtraw, xbc=xbc, y2=y2,
                 states=states,
                 ynw=ynw, o_ssd=o_ssd, ps=ps, o_pool=o_pool, mg=mg, mo=mo, x1=x1, h2=h2, gu=gu, act=act, dn=dn,
                 cols=(c_z, c_g, c_p, c_dt))
    return (x1, dn, ga2), saved


def _resid_norm_args(rows, pre, g, sh, sc, D):
    x1, dn, ga2 = pre
    return [rows.row(x1, D), rows.row(dn, D), rows.segvec(ga2), rows.vec(g), rows.segvec(sh), rows.segvec(sc)]


def f_norm_mod_keep(x, g, sh, sc):
    return f_norm_mod(x, g, sh, sc)[0], x


def _layer_bwd(l, cot, cond_s, w, s, rows, n_ctx, pc, hosts=None, box=None):
    dx1, ddn, dga2 = cot
    T, D = dx1.shape
    nt, nct, tm = rows.nt, rows.nct, rows.tm
    n = lambda t: f"l{l}_{t}_bwd"
    sh1, sc1, ga1, sh2, sc2, ga2 = s["mods"]
    c_z, c_g, c_p, c_dt = s["cols"]
    x, proj, xbc, y2, gu = s["x"], s["proj"], s["xbc"], s["y2"], s["gu"]
    g = {}
    if box is not None:
        box["g"] = g

    ex = _hosted(hosts, box, "down_dx")
    dact = matmul_nt(n("down_dx"), ddn, w["w_down"], ex=ex)
    if ex is not None:
        dact, box["down_dx"] = dact
    ex = _hosted(hosts, box, "down_dw")
    g["w_down"] = matmul_tn(n("down_dw"), s["act"], ddn, ex=ex)
    if ex is not None:
        g["w_down"], box["down_dw"] = g["w_down"]
    fh = gu.shape[1] // 2
    (dgu,) = stage_bwd(n("swiglu"), f_swiglu, rows, [rows.row(gu, 2 * fh)], [rows.row(dact, fh)], [ACT_DTYPE])
    dh2 = matmul_nt(n("gate_up_dx"), dgu, w["w_gate_up"])
    g["w_gate_up"] = matmul_tn(n("gate_up_dw"), s["h2"], dgu, blocks=w["w_gate_up"].shape[0])

    rn_args = [rows.row(x, D), rows.row(s["mo"], D), rows.segvec(ga1), rows.vec(w["g_ffn"]), rows.segvec(sh2), rows.segvec(sc2)]
    dxr, dmo, dga1, g["g_ffn"], dsh2, dsc2 = stage_bwd(
        n("norm2"), f_resid_norm_mod, rows, rn_args, [rows.row(dx1, D), rows.row(dh2, D)], [F32, ACT_DTYPE])
    dmg = matmul_nt(n("out_dx"), dmo, w["w_out"])
    g["w_out"] = matmul_tn(n("out_dw"), s["mg"], dmo)

    pw = w["pool_scale"].shape[1]
    merge_args = [rows.row(s["o_ssd"], D), rows.row(s["o_pool"], D), rows.row(proj, pw, c_g // pw), rows.row(proj, pw, c_g // pw + 1)]
    do_ssd, do_pool, dgl_s, dgl_p = stage_bwd(n("merge"), f_merge, rows, merge_args, [rows.row(dmg, D)], [ACT_DTYPE] * 4)
    dps = matmul_nt(n("pool_out_dx"), do_pool, w["w_pool_out"])
    g["w_pool_out"] = matmul_tn(n("pool_out_dw"), s["ps"], do_pool)

    nw = len(POOL_WINDOWS)
    pg = pw // nw
    du_pool, g["pool_scale"], *dpw = stage_bwd(n("pool"), f_pool_all, rows, _pool_args(rows, proj, c_p // pw, pw, pc, w),
                                               [rows.row(dps, pw)], [ACT_DTYPE])
    g["pool_w"] = jnp.stack(dpw)

    dynw = matmul_nt(n("ssd_out_dx"), do_ssd, w["w_ssd_out"])
    g["w_ssd_out"] = matmul_tn(n("ssd_out_dw"), s["ynw"], do_ssd)
    G = SSD_GROUPS
    di = w["ssd_norm_w"].shape[1]
    gw = di // G
    r8 = _tall_rows(T, G)
    gate_args = [r8.row(y2[0], gw, 0, True), r8.row(y2[1], gw, 0, True), r8.row(xbc, gw, 0, True, stride=2),
                 r8.row(proj, gw, c_z // gw, True), r8.vec(w["dskip"], True), r8.vec(w["ssd_norm_w"], True)]
    gate_args[1].kind = "const"
    ex = _hosted(hosts, box, "ssd_gate")
    res = stage_bwd(n("ssd_gate"), f_ssd_gate, r8, gate_args, [r8.row(dynw, gw, 0, True)], [ACT_DTYPE] * 3, ex)
    if ex is not None:
        res, box["ssd_gate"] = res
    dy, dxs_skip, dz, g["dskip"], g["ssd_norm_w"] = res

    ex = _hosted(hosts, box, "ssd")
    dxbc0, dxbc1, ddt, g["dt_bias"], g["a_log"], xres = ssd_bwd(n("ssd"), xbc, s["dtraw"], w["dt_bias"], w["a_log"],
                                                                s["states"], dy, n_ctx, ex)
    dxbc2 = (dxbc0, dxbc1)
    if ex is not None:
        box["ssd"] = xres
    xbc_w = xbc.shape[1]
    ex = _hosted(hosts, box, "conv")
    dxbc_raw, g["conv_w"], g["conv_b"], xres = conv_bwd(n("conv"), proj, w["conv_w"], w["conv_b"], dxbc2, dxs_skip,
                                                         n_ctx, xbc_w, ex)
    if ex is not None:
        box["conv"] = xres
    pieces = [dxbc_raw, dz, dgl_s, dgl_p, du_pool, ddt]
    offsets = [0, c_z, c_g, c_g + pw, c_p, c_dt]
    ex = _hosted(hosts, box, "in_dx")
    dh1 = matmul_nt(n("in_dx"), pieces, w["w1"], ex=ex, offsets=offsets)
    if ex is not None:
        dh1, box["in_dx"] = dh1
    ex = _hosted(hosts, box, "in_dw")
    first = matmul_tn(n("in_dw0"), pieces[0], s["h1"], ex=ex)
    if ex is not None:
        first, box["in_dw"] = first
    g["w1"] = [first] + [matmul_tn(n(f"in_dw{k}"), p, s["h1"]) for k, p in enumerate(pieces) if k]

    if isinstance(s["pre"], tuple):
        dx1p, ddnp, dga2p, g["g_mix"], dsh1, dsc1 = stage_bwd(
            n("norm1"), f_resid_norm_mod, rows, _resid_norm_args(rows, s["pre"], w["g_mix"], sh1, sc1, D),
            [rows.row(dxr, D), rows.row(dh1, D)], [F32, ACT_DTYPE])
        dx = (dx1p, ddnp, dga2p)
    else:
        n1_args = [rows.row(x, D), rows.vec(w["g_mix"]), rows.segvec(sh1), rows.segvec(sc1)]
        dx, g["g_mix"], dsh1, dsc1 = stage_bwd(n("norm1"), f_norm_mod_keep, rows, n1_args,
                                               [rows.row(dh1, D), rows.row(dxr, D)], [F32])

    dm = jnp.concatenate([v.reshape(2, D) for v in (dsh1, dsc1, dga1, dsh2, dsc2, dga2)], axis=1)
    dm = jnp.concatenate([dm, jnp.zeros((COND_ROWS - 2, dm.shape[1]), F32)], axis=0)
    crow = Rows(1, 0, COND_ROWS)
    dmraw, g["b_ada"] = stage_bwd(n("ada_bias"), f_bias, crow, [crow.row(s["mraw"], dm.shape[1]), crow.vec(w["b_ada"])],
                                  [crow.row(dm, dm.shape[1])], [ACT_DTYPE])
    dcs = matmul_nt(n("ada_dx"), dmraw, w["w_ada"])
    g["w_ada"] = matmul_tn(n("ada_dw"), cond_s, dmraw, blocks=w["w_ada"].shape[0])
    return dx, dcs, g


def local_step(x, ctx, c, c_ctx, target, layer_w_fn, n_layers, g_final, fwd_hosts=None, bwd_hosts=None):
    L, D = x.shape
    n_ctx = ctx.shape[0]
    tm = ROW_TILE
    T = L + n_ctx
    rows = Rows(T // tm, n_ctx // tm, tm)
    pc = _pool_consts(tm, n_ctx)
    xa = jnp.concatenate([ctx, x], axis=0)
    cond = jnp.concatenate([c_ctx.reshape(1, D), c.reshape(1, D), jnp.zeros((COND_ROWS - 2, D), F32)], axis=0)
    crow = Rows(1, 0, COND_ROWS)
    (cond_s,) = stage_fwd("cond_silu", f_silu, crow, [crow.row(cond, D)], [(D, ACT_DTYPE, False)])

    saved, layer_w = [], []
    for l in range(n_layers):
        layer_w.append(layer_w_fn(l))
        box = {}
        xa, s = _layer_fwd(l, xa, cond_s, layer_w[l], rows, n_ctx, pc, fwd_hosts(l, box) if fwd_hosts else None, box)
        saved.append(s)

    x1, dn, ga2 = xa
    rl = Rows(L // tm, 0, tm)
    gf = g_final.reshape(1, D)
    tgt = rl.row(target, D)
    tgt.kind = "const"
    off = n_ctx // tm
    loss_args = [rl.row(x1, D, roff=off), rl.row(dn, D, roff=off), rl.vec(ga2[1]), tgt, rl.vec(gf)]
    ones = jnp.ones((L, 1), F32)
    dx1_lat, ddn_lat, dga2_lat, dgf, loss_rows = stage_bwd("loss", f_loss_resid, rl, loss_args, [rl.row(ones, 1)],
                                                           [F32, ACT_DTYPE], primal=[(1, F32)])
    loss = jnp.sum(loss_rows)
    cot = (jnp.concatenate([jnp.zeros((n_ctx, D), F32), dx1_lat], axis=0),
           jnp.concatenate([jnp.zeros((n_ctx, D), ACT_DTYPE), ddn_lat], axis=0),
           jnp.stack([jnp.zeros((1, D), F32), dga2_lat]))

    grads = [None] * n_layers
    dcs = jnp.zeros((COND_ROWS, D), F32)
    for l in reversed(range(n_layers)):
        box = {}
        hosts = bwd_hosts(l, grads, box) if bwd_hosts else None
        cot, dcs_l, grads[l] = _layer_bwd(l, cot, cond_s, layer_w[l], saved[l], rows, n_ctx, pc, hosts, box)
        dcs = dcs + dcs_l
    dx = cot
    (dcond,) = stage_bwd("cond_silu_bwd", f_silu, crow, [crow.row(cond, D)], [crow.row(dcs, D)], [F32])
    return loss, dx[n_ctx:], grads, dcond[0], dgf


def gather_chips(halves, conv=None):
    n = len(halves)
    ops = list(halves) + ([conv] if conv is not None else [])

    def copies(ins, outs, pos):
        c, me = pos[2], _chip_index(pos)
        pairs = [(s.at[c], o.at[me, c]) for s, o in zip(ins[:n], outs[:n])]
        pairs += [(s, o.at[me]) for s, o in zip(ins[n:], outs[n:])]
        return pairs, [(s, d, _flip(pos, rel)) for rel in PLANE for s, d in pairs]

    shapes = [jax.ShapeDtypeStruct((4,) + s.shape, s.dtype) for s in ops]
    return Exchange(copies, 3 * len(ops), len(ops), ops, shapes)


def gather_pair(gathered):
    n = len(gathered)

    def copies(ins, outs, pos):
        c = pos[2]
        return [], [(s.at[b, c], o.at[b, c], _flip(pos, PAIR[0])) for s, o in zip(ins, outs) for b in range(4)]

    shapes = [jax.ShapeDtypeStruct(g.shape, g.dtype) for g in gathered]
    return Exchange(copies, 4 * n, 0, gathered, shapes, aliases={k: k for k in range(n)})


def swap_halves(grads):
    n = len(grads)

    def copies(ins, outs, pos):
        c = pos[2]
        return [], [(g.at[b, 1 - c], o.at[b], _flip(pos, PAIR[0])) for g, o in zip(ins, outs) for b in range(4)]

    shapes = [jax.ShapeDtypeStruct((g.shape[0],) + g.shape[2:], g.dtype) for g in grads]
    return Exchange(copies, 4 * n, 0, grads, shapes)


def scatter_chips(sums):
    n = len(sums)

    def copies(ins, outs, pos):
        me = _chip_index(pos)
        local = [(p.at[me], o.at[me]) for p, o in zip(ins, outs)]
        remote = []
        for rel in PLANE:
            peer = _flip(pos, rel)
            remote += [(p.at[_chip_index(peer)], o.at[me], peer) for p, o in zip(ins, outs)]
        return local, remote

    shapes = [jax.ShapeDtypeStruct(p.shape, p.dtype) for p in sums]
    return Exchange(copies, 3 * n, n, sums, shapes)


def share_halves(finals):
    n = len(finals)

    def copies(ins, outs, pos):
        c = pos[2]
        return [], [(f.at[c], o.at[c], _flip(pos, PAIR[0])) for f, o in zip(ins, outs)]

    shapes = [jax.ShapeDtypeStruct(f.shape, f.dtype) for f in finals]
    return Exchange(copies, n, 0, finals, shapes, aliases={k: k for k in range(n)})


def gather_everyone(vec):
    def copies(ins, outs, pos):
        me = _device_index(pos)
        (v,), (o,) = ins, outs
        return [(v, o.at[me])], [(v, o.at[me], _flip(pos, rel)) for rel in EVERYONE]

    return Exchange(copies, len(EVERYONE), 1, [vec], [jax.ShapeDtypeStruct((8,) + vec.shape, vec.dtype)])


def _row_tile(rows, cols, n_bufs, mult=8):
    cap = VMEM_LIMIT_BYTES // 2 // (2 * n_bufs * cols * 4)
    for t in range(min(rows, cap) // mult * mult, 0, -mult):
        if rows % t == 0:
            return t
    return rows


def _adamw_update(w, g, m, v):
    nm = ADAM_B1 * m + (1.0 - ADAM_B1) * g
    nv = ADAM_B2 * v + (1.0 - ADAM_B2) * jnp.square(g)
    m_hat = nm / (1.0 - ADAM_B1 ** ADAM_STEP)
    v_hat = nv / (1.0 - ADAM_B2 ** ADAM_STEP)
    return -ADAM_LR * (m_hat / (jnp.sqrt(v_hat) + ADAM_EPS) + ADAM_WD * w), nm, nv


def adamw_small(name, ws, gs, ms, vs):
    n = len(ws)

    def body(*refs):
        ins, outs = refs[:4 * n], refs[4 * n:]
        for k in range(n):
            d, nm, nv = _adamw_update(ins[k][...], ins[n + k][...], ins[2 * n + k][...], ins[3 * n + k][...])
            outs[k][...] = d
            outs[n + k][...] = nm
            outs[2 * n + k][...] = nv

    shapes = [jax.ShapeDtypeStruct(a.shape, F32) for a in ws]
    vmem = pl.BlockSpec(memory_space=pltpu.VMEM)
    res = _pcall(body, name=name, out_shape=shapes * 3, in_specs=[vmem] * (4 * n), out_specs=[vmem] * (3 * n),
                 compiler_params=pltpu.CompilerParams(vmem_limit_bytes=VMEM_LIMIT_BYTES))(*ws, *gs, *ms, *vs)
    return res[:n], res[n:2 * n], res[2 * n:]


WIRE_DTYPE = jnp.bfloat16


def add_own_half(name, grads, recv, c):
    nb, _, R, C = grads.shape
    tr = _row_tile(R, C, 3, mult=16)

    def body(c_ref, g_ref, r_ref, o_ref):
        o_ref[...] = (g_ref[...] + r_ref[...]).astype(o_ref.dtype)

    spec = pl.BlockSpec((None, tr, C), lambda b, i, c_ref: (b, i, 0))
    return _pcall(
        body, name=name, out_shape=jax.ShapeDtypeStruct(recv.shape, WIRE_DTYPE),
        grid_spec=pltpu.PrefetchScalarGridSpec(
            num_scalar_prefetch=1, grid=(nb, R // tr),
            in_specs=[pl.BlockSpec((None, None, tr, C), lambda b, i, c_ref: (b, c_ref[0], i, 0)), spec],
            out_specs=spec),
        compiler_params=_params("parallel", "parallel"),
    )(c, grads, recv)


def sum_slots(name, a, c=None):
    n, R, C = a.shape
    tr = _row_tile(R, C, n + 1, mult=16 if a.dtype.itemsize == 2 else 8)

    def body(*refs):
        a_ref, o_ref = refs[-2:]
        acc = a_ref[0].astype(F32)
        for k in range(1, n):
            acc = acc + a_ref[k].astype(F32)
        o_ref[...] = acc

    if c is None:
        return _pcall(
            body, name=name, out_shape=jax.ShapeDtypeStruct((R, C), F32), grid=(R // tr,),
            in_specs=[pl.BlockSpec((n, tr, C), lambda i: (0, i, 0))], out_specs=pl.BlockSpec((tr, C), lambda i: (i, 0)),
            compiler_params=_params("parallel"),
        )(a)
    return _pcall(
        body, name=name, out_shape=jax.ShapeDtypeStruct((2, R, C), F32),
        grid_spec=pltpu.PrefetchScalarGridSpec(
            num_scalar_prefetch=1, grid=(R // tr,),
            in_specs=[pl.BlockSpec((n, tr, C), lambda i, c_ref: (0, i, 0))],
            out_specs=pl.BlockSpec((None, tr, C), lambda i, c_ref: (c_ref[0], i, 0))),
        compiler_params=_params("parallel"),
    )(c, a)


def adamw(name, w, g_layers, m, v):
    nl, R, C = w.shape
    assert len(g_layers) == nl
    tr = _row_tile(R, C, 8 + nl)
    nr = R // tr

    def body(*refs):
        w_ref, m_ref, v_ref = refs[:3]
        g_refs = refs[3:3 + nl]
        go_ref, d_ref, nm_ref, nv_ref = refs[3 + nl:]
        l = pl.program_id(0)
        gr = g_refs[0][...]
        for k in range(1, nl):
            gr = jnp.where(l == k, g_refs[k][...], gr)
        d_ref[...], nm_ref[...], nv_ref[...] = _adamw_update(w_ref[...], gr, m_ref[...], v_ref[...])
        go_ref[...] = gr

    spec = pl.BlockSpec((None, tr, C), lambda l, i: (l, i, 0))
    g_specs = [pl.BlockSpec((tr, C), (lambda l, i, k=k: (jnp.where(l == k, i, jnp.where(l < k, 0, nr - 1)), 0)))
               for k in range(nl)]
    return _pcall(
        body, name=name, out_shape=[jax.ShapeDtypeStruct((nl, R, C), F32)] * 4, grid=(nl, nr),
        in_specs=[spec] * 3 + g_specs, out_specs=[spec] * 4, compiler_params=_params("arbitrary", "arbitrary"),
    )(w, m, v, *g_layers)


BIG = ("w_ada", "w_in", "w_ssd_out", "pool_w", "w_pool_out", "w_out", "w_gate_up", "w_down")
COL_SHARDED = ("w_ada", "w_in", "w_gate_up")
BLOCK_LAYOUT = ("w_ada", "w_gate_up")
GRAD_TRANSPOSED = ("w_in",)
FIRST_USED = ("w_ada", "w_in")
MID_USED = ("w_ssd_out", "pool_w", "w_pool_out", "w_out")
END_USED = ("w_gate_up", "w_down")
LATER_USED = MID_USED + END_USED
assert FIRST_USED + LATER_USED == BIG
READY_LAST = FIRST_USED
READY_EARLY = LATER_USED
SMALL = ("c_ctx", "b_ada", "g_mix", "conv_w", "conv_b", "dt_bias", "a_log", "d_skip", "ssd_norm_w", "pool_scale",
         "g_ffn", "g_final")
WEIGHTS = ("c_ctx", "w_ada", "b_ada", "g_mix", "w_in", "conv_w", "conv_b", "dt_bias", "a_log", "d_skip", "ssd_norm_w",
           "w_ssd_out", "pool_w", "pool_scale", "w_pool_out", "w_out", "g_ffn", "w_gate_up", "w_down", "g_final")
LAYER_KEYS = ("w_ada", "b_ada", "g_mix", "w_in", "conv_w", "conv_b", "dt_bias", "a_log", "d_skip", "ssd_norm_w",
              "w_ssd_out", "pool_w", "pool_scale", "w_pool_out", "w_out", "g_ffn", "w_gate_up", "w_down")


def _shard2d(name, a):
    if name == "pool_w":
        return a.reshape(a.shape[0], a.shape[1] * a.shape[2], a.shape[3])
    return a


def _full_from_blocks(name, a):
    nb, R, C = a.shape
    if name in BLOCK_LAYOUT:
        return a
    if name in COL_SHARDED:
        return jnp.transpose(a, (1, 0, 2)).reshape(R, nb * C)
    if name == "pool_w":
        nw = len(POOL_WINDOWS)
        return jnp.transpose(a.reshape(nb, nw, R // nw, C), (1, 0, 2, 3)).reshape(nw, nb * R // nw, C)
    return a.reshape(nb * R, C)


def _blocks_from_full(name, g):
    nb = 4
    if name in BLOCK_LAYOUT:
        return g
    if name in COL_SHARDED and name not in GRAD_TRANSPOSED:
        K, N = g.shape
        return jnp.transpose(g.reshape(K, nb, N // nb), (1, 0, 2))
    if name == "pool_w":
        nw, r, C = g.shape
        return jnp.transpose(g.reshape(nw, nb, r // nb, C), (1, 0, 2, 3)).reshape(nb, nw * r // nb, C)
    return g.reshape(nb, g.shape[0] // nb, g.shape[1])


def _pack(arrs, rows):
    flat = jnp.concatenate([a.reshape(-1).astype(F32) for a in arrs])
    return jnp.concatenate([flat, jnp.zeros((rows * 128 - flat.size,), F32)]).reshape(rows, 128)


def _unpack(vec, shapes):
    flat = vec.reshape(-1)
    out, o = [], 0
    for s in shapes:
        n = int(np.prod(s))
        out.append(flat[o:o + n].reshape(s))
        o += n
    return out


def _rows_for(shapes):
    n = sum(int(np.prod(s)) for s in shapes)
    return -(-n // (8 * 128)) * 8


def kernel(x, c, ctx, c_ctx, w_ada, b_ada, g_mix, w_in, conv_w, conv_b, dt_bias, a_log, d_skip, ssd_norm_w, w_ssd_out, pool_w, pool_scale, w_pool_out, w_out, g_ffn, w_gate_up, w_down, g_final, loss_target, m_c_ctx, m_w_ada, m_b_ada, m_g_mix, m_w_in, m_conv_w, m_conv_b, m_dt_bias, m_a_log, m_d_skip, m_ssd_norm_w, m_w_ssd_out, m_pool_w, m_pool_scale, m_w_pool_out, m_w_out, m_g_ffn, m_w_gate_up, m_w_down, m_g_final, v_c_ctx, v_w_ada, v_b_ada, v_g_mix, v_w_in, v_conv_w, v_conv_b, v_dt_bias, v_a_log, v_d_skip, v_ssd_norm_w, v_w_ssd_out, v_pool_w, v_pool_scale, v_w_pool_out, v_w_out, v_g_ffn, v_w_gate_up, v_w_down, v_g_final):
    w = dict(c_ctx=c_ctx, w_ada=w_ada, b_ada=b_ada, g_mix=g_mix, w_in=w_in, conv_w=conv_w, conv_b=conv_b, dt_bias=dt_bias,
             a_log=a_log, d_skip=d_skip, ssd_norm_w=ssd_norm_w, w_ssd_out=w_ssd_out, pool_w=pool_w, pool_scale=pool_scale,
             w_pool_out=w_pool_out, w_out=w_out, g_ffn=g_ffn, w_gate_up=w_gate_up, w_down=w_down, g_final=g_final)
    m = dict(c_ctx=m_c_ctx, w_ada=m_w_ada, b_ada=m_b_ada, g_mix=m_g_mix, w_in=m_w_in, conv_w=m_conv_w, conv_b=m_conv_b,
             dt_bias=m_dt_bias, a_log=m_a_log, d_skip=m_d_skip, ssd_norm_w=m_ssd_norm_w, w_ssd_out=m_w_ssd_out,
             pool_w=m_pool_w, pool_scale=m_pool_scale, w_pool_out=m_w_pool_out, w_out=m_w_out, g_ffn=m_g_ffn,
             w_gate_up=m_w_gate_up, w_down=m_w_down, g_final=m_g_final)
    v = dict(c_ctx=v_c_ctx, w_ada=v_w_ada, b_ada=v_b_ada, g_mix=v_g_mix, w_in=v_w_in, conv_w=v_conv_w, conv_b=v_conv_b,
             dt_bias=v_dt_bias, a_log=v_a_log, d_skip=v_d_skip, ssd_norm_w=v_ssd_norm_w, w_ssd_out=v_w_ssd_out,
             pool_w=v_pool_w, pool_scale=v_pool_scale, w_pool_out=v_w_pool_out, w_out=v_w_out, g_ffn=v_g_ffn,
             w_gate_up=v_w_gate_up, w_down=v_w_down, g_final=v_g_final)
    assert x.shape[0] == 1, "one example per device"
    pos = _position()
    core = pos[2].astype(jnp.int32).reshape(1)
    n_layers = w_in.shape[0]
    assert n_layers == 2
    dims = (ssd_norm_w.shape[1], conv_w.shape[2] * 4, dt_bias[0].size, pool_scale.shape[1])
    shard = {k: _shard2d(k, w[k]) for k in BIG}

    def halves(a):
        return a.reshape(a.shape[:-2] + (2, a.shape[-2] // 2, a.shape[-1]))

    def whole(a):
        return a.reshape(a.shape[:-3] + (2 * a.shape[-2], a.shape[-1]))

    def wire_shards(l, names):
        return [halves(shard[k][l].astype(MXU_DTYPE)) for k in names]

    def full_weights(names, gathered):
        return {k: _full_from_blocks(k, whole(a)) for k, a in zip(names, gathered)}

    first = comm_call("gather0_chips", gather_chips(wire_shards(0, FIRST_USED), conv=conv_w))
    got0 = full_weights(FIRST_USED, comm_call("gather0_pair", gather_pair(first[:-1])))
    conv_all = first[-1]
    conv_full = [jnp.transpose(conv_all[:, l], (1, 0, 2)).reshape(conv_all.shape[2], -1) for l in range(n_layers)]

    boxes = {}

    n_first, n_mid = len(FIRST_USED), len(MID_USED)

    n_first, n_end = len(FIRST_USED), len(END_USED)

    def layer_w_fn(l):
        if l == 0:
            full = dict(got0)
        else:
            f0 = boxes[("fwd", 0)]
            full = full_weights(("w_in",), f0["gate_up_mm"][:1])
            full.update(full_weights(("w_ada",), f0["down_mm"]))
        late = {k: (lambda i=i: boxes[("fwd", l)]["conv"][i]) for i, k in enumerate(MID_USED)}
        late.update({k: (lambda i=i: boxes[("fwd", l)]["ssd_out_mm"][i]) for i, k in enumerate(END_USED)})
        full["conv_w"] = conv_full[l]
        lw = LazyDict(_prep_layer_weights(*[full[k] if k in full else (None if k in late else w[k][l]) for k in LAYER_KEYS]))
        for k, get in late.items():
            lw[k] = (lambda k=k, get=get: _full_from_blocks(k, whole(get())))
        return lw

    def fwd_hosts(l, box):
        boxes[("fwd", l)] = box
        hosts = {"in_mm": lambda box: gather_chips(wire_shards(l, MID_USED)),
                 "conv": lambda box: gather_pair(box["in_mm"]),
                 "ssd": lambda box: gather_chips(wire_shards(l, END_USED)),
                 "ssd_out_mm": lambda box: gather_pair(box["ssd"][:n_end])}
        if l == 0:
            hosts["ssd"] = lambda box: combine(gather_chips(wire_shards(0, END_USED)), gather_chips(wire_shards(1, ("w_in",))))
            hosts["gate_up_mm"] = lambda box: combine(gather_pair(box["ssd"][n_end:]),
                                                      gather_chips(wire_shards(1, ("w_ada",))))
            hosts["down_mm"] = lambda box: gather_pair(box["gate_up_mm"][1:])
        return hosts

    def blocks(gl, names):
        return [halves(_blocks_from_full(k, gl[k])) for k in names]

    def pair_sums(tag, names, G, recv):
        return [add_own_half(f"pair_sum{tag}_{k}", g, r, core) for k, g, r in zip(names, G, recv)]

    def chip_sums(tag, names, parts):
        return [sum_slots(f"chip_sum{tag}_{k}", p, core) for k, p in zip(names, parts)]

    def reduce_now(tag, gl, names):
        G = blocks(gl, names)
        pair = pair_sums(tag, names, G, comm_call(f"swap{tag}", swap_halves(G)))
        return chip_sums(tag, names, comm_call(f"scatter{tag}", scatter_chips(pair)))

    small_layers = {}
    n_big = len(BIG)

    def bwd_hosts(l, grads, box):
        boxes[("bwd", l)] = box
        if l != 0:
            return None
        gl1 = _unprep_layer_grads(grads[1], dims)
        small_layers[1] = gl1
        G1 = blocks(gl1, BIG)
        early = {}

        def gate_host(box):
            early["G"] = blocks(box["g"], READY_EARLY)
            return swap_halves(early["G"])

        def scan_host(box):
            return combine(scatter_chips(pair_sums("1", BIG, G1, box["down_dx"] + box["down_dw"])),
                           scatter_chips(pair_sums("0e", READY_EARLY, early["G"], box["ssd_gate"])))

        def conv_host(box):
            return combine(share_halves(chip_sums("1", BIG, box["ssd"][:n_big])),
                           share_halves(chip_sums("0e", READY_EARLY, box["ssd"][n_big:])))

        return {"down_dx": lambda box: swap_halves(G1[:n_first]), "down_dw": lambda box: swap_halves(G1[n_first:]),
                "ssd_gate": gate_host, "ssd": scan_host, "in_dx": conv_host}

    loss, grad_x, grads, d_c_ctx, d_g_final = local_step(
        x[0], ctx[0], c[0], c_ctx, loss_target[0], layer_w_fn, n_layers, g_final, fwd_hosts, bwd_hosts)
    shared =[whole(a) for a in boxes[("bwd", 0)]["in_dx"]]
    reduced1 = shared[:n_big]
    gl0 = _unprep_layer_grads(grads[0], dims)
    small_layers[0] = gl0
    last_halves = reduce_now("0", gl0, READY_LAST)

    small_full = dict(c_ctx=d_c_ctx, g_final=d_g_final.reshape(-1))
    for k in SMALL:
        if k not in small_full:
            small_full[k] = jnp.stack([small_layers[l][k] for l in range(n_layers)])
    shapes = [small_full[k].shape for k in SMALL] + [(1,)]
    packed = _pack([small_full[k] for k in SMALL] + [loss.reshape(1)], _rows_for(shapes))
    *last, everyone = comm_call("share0_small", combine(share_halves(last_halves), gather_everyone(packed)))
    red0 = dict(zip(READY_EARLY, shared[n_big:]))
    red0.update(zip(READY_LAST, [whole(a) for a in last]))
    reduced0 = [red0[k] for k in BIG]
    total = sum_slots("small_sum", everyone)
    *small_vals, loss = _unpack(total, shapes)
    loss = loss.reshape(())
    small_g = dict(zip(SMALL, small_vals))
    cw = conv_w.shape[2]
    small_g["conv_w"] = lax.dynamic_slice_in_dim(small_g["conv_w"], _chip_index(pos) * cw, cw, axis=2)

    grad, delta, new_m, new_v = {}, {}, {}, {}
    for k, g0, g1 in zip(BIG, reduced0, reduced1):
        shp = w[k].shape
        if k in GRAD_TRANSPOSED:
            flat = lambda a: jnp.swapaxes(a, 1, 2)
            back = lambda a: jnp.swapaxes(a, 1, 2)
        else:
            flat = lambda a: _shard2d(k, a)
            back = lambda a: a.reshape(shp)
        outs = adamw(f"adamw_{k}", flat(w[k]), [g0, g1], flat(m[k]), flat(v[k]))
        grad[k], delta[k], new_m[k], new_v[k] = [back(a) for a in outs]
    flat2 = lambda d: [d[k].reshape(-1, d[k].shape[-1]) for k in SMALL]
    d_, m_, v_ = adamw_small("adamw_small", flat2(w), flat2(small_g), flat2(m), flat2(v))
    for k, dd, mm, vv in zip(SMALL, d_, m_, v_):
        shp = w[k].shape
        grad[k], delta[k], new_m[k], new_v[k] = small_g[k], dd.reshape(shp), mm.reshape(shp), vv.reshape(shp)

    return (loss, grad_x[None], *[grad[k] for k in WEIGHTS], *[delta[k] for k in WEIGHTS],
            *[new_m[k] for k in WEIGHTS], *[new_v[k] for k in WEIGHTS])
```
